```python
import math
import jax
import jax.numpy as jnp
from jax import lax
import numpy as np

D_MODEL = 1024
BATCH = 8
SEQ = 8192
DEPTH = 2

HEAD_DIM = 64
BRANCH_WIDTH = D_MODEL // 2
N_BRANCH = 3
CONV_WIDTH = BRANCH_WIDTH
CONV_K = 3
FOX_HEADS = BRANCH_WIDTH // HEAD_DIM
SWA_HEADS = BRANCH_WIDTH // HEAD_DIM
SWA_KV_HEADS = 2
SWA_GROUP = SWA_HEADS // SWA_KV_HEADS
WINDOW = 128
BLOCK = 128
N_BUCKETS = 32
MAX_DISTANCE = WINDOW
MEM_LEN = 256
X_HEADS = 4
X_HEAD_DIM = D_MODEL // X_HEADS
_FF_RAW = -(-8 * D_MODEL // 3)
D_FF = -(-_FF_RAW // 256) * 256
IN_COLS = (3 * CONV_WIDTH + 3 * FOX_HEADS * HEAD_DIM + FOX_HEADS
           + (SWA_HEADS + 2 * SWA_KV_HEADS) * HEAD_DIM + N_BRANCH * D_MODEL)
RMS_EPS = 1e-6
NEG_INF = -1e30

kernel_name = "hybrid_conv_fox_swa_block"


def rms_norm(x, g):
    xf = x.astype(jnp.float32)
    y = xf * lax.rsqrt(jnp.mean(xf * xf, axis=-1, keepdims=True) + RMS_EPS)
    return (y * g.astype(jnp.float32)).astype(x.dtype)


def split_proj(proj):
    sizes = ([CONV_WIDTH] * 3 + [FOX_HEADS * HEAD_DIM] * 3 + [FOX_HEADS]
             + [SWA_HEADS * HEAD_DIM, SWA_KV_HEADS * HEAD_DIM, SWA_KV_HEADS * HEAD_DIM]
             + [N_BRANCH * D_MODEL])
    parts, off = [], 0
    for s in sizes:
        parts.append(proj[..., off:off + s])
        off += s
    return parts


def short_conv_branch(gate_b, gate_c, u, conv_w):
    z = gate_c * u
    y = lax.conv_general_dilated(
        z, conv_w[:, None, :].astype(z.dtype), window_strides=(1,),
        padding=[(CONV_K - 1, 0)], dimension_numbers=('NWC', 'WIO', 'NWC'),
        feature_group_count=CONV_WIDTH)
    return gate_b * y


def fox_branch(q, k, v, f_logit, f_bias):
    b, s = q.shape[0], q.shape[1]
    nb = s // BLOCK
    q = q.reshape(b, s, FOX_HEADS, HEAD_DIM)
    k = k.reshape(b, s, FOX_HEADS, HEAD_DIM)
    v = v.reshape(b, s, FOX_HEADS, HEAD_DIM)
    log_f = jax.nn.log_sigmoid(f_logit.astype(jnp.float32) + f_bias.astype(jnp.float32))
    c = jnp.cumsum(log_f, axis=1)
    c_k = c.transpose(0, 2, 1)
    q_blocks = q.reshape(b, nb, BLOCK, FOX_HEADS, HEAD_DIM).transpose(1, 0, 2, 3, 4)
    c_blocks = c.reshape(b, nb, BLOCK, FOX_HEADS).transpose(1, 0, 2, 3)
    starts = jnp.arange(nb, dtype=jnp.int32) * BLOCK
    k_pos = jnp.arange(s, dtype=jnp.int32)
    scale = HEAD_DIM ** -0.5

    def one_block(args):
        qi, ci, start = args
        logits = jnp.einsum('bqhd,bkhd->bhqk', qi, k).astype(jnp.float32) * scale
        logits = logits + ci.transpose(0, 2, 1)[..., None] - c_k[:, :, None, :]
        q_pos = start + jnp.arange(BLOCK, dtype=jnp.int32)
        causal = k_pos[None, :] <= q_pos[:, None]
        logits = jnp.where(causal[None, None], logits, NEG_INF)
        p = jax.nn.softmax(logits, axis=-1)
        return jnp.einsum('bhqk,bkhd->bqhd', p.astype(v.dtype), v)

    out = lax.map(one_block, (q_blocks, c_blocks, starts))
    return out.transpose(1, 0, 2, 3, 4).reshape(b, s, FOX_HEADS * HEAD_DIM)


def t5_bucket(n):
    n = jnp.maximum(n, 0)
    max_exact = N_BUCKETS // 2
    large = max_exact + (
        jnp.log(jnp.maximum(n, 1).astype(jnp.float32) / max_exact)
        / math.log(MAX_DISTANCE / max_exact) * (N_BUCKETS - max_exact)).astype(jnp.int32)
    large = jnp.minimum(large, N_BUCKETS - 1)
    return jnp.where(n < max_exact, n, large)


def swa_sink_branch(q, k, v, rel_bias, sink):
    b, s = q.shape[0], q.shape[1]
    nb = s // BLOCK
    qb = q.reshape(b, nb, BLOCK, SWA_KV_HEADS, SWA_GROUP, HEAD_DIM)
    kb = k.reshape(b, nb, BLOCK, SWA_KV_HEADS, HEAD_DIM)
    vb = v.reshape(b, nb, BLOCK, SWA_KV_HEADS, HEAD_DIM)

    def band(t):
        prev = jnp.concatenate([jnp.zeros_like(t[:, :1]), t[:, :-1]], axis=1)
        return jnp.concatenate([prev, t], axis=2)

    k_band, v_band = band(kb), band(vb)
    tq = jnp.arange(BLOCK, dtype=jnp.int32)
    sk = jnp.arange(2 * BLOCK, dtype=jnp.int32)
    dist = BLOCK + tq[:, None] - sk[None, :]
    in_window = (dist >= 0) & (dist < WINDOW)
    key_pos = jnp.arange(nb, dtype=jnp.int32)[:, None] * BLOCK - BLOCK + sk[None, :]
    mask = in_window[None] & (key_pos >= 0)[:, None, :]
    bias = rel_bias.astype(jnp.float32)[t5_bucket(dist)]
    bias = bias.reshape(BLOCK, 2 * BLOCK, SWA_KV_HEADS, SWA_GROUP).transpose(2, 3, 0, 1)
    logits = jnp.einsum('bnqkgd,bnskd->bnkgqs', qb, k_band).astype(jnp.float32) * HEAD_DIM ** -0.5
    logits = jnp.where(mask[None, :, None, None], logits + bias, NEG_INF)
    sink_l = sink.astype(jnp.float32).reshape(SWA_KV_HEADS, SWA_GROUP)[None, None, :, :, None]
    m = jnp.maximum(logits.max(axis=-1), sink_l)
    p = jnp.exp(logits - m[..., None])
    denom = p.sum(axis=-1) + jnp.exp(sink_l - m)
    p = p / denom[..., None]
    out = jnp.einsum('bnkgqs,bnskd->bnqkgd', p.astype(v.dtype), v_band)
    return out.reshape(b, s, SWA_HEADS * HEAD_DIM)


def cross_attention(xn, mem_n, w_q, w_kv, w_o):
    b, s = xn.shape[0], xn.shape[1]
    q = (xn @ w_q).reshape(b, s, X_HEADS, X_HEAD_DIM)
    kv = mem_n @ w_kv
    k = kv[..., :X_HEADS * X_HEAD_DIM].reshape(b, -1, X_HEADS, X_HEAD_DIM)
    v = kv[..., X_HEADS * X_HEAD_DIM:].reshape(b, -1, X_HEADS, X_HEAD_DIM)
    logits = jnp.einsum('bshd,bmhd->bhsm', q, k).astype(jnp.float32) * X_HEAD_DIM ** -0.5
    p = jax.nn.softmax(logits, axis=-1)
    o = jnp.einsum('bhsm,bmhd->bshd', p.astype(v.dtype), v).reshape(b, s, X_HEADS * X_HEAD_DIM)
    return o @ w_o


def swiglu(xn, w_gate, w_up, w_down):
    return (jax.nn.silu(xn @ w_gate) * (xn @ w_up)) @ w_down


def _fwd_setup_inputs(seed: int = 0) -> dict:
    key = jax.random.key(seed)
    ks = jax.random.split(key, 22)
    f32 = jnp.float32

    def nrm(k, shape, scale):
        return scale * jax.random.normal(k, shape, f32)

    def gain(k, shape):
        return 1.0 + 0.1 * jax.random.normal(k, shape, f32)

    return {
        "x": nrm(ks[0], (BATCH, SEQ, D_MODEL), 1.0),
        "mem": nrm(ks[1], (BATCH, MEM_LEN, D_MODEL), 1.0),
        "mix_norm_g": gain(ks[2], (DEPTH, D_MODEL)),
        "w_in": nrm(ks[3], (DEPTH, D_MODEL, IN_COLS), D_MODEL ** -0.5),
        "forget_bias": 4.0 + 0.5 * jax.random.normal(ks[4], (DEPTH, FOX_HEADS), f32),
        "conv_w": nrm(ks[5], (DEPTH, CONV_K, CONV_WIDTH), CONV_K ** -0.5),
        "sink": nrm(ks[6], (DEPTH, SWA_HEADS), 0.5),
        "w_branch": nrm(ks[7], (DEPTH, N_BRANCH, BRANCH_WIDTH, D_MODEL), BRANCH_WIDTH ** -0.5),
        "w_mix_out": nrm(ks[8], (DEPTH, D_MODEL, D_MODEL), D_MODEL ** -0.5),
        "rel_bias": nrm(ks[9], (N_BUCKETS, SWA_HEADS), 0.5),
        "xattn_norm_g": gain(ks[10], (DEPTH, D_MODEL)),
        "mem_norm_g": gain(ks[11], (DEPTH, D_MODEL)),
        "w_xq": nrm(ks[12], (DEPTH, D_MODEL, X_HEADS * X_HEAD_DIM), D_MODEL ** -0.5),
        "w_xkv": nrm(ks[13], (DEPTH, D_MODEL, 2 * X_HEADS * X_HEAD_DIM), D_MODEL ** -0.5),
        "w_xo": nrm(ks[14], (DEPTH, X_HEADS * X_HEAD_DIM, D_MODEL), (X_HEADS * X_HEAD_DIM) ** -0.5),
        "ffn_norm_g": gain(ks[15], (DEPTH, D_MODEL)),
        "w_ffn_gate": nrm(ks[16], (DEPTH, D_MODEL, D_FF), D_MODEL ** -0.5),
        "w_ffn_up": nrm(ks[17], (DEPTH, D_MODEL, D_FF), D_MODEL ** -0.5),
        "w_ffn_down": nrm(ks[18], (DEPTH, D_FF, D_MODEL), D_FF ** -0.5),
        "final_norm_g": gain(ks[19], (D_MODEL,)),
    }


def _fwd_reference(x, mem, mix_norm_g, w_in, forget_bias, conv_w, sink, w_branch, w_mix_out,
              rel_bias, xattn_norm_g, mem_norm_g, w_xq, w_xkv, w_xo, ffn_norm_g,
              w_ffn_gate, w_ffn_up, w_ffn_down, final_norm_g):
    b, s = x.shape[0], x.shape[1]
    for l in range(DEPTH):
        h = rms_norm(x, mix_norm_g[l])
        (c_b, c_c, c_u, f_q, f_k, f_v, f_g, s_q, s_k, s_v, gate_logits) = split_proj(h @ w_in[l])
        y_conv = short_conv_branch(c_b, c_c, c_u, conv_w[l])
        y_fox = fox_branch(f_q, f_k, f_v, f_g, forget_bias[l])
        y_swa = swa_sink_branch(s_q, s_k, s_v, rel_bias, sink[l])
        gates = jax.nn.sigmoid(gate_logits.reshape(b, s, N_BRANCH, D_MODEL))
        merged = (gates[:, :, 0] * (y_conv @ w_branch[l, 0])
                  + gates[:, :, 1] * (y_fox @ w_branch[l, 1])
                  + gates[:, :, 2] * (y_swa @ w_branch[l, 2]))
        x = x + merged @ w_mix_out[l]
        x = x + cross_attention(rms_norm(x, xattn_norm_g[l]), rms_norm(mem, mem_norm_g[l]),
                                w_xq[l], w_xkv[l], w_xo[l])
        x = x + swiglu(rms_norm(x, ffn_norm_g[l]), w_ffn_gate[l], w_ffn_up[l], w_ffn_down[l])
    return rms_norm(x, final_norm_g)


import jax as _jax
import jax.numpy as _jnp

TWIN_FORMAT = 'train_step'
FWD_PARAMS = ['x', 'mem', 'mix_norm_g', 'w_in', 'forget_bias', 'conv_w', 'sink', 'w_branch', 'w_mix_out', 'rel_bias', 'xattn_norm_g', 'mem_norm_g', 'w_xq', 'w_xkv', 'w_xo', 'ffn_norm_g', 'w_ffn_gate', 'w_ffn_up', 'w_ffn_down', 'final_norm_g']
TWIN_WEIGHTS = ['mix_norm_g', 'w_in', 'forget_bias', 'conv_w', 'sink', 'w_branch', 'w_mix_out', 'rel_bias', 'xattn_norm_g', 'mem_norm_g', 'w_xq', 'w_xkv', 'w_xo', 'ffn_norm_g', 'w_ffn_gate', 'w_ffn_up', 'w_ffn_down', 'final_norm_g']
TWIN_DIFF_INPUT = 'x'
TWIN_INPUTS = ['x', 'mem', 'mix_norm_g', 'w_in', 'forget_bias', 'conv_w', 'sink', 'w_branch', 'w_mix_out', 'rel_bias', 'xattn_norm_g', 'mem_norm_g', 'w_xq', 'w_xkv', 'w_xo', 'ffn_norm_g', 'w_ffn_gate', 'w_ffn_up', 'w_ffn_down', 'final_norm_g', 'loss_target', 'm_mix_norm_g', 'm_w_in', 'm_forget_bias', 'm_conv_w', 'm_sink', 'm_w_branch', 'm_w_mix_out', 'm_rel_bias', 'm_xattn_norm_g', 'm_mem_norm_g', 'm_w_xq', 'm_w_xkv', 'm_w_xo', 'm_ffn_norm_g', 'm_w_ffn_gate', 'm_w_ffn_up', 'm_w_ffn_down', 'm_final_norm_g', 'v_mix_norm_g', 'v_w_in', 'v_forget_bias', 'v_conv_w', 'v_sink', 'v_w_branch', 'v_w_mix_out', 'v_rel_bias', 'v_xattn_norm_g', 'v_mem_norm_g', 'v_w_xq', 'v_w_xkv', 'v_w_xo', 'v_ffn_norm_g', 'v_w_ffn_gate', 'v_w_ffn_up', 'v_w_ffn_down', 'v_final_norm_g']
TWIN_OUTPUTS = ['loss', 'grad_x', 'grad_mix_norm_g', 'grad_w_in', 'grad_forget_bias', 'grad_conv_w', 'grad_sink', 'grad_w_branch', 'grad_w_mix_out', 'grad_rel_bias', 'grad_xattn_norm_g', 'grad_mem_norm_g', 'grad_w_xq', 'grad_w_xkv', 'grad_w_xo', 'grad_ffn_norm_g', 'grad_w_ffn_gate', 'grad_w_ffn_up', 'grad_w_ffn_down', 'grad_final_norm_g', 'delta_mix_norm_g', 'delta_w_in', 'delta_forget_bias', 'delta_conv_w', 'delta_sink', 'delta_w_branch', 'delta_w_mix_out', 'delta_rel_bias', 'delta_xattn_norm_g', 'delta_mem_norm_g', 'delta_w_xq', 'delta_w_xkv', 'delta_w_xo', 'delta_ffn_norm_g', 'delta_w_ffn_gate', 'delta_w_ffn_up', 'delta_w_ffn_down', 'delta_final_norm_g', 'new_m_mix_norm_g', 'new_m_w_in', 'new_m_forget_bias', 'new_m_conv_w', 'new_m_sink', 'new_m_w_branch', 'new_m_w_mix_out', 'new_m_rel_bias', 'new_m_xattn_norm_g', 'new_m_mem_norm_g', 'new_m_w_xq', 'new_m_w_xkv', 'new_m_w_xo', 'new_m_ffn_norm_g', 'new_m_w_ffn_gate', 'new_m_w_ffn_up', 'new_m_w_ffn_down', 'new_m_final_norm_g', 'new_v_mix_norm_g', 'new_v_w_in', 'new_v_forget_bias', 'new_v_conv_w', 'new_v_sink', 'new_v_w_branch', 'new_v_w_mix_out', 'new_v_rel_bias', 'new_v_xattn_norm_g', 'new_v_mem_norm_g', 'new_v_w_xq', 'new_v_w_xkv', 'new_v_w_xo', 'new_v_ffn_norm_g', 'new_v_w_ffn_gate', 'new_v_w_ffn_up', 'new_v_w_ffn_down', 'new_v_final_norm_g']
TWIN_LEAF_KINDS = {'loss': 'loss', 'grad_x': 'grad_x', 'grad_mix_norm_g': 'grad_w', 'grad_w_in': 'grad_w', 'grad_forget_bias': 'grad_w', 'grad_conv_w': 'grad_w', 'grad_sink': 'grad_w', 'grad_w_branch': 'grad_w', 'grad_w_mix_out': 'grad_w', 'grad_rel_bias': 'grad_w', 'grad_xattn_norm_g': 'grad_w', 'grad_mem_norm_g': 'grad_w', 'grad_w_xq': 'grad_w', 'grad_w_xkv': 'grad_w', 'grad_w_xo': 'grad_w', 'grad_ffn_norm_g': 'grad_w', 'grad_w_ffn_gate': 'grad_w', 'grad_w_ffn_up': 'grad_w', 'grad_w_ffn_down': 'grad_w', 'grad_final_norm_g': 'grad_w', 'delta_mix_norm_g': 'delta_w', 'delta_w_in': 'delta_w', 'delta_forget_bias': 'delta_w', 'delta_conv_w': 'delta_w', 'delta_sink': 'delta_w', 'delta_w_branch': 'delta_w', 'delta_w_mix_out': 'delta_w', 'delta_rel_bias': 'delta_w', 'delta_xattn_norm_g': 'delta_w', 'delta_mem_norm_g': 'delta_w', 'delta_w_xq': 'delta_w', 'delta_w_xkv': 'delta_w', 'delta_w_xo': 'delta_w', 'delta_ffn_norm_g': 'delta_w', 'delta_w_ffn_gate': 'delta_w', 'delta_w_ffn_up': 'delta_w', 'delta_w_ffn_down': 'delta_w', 'delta_final_norm_g': 'delta_w', 'new_m_mix_norm_g': 'new_m', 'new_m_w_in': 'new_m', 'new_m_forget_bias': 'new_m', 'new_m_conv_w': 'new_m', 'new_m_sink': 'new_m', 'new_m_w_branch': 'new_m', 'new_m_w_mix_out': 'new_m', 'new_m_rel_bias': 'new_m', 'new_m_xattn_norm_g': 'new_m', 'new_m_mem_norm_g': 'new_m', 'new_m_w_xq': 'new_m', 'new_m_w_xkv': 'new_m', 'new_m_w_xo': 'new_m', 'new_m_ffn_norm_g': 'new_m', 'new_m_w_ffn_gate': 'new_m', 'new_m_w_ffn_up': 'new_m', 'new_m_w_ffn_down': 'new_m', 'new_m_final_norm_g': 'new_m', 'new_v_mix_norm_g': 'new_v', 'new_v_w_in': 'new_v', 'new_v_forget_bias': 'new_v', 'new_v_conv_w': 'new_v', 'new_v_sink': 'new_v', 'new_v_w_branch': 'new_v', 'new_v_w_mix_out': 'new_v', 'new_v_rel_bias': 'new_v', 'new_v_xattn_norm_g': 'new_v', 'new_v_mem_norm_g': 'new_v', 'new_v_w_xq': 'new_v', 'new_v_w_xkv': 'new_v', 'new_v_w_xo': 'new_v', 'new_v_ffn_norm_g': 'new_v', 'new_v_w_ffn_gate': 'new_v', 'new_v_w_ffn_up': 'new_v', 'new_v_w_ffn_down': 'new_v', 'new_v_final_norm_g': 'new_v'}


def _forward(args):
    return _fwd_reference(*[args[k] for k in FWD_PARAMS])


def _output_shape():
    def fwd():
        inp = _fwd_setup_inputs(0)
        return _fwd_reference(*[inp[k] for k in FWD_PARAMS])
    out = _jax.eval_shape(fwd)
    return out.shape, out.dtype

N_MICROBATCH = 1
ADAM_LR = 0.001
ADAM_B1 = 0.9
ADAM_B2 = 0.999
ADAM_EPS = 1e-08
ADAM_WD = 0.01
ADAM_STEP = 10
PER_EXAMPLE_BATCH_AXIS = {'x': 0, 'mem': 0, 'loss_target': 0}
SHARED_INPUTS = []
_WEIGHT_DTYPES = {'mix_norm_g': _jnp.float32, 'w_in': _jnp.float32, 'forget_bias': _jnp.float32, 'conv_w': _jnp.float32, 'sink': _jnp.float32, 'w_branch': _jnp.float32, 'w_mix_out': _jnp.float32, 'rel_bias': _jnp.float32, 'xattn_norm_g': _jnp.float32, 'mem_norm_g': _jnp.float32, 'w_xq': _jnp.float32, 'w_xkv': _jnp.float32, 'w_xo': _jnp.float32, 'ffn_norm_g': _jnp.float32, 'w_ffn_gate': _jnp.float32, 'w_ffn_up': _jnp.float32, 'w_ffn_down': _jnp.float32, 'final_norm_g': _jnp.float32}
MOMENT_SCALE = {'mix_norm_g': 2.684210e-01, 'w_in': 9.952049e-02, 'forget_bias': 1.522034e-01, 'conv_w': 2.113000e-01, 'sink': 3.180257e-02, 'w_branch': 8.613275e-02, 'w_mix_out': 1.518410e-01, 'rel_bias': 7.315095e-02, 'xattn_norm_g': 2.750027e-02, 'mem_norm_g': 4.362313e-02, 'w_xq': 2.786435e-02, 'w_xkv': 2.864814e-02, 'w_xo': 2.937972e-02, 'ffn_norm_g': 1.851630e-01, 'w_ffn_gate': 7.807728e-02, 'w_ffn_up': 7.744290e-02, 'w_ffn_down': 1.295093e-01, 'final_norm_g': 6.418222e+01}


def _to_microbatches(a, axis):
    t = _jnp.moveaxis(a, axis, 0)
    t = t.reshape((N_MICROBATCH, t.shape[0] // N_MICROBATCH) + t.shape[1:])
    return _jnp.moveaxis(t, 1, axis + 1)


def setup_inputs(seed: int = 0) -> dict:
    inp = _fwd_setup_inputs(seed)
    key = _jax.random.fold_in(_jax.random.key(seed), 7919)
    shape, _ = _output_shape()
    out = dict(inp)
    out["loss_target"] = _jax.random.normal(_jax.random.fold_in(key, 0), shape, _jnp.float32)
    for i, name in enumerate(TWIN_WEIGHTS):
        w = inp[name].astype(_jnp.float32)
        if MOMENT_SCALE is None:
            s = _jnp.sqrt(_jnp.mean(_jnp.square(w)) + 1e-30)
        else:
            s = MOMENT_SCALE[name]
        km, kv = _jax.random.split(_jax.random.fold_in(key, i + 1))
        out[name] = w
        out["m_" + name] = s * _jax.random.normal(km, w.shape, _jnp.float32)
        out["v_" + name] = (s * s) * _jax.random.uniform(kv, w.shape, _jnp.float32, 0.5, 1.5)
    if N_MICROBATCH > 1:
        for name, axis in PER_EXAMPLE_BATCH_AXIS.items():
            out[name] = _to_microbatches(out[name], axis)
    return {'x': out['x'], 'mem': out['mem'], 'mix_norm_g': out['mix_norm_g'], 'w_in': out['w_in'], 'forget_bias': out['forget_bias'], 'conv_w': out['conv_w'], 'sink': out['sink'], 'w_branch': out['w_branch'], 'w_mix_out': out['w_mix_out'], 'rel_bias': out['rel_bias'], 'xattn_norm_g': out['xattn_norm_g'], 'mem_norm_g': out['mem_norm_g'], 'w_xq': out['w_xq'], 'w_xkv': out['w_xkv'], 'w_xo': out['w_xo'], 'ffn_norm_g': out['ffn_norm_g'], 'w_ffn_gate': out['w_ffn_gate'], 'w_ffn_up': out['w_ffn_up'], 'w_ffn_down': out['w_ffn_down'], 'final_norm_g': out['final_norm_g'], 'loss_target': out['loss_target'], 'm_mix_norm_g': out['m_mix_norm_g'], 'm_w_in': out['m_w_in'], 'm_forget_bias': out['m_forget_bias'], 'm_conv_w': out['m_conv_w'], 'm_sink': out['m_sink'], 'm_w_branch': out['m_w_branch'], 'm_w_mix_out': out['m_w_mix_out'], 'm_rel_bias': out['m_rel_bias'], 'm_xattn_norm_g': out['m_xattn_norm_g'], 'm_mem_norm_g': out['m_mem_norm_g'], 'm_w_xq': out['m_w_xq'], 'm_w_xkv': out['m_w_xkv'], 'm_w_xo': out['m_w_xo'], 'm_ffn_norm_g': out['m_ffn_norm_g'], 'm_w_ffn_gate': out['m_w_ffn_gate'], 'm_w_ffn_up': out['m_w_ffn_up'], 'm_w_ffn_down': out['m_w_ffn_down'], 'm_final_norm_g': out['m_final_norm_g'], 'v_mix_norm_g': out['v_mix_norm_g'], 'v_w_in': out['v_w_in'], 'v_forget_bias': out['v_forget_bias'], 'v_conv_w': out['v_conv_w'], 'v_sink': out['v_sink'], 'v_w_branch': out['v_w_branch'], 'v_w_mix_out': out['v_w_mix_out'], 'v_rel_bias': out['v_rel_bias'], 'v_xattn_norm_g': out['v_xattn_norm_g'], 'v_mem_norm_g': out['v_mem_norm_g'], 'v_w_xq': out['v_w_xq'], 'v_w_xkv': out['v_w_xkv'], 'v_w_xo': out['v_w_xo'], 'v_ffn_norm_g': out['v_ffn_norm_g'], 'v_w_ffn_gate': out['v_w_ffn_gate'], 'v_w_ffn_up': out['v_w_ffn_up'], 'v_w_ffn_down': out['v_w_ffn_down'], 'v_final_norm_g': out['v_final_norm_g']}


def _loss(weights, diff, rest, loss_target):
    with _jax.named_scope("forward"):
        args = {**rest, TWIN_DIFF_INPUT: diff, **{k: w.astype(_WEIGHT_DTYPES[k]) for k, w in weights.items()}}
        y = _forward(args)
    with _jax.named_scope("loss_head"):
        err = _jnp.square(y.astype(_jnp.float32) - loss_target)
        return 0.5 * _jnp.sum(_jnp.mean(err, axis=-1)) if err.ndim else 0.5 * err


def _adamw(w, g, m, v):
    m = ADAM_B1 * m + (1.0 - ADAM_B1) * g
    v = ADAM_B2 * v + (1.0 - ADAM_B2) * _jnp.square(g)
    m_hat = m / (1.0 - ADAM_B1 ** ADAM_STEP)
    v_hat = v / (1.0 - ADAM_B2 ** ADAM_STEP)
    delta = -ADAM_LR * (m_hat / (_jnp.sqrt(v_hat) + ADAM_EPS) + ADAM_WD * w)
    return delta, m, v


def reference(x, mem, mix_norm_g, w_in, forget_bias, conv_w, sink, w_branch, w_mix_out, rel_bias, xattn_norm_g, mem_norm_g, w_xq, w_xkv, w_xo, ffn_norm_g, w_ffn_gate, w_ffn_up, w_ffn_down, final_norm_g, loss_target, m_mix_norm_g, m_w_in, m_forget_bias, m_conv_w, m_sink, m_w_branch, m_w_mix_out, m_rel_bias, m_xattn_norm_g, m_mem_norm_g, m_w_xq, m_w_xkv, m_w_xo, m_ffn_norm_g, m_w_ffn_gate, m_w_ffn_up, m_w_ffn_down, m_final_norm_g, v_mix_norm_g, v_w_in, v_forget_bias, v_conv_w, v_sink, v_w_branch, v_w_mix_out, v_rel_bias, v_xattn_norm_g, v_mem_norm_g, v_w_xq, v_w_xkv, v_w_xo, v_ffn_norm_g, v_w_ffn_gate, v_w_ffn_up, v_w_ffn_down, v_final_norm_g):
    given = dict(x=x, mem=mem, mix_norm_g=mix_norm_g, w_in=w_in, forget_bias=forget_bias, conv_w=conv_w, sink=sink, w_branch=w_branch, w_mix_out=w_mix_out, rel_bias=rel_bias, xattn_norm_g=xattn_norm_g, mem_norm_g=mem_norm_g, w_xq=w_xq, w_xkv=w_xkv, w_xo=w_xo, ffn_norm_g=ffn_norm_g, w_ffn_gate=w_ffn_gate, w_ffn_up=w_ffn_up, w_ffn_down=w_ffn_down, final_norm_g=final_norm_g, loss_target=loss_target, m_mix_norm_g=m_mix_norm_g, m_w_in=m_w_in, m_forget_bias=m_forget_bias, m_conv_w=m_conv_w, m_sink=m_sink, m_w_branch=m_w_branch, m_w_mix_out=m_w_mix_out, m_rel_bias=m_rel_bias, m_xattn_norm_g=m_xattn_norm_g, m_mem_norm_g=m_mem_norm_g, m_w_xq=m_w_xq, m_w_xkv=m_w_xkv, m_w_xo=m_w_xo, m_ffn_norm_g=m_ffn_norm_g, m_w_ffn_gate=m_w_ffn_gate, m_w_ffn_up=m_w_ffn_up, m_w_ffn_down=m_w_ffn_down, m_final_norm_g=m_final_norm_g, v_mix_norm_g=v_mix_norm_g, v_w_in=v_w_in, v_forget_bias=v_forget_bias, v_conv_w=v_conv_w, v_sink=v_sink, v_w_branch=v_w_branch, v_w_mix_out=v_w_mix_out, v_rel_bias=v_rel_bias, v_xattn_norm_g=v_xattn_norm_g, v_mem_norm_g=v_mem_norm_g, v_w_xq=v_w_xq, v_w_xkv=v_w_xkv, v_w_xo=v_w_xo, v_ffn_norm_g=v_ffn_norm_g, v_w_ffn_gate=v_w_ffn_gate, v_w_ffn_up=v_w_ffn_up, v_w_ffn_down=v_w_ffn_down, v_final_norm_g=v_final_norm_g)
    weights = {n: given[n] for n in TWIN_WEIGHTS}
    shared = {n: given[n] for n in SHARED_INPUTS}
    per_example = {n: given[n] for n in ['x', 'mem']}
    grad_fn = _jax.value_and_grad(_loss, argnums=(0, 1))

    def one_microbatch(ex, loss_target):
        ex = dict(ex)
        diff = ex.pop(TWIN_DIFF_INPUT)
        return grad_fn(weights, diff, {**shared, **ex}, loss_target)

    if N_MICROBATCH == 1:
        loss, (grad_w, grad_x) = one_microbatch(per_example, given["loss_target"])
    else:
        def body(carry, xs):
            loss_sum, grad_sum = carry
            l_k, (gw_k, gx_k) = one_microbatch(xs[0], xs[1])
            with _jax.named_scope("update"):
                return (loss_sum + l_k, _jax.tree.map(_jnp.add, grad_sum, gw_k)), gx_k

        init = (_jnp.zeros((), _jnp.float32), _jax.tree.map(_jnp.zeros_like, weights))
        (loss, grad_w), grad_x = _jax.lax.scan(body, init, (per_example, given["loss_target"]))
    with _jax.named_scope("update"):
        delta_w, new_m, new_v = {}, {}, {}
        for n in TWIN_WEIGHTS:
            delta_w[n], new_m[n], new_v[n] = _adamw(weights[n], grad_w[n], given["m_" + n], given["v_" + n])
    return (loss, grad_x, *[grad_w[n] for n in TWIN_WEIGHTS], *[delta_w[n] for n in TWIN_WEIGHTS],
            *[new_m[n] for n in TWIN_WEIGHTS], *[new_v[n] for n in TWIN_WEIGHTS])
```

```python
import functools
import math

import numpy as np
import jax
import jax.numpy as jnp
from jax import lax
from jax.experimental import pallas as pl
from jax.experimental.pallas import tpu as pltpu

F32 = jnp.float32
BF16 = jnp.bfloat16
MESH = pl.DeviceIdType.MESH

LANE = 128
BF16_SUBLANE = 16
V7X_VMEM_REQUEST_CAP = 56 * 2 ** 20
N_DEV = 8

D_MODEL = 1024
DEPTH = 2
HEAD = 64
BRANCH = 512
SWA_BLOCK = 128
SWA_GROUP = 4
N_BUCKETS = 32
X_HEADS = 4
X_HEAD = 256
D_FF = 2816
RMS_EPS = 1e-6
NEG = -1e30
ADAM_LR, ADAM_B1, ADAM_B2, ADAM_EPS, ADAM_WD, ADAM_STEP = 0.001, 0.9, 0.999, 1e-08, 0.01, 10

IN_COLS = 6920
PROJ_COLS = 7040
COL_GATE, COL_CONV, COL_FOX, COL_SQ, COL_SK, COL_SV, COL_FG = 0, 3072, 4608, 6144, 6656, 6784, 6912

ROW_TILE = 512
FOX_TILE = 512
MM_TM, MM_TN, MM_TK = 1024, 1536, 1024
PACK_ROWS = 2048


def _pick(n, cap, mult):
    best = None
    for d in range(mult, min(n, cap) + 1, mult):
        if n % d == 0:
            best = d
    return n if best is None else best


def _params(semantics, block_bytes):
    limit = int(min(max(2 * block_bytes + (8 << 20), 24 << 20), V7X_VMEM_REQUEST_CAP))
    return pltpu.CompilerParams(dimension_semantics=semantics, vmem_limit_bytes=limit)


def _nbytes(shape, dtype):
    return int(np.prod(shape)) * jnp.dtype(dtype).itemsize


def _dot(a, b, dims):
    return lax.dot_general(a, b, (dims, ((), ())), preferred_element_type=F32)


NN = ((1,), (0,))
NT = ((1,), (1,))
TN = ((0,), (0,))


def _matmul(a, b, mode, out_dtype, name, residual=None):
    if mode == "nn":
        (m, k), (k2, n) = a.shape, b.shape
    elif mode == "nt":
        (m, k), (n, k2) = a.shape, b.shape
    else:
        (k, m), (k2, n) = a.shape, b.shape
    assert k == k2, (name, a.shape, b.shape)
    tm, tn, tk = _pick(m, MM_TM, LANE), _pick(n, MM_TN, LANE), _pick(k, MM_TK, LANE)
    nk = k // tk
    dims = {"nn": NN, "nt": NT, "tn": TN}[mode]
    has_res = residual is not None

    def body(*refs):
        a_ref, b_ref = refs[0], refs[1]
        r_ref = refs[2] if has_res else None
        o_ref = refs[3] if has_res else refs[2]
        kk = pl.program_id(2)
        p = _dot(a_ref[...].astype(BF16), b_ref[...].astype(BF16), dims)
        if nk == 1:
            if has_res:
                p = p + r_ref[...]
            o_ref[...] = p.astype(out_dtype)
        else:
            acc_ref = refs[-1]

            @pl.when(kk == 0)
            def _():
                acc_ref[...] = p

            @pl.when(kk > 0)
            def _():
                acc_ref[...] += p

            @pl.when(kk == nk - 1)
            def _():
                res = acc_ref[...]
                if has_res:
                    res = res + r_ref[...]
                o_ref[...] = res.astype(out_dtype)

    if mode == "nn":
        a_spec = pl.BlockSpec((tm, tk), lambda i, j, kk: (i, kk))
        b_spec = pl.BlockSpec((tk, tn), lambda i, j, kk: (kk, j))
    elif mode == "nt":
        a_spec = pl.BlockSpec((tm, tk), lambda i, j, kk: (i, kk))
        b_spec = pl.BlockSpec((tn, tk), lambda i, j, kk: (j, kk))
    else:
        a_spec = pl.BlockSpec((tk, tm), lambda i, j, kk: (kk, i))
        b_spec = pl.BlockSpec((tk, tn), lambda i, j, kk: (kk, j))
    o_spec = pl.BlockSpec((tm, tn), lambda i, j, kk: (i, j))
    in_specs, args = [a_spec, b_spec], [a, b]
    if has_res:
        in_specs.append(o_spec)
        args.append(residual)
    blk = (_nbytes((tm, tk), a.dtype) + _nbytes((tk, tn), b.dtype) + _nbytes((tm, tn), out_dtype)
           + (_nbytes((tm, tn), F32) if has_res else 0))
    scratch = [pltpu.VMEM((tm, tn), F32)] if nk > 1 else []
    return pl.pallas_call(
        body, name=name, grid=(m // tm, n // tn, nk),
        out_shape=jax.ShapeDtypeStruct((m, n), out_dtype),
        in_specs=in_specs, out_specs=o_spec, scratch_shapes=scratch,
        compiler_params=_params(("parallel", "parallel", "arbitrary"), blk + _nbytes((tm, tn), F32)),
    )(*args)


def _rms_fwd(x, g, name):
    t, d = x.shape
    tr = _pick(t, ROW_TILE, BF16_SUBLANE)

    def body(x_ref, g_ref, y_ref):
        xv = x_ref[...]
        r = lax.rsqrt(jnp.mean(xv * xv, axis=-1, keepdims=True) + RMS_EPS)
        y_ref[...] = ((xv * r) * g_ref[...]).astype(BF16)

    return pl.pallas_call(
        body, name=name, grid=(t // tr,),
        out_shape=jax.ShapeDtypeStruct((t, d), BF16),
        in_specs=[pl.BlockSpec((tr, d), lambda i: (i, 0)), pl.BlockSpec((1, d), lambda i: (0, 0))],
        out_specs=pl.BlockSpec((tr, d), lambda i: (i, 0)),
        compiler_params=_params(("parallel",), 2 * _nbytes((tr, d), F32)),
    )(x, g.reshape(1, d))


def _rms_bwd(x, g, dy, dres, name):
    t, d = x.shape
    tr = _pick(t, ROW_TILE, BF16_SUBLANE)
    has_res = dres is not None

    def body(*refs):
        x_ref, g_ref, dy_ref = refs[:3]
        r_ref = refs[3] if has_res else None
        dx_ref, dxb_ref, dg_ref = refs[-3:]
        i = pl.program_id(0)
        xv = x_ref[...]
        r = lax.rsqrt(jnp.mean(xv * xv, axis=-1, keepdims=True) + RMS_EPS)
        xh = xv * r
        dyv = dy_ref[...].astype(F32)
        dxh = dyv * g_ref[...]
        dx = r * (dxh - xh * jnp.mean(dxh * xh, axis=-1, keepdims=True))
        if has_res:
            dx = dx + r_ref[...]
        dx_ref[...] = dx
        dxb_ref[...] = dx.astype(BF16)

        @pl.when(i == 0)
        def _():
            dg_ref[...] = jnp.zeros_like(dg_ref)

        dg_ref[...] += jnp.sum(dyv * xh, axis=0, keepdims=True)

    row = pl.BlockSpec((tr, d), lambda i: (i, 0))
    vec = pl.BlockSpec((1, d), lambda i: (0, 0))
    in_specs, args = [row, vec, row], [x, g.reshape(1, d), dy]
    if has_res:
        in_specs.append(row)
        args.append(dres)
    return pl.pallas_call(
        body, name=name, grid=(t // tr,),
        out_shape=(jax.ShapeDtypeStruct((t, d), F32), jax.ShapeDtypeStruct((t, d), BF16),
                   jax.ShapeDtypeStruct((1, d), F32)),
        in_specs=in_specs, out_specs=(row, row, vec),
        compiler_params=_params(("arbitrary",), 5 * _nbytes((tr, d), F32)),
    )(*args)


def _loss_head(x, g, target, name):
    t, d = x.shape
    tr = _pick(t, ROW_TILE, BF16_SUBLANE)

    def body(x_ref, g_ref, t_ref, loss_ref, dx_ref, dxb_ref, dg_ref):
        i = pl.program_id(0)
        xv = x_ref[...]
        gv = g_ref[...]
        r = lax.rsqrt(jnp.mean(xv * xv, axis=-1, keepdims=True) + RMS_EPS)
        xh = xv * r
        diff = xh * gv - t_ref[...]
        part = 0.5 * jnp.sum(jnp.mean(diff * diff, axis=-1, keepdims=True), axis=0, keepdims=True)
        dyv = diff * (1.0 / d)
        dxh = dyv * gv
        dx = r * (dxh - xh * jnp.mean(dxh * xh, axis=-1, keepdims=True))
        dx_ref[...] = dx
        dxb_ref[...] = dx.astype(BF16)

        @pl.when(i == 0)
        def _():
            dg_ref[...] = jnp.zeros_like(dg_ref)
            loss_ref[...] = jnp.zeros_like(loss_ref)

        dg_ref[...] += jnp.sum(dyv * xh, axis=0, keepdims=True)
        loss_ref[...] += jnp.broadcast_to(part, loss_ref.shape)

    row = pl.BlockSpec((tr, d), lambda i: (i, 0))
    vec = pl.BlockSpec((1, d), lambda i: (0, 0))
    return pl.pallas_call(
        body, name=name, grid=(t // tr,),
        out_shape=(jax.ShapeDtypeStruct((1, LANE), F32), jax.ShapeDtypeStruct((t, d), F32),
                   jax.ShapeDtypeStruct((t, d), BF16), jax.ShapeDtypeStruct((1, d), F32)),
        in_specs=[row, vec, row],
        out_specs=(pl.BlockSpec((1, LANE), lambda i: (0, 0)), row, row, vec),
        compiler_params=_params(("arbitrary",), 5 * _nbytes((tr, d), F32)),
    )(x, g.reshape(1, d), target)


HALO = 8


def _conv_fwd(proj, conv_w, name):
    t = proj.shape[0]
    tr = _pick(t, ROW_TILE, BF16_SUBLANE)
    c0 = COL_CONV // BRANCH
    hb = tr // HALO

    def body(cb_ref, cc_ref, cu_ref, hc_ref, hu_ref, w_ref, y_ref):
        i = pl.program_id(0)
        z = cc_ref[...] * cu_ref[...]
        hz = jnp.where(i > 0, hc_ref[...] * hu_ref[...], 0.0)
        zf = jnp.concatenate([hz, z], axis=0)
        z1 = pltpu.roll(zf, 1, 0)[HALO:]
        z2 = pltpu.roll(zf, 2, 0)[HALO:]
        y = w_ref[2:3, :] * z + w_ref[1:2, :] * z1 + w_ref[0:1, :] * z2
        y_ref[...] = (cb_ref[...] * y).astype(BF16)

    def col(c):
        return pl.BlockSpec((tr, BRANCH), lambda i, c=c: (i, c0 + c))

    def prev(c):
        return pl.BlockSpec((HALO, BRANCH), lambda i, c=c: (jnp.maximum(i * hb - 1, 0), c0 + c))

    return pl.pallas_call(
        body, name=name, grid=(t // tr,),
        out_shape=jax.ShapeDtypeStruct((t, BRANCH), BF16),
        in_specs=[col(0), col(1), col(2), prev(1), prev(2), pl.BlockSpec((3, BRANCH), lambda i: (0, 0))],
        out_specs=pl.BlockSpec((tr, BRANCH), lambda i: (i, 0)),
        compiler_params=_params(("parallel",), 6 * _nbytes((tr, BRANCH), F32)),
    )(proj, proj, proj, proj, proj, conv_w)


def _conv_bwd(proj, conv_w, dout, name):
    t = proj.shape[0]
    tr = _pick(t, ROW_TILE, BF16_SUBLANE)
    nblk = t // tr
    c0 = COL_CONV // BRANCH
    hb = tr // HALO
    last_halo = t // HALO - 1

    def body(cb_ref, cc_ref, cu_ref, hc_ref, hu_ref, do_ref, ndo_ref, ncb_ref, w_ref,
             dcb_ref, dcc_ref, dcu_ref, dw_ref):
        i = pl.program_id(0)
        cb, cc, cu = cb_ref[...], cc_ref[...], cu_ref[...]
        w0, w1, w2 = w_ref[0:1, :], w_ref[1:2, :], w_ref[2:3, :]
        z = cc * cu
        hz = jnp.where(i > 0, hc_ref[...] * hu_ref[...], 0.0)
        zf = jnp.concatenate([hz, z], axis=0)
        z1 = pltpu.roll(zf, 1, 0)[HALO:]
        z2 = pltpu.roll(zf, 2, 0)[HALO:]
        y = w2 * z + w1 * z1 + w0 * z2
        dout_v = do_ref[...]
        dyc = dout_v * cb
        hdy = jnp.where(i < nblk - 1, ndo_ref[...] * ncb_ref[...], 0.0)
        dyf = jnp.concatenate([dyc, hdy], axis=0)
        dy1 = pltpu.roll(dyf, tr + HALO - 1, 0)[:tr]
        dy2 = pltpu.roll(dyf, tr + HALO - 2, 0)[:tr]
        dz = w2 * dyc + w1 * dy1 + w0 * dy2
        dcb_ref[...] = (dout_v * y).astype(BF16)
        dcc_ref[...] = (dz * cu).astype(BF16)
        dcu_ref[...] = (dz * cc).astype(BF16)

        @pl.when(i == 0)
        def _():
            dw_ref[...] = jnp.zeros_like(dw_ref)

        dw_ref[0:1, :] += jnp.sum(dyc * z2, axis=0, keepdims=True)
        dw_ref[1:2, :] += jnp.sum(dyc * z1, axis=0, keepdims=True)
        dw_ref[2:3, :] += jnp.sum(dyc * z, axis=0, keepdims=True)

    def col(c):
        return pl.BlockSpec((tr, BRANCH), lambda i, c=c: (i, c0 + c))

    def prev(c):
        return pl.BlockSpec((HALO, BRANCH), lambda i, c=c: (jnp.maximum(i * hb - 1, 0), c0 + c))

    def nxt(c):
        return pl.BlockSpec((HALO, BRANCH), lambda i, c=c: (jnp.minimum((i + 1) * hb, last_halo), c))

    row = pl.BlockSpec((tr, BRANCH), lambda i: (i, 0))
    return pl.pallas_call(
        body, name=name, grid=(nblk,),
        out_shape=(jax.ShapeDtypeStruct((t, BRANCH), BF16),) * 3 + (jax.ShapeDtypeStruct((HALO, BRANCH), F32),),
        in_specs=[col(0), col(1), col(2), prev(1), prev(2), row, nxt(0), nxt(c0),
                  pl.BlockSpec((3, BRANCH), lambda i: (0, 0))],
        out_specs=(row, row, row, pl.BlockSpec((HALO, BRANCH), lambda i: (0, 0))),
        compiler_params=_params(("arbitrary",), 8 * _nbytes((tr, BRANCH), F32)),
    )(proj, proj, proj, proj, proj, dout, dout, proj, conv_w)


def _tri(lower):
    r = lax.broadcasted_iota(jnp.int32, (LANE, LANE), 0)
    c = lax.broadcasted_iota(jnp.int32, (LANE, LANE), 1)
    return jnp.where((c <= r) if lower else (c >= r), 1.0, 0.0).astype(F32)


def _logf_cumsum(proj, fbias_row, name):
    t = proj.shape[0]
    nchunk = t // LANE

    def body(f_ref, b_ref, c_ref, run_sc):
        tri = _tri(True)
        run_sc[...] = jnp.zeros_like(run_sc)

        @pl.loop(0, nchunk)
        def _(i):
            rows = pl.ds(pl.multiple_of(i * LANE, LANE), LANE)
            z = f_ref[rows, :] + b_ref[...]
            logf = jnp.minimum(z, 0.0) - jnp.log(1.0 + jnp.exp(-jnp.abs(z)))
            cs = lax.dot_general(tri, logf, (NN, ((), ())), precision=lax.Precision.HIGHEST,
                                 preferred_element_type=F32) + run_sc[0:1, :]
            c_ref[rows, :] = cs
            run_sc[0:1, :] = cs[LANE - 1:LANE, :]

    return pl.pallas_call(
        body, name=name, grid=(1,),
        out_shape=jax.ShapeDtypeStruct((t, LANE), F32),
        in_specs=[pl.BlockSpec((t, LANE), lambda i: (0, COL_FG // LANE)), pl.BlockSpec((1, LANE), lambda i: (0, 0))],
        out_specs=pl.BlockSpec((t, LANE), lambda i: (0, 0)),
        scratch_shapes=[pltpu.VMEM((8, LANE), F32)],
        compiler_params=_params(("arbitrary",), 2 * _nbytes((t, LANE), F32)),
    )(proj, fbias_row)


def _logf_cumsum_bwd(proj, fbias_row, dc, name):
    t = proj.shape[0]
    nchunk = t // LANE

    def body(f_ref, b_ref, dc_ref, df_ref, db_ref, run_sc):
        tri = _tri(False)
        run_sc[...] = jnp.zeros_like(run_sc)
        db_ref[...] = jnp.zeros_like(db_ref)

        @pl.loop(0, nchunk)
        def _(k):
            rows = pl.ds(pl.multiple_of((nchunk - 1 - k) * LANE, LANE), LANE)
            ss = lax.dot_general(tri, dc_ref[rows, :], (NN, ((), ())), precision=lax.Precision.HIGHEST,
                                 preferred_element_type=F32) + run_sc[0:1, :]
            z = f_ref[rows, :] + b_ref[...]
            dz = ss * (1.0 / (1.0 + jnp.exp(z)))
            df_ref[rows, :] = dz.astype(BF16)
            run_sc[0:1, :] = ss[0:1, :]
            db_ref[...] += jnp.sum(dz, axis=0, keepdims=True)

    return pl.pallas_call(
        body, name=name, grid=(1,),
        out_shape=(jax.ShapeDtypeStruct((t, LANE), BF16), jax.ShapeDtypeStruct((1, LANE), F32)),
        in_specs=[pl.BlockSpec((t, LANE), lambda i: (0, COL_FG // LANE)), pl.BlockSpec((1, LANE), lambda i: (0, 0)),
                  pl.BlockSpec((t, LANE), lambda i: (0, 0))],
        out_specs=(pl.BlockSpec((t, LANE), lambda i: (0, 0)), pl.BlockSpec((1, LANE), lambda i: (0, 0))),
        scratch_shapes=[pltpu.VMEM((8, LANE), F32)],
        compiler_params=_params(("arbitrary",), 3 * _nbytes((t, LANE), F32)),
    )(proj, fbias_row, dc)


def _lo_mask():
    return lax.broadcasted_iota(jnp.int32, (1, LANE), 1) < HEAD


def _fox_fwd(proj, c_col, c_row, name):
    t = proj.shape[0]
    tq = _pick(t, FOX_TILE, LANE)
    nq = t // tq
    rep = tq // LANE
    scale = HEAD ** -0.5
    cq, ck, cv = COL_FOX // LANE, COL_FOX // LANE + 4, COL_FOX // LANE + 8

    def body(q_ref, k_ref, v_ref, cc_ref, cr_ref, y_ref, lse_ref, m_sc, l_sc, acc_sc):
        iq, ik = pl.program_id(1), pl.program_id(2)
        lo = _lo_mask()

        @pl.when(ik == 0)
        def _():
            m_sc[...] = jnp.full(m_sc.shape, NEG, F32)
            l_sc[...] = jnp.zeros_like(l_sc)
            acc_sc[...] = jnp.zeros_like(acc_sc)

        def step(diag):
            q2 = q_ref[...].astype(BF16)
            k2 = k_ref[...].astype(BF16)
            v2 = v_ref[...].astype(BF16)
            for h in range(2):
                msk = lo if h == 0 else jnp.logical_not(lo)
                qh = jnp.where(msk, q2, jnp.zeros_like(q2))
                s = _dot(qh, k2, NT) * scale
                s = s + jnp.tile(cc_ref[h], (1, rep)) - cr_ref[h]
                if diag:
                    row = lax.broadcasted_iota(jnp.int32, (tq, tq), 0)
                    col = lax.broadcasted_iota(jnp.int32, (tq, tq), 1)
                    s = jnp.where(col <= row, s, NEG)
                m_prev = m_sc[h]
                m_new = jnp.maximum(m_prev, jnp.max(s, axis=1, keepdims=True))
                alpha = jnp.exp(m_prev - m_new)
                p = jnp.exp(s - jnp.tile(m_new, (1, rep)))
                l_sc[h] = alpha * l_sc[h] + jnp.sum(p, axis=1, keepdims=True)
                acc_sc[h] = alpha * acc_sc[h] + _dot(p.astype(BF16), v2, NN)
                m_sc[h] = m_new

        @pl.when(ik < iq)
        def _():
            step(False)

        @pl.when(ik == iq)
        def _():
            step(True)

        @pl.when(ik == nq - 1)
        def _():
            y = jnp.where(lo, acc_sc[0] / l_sc[0], acc_sc[1] / l_sc[1])
            y_ref[...] = y.astype(BF16)
            lse_ref[...] = m_sc[...] + jnp.log(l_sc[...])

    def kv(c):
        return pl.BlockSpec((tq, LANE), lambda j, iq, ik, c=c: (jnp.minimum(ik, iq), c + j))

    return pl.pallas_call(
        body, name=name, grid=(4, nq, nq),
        out_shape=(jax.ShapeDtypeStruct((t, BRANCH), BF16), jax.ShapeDtypeStruct((8, t, LANE), F32)),
        in_specs=[pl.BlockSpec((tq, LANE), lambda j, iq, ik: (iq, cq + j)), kv(ck), kv(cv),
                  pl.BlockSpec((2, tq, LANE), lambda j, iq, ik: (j, iq, 0)),
                  pl.BlockSpec((2, 1, tq), lambda j, iq, ik: (j, 0, jnp.minimum(ik, iq)))],
        out_specs=(pl.BlockSpec((tq, LANE), lambda j, iq, ik: (iq, j)),
                   pl.BlockSpec((2, tq, LANE), lambda j, iq, ik: (j, iq, 0))),
        scratch_shapes=[pltpu.VMEM((2, tq, LANE), F32)] * 3,
        compiler_params=_params(("parallel", "parallel", "arbitrary"),
                                16 * _nbytes((tq, LANE), F32) + 6 * _nbytes((tq, tq), F32)),
    )(proj, proj, proj, c_col, c_row)


def _fox_dq(proj, c_col, c_row, lse, y, dy, name):
    t = proj.shape[0]
    tq = _pick(t, FOX_TILE, LANE)
    nq = t // tq
    rep = tq // LANE
    scale = HEAD ** -0.5
    cq, ck, cv = COL_FOX // LANE, COL_FOX // LANE + 4, COL_FOX // LANE + 8

    def body(q_ref, k_ref, v_ref, cc_ref, cr_ref, lse_ref, y_ref, dy_ref, dq_ref, d_ref, dcq_ref, dq_sc):
        iq, ik = pl.program_id(1), pl.program_id(2)
        lo = _lo_mask()

        @pl.when(ik == 0)
        def _():
            dq_sc[...] = jnp.zeros_like(dq_sc)
            dcq_ref[...] = jnp.zeros_like(dcq_ref)
            prod = y_ref[...].astype(F32) * dy_ref[...].astype(F32)
            d_ref[0] = jnp.broadcast_to(jnp.sum(jnp.where(lo, prod, 0.0), axis=1, keepdims=True), (tq, LANE))
            d_ref[1] = jnp.broadcast_to(jnp.sum(jnp.where(lo, 0.0, prod), axis=1, keepdims=True), (tq, LANE))

        def step(diag):
            q2 = q_ref[...].astype(BF16)
            k2 = k_ref[...].astype(BF16)
            v2 = v_ref[...].astype(BF16)
            do2 = dy_ref[...]
            for h in range(2):
                msk = lo if h == 0 else jnp.logical_not(lo)
                qh = jnp.where(msk, q2, jnp.zeros_like(q2))
                doh = jnp.where(msk, do2, jnp.zeros_like(do2))
                s = _dot(qh, k2, NT) * scale
                s = s + jnp.tile(cc_ref[h], (1, rep)) - cr_ref[h]
                if diag:
                    row = lax.broadcasted_iota(jnp.int32, (tq, tq), 0)
                    col = lax.broadcasted_iota(jnp.int32, (tq, tq), 1)
                    s = jnp.where(col <= row, s, NEG)
                p = jnp.exp(s - jnp.tile(lse_ref[h], (1, rep)))
                dp = _dot(doh, v2, NT)
                ds = p * (dp - jnp.tile(d_ref[h], (1, rep)))
                dq_sc[h] += _dot(ds.astype(BF16), k2, NN)
                dcq_ref[h] += jnp.sum(ds, axis=1, keepdims=True)

        @pl.when(ik < iq)
        def _():
            step(False)

        @pl.when(ik == iq)
        def _():
            step(True)

        @pl.when(ik == nq - 1)
        def _():
            dq_ref[...] = (jnp.where(lo, dq_sc[0], dq_sc[1]) * scale).astype(BF16)

    def kv(c):
        return pl.BlockSpec((tq, LANE), lambda j, iq, ik, c=c: (jnp.minimum(ik, iq), c + j))

    stat = pl.BlockSpec((2, tq, LANE), lambda j, iq, ik: (j, iq, 0))
    pair = pl.BlockSpec((tq, LANE), lambda j, iq, ik: (iq, j))
    return pl.pallas_call(
        body, name=name, grid=(4, nq, nq),
        out_shape=(jax.ShapeDtypeStruct((t, BRANCH), BF16), jax.ShapeDtypeStruct((8, t, LANE), F32),
                   jax.ShapeDtypeStruct((8, t, LANE), F32)),
        in_specs=[pl.BlockSpec((tq, LANE), lambda j, iq, ik: (iq, cq + j)), kv(ck), kv(cv), stat,
                  pl.BlockSpec((2, 1, tq), lambda j, iq, ik: (j, 0, jnp.minimum(ik, iq))), stat, pair, pair],
        out_specs=(pair, stat, stat),
        scratch_shapes=[pltpu.VMEM((2, tq, LANE), F32)],
        compiler_params=_params(("parallel", "parallel", "arbitrary"),
                                20 * _nbytes((tq, LANE), F32) + 8 * _nbytes((tq, tq), F32)),
    )(proj, proj, proj, c_col, c_row, lse, y, dy)


def _fox_dkv(proj, c_col, c_row, lse_row, d_row, dy, name):
    t = proj.shape[0]
    tk = _pick(t, FOX_TILE, LANE)
    nk = t // tk
    rep = tk // LANE
    scale = HEAD ** -0.5
    cq, ck, cv = COL_FOX // LANE, COL_FOX // LANE + 4, COL_FOX // LANE + 8

    def body(k_ref, v_ref, q_ref, dy_ref, ck_ref, cqr_ref, lser_ref, dr_ref,
             dk_ref, dv_ref, dc_ref, dk_sc, dv_sc, dc_sc):
        ik, iq = pl.program_id(1), pl.program_id(2)
        lo = _lo_mask()

        @pl.when(iq == 0)
        def _():
            dk_sc[...] = jnp.zeros_like(dk_sc)
            dv_sc[...] = jnp.zeros_like(dv_sc)
            dc_sc[...] = jnp.zeros_like(dc_sc)

        def step(diag):
            k2 = k_ref[...].astype(BF16)
            v2 = v_ref[...].astype(BF16)
            q2 = q_ref[...].astype(BF16)
            do2 = dy_ref[...]
            for h in range(2):
                msk = lo if h == 0 else jnp.logical_not(lo)
                kh = jnp.where(msk, k2, jnp.zeros_like(k2))
                vh = jnp.where(msk, v2, jnp.zeros_like(v2))
                st = _dot(kh, q2, NT) * scale
                st = st + cqr_ref[h] - jnp.tile(ck_ref[h], (1, rep))
                if diag:
                    krow = lax.broadcasted_iota(jnp.int32, (tk, tk), 0)
                    qcol = lax.broadcasted_iota(jnp.int32, (tk, tk), 1)
                    st = jnp.where(krow <= qcol, st, NEG)
                pt = jnp.exp(st - lser_ref[h])
                dpt = _dot(vh, do2, NT)
                dst = pt * (dpt - dr_ref[h])
                dv_sc[h] += _dot(pt.astype(BF16), do2, NN)
                dk_sc[h] += _dot(dst.astype(BF16), q2, NN)
                dc_sc[h] -= jnp.sum(dst, axis=1, keepdims=True)

        @pl.when(iq > ik)
        def _():
            step(False)

        @pl.when(iq == ik)
        def _():
            step(True)

        @pl.when(iq == nk - 1)
        def _():
            dk_ref[...] = (jnp.where(lo, dk_sc[0], dk_sc[1]) * scale).astype(BF16)
            dv_ref[...] = jnp.where(lo, dv_sc[0], dv_sc[1]).astype(BF16)
            dc_ref[...] = dc_sc[...]

    def kcol(c):
        return pl.BlockSpec((tk, LANE), lambda j, ik, iq, c=c: (ik, c + j))

    qrow = pl.BlockSpec((2, 1, tk), lambda j, ik, iq: (j, 0, jnp.maximum(iq, ik)))
    pair_k = pl.BlockSpec((tk, LANE), lambda j, ik, iq: (ik, j))
    return pl.pallas_call(
        body, name=name, grid=(4, nk, nk),
        out_shape=(jax.ShapeDtypeStruct((t, BRANCH), BF16), jax.ShapeDtypeStruct((t, BRANCH), BF16),
                   jax.ShapeDtypeStruct((8, t, LANE), F32)),
        in_specs=[kcol(ck), kcol(cv),
                  pl.BlockSpec((tk, LANE), lambda j, ik, iq: (jnp.maximum(iq, ik), cq + j)),
                  pl.BlockSpec((tk, LANE), lambda j, ik, iq: (jnp.maximum(iq, ik), j)),
                  pl.BlockSpec((2, tk, LANE), lambda j, ik, iq: (j, ik, 0)), qrow, qrow, qrow],
        out_specs=(pair_k, pair_k, pl.BlockSpec((2, tk, LANE), lambda j, ik, iq: (j, ik, 0))),
        scratch_shapes=[pltpu.VMEM((2, tk, LANE), F32)] * 3,
        compiler_params=_params(("parallel", "parallel", "arbitrary"),
                                24 * _nbytes((tk, LANE), F32) + 8 * _nbytes((tk, tk), F32)),
    )(proj, proj, proj, dy, c_col, c_row, lse_row, d_row)


def _swa_tables(rel_bias):
    tq = np.arange(SWA_BLOCK)[:, None]
    sk = np.arange(2 * SWA_BLOCK)[None, :]
    dist = SWA_BLOCK + tq - sk
    inwin = (dist >= 0) & (dist < SWA_BLOCK)
    n = np.maximum(dist, 0)
    max_exact = N_BUCKETS // 2
    large = max_exact + (np.log(np.maximum(n, 1).astype(np.float32) / max_exact)
                         / math.log(SWA_BLOCK / max_exact) * (N_BUCKETS - max_exact)).astype(np.int32)
    bucket = np.where(n < max_exact, n, np.minimum(large, N_BUCKETS - 1))
    onehot = (bucket[..., None] == np.arange(N_BUCKETS)) & inwin[..., None]
    onehot = jnp.asarray(onehot.astype(np.float32))
    bias = jnp.einsum("tsb,bh->hts", onehot, rel_bias, precision=lax.Precision.HIGHEST)
    bias = jnp.where(jnp.asarray(inwin)[None], bias, NEG)
    return onehot, bias


def _swa_fwd(proj, bias, sink_rep, name):
    t = proj.shape[0]
    nb = t // SWA_BLOCK
    scale = HEAD ** -0.5
    csq, csk, csv = COL_SQ // 256, COL_SK // LANE, COL_SV // LANE

    def body(q_ref, kp_ref, kc_ref, vp_ref, vc_ref, b_ref, sk_ref, y_ref, lse_ref):
        kvh, n = pl.program_id(0), pl.program_id(1)
        lane = lax.broadcasted_iota(jnp.int32, (1, LANE), 1)
        lo = lane < HEAD
        kvm = jnp.logical_and(lane >= kvh * HEAD, lane < (kvh + 1) * HEAD)

        def both(prev_ref, cur_ref):
            band = jnp.concatenate([prev_ref[...], cur_ref[...]], axis=0)
            band = jnp.where(kvm, band, 0.0)
            return (band + pltpu.roll(band, HEAD, 1)).astype(BF16)

        kb, vb = both(kp_ref, kc_ref), both(vp_ref, vc_ref)
        col = lax.broadcasted_iota(jnp.int32, (SWA_BLOCK, 2 * SWA_BLOCK), 1)
        first = jnp.logical_and(n == 0, col < SWA_BLOCK)
        outs = []
        for g in range(SWA_GROUP):
            half = q_ref[:, (g // 2) * LANE:(g // 2 + 1) * LANE]
            hm = lo if g % 2 == 0 else jnp.logical_not(lo)
            qg = jnp.where(hm, half, 0.0).astype(BF16)
            s = _dot(qg, kb, NT) * scale + b_ref[g]
            s = jnp.where(first, NEG, s)
            snk = sk_ref[g:g + 1, :]
            m = jnp.maximum(jnp.max(s, axis=1, keepdims=True), snk)
            p = jnp.exp(s - jnp.tile(m, (1, 2)))
            denom = jnp.sum(p, axis=1, keepdims=True) + jnp.exp(snk - m)
            outs.append(_dot(p.astype(BF16), vb, NN) / denom)
            lse_ref[g] = m + jnp.log(denom)
        y_ref[:, 0:LANE] = jnp.where(lo, outs[0], outs[1]).astype(BF16)
        y_ref[:, LANE:2 * LANE] = jnp.where(lo, outs[2], outs[3]).astype(BF16)

    def blk(c, shift):
        return pl.BlockSpec((SWA_BLOCK, LANE), lambda kvh, n, c=c, s=shift: (jnp.maximum(n - s, 0), c))

    return pl.pallas_call(
        body, name=name, grid=(2, nb),
        out_shape=(jax.ShapeDtypeStruct((t, BRANCH), BF16), jax.ShapeDtypeStruct((8, t, LANE), F32)),
        in_specs=[pl.BlockSpec((SWA_BLOCK, 256), lambda kvh, n: (n, csq + kvh)),
                  blk(csk, 1), blk(csk, 0), blk(csv, 1), blk(csv, 0),
                  pl.BlockSpec((None, SWA_GROUP, SWA_BLOCK, 256), lambda kvh, n: (kvh, 0, 0, 0)),
                  pl.BlockSpec((None, SWA_GROUP, LANE), lambda kvh, n: (kvh, 0, 0))],
        out_specs=(pl.BlockSpec((SWA_BLOCK, 256), lambda kvh, n: (n, kvh)),
                   pl.BlockSpec((SWA_GROUP, SWA_BLOCK, LANE), lambda kvh, n: (kvh, n, 0))),
        compiler_params=_params(("parallel", "arbitrary"), 4 << 20),
    )(proj, proj, proj, proj, proj, bias.reshape(2, SWA_GROUP, SWA_BLOCK, 256), sink_rep)


def _swa_bwd(proj, bias, sink_rep, lse, y, dy, name):
    t = proj.shape[0]
    nb = t // SWA_BLOCK
    scale = HEAD ** -0.5
    csq, csk, csv = COL_SQ // 256, COL_SK // LANE, COL_SV // LANE

    def body(q_ref, kp_ref, kc_ref, vp_ref, vc_ref, b_ref, sk_ref, lse_ref, y_ref, dy_ref,
             dq_ref, dkp_ref, dvp_ref, db_ref, dsk_ref):
        kvh, n = pl.program_id(0), pl.program_id(1)
        lane = lax.broadcasted_iota(jnp.int32, (1, LANE), 1)
        lo = lane < HEAD
        kvm = jnp.logical_and(lane >= kvh * HEAD, lane < (kvh + 1) * HEAD)

        def both(prev_ref, cur_ref):
            band = jnp.concatenate([prev_ref[...], cur_ref[...]], axis=0)
            band = jnp.where(kvm, band, 0.0)
            return (band + pltpu.roll(band, HEAD, 1)).astype(BF16)

        kb, vb = both(kp_ref, kc_ref), both(vp_ref, vc_ref)
        col = lax.broadcasted_iota(jnp.int32, (SWA_BLOCK, 2 * SWA_BLOCK), 1)
        first = jnp.logical_and(n == 0, col < SWA_BLOCK)

        @pl.when(n == 0)
        def _():
            db_ref[...] = jnp.zeros_like(db_ref)
            dsk_ref[...] = jnp.zeros_like(dsk_ref)

        dk_full = jnp.zeros((2 * SWA_BLOCK, LANE), F32)
        dv_full = jnp.zeros((2 * SWA_BLOCK, LANE), F32)
        dqs = []
        for g in range(SWA_GROUP):
            sl = slice((g // 2) * LANE, (g // 2 + 1) * LANE)
            hm = lo if g % 2 == 0 else jnp.logical_not(lo)
            qg = jnp.where(hm, q_ref[:, sl], 0.0).astype(BF16)
            dog = jnp.where(hm, dy_ref[:, sl], jnp.zeros((SWA_BLOCK, LANE), BF16))
            dmat = jnp.where(hm, y_ref[:, sl].astype(F32) * dy_ref[:, sl].astype(F32), 0.0)
            dg = jnp.sum(dmat, axis=1, keepdims=True)
            s = _dot(qg, kb, NT) * scale + b_ref[g]
            s = jnp.where(first, NEG, s)
            lse_g = lse_ref[g]
            p = jnp.exp(s - jnp.tile(lse_g, (1, 2)))
            dp = _dot(dog, vb, NT)
            ds = p * (dp - dg)
            dsb = ds.astype(BF16)
            dqs.append(_dot(dsb, kb, NN) * scale)
            dk_full = dk_full + _dot(dsb, qg, TN)
            dv_full = dv_full + _dot(p.astype(BF16), dog, TN)
            db_ref[g] += ds
            psink = jnp.exp(sk_ref[g:g + 1, :] - lse_g)
            dsk_ref[g:g + 1, :] -= jnp.sum(psink * dg, axis=0, keepdims=True)
        dq_ref[:, 0:LANE] = jnp.where(lo, dqs[0], dqs[1]).astype(BF16)
        dq_ref[:, LANE:2 * LANE] = jnp.where(lo, dqs[2], dqs[3]).astype(BF16)
        dkp_ref[...] = jnp.where(kvm, (dk_full + pltpu.roll(dk_full, HEAD, 1)) * scale, 0.0)
        dvp_ref[...] = jnp.where(kvm, dv_full + pltpu.roll(dv_full, HEAD, 1), 0.0)

    def blk(c, shift):
        return pl.BlockSpec((SWA_BLOCK, LANE), lambda kvh, n, c=c, s=shift: (jnp.maximum(n - s, 0), c))

    qblk = pl.BlockSpec((SWA_BLOCK, 256), lambda kvh, n: (n, kvh))
    part = pl.BlockSpec((None, None, 2 * SWA_BLOCK, LANE), lambda kvh, n: (kvh, n, 0, 0))
    bspec = pl.BlockSpec((None, SWA_GROUP, SWA_BLOCK, 256), lambda kvh, n: (kvh, 0, 0, 0))
    sspec = pl.BlockSpec((None, SWA_GROUP, LANE), lambda kvh, n: (kvh, 0, 0))
    return pl.pallas_call(
        body, name=name, grid=(2, nb),
        out_shape=(jax.ShapeDtypeStruct((t, BRANCH), BF16),
                   jax.ShapeDtypeStruct((2, nb, 2 * SWA_BLOCK, LANE), F32),
                   jax.ShapeDtypeStruct((2, nb, 2 * SWA_BLOCK, LANE), F32),
                   jax.ShapeDtypeStruct((2, SWA_GROUP, SWA_BLOCK, 256), F32),
                   jax.ShapeDtypeStruct((2, SWA_GROUP, LANE), F32)),
        in_specs=[pl.BlockSpec((SWA_BLOCK, 256), lambda kvh, n: (n, csq + kvh)),
                  blk(csk, 1), blk(csk, 0), blk(csv, 1), blk(csv, 0), bspec, sspec,
                  pl.BlockSpec((SWA_GROUP, SWA_BLOCK, LANE), lambda kvh, n: (kvh, n, 0)), qblk, qblk],
        out_specs=(qblk, part, part, bspec, sspec),
        compiler_params=_params(("parallel", "arbitrary"), 6 << 20),
    )(proj, proj, proj, proj, proj, bias.reshape(2, SWA_GROUP, SWA_BLOCK, 256), sink_rep, lse, y, dy)


def _gate_fwd(proj, pb, name):
    t = proj.shape[0]
    tr = _pick(t, ROW_TILE // 2, BF16_SUBLANE)

    def body(g0, g1, g2, p0, p1, p2, o_ref):
        acc = jax.nn.sigmoid(g0[...]) * p0[...]
        acc = acc + jax.nn.sigmoid(g1[...]) * p1[...]
        acc = acc + jax.nn.sigmoid(g2[...]) * p2[...]
        o_ref[...] = acc.astype(BF16)

    row = pl.BlockSpec((tr, D_MODEL), lambda i: (i, 0))
    gates = [pl.BlockSpec((tr, D_MODEL), lambda i, b=b: (i, b)) for b in range(3)]
    return pl.pallas_call(
        body, name=name, grid=(t // tr,),
        out_shape=jax.ShapeDtypeStruct((t, D_MODEL), BF16),
        in_specs=gates + [row] * 3, out_specs=row,
        compiler_params=_params(("parallel",), 7 * _nbytes((tr, D_MODEL), F32)),
    )(proj, proj, proj, *pb)


def _gate_bwd(proj, pb, dmerged, name):
    t = proj.shape[0]
    tr = _pick(t, ROW_TILE // 2, BF16_SUBLANE)

    def body(g0, g1, g2, p0, p1, p2, dm_ref, dp0, dp1, dp2, dg_ref):
        dm = dm_ref[...]
        for b, (g_ref, p_ref, dp_ref) in enumerate(((g0, p0, dp0), (g1, p1, dp1), (g2, p2, dp2))):
            sg = jax.nn.sigmoid(g_ref[...])
            dp_ref[...] = (dm * sg).astype(BF16)
            dg_ref[:, b * D_MODEL:(b + 1) * D_MODEL] = (dm * p_ref[...] * sg * (1.0 - sg)).astype(BF16)

    row = pl.BlockSpec((tr, D_MODEL), lambda i: (i, 0))
    gates = [pl.BlockSpec((tr, D_MODEL), lambda i, b=b: (i, b)) for b in range(3)]
    return pl.pallas_call(
        body, name=name, grid=(t // tr,),
        out_shape=(jax.ShapeDtypeStruct((t, D_MODEL), BF16),) * 3 + (jax.ShapeDtypeStruct((t, 3 * D_MODEL), BF16),),
        in_specs=gates + [row] * 4,
        out_specs=(row, row, row, pl.BlockSpec((tr, 3 * D_MODEL), lambda i: (i, 0))),
        compiler_params=_params(("parallel",), 11 * _nbytes((tr, D_MODEL), F32)),
    )(proj, proj, proj, *pb, dmerged)


def _swiglu_fwd(ab, name):
    t = ab.shape[0]
    tr = _pick(t, ROW_TILE, BF16_SUBLANE)
    tc = D_FF // 2

    def body(a_ref, b_ref, o_ref):
        a = a_ref[...]
        o_ref[...] = (a * jax.nn.sigmoid(a) * b_ref[...]).astype(BF16)

    return pl.pallas_call(
        body, name=name, grid=(t // tr, 2),
        out_shape=jax.ShapeDtypeStruct((t, D_FF), BF16),
        in_specs=[pl.BlockSpec((tr, tc), lambda i, j: (i, j)), pl.BlockSpec((tr, tc), lambda i, j: (i, j + 2))],
        out_specs=pl.BlockSpec((tr, tc), lambda i, j: (i, j)),
        compiler_params=_params(("parallel", "parallel"), 3 * _nbytes((tr, tc), F32)),
    )(ab, ab)


def _swiglu_bwd(ab, dh, name):
    t = ab.shape[0]
    tr = _pick(t, ROW_TILE, BF16_SUBLANE)
    tc = D_FF // 2

    def body(a_ref, b_ref, dh_ref, o_ref):
        jj = pl.program_id(1)
        a, b, d = a_ref[...], b_ref[...], dh_ref[...]
        sg = jax.nn.sigmoid(a)
        da = d * b * (sg * (1.0 + a * (1.0 - sg)))
        db = d * (a * sg)
        o_ref[...] = jnp.where(jj < 2, da, db).astype(BF16)

    return pl.pallas_call(
        body, name=name, grid=(t // tr, 4),
        out_shape=jax.ShapeDtypeStruct((t, 2 * D_FF), BF16),
        in_specs=[pl.BlockSpec((tr, tc), lambda i, j: (i, j % 2)),
                  pl.BlockSpec((tr, tc), lambda i, j: (i, j % 2 + 2)),
                  pl.BlockSpec((tr, tc), lambda i, j: (i, j % 2))],
        out_specs=pl.BlockSpec((tr, tc), lambda i, j: (i, j)),
        compiler_params=_params(("parallel", "parallel"), 4 * _nbytes((tr, tc), F32)),
    )(ab, ab, dh)


def _xattn_fwd(q, kv, name):
    t = q.shape[0]
    tq = _pick(t, ROW_TILE, BF16_SUBLANE)
    mlen = kv.shape[0]
    scale = X_HEAD ** -0.5

    def body(q_ref, kv_ref, o_ref):
        for h in range(X_HEADS):
            sl = slice(h * X_HEAD, (h + 1) * X_HEAD)
            kh = kv_ref[:, sl]
            vh = kv_ref[:, D_MODEL + h * X_HEAD:D_MODEL + (h + 1) * X_HEAD]
            s = _dot(q_ref[:, sl], kh, NT) * scale
            p = jnp.exp(s - jnp.max(s, axis=1, keepdims=True))
            l = jnp.sum(p, axis=1, keepdims=True)
            o_ref[:, sl] = (_dot(p.astype(BF16), vh, NN) / l).astype(BF16)

    return pl.pallas_call(
        body, name=name, grid=(t // tq,),
        out_shape=jax.ShapeDtypeStruct((t, D_MODEL), BF16),
        in_specs=[pl.BlockSpec((tq, D_MODEL), lambda i: (i, 0)), pl.BlockSpec((mlen, 2 * D_MODEL), lambda i: (0, 0))],
        out_specs=pl.BlockSpec((tq, D_MODEL), lambda i: (i, 0)),
        compiler_params=_params(("parallel",), 4 * _nbytes((tq, D_MODEL), F32)),
    )(q, kv)


def _xattn_bwd(q, kv, do, name):
    t = q.shape[0]
    tq = _pick(t, ROW_TILE, BF16_SUBLANE)
    mlen = kv.shape[0]
    scale = X_HEAD ** -0.5

    def body(q_ref, kv_ref, do_ref, dq_ref, dkv_ref):
        i = pl.program_id(0)

        @pl.when(i == 0)
        def _():
            dkv_ref[...] = jnp.zeros_like(dkv_ref)

        for h in range(X_HEADS):
            sl = slice(h * X_HEAD, (h + 1) * X_HEAD)
            vsl = slice(D_MODEL + h * X_HEAD, D_MODEL + (h + 1) * X_HEAD)
            qh, kh, vh, doh = q_ref[:, sl], kv_ref[:, sl], kv_ref[:, vsl], do_ref[:, sl]
            s = _dot(qh, kh, NT) * scale
            p = jnp.exp(s - jnp.max(s, axis=1, keepdims=True))
            p = p / jnp.sum(p, axis=1, keepdims=True)
            dp = _dot(doh, vh, NT)
            ds = p * (dp - jnp.sum(p * dp, axis=1, keepdims=True))
            dsb = ds.astype(BF16)
            dq_ref[:, sl] = (_dot(dsb, kh, NN) * scale).astype(BF16)
            dkv_ref[:, sl] += _dot(dsb, qh, TN) * scale
            dkv_ref[:, vsl] += _dot(p.astype(BF16), doh, TN)

    row = pl.BlockSpec((tq, D_MODEL), lambda i: (i, 0))
    whole = pl.BlockSpec((mlen, 2 * D_MODEL), lambda i: (0, 0))
    return pl.pallas_call(
        body, name=name, grid=(t // tq,),
        out_shape=(jax.ShapeDtypeStruct((t, D_MODEL), BF16), jax.ShapeDtypeStruct((mlen, 2 * D_MODEL), F32)),
        in_specs=[row, whole, row], out_specs=(row, whole),
        compiler_params=_params(("arbitrary",), 6 * _nbytes((tq, D_MODEL), F32)),
    )(q, kv, do)


def _position():
    return lax.axis_index("x"), lax.axis_index("y"), lax.axis_index("c")


def _all_gather(x, name):
    r, l = x.shape

    def body(x_ref, out_ref, send_sems, recv_sems, local_sem):
        mx, my, mc = _position()
        me, sib = (mx, my, mc), (mx, my, 1 - mc)
        chips = [(1 - mx, my), (mx, 1 - my), (1 - mx, 1 - my)]

        def slot(p):
            return out_ref.at[4 * p[0] + 2 * p[1] + p[2]]

        def copy(k, block, to, src=None):
            return pltpu.make_async_remote_copy(
                src_ref=slot(block) if src is None else src, dst_ref=slot(block),
                send_sem=send_sems.at[k], recv_sem=recv_sems.at[k], device_id=to, device_id_type=MESH)

        mine = pltpu.make_async_copy(x_ref, slot(me), local_sem)
        mine.start()
        first = [copy(0, me, sib, src=x_ref)]
        first += [copy(1 + j, me, (*chip, mc), src=x_ref) for j, chip in enumerate(chips)]
        for cp in first:
            cp.start()
        passed = [copy(4 + j, (*chip, mc), sib) for j, chip in enumerate(chips)]
        for j, chip in enumerate(chips):
            copy(1 + j, (*chip, mc), me).wait_recv()
            passed[j].start()
        copy(0, sib, me).wait_recv()
        for j, chip in enumerate(chips):
            copy(4 + j, (*chip, 1 - mc), me).wait_recv()
        for cp in first + passed:
            cp.wait_send()
        mine.wait()

    return pl.pallas_call(
        body, name=name,
        out_shape=jax.ShapeDtypeStruct((N_DEV, r, l), x.dtype),
        in_specs=[pl.BlockSpec(memory_space=pl.ANY)], out_specs=pl.BlockSpec(memory_space=pl.ANY),
        scratch_shapes=[pltpu.SemaphoreType.DMA((7,)), pltpu.SemaphoreType.DMA((7,)), pltpu.SemaphoreType.DMA],
    )(x)


def _exchange(parts, name):
    _, r, l = parts.shape
    rels = [(dx, dy, dc) for dx in (0, 1) for dy in (0, 1) for dc in (0, 1)][1:]

    def body(g_ref, out_ref, send_sems, recv_sems, local_sem):
        mx, my, mc = _position()
        me_idx = 4 * mx + 2 * my + mc

        def peer(rel):
            return tuple((1 - v) if f else v for f, v in zip(rel, (mx, my, mc)))

        def copy(k, p):
            p_idx = 4 * p[0] + 2 * p[1] + p[2]
            return pltpu.make_async_remote_copy(
                src_ref=g_ref.at[p_idx], dst_ref=out_ref.at[me_idx],
                send_sem=send_sems.at[k], recv_sem=recv_sems.at[k], device_id=p, device_id_type=MESH)

        def arrival(k, p):
            p_idx = 4 * p[0] + 2 * p[1] + p[2]
            return pltpu.make_async_remote_copy(
                src_ref=g_ref.at[me_idx], dst_ref=out_ref.at[p_idx],
                send_sem=send_sems.at[k], recv_sem=recv_sems.at[k], device_id=p, device_id_type=MESH)

        mine = pltpu.make_async_copy(g_ref.at[me_idx], out_ref.at[me_idx], local_sem)
        mine.start()
        sends = [copy(k, peer(rel)) for k, rel in enumerate(rels)]
        for cp in sends:
            cp.start()
        for k, rel in enumerate(rels):
            arrival(k, peer(rel)).wait_recv()
        for cp in sends:
            cp.wait_send()
        mine.wait()

    return pl.pallas_call(
        body, name=name,
        out_shape=jax.ShapeDtypeStruct((N_DEV, r, l), parts.dtype),
        in_specs=[pl.BlockSpec(memory_space=pl.ANY)], out_specs=pl.BlockSpec(memory_space=pl.ANY),
        scratch_shapes=[pltpu.SemaphoreType.DMA((7,)), pltpu.SemaphoreType.DMA((7,)), pltpu.SemaphoreType.DMA],
    )(parts)


def _adamw(parts, w, m, v, name):
    r, l = w.shape
    tr = _pick(r, PACK_ROWS, BF16_SUBLANE)
    c1 = 1.0 - ADAM_B1 ** ADAM_STEP
    c2 = 1.0 - ADAM_B2 ** ADAM_STEP

    def body(p_ref, w_ref, m_ref, v_ref, g_ref, d_ref, nm_ref, nv_ref):
        g = p_ref[0].astype(F32)
        for s in range(1, N_DEV):
            g = g + p_ref[s].astype(F32)
        nm = ADAM_B1 * m_ref[...] + (1.0 - ADAM_B1) * g
        nv = ADAM_B2 * v_ref[...] + (1.0 - ADAM_B2) * (g * g)
        m_hat = nm / c1
        v_hat = nv / c2
        g_ref[...] = g
        d_ref[...] = -ADAM_LR * (m_hat / (jnp.sqrt(v_hat) + ADAM_EPS) + ADAM_WD * w_ref[...])
        nm_ref[...] = nm
        nv_ref[...] = nv

    row = pl.BlockSpec((tr, l), lambda i: (i, 0))
    return pl.pallas_call(
        body, name=name, grid=(r // tr,),
        out_shape=(jax.ShapeDtypeStruct((r, l), F32),) * 4,
        in_specs=[pl.BlockSpec((N_DEV, tr, l), lambda i: (0, i, 0)), row, row, row],
        out_specs=(row,) * 4,
        compiler_params=_params(("parallel",), 12 * _nbytes((tr, l), F32)),
    )(parts, w, m, v)


MATRIX_WEIGHTS = (("w_in", 2), ("conv_w", 2), ("w_branch", 3), ("w_mix_out", 1), ("w_xq", 1), ("w_xkv", 2),
                  ("w_xo", 1), ("w_ffn_gate", 2), ("w_ffn_up", 2), ("w_ffn_down", 1))
SMALL_PARAMS = ("mix_norm_g", "xattn_norm_g", "mem_norm_g", "ffn_norm_g", "final_norm_g", "forget_bias", "sink",
                "rel_bias")


def _pack_rows(flat_parts, dtype, lead=()):
    flat = jnp.concatenate([p.astype(dtype) for p in flat_parts], axis=-1)
    n = flat.shape[-1]
    quantum = LANE * (PACK_ROWS if n > LANE * PACK_ROWS else 8)
    total = -(-n // quantum) * quantum
    flat = jnp.pad(flat, [(0, 0)] * len(lead) + [(0, total - n)])
    return flat.reshape(lead + (total // LANE, LANE))


def _to_full(gathered, axis):
    moved = jnp.moveaxis(gathered, 0, axis)
    shape = list(moved.shape)
    shape[axis:axis + 2] = [shape[axis] * shape[axis + 1]]
    return moved.reshape(shape)


def _to_blocks(full, axis):
    shape = list(full.shape)
    shape[axis:axis + 1] = [N_DEV, shape[axis] // N_DEV]
    return jnp.moveaxis(full.reshape(shape), axis, 0)


def _perm_in(w_in):
    pad = jnp.zeros((w_in.shape[0], PROJ_COLS - IN_COLS), w_in.dtype)
    return jnp.concatenate([w_in[:, 3848:6920], w_in[:, 0:3072], w_in[:, 3080:3848], w_in[:, 3072:3080], pad], axis=1)


def _unperm_in(dw):
    return jnp.concatenate([dw[:, 3072:6144], dw[:, 6912:6920], dw[:, 6144:6912], dw[:, 0:3072]], axis=1)


def _layer_fwd(l, x, mem, wt, sm):
    t = x.shape[0]
    tag = f"l{l}_"
    h = _rms_fwd(x, sm["mix_norm_g"][l], tag + "mix_norm")
    proj = _matmul(h, wt["w_in"][l], "nn", F32, tag + "in_proj")
    y_conv = _conv_fwd(proj, wt["conv_w"][l], tag + "conv")
    fbias_row = jnp.pad(sm["forget_bias"][l], (0, LANE - 8)).reshape(1, LANE)
    c = _logf_cumsum(proj, fbias_row, tag + "logf_cumsum")
    c8 = c[:, :8].T
    c_col = jnp.broadcast_to(c8[:, :, None], (8, t, LANE))
    c_row = c8.reshape(8, 1, t)
    y_fox, lse_fox = _fox_fwd(proj, c_col, c_row, tag + "fox")
    onehot, bias = _swa_tables(sm["rel_bias"])
    sink_rep = jnp.broadcast_to(sm["sink"][l].reshape(2, SWA_GROUP, 1), (2, SWA_GROUP, LANE))
    y_swa, lse_swa = _swa_fwd(proj, bias, sink_rep, tag + "swa")
    ys = (y_conv, y_fox, y_swa)
    pb = tuple(_matmul(ys[b], wt["w_branch"][l, b], "nn", F32, tag + f"branch{b}") for b in range(3))
    merged = _gate_fwd(proj, pb, tag + "gate")
    x1 = _matmul(merged, wt["w_mix_out"][l], "nn", F32, tag + "mix_out", residual=x)
    xn2 = _rms_fwd(x1, sm["xattn_norm_g"][l], tag + "xattn_norm")
    q = _matmul(xn2, wt["w_xq"][l], "nn", BF16, tag + "xq")
    mem_n = _rms_fwd(mem, sm["mem_norm_g"][l], tag + "mem_norm")
    kv = _matmul(mem_n, wt["w_xkv"][l], "nn", BF16, tag + "xkv")
    o = _xattn_fwd(q, kv, tag + "xattn")
    x2 = _matmul(o, wt["w_xo"][l], "nn", F32, tag + "xo", residual=x1)
    xn3 = _rms_fwd(x2, sm["ffn_norm_g"][l], tag + "ffn_norm")
    ab = _matmul(xn3, wt["w_gu"][l], "nn", F32, tag + "ffn_gu")
    h1 = _swiglu_fwd(ab, tag + "swiglu")
    x3 = _matmul(h1, wt["w_ffn_down"][l], "nn", F32, tag + "ffn_down", residual=x2)
    saved = dict(x=x, h=h, proj=proj, fbias_row=fbias_row, c_col=c_col, c_row=c_row, ys=ys, lse_fox=lse_fox,
                 onehot=onehot, bias=bias, sink_rep=sink_rep, lse_swa=lse_swa, pb=pb, merged=merged, x1=x1,
                 xn2=xn2, q=q, mem_n=mem_n, kv=kv, o=o, x2=x2, xn3=xn3, ab=ab, h1=h1)
    return x3, saved


def _layer_bwd(l, dx3, dx3_b, mem, wt, sm, sv):
    t = dx3.shape[0]
    nb = t // SWA_BLOCK
    tag = f"l{l}_b_"
    gw, gs = {}, {}
    dh1 = _matmul(dx3_b, wt["w_ffn_down"][l], "nt", F32, tag + "d_h1")
    gw["w_ffn_down"] = _matmul(sv["h1"], dx3_b, "tn", F32, tag + "dw_down")
    dab = _swiglu_bwd(sv["ab"], dh1, tag + "swiglu")
    dxn3 = _matmul(dab, wt["w_gu"][l], "nt", F32, tag + "d_xn3")
    dw_gu = _matmul(sv["xn3"], dab, "tn", F32, tag + "dw_gu")
    gw["w_ffn_gate"], gw["w_ffn_up"] = dw_gu[:, :D_FF], dw_gu[:, D_FF:]
    dx2, dx2_b, gs["ffn_norm_g"] = _rms_bwd(sv["x2"], sm["ffn_norm_g"][l], dxn3, dx3, tag + "ffn_norm")
    do = _matmul(dx2_b, wt["w_xo"][l], "nt", BF16, tag + "d_o")
    gw["w_xo"] = _matmul(sv["o"], dx2_b, "tn", F32, tag + "dw_xo")
    dq, dkv = _xattn_bwd(sv["q"], sv["kv"], do, tag + "xattn")
    gw["w_xkv"] = _matmul(sv["mem_n"], dkv, "tn", F32, tag + "dw_xkv")
    dmem_n = _matmul(dkv, wt["w_xkv"][l], "nt", F32, tag + "d_memn")
    _, _, gs["mem_norm_g"] = _rms_bwd(mem, sm["mem_norm_g"][l], dmem_n, None, tag + "mem_norm")
    gw["w_xq"] = _matmul(sv["xn2"], dq, "tn", F32, tag + "dw_xq")
    dxn2 = _matmul(dq, wt["w_xq"][l], "nt", F32, tag + "d_xn2")
    dx1, dx1_b, gs["xattn_norm_g"] = _rms_bwd(sv["x1"], sm["xattn_norm_g"][l], dxn2, dx2, tag + "xattn_norm")
    dmerged = _matmul(dx1_b, wt["w_mix_out"][l], "nt", F32, tag + "d_merged")
    gw["w_mix_out"] = _matmul(sv["merged"], dx1_b, "tn", F32, tag + "dw_mix_out")
    dp0, dp1, dp2, dgate = _gate_bwd(sv["proj"], sv["pb"], dmerged, tag + "gate")
    dps = (dp0, dp1, dp2)
    dy_dtypes = (F32, BF16, BF16)
    dys = [_matmul(dps[b], wt["w_branch"][l, b], "nt", dy_dtypes[b], tag + f"d_y{b}") for b in range(3)]
    gw["w_branch"] = jnp.stack(
        [_matmul(sv["ys"][b], dps[b], "tn", F32, tag + f"dw_branch{b}") for b in range(3)])
    dsq, dkp, dvp, dbias, dsink = _swa_bwd(sv["proj"], sv["bias"], sv["sink_rep"], sv["lse_swa"], sv["ys"][2],
                                           dys[2], tag + "swa")

    def band_add(part):
        tot = part[0] + part[1]
        cur = tot[:, SWA_BLOCK:, :]
        nxt = jnp.concatenate([tot[1:, :SWA_BLOCK, :], jnp.zeros((1, SWA_BLOCK, LANE), F32)], axis=0)
        return (cur + nxt).reshape(t, LANE).astype(BF16)

    dsk, dsv = band_add(dkp), band_add(dvp)
    gs["rel_bias_l"] = jnp.einsum("hts,tsb->bh", dbias.reshape(8, SWA_BLOCK, 2 * SWA_BLOCK), sv["onehot"],
                                  precision=lax.Precision.HIGHEST)
    gs["sink"] = dsink[:, :, 0].reshape(8)
    lse_row = sv["lse_fox"][:, :, 0].reshape(8, 1, t)
    dfq, d_fox, dcq_rep = _fox_dq(sv["proj"], sv["c_col"], sv["c_row"], sv["lse_fox"], sv["ys"][1], dys[1], tag + "fox_dq")
    d_row = d_fox[:, :, 0].reshape(8, 1, t)
    dfk, dfv, dc_rep = _fox_dkv(sv["proj"], sv["c_col"], sv["c_row"], lse_row, d_row, dys[1], tag + "fox_dkv")
    dc = jnp.pad((dcq_rep[:, :, 0] + dc_rep[:, :, 0]).T, ((0, 0), (0, LANE - 8)))
    dfg, dfb = _logf_cumsum_bwd(sv["proj"], sv["fbias_row"], dc, tag + "logf_cumsum")
    gs["forget_bias"] = dfb[0, :8]
    dcb, dcc, dcu, dconv = _conv_bwd(sv["proj"], wt["conv_w"][l], dys[0], tag + "conv")
    gw["conv_w"] = dconv[:3]
    dproj = jnp.concatenate([dgate, dcb, dcc, dcu, dfq, dfk, dfv, dsq, dsk, dsv, dfg], axis=1)
    dh = _matmul(dproj, wt["w_in"][l], "nt", F32, tag + "d_h")
    gw["w_in"] = _unperm_in(_matmul(sv["h"], dproj, "tn", F32, tag + "dw_in"))
    dx, dx_b, gs["mix_norm_g"] = _rms_bwd(sv["x"], sm["mix_norm_g"][l], dh, dx1, tag + "mix_norm")
    return dx, dx_b, gw, gs


def kernel(x, mem, mix_norm_g, w_in, forget_bias, conv_w, sink, w_branch, w_mix_out, rel_bias, xattn_norm_g, mem_norm_g, w_xq, w_xkv, w_xo, ffn_norm_g, w_ffn_gate, w_ffn_up, w_ffn_down, final_norm_g, loss_target, m_mix_norm_g, m_w_in, m_forget_bias, m_conv_w, m_sink, m_w_branch, m_w_mix_out, m_rel_bias, m_xattn_norm_g, m_mem_norm_g, m_w_xq, m_w_xkv, m_w_xo, m_ffn_norm_g, m_w_ffn_gate, m_w_ffn_up, m_w_ffn_down, m_final_norm_g, v_mix_norm_g, v_w_in, v_forget_bias, v_conv_w, v_sink, v_w_branch, v_w_mix_out, v_rel_bias, v_xattn_norm_g, v_mem_norm_g, v_w_xq, v_w_xkv, v_w_xo, v_ffn_norm_g, v_w_ffn_gate, v_w_ffn_up, v_w_ffn_down, v_final_norm_g):
    args = dict(locals())
    names = [n for n, _ in MATRIX_WEIGHTS] + list(SMALL_PARAMS)
    w = {n: args[n] for n in names}
    mo = {n: args["m_" + n] for n in names}
    vo = {n: args["v_" + n] for n in names}
    x2d, mem2d, tgt = x[0], mem[0], loss_target[0]

    pieces = []
    for n, _ in MATRIX_WEIGHTS:
        if n == "conv_w":
            pieces.append(lax.bitcast_convert_type(w[n], BF16).reshape(-1))
        else:
            pieces.append(w[n].astype(BF16).reshape(-1))
    sizes = [p.shape[0] for p in pieces]
    gathered = _all_gather(_pack_rows(pieces, BF16), "weights_all_gather").reshape(N_DEV, -1)
    wt, off = {}, 0
    for (n, ax), sz in zip(MATRIX_WEIGHTS, sizes):
        blk = gathered[:, off:off + sz]
        off += sz
        if n == "conv_w":
            blk = lax.bitcast_convert_type(blk.reshape((N_DEV,) + w[n].shape + (2,)), F32)
        else:
            blk = blk.reshape((N_DEV,) + w[n].shape)
        wt[n] = _to_full(blk, ax)
    wt["w_in"] = jnp.stack([_perm_in(wt["w_in"][l]) for l in range(DEPTH)])
    wt["w_gu"] = jnp.concatenate([wt["w_ffn_gate"], wt["w_ffn_up"]], axis=2)
    sm = {n: w[n] for n in SMALL_PARAMS}

    saved = []
    xc = x2d
    for l in range(DEPTH):
        xc, sv = _layer_fwd(l, xc, mem2d, wt, sm)
        saved.append(sv)
    loss_row, dx, dx_b, dg_final = _loss_head(xc, sm["final_norm_g"], tgt, "loss_head")
    loss = lax.psum(loss_row[0, 0], ("x", "y", "c"))
    gw_all, gs_all = [None] * DEPTH, [None] * DEPTH
    for l in reversed(range(DEPTH)):
        dx, dx_b, gw_all[l], gs_all[l] = _layer_bwd(l, dx, dx_b, mem2d, wt, sm, saved[l])
    grad_x = dx[None]

    gfull = {n: jnp.stack([gw_all[l][n] for l in range(DEPTH)]) for n, _ in MATRIX_WEIGHTS}
    gparts = [_to_blocks(gfull[n], ax).reshape(N_DEV, -1) for n, ax in MATRIX_WEIGHTS]
    gsizes = [p.shape[1] for p in gparts]
    recv = _exchange(_pack_rows(gparts, BF16, lead=(N_DEV,)), "grads_exchange")

    def pack_local(d):
        return _pack_rows([d[n].reshape(-1) for n, _ in MATRIX_WEIGHTS], F32)

    outs_big = _adamw(recv, pack_local(w), pack_local(mo), pack_local(vo), "adamw_matrix")

    def unpack_big(packed):
        flat, res, o = packed.reshape(-1), {}, 0
        for (n, _), sz in zip(MATRIX_WEIGHTS, gsizes):
            res[n] = flat[o:o + sz].reshape(w[n].shape)
            o += sz
        return res

    gsm = {n: jnp.stack([gs_all[l][n] for l in range(DEPTH)])
           for n in ("mix_norm_g", "xattn_norm_g", "mem_norm_g", "ffn_norm_g", "forget_bias", "sink")}
    gsm = {n: g.reshape(w[n].shape) for n, g in gsm.items()}
    gsm["final_norm_g"] = dg_final.reshape(-1)
    gsm["rel_bias"] = gs_all[0]["rel_bias_l"] + gs_all[1]["rel_bias_l"]

    def pack_small(d):
        return _pack_rows([d[n].reshape(-1) for n in SMALL_PARAMS], F32)

    small_parts = _all_gather(pack_small(gsm), "small_grads_all_gather")
    outs_small = _adamw(small_parts, pack_small(w), pack_small(mo), pack_small(vo), "adamw_small")

    def unpack_small(packed):
        flat, res, o = packed.reshape(-1), {}, 0
        for n in SMALL_PARAMS:
            sz = int(np.prod(w[n].shape))
            res[n] = flat[o:o + sz].reshape(w[n].shape)
            o += sz
        return res

    order = ["mix_norm_g", "w_in", "forget_bias", "conv_w", "sink", "w_branch", "w_mix_out", "rel_bias",
             "xattn_norm_g", "mem_norm_g", "w_xq", "w_xkv", "w_xo", "ffn_norm_g", "w_ffn_gate", "w_ffn_up",
             "w_ffn_down", "final_norm_g"]
    result = [loss, grad_x]
    for kind in range(4):
        merged = {**unpack_big(outs_big[kind]), **unpack_small(outs_small[kind])}
        result += [merged[n] for n in order]
    return tuple(result)
```

```python
import math

import numpy as np
import jax
import jax.numpy as jnp
from jax import lax
from jax.experimental import pallas as pl
from jax.experimental.pallas import tpu as pltpu

F32 = jnp.float32
BF16 = jnp.bfloat16
MESH = pl.DeviceIdType.MESH

LANE = 128
BF16_SUBLANE = 16
V7X_VMEM_REQUEST_CAP = 56 * 2 ** 20
N_DEV = 8

D_MODEL = 1024
DEPTH = 2
HEAD = 64
BRANCH = 512
SWA_BLOCK = 128
SWA_GROUP = 4
N_BUCKETS = 32
X_HEADS = 4
X_HEAD = 256
D_FF = 2816
RMS_EPS = 1e-6
NEG = -1e30
ADAM_LR, ADAM_B1, ADAM_B2, ADAM_EPS, ADAM_WD, ADAM_STEP = 0.001, 0.9, 0.999, 1e-08, 0.01, 10

IN_COLS = 6920
PROJ_COLS = 7040
COL_GATE, COL_CONV, COL_FOX, COL_SQ, COL_SK, COL_SV, COL_FG = 0, 3072, 4608, 6144, 6656, 6784, 6912

ROW_TILE = 512
FOX_TILE = 512
MM_TM, MM_TN, MM_TK = 1024, 1536, 1024


def _pick(n, cap, mult):
    best = None
    for d in range(mult, min(n, cap) + 1, mult):
        if n % d == 0:
            best = d
    return n if best is None else best


def _params(semantics, block_bytes):
    limit = int(min(max(2 * block_bytes + (8 << 20), 24 << 20), V7X_VMEM_REQUEST_CAP))
    return pltpu.CompilerParams(dimension_semantics=semantics, vmem_limit_bytes=limit)


def _nbytes(shape, dtype):
    return int(np.prod(shape)) * jnp.dtype(dtype).itemsize


def _dot(a, b, dims):
    return lax.dot_general(a, b, (dims, ((), ())), preferred_element_type=F32)


NN = ((1,), (0,))
NT = ((1,), (1,))
TN = ((0,), (0,))


def _matmul(a, b, mode, out_dtype, name, residual=None):
    if mode == "nn":
        (m, k), (k2, n) = a.shape, b.shape
    elif mode == "nt":
        (m, k), (n, k2) = a.shape, b.shape
    else:
        (k, m), (k2, n) = a.shape, b.shape
    assert k == k2, (name, a.shape, b.shape)
    tm, tn, tk = _pick(m, MM_TM, LANE), _pick(n, MM_TN, LANE), _pick(k, MM_TK, LANE)
    nk = k // tk
    dims = {"nn": NN, "nt": NT, "tn": TN}[mode]
    has_res = residual is not None

    def body(*refs):
        a_ref, b_ref = refs[0], refs[1]
        r_ref = refs[2] if has_res else None
        o_ref = refs[3] if has_res else refs[2]
        kk = pl.program_id(2)
        p = _dot(a_ref[...].astype(BF16), b_ref[...].astype(BF16), dims)
        if nk == 1:
            if has_res:
                p = p + r_ref[...]
            o_ref[...] = p.astype(out_dtype)
        else:
            acc_ref = refs[-1]

            @pl.when(kk == 0)
            def _():
                acc_ref[...] = p

            @pl.when(kk > 0)
            def _():
                acc_ref[...] += p

            @pl.when(kk == nk - 1)
            def _():
                res = acc_ref[...]
                if has_res:
                    res = res + r_ref[...]
                o_ref[...] = res.astype(out_dtype)

    if mode == "nn":
        a_spec = pl.BlockSpec((tm, tk), lambda i, j, kk: (i, kk))
        b_spec = pl.BlockSpec((tk, tn), lambda i, j, kk: (kk, j))
    elif mode == "nt":
        a_spec = pl.BlockSpec((tm, tk), lambda i, j, kk: (i, kk))
        b_spec = pl.BlockSpec((tn, tk), lambda i, j, kk: (j, kk))
    else:
        a_spec = pl.BlockSpec((tk, tm), lambda i, j, kk: (kk, i))
        b_spec = pl.BlockSpec((tk, tn), lambda i, j, kk: (kk, j))
    o_spec = pl.BlockSpec((tm, tn), lambda i, j, kk: (i, j))
    in_specs, args = [a_spec, b_spec], [a, b]
    if has_res:
        in_specs.append(o_spec)
        args.append(residual)
    blk = (_nbytes((tm, tk), a.dtype) + _nbytes((tk, tn), b.dtype) + _nbytes((tm, tn), out_dtype)
           + (_nbytes((tm, tn), F32) if has_res else 0))
    scratch = [pltpu.VMEM((tm, tn), F32)] if nk > 1 else []
    return pl.pallas_call(
        body, name=name, grid=(m // tm, n // tn, nk),
        out_shape=jax.ShapeDtypeStruct((m, n), out_dtype),
        in_specs=in_specs, out_specs=o_spec, scratch_shapes=scratch,
        compiler_params=_params(("parallel", "parallel", "arbitrary"), blk + _nbytes((tm, tn), F32)),
    )(*args)


def _rms_fwd(x, g, name):
    t, d = x.shape
    tr = _pick(t, ROW_TILE, BF16_SUBLANE)

    def body(x_ref, g_ref, y_ref):
        xv = x_ref[...]
        r = lax.rsqrt(jnp.mean(xv * xv, axis=-1, keepdims=True) + RMS_EPS)
        y_ref[...] = ((xv * r) * g_ref[...]).astype(BF16)

    return pl.pallas_call(
        body, name=name, grid=(t // tr,),
        out_shape=jax.ShapeDtypeStruct((t, d), BF16),
        in_specs=[pl.BlockSpec((tr, d), lambda i: (i, 0)), pl.BlockSpec((1, d), lambda i: (0, 0))],
        out_specs=pl.BlockSpec((tr, d), lambda i: (i, 0)),
        compiler_params=_params(("parallel",), 2 * _nbytes((tr, d), F32)),
    )(x, g.reshape(1, d))


def _rms_bwd(x, g, dy, dres, name):
    t, d = x.shape
    tr = _pick(t, ROW_TILE, BF16_SUBLANE)
    has_res = dres is not None

    def body(*refs):
        x_ref, g_ref, dy_ref = refs[:3]
        r_ref = refs[3] if has_res else None
        dx_ref, dxb_ref, dg_ref = refs[-3:]
        i = pl.program_id(0)
        xv = x_ref[...]
        r = lax.rsqrt(jnp.mean(xv * xv, axis=-1, keepdims=True) + RMS_EPS)
        xh = xv * r
        dyv = dy_ref[...].astype(F32)
        dxh = dyv * g_ref[...]
        dx = r * (dxh - xh * jnp.mean(dxh * xh, axis=-1, keepdims=True))
        if has_res:
            dx = dx + r_ref[...]
        dx_ref[...] = dx
        dxb_ref[...] = dx.astype(BF16)

        @pl.when(i == 0)
        def _():
            dg_ref[...] = jnp.zeros_like(dg_ref)

        dg_ref[...] += jnp.sum(dyv * xh, axis=0, keepdims=True)

    row = pl.BlockSpec((tr, d), lambda i: (i, 0))
    vec = pl.BlockSpec((1, d), lambda i: (0, 0))
    in_specs, args = [row, vec, row], [x, g.reshape(1, d), dy]
    if has_res:
        in_specs.append(row)
        args.append(dres)
    return pl.pallas_call(
        body, name=name, grid=(t // tr,),
        out_shape=(jax.ShapeDtypeStruct((t, d), F32), jax.ShapeDtypeStruct((t, d), BF16),
                   jax.ShapeDtypeStruct((1, d), F32)),
        in_specs=in_specs, out_specs=(row, row, vec),
        compiler_params=_params(("arbitrary",), 5 * _nbytes((tr, d), F32)),
    )(*args)


def _loss_head(x, g, target, name):
    t, d = x.shape
    tr = _pick(t, ROW_TILE, BF16_SUBLANE)

    def body(x_ref, g_ref, t_ref, loss_ref, dx_ref, dxb_ref, dg_ref):
        i = pl.program_id(0)
        xv = x_ref[...]
        gv = g_ref[...]
        r = lax.rsqrt(jnp.mean(xv * xv, axis=-1, keepdims=True) + RMS_EPS)
        xh = xv * r
        diff = xh * gv - t_ref[...]
        part = 0.5 * jnp.sum(jnp.mean(diff * diff, axis=-1, keepdims=True), axis=0, keepdims=True)
        dyv = diff * (1.0 / d)
        dxh = dyv * gv
        dx = r * (dxh - xh * jnp.mean(dxh * xh, axis=-1, keepdims=True))
        dx_ref[...] = dx
        dxb_ref[...] = dx.astype(BF16)

        @pl.when(i == 0)
        def _():
            dg_ref[...] = jnp.zeros_like(dg_ref)
            loss_ref[...] = jnp.zeros_like(loss_ref)

        dg_ref[...] += jnp.sum(dyv * xh, axis=0, keepdims=True)
        loss_ref[...] += jnp.broadcast_to(part, loss_ref.shape)

    row = pl.BlockSpec((tr, d), lambda i: (i, 0))
    vec = pl.BlockSpec((1, d), lambda i: (0, 0))
    return pl.pallas_call(
        body, name=name, grid=(t // tr,),
        out_shape=(jax.ShapeDtypeStruct((1, LANE), F32), jax.ShapeDtypeStruct((t, d), F32),
                   jax.ShapeDtypeStruct((t, d), BF16), jax.ShapeDtypeStruct((1, d), F32)),
        in_specs=[row, vec, row],
        out_specs=(pl.BlockSpec((1, LANE), lambda i: (0, 0)), row, row, vec),
        compiler_params=_params(("arbitrary",), 5 * _nbytes((tr, d), F32)),
    )(x, g.reshape(1, d), target)


HALO = 8


def _conv_fwd(proj, conv_w, name):
    t = proj.shape[0]
    tr = _pick(t, ROW_TILE, BF16_SUBLANE)
    c0 = COL_CONV // BRANCH
    hb = tr // HALO

    def body(cb_ref, cc_ref, cu_ref, hc_ref, hu_ref, w_ref, y_ref):
        i = pl.program_id(0)
        z = cc_ref[...] * cu_ref[...]
        hz = jnp.where(i > 0, hc_ref[...] * hu_ref[...], 0.0)
        zf = jnp.concatenate([hz, z], axis=0)
        z1 = pltpu.roll(zf, 1, 0)[HALO:]
        z2 = pltpu.roll(zf, 2, 0)[HALO:]
        y = w_ref[2:3, :] * z + w_ref[1:2, :] * z1 + w_ref[0:1, :] * z2
        y_ref[...] = (cb_ref[...] * y).astype(BF16)

    def col(c):
        return pl.BlockSpec((tr, BRANCH), lambda i, c=c: (i, c0 + c))

    def prev(c):
        return pl.BlockSpec((HALO, BRANCH), lambda i, c=c: (jnp.maximum(i * hb - 1, 0), c0 + c))

    return pl.pallas_call(
        body, name=name, grid=(t // tr,),
        out_shape=jax.ShapeDtypeStruct((t, BRANCH), BF16),
        in_specs=[col(0), col(1), col(2), prev(1), prev(2), pl.BlockSpec((3, BRANCH), lambda i: (0, 0))],
        out_specs=pl.BlockSpec((tr, BRANCH), lambda i: (i, 0)),
        compiler_params=_params(("parallel",), 6 * _nbytes((tr, BRANCH), F32)),
    )(proj, proj, proj, proj, proj, conv_w)


def _conv_bwd(proj, conv_w, dout, name):
    t = proj.shape[0]
    tr = _pick(t, ROW_TILE, BF16_SUBLANE)
    nblk = t // tr
    c0 = COL_CONV // BRANCH
    hb = tr // HALO
    last_halo = t // HALO - 1

    def body(cb_ref, cc_ref, cu_ref, hc_ref, hu_ref, do_ref, ndo_ref, ncb_ref, w_ref,
             dcb_ref, dcc_ref, dcu_ref, dw_ref):
        i = pl.program_id(0)
        cb, cc, cu = cb_ref[...], cc_ref[...], cu_ref[...]
        w0, w1, w2 = w_ref[0:1, :], w_ref[1:2, :], w_ref[2:3, :]
        z = cc * cu
        hz = jnp.where(i > 0, hc_ref[...] * hu_ref[...], 0.0)
        zf = jnp.concatenate([hz, z], axis=0)
        z1 = pltpu.roll(zf, 1, 0)[HALO:]
        z2 = pltpu.roll(zf, 2, 0)[HALO:]
        y = w2 * z + w1 * z1 + w0 * z2
        dout_v = do_ref[...]
        dyc = dout_v * cb
        hdy = jnp.where(i < nblk - 1, ndo_ref[...] * ncb_ref[...], 0.0)
        dyf = jnp.concatenate([dyc, hdy], axis=0)
        dy1 = pltpu.roll(dyf, tr + HALO - 1, 0)[:tr]
        dy2 = pltpu.roll(dyf, tr + HALO - 2, 0)[:tr]
        dz = w2 * dyc + w1 * dy1 + w0 * dy2
        dcb_ref[...] = (dout_v * y).astype(BF16)
        dcc_ref[...] = (dz * cu).astype(BF16)
        dcu_ref[...] = (dz * cc).astype(BF16)

        @pl.when(i == 0)
        def _():
            dw_ref[...] = jnp.zeros_like(dw_ref)

        dw_ref[0:1, :] += jnp.sum(dyc * z2, axis=0, keepdims=True)
        dw_ref[1:2, :] += jnp.sum(dyc * z1, axis=0, keepdims=True)
        dw_ref[2:3, :] += jnp.sum(dyc * z, axis=0, keepdims=True)

    def col(c):
        return pl.BlockSpec((tr, BRANCH), lambda i, c=c: (i, c0 + c))

    def prev(c):
        return pl.BlockSpec((HALO, BRANCH), lambda i, c=c: (jnp.maximum(i * hb - 1, 0), c0 + c))

    def nxt(c):
        return pl.BlockSpec((HALO, BRANCH), lambda i, c=c: (jnp.minimum((i + 1) * hb, last_halo), c))

    row = pl.BlockSpec((tr, BRANCH), lambda i: (i, 0))
    return pl.pallas_call(
        body, name=name, grid=(nblk,),
        out_shape=(jax.ShapeDtypeStruct((t, BRANCH), BF16),) * 3 + (jax.ShapeDtypeStruct((HALO, BRANCH), F32),),
        in_specs=[col(0), col(1), col(2), prev(1), prev(2), row, nxt(0), nxt(c0),
                  pl.BlockSpec((3, BRANCH), lambda i: (0, 0))],
        out_specs=(row, row, row, pl.BlockSpec((HALO, BRANCH), lambda i: (0, 0))),
        compiler_params=_params(("arbitrary",), 8 * _nbytes((tr, BRANCH), F32)),
    )(proj, proj, proj, proj, proj, dout, dout, proj, conv_w)


def _tri(lower):
    r = lax.broadcasted_iota(jnp.int32, (LANE, LANE), 0)
    c = lax.broadcasted_iota(jnp.int32, (LANE, LANE), 1)
    return jnp.where((c <= r) if lower else (c >= r), 1.0, 0.0).astype(F32)


def _logf_cumsum(proj, fbias_row, name):
    t = proj.shape[0]
    nchunk = t // LANE

    def body(f_ref, b_ref, c_ref, run_sc):
        tri = _tri(True)
        run_sc[...] = jnp.zeros_like(run_sc)

        @pl.loop(0, nchunk)
        def _(i):
            rows = pl.ds(pl.multiple_of(i * LANE, LANE), LANE)
            z = f_ref[rows, :] + b_ref[...]
            logf = jnp.minimum(z, 0.0) - jnp.log(1.0 + jnp.exp(-jnp.abs(z)))
            cs = lax.dot_general(tri, logf, (NN, ((), ())), precision=lax.Precision.HIGHEST,
                                 preferred_element_type=F32) + run_sc[0:1, :]
            c_ref[rows, :] = cs
            run_sc[0:1, :] = cs[LANE - 1:LANE, :]

    return pl.pallas_call(
        body, name=name, grid=(1,),
        out_shape=jax.ShapeDtypeStruct((t, LANE), F32),
        in_specs=[pl.BlockSpec((t, LANE), lambda i: (0, COL_FG // LANE)), pl.BlockSpec((1, LANE), lambda i: (0, 0))],
        out_specs=pl.BlockSpec((t, LANE), lambda i: (0, 0)),
        scratch_shapes=[pltpu.VMEM((8, LANE), F32)],
        compiler_params=_params(("arbitrary",), 2 * _nbytes((t, LANE), F32)),
    )(proj, fbias_row)


def _logf_cumsum_bwd(proj, fbias_row, dc, name):
    t = proj.shape[0]
    nchunk = t // LANE

    def body(f_ref, b_ref, dc_ref, df_ref, db_ref, run_sc):
        tri = _tri(False)
        run_sc[...] = jnp.zeros_like(run_sc)
        db_ref[...] = jnp.zeros_like(db_ref)

        @pl.loop(0, nchunk)
        def _(k):
            rows = pl.ds(pl.multiple_of((nchunk - 1 - k) * LANE, LANE), LANE)
            ss = lax.dot_general(tri, dc_ref[rows, :], (NN, ((), ())), precision=lax.Precision.HIGHEST,
                                 preferred_element_type=F32) + run_sc[0:1, :]
            z = f_ref[rows, :] + b_ref[...]
            dz = ss * (1.0 / (1.0 + jnp.exp(z)))
            df_ref[rows, :] = dz.astype(BF16)
            run_sc[0:1, :] = ss[0:1, :]
            db_ref[...] += jnp.sum(dz, axis=0, keepdims=True)

    return pl.pallas_call(
        body, name=name, grid=(1,),
        out_shape=(jax.ShapeDtypeStruct((t, LANE), BF16), jax.ShapeDtypeStruct((1, LANE), F32)),
        in_specs=[pl.BlockSpec((t, LANE), lambda i: (0, COL_FG // LANE)), pl.BlockSpec((1, LANE), lambda i: (0, 0)),
                  pl.BlockSpec((t, LANE), lambda i: (0, 0))],
        out_specs=(pl.BlockSpec((t, LANE), lambda i: (0, 0)), pl.BlockSpec((1, LANE), lambda i: (0, 0))),
        scratch_shapes=[pltpu.VMEM((8, LANE), F32)],
        compiler_params=_params(("arbitrary",), 3 * _nbytes((t, LANE), F32)),
    )(proj, fbias_row, dc)


def _lo_mask():
    return lax.broadcasted_iota(jnp.int32, (1, LANE), 1) < HEAD


def _fox_fwd(proj, c_col, c_row, name):
    t = proj.shape[0]
    tq = _pick(t, FOX_TILE, LANE)
    nq = t // tq
    rep = tq // LANE
    scale = HEAD ** -0.5
    cq, ck, cv = COL_FOX // LANE, COL_FOX // LANE + 4, COL_FOX // LANE + 8

    def body(q_ref, k_ref, v_ref, cc_ref, cr_ref, y_ref, lse_ref, m_sc, l_sc, acc_sc):
        iq, ik = pl.program_id(1), pl.program_id(2)
        lo = _lo_mask()

        @pl.when(ik == 0)
        def _():
            m_sc[...] = jnp.full(m_sc.shape, NEG, F32)
            l_sc[...] = jnp.zeros_like(l_sc)
            acc_sc[...] = jnp.zeros_like(acc_sc)

        def step(diag):
            q2 = q_ref[...].astype(BF16)
            k2 = k_ref[...].astype(BF16)
            v2 = v_ref[...].astype(BF16)
            for h in range(2):
                msk = lo if h == 0 else jnp.logical_not(lo)
                qh = jnp.where(msk, q2, jnp.zeros_like(q2))
                s = _dot(qh, k2, NT) * scale
                s = s + jnp.tile(cc_ref[h], (1, rep)) - cr_ref[h]
                if diag:
                    row = lax.broadcasted_iota(jnp.int32, (tq, tq), 0)
                    col = lax.broadcasted_iota(jnp.int32, (tq, tq), 1)
                    s = jnp.where(col <= row, s, NEG)
                m_prev = m_sc[h]
                m_new = jnp.maximum(m_prev, jnp.max(s, axis=1, keepdims=True))
                alpha = jnp.exp(m_prev - m_new)
                p = jnp.exp(s - jnp.tile(m_new, (1, rep)))
                l_sc[h] = alpha * l_sc[h] + jnp.sum(p, axis=1, keepdims=True)
                acc_sc[h] = alpha * acc_sc[h] + _dot(p.astype(BF16), v2, NN)
                m_sc[h] = m_new

        @pl.when(ik < iq)
        def _():
            step(False)

        @pl.when(ik == iq)
        def _():
            step(True)

        @pl.when(ik == nq - 1)
        def _():
            y = jnp.where(lo, acc_sc[0] / l_sc[0], acc_sc[1] / l_sc[1])
            y_ref[...] = y.astype(BF16)
            lse_ref[...] = m_sc[...] + jnp.log(l_sc[...])

    def kv(c):
        return pl.BlockSpec((tq, LANE), lambda j, iq, ik, c=c: (jnp.minimum(ik, iq), c + j))

    return pl.pallas_call(
        body, name=name, grid=(4, nq, nq),
        out_shape=(jax.ShapeDtypeStruct((t, BRANCH), BF16), jax.ShapeDtypeStruct((8, t, LANE), F32)),
        in_specs=[pl.BlockSpec((tq, LANE), lambda j, iq, ik: (iq, cq + j)), kv(ck), kv(cv),
                  pl.BlockSpec((2, tq, LANE), lambda j, iq, ik: (j, iq, 0)),
                  pl.BlockSpec((2, 1, tq), lambda j, iq, ik: (j, 0, jnp.minimum(ik, iq)))],
        out_specs=(pl.BlockSpec((tq, LANE), lambda j, iq, ik: (iq, j)),
                   pl.BlockSpec((2, tq, LANE), lambda j, iq, ik: (j, iq, 0))),
        scratch_shapes=[pltpu.VMEM((2, tq, LANE), F32)] * 3,
        compiler_params=_params(("parallel", "parallel", "arbitrary"),
                                16 * _nbytes((tq, LANE), F32) + 6 * _nbytes((tq, tq), F32)),
    )(proj, proj, proj, c_col, c_row)


def _fox_dq(proj, c_col, c_row, lse, y, dy, name):
    t = proj.shape[0]
    tq = _pick(t, FOX_TILE, LANE)
    nq = t // tq
    rep = tq // LANE
    scale = HEAD ** -0.5
    cq, ck, cv = COL_FOX // LANE, COL_FOX // LANE + 4, COL_FOX // LANE + 8

    def body(q_ref, k_ref, v_ref, cc_ref, cr_ref, lse_ref, y_ref, dy_ref, dq_ref, d_ref, dcq_ref, dq_sc):
        iq, ik = pl.program_id(1), pl.program_id(2)
        lo = _lo_mask()

        @pl.when(ik == 0)
        def _():
            dq_sc[...] = jnp.zeros_like(dq_sc)
            dcq_ref[...] = jnp.zeros_like(dcq_ref)
            prod = y_ref[...].astype(F32) * dy_ref[...].astype(F32)
            d_ref[0] = jnp.broadcast_to(jnp.sum(jnp.where(lo, prod, 0.0), axis=1, keepdims=True), (tq, LANE))
            d_ref[1] = jnp.broadcast_to(jnp.sum(jnp.where(lo, 0.0, prod), axis=1, keepdims=True), (tq, LANE))

        def step(diag):
            q2 = q_ref[...].astype(BF16)
            k2 = k_ref[...].astype(BF16)
            v2 = v_ref[...].astype(BF16)
            do2 = dy_ref[...]
            for h in range(2):
                msk = lo if h == 0 else jnp.logical_not(lo)
                qh = jnp.where(msk, q2, jnp.zeros_like(q2))
                doh = jnp.where(msk, do2, jnp.zeros_like(do2))
                s = _dot(qh, k2, NT) * scale
                s = s + jnp.tile(cc_ref[h], (1, rep)) - cr_ref[h]
                if diag:
                    row = lax.broadcasted_iota(jnp.int32, (tq, tq), 0)
                    col = lax.broadcasted_iota(jnp.int32, (tq, tq), 1)
                    s = jnp.where(col <= row, s, NEG)
                p = jnp.exp(s - jnp.tile(lse_ref[h], (1, rep)))
                dp = _dot(doh, v2, NT)
                ds = p * (dp - jnp.tile(d_ref[h], (1, rep)))
                dq_sc[h] += _dot(ds.astype(BF16), k2, NN)
                dcq_ref[h] += jnp.sum(ds, axis=1, keepdims=True)

        @pl.when(ik < iq)
        def _():
            step(False)

        @pl.when(ik == iq)
        def _():
            step(True)

        @pl.when(ik == nq - 1)
        def _():
            dq_ref[...] = (jnp.where(lo, dq_sc[0], dq_sc[1]) * scale).astype(BF16)

    def kv(c):
        return pl.BlockSpec((tq, LANE), lambda j, iq, ik, c=c: (jnp.minimum(ik, iq), c + j))

    stat = pl.BlockSpec((2, tq, LANE), lambda j, iq, ik: (j, iq, 0))
    pair = pl.BlockSpec((tq, LANE), lambda j, iq, ik: (iq, j))
    return pl.pallas_call(
        body, name=name, grid=(4, nq, nq),
        out_shape=(jax.ShapeDtypeStruct((t, BRANCH), BF16), jax.ShapeDtypeStruct((8, t, LANE), F32),
                   jax.ShapeDtypeStruct((8, t, LANE), F32)),
        in_specs=[pl.BlockSpec((tq, LANE), lambda j, iq, ik: (iq, cq + j)), kv(ck), kv(cv), stat,
                  pl.BlockSpec((2, 1, tq), lambda j, iq, ik: (j, 0, jnp.minimum(ik, iq))), stat, pair, pair],
        out_specs=(pair, stat, stat),
        scratch_shapes=[pltpu.VMEM((2, tq, LANE), F32)],
        compiler_params=_params(("parallel", "parallel", "arbitrary"),
                                20 * _nbytes((tq, LANE), F32) + 8 * _nbytes((tq, tq), F32)),
    )(proj, proj, proj, c_col, c_row, lse, y, dy)


def _fox_dkv(proj, c_col, c_row, lse_row, d_row, dy, name):
    t = proj.shape[0]
    tk = _pick(t, FOX_TILE, LANE)
    nk = t // tk
    rep = tk // LANE
    scale = HEAD ** -0.5
    cq, ck, cv = COL_FOX // LANE, COL_FOX // LANE + 4, COL_FOX // LANE + 8

    def body(k_ref, v_ref, q_ref, dy_ref, ck_ref, cqr_ref, lser_ref, dr_ref,
             dk_ref, dv_ref, dc_ref, dk_sc, dv_sc, dc_sc):
        ik, iq = pl.program_id(1), pl.program_id(2)
        lo = _lo_mask()

        @pl.when(iq == 0)
        def _():
            dk_sc[...] = jnp.zeros_like(dk_sc)
            dv_sc[...] = jnp.zeros_like(dv_sc)
            dc_sc[...] = jnp.zeros_like(dc_sc)

        def step(diag):
            k2 = k_ref[...].astype(BF16)
            v2 = v_ref[...].astype(BF16)
            q2 = q_ref[...].astype(BF16)
            do2 = dy_ref[...]
            for h in range(2):
                msk = lo if h == 0 else jnp.logical_not(lo)
                kh = jnp.where(msk, k2, jnp.zeros_like(k2))
                vh = jnp.where(msk, v2, jnp.zeros_like(v2))
                st = _dot(kh, q2, NT) * scale
                st = st + cqr_ref[h] - jnp.tile(ck_ref[h], (1, rep))
                if diag:
                    krow = lax.broadcasted_iota(jnp.int32, (tk, tk), 0)
                    qcol = lax.broadcasted_iota(jnp.int32, (tk, tk), 1)
                    st = jnp.where(krow <= qcol, st, NEG)
                pt = jnp.exp(st - lser_ref[h])
                dpt = _dot(vh, do2, NT)
                dst = pt * (dpt - dr_ref[h])
                dv_sc[h] += _dot(pt.astype(BF16), do2, NN)
                dk_sc[h] += _dot(dst.astype(BF16), q2, NN)
                dc_sc[h] -= jnp.sum(dst, axis=1, keepdims=True)

        @pl.when(iq > ik)
        def _():
            step(False)

        @pl.when(iq == ik)
        def _():
            step(True)

        @pl.when(iq == nk - 1)
        def _():
            dk_ref[...] = (jnp.where(lo, dk_sc[0], dk_sc[1]) * scale).astype(BF16)
            dv_ref[...] = jnp.where(lo, dv_sc[0], dv_sc[1]).astype(BF16)
            dc_ref[...] = dc_sc[...]

    def kcol(c):
        return pl.BlockSpec((tk, LANE), lambda j, ik, iq, c=c: (ik, c + j))

    qrow = pl.BlockSpec((2, 1, tk), lambda j, ik, iq: (j, 0, jnp.maximum(iq, ik)))
    pair_k = pl.BlockSpec((tk, LANE), lambda j, ik, iq: (ik, j))
    return pl.pallas_call(
        body, name=name, grid=(4, nk, nk),
        out_shape=(jax.ShapeDtypeStruct((t, BRANCH), BF16), jax.ShapeDtypeStruct((t, BRANCH), BF16),
                   jax.ShapeDtypeStruct((8, t, LANE), F32)),
        in_specs=[kcol(ck), kcol(cv),
                  pl.BlockSpec((tk, LANE), lambda j, ik, iq: (jnp.maximum(iq, ik), cq + j)),
                  pl.BlockSpec((tk, LANE), lambda j, ik, iq: (jnp.maximum(iq, ik), j)),
                  pl.BlockSpec((2, tk, LANE), lambda j, ik, iq: (j, ik, 0)), qrow, qrow, qrow],
        out_specs=(pair_k, pair_k, pl.BlockSpec((2, tk, LANE), lambda j, ik, iq: (j, ik, 0))),
        scratch_shapes=[pltpu.VMEM((2, tk, LANE), F32)] * 3,
        compiler_params=_params(("parallel", "parallel", "arbitrary"),
                                24 * _nbytes((tk, LANE), F32) + 8 * _nbytes((tk, tk), F32)),
    )(proj, proj, proj, dy, c_col, c_row, lse_row, d_row)


def _swa_tables(rel_bias):
    tq = np.arange(SWA_BLOCK)[:, None]
    sk = np.arange(2 * SWA_BLOCK)[None, :]
    dist = SWA_BLOCK + tq - sk
    inwin = (dist >= 0) & (dist < SWA_BLOCK)
    n = np.maximum(dist, 0)
    max_exact = N_BUCKETS // 2
    large = max_exact + (np.log(np.maximum(n, 1).astype(np.float32) / max_exact)
                         / math.log(SWA_BLOCK / max_exact) * (N_BUCKETS - max_exact)).astype(np.int32)
    bucket = np.where(n < max_exact, n, np.minimum(large, N_BUCKETS - 1))
    onehot = (bucket[..., None] == np.arange(N_BUCKETS)) & inwin[..., None]
    onehot = jnp.asarray(onehot.astype(np.float32))
    bias = jnp.einsum("tsb,bh->hts", onehot, rel_bias, precision=lax.Precision.HIGHEST)
    bias = jnp.where(jnp.asarray(inwin)[None], bias, NEG)
    return onehot, bias


def _swa_fwd(proj, bias, sink_rep, name):
    t = proj.shape[0]
    nb = t // SWA_BLOCK
    scale = HEAD ** -0.5
    csq, csk, csv = COL_SQ // 256, COL_SK // LANE, COL_SV // LANE

    def body(q_ref, kp_ref, kc_ref, vp_ref, vc_ref, b_ref, sk_ref, y_ref, lse_ref):
        kvh, n = pl.program_id(0), pl.program_id(1)
        lane = lax.broadcasted_iota(jnp.int32, (1, LANE), 1)
        lo = lane < HEAD
        kvm = jnp.logical_and(lane >= kvh * HEAD, lane < (kvh + 1) * HEAD)

        def both(prev_ref, cur_ref):
            band = jnp.concatenate([prev_ref[...], cur_ref[...]], axis=0)
            band = jnp.where(kvm, band, 0.0)
            return (band + pltpu.roll(band, HEAD, 1)).astype(BF16)

        kb, vb = both(kp_ref, kc_ref), both(vp_ref, vc_ref)
        col = lax.broadcasted_iota(jnp.int32, (SWA_BLOCK, 2 * SWA_BLOCK), 1)
        first = jnp.logical_and(n == 0, col < SWA_BLOCK)
        outs = []
        for g in range(SWA_GROUP):
            half = q_ref[:, (g // 2) * LANE:(g // 2 + 1) * LANE]
            hm = lo if g % 2 == 0 else jnp.logical_not(lo)
            qg = jnp.where(hm, half, 0.0).astype(BF16)
            s = _dot(qg, kb, NT) * scale + b_ref[g]
            s = jnp.where(first, NEG, s)
            snk = sk_ref[g:g + 1, :]
            m = jnp.maximum(jnp.max(s, axis=1, keepdims=True), snk)
            p = jnp.exp(s - jnp.tile(m, (1, 2)))
            denom = jnp.sum(p, axis=1, keepdims=True) + jnp.exp(snk - m)
            outs.append(_dot(p.astype(BF16), vb, NN) / denom)
            lse_ref[g] = m + jnp.log(denom)
        y_ref[:, 0:LANE] = jnp.where(lo, outs[0], outs[1]).astype(BF16)
        y_ref[:, LANE:2 * LANE] = jnp.where(lo, outs[2], outs[3]).astype(BF16)

    def blk(c, shift):
        return pl.BlockSpec((SWA_BLOCK, LANE), lambda kvh, n, c=c, s=shift: (jnp.maximum(n - s, 0), c))

    return pl.pallas_call(
        body, name=name, grid=(2, nb),
        out_shape=(jax.ShapeDtypeStruct((t, BRANCH), BF16), jax.ShapeDtypeStruct((8, t, LANE), F32)),
        in_specs=[pl.BlockSpec((SWA_BLOCK, 256), lambda kvh, n: (n, csq + kvh)),
                  blk(csk, 1), blk(csk, 0), blk(csv, 1), blk(csv, 0),
                  pl.BlockSpec((None, SWA_GROUP, SWA_BLOCK, 256), lambda kvh, n: (kvh, 0, 0, 0)),
                  pl.BlockSpec((None, SWA_GROUP, LANE), lambda kvh, n: (kvh, 0, 0))],
        out_specs=(pl.BlockSpec((SWA_BLOCK, 256), lambda kvh, n: (n, kvh)),
                   pl.BlockSpec((SWA_GROUP, SWA_BLOCK, LANE), lambda kvh, n: (kvh, n, 0))),
        compiler_params=_params(("parallel", "arbitrary"), 4 << 20),
    )(proj, proj, proj, proj, proj, bias.reshape(2, SWA_GROUP, SWA_BLOCK, 256), sink_rep)


def _swa_bwd(proj, bias, sink_rep, lse, y, dy, name):
    t = proj.shape[0]
    nb = t // SWA_BLOCK
    scale = HEAD ** -0.5
    csq, csk, csv = COL_SQ // 256, COL_SK // LANE, COL_SV // LANE

    def body(q_ref, kp_ref, kc_ref, vp_ref, vc_ref, b_ref, sk_ref, lse_ref, y_ref, dy_ref,
             dq_ref, dkp_ref, dvp_ref, db_ref, dsk_ref):
        kvh, n = pl.program_id(0), pl.program_id(1)
        lane = lax.broadcasted_iota(jnp.int32, (1, LANE), 1)
        lo = lane < HEAD
        kvm = jnp.logical_and(lane >= kvh * HEAD, lane < (kvh + 1) * HEAD)

        def both(prev_ref, cur_ref):
            band = jnp.concatenate([prev_ref[...], cur_ref[...]], axis=0)
            band = jnp.where(kvm, band, 0.0)
            return (band + pltpu.roll(band, HEAD, 1)).astype(BF16)

        kb, vb = both(kp_ref, kc_ref), both(vp_ref, vc_ref)
        col = lax.broadcasted_iota(jnp.int32, (SWA_BLOCK, 2 * SWA_BLOCK), 1)
        first = jnp.logical_and(n == 0, col < SWA_BLOCK)

        @pl.when(n == 0)
        def _():
            db_ref[...] = jnp.zeros_like(db_ref)
            dsk_ref[...] = jnp.zeros_like(dsk_ref)

        dk_full = jnp.zeros((2 * SWA_BLOCK, LANE), F32)
        dv_full = jnp.zeros((2 * SWA_BLOCK, LANE), F32)
        dqs = []
        for g in range(SWA_GROUP):
            sl = slice((g // 2) * LANE, (g // 2 + 1) * LANE)
            hm = lo if g % 2 == 0 else jnp.logical_not(lo)
            qg = jnp.where(hm, q_ref[:, sl], 0.0).astype(BF16)
            dog = jnp.where(hm, dy_ref[:, sl], jnp.zeros((SWA_BLOCK, LANE), BF16))
            dmat = jnp.where(hm, y_ref[:, sl].astype(F32) * dy_ref[:, sl].astype(F32), 0.0)
            dg = jnp.sum(dmat, axis=1, keepdims=True)
            s = _dot(qg, kb, NT) * scale + b_ref[g]
            s = jnp.where(first, NEG, s)
            lse_g = lse_ref[g]
            p = jnp.exp(s - jnp.tile(lse_g, (1, 2)))
            dp = _dot(dog, vb, NT)
            ds = p * (dp - dg)
            dsb = ds.astype(BF16)
            dqs.append(_dot(dsb, kb, NN) * scale)
            dk_full = dk_full + _dot(dsb, qg, TN)
            dv_full = dv_full + _dot(p.astype(BF16), dog, TN)
            db_ref[g] += ds
            psink = jnp.exp(sk_ref[g:g + 1, :] - lse_g)
            dsk_ref[g:g + 1, :] -= jnp.sum(psink * dg, axis=0, keepdims=True)
        dq_ref[:, 0:LANE] = jnp.where(lo, dqs[0], dqs[1]).astype(BF16)
        dq_ref[:, LANE:2 * LANE] = jnp.where(lo, dqs[2], dqs[3]).astype(BF16)
        dkp_ref[...] = jnp.where(kvm, (dk_full + pltpu.roll(dk_full, HEAD, 1)) * scale, 0.0)
        dvp_ref[...] = jnp.where(kvm, dv_full + pltpu.roll(dv_full, HEAD, 1), 0.0)

    def blk(c, shift):
        return pl.BlockSpec((SWA_BLOCK, LANE), lambda kvh, n, c=c, s=shift: (jnp.maximum(n - s, 0), c))

    qblk = pl.BlockSpec((SWA_BLOCK, 256), lambda kvh, n: (n, kvh))
    part = pl.BlockSpec((None, None, 2 * SWA_BLOCK, LANE), lambda kvh, n: (kvh, n, 0, 0))
    bspec = pl.BlockSpec((None, SWA_GROUP, SWA_BLOCK, 256), lambda kvh, n: (kvh, 0, 0, 0))
    sspec = pl.BlockSpec((None, SWA_GROUP, LANE), lambda kvh, n: (kvh, 0, 0))
    return pl.pallas_call(
        body, name=name, grid=(2, nb),
        out_shape=(jax.ShapeDtypeStruct((t, BRANCH), BF16),
                   jax.ShapeDtypeStruct((2, nb, 2 * SWA_BLOCK, LANE), F32),
                   jax.ShapeDtypeStruct((2, nb, 2 * SWA_BLOCK, LANE), F32),
                   jax.ShapeDtypeStruct((2, SWA_GROUP, SWA_BLOCK, 256), F32),
                   jax.ShapeDtypeStruct((2, SWA_GROUP, LANE), F32)),
        in_specs=[pl.BlockSpec((SWA_BLOCK, 256), lambda kvh, n: (n, csq + kvh)),
                  blk(csk, 1), blk(csk, 0), blk(csv, 1), blk(csv, 0), bspec, sspec,
                  pl.BlockSpec((SWA_GROUP, SWA_BLOCK, LANE), lambda kvh, n: (kvh, n, 0)), qblk, qblk],
        out_specs=(qblk, part, part, bspec, sspec),
        compiler_params=_params(("parallel", "arbitrary"), 6 << 20),
    )(proj, proj, proj, proj, proj, bias.reshape(2, SWA_GROUP, SWA_BLOCK, 256), sink_rep, lse, y, dy)


def _gate_fwd(proj, pb, name):
    t = proj.shape[0]
    tr = _pick(t, ROW_TILE // 2, BF16_SUBLANE)

    def body(g0, g1, g2, p0, p1, p2, o_ref):
        acc = jax.nn.sigmoid(g0[...]) * p0[...]
        acc = acc + jax.nn.sigmoid(g1[...]) * p1[...]
        acc = acc + jax.nn.sigmoid(g2[...]) * p2[...]
        o_ref[...] = acc.astype(BF16)

    row = pl.BlockSpec((tr, D_MODEL), lambda i: (i, 0))
    gates = [pl.BlockSpec((tr, D_MODEL), lambda i, b=b: (i, b)) for b in range(3)]
    return pl.pallas_call(
        body, name=name, grid=(t // tr,),
        out_shape=jax.ShapeDtypeStruct((t, D_MODEL), BF16),
        in_specs=gates + [row] * 3, out_specs=row,
        compiler_params=_params(("parallel",), 7 * _nbytes((tr, D_MODEL), F32)),
    )(proj, proj, proj, *pb)


def _gate_bwd(proj, pb, dmerged, name):
    t = proj.shape[0]
    tr = _pick(t, ROW_TILE // 2, BF16_SUBLANE)

    def body(g0, g1, g2, p0, p1, p2, dm_ref, dp0, dp1, dp2, dg_ref):
        dm = dm_ref[...]
        for b, (g_ref, p_ref, dp_ref) in enumerate(((g0, p0, dp0), (g1, p1, dp1), (g2, p2, dp2))):
            sg = jax.nn.sigmoid(g_ref[...])
            dp_ref[...] = (dm * sg).astype(BF16)
            dg_ref[:, b * D_MODEL:(b + 1) * D_MODEL] = (dm * p_ref[...] * sg * (1.0 - sg)).astype(BF16)

    row = pl.BlockSpec((tr, D_MODEL), lambda i: (i, 0))
    gates = [pl.BlockSpec((tr, D_MODEL), lambda i, b=b: (i, b)) for b in range(3)]
    return pl.pallas_call(
        body, name=name, grid=(t // tr,),
        out_shape=(jax.ShapeDtypeStruct((t, D_MODEL), BF16),) * 3 + (jax.ShapeDtypeStruct((t, 3 * D_MODEL), BF16),),
        in_specs=gates + [row] * 4,
        out_specs=(row, row, row, pl.BlockSpec((tr, 3 * D_MODEL), lambda i: (i, 0))),
        compiler_params=_params(("parallel",), 11 * _nbytes((tr, D_MODEL), F32)),
    )(proj, proj, proj, *pb, dmerged)


def _swiglu_fwd(ab, name):
    t = ab.shape[0]
    tr = _pick(t, ROW_TILE, BF16_SUBLANE)
    tc = D_FF // 2

    def body(a_ref, b_ref, o_ref):
        a = a_ref[...]
        o_ref[...] = (a * jax.nn.sigmoid(a) * b_ref[...]).astype(BF16)

    return pl.pallas_call(
        body, name=name, grid=(t // tr, 2),
        out_shape=jax.ShapeDtypeStruct((t, D_FF), BF16),
        in_specs=[pl.BlockSpec((tr, tc), lambda i, j: (i, j)), pl.BlockSpec((tr, tc), lambda i, j: (i, j + 2))],
        out_specs=pl.BlockSpec((tr, tc), lambda i, j: (i, j)),
        compiler_params=_params(("parallel", "parallel"), 3 * _nbytes((tr, tc), F32)),
    )(ab, ab)


def _swiglu_bwd(ab, dh, name):
    t = ab.shape[0]
    tr = _pick(t, ROW_TILE, BF16_SUBLANE)
    tc = D_FF // 2

    def body(a_ref, b_ref, dh_ref, o_ref):
        jj = pl.program_id(1)
        a, b, d = a_ref[...], b_ref[...], dh_ref[...]
        sg = jax.nn.sigmoid(a)
        da = d * b * (sg * (1.0 + a * (1.0 - sg)))
        db = d * (a * sg)
        o_ref[...] = jnp.where(jj < 2, da, db).astype(BF16)

    return pl.pallas_call(
        body, name=name, grid=(t // tr, 4),
        out_shape=jax.ShapeDtypeStruct((t, 2 * D_FF), BF16),
        in_specs=[pl.BlockSpec((tr, tc), lambda i, j: (i, j % 2)),
                  pl.BlockSpec((tr, tc), lambda i, j: (i, j % 2 + 2)),
                  pl.BlockSpec((tr, tc), lambda i, j: (i, j % 2))],
        out_specs=pl.BlockSpec((tr, tc), lambda i, j: (i, j)),
        compiler_params=_params(("parallel", "parallel"), 4 * _nbytes((tr, tc), F32)),
    )(ab, ab, dh)


def _xattn_fwd(q, kv, name):
    t = q.shape[0]
    tq = _pick(t, ROW_TILE, BF16_SUBLANE)
    mlen = kv.shape[0]
    scale = X_HEAD ** -0.5

    def body(q_ref, kv_ref, o_ref):
        for h in range(X_HEADS):
            sl = slice(h * X_HEAD, (h + 1) * X_HEAD)
            kh = kv_ref[:, sl]
            vh = kv_ref[:, D_MODEL + h * X_HEAD:D_MODEL + (h + 1) * X_HEAD]
            s = _dot(q_ref[:, sl], kh, NT) * scale
            p = jnp.exp(s - jnp.max(s, axis=1, keepdims=True))
            l = jnp.sum(p, axis=1, keepdims=True)
            o_ref[:, sl] = (_dot(p.astype(BF16), vh, NN) / l).astype(BF16)

    return pl.pallas_call(
        body, name=name, grid=(t // tq,),
        out_shape=jax.ShapeDtypeStruct((t, D_MODEL), BF16),
        in_specs=[pl.BlockSpec((tq, D_MODEL), lambda i: (i, 0)), pl.BlockSpec((mlen, 2 * D_MODEL), lambda i: (0, 0))],
        out_specs=pl.BlockSpec((tq, D_MODEL), lambda i: (i, 0)),
        compiler_params=_params(("parallel",), 4 * _nbytes((tq, D_MODEL), F32)),
    )(q, kv)


def _xattn_bwd(q, kv, do, name):
    t = q.shape[0]
    tq = _pick(t, ROW_TILE, BF16_SUBLANE)
    mlen = kv.shape[0]
    scale = X_HEAD ** -0.5

    def body(q_ref, kv_ref, do_ref, dq_ref, dkv_ref):
        i = pl.program_id(0)

        @pl.when(i == 0)
        def _():
            dkv_ref[...] = jnp.zeros_like(dkv_ref)

        for h in range(X_HEADS):
            sl = slice(h * X_HEAD, (h + 1) * X_HEAD)
            vsl = slice(D_MODEL + h * X_HEAD, D_MODEL + (h + 1) * X_HEAD)
            qh, kh, vh, doh = q_ref[:, sl], kv_ref[:, sl], kv_ref[:, vsl], do_ref[:, sl]
            s = _dot(qh, kh, NT) * scale
            p = jnp.exp(s - jnp.max(s, axis=1, keepdims=True))
            p = p / jnp.sum(p, axis=1, keepdims=True)
            dp = _dot(doh, vh, NT)
            ds = p * (dp - jnp.sum(p * dp, axis=1, keepdims=True))
            dsb = ds.astype(BF16)
            dq_ref[:, sl] = (_dot(dsb, kh, NN) * scale).astype(BF16)
            dkv_ref[:, sl] += _dot(dsb, qh, TN) * scale
            dkv_ref[:, vsl] += _dot(p.astype(BF16), doh, TN)

    row = pl.BlockSpec((tq, D_MODEL), lambda i: (i, 0))
    whole = pl.BlockSpec((mlen, 2 * D_MODEL), lambda i: (0, 0))
    return pl.pallas_call(
        body, name=name, grid=(t // tq,),
        out_shape=(jax.ShapeDtypeStruct((t, D_MODEL), BF16), jax.ShapeDtypeStruct((mlen, 2 * D_MODEL), F32)),
        in_specs=[row, whole, row], out_specs=(row, whole),
        compiler_params=_params(("arbitrary",), 6 * _nbytes((tq, D_MODEL), F32)),
    )(q, kv, do)


def _position():
    return lax.axis_index("x"), lax.axis_index("y"), lax.axis_index("c")


N_PEER = N_DEV - 1


def _all_gather(xs, name):
    n = len(xs)

    def body(*refs):
        x_refs, out_refs = refs[:n], refs[n:2 * n]
        send_sems, recv_sems, local_sems = refs[2 * n:]
        mx, my, mc = _position()
        me, sib = (mx, my, mc), (mx, my, 1 - mc)
        chips = [(1 - mx, my), (mx, 1 - my), (1 - mx, 1 - my)]

        def slot(i, p):
            return out_refs[i].at[4 * p[0] + 2 * p[1] + p[2]]

        def copy(i, k, block, to, src=None):
            return pltpu.make_async_remote_copy(
                src_ref=slot(i, block) if src is None else src, dst_ref=slot(i, block),
                send_sem=send_sems.at[i * N_PEER + k], recv_sem=recv_sems.at[i * N_PEER + k],
                device_id=to, device_id_type=MESH)

        mine = [pltpu.make_async_copy(x_refs[i], slot(i, me), local_sems.at[i]) for i in range(n)]
        for cp in mine:
            cp.start()
        first = [copy(i, 1 + j, me, (*chip, mc), src=x_refs[i]) for j, chip in enumerate(chips) for i in range(n)]
        first += [copy(i, 0, me, sib, src=x_refs[i]) for i in range(n)]
        for cp in first:
            cp.start()
        passed = []
        for j, chip in enumerate(chips):
            for i in range(n):
                copy(i, 1 + j, (*chip, mc), me).wait_recv()
                passed.append(copy(i, 4 + j, (*chip, mc), sib))
                passed[-1].start()
        for i in range(n):
            copy(i, 0, sib, me).wait_recv()
        for j, chip in enumerate(chips):
            for i in range(n):
                copy(i, 4 + j, (*chip, 1 - mc), me).wait_recv()
        for cp in first + passed:
            cp.wait_send()
        for cp in mine:
            cp.wait()

    return pl.pallas_call(
        body, name=name,
        out_shape=tuple(jax.ShapeDtypeStruct((N_DEV,) + x.shape, x.dtype) for x in xs),
        in_specs=[pl.BlockSpec(memory_space=pl.ANY)] * n, out_specs=(pl.BlockSpec(memory_space=pl.ANY),) * n,
        scratch_shapes=[pltpu.SemaphoreType.DMA((n * N_PEER,)), pltpu.SemaphoreType.DMA((n * N_PEER,)),
                        pltpu.SemaphoreType.DMA((n,))],
    )(*xs)


def _exchange(parts, name):
    n = len(parts)
    rels = [(dx, dy, dc) for dx in (0, 1) for dy in (0, 1) for dc in (0, 1)][1:]

    def body(*refs):
        g_refs, out_refs = refs[:n], refs[n:2 * n]
        send_sems, recv_sems, local_sems = refs[2 * n:]
        mx, my, mc = _position()
        me_idx = 4 * mx + 2 * my + mc

        def peer(rel):
            return tuple((1 - v) if f else v for f, v in zip(rel, (mx, my, mc)))

        def copy(i, k, p, arriving):
            p_idx = 4 * p[0] + 2 * p[1] + p[2]
            src, dst = (me_idx, p_idx) if arriving else (p_idx, me_idx)
            return pltpu.make_async_remote_copy(
                src_ref=g_refs[i].at[src], dst_ref=out_refs[i].at[dst],
                send_sem=send_sems.at[i * N_PEER + k], recv_sem=recv_sems.at[i * N_PEER + k],
                device_id=p, device_id_type=MESH)

        mine = [pltpu.make_async_copy(g_refs[i].at[me_idx], out_refs[i].at[me_idx], local_sems.at[i]) for i in range(n)]
        for cp in mine:
            cp.start()
        sends = [copy(i, k, peer(rel), False) for i in range(n) for k, rel in enumerate(rels)]
        for cp in sends:
            cp.start()
        for i in range(n):
            for k, rel in enumerate(rels):
                copy(i, k, peer(rel), True).wait_recv()
        for cp in sends:
            cp.wait_send()
        for cp in mine:
            cp.wait()

    return pl.pallas_call(
        body, name=name,
        out_shape=tuple(jax.ShapeDtypeStruct(p.shape, p.dtype) for p in parts),
        in_specs=[pl.BlockSpec(memory_space=pl.ANY)] * n, out_specs=(pl.BlockSpec(memory_space=pl.ANY),) * n,
        scratch_shapes=[pltpu.SemaphoreType.DMA((n * N_PEER,)), pltpu.SemaphoreType.DMA((n * N_PEER,)),
                        pltpu.SemaphoreType.DMA((n,))],
    )(*parts)


ADAMW_BLOCK_BYTES = 1 << 20


def _adamw(parts, w, m, v, name):
    r, l = w.shape
    tr = _pick(r, max(ADAMW_BLOCK_BYTES // (4 * l), BF16_SUBLANE), BF16_SUBLANE)
    c1 = 1.0 - ADAM_B1 ** ADAM_STEP
    c2 = 1.0 - ADAM_B2 ** ADAM_STEP

    def body(p_ref, w_ref, m_ref, v_ref, g_ref, d_ref, nm_ref, nv_ref):
        g = p_ref[0].astype(F32)
        for s in range(1, N_DEV):
            g = g + p_ref[s].astype(F32)
        nm = ADAM_B1 * m_ref[...] + (1.0 - ADAM_B1) * g
        nv = ADAM_B2 * v_ref[...] + (1.0 - ADAM_B2) * (g * g)
        m_hat = nm / c1
        v_hat = nv / c2
        g_ref[...] = g
        d_ref[...] = -ADAM_LR * (m_hat / (jnp.sqrt(v_hat) + ADAM_EPS) + ADAM_WD * w_ref[...])
        nm_ref[...] = nm
        nv_ref[...] = nv

    row = pl.BlockSpec((tr, l), lambda i: (i, 0))
    return pl.pallas_call(
        body, name=name, grid=(r // tr,),
        out_shape=(jax.ShapeDtypeStruct((r, l), F32),) * 4,
        in_specs=[pl.BlockSpec((N_DEV, tr, l), lambda i: (0, i, 0)), row, row, row],
        out_specs=(row,) * 4,
        compiler_params=_params(("parallel",), 12 * _nbytes((tr, l), F32)),
    )(parts, w, m, v)


MATRIX_WEIGHTS = (("w_in", 2), ("conv_w", 2), ("w_branch", 3), ("w_mix_out", 1), ("w_xq", 1), ("w_xkv", 2),
                  ("w_xo", 1), ("w_ffn_gate", 2), ("w_ffn_up", 2), ("w_ffn_down", 1))
SMALL_PARAMS = ("mix_norm_g", "xattn_norm_g", "mem_norm_g", "ffn_norm_g", "final_norm_g", "forget_bias", "sink",
                "rel_bias")


def _pack_small(pieces):
    flat = jnp.concatenate([p.astype(F32).reshape(-1) for p in pieces])
    total = -(-flat.shape[0] // (8 * LANE)) * (8 * LANE)
    return jnp.pad(flat, (0, total - flat.shape[0])).reshape(total // LANE, LANE)


def _rows(a):
    return a.reshape(-1, a.shape[-1])


def _to_full(gathered, axis):
    moved = jnp.moveaxis(gathered, 0, axis)
    shape = list(moved.shape)
    shape[axis:axis + 2] = [shape[axis] * shape[axis + 1]]
    return moved.reshape(shape)


def _to_blocks(full, axis):
    shape = list(full.shape)
    shape[axis:axis + 1] = [N_DEV, shape[axis] // N_DEV]
    return jnp.moveaxis(full.reshape(shape), axis, 0)


def _perm_in(w_in):
    pad = jnp.zeros((w_in.shape[0], PROJ_COLS - IN_COLS), w_in.dtype)
    return jnp.concatenate([w_in[:, 3848:6920], w_in[:, 0:3072], w_in[:, 3080:3848], w_in[:, 3072:3080], pad], axis=1)


def _unperm_in(dw):
    return jnp.concatenate([dw[:, 3072:6144], dw[:, 6912:6920], dw[:, 6144:6912], dw[:, 0:3072]], axis=1)


def _layer_fwd(l, x, mem, wt, sm):
    t = x.shape[0]
    tag = f"l{l}_"
    h = _rms_fwd(x, sm["mix_norm_g"][l], tag + "mix_norm")
    proj = _matmul(h, wt["w_in"][l], "nn", F32, tag + "in_proj")
    y_conv = _conv_fwd(proj, wt["conv_w"][l], tag + "conv")
    fbias_row = jnp.pad(sm["forget_bias"][l], (0, LANE - 8)).reshape(1, LANE)
    c = _logf_cumsum(proj, fbias_row, tag + "logf_cumsum")
    c8 = c[:, :8].T
    c_col = jnp.broadcast_to(c8[:, :, None], (8, t, LANE))
    c_row = c8.reshape(8, 1, t)
    y_fox, lse_fox = _fox_fwd(proj, c_col, c_row, tag + "fox")
    onehot, bias = _swa_tables(sm["rel_bias"])
    sink_rep = jnp.broadcast_to(sm["sink"][l].reshape(2, SWA_GROUP, 1), (2, SWA_GROUP, LANE))
    y_swa, lse_swa = _swa_fwd(proj, bias, sink_rep, tag + "swa")
    ys = (y_conv, y_fox, y_swa)
    pb = tuple(_matmul(ys[b], wt["w_branch"][l, b], "nn", F32, tag + f"branch{b}") for b in range(3))
    merged = _gate_fwd(proj, pb, tag + "gate")
    x1 = _matmul(merged, wt["w_mix_out"][l], "nn", F32, tag + "mix_out", residual=x)
    xn2 = _rms_fwd(x1, sm["xattn_norm_g"][l], tag + "xattn_norm")
    q = _matmul(xn2, wt["w_xq"][l], "nn", BF16, tag + "xq")
    mem_n = _rms_fwd(mem, sm["mem_norm_g"][l], tag + "mem_norm")
    kv = _matmul(mem_n, wt["w_xkv"][l], "nn", BF16, tag + "xkv")
    o = _xattn_fwd(q, kv, tag + "xattn")
    x2 = _matmul(o, wt["w_xo"][l], "nn", F32, tag + "xo", residual=x1)
    xn3 = _rms_fwd(x2, sm["ffn_norm_g"][l], tag + "ffn_norm")
    ab = _matmul(xn3, wt["w_gu"][l], "nn", F32, tag + "ffn_gu")
    h1 = _swiglu_fwd(ab, tag + "swiglu")
    x3 = _matmul(h1, wt["w_ffn_down"][l], "nn", F32, tag + "ffn_down", residual=x2)
    saved = dict(x=x, h=h, proj=proj, fbias_row=fbias_row, c_col=c_col, c_row=c_row, ys=ys, lse_fox=lse_fox,
                 onehot=onehot, bias=bias, sink_rep=sink_rep, lse_swa=lse_swa, pb=pb, merged=merged, x1=x1,
                 xn2=xn2, q=q, mem_n=mem_n, kv=kv, o=o, x2=x2, xn3=xn3, ab=ab, h1=h1)
    return x3, saved


def _layer_bwd(l, dx3, dx3_b, mem, wt, sm, sv):
    t = dx3.shape[0]
    nb = t // SWA_BLOCK
    tag = f"l{l}_b_"
    gw, gs = {}, {}
    dh1 = _matmul(dx3_b, wt["w_ffn_down"][l], "nt", F32, tag + "d_h1")
    gw["w_ffn_down"] = _matmul(sv["h1"], dx3_b, "tn", F32, tag + "dw_down")
    dab = _swiglu_bwd(sv["ab"], dh1, tag + "swiglu")
    dxn3 = _matmul(dab, wt["w_gu"][l], "nt", F32, tag + "d_xn3")
    dw_gu = _matmul(sv["xn3"], dab, "tn", F32, tag + "dw_gu")
    gw["w_ffn_gate"], gw["w_ffn_up"] = dw_gu[:, :D_FF], dw_gu[:, D_FF:]
    dx2, dx2_b, gs["ffn_norm_g"] = _rms_bwd(sv["x2"], sm["ffn_norm_g"][l], dxn3, dx3, tag + "ffn_norm")
    do = _matmul(dx2_b, wt["w_xo"][l], "nt", BF16, tag + "d_o")
    gw["w_xo"] = _matmul(sv["o"], dx2_b, "tn", F32, tag + "dw_xo")
    dq, dkv = _xattn_bwd(sv["q"], sv["kv"], do, tag + "xattn")
    gw["w_xkv"] = _matmul(sv["mem_n"], dkv, "tn", F32, tag + "dw_xkv")
    dmem_n = _matmul(dkv, wt["w_xkv"][l], "nt", F32, tag + "d_memn")
    _, _, gs["mem_norm_g"] = _rms_bwd(mem, sm["mem_norm_g"][l], dmem_n, None, tag + "mem_norm")
    gw["w_xq"] = _matmul(sv["xn2"], dq, "tn", F32, tag + "dw_xq")
    dxn2 = _matmul(dq, wt["w_xq"][l], "nt", F32, tag + "d_xn2")
    dx1, dx1_b, gs["xattn_norm_g"] = _rms_bwd(sv["x1"], sm["xattn_norm_g"][l], dxn2, dx2, tag + "xattn_norm")
    dmerged = _matmul(dx1_b, wt["w_mix_out"][l], "nt", F32, tag + "d_merged")
    gw["w_mix_out"] = _matmul(sv["merged"], dx1_b, "tn", F32, tag + "dw_mix_out")
    dp0, dp1, dp2, dgate = _gate_bwd(sv["proj"], sv["pb"], dmerged, tag + "gate")
    dps = (dp0, dp1, dp2)
    dy_dtypes = (F32, BF16, BF16)
    dys = [_matmul(dps[b], wt["w_branch"][l, b], "nt", dy_dtypes[b], tag + f"d_y{b}") for b in range(3)]
    gw["w_branch"] = jnp.stack(
        [_matmul(sv["ys"][b], dps[b], "tn", F32, tag + f"dw_branch{b}") for b in range(3)])
    dsq, dkp, dvp, dbias, dsink = _swa_bwd(sv["proj"], sv["bias"], sv["sink_rep"], sv["lse_swa"], sv["ys"][2],
                                           dys[2], tag + "swa")

    def band_add(part):
        tot = part[0] + part[1]
        cur = tot[:, SWA_BLOCK:, :]
        nxt = jnp.concatenate([tot[1:, :SWA_BLOCK, :], jnp.zeros((1, SWA_BLOCK, LANE), F32)], axis=0)
        return (cur + nxt).reshape(t, LANE).astype(BF16)

    dsk, dsv = band_add(dkp), band_add(dvp)
    gs["rel_bias_l"] = jnp.einsum("hts,tsb->bh", dbias.reshape(8, SWA_BLOCK, 2 * SWA_BLOCK), sv["onehot"],
                                  precision=lax.Precision.HIGHEST)
    gs["sink"] = dsink[:, :, 0].reshape(8)
    lse_row = sv["lse_fox"][:, :, 0].reshape(8, 1, t)
    dfq, d_fox, dcq_rep = _fox_dq(sv["proj"], sv["c_col"], sv["c_row"], sv["lse_fox"], sv["ys"][1], dys[1], tag + "fox_dq")
    d_row = d_fox[:, :, 0].reshape(8, 1, t)
    dfk, dfv, dc_rep = _fox_dkv(sv["proj"], sv["c_col"], sv["c_row"], lse_row, d_row, dys[1], tag + "fox_dkv")
    dc = jnp.pad((dcq_rep[:, :, 0] + dc_rep[:, :, 0]).T, ((0, 0), (0, LANE - 8)))
    dfg, dfb = _logf_cumsum_bwd(sv["proj"], sv["fbias_row"], dc, tag + "logf_cumsum")
    gs["forget_bias"] = dfb[0, :8]
    dcb, dcc, dcu, dconv = _conv_bwd(sv["proj"], wt["conv_w"][l], dys[0], tag + "conv")
    gw["conv_w"] = dconv[:3]
    dproj = jnp.concatenate([dgate, dcb, dcc, dcu, dfq, dfk, dfv, dsq, dsk, dsv, dfg], axis=1)
    dh = _matmul(dproj, wt["w_in"][l], "nt", F32, tag + "d_h")
    gw["w_in"] = _unperm_in(_matmul(sv["h"], dproj, "tn", F32, tag + "dw_in"))
    dx, dx_b, gs["mix_norm_g"] = _rms_bwd(sv["x"], sm["mix_norm_g"][l], dh, dx1, tag + "mix_norm")
    return dx, dx_b, gw, gs


def kernel(x, mem, mix_norm_g, w_in, forget_bias, conv_w, sink, w_branch, w_mix_out, rel_bias, xattn_norm_g, mem_norm_g, w_xq, w_xkv, w_xo, ffn_norm_g, w_ffn_gate, w_ffn_up, w_ffn_down, final_norm_g, loss_target, m_mix_norm_g, m_w_in, m_forget_bias, m_conv_w, m_sink, m_w_branch, m_w_mix_out, m_rel_bias, m_xattn_norm_g, m_mem_norm_g, m_w_xq, m_w_xkv, m_w_xo, m_ffn_norm_g, m_w_ffn_gate, m_w_ffn_up, m_w_ffn_down, m_final_norm_g, v_mix_norm_g, v_w_in, v_forget_bias, v_conv_w, v_sink, v_w_branch, v_w_mix_out, v_rel_bias, v_xattn_norm_g, v_mem_norm_g, v_w_xq, v_w_xkv, v_w_xo, v_ffn_norm_g, v_w_ffn_gate, v_w_ffn_up, v_w_ffn_down, v_final_norm_g):
    args = dict(locals())
    names = [n for n, _ in MATRIX_WEIGHTS] + list(SMALL_PARAMS)
    w = {n: args[n] for n in names}
    mo = {n: args["m_" + n] for n in names}
    vo = {n: args["v_" + n] for n in names}
    x2d, mem2d, tgt = x[0], mem[0], loss_target[0]

    wire = {n: (F32 if n == "conv_w" else BF16) for n, _ in MATRIX_WEIGHTS}
    gathered = _all_gather([w[n].astype(wire[n]) for n, _ in MATRIX_WEIGHTS], "weights_all_gather")
    wt = {n: _to_full(g, ax) for (n, ax), g in zip(MATRIX_WEIGHTS, gathered)}
    wt["w_in"] = jnp.stack([_perm_in(wt["w_in"][l]) for l in range(DEPTH)])
    wt["w_gu"] = jnp.concatenate([wt["w_ffn_gate"], wt["w_ffn_up"]], axis=2)
    sm = {n: w[n] for n in SMALL_PARAMS}

    saved = []
    xc = x2d
    for l in range(DEPTH):
        xc, sv = _layer_fwd(l, xc, mem2d, wt, sm)
        saved.append(sv)
    loss_row, dx, dx_b, dg_final = _loss_head(xc, sm["final_norm_g"], tgt, "loss_head")
    loss = lax.psum(loss_row[0, 0], ("x", "y", "c"))
    gw_all, gs_all = [None] * DEPTH, [None] * DEPTH
    for l in reversed(range(DEPTH)):
        dx, dx_b, gw_all[l], gs_all[l] = _layer_bwd(l, dx, dx_b, mem2d, wt, sm, saved[l])
    grad_x = dx[None]

    gparts = []
    for n, ax in MATRIX_WEIGHTS:
        full = jnp.stack([gw_all[l][n] for l in range(DEPTH)])
        gparts.append(_to_blocks(full, ax).astype(wire[n]))
    recv = _exchange(gparts, "grads_exchange")
    outs = {}
    for (n, _), r in zip(MATRIX_WEIGHTS, recv):
        res = _adamw(r.reshape((N_DEV,) + _rows(w[n]).shape), _rows(w[n]), _rows(mo[n]), _rows(vo[n]), "adamw_" + n)
        outs[n] = [o.reshape(w[n].shape) for o in res]

    gsm = {n: jnp.stack([gs_all[l][n] for l in range(DEPTH)])
           for n in ("mix_norm_g", "xattn_norm_g", "mem_norm_g", "ffn_norm_g", "forget_bias", "sink")}
    gsm["final_norm_g"] = dg_final
    gsm["rel_bias"] = gs_all[0]["rel_bias_l"] + gs_all[1]["rel_bias_l"]
    (small_parts,) = _all_gather([_pack_small([gsm[n] for n in SMALL_PARAMS])], "small_grads_all_gather")
    outs_small = _adamw(small_parts, *[_pack_small([d[n] for n in SMALL_PARAMS]) for d in (w, mo, vo)], "adamw_small")
    for kind in range(4):
        flat, o = outs_small[kind].reshape(-1), 0
        for n in SMALL_PARAMS:
            sz = int(np.prod(w[n].shape))
            outs.setdefault(n, []).append(flat[o:o + sz].reshape(w[n].shape))
            o += sz

    order = ["mix_norm_g", "w_in", "forget_bias", "conv_w", "sink", "w_branch", "w_mix_out", "rel_bias",
             "xattn_norm_g", "mem_norm_g", "w_xq", "w_xkv", "w_xo", "ffn_norm_g", "w_ffn_gate", "w_ffn_up",
             "w_ffn_down", "final_norm_g"]
    result = [loss, grad_x]
    for kind in range(4):
        result += [outs[n][kind] for n in order]
    return tuple(result)
```

```python
import math

import numpy as np
import jax
import jax.numpy as jnp
from jax import lax
from jax.experimental import pallas as pl
from jax.experimental.pallas import tpu as pltpu

F32 = jnp.float32
BF16 = jnp.bfloat16
MESH = pl.DeviceIdType.MESH

LANE = 128
BF16_SUBLANE = 16
V7X_VMEM_REQUEST_CAP = 56 * 2 ** 20
N_DEV = 8

D_MODEL = 1024
DEPTH = 2
HEAD = 64
BRANCH = 512
SWA_BLOCK = 128
SWA_GROUP = 4
N_BUCKETS = 32
X_HEADS = 4
X_HEAD = 256
D_FF = 2816
RMS_EPS = 1e-6
NEG = -1e30
ADAM_LR, ADAM_B1, ADAM_B2, ADAM_EPS, ADAM_WD, ADAM_STEP = 0.001, 0.9, 0.999, 1e-08, 0.01, 10

IN_COLS = 6920
PROJ_COLS = 7040
COL_GATE, COL_CONV, COL_FOX, COL_SQ, COL_SK, COL_SV, COL_FG = 0, 3072, 4608, 6144, 6656, 6784, 6912

ROW_TILE = 512
FOX_TILE = 512
MM_TM, MM_TN, MM_TK = 1024, 1536, 1024


def _pick(n, cap, mult):
    best = None
    for d in range(mult, min(n, cap) + 1, mult):
        if n % d == 0:
            best = d
    return n if best is None else best


def _params(semantics, block_bytes):
    limit = int(min(max(2 * block_bytes + (8 << 20), 24 << 20), V7X_VMEM_REQUEST_CAP))
    return pltpu.CompilerParams(dimension_semantics=semantics, vmem_limit_bytes=limit)


def _nbytes(shape, dtype):
    return int(np.prod(shape)) * jnp.dtype(dtype).itemsize


def _dot(a, b, dims):
    return lax.dot_general(a, b, (dims, ((), ())), preferred_element_type=F32)


NN = ((1,), (0,))
NT = ((1,), (1,))
TN = ((0,), (0,))


def _matmul(a, b, mode, out_dtype, name, residual=None):
    if mode == "nn":
        (m, k), (k2, n) = a.shape, b.shape
    elif mode == "nt":
        (m, k), (n, k2) = a.shape, b.shape
    else:
        (k, m), (k2, n) = a.shape, b.shape
    assert k == k2, (name, a.shape, b.shape)
    tm, tn, tk = _pick(m, MM_TM, LANE), _pick(n, MM_TN, LANE), _pick(k, MM_TK, LANE)
    nk = k // tk
    dims = {"nn": NN, "nt": NT, "tn": TN}[mode]
    has_res = residual is not None

    def body(*refs):
        a_ref, b_ref = refs[0], refs[1]
        r_ref = refs[2] if has_res else None
        o_ref = refs[3] if has_res else refs[2]
        kk = pl.program_id(2)
        p = _dot(a_ref[...].astype(BF16), b_ref[...].astype(BF16), dims)
        if nk == 1:
            if has_res:
                p = p + r_ref[...]
            o_ref[...] = p.astype(out_dtype)
        else:
            acc_ref = refs[-1]

            @pl.when(kk == 0)
            def _():
                acc_ref[...] = p

            @pl.when(kk > 0)
            def _():
                acc_ref[...] += p

            @pl.when(kk == nk - 1)
            def _():
                res = acc_ref[...]
                if has_res:
                    res = res + r_ref[...]
                o_ref[...] = res.astype(out_dtype)

    if mode == "nn":
        a_spec = pl.BlockSpec((tm, tk), lambda i, j, kk: (i, kk))
        b_spec = pl.BlockSpec((tk, tn), lambda i, j, kk: (kk, j))
    elif mode == "nt":
        a_spec = pl.BlockSpec((tm, tk), lambda i, j, kk: (i, kk))
        b_spec = pl.BlockSpec((tn, tk), lambda i, j, kk: (j, kk))
    else:
        a_spec = pl.BlockSpec((tk, tm), lambda i, j, kk: (kk, i))
        b_spec = pl.BlockSpec((tk, tn), lambda i, j, kk: (kk, j))
    o_spec = pl.BlockSpec((tm, tn), lambda i, j, kk: (i, j))
    in_specs, args = [a_spec, b_spec], [a, b]
    if has_res:
        in_specs.append(o_spec)
        args.append(residual)
    blk = (_nbytes((tm, tk), a.dtype) + _nbytes((tk, tn), b.dtype) + _nbytes((tm, tn), out_dtype)
           + (_nbytes((tm, tn), F32) if has_res else 0))
    scratch = [pltpu.VMEM((tm, tn), F32)] if nk > 1 else []
    return pl.pallas_call(
        body, name=name, grid=(m // tm, n // tn, nk),
        out_shape=jax.ShapeDtypeStruct((m, n), out_dtype),
        in_specs=in_specs, out_specs=o_spec, scratch_shapes=scratch,
        compiler_params=_params(("parallel", "parallel", "arbitrary"), blk + _nbytes((tm, tn), F32)),
    )(*args)


def _rms_fwd(x, g, name):
    t, d = x.shape
    tr = _pick(t, ROW_TILE, BF16_SUBLANE)

    def body(x_ref, g_ref, y_ref):
        xv = x_ref[...]
        r = lax.rsqrt(jnp.mean(xv * xv, axis=-1, keepdims=True) + RMS_EPS)
        y_ref[...] = ((xv * r) * g_ref[...]).astype(BF16)

    return pl.pallas_call(
        body, name=name, grid=(t // tr,),
        out_shape=jax.ShapeDtypeStruct((t, d), BF16),
        in_specs=[pl.BlockSpec((tr, d), lambda i: (i, 0)), pl.BlockSpec((1, d), lambda i: (0, 0))],
        out_specs=pl.BlockSpec((tr, d), lambda i: (i, 0)),
        compiler_params=_params(("parallel",), 2 * _nbytes((tr, d), F32)),
    )(x, g.reshape(1, d))


def _rms_bwd(x, g, dy, dres, name):
    t, d = x.shape
    tr = _pick(t, ROW_TILE, BF16_SUBLANE)
    has_res = dres is not None

    def body(*refs):
        x_ref, g_ref, dy_ref = refs[:3]
        r_ref = refs[3] if has_res else None
        dx_ref, dxb_ref, dg_ref = refs[-3:]
        i = pl.program_id(0)
        xv = x_ref[...]
        r = lax.rsqrt(jnp.mean(xv * xv, axis=-1, keepdims=True) + RMS_EPS)
        xh = xv * r
        dyv = dy_ref[...].astype(F32)
        dxh = dyv * g_ref[...]
        dx = r * (dxh - xh * jnp.mean(dxh * xh, axis=-1, keepdims=True))
        if has_res:
            dx = dx + r_ref[...]
        dx_ref[...] = dx
        dxb_ref[...] = dx.astype(BF16)

        @pl.when(i == 0)
        def _():
            dg_ref[...] = jnp.zeros_like(dg_ref)

        dg_ref[...] += jnp.sum(dyv * xh, axis=0, keepdims=True)

    row = pl.BlockSpec((tr, d), lambda i: (i, 0))
    vec = pl.BlockSpec((1, d), lambda i: (0, 0))
    in_specs, args = [row, vec, row], [x, g.reshape(1, d), dy]
    if has_res:
        in_specs.append(row)
        args.append(dres)
    return pl.pallas_call(
        body, name=name, grid=(t // tr,),
        out_shape=(jax.ShapeDtypeStruct((t, d), F32), jax.ShapeDtypeStruct((t, d), BF16),
                   jax.ShapeDtypeStruct((1, d), F32)),
        in_specs=in_specs, out_specs=(row, row, vec),
        compiler_params=_params(("arbitrary",), 5 * _nbytes((tr, d), F32)),
    )(*args)


def _loss_head(x, g, target, name):
    t, d = x.shape
    tr = _pick(t, ROW_TILE, BF16_SUBLANE)

    def body(x_ref, g_ref, t_ref, loss_ref, dx_ref, dxb_ref, dg_ref):
        i = pl.program_id(0)
        xv = x_ref[...]
        gv = g_ref[...]
        r = lax.rsqrt(jnp.mean(xv * xv, axis=-1, keepdims=True) + RMS_EPS)
        xh = xv * r
        diff = xh * gv - t_ref[...]
        part = 0.5 * jnp.sum(jnp.mean(diff * diff, axis=-1, keepdims=True), axis=0, keepdims=True)
        dyv = diff * (1.0 / d)
        dxh = dyv * gv
        dx = r * (dxh - xh * jnp.mean(dxh * xh, axis=-1, keepdims=True))
        dx_ref[...] = dx
        dxb_ref[...] = dx.astype(BF16)

        @pl.when(i == 0)
        def _():
            dg_ref[...] = jnp.zeros_like(dg_ref)
            loss_ref[...] = jnp.zeros_like(loss_ref)

        dg_ref[...] += jnp.sum(dyv * xh, axis=0, keepdims=True)
        loss_ref[...] += jnp.broadcast_to(part, loss_ref.shape)

    row = pl.BlockSpec((tr, d), lambda i: (i, 0))
    vec = pl.BlockSpec((1, d), lambda i: (0, 0))
    return pl.pallas_call(
        body, name=name, grid=(t // tr,),
        out_shape=(jax.ShapeDtypeStruct((1, LANE), F32), jax.ShapeDtypeStruct((t, d), F32),
                   jax.ShapeDtypeStruct((t, d), BF16), jax.ShapeDtypeStruct((1, d), F32)),
        in_specs=[row, vec, row],
        out_specs=(pl.BlockSpec((1, LANE), lambda i: (0, 0)), row, row, vec),
        compiler_params=_params(("arbitrary",), 5 * _nbytes((tr, d), F32)),
    )(x, g.reshape(1, d), target)


HALO = 8


def _conv_fwd(proj, conv_w, name):
    t = proj.shape[0]
    tr = _pick(t, ROW_TILE, BF16_SUBLANE)
    c0 = COL_CONV // BRANCH
    hb = tr // HALO

    def body(cb_ref, cc_ref, cu_ref, hc_ref, hu_ref, w_ref, y_ref):
        i = pl.program_id(0)
        z = cc_ref[...] * cu_ref[...]
        hz = jnp.where(i > 0, hc_ref[...] * hu_ref[...], 0.0)
        zf = jnp.concatenate([hz, z], axis=0)
        z1 = pltpu.roll(zf, 1, 0)[HALO:]
        z2 = pltpu.roll(zf, 2, 0)[HALO:]
        y = w_ref[2:3, :] * z + w_ref[1:2, :] * z1 + w_ref[0:1, :] * z2
        y_ref[...] = (cb_ref[...] * y).astype(BF16)

    def col(c):
        return pl.BlockSpec((tr, BRANCH), lambda i, c=c: (i, c0 + c))

    def prev(c):
        return pl.BlockSpec((HALO, BRANCH), lambda i, c=c: (jnp.maximum(i * hb - 1, 0), c0 + c))

    return pl.pallas_call(
        body, name=name, grid=(t // tr,),
        out_shape=jax.ShapeDtypeStruct((t, BRANCH), BF16),
        in_specs=[col(0), col(1), col(2), prev(1), prev(2), pl.BlockSpec((3, BRANCH), lambda i: (0, 0))],
        out_specs=pl.BlockSpec((tr, BRANCH), lambda i: (i, 0)),
        compiler_params=_params(("parallel",), 6 * _nbytes((tr, BRANCH), F32)),
    )(proj, proj, proj, proj, proj, conv_w)


def _conv_bwd(proj, conv_w, dout, name):
    t = proj.shape[0]
    tr = _pick(t, ROW_TILE, BF16_SUBLANE)
    nblk = t // tr
    c0 = COL_CONV // BRANCH
    hb = tr // HALO
    last_halo = t // HALO - 1

    def body(cb_ref, cc_ref, cu_ref, hc_ref, hu_ref, do_ref, ndo_ref, ncb_ref, w_ref,
             dcb_ref, dcc_ref, dcu_ref, dw_ref):
        i = pl.program_id(0)
        cb, cc, cu = cb_ref[...], cc_ref[...], cu_ref[...]
        w0, w1, w2 = w_ref[0:1, :], w_ref[1:2, :], w_ref[2:3, :]
        z = cc * cu
        hz = jnp.where(i > 0, hc_ref[...] * hu_ref[...], 0.0)
        zf = jnp.concatenate([hz, z], axis=0)
        z1 = pltpu.roll(zf, 1, 0)[HALO:]
        z2 = pltpu.roll(zf, 2, 0)[HALO:]
        y = w2 * z + w1 * z1 + w0 * z2
        dout_v = do_ref[...]
        dyc = dout_v * cb
        hdy = jnp.where(i < nblk - 1, ndo_ref[...] * ncb_ref[...], 0.0)
        dyf = jnp.concatenate([dyc, hdy], axis=0)
        dy1 = pltpu.roll(dyf, tr + HALO - 1, 0)[:tr]
        dy2 = pltpu.roll(dyf, tr + HALO - 2, 0)[:tr]
        dz = w2 * dyc + w1 * dy1 + w0 * dy2
        dcb_ref[...] = (dout_v * y).astype(BF16)
        dcc_ref[...] = (dz * cu).astype(BF16)
        dcu_ref[...] = (dz * cc).astype(BF16)

        @pl.when(i == 0)
        def _():
            dw_ref[...] = jnp.zeros_like(dw_ref)

        dw_ref[0:1, :] += jnp.sum(dyc * z2, axis=0, keepdims=True)
        dw_ref[1:2, :] += jnp.sum(dyc * z1, axis=0, keepdims=True)
        dw_ref[2:3, :] += jnp.sum(dyc * z, axis=0, keepdims=True)

    def col(c):
        return pl.BlockSpec((tr, BRANCH), lambda i, c=c: (i, c0 + c))

    def prev(c):
        return pl.BlockSpec((HALO, BRANCH), lambda i, c=c: (jnp.maximum(i * hb - 1, 0), c0 + c))

    def nxt(c):
        return pl.BlockSpec((HALO, BRANCH), lambda i, c=c: (jnp.minimum((i + 1) * hb, last_halo), c))

    row = pl.BlockSpec((tr, BRANCH), lambda i: (i, 0))
    return pl.pallas_call(
        body, name=name, grid=(nblk,),
        out_shape=(jax.ShapeDtypeStruct((t, BRANCH), BF16),) * 3 + (jax.ShapeDtypeStruct((HALO, BRANCH), F32),),
        in_specs=[col(0), col(1), col(2), prev(1), prev(2), row, nxt(0), nxt(c0),
                  pl.BlockSpec((3, BRANCH), lambda i: (0, 0))],
        out_specs=(row, row, row, pl.BlockSpec((HALO, BRANCH), lambda i: (0, 0))),
        compiler_params=_params(("arbitrary",), 8 * _nbytes((tr, BRANCH), F32)),
    )(proj, proj, proj, proj, proj, dout, dout, proj, conv_w)


def _tri(lower):
    r = lax.broadcasted_iota(jnp.int32, (LANE, LANE), 0)
    c = lax.broadcasted_iota(jnp.int32, (LANE, LANE), 1)
    return jnp.where((c <= r) if lower else (c >= r), 1.0, 0.0).astype(F32)


def _logf_cumsum(proj, fbias_row, name):
    t = proj.shape[0]
    nchunk = t // LANE

    def body(f_ref, b_ref, c_ref, run_sc):
        tri = _tri(True)
        run_sc[...] = jnp.zeros_like(run_sc)

        @pl.loop(0, nchunk)
        def _(i):
            rows = pl.ds(pl.multiple_of(i * LANE, LANE), LANE)
            z = f_ref[rows, :] + b_ref[...]
            logf = jnp.minimum(z, 0.0) - jnp.log(1.0 + jnp.exp(-jnp.abs(z)))
            cs = lax.dot_general(tri, logf, (NN, ((), ())), precision=lax.Precision.HIGHEST,
                                 preferred_element_type=F32) + run_sc[0:1, :]
            c_ref[rows, :] = cs
            run_sc[0:1, :] = cs[LANE - 1:LANE, :]

    return pl.pallas_call(
        body, name=name, grid=(1,),
        out_shape=jax.ShapeDtypeStruct((t, LANE), F32),
        in_specs=[pl.BlockSpec((t, LANE), lambda i: (0, COL_FG // LANE)), pl.BlockSpec((1, LANE), lambda i: (0, 0))],
        out_specs=pl.BlockSpec((t, LANE), lambda i: (0, 0)),
        scratch_shapes=[pltpu.VMEM((8, LANE), F32)],
        compiler_params=_params(("arbitrary",), 2 * _nbytes((t, LANE), F32)),
    )(proj, fbias_row)


def _logf_cumsum_bwd(proj, fbias_row, dcq, dck, name):
    t = proj.shape[0]
    tb = _pick(t, 2 * ROW_TILE, LANE)
    nblk = t // tb

    def body(f_ref, b_ref, dq_ref, dk_ref, df_ref, db_ref, run_sc):
        i = pl.program_id(0)
        tri = _tri(False)

        @pl.when(i == 0)
        def _():
            run_sc[...] = jnp.zeros_like(run_sc)
            db_ref[...] = jnp.zeros_like(db_ref)

        for c in reversed(range(tb // LANE)):
            rows = slice(c * LANE, (c + 1) * LANE)
            dcc = dq_ref[0, rows, :] + dk_ref[0, rows, :]
            for p in range(1, 4):
                dcc = dcc + (dq_ref[p, rows, :] + dk_ref[p, rows, :])
            ss = lax.dot_general(tri, dcc, (NN, ((), ())), precision=lax.Precision.HIGHEST,
                                 preferred_element_type=F32) + run_sc[0:1, :]
            z = f_ref[rows, :] + b_ref[...]
            dz = ss * (1.0 / (1.0 + jnp.exp(z)))
            df_ref[rows, :] = dz.astype(BF16)
            run_sc[0:1, :] = ss[0:1, :]
            db_ref[...] += jnp.sum(dz, axis=0, keepdims=True)

    piece = pl.BlockSpec((4, tb, LANE), lambda i: (0, nblk - 1 - i, 0))
    return pl.pallas_call(
        body, name=name, grid=(nblk,),
        out_shape=(jax.ShapeDtypeStruct((t, LANE), BF16), jax.ShapeDtypeStruct((1, LANE), F32)),
        in_specs=[pl.BlockSpec((tb, LANE), lambda i: (nblk - 1 - i, COL_FG // LANE)),
                  pl.BlockSpec((1, LANE), lambda i: (0, 0)), piece, piece],
        out_specs=(pl.BlockSpec((tb, LANE), lambda i: (nblk - 1 - i, 0)), pl.BlockSpec((1, LANE), lambda i: (0, 0))),
        scratch_shapes=[pltpu.VMEM((8, LANE), F32)],
        compiler_params=_params(("arbitrary",), 12 * _nbytes((tb, LANE), F32)),
    )(proj, fbias_row, dcq, dck)


def _lo_mask():
    return lax.broadcasted_iota(jnp.int32, (1, LANE), 1) < HEAD


def _causal_steps(n, key_major):
    if key_major:
        pairs = [(iq, ik) for ik in range(n) for iq in range(ik, n)]
    else:
        pairs = [(iq, ik) for iq in range(n) for ik in range(iq + 1)]
    return (jnp.asarray([p[0] for p in pairs], jnp.int32), jnp.asarray([p[1] for p in pairs], jnp.int32))


def _head_lanes(j, pair_vals):
    lane = lax.broadcasted_iota(jnp.int32, (1, LANE), 1)
    return jnp.where(lane == 2 * j, pair_vals[0], 0.0) + jnp.where(lane == 2 * j + 1, pair_vals[1], 0.0)


def _fox_fwd(proj, c_col, c_row, name):
    t = proj.shape[0]
    tq = _pick(t, FOX_TILE, LANE)
    nq = t // tq
    rep = tq // LANE
    scale = HEAD ** -0.5
    cq, ck, cv = COL_FOX // LANE, COL_FOX // LANE + 4, COL_FOX // LANE + 8
    q_tab, k_tab = _causal_steps(nq, False)

    def body(qt_ref, kt_ref, q_ref, k_ref, v_ref, cc_ref, cr_ref, y_ref, lse_ref, m_sc, l_sc, acc_sc):
        step_id = pl.program_id(1)
        iq, ik = qt_ref[step_id], kt_ref[step_id]
        lo = _lo_mask()

        @pl.when(ik == 0)
        def _():
            m_sc[...] = jnp.full(m_sc.shape, NEG, F32)
            l_sc[...] = jnp.zeros_like(l_sc)
            acc_sc[...] = jnp.zeros_like(acc_sc)

        def step(diag):
            q2 = q_ref[...].astype(BF16)
            k2 = k_ref[...].astype(BF16)
            v2 = v_ref[...].astype(BF16)
            for h in range(2):
                msk = lo if h == 0 else jnp.logical_not(lo)
                qh = jnp.where(msk, q2, jnp.zeros_like(q2))
                s = _dot(qh, k2, NT) * scale
                s = s + jnp.tile(cc_ref[h], (1, rep)) - cr_ref[h]
                if diag:
                    row = lax.broadcasted_iota(jnp.int32, (tq, tq), 0)
                    col = lax.broadcasted_iota(jnp.int32, (tq, tq), 1)
                    s = jnp.where(col <= row, s, NEG)
                m_prev = m_sc[h]
                m_new = jnp.maximum(m_prev, jnp.max(s, axis=1, keepdims=True))
                alpha = jnp.exp(m_prev - m_new)
                p = jnp.exp(s - jnp.tile(m_new, (1, rep)))
                l_sc[h] = alpha * l_sc[h] + jnp.sum(p, axis=1, keepdims=True)
                acc_sc[h] = alpha * acc_sc[h] + _dot(p.astype(BF16), v2, NN)
                m_sc[h] = m_new

        @pl.when(ik < iq)
        def _():
            step(False)

        @pl.when(ik == iq)
        def _():
            step(True)
            y = jnp.where(lo, acc_sc[0] / l_sc[0], acc_sc[1] / l_sc[1])
            y_ref[...] = y.astype(BF16)
            lse_ref[...] = m_sc[...] + jnp.log(l_sc[...])

    def kv(c):
        return pl.BlockSpec((tq, LANE), lambda j, s, qt, kt, c=c: (kt[s], c + j))

    grid_spec = pltpu.PrefetchScalarGridSpec(
        num_scalar_prefetch=2, grid=(4, int(q_tab.shape[0])),
        in_specs=[pl.BlockSpec((tq, LANE), lambda j, s, qt, kt: (qt[s], cq + j)), kv(ck), kv(cv),
                  pl.BlockSpec((2, tq, LANE), lambda j, s, qt, kt: (j, qt[s], 0)),
                  pl.BlockSpec((2, 1, tq), lambda j, s, qt, kt: (j, 0, kt[s]))],
        out_specs=(pl.BlockSpec((tq, LANE), lambda j, s, qt, kt: (qt[s], j)),
                   pl.BlockSpec((2, tq, LANE), lambda j, s, qt, kt: (j, qt[s], 0))),
        scratch_shapes=[pltpu.VMEM((2, tq, LANE), F32)] * 3)
    return pl.pallas_call(
        body, name=name, grid_spec=grid_spec,
        out_shape=(jax.ShapeDtypeStruct((t, BRANCH), BF16), jax.ShapeDtypeStruct((8, t, LANE), F32)),
        compiler_params=_params(("parallel", "arbitrary"),
                                16 * _nbytes((tq, LANE), F32) + 6 * _nbytes((tq, tq), F32)),
    )(q_tab, k_tab, proj, proj, proj, c_col, c_row)


def _fox_dq(proj, c_col, c_row, lse, y, dy, name):
    t = proj.shape[0]
    tq = _pick(t, FOX_TILE, LANE)
    nq = t // tq
    rep = tq // LANE
    scale = HEAD ** -0.5
    cq, ck, cv = COL_FOX // LANE, COL_FOX // LANE + 4, COL_FOX // LANE + 8
    q_tab, k_tab = _causal_steps(nq, False)

    def body(qt_ref, kt_ref, q_ref, k_ref, v_ref, cc_ref, cr_ref, lse_ref, y_ref, dy_ref,
             dq_ref, d_ref, dcq_ref, dq_sc, dcq_sc):
        j, step_id = pl.program_id(0), pl.program_id(1)
        iq, ik = qt_ref[step_id], kt_ref[step_id]
        lo = _lo_mask()

        @pl.when(ik == 0)
        def _():
            dq_sc[...] = jnp.zeros_like(dq_sc)
            dcq_sc[...] = jnp.zeros_like(dcq_sc)
            prod = y_ref[...].astype(F32) * dy_ref[...].astype(F32)
            d_ref[0] = jnp.broadcast_to(jnp.sum(jnp.where(lo, prod, 0.0), axis=1, keepdims=True), (tq, LANE))
            d_ref[1] = jnp.broadcast_to(jnp.sum(jnp.where(lo, 0.0, prod), axis=1, keepdims=True), (tq, LANE))

        def step(diag):
            q2 = q_ref[...].astype(BF16)
            k2 = k_ref[...].astype(BF16)
            v2 = v_ref[...].astype(BF16)
            do2 = dy_ref[...]
            for h in range(2):
                msk = lo if h == 0 else jnp.logical_not(lo)
                qh = jnp.where(msk, q2, jnp.zeros_like(q2))
                doh = jnp.where(msk, do2, jnp.zeros_like(do2))
                s = _dot(qh, k2, NT) * scale
                s = s + jnp.tile(cc_ref[h], (1, rep)) - cr_ref[h]
                if diag:
                    row = lax.broadcasted_iota(jnp.int32, (tq, tq), 0)
                    col = lax.broadcasted_iota(jnp.int32, (tq, tq), 1)
                    s = jnp.where(col <= row, s, NEG)
                p = jnp.exp(s - jnp.tile(lse_ref[h], (1, rep)))
                dp = _dot(doh, v2, NT)
                ds = p * (dp - jnp.tile(d_ref[h], (1, rep)))
                dq_sc[h] += _dot(ds.astype(BF16), k2, NN)
                dcq_sc[h] += jnp.sum(ds, axis=1, keepdims=True)

        @pl.when(ik < iq)
        def _():
            step(False)

        @pl.when(ik == iq)
        def _():
            step(True)
            dq_ref[...] = (jnp.where(lo, dq_sc[0], dq_sc[1]) * scale).astype(BF16)
            dcq_ref[...] = _head_lanes(j, dcq_sc)

    def kv(c):
        return pl.BlockSpec((tq, LANE), lambda j, s, qt, kt, c=c: (kt[s], c + j))

    stat = pl.BlockSpec((2, tq, LANE), lambda j, s, qt, kt: (j, qt[s], 0))
    pair = pl.BlockSpec((tq, LANE), lambda j, s, qt, kt: (qt[s], j))
    grid_spec = pltpu.PrefetchScalarGridSpec(
        num_scalar_prefetch=2, grid=(4, int(q_tab.shape[0])),
        in_specs=[pl.BlockSpec((tq, LANE), lambda j, s, qt, kt: (qt[s], cq + j)), kv(ck), kv(cv), stat,
                  pl.BlockSpec((2, 1, tq), lambda j, s, qt, kt: (j, 0, kt[s])), stat, pair, pair],
        out_specs=(pair, stat, pl.BlockSpec((None, tq, LANE), lambda j, s, qt, kt: (j, qt[s], 0))),
        scratch_shapes=[pltpu.VMEM((2, tq, LANE), F32)] * 2)
    return pl.pallas_call(
        body, name=name, grid_spec=grid_spec,
        out_shape=(jax.ShapeDtypeStruct((t, BRANCH), BF16), jax.ShapeDtypeStruct((8, t, LANE), F32),
                   jax.ShapeDtypeStruct((4, t, LANE), F32)),
        compiler_params=_params(("parallel", "arbitrary"),
                                22 * _nbytes((tq, LANE), F32) + 8 * _nbytes((tq, tq), F32)),
    )(q_tab, k_tab, proj, proj, proj, c_col, c_row, lse, y, dy)


def _fox_dkv(proj, c_col, c_row, lse_row, d_row, dy, name):
    t = proj.shape[0]
    tk = _pick(t, FOX_TILE, LANE)
    nk = t // tk
    rep = tk // LANE
    scale = HEAD ** -0.5
    cq, ck, cv = COL_FOX // LANE, COL_FOX // LANE + 4, COL_FOX // LANE + 8
    q_tab, k_tab = _causal_steps(nk, True)

    def body(qt_ref, kt_ref, k_ref, v_ref, q_ref, dy_ref, ck_ref, cqr_ref, lser_ref, dr_ref,
             dk_ref, dv_ref, dc_ref, dk_sc, dv_sc, dc_sc):
        j, step_id = pl.program_id(0), pl.program_id(1)
        iq, ik = qt_ref[step_id], kt_ref[step_id]
        lo = _lo_mask()

        @pl.when(iq == ik)
        def _():
            dk_sc[...] = jnp.zeros_like(dk_sc)
            dv_sc[...] = jnp.zeros_like(dv_sc)
            dc_sc[...] = jnp.zeros_like(dc_sc)

        def step(diag):
            k2 = k_ref[...].astype(BF16)
            v2 = v_ref[...].astype(BF16)
            q2 = q_ref[...].astype(BF16)
            do2 = dy_ref[...]
            for h in range(2):
                msk = lo if h == 0 else jnp.logical_not(lo)
                kh = jnp.where(msk, k2, jnp.zeros_like(k2))
                vh = jnp.where(msk, v2, jnp.zeros_like(v2))
                st = _dot(kh, q2, NT) * scale
                st = st + cqr_ref[h] - jnp.tile(ck_ref[h], (1, rep))
                if diag:
                    krow = lax.broadcasted_iota(jnp.int32, (tk, tk), 0)
                    qcol = lax.broadcasted_iota(jnp.int32, (tk, tk), 1)
                    st = jnp.where(krow <= qcol, st, NEG)
                pt = jnp.exp(st - lser_ref[h])
                dpt = _dot(vh, do2, NT)
                dst = pt * (dpt - dr_ref[h])
                dv_sc[h] += _dot(pt.astype(BF16), do2, NN)
                dk_sc[h] += _dot(dst.astype(BF16), q2, NN)
                dc_sc[h] -= jnp.sum(dst, axis=1, keepdims=True)

        @pl.when(iq > ik)
        def _():
            step(False)

        @pl.when(iq == ik)
        def _():
            step(True)

        @pl.when(iq == nk - 1)
        def _():
            dk_ref[...] = (jnp.where(lo, dk_sc[0], dk_sc[1]) * scale).astype(BF16)
            dv_ref[...] = jnp.where(lo, dv_sc[0], dv_sc[1]).astype(BF16)
            dc_ref[...] = _head_lanes(j, dc_sc)

    def kcol(c):
        return pl.BlockSpec((tk, LANE), lambda j, s, qt, kt, c=c: (kt[s], c + j))

    qrow = pl.BlockSpec((2, 1, tk), lambda j, s, qt, kt: (j, 0, qt[s]))
    pair_k = pl.BlockSpec((tk, LANE), lambda j, s, qt, kt: (kt[s], j))
    grid_spec = pltpu.PrefetchScalarGridSpec(
        num_scalar_prefetch=2, grid=(4, int(q_tab.shape[0])),
        in_specs=[kcol(ck), kcol(cv),
                  pl.BlockSpec((tk, LANE), lambda j, s, qt, kt: (qt[s], cq + j)),
                  pl.BlockSpec((tk, LANE), lambda j, s, qt, kt: (qt[s], j)),
                  pl.BlockSpec((2, tk, LANE), lambda j, s, qt, kt: (j, kt[s], 0)), qrow, qrow, qrow],
        out_specs=(pair_k, pair_k, pl.BlockSpec((None, tk, LANE), lambda j, s, qt, kt: (j, kt[s], 0))),
        scratch_shapes=[pltpu.VMEM((2, tk, LANE), F32)] * 3)
    return pl.pallas_call(
        body, name=name, grid_spec=grid_spec,
        out_shape=(jax.ShapeDtypeStruct((t, BRANCH), BF16), jax.ShapeDtypeStruct((t, BRANCH), BF16),
                   jax.ShapeDtypeStruct((4, t, LANE), F32)),
        compiler_params=_params(("parallel", "arbitrary"),
                                24 * _nbytes((tk, LANE), F32) + 8 * _nbytes((tk, tk), F32)),
    )(q_tab, k_tab, proj, proj, proj, dy, c_col, c_row, lse_row, d_row)


def _swa_tables(rel_bias):
    tq = np.arange(SWA_BLOCK)[:, None]
    sk = np.arange(2 * SWA_BLOCK)[None, :]
    dist = SWA_BLOCK + tq - sk
    inwin = (dist >= 0) & (dist < SWA_BLOCK)
    n = np.maximum(dist, 0)
    max_exact = N_BUCKETS // 2
    large = max_exact + (np.log(np.maximum(n, 1).astype(np.float32) / max_exact)
                         / math.log(SWA_BLOCK / max_exact) * (N_BUCKETS - max_exact)).astype(np.int32)
    bucket = np.where(n < max_exact, n, np.minimum(large, N_BUCKETS - 1))
    onehot = (bucket[..., None] == np.arange(N_BUCKETS)) & inwin[..., None]
    onehot = jnp.asarray(onehot.astype(np.float32))
    bias = jnp.einsum("tsb,bh->hts", onehot, rel_bias, precision=lax.Precision.HIGHEST)
    bias = jnp.where(jnp.asarray(inwin)[None], bias, NEG)
    return onehot, bias


def _swa_fwd(proj, bias, sink_rep, name):
    t = proj.shape[0]
    nb = t // SWA_BLOCK
    scale = HEAD ** -0.5
    csq, csk, csv = COL_SQ // 256, COL_SK // LANE, COL_SV // LANE

    def body(q_ref, kp_ref, kc_ref, vp_ref, vc_ref, b_ref, sk_ref, y_ref, lse_ref):
        kvh, n = pl.program_id(0), pl.program_id(1)
        lane = lax.broadcasted_iota(jnp.int32, (1, LANE), 1)
        lo = lane < HEAD
        kvm = jnp.logical_and(lane >= kvh * HEAD, lane < (kvh + 1) * HEAD)

        def both(prev_ref, cur_ref):
            band = jnp.concatenate([prev_ref[...], cur_ref[...]], axis=0)
            band = jnp.where(kvm, band, 0.0)
            return (band + pltpu.roll(band, HEAD, 1)).astype(BF16)

        kb, vb = both(kp_ref, kc_ref), both(vp_ref, vc_ref)
        col = lax.broadcasted_iota(jnp.int32, (SWA_BLOCK, 2 * SWA_BLOCK), 1)
        first = jnp.logical_and(n == 0, col < SWA_BLOCK)
        outs = []
        for g in range(SWA_GROUP):
            half = q_ref[:, (g // 2) * LANE:(g // 2 + 1) * LANE]
            hm = lo if g % 2 == 0 else jnp.logical_not(lo)
            qg = jnp.where(hm, half, 0.0).astype(BF16)
            s = _dot(qg, kb, NT) * scale + b_ref[g]
            s = jnp.where(first, NEG, s)
            snk = sk_ref[g:g + 1, :]
            m = jnp.maximum(jnp.max(s, axis=1, keepdims=True), snk)
            p = jnp.exp(s - jnp.tile(m, (1, 2)))
            denom = jnp.sum(p, axis=1, keepdims=True) + jnp.exp(snk - m)
            outs.append(_dot(p.astype(BF16), vb, NN) / denom)
            lse_ref[g] = m + jnp.log(denom)
        y_ref[:, 0:LANE] = jnp.where(lo, outs[0], outs[1]).astype(BF16)
        y_ref[:, LANE:2 * LANE] = jnp.where(lo, outs[2], outs[3]).astype(BF16)

    def blk(c, shift):
        return pl.BlockSpec((SWA_BLOCK, LANE), lambda kvh, n, c=c, s=shift: (jnp.maximum(n - s, 0), c))

    return pl.pallas_call(
        body, name=name, grid=(2, nb),
        out_shape=(jax.ShapeDtypeStruct((t, BRANCH), BF16), jax.ShapeDtypeStruct((8, t, LANE), F32)),
        in_specs=[pl.BlockSpec((SWA_BLOCK, 256), lambda kvh, n: (n, csq + kvh)),
                  blk(csk, 1), blk(csk, 0), blk(csv, 1), blk(csv, 0),
                  pl.BlockSpec((None, SWA_GROUP, SWA_BLOCK, 256), lambda kvh, n: (kvh, 0, 0, 0)),
                  pl.BlockSpec((None, SWA_GROUP, LANE), lambda kvh, n: (kvh, 0, 0))],
        out_specs=(pl.BlockSpec((SWA_BLOCK, 256), lambda kvh, n: (n, kvh)),
                   pl.BlockSpec((SWA_GROUP, SWA_BLOCK, LANE), lambda kvh, n: (kvh, n, 0))),
        compiler_params=_params(("parallel", "arbitrary"), 4 << 20),
    )(proj, proj, proj, proj, proj, bias.reshape(2, SWA_GROUP, SWA_BLOCK, 256), sink_rep)


def _swa_bwd(proj, bias, sink_rep, lse, y, dy, name):
    t = proj.shape[0]
    nb = t // SWA_BLOCK
    scale = HEAD ** -0.5
    csq, csk, csv = COL_SQ // 256, COL_SK // LANE, COL_SV // LANE

    def body(q_ref, kp_ref, kc_ref, vp_ref, vc_ref, b_ref, sk_ref, lse_ref, y_ref, dy_ref,
             dq_ref, dkp_ref, dvp_ref, db_ref, dsk_ref):
        kvh, n = pl.program_id(0), pl.program_id(1)
        lane = lax.broadcasted_iota(jnp.int32, (1, LANE), 1)
        lo = lane < HEAD
        kvm = jnp.logical_and(lane >= kvh * HEAD, lane < (kvh + 1) * HEAD)

        def both(prev_ref, cur_ref):
            band = jnp.concatenate([prev_ref[...], cur_ref[...]], axis=0)
            band = jnp.where(kvm, band, 0.0)
            return (band + pltpu.roll(band, HEAD, 1)).astype(BF16)

        kb, vb = both(kp_ref, kc_ref), both(vp_ref, vc_ref)
        col = lax.broadcasted_iota(jnp.int32, (SWA_BLOCK, 2 * SWA_BLOCK), 1)
        first = jnp.logical_and(n == 0, col < SWA_BLOCK)

        @pl.when(n == 0)
        def _():
            db_ref[...] = jnp.zeros_like(db_ref)
            dsk_ref[...] = jnp.zeros_like(dsk_ref)

        dk_full = jnp.zeros((2 * SWA_BLOCK, LANE), F32)
        dv_full = jnp.zeros((2 * SWA_BLOCK, LANE), F32)
        dqs = []
        for g in range(SWA_GROUP):
            sl = slice((g // 2) * LANE, (g // 2 + 1) * LANE)
            hm = lo if g % 2 == 0 else jnp.logical_not(lo)
            qg = jnp.where(hm, q_ref[:, sl], 0.0).astype(BF16)
            dog = jnp.where(hm, dy_ref[:, sl], jnp.zeros((SWA_BLOCK, LANE), BF16))
            dmat = jnp.where(hm, y_ref[:, sl].astype(F32) * dy_ref[:, sl].astype(F32), 0.0)
            dg = jnp.sum(dmat, axis=1, keepdims=True)
            s = _dot(qg, kb, NT) * scale + b_ref[g]
            s = jnp.where(first, NEG, s)
            lse_g = lse_ref[g]
            p = jnp.exp(s - jnp.tile(lse_g, (1, 2)))
            dp = _dot(dog, vb, NT)
            ds = p * (dp - dg)
            dsb = ds.astype(BF16)
            dqs.append(_dot(dsb, kb, NN) * scale)
            dk_full = dk_full + _dot(dsb, qg, TN)
            dv_full = dv_full + _dot(p.astype(BF16), dog, TN)
            db_ref[g] += ds
            psink = jnp.exp(sk_ref[g:g + 1, :] - lse_g)
            dsk_ref[g:g + 1, :] -= jnp.sum(psink * dg, axis=0, keepdims=True)
        dq_ref[:, 0:LANE] = jnp.where(lo, dqs[0], dqs[1]).astype(BF16)
        dq_ref[:, LANE:2 * LANE] = jnp.where(lo, dqs[2], dqs[3]).astype(BF16)
        dkp_ref[...] = jnp.where(kvm, (dk_full + pltpu.roll(dk_full, HEAD, 1)) * scale, 0.0)
        dvp_ref[...] = jnp.where(kvm, dv_full + pltpu.roll(dv_full, HEAD, 1), 0.0)

    def blk(c, shift):
        return pl.BlockSpec((SWA_BLOCK, LANE), lambda kvh, n, c=c, s=shift: (jnp.maximum(n - s, 0), c))

    qblk = pl.BlockSpec((SWA_BLOCK, 256), lambda kvh, n: (n, kvh))
    part = pl.BlockSpec((None, None, 2 * SWA_BLOCK, LANE), lambda kvh, n: (kvh, n, 0, 0))
    bspec = pl.BlockSpec((None, SWA_GROUP, SWA_BLOCK, 256), lambda kvh, n: (kvh, 0, 0, 0))
    sspec = pl.BlockSpec((None, SWA_GROUP, LANE), lambda kvh, n: (kvh, 0, 0))
    return pl.pallas_call(
        body, name=name, grid=(2, nb),
        out_shape=(jax.ShapeDtypeStruct((t, BRANCH), BF16),
                   jax.ShapeDtypeStruct((2, nb, 2 * SWA_BLOCK, LANE), F32),
                   jax.ShapeDtypeStruct((2, nb, 2 * SWA_BLOCK, LANE), F32),
                   jax.ShapeDtypeStruct((2, SWA_GROUP, SWA_BLOCK, 256), F32),
                   jax.ShapeDtypeStruct((2, SWA_GROUP, LANE), F32)),
        in_specs=[pl.BlockSpec((SWA_BLOCK, 256), lambda kvh, n: (n, csq + kvh)),
                  blk(csk, 1), blk(csk, 0), blk(csv, 1), blk(csv, 0), bspec, sspec,
                  pl.BlockSpec((SWA_GROUP, SWA_BLOCK, LANE), lambda kvh, n: (kvh, n, 0)), qblk, qblk],
        out_specs=(qblk, part, part, bspec, sspec),
        compiler_params=_params(("parallel", "arbitrary"), 6 << 20),
    )(proj, proj, proj, proj, proj, bias.reshape(2, SWA_GROUP, SWA_BLOCK, 256), sink_rep, lse, y, dy)


def _gate_fwd(proj, pb, name):
    t = proj.shape[0]
    tr = _pick(t, ROW_TILE // 2, BF16_SUBLANE)

    def body(g0, g1, g2, p0, p1, p2, o_ref):
        acc = jax.nn.sigmoid(g0[...]) * p0[...]
        acc = acc + jax.nn.sigmoid(g1[...]) * p1[...]
        acc = acc + jax.nn.sigmoid(g2[...]) * p2[...]
        o_ref[...] = acc.astype(BF16)

    row = pl.BlockSpec((tr, D_MODEL), lambda i: (i, 0))
    gates = [pl.BlockSpec((tr, D_MODEL), lambda i, b=b: (i, b)) for b in range(3)]
    return pl.pallas_call(
        body, name=name, grid=(t // tr,),
        out_shape=jax.ShapeDtypeStruct((t, D_MODEL), BF16),
        in_specs=gates + [row] * 3, out_specs=row,
        compiler_params=_params(("parallel",), 7 * _nbytes((tr, D_MODEL), F32)),
    )(proj, proj, proj, *pb)


def _gate_bwd(proj, pb, dmerged, name):
    t = proj.shape[0]
    tr = _pick(t, ROW_TILE // 2, BF16_SUBLANE)

    def body(g0, g1, g2, p0, p1, p2, dm_ref, dp0, dp1, dp2, dg_ref):
        dm = dm_ref[...]
        for b, (g_ref, p_ref, dp_ref) in enumerate(((g0, p0, dp0), (g1, p1, dp1), (g2, p2, dp2))):
            sg = jax.nn.sigmoid(g_ref[...])
            dp_ref[...] = (dm * sg).astype(BF16)
            dg_ref[:, b * D_MODEL:(b + 1) * D_MODEL] = (dm * p_ref[...] * sg * (1.0 - sg)).astype(BF16)

    row = pl.BlockSpec((tr, D_MODEL), lambda i: (i, 0))
    gates = [pl.BlockSpec((tr, D_MODEL), lambda i, b=b: (i, b)) for b in range(3)]
    return pl.pallas_call(
        body, name=name, grid=(t // tr,),
        out_shape=(jax.ShapeDtypeStruct((t, D_MODEL), BF16),) * 3 + (jax.ShapeDtypeStruct((t, 3 * D_MODEL), BF16),),
        in_specs=gates + [row] * 4,
        out_specs=(row, row, row, pl.BlockSpec((tr, 3 * D_MODEL), lambda i: (i, 0))),
        compiler_params=_params(("parallel",), 11 * _nbytes((tr, D_MODEL), F32)),
    )(proj, proj, proj, *pb, dmerged)


def _swiglu_fwd(ab, name):
    t = ab.shape[0]
    tr = _pick(t, ROW_TILE, BF16_SUBLANE)
    tc = D_FF // 2

    def body(a_ref, b_ref, o_ref):
        a = a_ref[...]
        o_ref[...] = (a * jax.nn.sigmoid(a) * b_ref[...]).astype(BF16)

    return pl.pallas_call(
        body, name=name, grid=(t // tr, 2),
        out_shape=jax.ShapeDtypeStruct((t, D_FF), BF16),
        in_specs=[pl.BlockSpec((tr, tc), lambda i, j: (i, j)), pl.BlockSpec((tr, tc), lambda i, j: (i, j + 2))],
        out_specs=pl.BlockSpec((tr, tc), lambda i, j: (i, j)),
        compiler_params=_params(("parallel", "parallel"), 3 * _nbytes((tr, tc), F32)),
    )(ab, ab)


def _swiglu_bwd(ab, dh, name):
    t = ab.shape[0]
    tr = _pick(t, ROW_TILE, BF16_SUBLANE)
    tc = D_FF // 2

    def body(a_ref, b_ref, dh_ref, o_ref):
        jj = pl.program_id(1)
        a, b, d = a_ref[...], b_ref[...], dh_ref[...]
        sg = jax.nn.sigmoid(a)
        da = d * b * (sg * (1.0 + a * (1.0 - sg)))
        db = d * (a * sg)
        o_ref[...] = jnp.where(jj < 2, da, db).astype(BF16)

    return pl.pallas_call(
        body, name=name, grid=(t // tr, 4),
        out_shape=jax.ShapeDtypeStruct((t, 2 * D_FF), BF16),
        in_specs=[pl.BlockSpec((tr, tc), lambda i, j: (i, j % 2)),
                  pl.BlockSpec((tr, tc), lambda i, j: (i, j % 2 + 2)),
                  pl.BlockSpec((tr, tc), lambda i, j: (i, j % 2))],
        out_specs=pl.BlockSpec((tr, tc), lambda i, j: (i, j)),
        compiler_params=_params(("parallel", "parallel"), 4 * _nbytes((tr, tc), F32)),
    )(ab, ab, dh)


def _xattn_fwd(q, kv, name):
    t = q.shape[0]
    tq = _pick(t, ROW_TILE, BF16_SUBLANE)
    mlen = kv.shape[0]
    scale = X_HEAD ** -0.5

    def body(q_ref, kv_ref, o_ref):
        for h in range(X_HEADS):
            sl = slice(h * X_HEAD, (h + 1) * X_HEAD)
            kh = kv_ref[:, sl]
            vh = kv_ref[:, D_MODEL + h * X_HEAD:D_MODEL + (h + 1) * X_HEAD]
            s = _dot(q_ref[:, sl], kh, NT) * scale
            p = jnp.exp(s - jnp.max(s, axis=1, keepdims=True))
            l = jnp.sum(p, axis=1, keepdims=True)
            o_ref[:, sl] = (_dot(p.astype(BF16), vh, NN) / l).astype(BF16)

    return pl.pallas_call(
        body, name=name, grid=(t // tq,),
        out_shape=jax.ShapeDtypeStruct((t, D_MODEL), BF16),
        in_specs=[pl.BlockSpec((tq, D_MODEL), lambda i: (i, 0)), pl.BlockSpec((mlen, 2 * D_MODEL), lambda i: (0, 0))],
        out_specs=pl.BlockSpec((tq, D_MODEL), lambda i: (i, 0)),
        compiler_params=_params(("parallel",), 4 * _nbytes((tq, D_MODEL), F32)),
    )(q, kv)


def _xattn_bwd(q, kv, do, name):
    t = q.shape[0]
    tq = _pick(t, ROW_TILE, BF16_SUBLANE)
    mlen = kv.shape[0]
    scale = X_HEAD ** -0.5

    def body(q_ref, kv_ref, do_ref, dq_ref, dkv_ref):
        i = pl.program_id(0)

        @pl.when(i == 0)
        def _():
            dkv_ref[...] = jnp.zeros_like(dkv_ref)

        for h in range(X_HEADS):
            sl = slice(h * X_HEAD, (h + 1) * X_HEAD)
            vsl = slice(D_MODEL + h * X_HEAD, D_MODEL + (h + 1) * X_HEAD)
            qh, kh, vh, doh = q_ref[:, sl], kv_ref[:, sl], kv_ref[:, vsl], do_ref[:, sl]
            s = _dot(qh, kh, NT) * scale
            p = jnp.exp(s - jnp.max(s, axis=1, keepdims=True))
            p = p / jnp.sum(p, axis=1, keepdims=True)
            dp = _dot(doh, vh, NT)
            ds = p * (dp - jnp.sum(p * dp, axis=1, keepdims=True))
            dsb = ds.astype(BF16)
            dq_ref[:, sl] = (_dot(dsb, kh, NN) * scale).astype(BF16)
            dkv_ref[:, sl] += _dot(dsb, qh, TN) * scale
            dkv_ref[:, vsl] += _dot(p.astype(BF16), doh, TN)

    row = pl.BlockSpec((tq, D_MODEL), lambda i: (i, 0))
    whole = pl.BlockSpec((mlen, 2 * D_MODEL), lambda i: (0, 0))
    return pl.pallas_call(
        body, name=name, grid=(t // tq,),
        out_shape=(jax.ShapeDtypeStruct((t, D_MODEL), BF16), jax.ShapeDtypeStruct((mlen, 2 * D_MODEL), F32)),
        in_specs=[row, whole, row], out_specs=(row, whole),
        compiler_params=_params(("arbitrary",), 6 * _nbytes((tq, D_MODEL), F32)),
    )(q, kv, do)


def _position():
    return lax.axis_index("x"), lax.axis_index("y"), lax.axis_index("c")


N_PEER = N_DEV - 1


def _all_gather(xs, name):
    n = len(xs)

    def body(*refs):
        x_refs, out_refs = refs[:n], refs[n:2 * n]
        send_sems, recv_sems, local_sems = refs[2 * n:]
        mx, my, mc = _position()
        me, sib = (mx, my, mc), (mx, my, 1 - mc)
        chips = [(1 - mx, my), (mx, 1 - my), (1 - mx, 1 - my)]

        def slot(i, p):
            return out_refs[i].at[4 * p[0] + 2 * p[1] + p[2]]

        def copy(i, k, block, to, src=None):
            return pltpu.make_async_remote_copy(
                src_ref=slot(i, block) if src is None else src, dst_ref=slot(i, block),
                send_sem=send_sems.at[i * N_PEER + k], recv_sem=recv_sems.at[i * N_PEER + k],
                device_id=to, device_id_type=MESH)

        mine = [pltpu.make_async_copy(x_refs[i], slot(i, me), local_sems.at[i]) for i in range(n)]
        for cp in mine:
            cp.start()
        first = [copy(i, 1 + j, me, (*chip, mc), src=x_refs[i]) for j, chip in enumerate(chips) for i in range(n)]
        first += [copy(i, 0, me, sib, src=x_refs[i]) for i in range(n)]
        for cp in first:
            cp.start()
        passed = []
        for j, chip in enumerate(chips):
            for i in range(n):
                copy(i, 1 + j, (*chip, mc), me).wait_recv()
                passed.append(copy(i, 4 + j, (*chip, mc), sib))
                passed[-1].start()
        for i in range(n):
            copy(i, 0, sib, me).wait_recv()
        for j, chip in enumerate(chips):
            for i in range(n):
                copy(i, 4 + j, (*chip, 1 - mc), me).wait_recv()
        for cp in first + passed:
            cp.wait_send()
        for cp in mine:
            cp.wait()

    return pl.pallas_call(
        body, name=name,
        out_shape=tuple(jax.ShapeDtypeStruct((N_DEV,) + x.shape, x.dtype) for x in xs),
        in_specs=[pl.BlockSpec(memory_space=pl.ANY)] * n, out_specs=(pl.BlockSpec(memory_space=pl.ANY),) * n,
        scratch_shapes=[pltpu.SemaphoreType.DMA((n * N_PEER,)), pltpu.SemaphoreType.DMA((n * N_PEER,)),
                        pltpu.SemaphoreType.DMA((n,))],
    )(*xs)


def _exchange(parts, name):
    n = len(parts)
    rels = [(dx, dy, dc) for dx in (0, 1) for dy in (0, 1) for dc in (0, 1)][1:]

    def body(*refs):
        g_refs, out_refs = refs[:n], refs[n:2 * n]
        send_sems, recv_sems, local_sems = refs[2 * n:]
        mx, my, mc = _position()
        me_idx = 4 * mx + 2 * my + mc

        def peer(rel):
            return tuple((1 - v) if f else v for f, v in zip(rel, (mx, my, mc)))

        def copy(i, k, p, arriving):
            p_idx = 4 * p[0] + 2 * p[1] + p[2]
            src, dst = (me_idx, p_idx) if arriving else (p_idx, me_idx)
            return pltpu.make_async_remote_copy(
                src_ref=g_refs[i].at[src], dst_ref=out_refs[i].at[dst],
                send_sem=send_sems.at[i * N_PEER + k], recv_sem=recv_sems.at[i * N_PEER + k],
                device_id=p, device_id_type=MESH)

        mine = [pltpu.make_async_copy(g_refs[i].at[me_idx], out_refs[i].at[me_idx], local_sems.at[i]) for i in range(n)]
        for cp in mine:
            cp.start()
        sends = [copy(i, k, peer(rel), False) for i in range(n) for k, rel in enumerate(rels)]
        for cp in sends:
            cp.start()
        for i in range(n):
            for k, rel in enumerate(rels):
                copy(i, k, peer(rel), True).wait_recv()
        for cp in sends:
            cp.wait_send()
        for cp in mine:
            cp.wait()

    return pl.pallas_call(
        body, name=name,
        out_shape=tuple(jax.ShapeDtypeStruct(p.shape, p.dtype) for p in parts),
        in_specs=[pl.BlockSpec(memory_space=pl.ANY)] * n, out_specs=(pl.BlockSpec(memory_space=pl.ANY),) * n,
        scratch_shapes=[pltpu.SemaphoreType.DMA((n * N_PEER,)), pltpu.SemaphoreType.DMA((n * N_PEER,)),
                        pltpu.SemaphoreType.DMA((n,))],
    )(*parts)


ADAMW_BLOCK_BYTES = 1 << 20


def _adamw(parts, w, m, v, name):
    r, l = w.shape
    tr = _pick(r, max(ADAMW_BLOCK_BYTES // (4 * l), BF16_SUBLANE), BF16_SUBLANE)
    c1 = 1.0 - ADAM_B1 ** ADAM_STEP
    c2 = 1.0 - ADAM_B2 ** ADAM_STEP

    def body(p_ref, w_ref, m_ref, v_ref, g_ref, d_ref, nm_ref, nv_ref):
        g = p_ref[0].astype(F32)
        for s in range(1, N_DEV):
            g = g + p_ref[s].astype(F32)
        nm = ADAM_B1 * m_ref[...] + (1.0 - ADAM_B1) * g
        nv = ADAM_B2 * v_ref[...] + (1.0 - ADAM_B2) * (g * g)
        m_hat = nm / c1
        v_hat = nv / c2
        g_ref[...] = g
        d_ref[...] = -ADAM_LR * (m_hat / (jnp.sqrt(v_hat) + ADAM_EPS) + ADAM_WD * w_ref[...])
        nm_ref[...] = nm
        nv_ref[...] = nv

    row = pl.BlockSpec((tr, l), lambda i: (i, 0))
    return pl.pallas_call(
        body, name=name, grid=(r // tr,),
        out_shape=(jax.ShapeDtypeStruct((r, l), F32),) * 4,
        in_specs=[pl.BlockSpec((N_DEV, tr, l), lambda i: (0, i, 0)), row, row, row],
        out_specs=(row,) * 4,
        compiler_params=_params(("parallel",), 12 * _nbytes((tr, l), F32)),
    )(parts, w, m, v)


MATRIX_WEIGHTS = (("w_in", 2), ("conv_w", 2), ("w_branch", 3), ("w_mix_out", 1), ("w_xq", 1), ("w_xkv", 2),
                  ("w_xo", 1), ("w_ffn_gate", 2), ("w_ffn_up", 2), ("w_ffn_down", 1))
SMALL_PARAMS = ("mix_norm_g", "xattn_norm_g", "mem_norm_g", "ffn_norm_g", "final_norm_g", "forget_bias", "sink",
                "rel_bias")


def _pack_small(pieces):
    flat = jnp.concatenate([p.astype(F32).reshape(-1) for p in pieces])
    total = -(-flat.shape[0] // (8 * LANE)) * (8 * LANE)
    return jnp.pad(flat, (0, total - flat.shape[0])).reshape(total // LANE, LANE)


def _rows(a):
    return a.reshape(-1, a.shape[-1])


def _to_full(gathered, axis):
    moved = jnp.moveaxis(gathered, 0, axis)
    shape = list(moved.shape)
    shape[axis:axis + 2] = [shape[axis] * shape[axis + 1]]
    return moved.reshape(shape)


def _to_blocks(full, axis):
    shape = list(full.shape)
    shape[axis:axis + 1] = [N_DEV, shape[axis] // N_DEV]
    return jnp.moveaxis(full.reshape(shape), axis, 0)


def _perm_in(w_in):
    pad = jnp.zeros((w_in.shape[0], PROJ_COLS - IN_COLS), w_in.dtype)
    return jnp.concatenate([w_in[:, 3848:6920], w_in[:, 0:3072], w_in[:, 3080:3848], w_in[:, 3072:3080], pad], axis=1)


def _unperm_in(dw):
    return jnp.concatenate([dw[:, 3072:6144], dw[:, 6912:6920], dw[:, 6144:6912], dw[:, 0:3072]], axis=1)


def _layer_fwd(l, x, mem, wt, sm):
    t = x.shape[0]
    tag = f"l{l}_"
    h = _rms_fwd(x, sm["mix_norm_g"][l], tag + "mix_norm")
    proj = _matmul(h, wt["w_in"][l], "nn", F32, tag + "in_proj")
    y_conv = _conv_fwd(proj, wt["conv_w"][l], tag + "conv")
    fbias_row = jnp.pad(sm["forget_bias"][l], (0, LANE - 8)).reshape(1, LANE)
    c = _logf_cumsum(proj, fbias_row, tag + "logf_cumsum")
    c8 = c[:, :8].T
    c_col = jnp.broadcast_to(c8[:, :, None], (8, t, LANE))
    c_row = c8.reshape(8, 1, t)
    y_fox, lse_fox = _fox_fwd(proj, c_col, c_row, tag + "fox")
    onehot, bias = _swa_tables(sm["rel_bias"])
    sink_rep = jnp.broadcast_to(sm["sink"][l].reshape(2, SWA_GROUP, 1), (2, SWA_GROUP, LANE))
    y_swa, lse_swa = _swa_fwd(proj, bias, sink_rep, tag + "swa")
    ys = (y_conv, y_fox, y_swa)
    pb = tuple(_matmul(ys[b], wt["w_branch"][l, b], "nn", F32, tag + f"branch{b}") for b in range(3))
    merged = _gate_fwd(proj, pb, tag + "gate")
    x1 = _matmul(merged, wt["w_mix_out"][l], "nn", F32, tag + "mix_out", residual=x)
    xn2 = _rms_fwd(x1, sm["xattn_norm_g"][l], tag + "xattn_norm")
    q = _matmul(xn2, wt["w_xq"][l], "nn", BF16, tag + "xq")
    mem_n = _rms_fwd(mem, sm["mem_norm_g"][l], tag + "mem_norm")
    kv = _matmul(mem_n, wt["w_xkv"][l], "nn", BF16, tag + "xkv")
    o = _xattn_fwd(q, kv, tag + "xattn")
    x2 = _matmul(o, wt["w_xo"][l], "nn", F32, tag + "xo", residual=x1)
    xn3 = _rms_fwd(x2, sm["ffn_norm_g"][l], tag + "ffn_norm")
    ab = _matmul(xn3, wt["w_gu"][l], "nn", F32, tag + "ffn_gu")
    h1 = _swiglu_fwd(ab, tag + "swiglu")
    x3 = _matmul(h1, wt["w_ffn_down"][l], "nn", F32, tag + "ffn_down", residual=x2)
    saved = dict(x=x, h=h, proj=proj, fbias_row=fbias_row, c_col=c_col, c_row=c_row, ys=ys, lse_fox=lse_fox,
                 onehot=onehot, bias=bias, sink_rep=sink_rep, lse_swa=lse_swa, pb=pb, merged=merged, x1=x1,
                 xn2=xn2, q=q, mem_n=mem_n, kv=kv, o=o, x2=x2, xn3=xn3, ab=ab, h1=h1)
    return x3, saved


def _layer_bwd(l, dx3, dx3_b, mem, wt, sm, sv):
    t = dx3.shape[0]
    nb = t // SWA_BLOCK
    tag = f"l{l}_b_"
    gw, gs = {}, {}
    dh1 = _matmul(dx3_b, wt["w_ffn_down"][l], "nt", F32, tag + "d_h1")
    gw["w_ffn_down"] = _matmul(sv["h1"], dx3_b, "tn", F32, tag + "dw_down")
    dab = _swiglu_bwd(sv["ab"], dh1, tag + "swiglu")
    dxn3 = _matmul(dab, wt["w_gu"][l], "nt", F32, tag + "d_xn3")
    dw_gu = _matmul(sv["xn3"], dab, "tn", F32, tag + "dw_gu")
    gw["w_ffn_gate"], gw["w_ffn_up"] = dw_gu[:, :D_FF], dw_gu[:, D_FF:]
    dx2, dx2_b, gs["ffn_norm_g"] = _rms_bwd(sv["x2"], sm["ffn_norm_g"][l], dxn3, dx3, tag + "ffn_norm")
    do = _matmul(dx2_b, wt["w_xo"][l], "nt", BF16, tag + "d_o")
    gw["w_xo"] = _matmul(sv["o"], dx2_b, "tn", F32, tag + "dw_xo")
    dq, dkv = _xattn_bwd(sv["q"], sv["kv"], do, tag + "xattn")
    gw["w_xkv"] = _matmul(sv["mem_n"], dkv, "tn", F32, tag + "dw_xkv")
    dmem_n = _matmul(dkv, wt["w_xkv"][l], "nt", F32, tag + "d_memn")
    _, _, gs["mem_norm_g"] = _rms_bwd(mem, sm["mem_norm_g"][l], dmem_n, None, tag + "mem_norm")
    gw["w_xq"] = _matmul(sv["xn2"], dq, "tn", F32, tag + "dw_xq")
    dxn2 = _matmul(dq, wt["w_xq"][l], "nt", F32, tag + "d_xn2")
    dx1, dx1_b, gs["xattn_norm_g"] = _rms_bwd(sv["x1"], sm["xattn_norm_g"][l], dxn2, dx2, tag + "xattn_norm")
    dmerged = _matmul(dx1_b, wt["w_mix_out"][l], "nt", F32, tag + "d_merged")
    gw["w_mix_out"] = _matmul(sv["merged"], dx1_b, "tn", F32, tag + "dw_mix_out")
    dp0, dp1, dp2, dgate = _gate_bwd(sv["proj"], sv["pb"], dmerged, tag + "gate")
    dps = (dp0, dp1, dp2)
    dy_dtypes = (F32, BF16, BF16)
    dys = [_matmul(dps[b], wt["w_branch"][l, b], "nt", dy_dtypes[b], tag + f"d_y{b}") for b in range(3)]
    gw["w_branch"] = jnp.stack(
        [_matmul(sv["ys"][b], dps[b], "tn", F32, tag + f"dw_branch{b}") for b in range(3)])
    dsq, dkp, dvp, dbias, dsink = _swa_bwd(sv["proj"], sv["bias"], sv["sink_rep"], sv["lse_swa"], sv["ys"][2],
                                           dys[2], tag + "swa")

    def band_add(part):
        tot = part[0] + part[1]
        cur = tot[:, SWA_BLOCK:, :]
        nxt = jnp.concatenate([tot[1:, :SWA_BLOCK, :], jnp.zeros((1, SWA_BLOCK, LANE), F32)], axis=0)
        return (cur + nxt).reshape(t, LANE).astype(BF16)

    dsk, dsv = band_add(dkp), band_add(dvp)
    gs["rel_bias_l"] = jnp.einsum("hts,tsb->bh", dbias.reshape(8, SWA_BLOCK, 2 * SWA_BLOCK), sv["onehot"],
                                  precision=lax.Precision.HIGHEST)
    gs["sink"] = dsink[:, :, 0].reshape(8)
    lse_row = sv["lse_fox"][:, :, 0].reshape(8, 1, t)
    dfq, d_fox, dcq = _fox_dq(sv["proj"], sv["c_col"], sv["c_row"], sv["lse_fox"], sv["ys"][1], dys[1], tag + "fox_dq")
    d_row = d_fox[:, :, 0].reshape(8, 1, t)
    dfk, dfv, dck = _fox_dkv(sv["proj"], sv["c_col"], sv["c_row"], lse_row, d_row, dys[1], tag + "fox_dkv")
    dfg, dfb = _logf_cumsum_bwd(sv["proj"], sv["fbias_row"], dcq, dck, tag + "logf_cumsum")
    gs["forget_bias"] = dfb[0, :8]
    dcb, dcc, dcu, dconv = _conv_bwd(sv["proj"], wt["conv_w"][l], dys[0], tag + "conv")
    gw["conv_w"] = dconv[:3]
    dproj = jnp.concatenate([dgate, dcb, dcc, dcu, dfq, dfk, dfv, dsq, dsk, dsv, dfg], axis=1)
    dh = _matmul(dproj, wt["w_in"][l], "nt", F32, tag + "d_h")
    gw["w_in"] = _unperm_in(_matmul(sv["h"], dproj, "tn", F32, tag + "dw_in"))
    dx, dx_b, gs["mix_norm_g"] = _rms_bwd(sv["x"], sm["mix_norm_g"][l], dh, dx1, tag + "mix_norm")
    return dx, dx_b, gw, gs


def kernel(x, mem, mix_norm_g, w_in, forget_bias, conv_w, sink, w_branch, w_mix_out, rel_bias, xattn_norm_g, mem_norm_g, w_xq, w_xkv, w_xo, ffn_norm_g, w_ffn_gate, w_ffn_up, w_ffn_down, final_norm_g, loss_target, m_mix_norm_g, m_w_in, m_forget_bias, m_conv_w, m_sink, m_w_branch, m_w_mix_out, m_rel_bias, m_xattn_norm_g, m_mem_norm_g, m_w_xq, m_w_xkv, m_w_xo, m_ffn_norm_g, m_w_ffn_gate, m_w_ffn_up, m_w_ffn_down, m_final_norm_g, v_mix_norm_g, v_w_in, v_forget_bias, v_conv_w, v_sink, v_w_branch, v_w_mix_out, v_rel_bias, v_xattn_norm_g, v_mem_norm_g, v_w_xq, v_w_xkv, v_w_xo, v_ffn_norm_g, v_w_ffn_gate, v_w_ffn_up, v_w_ffn_down, v_final_norm_g):
    args = dict(locals())
    names = [n for n, _ in MATRIX_WEIGHTS] + list(SMALL_PARAMS)
    w = {n: args[n] for n in names}
    mo = {n: args["m_" + n] for n in names}
    vo = {n: args["v_" + n] for n in names}
    x2d, mem2d, tgt = x[0], mem[0], loss_target[0]

    wire = {n: (F32 if n == "conv_w" else BF16) for n, _ in MATRIX_WEIGHTS}
    gathered = _all_gather([w[n].astype(wire[n]) for n, _ in MATRIX_WEIGHTS], "weights_all_gather")
    wt = {n: _to_full(g, ax) for (n, ax), g in zip(MATRIX_WEIGHTS, gathered)}
    wt["w_in"] = jnp.stack([_perm_in(wt["w_in"][l]) for l in range(DEPTH)])
    wt["w_gu"] = jnp.concatenate([wt["w_ffn_gate"], wt["w_ffn_up"]], axis=2)
    sm = {n: w[n] for n in SMALL_PARAMS}

    saved = []
    xc = x2d
    for l in range(DEPTH):
        xc, sv = _layer_fwd(l, xc, mem2d, wt, sm)
        saved.append(sv)
    loss_row, dx, dx_b, dg_final = _loss_head(xc, sm["final_norm_g"], tgt, "loss_head")
    loss = lax.psum(loss_row[0, 0], ("x", "y", "c"))
    gw_all, gs_all = [None] * DEPTH, [None] * DEPTH
    for l in reversed(range(DEPTH)):
        dx, dx_b, gw_all[l], gs_all[l] = _layer_bwd(l, dx, dx_b, mem2d, wt, sm, saved[l])
    grad_x = dx[None]

    gparts = []
    for n, ax in MATRIX_WEIGHTS:
        full = jnp.stack([gw_all[l][n] for l in range(DEPTH)])
        gparts.append(_to_blocks(full, ax).astype(wire[n]))
    recv = _exchange(gparts, "grads_exchange")
    outs = {}
    for (n, _), r in zip(MATRIX_WEIGHTS, recv):
        res = _adamw(r.reshape((N_DEV,) + _rows(w[n]).shape), _rows(w[n]), _rows(mo[n]), _rows(vo[n]), "adamw_" + n)
        outs[n] = [o.reshape(w[n].shape) for o in res]

    gsm = {n: jnp.stack([gs_all[l][n] for l in range(DEPTH)])
           for n in ("mix_norm_g", "xattn_norm_g", "mem_norm_g", "ffn_norm_g", "forget_bias", "sink")}
    gsm["final_norm_g"] = dg_final
    gsm["rel_bias"] = gs_all[0]["rel_bias_l"] + gs_all[1]["rel_bias_l"]
    (small_parts,) = _all_gather([_pack_small([gsm[n] for n in SMALL_PARAMS])], "small_grads_all_gather")
    outs_small = _adamw(small_parts, *[_pack_small([d[n] for n in SMALL_PARAMS]) for d in (w, mo, vo)], "adamw_small")
    for kind in range(4):
        flat, o = outs_small[kind].reshape(-1), 0
        for n in SMALL_PARAMS:
            sz = int(np.prod(w[n].shape))
            outs.setdefault(n, []).append(flat[o:o + sz].reshape(w[n].shape))
            o += sz

    order = ["mix_norm_g", "w_in", "forget_bias", "conv_w", "sink", "w_branch", "w_mix_out", "rel_bias",
             "xattn_norm_g", "mem_norm_g", "w_xq", "w_xkv", "w_xo", "ffn_norm_g", "w_ffn_gate", "w_ffn_up",
             "w_ffn_down", "final_norm_g"]
    result = [loss, grad_x]
    for kind in range(4):
        result += [outs[n][kind] for n in order]
    return tuple(result)
```

```python
import math

import numpy as np
import jax
import jax.numpy as jnp
from jax import lax
from jax.experimental import pallas as pl
from jax.experimental.pallas import tpu as pltpu

F32 = jnp.float32
BF16 = jnp.bfloat16
MESH = pl.DeviceIdType.MESH

LANE = 128
BF16_SUBLANE = 16
V7X_VMEM_REQUEST_CAP = 56 * 2 ** 20
N_DEV = 8

D_MODEL = 1024
DEPTH = 2
HEAD = 64
BRANCH = 512
SWA_BLOCK = 128
SWA_GROUP = 4
N_BUCKETS = 32
X_HEADS = 4
X_HEAD = 256
D_FF = 2816
RMS_EPS = 1e-6
NEG = -1e30
ADAM_LR, ADAM_B1, ADAM_B2, ADAM_EPS, ADAM_WD, ADAM_STEP = 0.001, 0.9, 0.999, 1e-08, 0.01, 10

IN_COLS = 6920
PROJ_COLS = 7040
COL_GATE, COL_CONV, COL_FOX, COL_SQ, COL_SK, COL_SV, COL_FG = 0, 3072, 4608, 6144, 6656, 6784, 6912

ROW_TILE = 512
FOX_TILE = 512
MM_TM, MM_TN, MM_TK = 1024, 1536, 1024


def _pick(n, cap, mult):
    best = None
    for d in range(mult, min(n, cap) + 1, mult):
        if n % d == 0:
            best = d
    return n if best is None else best


def _params(semantics, block_bytes):
    limit = int(min(max(2 * block_bytes + (8 << 20), 24 << 20), V7X_VMEM_REQUEST_CAP))
    return pltpu.CompilerParams(dimension_semantics=semantics, vmem_limit_bytes=limit)


def _nbytes(shape, dtype):
    return int(np.prod(shape)) * jnp.dtype(dtype).itemsize


def _dot(a, b, dims):
    return lax.dot_general(a, b, (dims, ((), ())), preferred_element_type=F32)


NN = ((1,), (0,))
NT = ((1,), (1,))
TN = ((0,), (0,))


def _matmul(a, b, mode, out_dtype, name, residual=None):
    if mode == "nn":
        (m, k), (k2, n) = a.shape, b.shape
    elif mode == "nt":
        (m, k), (n, k2) = a.shape, b.shape
    else:
        (k, m), (k2, n) = a.shape, b.shape
    assert k == k2, (name, a.shape, b.shape)
    tm, tn, tk = _pick(m, MM_TM, LANE), _pick(n, MM_TN, LANE), _pick(k, MM_TK, LANE)
    nk = k // tk
    dims = {"nn": NN, "nt": NT, "tn": TN}[mode]
    has_res = residual is not None

    def body(*refs):
        a_ref, b_ref = refs[0], refs[1]
        r_ref = refs[2] if has_res else None
        o_ref = refs[3] if has_res else refs[2]
        kk = pl.program_id(2)
        p = _dot(a_ref[...].astype(BF16), b_ref[...].astype(BF16), dims)
        if nk == 1:
            if has_res:
                p = p + r_ref[...]
            o_ref[...] = p.astype(out_dtype)
        else:
            acc_ref = refs[-1]

            @pl.when(kk == 0)
            def _():
                acc_ref[...] = p

            @pl.when(kk > 0)
            def _():
                acc_ref[...] += p

            @pl.when(kk == nk - 1)
            def _():
                res = acc_ref[...]
                if has_res:
                    res = res + r_ref[...]
                o_ref[...] = res.astype(out_dtype)

    if mode == "nn":
        a_spec = pl.BlockSpec((tm, tk), lambda i, j, kk: (i, kk))
        b_spec = pl.BlockSpec((tk, tn), lambda i, j, kk: (kk, j))
    elif mode == "nt":
        a_spec = pl.BlockSpec((tm, tk), lambda i, j, kk: (i, kk))
        b_spec = pl.BlockSpec((tn, tk), lambda i, j, kk: (j, kk))
    else:
        a_spec = pl.BlockSpec((tk, tm), lambda i, j, kk: (kk, i))
        b_spec = pl.BlockSpec((tk, tn), lambda i, j, kk: (kk, j))
    o_spec = pl.BlockSpec((tm, tn), lambda i, j, kk: (i, j))
    in_specs, args = [a_spec, b_spec], [a, b]
    if has_res:
        in_specs.append(o_spec)
        args.append(residual)
    blk = (_nbytes((tm, tk), a.dtype) + _nbytes((tk, tn), b.dtype) + _nbytes((tm, tn), out_dtype)
           + (_nbytes((tm, tn), F32) if has_res else 0))
    scratch = [pltpu.VMEM((tm, tn), F32)] if nk > 1 else []
    return pl.pallas_call(
        body, name=name, grid=(m // tm, n // tn, nk),
        out_shape=jax.ShapeDtypeStruct((m, n), out_dtype),
        in_specs=in_specs, out_specs=o_spec, scratch_shapes=scratch,
        compiler_params=_params(("parallel", "parallel", "arbitrary"), blk + _nbytes((tm, tn), F32)),
    )(*args)


def _rms_fwd(x, g, name):
    t, d = x.shape
    tr = _pick(t, ROW_TILE, BF16_SUBLANE)

    def body(x_ref, g_ref, y_ref):
        xv = x_ref[...]
        r = lax.rsqrt(jnp.mean(xv * xv, axis=-1, keepdims=True) + RMS_EPS)
        y_ref[...] = ((xv * r) * g_ref[...]).astype(BF16)

    return pl.pallas_call(
        body, name=name, grid=(t // tr,),
        out_shape=jax.ShapeDtypeStruct((t, d), BF16),
        in_specs=[pl.BlockSpec((tr, d), lambda i: (i, 0)), pl.BlockSpec((1, d), lambda i: (0, 0))],
        out_specs=pl.BlockSpec((tr, d), lambda i: (i, 0)),
        compiler_params=_params(("parallel",), 2 * _nbytes((tr, d), F32)),
    )(x, g.reshape(1, d))


def _rms_bwd(x, g, dy, dres, name):
    t, d = x.shape
    tr = _pick(t, ROW_TILE, BF16_SUBLANE)
    has_res = dres is not None

    def body(*refs):
        x_ref, g_ref, dy_ref = refs[:3]
        r_ref = refs[3] if has_res else None
        dx_ref, dxb_ref, dg_ref = refs[-3:]
        i = pl.program_id(0)
        xv = x_ref[...]
        r = lax.rsqrt(jnp.mean(xv * xv, axis=-1, keepdims=True) + RMS_EPS)
        xh = xv * r
        dyv = dy_ref[...].astype(F32)
        dxh = dyv * g_ref[...]
        dx = r * (dxh - xh * jnp.mean(dxh * xh, axis=-1, keepdims=True))
        if has_res:
            dx = dx + r_ref[...]
        dx_ref[...] = dx
        dxb_ref[...] = dx.astype(BF16)

        @pl.when(i == 0)
        def _():
            dg_ref[...] = jnp.zeros_like(dg_ref)

        dg_ref[...] += jnp.sum(dyv * xh, axis=0, keepdims=True)

    row = pl.BlockSpec((tr, d), lambda i: (i, 0))
    vec = pl.BlockSpec((1, d), lambda i: (0, 0))
    in_specs, args = [row, vec, row], [x, g.reshape(1, d), dy]
    if has_res:
        in_specs.append(row)
        args.append(dres)
    return pl.pallas_call(
        body, name=name, grid=(t // tr,),
        out_shape=(jax.ShapeDtypeStruct((t, d), F32), jax.ShapeDtypeStruct((t, d), BF16),
                   jax.ShapeDtypeStruct((1, d), F32)),
        in_specs=in_specs, out_specs=(row, row, vec),
        compiler_params=_params(("arbitrary",), 5 * _nbytes((tr, d), F32)),
    )(*args)


def _loss_head(x, g, target, name):
    t, d = x.shape
    tr = _pick(t, ROW_TILE, BF16_SUBLANE)

    def body(x_ref, g_ref, t_ref, loss_ref, dx_ref, dxb_ref, dg_ref):
        i = pl.program_id(0)
        xv = x_ref[...]
        gv = g_ref[...]
        r = lax.rsqrt(jnp.mean(xv * xv, axis=-1, keepdims=True) + RMS_EPS)
        xh = xv * r
        diff = xh * gv - t_ref[...]
        part = 0.5 * jnp.sum(jnp.mean(diff * diff, axis=-1, keepdims=True), axis=0, keepdims=True)
        dyv = diff * (1.0 / d)
        dxh = dyv * gv
        dx = r * (dxh - xh * jnp.mean(dxh * xh, axis=-1, keepdims=True))
        dx_ref[...] = dx
        dxb_ref[...] = dx.astype(BF16)

        @pl.when(i == 0)
        def _():
            dg_ref[...] = jnp.zeros_like(dg_ref)
            loss_ref[...] = jnp.zeros_like(loss_ref)

        dg_ref[...] += jnp.sum(dyv * xh, axis=0, keepdims=True)
        loss_ref[...] += jnp.broadcast_to(part, loss_ref.shape)

    row = pl.BlockSpec((tr, d), lambda i: (i, 0))
    vec = pl.BlockSpec((1, d), lambda i: (0, 0))
    return pl.pallas_call(
        body, name=name, grid=(t // tr,),
        out_shape=(jax.ShapeDtypeStruct((1, LANE), F32), jax.ShapeDtypeStruct((t, d), F32),
                   jax.ShapeDtypeStruct((t, d), BF16), jax.ShapeDtypeStruct((1, d), F32)),
        in_specs=[row, vec, row],
        out_specs=(pl.BlockSpec((1, LANE), lambda i: (0, 0)), row, row, vec),
        compiler_params=_params(("arbitrary",), 5 * _nbytes((tr, d), F32)),
    )(x, g.reshape(1, d), target)


HALO = 8


def _conv_fwd(proj, conv_w, name):
    t = proj.shape[0]
    tr = _pick(t, ROW_TILE, BF16_SUBLANE)
    c0 = COL_CONV // BRANCH
    hb = tr // HALO

    def body(cb_ref, cc_ref, cu_ref, hc_ref, hu_ref, w_ref, y_ref):
        i = pl.program_id(0)
        z = cc_ref[...] * cu_ref[...]
        hz = jnp.where(i > 0, hc_ref[...] * hu_ref[...], 0.0)
        zf = jnp.concatenate([hz, z], axis=0)
        z1 = pltpu.roll(zf, 1, 0)[HALO:]
        z2 = pltpu.roll(zf, 2, 0)[HALO:]
        y = w_ref[2:3, :] * z + w_ref[1:2, :] * z1 + w_ref[0:1, :] * z2
        y_ref[...] = (cb_ref[...] * y).astype(BF16)

    def col(c):
        return pl.BlockSpec((tr, BRANCH), lambda i, c=c: (i, c0 + c))

    def prev(c):
        return pl.BlockSpec((HALO, BRANCH), lambda i, c=c: (jnp.maximum(i * hb - 1, 0), c0 + c))

    return pl.pallas_call(
        body, name=name, grid=(t // tr,),
        out_shape=jax.ShapeDtypeStruct((t, BRANCH), BF16),
        in_specs=[col(0), col(1), col(2), prev(1), prev(2), pl.BlockSpec((3, BRANCH), lambda i: (0, 0))],
        out_specs=pl.BlockSpec((tr, BRANCH), lambda i: (i, 0)),
        compiler_params=_params(("parallel",), 6 * _nbytes((tr, BRANCH), F32)),
    )(proj, proj, proj, proj, proj, conv_w)


def _conv_bwd(proj, conv_w, dout, name):
    t = proj.shape[0]
    tr = _pick(t, ROW_TILE, BF16_SUBLANE)
    nblk = t // tr
    c0 = COL_CONV // BRANCH
    hb = tr // HALO
    last_halo = t // HALO - 1

    def body(cb_ref, cc_ref, cu_ref, hc_ref, hu_ref, do_ref, ndo_ref, ncb_ref, w_ref,
             dcb_ref, dcc_ref, dcu_ref, dw_ref):
        i = pl.program_id(0)
        cb, cc, cu = cb_ref[...], cc_ref[...], cu_ref[...]
        w0, w1, w2 = w_ref[0:1, :], w_ref[1:2, :], w_ref[2:3, :]
        z = cc * cu
        hz = jnp.where(i > 0, hc_ref[...] * hu_ref[...], 0.0)
        zf = jnp.concatenate([hz, z], axis=0)
        z1 = pltpu.roll(zf, 1, 0)[HALO:]
        z2 = pltpu.roll(zf, 2, 0)[HALO:]
        y = w2 * z + w1 * z1 + w0 * z2
        dout_v = do_ref[...]
        dyc = dout_v * cb
        hdy = jnp.where(i < nblk - 1, ndo_ref[...] * ncb_ref[...], 0.0)
        dyf = jnp.concatenate([dyc, hdy], axis=0)
        dy1 = pltpu.roll(dyf, tr + HALO - 1, 0)[:tr]
        dy2 = pltpu.roll(dyf, tr + HALO - 2, 0)[:tr]
        dz = w2 * dyc + w1 * dy1 + w0 * dy2
        dcb_ref[...] = (dout_v * y).astype(BF16)
        dcc_ref[...] = (dz * cu).astype(BF16)
        dcu_ref[...] = (dz * cc).astype(BF16)

        @pl.when(i == 0)
        def _():
            dw_ref[...] = jnp.zeros_like(dw_ref)

        dw_ref[0:1, :] += jnp.sum(dyc * z2, axis=0, keepdims=True)
        dw_ref[1:2, :] += jnp.sum(dyc * z1, axis=0, keepdims=True)
        dw_ref[2:3, :] += jnp.sum(dyc * z, axis=0, keepdims=True)

    def col(c):
        return pl.BlockSpec((tr, BRANCH), lambda i, c=c: (i, c0 + c))

    def prev(c):
        return pl.BlockSpec((HALO, BRANCH), lambda i, c=c: (jnp.maximum(i * hb - 1, 0), c0 + c))

    def nxt(c):
        return pl.BlockSpec((HALO, BRANCH), lambda i, c=c: (jnp.minimum((i + 1) * hb, last_halo), c))

    row = pl.BlockSpec((tr, BRANCH), lambda i: (i, 0))
    return pl.pallas_call(
        body, name=name, grid=(nblk,),
        out_shape=(jax.ShapeDtypeStruct((t, BRANCH), BF16),) * 3 + (jax.ShapeDtypeStruct((HALO, BRANCH), F32),),
        in_specs=[col(0), col(1), col(2), prev(1), prev(2), row, nxt(0), nxt(c0),
                  pl.BlockSpec((3, BRANCH), lambda i: (0, 0))],
        out_specs=(row, row, row, pl.BlockSpec((HALO, BRANCH), lambda i: (0, 0))),
        compiler_params=_params(("arbitrary",), 8 * _nbytes((tr, BRANCH), F32)),
    )(proj, proj, proj, proj, proj, dout, dout, proj, conv_w)


def _tri(lower):
    r = lax.broadcasted_iota(jnp.int32, (LANE, LANE), 0)
    c = lax.broadcasted_iota(jnp.int32, (LANE, LANE), 1)
    return jnp.where((c <= r) if lower else (c >= r), 1.0, 0.0).astype(F32)


def _logf_cumsum(proj, fbias_row, name):
    t = proj.shape[0]
    nchunk = t // LANE

    def body(f_ref, b_ref, c_ref, run_sc):
        tri = _tri(True)
        run_sc[...] = jnp.zeros_like(run_sc)

        @pl.loop(0, nchunk)
        def _(i):
            rows = pl.ds(pl.multiple_of(i * LANE, LANE), LANE)
            z = f_ref[rows, :] + b_ref[...]
            logf = jnp.minimum(z, 0.0) - jnp.log(1.0 + jnp.exp(-jnp.abs(z)))
            cs = lax.dot_general(tri, logf, (NN, ((), ())), precision=lax.Precision.HIGHEST,
                                 preferred_element_type=F32) + run_sc[0:1, :]
            c_ref[rows, :] = cs
            run_sc[0:1, :] = cs[LANE - 1:LANE, :]

    return pl.pallas_call(
        body, name=name, grid=(1,),
        out_shape=jax.ShapeDtypeStruct((t, LANE), F32),
        in_specs=[pl.BlockSpec((t, LANE), lambda i: (0, COL_FG // LANE)), pl.BlockSpec((1, LANE), lambda i: (0, 0))],
        out_specs=pl.BlockSpec((t, LANE), lambda i: (0, 0)),
        scratch_shapes=[pltpu.VMEM((8, LANE), F32)],
        compiler_params=_params(("arbitrary",), 2 * _nbytes((t, LANE), F32)),
    )(proj, fbias_row)


def _logf_cumsum_bwd(proj, fbias_row, pieces, name):
    t = proj.shape[0]
    tb = _pick(t, 2 * ROW_TILE, LANE)
    nblk = t // tb
    npiece = len(pieces)

    def body(*refs):
        f_ref, b_ref = refs[:2]
        piece_refs = refs[2:2 + npiece]
        df_ref, db_ref, run_sc = refs[2 + npiece:]
        i = pl.program_id(0)
        tri = _tri(False)

        @pl.when(i == 0)
        def _():
            run_sc[...] = jnp.zeros_like(run_sc)
            db_ref[...] = jnp.zeros_like(db_ref)

        for c in reversed(range(tb // LANE)):
            rows = slice(c * LANE, (c + 1) * LANE)
            slabs = [p_ref[n, rows, :] for p_ref in piece_refs for n in range(p_ref.shape[0])]
            dcc = slabs[0]
            for slab in slabs[1:]:
                dcc = dcc + slab
            ss = lax.dot_general(tri, dcc, (NN, ((), ())), precision=lax.Precision.HIGHEST,
                                 preferred_element_type=F32) + run_sc[0:1, :]
            z = f_ref[rows, :] + b_ref[...]
            dz = ss * (1.0 / (1.0 + jnp.exp(z)))
            df_ref[rows, :] = dz.astype(BF16)
            run_sc[0:1, :] = ss[0:1, :]
            db_ref[...] += jnp.sum(dz, axis=0, keepdims=True)

    piece_specs = [pl.BlockSpec((p.shape[0], tb, LANE), lambda i: (0, nblk - 1 - i, 0)) for p in pieces]
    nslab = sum(p.shape[0] for p in pieces)
    return pl.pallas_call(
        body, name=name, grid=(nblk,),
        out_shape=(jax.ShapeDtypeStruct((t, LANE), BF16), jax.ShapeDtypeStruct((1, LANE), F32)),
        in_specs=[pl.BlockSpec((tb, LANE), lambda i: (nblk - 1 - i, COL_FG // LANE)),
                  pl.BlockSpec((1, LANE), lambda i: (0, 0))] + piece_specs,
        out_specs=(pl.BlockSpec((tb, LANE), lambda i: (nblk - 1 - i, 0)), pl.BlockSpec((1, LANE), lambda i: (0, 0))),
        scratch_shapes=[pltpu.VMEM((8, LANE), F32)],
        compiler_params=_params(("arbitrary",), (4 + nslab) * _nbytes((tb, LANE), F32)),
    )(proj, fbias_row, *pieces)


def _lo_mask():
    return lax.broadcasted_iota(jnp.int32, (1, LANE), 1) < HEAD


def _causal_steps(n, key_major):
    if key_major:
        pairs = [(iq, ik) for ik in range(n) for iq in range(ik, n)]
    else:
        pairs = [(iq, ik) for iq in range(n) for ik in range(iq + 1)]
    return (jnp.asarray([p[0] for p in pairs], jnp.int32), jnp.asarray([p[1] for p in pairs], jnp.int32))


def _head_lanes(j, pair_vals):
    lane = lax.broadcasted_iota(jnp.int32, (1, LANE), 1)
    return jnp.where(lane == 2 * j, pair_vals[0], 0.0) + jnp.where(lane == 2 * j + 1, pair_vals[1], 0.0)


def _fox_fwd(proj, c_col, c_row, name):
    t = proj.shape[0]
    tq = _pick(t, FOX_TILE, LANE)
    nq = t // tq
    rep = tq // LANE
    scale = HEAD ** -0.5
    cq, ck, cv = COL_FOX // LANE, COL_FOX // LANE + 4, COL_FOX // LANE + 8
    q_tab, k_tab = _causal_steps(nq, False)

    def body(qt_ref, kt_ref, q_ref, k_ref, v_ref, cc_ref, cr_ref, y_ref, lse_ref, m_sc, l_sc, acc_sc):
        step_id = pl.program_id(1)
        iq, ik = qt_ref[step_id], kt_ref[step_id]
        lo = _lo_mask()

        @pl.when(ik == 0)
        def _():
            m_sc[...] = jnp.full(m_sc.shape, NEG, F32)
            l_sc[...] = jnp.zeros_like(l_sc)
            acc_sc[...] = jnp.zeros_like(acc_sc)

        def step(diag):
            q2 = q_ref[...].astype(BF16)
            k2 = k_ref[...].astype(BF16)
            v2 = v_ref[...].astype(BF16)
            for h in range(2):
                msk = lo if h == 0 else jnp.logical_not(lo)
                qh = jnp.where(msk, q2, jnp.zeros_like(q2))
                s = _dot(qh, k2, NT) * scale
                s = s + jnp.tile(cc_ref[h], (1, rep)) - cr_ref[h]
                if diag:
                    row = lax.broadcasted_iota(jnp.int32, (tq, tq), 0)
                    col = lax.broadcasted_iota(jnp.int32, (tq, tq), 1)
                    s = jnp.where(col <= row, s, NEG)
                m_prev = m_sc[h]
                m_new = jnp.maximum(m_prev, jnp.max(s, axis=1, keepdims=True))
                alpha = jnp.exp(m_prev - m_new)
                p = jnp.exp(s - jnp.tile(m_new, (1, rep)))
                l_sc[h] = alpha * l_sc[h] + jnp.sum(p, axis=1, keepdims=True)
                acc_sc[h] = alpha * acc_sc[h] + _dot(p.astype(BF16), v2, NN)
                m_sc[h] = m_new

        @pl.when(ik < iq)
        def _():
            step(False)

        @pl.when(ik == iq)
        def _():
            step(True)
            y = jnp.where(lo, acc_sc[0] / l_sc[0], acc_sc[1] / l_sc[1])
            y_ref[...] = y.astype(BF16)
            lse_ref[...] = m_sc[...] + jnp.log(l_sc[...])

    def kv(c):
        return pl.BlockSpec((tq, LANE), lambda j, s, qt, kt, c=c: (kt[s], c + j))

    grid_spec = pltpu.PrefetchScalarGridSpec(
        num_scalar_prefetch=2, grid=(4, int(q_tab.shape[0])),
        in_specs=[pl.BlockSpec((tq, LANE), lambda j, s, qt, kt: (qt[s], cq + j)), kv(ck), kv(cv),
                  pl.BlockSpec((2, tq, LANE), lambda j, s, qt, kt: (j, qt[s], 0)),
                  pl.BlockSpec((2, 1, tq), lambda j, s, qt, kt: (j, 0, kt[s]))],
        out_specs=(pl.BlockSpec((tq, LANE), lambda j, s, qt, kt: (qt[s], j)),
                   pl.BlockSpec((2, tq, LANE), lambda j, s, qt, kt: (j, qt[s], 0))),
        scratch_shapes=[pltpu.VMEM((2, tq, LANE), F32)] * 3)
    return pl.pallas_call(
        body, name=name, grid_spec=grid_spec,
        out_shape=(jax.ShapeDtypeStruct((t, BRANCH), BF16), jax.ShapeDtypeStruct((8, t, LANE), F32)),
        compiler_params=_params(("parallel", "arbitrary"),
                                16 * _nbytes((tq, LANE), F32) + 6 * _nbytes((tq, tq), F32)),
    )(q_tab, k_tab, proj, proj, proj, c_col, c_row)


def _fox_bwd(proj, c_col, c_row, lse_row, y, dy, name):
    t = proj.shape[0]
    tb = _pick(t, FOX_TILE, LANE)
    nb = t // tb
    rep = tb // LANE
    scale = HEAD ** -0.5
    cq, ck, cv = COL_FOX // LANE, COL_FOX // LANE + 4, COL_FOX // LANE + 8
    q_tab, k_tab = _causal_steps(nb, True)
    nsteps = int(q_tab.shape[0])

    def body(qt_ref, kt_ref, k_ref, v_ref, q_ref, y_ref, dy_ref, ck_ref, cqr_ref, lser_ref,
             dq_ref, dk_ref, dv_ref, dck_ref, dcq_ref, dk_sc, dv_sc, dc_sc, dqt_sc, dcq_sc, d_sc):
        j, step_id = pl.program_id(0), pl.program_id(1)
        iq, ik = qt_ref[step_id], kt_ref[step_id]
        lo = _lo_mask()

        @pl.when(step_id == 0)
        def _():
            dqt_sc[...] = jnp.zeros_like(dqt_sc)
            dcq_sc[...] = jnp.zeros_like(dcq_sc)

        @pl.when(iq == ik)
        def _():
            dk_sc[...] = jnp.zeros_like(dk_sc)
            dv_sc[...] = jnp.zeros_like(dv_sc)
            dc_sc[...] = jnp.zeros_like(dc_sc)

        @pl.when(ik == 0)
        def _():
            prod = y_ref[...].astype(F32) * dy_ref[...].astype(F32)
            row = lax.broadcasted_iota(jnp.int32, (8, LANE), 0)
            sel = jnp.logical_or(jnp.logical_and(row == 0, lo), jnp.logical_and(row == 1, jnp.logical_not(lo)))
            d_sc[iq] = lax.dot_general(jnp.where(sel, 1.0, 0.0).astype(F32), prod, (NT, ((), ())),
                                       precision=lax.Precision.HIGHEST, preferred_element_type=F32)

        def step(diag):
            k2 = k_ref[...].astype(BF16)
            v2 = v_ref[...].astype(BF16)
            q2 = q_ref[...].astype(BF16)
            do2 = dy_ref[...]
            d_rows = d_sc[iq]
            for h in range(2):
                msk = lo if h == 0 else jnp.logical_not(lo)
                kh = jnp.where(msk, k2, jnp.zeros_like(k2))
                vh = jnp.where(msk, v2, jnp.zeros_like(v2))
                st = _dot(kh, q2, NT) * scale
                st = st + cqr_ref[h] - jnp.tile(ck_ref[h], (1, rep))
                if diag:
                    krow = lax.broadcasted_iota(jnp.int32, (tb, tb), 0)
                    qcol = lax.broadcasted_iota(jnp.int32, (tb, tb), 1)
                    st = jnp.where(krow <= qcol, st, NEG)
                pt = jnp.exp(st - lser_ref[h])
                dpt = _dot(vh, do2, NT)
                dst = pt * (dpt - d_rows[h:h + 1, :])
                dsb = dst.astype(BF16)
                dv_sc[h] += _dot(pt.astype(BF16), do2, NN)
                dk_sc[h] += _dot(dsb, q2, NN)
                dc_sc[h] -= jnp.sum(dst, axis=1, keepdims=True)
                dqt_sc[iq] += _dot(kh, dsb, TN)
                dcq_sc[h, iq] += jnp.sum(dst, axis=0, keepdims=True)

        @pl.when(iq > ik)
        def _():
            step(False)

        @pl.when(iq == ik)
        def _():
            step(True)

        @pl.when(iq == nb - 1)
        def _():
            dk_ref[...] = (jnp.where(lo, dk_sc[0], dk_sc[1]) * scale).astype(BF16)
            dv_ref[...] = jnp.where(lo, dv_sc[0], dv_sc[1]).astype(BF16)
            dck_ref[...] = _head_lanes(j, dc_sc)

        @pl.when(step_id == nsteps - 1)
        def _():
            for i in range(nb):
                dq_ref[i * tb:(i + 1) * tb, :] = (dqt_sc[i].T * scale).astype(BF16)
                for h in range(2):
                    dcq_ref[h, :, i * tb:(i + 1) * tb] = dcq_sc[h, i]

    def kcol(c):
        return pl.BlockSpec((tb, LANE), lambda j, s, qt, kt, c=c: (kt[s], c + j))

    qrow = pl.BlockSpec((2, 1, tb), lambda j, s, qt, kt: (j, 0, qt[s]))
    pair_q = pl.BlockSpec((tb, LANE), lambda j, s, qt, kt: (qt[s], j))
    pair_k = pl.BlockSpec((tb, LANE), lambda j, s, qt, kt: (kt[s], j))
    grid_spec = pltpu.PrefetchScalarGridSpec(
        num_scalar_prefetch=2, grid=(4, nsteps),
        in_specs=[kcol(ck), kcol(cv), pl.BlockSpec((tb, LANE), lambda j, s, qt, kt: (qt[s], cq + j)), pair_q, pair_q,
                  pl.BlockSpec((2, tb, LANE), lambda j, s, qt, kt: (j, kt[s], 0)), qrow, qrow],
        out_specs=(pl.BlockSpec((t, LANE), lambda j, s, qt, kt: (0, j)), pair_k, pair_k,
                   pl.BlockSpec((None, tb, LANE), lambda j, s, qt, kt: (j, kt[s], 0)),
                   pl.BlockSpec((2, 1, t), lambda j, s, qt, kt: (j, 0, 0))),
        scratch_shapes=[pltpu.VMEM((2, tb, LANE), F32)] * 3
        + [pltpu.VMEM((nb, LANE, tb), F32), pltpu.VMEM((2, nb, 1, tb), F32), pltpu.VMEM((nb, 8, tb), F32)])
    return pl.pallas_call(
        body, name=name, grid_spec=grid_spec,
        out_shape=(jax.ShapeDtypeStruct((t, BRANCH), BF16), jax.ShapeDtypeStruct((t, BRANCH), BF16),
                   jax.ShapeDtypeStruct((t, BRANCH), BF16), jax.ShapeDtypeStruct((4, t, LANE), F32),
                   jax.ShapeDtypeStruct((8, 1, t), F32)),
        compiler_params=_params(("parallel", "arbitrary"),
                                24 * _nbytes((tb, LANE), F32) + 8 * _nbytes((tb, tb), F32)
                                + 2 * _nbytes((t, LANE), F32)),
    )(q_tab, k_tab, proj, proj, proj, y, dy, c_col, c_row, lse_row)


def _swa_tables(rel_bias):
    tq = np.arange(SWA_BLOCK)[:, None]
    sk = np.arange(2 * SWA_BLOCK)[None, :]
    dist = SWA_BLOCK + tq - sk
    inwin = (dist >= 0) & (dist < SWA_BLOCK)
    n = np.maximum(dist, 0)
    max_exact = N_BUCKETS // 2
    large = max_exact + (np.log(np.maximum(n, 1).astype(np.float32) / max_exact)
                         / math.log(SWA_BLOCK / max_exact) * (N_BUCKETS - max_exact)).astype(np.int32)
    bucket = np.where(n < max_exact, n, np.minimum(large, N_BUCKETS - 1))
    onehot = (bucket[..., None] == np.arange(N_BUCKETS)) & inwin[..., None]
    onehot = jnp.asarray(onehot.astype(np.float32))
    bias = jnp.einsum("tsb,bh->hts", onehot, rel_bias, precision=lax.Precision.HIGHEST)
    bias = jnp.where(jnp.asarray(inwin)[None], bias, NEG)
    return onehot, bias


def _swa_fwd(proj, bias, sink_rep, name):
    t = proj.shape[0]
    nb = t // SWA_BLOCK
    scale = HEAD ** -0.5
    csq, csk, csv = COL_SQ // 256, COL_SK // LANE, COL_SV // LANE

    def body(q_ref, kp_ref, kc_ref, vp_ref, vc_ref, b_ref, sk_ref, y_ref, lse_ref):
        kvh, n = pl.program_id(0), pl.program_id(1)
        lane = lax.broadcasted_iota(jnp.int32, (1, LANE), 1)
        lo = lane < HEAD
        kvm = jnp.logical_and(lane >= kvh * HEAD, lane < (kvh + 1) * HEAD)

        def both(prev_ref, cur_ref):
            band = jnp.concatenate([prev_ref[...], cur_ref[...]], axis=0)
            band = jnp.where(kvm, band, 0.0)
            return (band + pltpu.roll(band, HEAD, 1)).astype(BF16)

        kb, vb = both(kp_ref, kc_ref), both(vp_ref, vc_ref)
        col = lax.broadcasted_iota(jnp.int32, (SWA_BLOCK, 2 * SWA_BLOCK), 1)
        first = jnp.logical_and(n == 0, col < SWA_BLOCK)
        outs = []
        for g in range(SWA_GROUP):
            half = q_ref[:, (g // 2) * LANE:(g // 2 + 1) * LANE]
            hm = lo if g % 2 == 0 else jnp.logical_not(lo)
            qg = jnp.where(hm, half, 0.0).astype(BF16)
            s = _dot(qg, kb, NT) * scale + b_ref[g]
            s = jnp.where(first, NEG, s)
            snk = sk_ref[g:g + 1, :]
            m = jnp.maximum(jnp.max(s, axis=1, keepdims=True), snk)
            p = jnp.exp(s - jnp.tile(m, (1, 2)))
            denom = jnp.sum(p, axis=1, keepdims=True) + jnp.exp(snk - m)
            outs.append(_dot(p.astype(BF16), vb, NN) / denom)
            lse_ref[g] = m + jnp.log(denom)
        y_ref[:, 0:LANE] = jnp.where(lo, outs[0], outs[1]).astype(BF16)
        y_ref[:, LANE:2 * LANE] = jnp.where(lo, outs[2], outs[3]).astype(BF16)

    def blk(c, shift):
        return pl.BlockSpec((SWA_BLOCK, LANE), lambda kvh, n, c=c, s=shift: (jnp.maximum(n - s, 0), c))

    return pl.pallas_call(
        body, name=name, grid=(2, nb),
        out_shape=(jax.ShapeDtypeStruct((t, BRANCH), BF16), jax.ShapeDtypeStruct((8, t, LANE), F32)),
        in_specs=[pl.BlockSpec((SWA_BLOCK, 256), lambda kvh, n: (n, csq + kvh)),
                  blk(csk, 1), blk(csk, 0), blk(csv, 1), blk(csv, 0),
                  pl.BlockSpec((None, SWA_GROUP, SWA_BLOCK, 256), lambda kvh, n: (kvh, 0, 0, 0)),
                  pl.BlockSpec((None, SWA_GROUP, LANE), lambda kvh, n: (kvh, 0, 0))],
        out_specs=(pl.BlockSpec((SWA_BLOCK, 256), lambda kvh, n: (n, kvh)),
                   pl.BlockSpec((SWA_GROUP, SWA_BLOCK, LANE), lambda kvh, n: (kvh, n, 0))),
        compiler_params=_params(("parallel", "arbitrary"), 4 << 20),
    )(proj, proj, proj, proj, proj, bias.reshape(2, SWA_GROUP, SWA_BLOCK, 256), sink_rep)


def _swa_bwd(proj, bias, sink_rep, lse, y, dy, name):
    t = proj.shape[0]
    nb = t // SWA_BLOCK
    scale = HEAD ** -0.5
    csq, csk, csv = COL_SQ // 256, COL_SK // LANE, COL_SV // LANE

    def body(q_ref, kp_ref, kc_ref, vp_ref, vc_ref, b_ref, sk_ref, lse_ref, y_ref, dy_ref,
             dq_ref, dkp_ref, dvp_ref, db_ref, dsk_ref):
        kvh, n = pl.program_id(0), pl.program_id(1)
        lane = lax.broadcasted_iota(jnp.int32, (1, LANE), 1)
        lo = lane < HEAD
        kvm = jnp.logical_and(lane >= kvh * HEAD, lane < (kvh + 1) * HEAD)

        def both(prev_ref, cur_ref):
            band = jnp.concatenate([prev_ref[...], cur_ref[...]], axis=0)
            band = jnp.where(kvm, band, 0.0)
            return (band + pltpu.roll(band, HEAD, 1)).astype(BF16)

        kb, vb = both(kp_ref, kc_ref), both(vp_ref, vc_ref)
        col = lax.broadcasted_iota(jnp.int32, (SWA_BLOCK, 2 * SWA_BLOCK), 1)
        first = jnp.logical_and(n == 0, col < SWA_BLOCK)

        @pl.when(n == 0)
        def _():
            db_ref[...] = jnp.zeros_like(db_ref)
            dsk_ref[...] = jnp.zeros_like(dsk_ref)

        dk_full = jnp.zeros((2 * SWA_BLOCK, LANE), F32)
        dv_full = jnp.zeros((2 * SWA_BLOCK, LANE), F32)
        dqs = []
        for g in range(SWA_GROUP):
            sl = slice((g // 2) * LANE, (g // 2 + 1) * LANE)
            hm = lo if g % 2 == 0 else jnp.logical_not(lo)
            qg = jnp.where(hm, q_ref[:, sl], 0.0).astype(BF16)
            dog = jnp.where(hm, dy_ref[:, sl], jnp.zeros((SWA_BLOCK, LANE), BF16))
            dmat = jnp.where(hm, y_ref[:, sl].astype(F32) * dy_ref[:, sl].astype(F32), 0.0)
            dg = jnp.sum(dmat, axis=1, keepdims=True)
            s = _dot(qg, kb, NT) * scale + b_ref[g]
            s = jnp.where(first, NEG, s)
            lse_g = lse_ref[g]
            p = jnp.exp(s - jnp.tile(lse_g, (1, 2)))
            dp = _dot(dog, vb, NT)
            ds = p * (dp - dg)
            dsb = ds.astype(BF16)
            dqs.append(_dot(dsb, kb, NN) * scale)
            dk_full = dk_full + _dot(dsb, qg, TN)
            dv_full = dv_full + _dot(p.astype(BF16), dog, TN)
            db_ref[g] += ds
            psink = jnp.exp(sk_ref[g:g + 1, :] - lse_g)
            dsk_ref[g:g + 1, :] -= jnp.sum(psink * dg, axis=0, keepdims=True)
        dq_ref[:, 0:LANE] = jnp.where(lo, dqs[0], dqs[1]).astype(BF16)
        dq_ref[:, LANE:2 * LANE] = jnp.where(lo, dqs[2], dqs[3]).astype(BF16)
        dkp_ref[...] = jnp.where(kvm, (dk_full + pltpu.roll(dk_full, HEAD, 1)) * scale, 0.0)
        dvp_ref[...] = jnp.where(kvm, dv_full + pltpu.roll(dv_full, HEAD, 1), 0.0)

    def blk(c, shift):
        return pl.BlockSpec((SWA_BLOCK, LANE), lambda kvh, n, c=c, s=shift: (jnp.maximum(n - s, 0), c))

    qblk = pl.BlockSpec((SWA_BLOCK, 256), lambda kvh, n: (n, kvh))
    part = pl.BlockSpec((None, None, 2 * SWA_BLOCK, LANE), lambda kvh, n: (kvh, n, 0, 0))
    bspec = pl.BlockSpec((None, SWA_GROUP, SWA_BLOCK, 256), lambda kvh, n: (kvh, 0, 0, 0))
    sspec = pl.BlockSpec((None, SWA_GROUP, LANE), lambda kvh, n: (kvh, 0, 0))
    return pl.pallas_call(
        body, name=name, grid=(2, nb),
        out_shape=(jax.ShapeDtypeStruct((t, BRANCH), BF16),
                   jax.ShapeDtypeStruct((2, nb, 2 * SWA_BLOCK, LANE), F32),
                   jax.ShapeDtypeStruct((2, nb, 2 * SWA_BLOCK, LANE), F32),
                   jax.ShapeDtypeStruct((2, SWA_GROUP, SWA_BLOCK, 256), F32),
                   jax.ShapeDtypeStruct((2, SWA_GROUP, LANE), F32)),
        in_specs=[pl.BlockSpec((SWA_BLOCK, 256), lambda kvh, n: (n, csq + kvh)),
                  blk(csk, 1), blk(csk, 0), blk(csv, 1), blk(csv, 0), bspec, sspec,
                  pl.BlockSpec((SWA_GROUP, SWA_BLOCK, LANE), lambda kvh, n: (kvh, n, 0)), qblk, qblk],
        out_specs=(qblk, part, part, bspec, sspec),
        compiler_params=_params(("parallel", "arbitrary"), 6 << 20),
    )(proj, proj, proj, proj, proj, bias.reshape(2, SWA_GROUP, SWA_BLOCK, 256), sink_rep, lse, y, dy)


def _gate_fwd(proj, pb, name):
    t = proj.shape[0]
    tr = _pick(t, ROW_TILE // 2, BF16_SUBLANE)

    def body(g0, g1, g2, p0, p1, p2, o_ref):
        acc = jax.nn.sigmoid(g0[...]) * p0[...]
        acc = acc + jax.nn.sigmoid(g1[...]) * p1[...]
        acc = acc + jax.nn.sigmoid(g2[...]) * p2[...]
        o_ref[...] = acc.astype(BF16)

    row = pl.BlockSpec((tr, D_MODEL), lambda i: (i, 0))
    gates = [pl.BlockSpec((tr, D_MODEL), lambda i, b=b: (i, b)) for b in range(3)]
    return pl.pallas_call(
        body, name=name, grid=(t // tr,),
        out_shape=jax.ShapeDtypeStruct((t, D_MODEL), BF16),
        in_specs=gates + [row] * 3, out_specs=row,
        compiler_params=_params(("parallel",), 7 * _nbytes((tr, D_MODEL), F32)),
    )(proj, proj, proj, *pb)


def _gate_bwd(proj, pb, dmerged, name):
    t = proj.shape[0]
    tr = _pick(t, ROW_TILE // 2, BF16_SUBLANE)

    def body(g0, g1, g2, p0, p1, p2, dm_ref, dp0, dp1, dp2, dg_ref):
        dm = dm_ref[...]
        for b, (g_ref, p_ref, dp_ref) in enumerate(((g0, p0, dp0), (g1, p1, dp1), (g2, p2, dp2))):
            sg = jax.nn.sigmoid(g_ref[...])
            dp_ref[...] = (dm * sg).astype(BF16)
            dg_ref[:, b * D_MODEL:(b + 1) * D_MODEL] = (dm * p_ref[...] * sg * (1.0 - sg)).astype(BF16)

    row = pl.BlockSpec((tr, D_MODEL), lambda i: (i, 0))
    gates = [pl.BlockSpec((tr, D_MODEL), lambda i, b=b: (i, b)) for b in range(3)]
    return pl.pallas_call(
        body, name=name, grid=(t // tr,),
        out_shape=(jax.ShapeDtypeStruct((t, D_MODEL), BF16),) * 3 + (jax.ShapeDtypeStruct((t, 3 * D_MODEL), BF16),),
        in_specs=gates + [row] * 4,
        out_specs=(row, row, row, pl.BlockSpec((tr, 3 * D_MODEL), lambda i: (i, 0))),
        compiler_params=_params(("parallel",), 11 * _nbytes((tr, D_MODEL), F32)),
    )(proj, proj, proj, *pb, dmerged)


def _swiglu_fwd(ab, name):
    t = ab.shape[0]
    tr = _pick(t, ROW_TILE, BF16_SUBLANE)
    tc = D_FF // 2

    def body(a_ref, b_ref, o_ref):
        a = a_ref[...]
        o_ref[...] = (a * jax.nn.sigmoid(a) * b_ref[...]).astype(BF16)

    return pl.pallas_call(
        body, name=name, grid=(t // tr, 2),
        out_shape=jax.ShapeDtypeStruct((t, D_FF), BF16),
        in_specs=[pl.BlockSpec((tr, tc), lambda i, j: (i, j)), pl.BlockSpec((tr, tc), lambda i, j: (i, j + 2))],
        out_specs=pl.BlockSpec((tr, tc), lambda i, j: (i, j)),
        compiler_params=_params(("parallel", "parallel"), 3 * _nbytes((tr, tc), F32)),
    )(ab, ab)


def _swiglu_bwd(ab, dh, name):
    t = ab.shape[0]
    tr = _pick(t, ROW_TILE, BF16_SUBLANE)
    tc = D_FF // 2

    def body(a_ref, b_ref, dh_ref, o_ref):
        jj = pl.program_id(1)
        a, b, d = a_ref[...], b_ref[...], dh_ref[...]
        sg = jax.nn.sigmoid(a)
        da = d * b * (sg * (1.0 + a * (1.0 - sg)))
        db = d * (a * sg)
        o_ref[...] = jnp.where(jj < 2, da, db).astype(BF16)

    return pl.pallas_call(
        body, name=name, grid=(t // tr, 4),
        out_shape=jax.ShapeDtypeStruct((t, 2 * D_FF), BF16),
        in_specs=[pl.BlockSpec((tr, tc), lambda i, j: (i, j % 2)),
                  pl.BlockSpec((tr, tc), lambda i, j: (i, j % 2 + 2)),
                  pl.BlockSpec((tr, tc), lambda i, j: (i, j % 2))],
        out_specs=pl.BlockSpec((tr, tc), lambda i, j: (i, j)),
        compiler_params=_params(("parallel", "parallel"), 4 * _nbytes((tr, tc), F32)),
    )(ab, ab, dh)


def _xattn_fwd(q, kv, name):
    t = q.shape[0]
    tq = _pick(t, ROW_TILE, BF16_SUBLANE)
    mlen = kv.shape[0]
    scale = X_HEAD ** -0.5

    def body(q_ref, kv_ref, o_ref):
        for h in range(X_HEADS):
            sl = slice(h * X_HEAD, (h + 1) * X_HEAD)
            kh = kv_ref[:, sl]
            vh = kv_ref[:, D_MODEL + h * X_HEAD:D_MODEL + (h + 1) * X_HEAD]
            s = _dot(q_ref[:, sl], kh, NT) * scale
            p = jnp.exp(s - jnp.max(s, axis=1, keepdims=True))
            l = jnp.sum(p, axis=1, keepdims=True)
            o_ref[:, sl] = (_dot(p.astype(BF16), vh, NN) / l).astype(BF16)

    return pl.pallas_call(
        body, name=name, grid=(t // tq,),
        out_shape=jax.ShapeDtypeStruct((t, D_MODEL), BF16),
        in_specs=[pl.BlockSpec((tq, D_MODEL), lambda i: (i, 0)), pl.BlockSpec((mlen, 2 * D_MODEL), lambda i: (0, 0))],
        out_specs=pl.BlockSpec((tq, D_MODEL), lambda i: (i, 0)),
        compiler_params=_params(("parallel",), 4 * _nbytes((tq, D_MODEL), F32)),
    )(q, kv)


def _xattn_bwd(q, kv, do, name):
    t = q.shape[0]
    tq = _pick(t, ROW_TILE, BF16_SUBLANE)
    mlen = kv.shape[0]
    scale = X_HEAD ** -0.5

    def body(q_ref, kv_ref, do_ref, dq_ref, dkv_ref):
        i = pl.program_id(0)

        @pl.when(i == 0)
        def _():
            dkv_ref[...] = jnp.zeros_like(dkv_ref)

        for h in range(X_HEADS):
            sl = slice(h * X_HEAD, (h + 1) * X_HEAD)
            vsl = slice(D_MODEL + h * X_HEAD, D_MODEL + (h + 1) * X_HEAD)
            qh, kh, vh, doh = q_ref[:, sl], kv_ref[:, sl], kv_ref[:, vsl], do_ref[:, sl]
            s = _dot(qh, kh, NT) * scale
            p = jnp.exp(s - jnp.max(s, axis=1, keepdims=True))
            p = p / jnp.sum(p, axis=1, keepdims=True)
            dp = _dot(doh, vh, NT)
            ds = p * (dp - jnp.sum(p * dp, axis=1, keepdims=True))
            dsb = ds.astype(BF16)
            dq_ref[:, sl] = (_dot(dsb, kh, NN) * scale).astype(BF16)
            dkv_ref[:, sl] += _dot(dsb, qh, TN) * scale
            dkv_ref[:, vsl] += _dot(p.astype(BF16), doh, TN)

    row = pl.BlockSpec((tq, D_MODEL), lambda i: (i, 0))
    whole = pl.BlockSpec((mlen, 2 * D_MODEL), lambda i: (0, 0))
    return pl.pallas_call(
        body, name=name, grid=(t // tq,),
        out_shape=(jax.ShapeDtypeStruct((t, D_MODEL), BF16), jax.ShapeDtypeStruct((mlen, 2 * D_MODEL), F32)),
        in_specs=[row, whole, row], out_specs=(row, whole),
        compiler_params=_params(("arbitrary",), 6 * _nbytes((tq, D_MODEL), F32)),
    )(q, kv, do)


def _position():
    return lax.axis_index("x"), lax.axis_index("y"), lax.axis_index("c")


N_PEER = N_DEV - 1


def _all_gather(xs, name):
    n = len(xs)

    def body(*refs):
        x_refs, out_refs = refs[:n], refs[n:2 * n]
        send_sems, recv_sems, local_sems = refs[2 * n:]
        mx, my, mc = _position()
        me, sib = (mx, my, mc), (mx, my, 1 - mc)
        chips = [(1 - mx, my), (mx, 1 - my), (1 - mx, 1 - my)]

        def slot(i, p):
            return out_refs[i].at[4 * p[0] + 2 * p[1] + p[2]]

        def copy(i, k, block, to, src=None):
            return pltpu.make_async_remote_copy(
                src_ref=slot(i, block) if src is None else src, dst_ref=slot(i, block),
                send_sem=send_sems.at[i * N_PEER + k], recv_sem=recv_sems.at[i * N_PEER + k],
                device_id=to, device_id_type=MESH)

        mine = [pltpu.make_async_copy(x_refs[i], slot(i, me), local_sems.at[i]) for i in range(n)]
        for cp in mine:
            cp.start()
        first = [copy(i, 1 + j, me, (*chip, mc), src=x_refs[i]) for j, chip in enumerate(chips) for i in range(n)]
        first += [copy(i, 0, me, sib, src=x_refs[i]) for i in range(n)]
        for cp in first:
            cp.start()
        passed = []
        for j, chip in enumerate(chips):
            for i in range(n):
                copy(i, 1 + j, (*chip, mc), me).wait_recv()
                passed.append(copy(i, 4 + j, (*chip, mc), sib))
                passed[-1].start()
        for i in range(n):
            copy(i, 0, sib, me).wait_recv()
        for j, chip in enumerate(chips):
            for i in range(n):
                copy(i, 4 + j, (*chip, 1 - mc), me).wait_recv()
        for cp in first + passed:
            cp.wait_send()
        for cp in mine:
            cp.wait()

    return pl.pallas_call(
        body, name=name,
        out_shape=tuple(jax.ShapeDtypeStruct((N_DEV,) + x.shape, x.dtype) for x in xs),
        in_specs=[pl.BlockSpec(memory_space=pl.ANY)] * n, out_specs=(pl.BlockSpec(memory_space=pl.ANY),) * n,
        scratch_shapes=[pltpu.SemaphoreType.DMA((n * N_PEER,)), pltpu.SemaphoreType.DMA((n * N_PEER,)),
                        pltpu.SemaphoreType.DMA((n,))],
    )(*xs)


def _exchange(parts, name):
    n = len(parts)
    rels = [(dx, dy, dc) for dx in (0, 1) for dy in (0, 1) for dc in (0, 1)][1:]

    def body(*refs):
        g_refs, out_refs = refs[:n], refs[n:2 * n]
        send_sems, recv_sems, local_sems = refs[2 * n:]
        mx, my, mc = _position()
        me_idx = 4 * mx + 2 * my + mc

        def peer(rel):
            return tuple((1 - v) if f else v for f, v in zip(rel, (mx, my, mc)))

        def copy(i, k, p, arriving):
            p_idx = 4 * p[0] + 2 * p[1] + p[2]
            src, dst = (me_idx, p_idx) if arriving else (p_idx, me_idx)
            return pltpu.make_async_remote_copy(
                src_ref=g_refs[i].at[src], dst_ref=out_refs[i].at[dst],
                send_sem=send_sems.at[i * N_PEER + k], recv_sem=recv_sems.at[i * N_PEER + k],
                device_id=p, device_id_type=MESH)

        mine = [pltpu.make_async_copy(g_refs[i].at[me_idx], out_refs[i].at[me_idx], local_sems.at[i]) for i in range(n)]
        for cp in mine:
            cp.start()
        sends = [copy(i, k, peer(rel), False) for i in range(n) for k, rel in enumerate(rels)]
        for cp in sends:
            cp.start()
        for i in range(n):
            for k, rel in enumerate(rels):
                copy(i, k, peer(rel), True).wait_recv()
        for cp in sends:
            cp.wait_send()
        for cp in mine:
            cp.wait()

    return pl.pallas_call(
        body, name=name,
        out_shape=tuple(jax.ShapeDtypeStruct(p.shape, p.dtype) for p in parts),
        in_specs=[pl.BlockSpec(memory_space=pl.ANY)] * n, out_specs=(pl.BlockSpec(memory_space=pl.ANY),) * n,
        scratch_shapes=[pltpu.SemaphoreType.DMA((n * N_PEER,)), pltpu.SemaphoreType.DMA((n * N_PEER,)),
                        pltpu.SemaphoreType.DMA((n,))],
    )(*parts)


ADAMW_BLOCK_BYTES = 1 << 20


def _adamw(parts, w, m, v, name):
    r, l = w.shape
    tr = _pick(r, max(ADAMW_BLOCK_BYTES // (4 * l), BF16_SUBLANE), BF16_SUBLANE)
    c1 = 1.0 - ADAM_B1 ** ADAM_STEP
    c2 = 1.0 - ADAM_B2 ** ADAM_STEP

    def body(p_ref, w_ref, m_ref, v_ref, g_ref, d_ref, nm_ref, nv_ref):
        g = p_ref[0].astype(F32)
        for s in range(1, N_DEV):
            g = g + p_ref[s].astype(F32)
        nm = ADAM_B1 * m_ref[...] + (1.0 - ADAM_B1) * g
        nv = ADAM_B2 * v_ref[...] + (1.0 - ADAM_B2) * (g * g)
        m_hat = nm / c1
        v_hat = nv / c2
        g_ref[...] = g
        d_ref[...] = -ADAM_LR * (m_hat / (jnp.sqrt(v_hat) + ADAM_EPS) + ADAM_WD * w_ref[...])
        nm_ref[...] = nm
        nv_ref[...] = nv

    row = pl.BlockSpec((tr, l), lambda i: (i, 0))
    return pl.pallas_call(
        body, name=name, grid=(r // tr,),
        out_shape=(jax.ShapeDtypeStruct((r, l), F32),) * 4,
        in_specs=[pl.BlockSpec((N_DEV, tr, l), lambda i: (0, i, 0)), row, row, row],
        out_specs=(row,) * 4,
        compiler_params=_params(("parallel",), 12 * _nbytes((tr, l), F32)),
    )(parts, w, m, v)


MATRIX_WEIGHTS = (("w_in", 2), ("conv_w", 2), ("w_branch", 3), ("w_mix_out", 1), ("w_xq", 1), ("w_xkv", 2),
                  ("w_xo", 1), ("w_ffn_gate", 2), ("w_ffn_up", 2), ("w_ffn_down", 1))
SMALL_PARAMS = ("mix_norm_g", "xattn_norm_g", "mem_norm_g", "ffn_norm_g", "final_norm_g", "forget_bias", "sink",
                "rel_bias")


def _pack_small(pieces):
    flat = jnp.concatenate([p.astype(F32).reshape(-1) for p in pieces])
    total = -(-flat.shape[0] // (8 * LANE)) * (8 * LANE)
    return jnp.pad(flat, (0, total - flat.shape[0])).reshape(total // LANE, LANE)


def _rows(a):
    return a.reshape(-1, a.shape[-1])


def _to_full(gathered, axis):
    moved = jnp.moveaxis(gathered, 0, axis)
    shape = list(moved.shape)
    shape[axis:axis + 2] = [shape[axis] * shape[axis + 1]]
    return moved.reshape(shape)


def _to_blocks(full, axis):
    shape = list(full.shape)
    shape[axis:axis + 1] = [N_DEV, shape[axis] // N_DEV]
    return jnp.moveaxis(full.reshape(shape), axis, 0)


def _perm_in(w_in):
    pad = jnp.zeros((w_in.shape[0], PROJ_COLS - IN_COLS), w_in.dtype)
    return jnp.concatenate([w_in[:, 3848:6920], w_in[:, 0:3072], w_in[:, 3080:3848], w_in[:, 3072:3080], pad], axis=1)


def _unperm_in(dw):
    return jnp.concatenate([dw[:, 3072:6144], dw[:, 6912:6920], dw[:, 6144:6912], dw[:, 0:3072]], axis=1)


def _layer_fwd(l, x, mem, wt, sm):
    t = x.shape[0]
    tag = f"l{l}_"
    h = _rms_fwd(x, sm["mix_norm_g"][l], tag + "mix_norm")
    proj = _matmul(h, wt["w_in"][l], "nn", F32, tag + "in_proj")
    y_conv = _conv_fwd(proj, wt["conv_w"][l], tag + "conv")
    fbias_row = jnp.pad(sm["forget_bias"][l], (0, LANE - 8)).reshape(1, LANE)
    c = _logf_cumsum(proj, fbias_row, tag + "logf_cumsum")
    c8 = c[:, :8].T
    c_col = jnp.broadcast_to(c8[:, :, None], (8, t, LANE))
    c_row = c8.reshape(8, 1, t)
    y_fox, lse_fox = _fox_fwd(proj, c_col, c_row, tag + "fox")
    onehot, bias = _swa_tables(sm["rel_bias"])
    sink_rep = jnp.broadcast_to(sm["sink"][l].reshape(2, SWA_GROUP, 1), (2, SWA_GROUP, LANE))
    y_swa, lse_swa = _swa_fwd(proj, bias, sink_rep, tag + "swa")
    ys = (y_conv, y_fox, y_swa)
    pb = tuple(_matmul(ys[b], wt["w_branch"][l, b], "nn", F32, tag + f"branch{b}") for b in range(3))
    merged = _gate_fwd(proj, pb, tag + "gate")
    x1 = _matmul(merged, wt["w_mix_out"][l], "nn", F32, tag + "mix_out", residual=x)
    xn2 = _rms_fwd(x1, sm["xattn_norm_g"][l], tag + "xattn_norm")
    q = _matmul(xn2, wt["w_xq"][l], "nn", BF16, tag + "xq")
    mem_n = _rms_fwd(mem, sm["mem_norm_g"][l], tag + "mem_norm")
    kv = _matmul(mem_n, wt["w_xkv"][l], "nn", BF16, tag + "xkv")
    o = _xattn_fwd(q, kv, tag + "xattn")
    x2 = _matmul(o, wt["w_xo"][l], "nn", F32, tag + "xo", residual=x1)
    xn3 = _rms_fwd(x2, sm["ffn_norm_g"][l], tag + "ffn_norm")
    ab = _matmul(xn3, wt["w_gu"][l], "nn", F32, tag + "ffn_gu")
    h1 = _swiglu_fwd(ab, tag + "swiglu")
    x3 = _matmul(h1, wt["w_ffn_down"][l], "nn", F32, tag + "ffn_down", residual=x2)
    saved = dict(x=x, h=h, proj=proj, fbias_row=fbias_row, c_col=c_col, c_row=c_row, ys=ys, lse_fox=lse_fox,
                 onehot=onehot, bias=bias, sink_rep=sink_rep, lse_swa=lse_swa, pb=pb, merged=merged, x1=x1,
                 xn2=xn2, q=q, mem_n=mem_n, kv=kv, o=o, x2=x2, xn3=xn3, ab=ab, h1=h1)
    return x3, saved


def _layer_bwd(l, dx3, dx3_b, mem, wt, sm, sv):
    t = dx3.shape[0]
    nb = t // SWA_BLOCK
    tag = f"l{l}_b_"
    gw, gs = {}, {}
    dh1 = _matmul(dx3_b, wt["w_ffn_down"][l], "nt", F32, tag + "d_h1")
    gw["w_ffn_down"] = _matmul(sv["h1"], dx3_b, "tn", F32, tag + "dw_down")
    dab = _swiglu_bwd(sv["ab"], dh1, tag + "swiglu")
    dxn3 = _matmul(dab, wt["w_gu"][l], "nt", F32, tag + "d_xn3")
    dw_gu = _matmul(sv["xn3"], dab, "tn", F32, tag + "dw_gu")
    gw["w_ffn_gate"], gw["w_ffn_up"] = dw_gu[:, :D_FF], dw_gu[:, D_FF:]
    dx2, dx2_b, gs["ffn_norm_g"] = _rms_bwd(sv["x2"], sm["ffn_norm_g"][l], dxn3, dx3, tag + "ffn_norm")
    do = _matmul(dx2_b, wt["w_xo"][l], "nt", BF16, tag + "d_o")
    gw["w_xo"] = _matmul(sv["o"], dx2_b, "tn", F32, tag + "dw_xo")
    dq, dkv = _xattn_bwd(sv["q"], sv["kv"], do, tag + "xattn")
    gw["w_xkv"] = _matmul(sv["mem_n"], dkv, "tn", F32, tag + "dw_xkv")
    dmem_n = _matmul(dkv, wt["w_xkv"][l], "nt", F32, tag + "d_memn")
    _, _, gs["mem_norm_g"] = _rms_bwd(mem, sm["mem_norm_g"][l], dmem_n, None, tag + "mem_norm")
    gw["w_xq"] = _matmul(sv["xn2"], dq, "tn", F32, tag + "dw_xq")
    dxn2 = _matmul(dq, wt["w_xq"][l], "nt", F32, tag + "d_xn2")
    dx1, dx1_b, gs["xattn_norm_g"] = _rms_bwd(sv["x1"], sm["xattn_norm_g"][l], dxn2, dx2, tag + "xattn_norm")
    dmerged = _matmul(dx1_b, wt["w_mix_out"][l], "nt", F32, tag + "d_merged")
    gw["w_mix_out"] = _matmul(sv["merged"], dx1_b, "tn", F32, tag + "dw_mix_out")
    dp0, dp1, dp2, dgate = _gate_bwd(sv["proj"], sv["pb"], dmerged, tag + "gate")
    dps = (dp0, dp1, dp2)
    dy_dtypes = (F32, BF16, BF16)
    dys = [_matmul(dps[b], wt["w_branch"][l, b], "nt", dy_dtypes[b], tag + f"d_y{b}") for b in range(3)]
    gw["w_branch"] = jnp.stack(
        [_matmul(sv["ys"][b], dps[b], "tn", F32, tag + f"dw_branch{b}") for b in range(3)])
    dsq, dkp, dvp, dbias, dsink = _swa_bwd(sv["proj"], sv["bias"], sv["sink_rep"], sv["lse_swa"], sv["ys"][2],
                                           dys[2], tag + "swa")

    def band_add(part):
        tot = part[0] + part[1]
        cur = tot[:, SWA_BLOCK:, :]
        nxt = jnp.concatenate([tot[1:, :SWA_BLOCK, :], jnp.zeros((1, SWA_BLOCK, LANE), F32)], axis=0)
        return (cur + nxt).reshape(t, LANE).astype(BF16)

    dsk, dsv = band_add(dkp), band_add(dvp)
    gs["rel_bias_l"] = jnp.einsum("hts,tsb->bh", dbias.reshape(8, SWA_BLOCK, 2 * SWA_BLOCK), sv["onehot"],
                                  precision=lax.Precision.HIGHEST)
    gs["sink"] = dsink[:, :, 0].reshape(8)
    lse_row = sv["lse_fox"][:, :, 0].reshape(8, 1, t)
    dfq, dfk, dfv, dck, dcq_row = _fox_bwd(sv["proj"], sv["c_col"], sv["c_row"], lse_row, sv["ys"][1], dys[1],
                                           tag + "fox_bwd")
    dcq = jnp.pad(dcq_row.reshape(8, t).T, ((0, 0), (0, LANE - 8))).reshape(1, t, LANE)
    dfg, dfb = _logf_cumsum_bwd(sv["proj"], sv["fbias_row"], [dck, dcq], tag + "logf_cumsum")
    gs["forget_bias"] = dfb[0, :8]
    dcb, dcc, dcu, dconv = _conv_bwd(sv["proj"], wt["conv_w"][l], dys[0], tag + "conv")
    gw["conv_w"] = dconv[:3]
    dproj = jnp.concatenate([dgate, dcb, dcc, dcu, dfq, dfk, dfv, dsq, dsk, dsv, dfg], axis=1)
    dh = _matmul(dproj, wt["w_in"][l], "nt", F32, tag + "d_h")
    gw["w_in"] = _unperm_in(_matmul(sv["h"], dproj, "tn", F32, tag + "dw_in"))
    dx, dx_b, gs["mix_norm_g"] = _rms_bwd(sv["x"], sm["mix_norm_g"][l], dh, dx1, tag + "mix_norm")
    return dx, dx_b, gw, gs


def kernel(x, mem, mix_norm_g, w_in, forget_bias, conv_w, sink, w_branch, w_mix_out, rel_bias, xattn_norm_g, mem_norm_g, w_xq, w_xkv, w_xo, ffn_norm_g, w_ffn_gate, w_ffn_up, w_ffn_down, final_norm_g, loss_target, m_mix_norm_g, m_w_in, m_forget_bias, m_conv_w, m_sink, m_w_branch, m_w_mix_out, m_rel_bias, m_xattn_norm_g, m_mem_norm_g, m_w_xq, m_w_xkv, m_w_xo, m_ffn_norm_g, m_w_ffn_gate, m_w_ffn_up, m_w_ffn_down, m_final_norm_g, v_mix_norm_g, v_w_in, v_forget_bias, v_conv_w, v_sink, v_w_branch, v_w_mix_out, v_rel_bias, v_xattn_norm_g, v_mem_norm_g, v_w_xq, v_w_xkv, v_w_xo, v_ffn_norm_g, v_w_ffn_gate, v_w_ffn_up, v_w_ffn_down, v_final_norm_g):
    args = dict(locals())
    names = [n for n, _ in MATRIX_WEIGHTS] + list(SMALL_PARAMS)
    w = {n: args[n] for n in names}
    mo = {n: args["m_" + n] for n in names}
    vo = {n: args["v_" + n] for n in names}
    x2d, mem2d, tgt = x[0], mem[0], loss_target[0]

    wire = {n: (F32 if n == "conv_w" else BF16) for n, _ in MATRIX_WEIGHTS}
    gathered = _all_gather([w[n].astype(wire[n]) for n, _ in MATRIX_WEIGHTS], "weights_all_gather")
    wt = {n: _to_full(g, ax) for (n, ax), g in zip(MATRIX_WEIGHTS, gathered)}
    wt["w_in"] = jnp.stack([_perm_in(wt["w_in"][l]) for l in range(DEPTH)])
    wt["w_gu"] = jnp.concatenate([wt["w_ffn_gate"], wt["w_ffn_up"]], axis=2)
    sm = {n: w[n] for n in SMALL_PARAMS}

    saved = []
    xc = x2d
    for l in range(DEPTH):
        xc, sv = _layer_fwd(l, xc, mem2d, wt, sm)
        saved.append(sv)
    loss_row, dx, dx_b, dg_final = _loss_head(xc, sm["final_norm_g"], tgt, "loss_head")
    loss = lax.psum(loss_row[0, 0], ("x", "y", "c"))
    gw_all, gs_all = [None] * DEPTH, [None] * DEPTH
    for l in reversed(range(DEPTH)):
        dx, dx_b, gw_all[l], gs_all[l] = _layer_bwd(l, dx, dx_b, mem2d, wt, sm, saved[l])
    grad_x = dx[None]

    gparts = []
    for n, ax in MATRIX_WEIGHTS:
        full = jnp.stack([gw_all[l][n] for l in range(DEPTH)])
        gparts.append(_to_blocks(full, ax).astype(wire[n]))
    recv = _exchange(gparts, "grads_exchange")
    outs = {}
    for (n, _), r in zip(MATRIX_WEIGHTS, recv):
        res = _adamw(r.reshape((N_DEV,) + _rows(w[n]).shape), _rows(w[n]), _rows(mo[n]), _rows(vo[n]), "adamw_" + n)
        outs[n] = [o.reshape(w[n].shape) for o in res]

    gsm = {n: jnp.stack([gs_all[l][n] for l in range(DEPTH)])
           for n in ("mix_norm_g", "xattn_norm_g", "mem_norm_g", "ffn_norm_g", "forget_bias", "sink")}
    gsm["final_norm_g"] = dg_final
    gsm["rel_bias"] = gs_all[0]["rel_bias_l"] + gs_all[1]["rel_bias_l"]
    (small_parts,) = _all_gather([_pack_small([gsm[n] for n in SMALL_PARAMS])], "small_grads_all_gather")
    outs_small = _adamw(small_parts, *[_pack_small([d[n] for n in SMALL_PARAMS]) for d in (w, mo, vo)], "adamw_small")
    for kind in range(4):
        flat, o = outs_small[kind].reshape(-1), 0
        for n in SMALL_PARAMS:
            sz = int(np.prod(w[n].shape))
            outs.setdefault(n, []).append(flat[o:o + sz].reshape(w[n].shape))
            o += sz

    order = ["mix_norm_g", "w_in", "forget_bias", "conv_w", "sink", "w_branch", "w_mix_out", "rel_bias",
             "xattn_norm_g", "mem_norm_g", "w_xq", "w_xkv", "w_xo", "ffn_norm_g", "w_ffn_gate", "w_ffn_up",
             "w_ffn_down", "final_norm_g"]
    result = [loss, grad_x]
    for kind in range(4):
        result += [outs[n][kind] for n in order]
    return tuple(result)
```

```python
import math

import numpy as np
import jax
import jax.numpy as jnp
from jax import lax
from jax.experimental import pallas as pl
from jax.experimental.pallas import tpu as pltpu

F32 = jnp.float32
BF16 = jnp.bfloat16
MESH = pl.DeviceIdType.MESH

LANE = 128
BF16_SUBLANE = 16
V7X_VMEM_REQUEST_CAP = 56 * 2 ** 20
N_DEV = 8

D_MODEL = 1024
DEPTH = 2
HEAD = 64
BRANCH = 512
SWA_BLOCK = 128
SWA_GROUP = 4
N_BUCKETS = 32
X_HEADS = 4
X_HEAD = 256
D_FF = 2816
RMS_EPS = 1e-6
NEG = -1e30
ADAM_LR, ADAM_B1, ADAM_B2, ADAM_EPS, ADAM_WD, ADAM_STEP = 0.001, 0.9, 0.999, 1e-08, 0.01, 10

IN_COLS = 6920
PROJ_COLS = 7040
COL_GATE, COL_CONV, COL_FOX, COL_SQ, COL_SK, COL_SV, COL_FG = 0, 3072, 4608, 6144, 6656, 6784, 6912

ROW_TILE = 512
FOX_TILE = 512
MM_TM, MM_TN, MM_TK = 1024, 1536, 1024


def _pick(n, cap, mult):
    best = None
    for d in range(mult, min(n, cap) + 1, mult):
        if n % d == 0:
            best = d
    return n if best is None else best


def _params(semantics, block_bytes):
    limit = int(min(max(2 * block_bytes + (8 << 20), 24 << 20), V7X_VMEM_REQUEST_CAP))
    return pltpu.CompilerParams(dimension_semantics=semantics, vmem_limit_bytes=limit)


def _nbytes(shape, dtype):
    return int(np.prod(shape)) * jnp.dtype(dtype).itemsize


def _dot(a, b, dims):
    return lax.dot_general(a, b, (dims, ((), ())), preferred_element_type=F32)


NN = ((1,), (0,))
NT = ((1,), (1,))
TN = ((0,), (0,))


def _matmul(a, b, mode, out_dtype, name, residual=None):
    if mode == "nn":
        (m, k), (k2, n) = a.shape, b.shape
    elif mode == "nt":
        (m, k), (n, k2) = a.shape, b.shape
    else:
        (k, m), (k2, n) = a.shape, b.shape
    assert k == k2, (name, a.shape, b.shape)
    tm, tn, tk = _pick(m, MM_TM, LANE), _pick(n, MM_TN, LANE), _pick(k, MM_TK, LANE)
    nk = k // tk
    dims = {"nn": NN, "nt": NT, "tn": TN}[mode]
    has_res = residual is not None

    def body(*refs):
        a_ref, b_ref = refs[0], refs[1]
        r_ref = refs[2] if has_res else None
        o_ref = refs[3] if has_res else refs[2]
        kk = pl.program_id(2)
        p = _dot(a_ref[...].astype(BF16), b_ref[...].astype(BF16), dims)
        if nk == 1:
            if has_res:
                p = p + r_ref[...]
            o_ref[...] = p.astype(out_dtype)
        else:
            acc_ref = refs[-1]

            @pl.when(kk == 0)
            def _():
                acc_ref[...] = p

            @pl.when(kk > 0)
            def _():
                acc_ref[...] += p

            @pl.when(kk == nk - 1)
            def _():
                res = acc_ref[...]
                if has_res:
                    res = res + r_ref[...]
                o_ref[...] = res.astype(out_dtype)

    if mode == "nn":
        a_spec = pl.BlockSpec((tm, tk), lambda i, j, kk: (i, kk))
        b_spec = pl.BlockSpec((tk, tn), lambda i, j, kk: (kk, j))
    elif mode == "nt":
        a_spec = pl.BlockSpec((tm, tk), lambda i, j, kk: (i, kk))
        b_spec = pl.BlockSpec((tn, tk), lambda i, j, kk: (j, kk))
    else:
        a_spec = pl.BlockSpec((tk, tm), lambda i, j, kk: (kk, i))
        b_spec = pl.BlockSpec((tk, tn), lambda i, j, kk: (kk, j))
    o_spec = pl.BlockSpec((tm, tn), lambda i, j, kk: (i, j))
    in_specs, args = [a_spec, b_spec], [a, b]
    if has_res:
        in_specs.append(o_spec)
        args.append(residual)
    blk = (_nbytes((tm, tk), a.dtype) + _nbytes((tk, tn), b.dtype) + _nbytes((tm, tn), out_dtype)
           + (_nbytes((tm, tn), F32) if has_res else 0))
    scratch = [pltpu.VMEM((tm, tn), F32)] if nk > 1 else []
    return pl.pallas_call(
        body, name=name, grid=(m // tm, n // tn, nk),
        out_shape=jax.ShapeDtypeStruct((m, n), out_dtype),
        in_specs=in_specs, out_specs=o_spec, scratch_shapes=scratch,
        compiler_params=_params(("parallel", "parallel", "arbitrary"), blk + _nbytes((tm, tn), F32)),
    )(*args)


def _rms_fwd(x, g, name):
    t, d = x.shape
    tr = _pick(t, ROW_TILE, BF16_SUBLANE)

    def body(x_ref, g_ref, y_ref):
        xv = x_ref[...]
        r = lax.rsqrt(jnp.mean(xv * xv, axis=-1, keepdims=True) + RMS_EPS)
        y_ref[...] = ((xv * r) * g_ref[...]).astype(BF16)

    return pl.pallas_call(
        body, name=name, grid=(t // tr,),
        out_shape=jax.ShapeDtypeStruct((t, d), BF16),
        in_specs=[pl.BlockSpec((tr, d), lambda i: (i, 0)), pl.BlockSpec((1, d), lambda i: (0, 0))],
        out_specs=pl.BlockSpec((tr, d), lambda i: (i, 0)),
        compiler_params=_params(("parallel",), 2 * _nbytes((tr, d), F32)),
    )(x, g.reshape(1, d))


def _rms_bwd(x, g, dy, dres, name):
    t, d = x.shape
    tr = _pick(t, ROW_TILE, BF16_SUBLANE)
    has_res = dres is not None

    def body(*refs):
        x_ref, g_ref, dy_ref = refs[:3]
        r_ref = refs[3] if has_res else None
        dx_ref, dxb_ref, dg_ref = refs[-3:]
        i = pl.program_id(0)
        xv = x_ref[...]
        r = lax.rsqrt(jnp.mean(xv * xv, axis=-1, keepdims=True) + RMS_EPS)
        xh = xv * r
        dyv = dy_ref[...].astype(F32)
        dxh = dyv * g_ref[...]
        dx = r * (dxh - xh * jnp.mean(dxh * xh, axis=-1, keepdims=True))
        if has_res:
            dx = dx + r_ref[...]
        dx_ref[...] = dx
        dxb_ref[...] = dx.astype(BF16)

        @pl.when(i == 0)
        def _():
            dg_ref[...] = jnp.zeros_like(dg_ref)

        dg_ref[...] += jnp.sum(dyv * xh, axis=0, keepdims=True)

    row = pl.BlockSpec((tr, d), lambda i: (i, 0))
    vec = pl.BlockSpec((1, d), lambda i: (0, 0))
    in_specs, args = [row, vec, row], [x, g.reshape(1, d), dy]
    if has_res:
        in_specs.append(row)
        args.append(dres)
    return pl.pallas_call(
        body, name=name, grid=(t // tr,),
        out_shape=(jax.ShapeDtypeStruct((t, d), F32), jax.ShapeDtypeStruct((t, d), BF16),
                   jax.ShapeDtypeStruct((1, d), F32)),
        in_specs=in_specs, out_specs=(row, row, vec),
        compiler_params=_params(("arbitrary",), 5 * _nbytes((tr, d), F32)),
    )(*args)


def _loss_head(x, g, target, name):
    t, d = x.shape
    tr = _pick(t, ROW_TILE, BF16_SUBLANE)

    def body(x_ref, g_ref, t_ref, loss_ref, dx_ref, dxb_ref, dg_ref):
        i = pl.program_id(0)
        xv = x_ref[...]
        gv = g_ref[...]
        r = lax.rsqrt(jnp.mean(xv * xv, axis=-1, keepdims=True) + RMS_EPS)
        xh = xv * r
        diff = xh * gv - t_ref[...]
        part = 0.5 * jnp.sum(jnp.mean(diff * diff, axis=-1, keepdims=True), axis=0, keepdims=True)
        dyv = diff * (1.0 / d)
        dxh = dyv * gv
        dx = r * (dxh - xh * jnp.mean(dxh * xh, axis=-1, keepdims=True))
        dx_ref[...] = dx
        dxb_ref[...] = dx.astype(BF16)

        @pl.when(i == 0)
        def _():
            dg_ref[...] = jnp.zeros_like(dg_ref)
            loss_ref[...] = jnp.zeros_like(loss_ref)

        dg_ref[...] += jnp.sum(dyv * xh, axis=0, keepdims=True)
        loss_ref[...] += jnp.broadcast_to(part, loss_ref.shape)

    row = pl.BlockSpec((tr, d), lambda i: (i, 0))
    vec = pl.BlockSpec((1, d), lambda i: (0, 0))
    return pl.pallas_call(
        body, name=name, grid=(t // tr,),
        out_shape=(jax.ShapeDtypeStruct((1, LANE), F32), jax.ShapeDtypeStruct((t, d), F32),
                   jax.ShapeDtypeStruct((t, d), BF16), jax.ShapeDtypeStruct((1, d), F32)),
        in_specs=[row, vec, row],
        out_specs=(pl.BlockSpec((1, LANE), lambda i: (0, 0)), row, row, vec),
        compiler_params=_params(("arbitrary",), 5 * _nbytes((tr, d), F32)),
    )(x, g.reshape(1, d), target)


HALO = 8


def _conv_fwd(proj, conv_w, name):
    t = proj.shape[0]
    tr = _pick(t, ROW_TILE, BF16_SUBLANE)
    c0 = COL_CONV // BRANCH
    hb = tr // HALO

    def body(cb_ref, cc_ref, cu_ref, hc_ref, hu_ref, w_ref, y_ref):
        i = pl.program_id(0)
        z = cc_ref[...] * cu_ref[...]
        hz = jnp.where(i > 0, hc_ref[...] * hu_ref[...], 0.0)
        zf = jnp.concatenate([hz, z], axis=0)
        z1 = pltpu.roll(zf, 1, 0)[HALO:]
        z2 = pltpu.roll(zf, 2, 0)[HALO:]
        y = w_ref[2:3, :] * z + w_ref[1:2, :] * z1 + w_ref[0:1, :] * z2
        y_ref[...] = (cb_ref[...] * y).astype(BF16)

    def col(c):
        return pl.BlockSpec((tr, BRANCH), lambda i, c=c: (i, c0 + c))

    def prev(c):
        return pl.BlockSpec((HALO, BRANCH), lambda i, c=c: (jnp.maximum(i * hb - 1, 0), c0 + c))

    return pl.pallas_call(
        body, name=name, grid=(t // tr,),
        out_shape=jax.ShapeDtypeStruct((t, BRANCH), BF16),
        in_specs=[col(0), col(1), col(2), prev(1), prev(2), pl.BlockSpec((3, BRANCH), lambda i: (0, 0))],
        out_specs=pl.BlockSpec((tr, BRANCH), lambda i: (i, 0)),
        compiler_params=_params(("parallel",), 6 * _nbytes((tr, BRANCH), F32)),
    )(proj, proj, proj, proj, proj, conv_w)


def _conv_bwd(proj, conv_w, dout, name):
    t = proj.shape[0]
    tr = _pick(t, ROW_TILE, BF16_SUBLANE)
    nblk = t // tr
    c0 = COL_CONV // BRANCH
    hb = tr // HALO
    last_halo = t // HALO - 1

    def body(cb_ref, cc_ref, cu_ref, hc_ref, hu_ref, do_ref, ndo_ref, ncb_ref, w_ref,
             dcb_ref, dcc_ref, dcu_ref, dw_ref):
        i = pl.program_id(0)
        cb, cc, cu = cb_ref[...], cc_ref[...], cu_ref[...]
        w0, w1, w2 = w_ref[0:1, :], w_ref[1:2, :], w_ref[2:3, :]
        z = cc * cu
        hz = jnp.where(i > 0, hc_ref[...] * hu_ref[...], 0.0)
        zf = jnp.concatenate([hz, z], axis=0)
        z1 = pltpu.roll(zf, 1, 0)[HALO:]
        z2 = pltpu.roll(zf, 2, 0)[HALO:]
        y = w2 * z + w1 * z1 + w0 * z2
        dout_v = do_ref[...]
        dyc = dout_v * cb
        hdy = jnp.where(i < nblk - 1, ndo_ref[...] * ncb_ref[...], 0.0)
        dyf = jnp.concatenate([dyc, hdy], axis=0)
        dy1 = pltpu.roll(dyf, tr + HALO - 1, 0)[:tr]
        dy2 = pltpu.roll(dyf, tr + HALO - 2, 0)[:tr]
        dz = w2 * dyc + w1 * dy1 + w0 * dy2
        dcb_ref[...] = (dout_v * y).astype(BF16)
        dcc_ref[...] = (dz * cu).astype(BF16)
        dcu_ref[...] = (dz * cc).astype(BF16)

        @pl.when(i == 0)
        def _():
            dw_ref[...] = jnp.zeros_like(dw_ref)

        dw_ref[0:1, :] += jnp.sum(dyc * z2, axis=0, keepdims=True)
        dw_ref[1:2, :] += jnp.sum(dyc * z1, axis=0, keepdims=True)
        dw_ref[2:3, :] += jnp.sum(dyc * z, axis=0, keepdims=True)

    def col(c):
        return pl.BlockSpec((tr, BRANCH), lambda i, c=c: (i, c0 + c))

    def prev(c):
        return pl.BlockSpec((HALO, BRANCH), lambda i, c=c: (jnp.maximum(i * hb - 1, 0), c0 + c))

    def nxt(c):
        return pl.BlockSpec((HALO, BRANCH), lambda i, c=c: (jnp.minimum((i + 1) * hb, last_halo), c))

    row = pl.BlockSpec((tr, BRANCH), lambda i: (i, 0))
    return pl.pallas_call(
        body, name=name, grid=(nblk,),
        out_shape=(jax.ShapeDtypeStruct((t, BRANCH), BF16),) * 3 + (jax.ShapeDtypeStruct((HALO, BRANCH), F32),),
        in_specs=[col(0), col(1), col(2), prev(1), prev(2), row, nxt(0), nxt(c0),
                  pl.BlockSpec((3, BRANCH), lambda i: (0, 0))],
        out_specs=(row, row, row, pl.BlockSpec((HALO, BRANCH), lambda i: (0, 0))),
        compiler_params=_params(("arbitrary",), 8 * _nbytes((tr, BRANCH), F32)),
    )(proj, proj, proj, proj, proj, dout, dout, proj, conv_w)


def _tri(lower):
    r = lax.broadcasted_iota(jnp.int32, (LANE, LANE), 0)
    c = lax.broadcasted_iota(jnp.int32, (LANE, LANE), 1)
    return jnp.where((c <= r) if lower else (c >= r), 1.0, 0.0).astype(F32)


def _logf_cumsum(proj, fbias_row, name):
    t = proj.shape[0]
    nchunk = t // LANE

    def body(f_ref, b_ref, c_ref, run_sc):
        tri = _tri(True)
        run_sc[...] = jnp.zeros_like(run_sc)

        @pl.loop(0, nchunk)
        def _(i):
            rows = pl.ds(pl.multiple_of(i * LANE, LANE), LANE)
            z = f_ref[rows, :] + b_ref[...]
            logf = jnp.minimum(z, 0.0) - jnp.log(1.0 + jnp.exp(-jnp.abs(z)))
            cs = lax.dot_general(tri, logf, (NN, ((), ())), precision=lax.Precision.HIGHEST,
                                 preferred_element_type=F32) + run_sc[0:1, :]
            c_ref[rows, :] = cs
            run_sc[0:1, :] = cs[LANE - 1:LANE, :]

    return pl.pallas_call(
        body, name=name, grid=(1,),
        out_shape=jax.ShapeDtypeStruct((t, LANE), F32),
        in_specs=[pl.BlockSpec((t, LANE), lambda i: (0, COL_FG // LANE)), pl.BlockSpec((1, LANE), lambda i: (0, 0))],
        out_specs=pl.BlockSpec((t, LANE), lambda i: (0, 0)),
        scratch_shapes=[pltpu.VMEM((8, LANE), F32)],
        compiler_params=_params(("arbitrary",), 2 * _nbytes((t, LANE), F32)),
    )(proj, fbias_row)


def _logf_cumsum_bwd(proj, fbias_row, pieces, name):
    t = proj.shape[0]
    tb = _pick(t, 2 * ROW_TILE, LANE)
    nblk = t // tb
    npiece = len(pieces)

    def body(*refs):
        f_ref, b_ref = refs[:2]
        piece_refs = refs[2:2 + npiece]
        df_ref, db_ref, run_sc = refs[2 + npiece:]
        i = pl.program_id(0)
        tri = _tri(False)

        @pl.when(i == 0)
        def _():
            run_sc[...] = jnp.zeros_like(run_sc)
            db_ref[...] = jnp.zeros_like(db_ref)

        for c in reversed(range(tb // LANE)):
            rows = slice(c * LANE, (c + 1) * LANE)
            slabs = [p_ref[n, rows, :] for p_ref in piece_refs for n in range(p_ref.shape[0])]
            dcc = slabs[0]
            for slab in slabs[1:]:
                dcc = dcc + slab
            ss = lax.dot_general(tri, dcc, (NN, ((), ())), precision=lax.Precision.HIGHEST,
                                 preferred_element_type=F32) + run_sc[0:1, :]
            z = f_ref[rows, :] + b_ref[...]
            dz = ss * (1.0 / (1.0 + jnp.exp(z)))
            df_ref[rows, :] = dz.astype(BF16)
            run_sc[0:1, :] = ss[0:1, :]
            db_ref[...] += jnp.sum(dz, axis=0, keepdims=True)

    piece_specs = [pl.BlockSpec((p.shape[0], tb, LANE), lambda i: (0, nblk - 1 - i, 0)) for p in pieces]
    nslab = sum(p.shape[0] for p in pieces)
    return pl.pallas_call(
        body, name=name, grid=(nblk,),
        out_shape=(jax.ShapeDtypeStruct((t, LANE), BF16), jax.ShapeDtypeStruct((1, LANE), F32)),
        in_specs=[pl.BlockSpec((tb, LANE), lambda i: (nblk - 1 - i, COL_FG // LANE)),
                  pl.BlockSpec((1, LANE), lambda i: (0, 0))] + piece_specs,
        out_specs=(pl.BlockSpec((tb, LANE), lambda i: (nblk - 1 - i, 0)), pl.BlockSpec((1, LANE), lambda i: (0, 0))),
        scratch_shapes=[pltpu.VMEM((8, LANE), F32)],
        compiler_params=_params(("arbitrary",), (4 + nslab) * _nbytes((tb, LANE), F32)),
    )(proj, fbias_row, *pieces)


def _lo_mask():
    return lax.broadcasted_iota(jnp.int32, (1, LANE), 1) < HEAD


def _causal_steps(n, key_major):
    if key_major:
        pairs = [(iq, ik) for ik in range(n) for iq in range(ik, n)]
    else:
        pairs = [(iq, ik) for iq in range(n) for ik in range(iq + 1)]
    return (jnp.asarray([p[0] for p in pairs], jnp.int32), jnp.asarray([p[1] for p in pairs], jnp.int32))


def _head_lanes(j, pair_vals):
    lane = lax.broadcasted_iota(jnp.int32, (1, LANE), 1)
    return jnp.where(lane == 2 * j, pair_vals[0], 0.0) + jnp.where(lane == 2 * j + 1, pair_vals[1], 0.0)


def _fox_fwd(proj, c_col, c_row, name):
    t = proj.shape[0]
    tq = _pick(t, FOX_TILE, LANE)
    nq = t // tq
    rep = tq // LANE
    scale = HEAD ** -0.5
    cq, ck, cv = COL_FOX // LANE, COL_FOX // LANE + 4, COL_FOX // LANE + 8
    q_tab, k_tab = _causal_steps(nq, False)

    def body(qt_ref, kt_ref, q_ref, k_ref, v_ref, cc_ref, cr_ref, y_ref, lse_ref, m_sc, l_sc, acc_sc):
        step_id = pl.program_id(1)
        iq, ik = qt_ref[step_id], kt_ref[step_id]
        lo = _lo_mask()

        @pl.when(ik == 0)
        def _():
            m_sc[...] = jnp.full(m_sc.shape, NEG, F32)
            l_sc[...] = jnp.zeros_like(l_sc)
            acc_sc[...] = jnp.zeros_like(acc_sc)

        def step(diag):
            q2 = q_ref[...].astype(BF16)
            k2 = k_ref[...].astype(BF16)
            v2 = v_ref[...].astype(BF16)
            for h in range(2):
                msk = lo if h == 0 else jnp.logical_not(lo)
                qh = jnp.where(msk, q2, jnp.zeros_like(q2))
                s = _dot(qh, k2, NT) * scale
                s = s + jnp.tile(cc_ref[h], (1, rep)) - cr_ref[h]
                if diag:
                    row = lax.broadcasted_iota(jnp.int32, (tq, tq), 0)
                    col = lax.broadcasted_iota(jnp.int32, (tq, tq), 1)
                    s = jnp.where(col <= row, s, NEG)
                m_prev = m_sc[h]
                m_new = jnp.maximum(m_prev, jnp.max(s, axis=1, keepdims=True))
                alpha = jnp.exp(m_prev - m_new)
                p = jnp.exp(s - jnp.tile(m_new, (1, rep)))
                l_sc[h] = alpha * l_sc[h] + jnp.sum(p, axis=1, keepdims=True)
                acc_sc[h] = alpha * acc_sc[h] + _dot(p.astype(BF16), v2, NN)
                m_sc[h] = m_new

        @pl.when(ik < iq)
        def _():
            step(False)

        @pl.when(ik == iq)
        def _():
            step(True)
            y = jnp.where(lo, acc_sc[0] / l_sc[0], acc_sc[1] / l_sc[1])
            y_ref[...] = y.astype(BF16)
            lse_ref[...] = m_sc[...] + jnp.log(l_sc[...])

    def kv(c):
        return pl.BlockSpec((tq, LANE), lambda j, s, qt, kt, c=c: (kt[s], c + j))

    grid_spec = pltpu.PrefetchScalarGridSpec(
        num_scalar_prefetch=2, grid=(4, int(q_tab.shape[0])),
        in_specs=[pl.BlockSpec((tq, LANE), lambda j, s, qt, kt: (qt[s], cq + j)), kv(ck), kv(cv),
                  pl.BlockSpec((2, tq, LANE), lambda j, s, qt, kt: (j, qt[s], 0)),
                  pl.BlockSpec((2, 1, tq), lambda j, s, qt, kt: (j, 0, kt[s]))],
        out_specs=(pl.BlockSpec((tq, LANE), lambda j, s, qt, kt: (qt[s], j)),
                   pl.BlockSpec((2, tq, LANE), lambda j, s, qt, kt: (j, qt[s], 0))),
        scratch_shapes=[pltpu.VMEM((2, tq, LANE), F32)] * 3)
    return pl.pallas_call(
        body, name=name, grid_spec=grid_spec,
        out_shape=(jax.ShapeDtypeStruct((t, BRANCH), BF16), jax.ShapeDtypeStruct((8, t, LANE), F32)),
        compiler_params=_params(("parallel", "arbitrary"),
                                16 * _nbytes((tq, LANE), F32) + 6 * _nbytes((tq, tq), F32)),
    )(q_tab, k_tab, proj, proj, proj, c_col, c_row)


def _fox_bwd(proj, c_col, c_row, lse_row, y, dy, name):
    t = proj.shape[0]
    tb = _pick(t, FOX_TILE, LANE)
    nb = t // tb
    rep = tb // LANE
    scale = HEAD ** -0.5
    cq, ck, cv = COL_FOX // LANE, COL_FOX // LANE + 4, COL_FOX // LANE + 8
    q_tab, k_tab = _causal_steps(nb, True)
    nsteps = int(q_tab.shape[0])

    def body(qt_ref, kt_ref, k_ref, v_ref, q_ref, y_ref, dy_ref, ck_ref, cqr_ref, lser_ref,
             dq_ref, dk_ref, dv_ref, dck_ref, dcq_ref, dk_sc, dv_sc, dc_sc, dqt_sc, dcq_sc, d_sc):
        j, step_id = pl.program_id(0), pl.program_id(1)
        iq, ik = qt_ref[step_id], kt_ref[step_id]
        lo = _lo_mask()

        @pl.when(step_id == 0)
        def _():
            dqt_sc[...] = jnp.zeros_like(dqt_sc)
            dcq_sc[...] = jnp.zeros_like(dcq_sc)

        @pl.when(iq == ik)
        def _():
            dk_sc[...] = jnp.zeros_like(dk_sc)
            dv_sc[...] = jnp.zeros_like(dv_sc)
            dc_sc[...] = jnp.zeros_like(dc_sc)

        @pl.when(ik == 0)
        def _():
            prod = y_ref[...].astype(F32) * dy_ref[...].astype(F32)
            row = lax.broadcasted_iota(jnp.int32, (8, LANE), 0)
            sel = jnp.logical_or(jnp.logical_and(row == 0, lo), jnp.logical_and(row == 1, jnp.logical_not(lo)))
            d_sc[iq] = lax.dot_general(jnp.where(sel, 1.0, 0.0).astype(F32), prod, (NT, ((), ())),
                                       precision=lax.Precision.HIGHEST, preferred_element_type=F32)

        def step(diag):
            k2 = k_ref[...].astype(BF16)
            v2 = v_ref[...].astype(BF16)
            q2 = q_ref[...].astype(BF16)
            do2 = dy_ref[...]
            d_rows = d_sc[iq]
            for h in range(2):
                msk = lo if h == 0 else jnp.logical_not(lo)
                kh = jnp.where(msk, k2, jnp.zeros_like(k2))
                vh = jnp.where(msk, v2, jnp.zeros_like(v2))
                st = _dot(kh, q2, NT) * scale
                st = st + cqr_ref[h] - jnp.tile(ck_ref[h], (1, rep))
                if diag:
                    krow = lax.broadcasted_iota(jnp.int32, (tb, tb), 0)
                    qcol = lax.broadcasted_iota(jnp.int32, (tb, tb), 1)
                    st = jnp.where(krow <= qcol, st, NEG)
                pt = jnp.exp(st - lser_ref[h])
                dpt = _dot(vh, do2, NT)
                dst = pt * (dpt - d_rows[h:h + 1, :])
                dsb = dst.astype(BF16)
                dv_sc[h] += _dot(pt.astype(BF16), do2, NN)
                dk_sc[h] += _dot(dsb, q2, NN)
                dc_sc[h] -= jnp.sum(dst, axis=1, keepdims=True)
                dqt_sc[iq] += _dot(kh, dsb, TN)
                dcq_sc[h, iq] += jnp.sum(dst, axis=0, keepdims=True)

        @pl.when(iq > ik)
        def _():
            step(False)

        @pl.when(iq == ik)
        def _():
            step(True)

        @pl.when(iq == nb - 1)
        def _():
            dk_ref[...] = (jnp.where(lo, dk_sc[0], dk_sc[1]) * scale).astype(BF16)
            dv_ref[...] = jnp.where(lo, dv_sc[0], dv_sc[1]).astype(BF16)
            dck_ref[...] = _head_lanes(j, dc_sc)

        @pl.when(step_id == nsteps - 1)
        def _():
            for i in range(nb):
                dq_ref[i * tb:(i + 1) * tb, :] = (dqt_sc[i].T * scale).astype(BF16)
                for h in range(2):
                    dcq_ref[h, :, i * tb:(i + 1) * tb] = dcq_sc[h, i]

    def kcol(c):
        return pl.BlockSpec((tb, LANE), lambda j, s, qt, kt, c=c: (kt[s], c + j))

    qrow = pl.BlockSpec((2, 1, tb), lambda j, s, qt, kt: (j, 0, qt[s]))
    pair_q = pl.BlockSpec((tb, LANE), lambda j, s, qt, kt: (qt[s], j))
    pair_k = pl.BlockSpec((tb, LANE), lambda j, s, qt, kt: (kt[s], j))
    grid_spec = pltpu.PrefetchScalarGridSpec(
        num_scalar_prefetch=2, grid=(4, nsteps),
        in_specs=[kcol(ck), kcol(cv), pl.BlockSpec((tb, LANE), lambda j, s, qt, kt: (qt[s], cq + j)), pair_q, pair_q,
                  pl.BlockSpec((2, tb, LANE), lambda j, s, qt, kt: (j, kt[s], 0)), qrow, qrow],
        out_specs=(pl.BlockSpec((t, LANE), lambda j, s, qt, kt: (0, j)), pair_k, pair_k,
                   pl.BlockSpec((None, tb, LANE), lambda j, s, qt, kt: (j, kt[s], 0)),
                   pl.BlockSpec((2, 1, t), lambda j, s, qt, kt: (j, 0, 0))),
        scratch_shapes=[pltpu.VMEM((2, tb, LANE), F32)] * 3
        + [pltpu.VMEM((nb, LANE, tb), F32), pltpu.VMEM((2, nb, 1, tb), F32), pltpu.VMEM((nb, 8, tb), F32)])
    return pl.pallas_call(
        body, name=name, grid_spec=grid_spec,
        out_shape=(jax.ShapeDtypeStruct((t, BRANCH), BF16), jax.ShapeDtypeStruct((t, BRANCH), BF16),
                   jax.ShapeDtypeStruct((t, BRANCH), BF16), jax.ShapeDtypeStruct((4, t, LANE), F32),
                   jax.ShapeDtypeStruct((8, 1, t), F32)),
        compiler_params=_params(("parallel", "arbitrary"),
                                24 * _nbytes((tb, LANE), F32) + 8 * _nbytes((tb, tb), F32)
                                + 2 * _nbytes((t, LANE), F32)),
    )(q_tab, k_tab, proj, proj, proj, y, dy, c_col, c_row, lse_row)


def _swa_tables(rel_bias):
    tq = np.arange(SWA_BLOCK)[:, None]
    sk = np.arange(2 * SWA_BLOCK)[None, :]
    dist = SWA_BLOCK + tq - sk
    inwin = (dist >= 0) & (dist < SWA_BLOCK)
    n = np.maximum(dist, 0)
    max_exact = N_BUCKETS // 2
    large = max_exact + (np.log(np.maximum(n, 1).astype(np.float32) / max_exact)
                         / math.log(SWA_BLOCK / max_exact) * (N_BUCKETS - max_exact)).astype(np.int32)
    bucket = np.where(n < max_exact, n, np.minimum(large, N_BUCKETS - 1))
    onehot = (bucket[..., None] == np.arange(N_BUCKETS)) & inwin[..., None]
    onehot = jnp.asarray(onehot.astype(np.float32))
    bias = jnp.einsum("tsb,bh->hts", onehot, rel_bias, precision=lax.Precision.HIGHEST)
    bias = jnp.where(jnp.asarray(inwin)[None], bias, NEG)
    return onehot, bias


def _swa_fwd(proj, bias, sink_rep, name):
    t = proj.shape[0]
    nb = t // SWA_BLOCK
    scale = HEAD ** -0.5
    csq, csk, csv = COL_SQ // 256, COL_SK // LANE, COL_SV // LANE

    def body(q_ref, kp_ref, kc_ref, vp_ref, vc_ref, b_ref, sk_ref, y_ref, lse_ref):
        kvh, n = pl.program_id(0), pl.program_id(1)
        lane = lax.broadcasted_iota(jnp.int32, (1, LANE), 1)
        lo = lane < HEAD
        kvm = jnp.logical_and(lane >= kvh * HEAD, lane < (kvh + 1) * HEAD)

        def both(prev_ref, cur_ref):
            band = jnp.concatenate([prev_ref[...], cur_ref[...]], axis=0)
            band = jnp.where(kvm, band, 0.0)
            return (band + pltpu.roll(band, HEAD, 1)).astype(BF16)

        kb, vb = both(kp_ref, kc_ref), both(vp_ref, vc_ref)
        col = lax.broadcasted_iota(jnp.int32, (SWA_BLOCK, 2 * SWA_BLOCK), 1)
        first = jnp.logical_and(n == 0, col < SWA_BLOCK)
        outs = []
        for g in range(SWA_GROUP):
            half = q_ref[:, (g // 2) * LANE:(g // 2 + 1) * LANE]
            hm = lo if g % 2 == 0 else jnp.logical_not(lo)
            qg = jnp.where(hm, half, 0.0).astype(BF16)
            s = _dot(qg, kb, NT) * scale + b_ref[g]
            s = jnp.where(first, NEG, s)
            snk = sk_ref[g:g + 1, :]
            m = jnp.maximum(jnp.max(s, axis=1, keepdims=True), snk)
            p = jnp.exp(s - jnp.tile(m, (1, 2)))
            denom = jnp.sum(p, axis=1, keepdims=True) + jnp.exp(snk - m)
            outs.append(_dot(p.astype(BF16), vb, NN) / denom)
            lse_ref[g] = m + jnp.log(denom)
        y_ref[:, 0:LANE] = jnp.where(lo, outs[0], outs[1]).astype(BF16)
        y_ref[:, LANE:2 * LANE] = jnp.where(lo, outs[2], outs[3]).astype(BF16)

    def blk(c, shift):
        return pl.BlockSpec((SWA_BLOCK, LANE), lambda kvh, n, c=c, s=shift: (jnp.maximum(n - s, 0), c))

    return pl.pallas_call(
        body, name=name, grid=(2, nb),
        out_shape=(jax.ShapeDtypeStruct((t, BRANCH), BF16), jax.ShapeDtypeStruct((8, t, LANE), F32)),
        in_specs=[pl.BlockSpec((SWA_BLOCK, 256), lambda kvh, n: (n, csq + kvh)),
                  blk(csk, 1), blk(csk, 0), blk(csv, 1), blk(csv, 0),
                  pl.BlockSpec((None, SWA_GROUP, SWA_BLOCK, 256), lambda kvh, n: (kvh, 0, 0, 0)),
                  pl.BlockSpec((None, SWA_GROUP, LANE), lambda kvh, n: (kvh, 0, 0))],
        out_specs=(pl.BlockSpec((SWA_BLOCK, 256), lambda kvh, n: (n, kvh)),
                   pl.BlockSpec((SWA_GROUP, SWA_BLOCK, LANE), lambda kvh, n: (kvh, n, 0))),
        compiler_params=_params(("parallel", "arbitrary"), 4 << 20),
    )(proj, proj, proj, proj, proj, bias.reshape(2, SWA_GROUP, SWA_BLOCK, 256), sink_rep)


def _swa_bwd(proj, bias, sink_rep, lse, y, dy, name):
    t = proj.shape[0]
    nb = t // SWA_BLOCK
    scale = HEAD ** -0.5
    csq, csk, csv = COL_SQ // 256, COL_SK // LANE, COL_SV // LANE

    def body(q_ref, kp_ref, kc_ref, vp_ref, vc_ref, b_ref, sk_ref, lse_ref, y_ref, dy_ref,
             dq_ref, dkp_ref, dvp_ref, db_ref, dsk_ref):
        kvh, n = pl.program_id(0), pl.program_id(1)
        lane = lax.broadcasted_iota(jnp.int32, (1, LANE), 1)
        lo = lane < HEAD
        kvm = jnp.logical_and(lane >= kvh * HEAD, lane < (kvh + 1) * HEAD)

        def both(prev_ref, cur_ref):
            band = jnp.concatenate([prev_ref[...], cur_ref[...]], axis=0)
            band = jnp.where(kvm, band, 0.0)
            return (band + pltpu.roll(band, HEAD, 1)).astype(BF16)

        kb, vb = both(kp_ref, kc_ref), both(vp_ref, vc_ref)
        col = lax.broadcasted_iota(jnp.int32, (SWA_BLOCK, 2 * SWA_BLOCK), 1)
        first = jnp.logical_and(n == 0, col < SWA_BLOCK)

        @pl.when(n == 0)
        def _():
            db_ref[...] = jnp.zeros_like(db_ref)
            dsk_ref[...] = jnp.zeros_like(dsk_ref)

        dk_full = jnp.zeros((2 * SWA_BLOCK, LANE), F32)
        dv_full = jnp.zeros((2 * SWA_BLOCK, LANE), F32)
        dqs = []
        for g in range(SWA_GROUP):
            sl = slice((g // 2) * LANE, (g // 2 + 1) * LANE)
            hm = lo if g % 2 == 0 else jnp.logical_not(lo)
            qg = jnp.where(hm, q_ref[:, sl], 0.0).astype(BF16)
            dog = jnp.where(hm, dy_ref[:, sl], jnp.zeros((SWA_BLOCK, LANE), BF16))
            dmat = jnp.where(hm, y_ref[:, sl].astype(F32) * dy_ref[:, sl].astype(F32), 0.0)
            dg = jnp.sum(dmat, axis=1, keepdims=True)
            s = _dot(qg, kb, NT) * scale + b_ref[g]
            s = jnp.where(first, NEG, s)
            lse_g = lse_ref[g]
            p = jnp.exp(s - jnp.tile(lse_g, (1, 2)))
            dp = _dot(dog, vb, NT)
            ds = p * (dp - dg)
            dsb = ds.astype(BF16)
            dqs.append(_dot(dsb, kb, NN) * scale)
            dk_full = dk_full + _dot(dsb, qg, TN)
            dv_full = dv_full + _dot(p.astype(BF16), dog, TN)
            db_ref[g] += ds
            psink = jnp.exp(sk_ref[g:g + 1, :] - lse_g)
            dsk_ref[g:g + 1, :] -= jnp.sum(psink * dg, axis=0, keepdims=True)
        dq_ref[:, 0:LANE] = jnp.where(lo, dqs[0], dqs[1]).astype(BF16)
        dq_ref[:, LANE:2 * LANE] = jnp.where(lo, dqs[2], dqs[3]).astype(BF16)
        dkp_ref[...] = jnp.where(kvm, (dk_full + pltpu.roll(dk_full, HEAD, 1)) * scale, 0.0)
        dvp_ref[...] = jnp.where(kvm, dv_full + pltpu.roll(dv_full, HEAD, 1), 0.0)

    def blk(c, shift):
        return pl.BlockSpec((SWA_BLOCK, LANE), lambda kvh, n, c=c, s=shift: (jnp.maximum(n - s, 0), c))

    qblk = pl.BlockSpec((SWA_BLOCK, 256), lambda kvh, n: (n, kvh))
    part = pl.BlockSpec((None, None, 2 * SWA_BLOCK, LANE), lambda kvh, n: (kvh, n, 0, 0))
    bspec = pl.BlockSpec((None, SWA_GROUP, SWA_BLOCK, 256), lambda kvh, n: (kvh, 0, 0, 0))
    sspec = pl.BlockSpec((None, SWA_GROUP, LANE), lambda kvh, n: (kvh, 0, 0))
    return pl.pallas_call(
        body, name=name, grid=(2, nb),
        out_shape=(jax.ShapeDtypeStruct((t, BRANCH), BF16),
                   jax.ShapeDtypeStruct((2, nb, 2 * SWA_BLOCK, LANE), F32),
                   jax.ShapeDtypeStruct((2, nb, 2 * SWA_BLOCK, LANE), F32),
                   jax.ShapeDtypeStruct((2, SWA_GROUP, SWA_BLOCK, 256), F32),
                   jax.ShapeDtypeStruct((2, SWA_GROUP, LANE), F32)),
        in_specs=[pl.BlockSpec((SWA_BLOCK, 256), lambda kvh, n: (n, csq + kvh)),
                  blk(csk, 1), blk(csk, 0), blk(csv, 1), blk(csv, 0), bspec, sspec,
                  pl.BlockSpec((SWA_GROUP, SWA_BLOCK, LANE), lambda kvh, n: (kvh, n, 0)), qblk, qblk],
        out_specs=(qblk, part, part, bspec, sspec),
        compiler_params=_params(("parallel", "arbitrary"), 6 << 20),
    )(proj, proj, proj, proj, proj, bias.reshape(2, SWA_GROUP, SWA_BLOCK, 256), sink_rep, lse, y, dy)


def _gate_fwd(proj, pb, name):
    t = proj.shape[0]
    tr = _pick(t, ROW_TILE // 2, BF16_SUBLANE)

    def body(g0, g1, g2, p0, p1, p2, o_ref):
        acc = jax.nn.sigmoid(g0[...]) * p0[...]
        acc = acc + jax.nn.sigmoid(g1[...]) * p1[...]
        acc = acc + jax.nn.sigmoid(g2[...]) * p2[...]
        o_ref[...] = acc.astype(BF16)

    row = pl.BlockSpec((tr, D_MODEL), lambda i: (i, 0))
    gates = [pl.BlockSpec((tr, D_MODEL), lambda i, b=b: (i, b)) for b in range(3)]
    return pl.pallas_call(
        body, name=name, grid=(t // tr,),
        out_shape=jax.ShapeDtypeStruct((t, D_MODEL), BF16),
        in_specs=gates + [row] * 3, out_specs=row,
        compiler_params=_params(("parallel",), 7 * _nbytes((tr, D_MODEL), F32)),
    )(proj, proj, proj, *pb)


def _gate_bwd(proj, pb, dmerged, name):
    t = proj.shape[0]
    tr = _pick(t, ROW_TILE // 2, BF16_SUBLANE)

    def body(g0, g1, g2, p0, p1, p2, dm_ref, dp0, dp1, dp2, dg_ref):
        dm = dm_ref[...]
        for b, (g_ref, p_ref, dp_ref) in enumerate(((g0, p0, dp0), (g1, p1, dp1), (g2, p2, dp2))):
            sg = jax.nn.sigmoid(g_ref[...])
            dp_ref[...] = (dm * sg).astype(BF16)
            dg_ref[:, b * D_MODEL:(b + 1) * D_MODEL] = (dm * p_ref[...] * sg * (1.0 - sg)).astype(BF16)

    row = pl.BlockSpec((tr, D_MODEL), lambda i: (i, 0))
    gates = [pl.BlockSpec((tr, D_MODEL), lambda i, b=b: (i, b)) for b in range(3)]
    return pl.pallas_call(
        body, name=name, grid=(t // tr,),
        out_shape=(jax.ShapeDtypeStruct((t, D_MODEL), BF16),) * 3 + (jax.ShapeDtypeStruct((t, 3 * D_MODEL), BF16),),
        in_specs=gates + [row] * 4,
        out_specs=(row, row, row, pl.BlockSpec((tr, 3 * D_MODEL), lambda i: (i, 0))),
        compiler_params=_params(("parallel",), 11 * _nbytes((tr, D_MODEL), F32)),
    )(proj, proj, proj, *pb, dmerged)


def _swiglu_fwd(ab, name):
    t = ab.shape[0]
    tr = _pick(t, ROW_TILE, BF16_SUBLANE)
    tc = D_FF // 2

    def body(a_ref, b_ref, o_ref):
        a = a_ref[...]
        o_ref[...] = (a * jax.nn.sigmoid(a) * b_ref[...]).astype(BF16)

    return pl.pallas_call(
        body, name=name, grid=(t // tr, 2),
        out_shape=jax.ShapeDtypeStruct((t, D_FF), BF16),
        in_specs=[pl.BlockSpec((tr, tc), lambda i, j: (i, j)), pl.BlockSpec((tr, tc), lambda i, j: (i, j + 2))],
        out_specs=pl.BlockSpec((tr, tc), lambda i, j: (i, j)),
        compiler_params=_params(("parallel", "parallel"), 3 * _nbytes((tr, tc), F32)),
    )(ab, ab)


def _swiglu_bwd(ab, dh, name):
    t = ab.shape[0]
    tr = _pick(t, ROW_TILE, BF16_SUBLANE)
    tc = D_FF // 2

    def body(a_ref, b_ref, dh_ref, o_ref):
        jj = pl.program_id(1)
        a, b, d = a_ref[...], b_ref[...], dh_ref[...]
        sg = jax.nn.sigmoid(a)
        da = d * b * (sg * (1.0 + a * (1.0 - sg)))
        db = d * (a * sg)
        o_ref[...] = jnp.where(jj < 2, da, db).astype(BF16)

    return pl.pallas_call(
        body, name=name, grid=(t // tr, 4),
        out_shape=jax.ShapeDtypeStruct((t, 2 * D_FF), BF16),
        in_specs=[pl.BlockSpec((tr, tc), lambda i, j: (i, j % 2)),
                  pl.BlockSpec((tr, tc), lambda i, j: (i, j % 2 + 2)),
                  pl.BlockSpec((tr, tc), lambda i, j: (i, j % 2))],
        out_specs=pl.BlockSpec((tr, tc), lambda i, j: (i, j)),
        compiler_params=_params(("parallel", "parallel"), 4 * _nbytes((tr, tc), F32)),
    )(ab, ab, dh)


def _xattn_fwd(q, kv, name):
    t = q.shape[0]
    tq = _pick(t, ROW_TILE, BF16_SUBLANE)
    mlen = kv.shape[0]
    scale = X_HEAD ** -0.5

    def body(q_ref, kv_ref, o_ref):
        for h in range(X_HEADS):
            sl = slice(h * X_HEAD, (h + 1) * X_HEAD)
            kh = kv_ref[:, sl]
            vh = kv_ref[:, D_MODEL + h * X_HEAD:D_MODEL + (h + 1) * X_HEAD]
            s = _dot(q_ref[:, sl], kh, NT) * scale
            p = jnp.exp(s - jnp.max(s, axis=1, keepdims=True))
            l = jnp.sum(p, axis=1, keepdims=True)
            o_ref[:, sl] = (_dot(p.astype(BF16), vh, NN) / l).astype(BF16)

    return pl.pallas_call(
        body, name=name, grid=(t // tq,),
        out_shape=jax.ShapeDtypeStruct((t, D_MODEL), BF16),
        in_specs=[pl.BlockSpec((tq, D_MODEL), lambda i: (i, 0)), pl.BlockSpec((mlen, 2 * D_MODEL), lambda i: (0, 0))],
        out_specs=pl.BlockSpec((tq, D_MODEL), lambda i: (i, 0)),
        compiler_params=_params(("parallel",), 4 * _nbytes((tq, D_MODEL), F32)),
    )(q, kv)


def _xattn_bwd(q, kv, do, name):
    t = q.shape[0]
    tq = _pick(t, ROW_TILE, BF16_SUBLANE)
    mlen = kv.shape[0]
    scale = X_HEAD ** -0.5

    def body(q_ref, kv_ref, do_ref, dq_ref, dkv_ref):
        i = pl.program_id(0)

        @pl.when(i == 0)
        def _():
            dkv_ref[...] = jnp.zeros_like(dkv_ref)

        for h in range(X_HEADS):
            sl = slice(h * X_HEAD, (h + 1) * X_HEAD)
            vsl = slice(D_MODEL + h * X_HEAD, D_MODEL + (h + 1) * X_HEAD)
            qh, kh, vh, doh = q_ref[:, sl], kv_ref[:, sl], kv_ref[:, vsl], do_ref[:, sl]
            s = _dot(qh, kh, NT) * scale
            p = jnp.exp(s - jnp.max(s, axis=1, keepdims=True))
            p = p / jnp.sum(p, axis=1, keepdims=True)
            dp = _dot(doh, vh, NT)
            ds = p * (dp - jnp.sum(p * dp, axis=1, keepdims=True))
            dsb = ds.astype(BF16)
            dq_ref[:, sl] = (_dot(dsb, kh, NN) * scale).astype(BF16)
            dkv_ref[:, sl] += _dot(dsb, qh, TN) * scale
            dkv_ref[:, vsl] += _dot(p.astype(BF16), doh, TN)

    row = pl.BlockSpec((tq, D_MODEL), lambda i: (i, 0))
    whole = pl.BlockSpec((mlen, 2 * D_MODEL), lambda i: (0, 0))
    return pl.pallas_call(
        body, name=name, grid=(t // tq,),
        out_shape=(jax.ShapeDtypeStruct((t, D_MODEL), BF16), jax.ShapeDtypeStruct((mlen, 2 * D_MODEL), F32)),
        in_specs=[row, whole, row], out_specs=(row, whole),
        compiler_params=_params(("arbitrary",), 6 * _nbytes((tq, D_MODEL), F32)),
    )(q, kv, do)


def _position():
    return lax.axis_index("x"), lax.axis_index("y"), lax.axis_index("c")


N_PEER = N_DEV - 1


def _all_gather(xs, name):
    n = len(xs)

    def body(*refs):
        x_refs, out_refs = refs[:n], refs[n:2 * n]
        send_sems, recv_sems, local_sems = refs[2 * n:]
        mx, my, mc = _position()
        me, sib = (mx, my, mc), (mx, my, 1 - mc)
        chips = [(1 - mx, my), (mx, 1 - my), (1 - mx, 1 - my)]

        def slot(i, p):
            return out_refs[i].at[4 * p[0] + 2 * p[1] + p[2]]

        def copy(i, k, block, to, src=None):
            return pltpu.make_async_remote_copy(
                src_ref=slot(i, block) if src is None else src, dst_ref=slot(i, block),
                send_sem=send_sems.at[i * N_PEER + k], recv_sem=recv_sems.at[i * N_PEER + k],
                device_id=to, device_id_type=MESH)

        mine = [pltpu.make_async_copy(x_refs[i], slot(i, me), local_sems.at[i]) for i in range(n)]
        for cp in mine:
            cp.start()
        first = [copy(i, 1 + j, me, (*chip, mc), src=x_refs[i]) for j, chip in enumerate(chips) for i in range(n)]
        first += [copy(i, 0, me, sib, src=x_refs[i]) for i in range(n)]
        for cp in first:
            cp.start()
        passed = []
        for j, chip in enumerate(chips):
            for i in range(n):
                copy(i, 1 + j, (*chip, mc), me).wait_recv()
                passed.append(copy(i, 4 + j, (*chip, mc), sib))
                passed[-1].start()
        for i in range(n):
            copy(i, 0, sib, me).wait_recv()
        for j, chip in enumerate(chips):
            for i in range(n):
                copy(i, 4 + j, (*chip, 1 - mc), me).wait_recv()
        for cp in first + passed:
            cp.wait_send()
        for cp in mine:
            cp.wait()

    return pl.pallas_call(
        body, name=name,
        out_shape=tuple(jax.ShapeDtypeStruct((N_DEV,) + x.shape, x.dtype) for x in xs),
        in_specs=[pl.BlockSpec(memory_space=pl.ANY)] * n, out_specs=(pl.BlockSpec(memory_space=pl.ANY),) * n,
        scratch_shapes=[pltpu.SemaphoreType.DMA((n * N_PEER,)), pltpu.SemaphoreType.DMA((n * N_PEER,)),
                        pltpu.SemaphoreType.DMA((n,))],
    )(*xs)


PEER_RELS = [(dx, dy, dc) for dx in (0, 1) for dy in (0, 1) for dc in (0, 1)][1:]


def _peer_copy(rel_k, i, src_refs, land_refs, send_sems, recv_sems, layer, gather, arriving):
    mx, my, mc = _position()
    me_idx = 4 * mx + 2 * my + mc
    p = tuple((1 - v) if f else v for f, v in zip(PEER_RELS[rel_k], (mx, my, mc)))
    p_idx = 4 * p[0] + 2 * p[1] + p[2]
    src_slot, dst_slot = (me_idx, p_idx) if arriving else (p_idx, me_idx)
    src = src_refs[i] if gather else src_refs[i].at[src_slot]
    dst = land_refs[i].at[dst_slot] if layer is None else land_refs[i].at[dst_slot, layer]
    return pltpu.make_async_remote_copy(
        src_ref=src, dst_ref=dst, send_sem=send_sems.at[i * N_PEER + rel_k], recv_sem=recv_sems.at[i * N_PEER + rel_k],
        device_id=p, device_id_type=MESH)


def _own_copy(i, src_refs, land_refs, sem, layer, gather):
    mx, my, mc = _position()
    me_idx = 4 * mx + 2 * my + mc
    src = src_refs[i] if gather else src_refs[i].at[me_idx]
    dst = land_refs[i].at[me_idx] if layer is None else land_refs[i].at[me_idx, layer]
    return pltpu.make_async_copy(src, dst, sem)


def _exchange(parts, lands, layer, name):
    n = len(parts)

    def body(*refs):
        g_refs, land_refs = refs[:n], refs[2 * n:3 * n]
        send_sems, recv_sems, local_sems = refs[3 * n:]
        args = (g_refs, land_refs, send_sems, recv_sems, layer, False)
        mine = [_own_copy(i, g_refs, land_refs, local_sems.at[i], layer, False) for i in range(n)]
        for cp in mine:
            cp.start()
        sends = [_peer_copy(k, i, *args, False) for i in range(n) for k in range(N_PEER)]
        for cp in sends:
            cp.start()
        for i in range(n):
            for k in range(N_PEER):
                _peer_copy(k, i, *args, True).wait_recv()
        for cp in sends:
            cp.wait_send()
        for cp in mine:
            cp.wait()

    return pl.pallas_call(
        body, name=name,
        out_shape=tuple(jax.ShapeDtypeStruct(l.shape, l.dtype) for l in lands),
        in_specs=[pl.BlockSpec(memory_space=pl.ANY)] * (2 * n), out_specs=(pl.BlockSpec(memory_space=pl.ANY),) * n,
        input_output_aliases={n + i: i for i in range(n)},
        scratch_shapes=[pltpu.SemaphoreType.DMA((n * N_PEER,)), pltpu.SemaphoreType.DMA((n * N_PEER,)),
                        pltpu.SemaphoreType.DMA((n,))],
    )(*parts, *lands)


HBM_SPEC = pl.BlockSpec(memory_space=pltpu.HBM)
SEM_SPEC = pl.BlockSpec(memory_space=pltpu.SEMAPHORE)
SIDE_EFFECT = pltpu.SideEffectType.DATAFLOW_SIDE_EFFECTING


def _own_slots(srcs, lands, layer, gather, name):
    n = len(srcs)

    def body(*refs):
        src_refs, land_refs, sems = refs[:n], refs[2 * n:3 * n], refs[3 * n]
        mine = [_own_copy(i, src_refs, land_refs, sems.at[i], layer, gather) for i in range(n)]
        for cp in mine:
            cp.start()
        for cp in mine:
            cp.wait()

    return list(pl.pallas_call(
        body, name=name,
        out_shape=tuple(jax.ShapeDtypeStruct(l.shape, l.dtype) for l in lands),
        in_specs=[pl.BlockSpec(memory_space=pl.ANY)] * (2 * n), out_specs=(pl.BlockSpec(memory_space=pl.ANY),) * n,
        input_output_aliases={n + i: i for i in range(n)},
        scratch_shapes=[pltpu.SemaphoreType.DMA((n,))],
    )(*srcs, *lands))


def _swap_start(srcs, lands, layer, gather, name):
    n = len(srcs)

    def body(*refs):
        src_refs, land_refs = refs[:n], refs[n:2 * n]
        send_sems, recv_sems = refs[2 * n], refs[2 * n + 1]
        token = refs[4 * n + 2]
        for i in range(n):
            for k in range(N_PEER):
                _peer_copy(k, i, src_refs, land_refs, send_sems, recv_sems, layer, gather, False).start()
        token[...] = jnp.zeros_like(token)

    hbm = [pltpu.with_memory_space_constraint(a, pltpu.HBM) for a in list(srcs) + list(lands)]
    out = pl.pallas_call(
        body, name=name,
        out_shape=(pltpu.SemaphoreType.DMA((n * N_PEER,)), pltpu.SemaphoreType.DMA((n * N_PEER,)))
        + tuple(pltpu.HBM(a.shape, a.dtype) for a in hbm) + (jax.ShapeDtypeStruct((8, LANE), F32),),
        in_specs=[HBM_SPEC] * (2 * n),
        out_specs=(SEM_SPEC, SEM_SPEC) + (HBM_SPEC,) * (2 * n) + (pl.BlockSpec(memory_space=pltpu.VMEM),),
        input_output_aliases={i: 2 + i for i in range(2 * n)},
        compiler_params=pltpu.CompilerParams(has_side_effects=SIDE_EFFECT),
    )(*hbm)
    return out[0], out[1], list(out[2:2 + n]), list(out[2 + n:2 + 2 * n]), out[2 + 2 * n]


def _swap_wait(send_sems, recv_sems, srcs, lands, after, layer, gather, name):
    n = len(srcs)

    def body(*refs):
        src_refs, land_refs = refs[:n], refs[n:2 * n]
        send_sems_ref, recv_sems_ref = refs[2 * n], refs[2 * n + 1]
        for i in range(n):
            for k in range(N_PEER):
                args = (src_refs, land_refs, send_sems_ref, recv_sems_ref, layer, gather)
                _peer_copy(k, i, *args, False).wait_send()
                _peer_copy(k, i, *args, True).wait_recv()

    out = pl.pallas_call(
        body, name=name,
        out_shape=tuple(pltpu.HBM(a.shape, a.dtype) for a in list(srcs) + list(lands)),
        in_specs=[HBM_SPEC] * (2 * n) + [SEM_SPEC, SEM_SPEC, pl.BlockSpec(memory_space=pl.ANY)],
        out_specs=(HBM_SPEC,) * (2 * n),
        input_output_aliases={i: i for i in range(2 * n)},
        compiler_params=pltpu.CompilerParams(has_side_effects=SIDE_EFFECT),
    )(*srcs, *lands, send_sems, recv_sems, after)
    return list(out[n:])


ADAMW_BLOCK_BYTES = 1 << 20


def _adamw(parts, w, m, v, name):
    r, l = w.shape
    tr = _pick(r, max(ADAMW_BLOCK_BYTES // (4 * l), BF16_SUBLANE), BF16_SUBLANE)
    c1 = 1.0 - ADAM_B1 ** ADAM_STEP
    c2 = 1.0 - ADAM_B2 ** ADAM_STEP

    def body(p_ref, w_ref, m_ref, v_ref, g_ref, d_ref, nm_ref, nv_ref):
        g = p_ref[0].astype(F32)
        for s in range(1, N_DEV):
            g = g + p_ref[s].astype(F32)
        nm = ADAM_B1 * m_ref[...] + (1.0 - ADAM_B1) * g
        nv = ADAM_B2 * v_ref[...] + (1.0 - ADAM_B2) * (g * g)
        m_hat = nm / c1
        v_hat = nv / c2
        g_ref[...] = g
        d_ref[...] = -ADAM_LR * (m_hat / (jnp.sqrt(v_hat) + ADAM_EPS) + ADAM_WD * w_ref[...])
        nm_ref[...] = nm
        nv_ref[...] = nv

    row = pl.BlockSpec((tr, l), lambda i: (i, 0))
    return pl.pallas_call(
        body, name=name, grid=(r // tr,),
        out_shape=(jax.ShapeDtypeStruct((r, l), F32),) * 4,
        in_specs=[pl.BlockSpec((N_DEV, tr, l), lambda i: (0, i, 0)), row, row, row],
        out_specs=(row,) * 4,
        compiler_params=_params(("parallel",), 12 * _nbytes((tr, l), F32)),
    )(parts, w, m, v)


MATRIX_WEIGHTS = (("w_in", 2), ("conv_w", 2), ("w_branch", 3), ("w_mix_out", 1), ("w_xq", 1), ("w_xkv", 2),
                  ("w_xo", 1), ("w_ffn_gate", 2), ("w_ffn_up", 2), ("w_ffn_down", 1))
SMALL_PARAMS = ("mix_norm_g", "xattn_norm_g", "mem_norm_g", "ffn_norm_g", "final_norm_g", "forget_bias", "sink",
                "rel_bias")


def _pack_small(pieces):
    flat = jnp.concatenate([p.astype(F32).reshape(-1) for p in pieces])
    total = -(-flat.shape[0] // (8 * LANE)) * (8 * LANE)
    return jnp.pad(flat, (0, total - flat.shape[0])).reshape(total // LANE, LANE)


def _rows(a):
    return a.reshape(-1, a.shape[-1])


def _to_full(gathered, axis):
    moved = jnp.moveaxis(gathered, 0, axis)
    shape = list(moved.shape)
    shape[axis:axis + 2] = [shape[axis] * shape[axis + 1]]
    return moved.reshape(shape)


def _to_blocks(full, axis):
    shape = list(full.shape)
    shape[axis:axis + 1] = [N_DEV, shape[axis] // N_DEV]
    return jnp.moveaxis(full.reshape(shape), axis, 0)


def _perm_in(w_in):
    pad = jnp.zeros((w_in.shape[0], PROJ_COLS - IN_COLS), w_in.dtype)
    return jnp.concatenate([w_in[:, 3848:6920], w_in[:, 0:3072], w_in[:, 3080:3848], w_in[:, 3072:3080], pad], axis=1)


def _unperm_in(dw):
    return jnp.concatenate([dw[:, 3072:6144], dw[:, 6912:6920], dw[:, 6144:6912], dw[:, 0:3072]], axis=1)


def _layer_fwd(l, x, mem, wt, sm):
    t = x.shape[0]
    tag = f"l{l}_"
    h = _rms_fwd(x, sm["mix_norm_g"][l], tag + "mix_norm")
    proj = _matmul(h, wt["w_in"][l], "nn", F32, tag + "in_proj")
    y_conv = _conv_fwd(proj, wt["conv_w"][l], tag + "conv")
    fbias_row = jnp.pad(sm["forget_bias"][l], (0, LANE - 8)).reshape(1, LANE)
    c = _logf_cumsum(proj, fbias_row, tag + "logf_cumsum")
    c8 = c[:, :8].T
    c_col = jnp.broadcast_to(c8[:, :, None], (8, t, LANE))
    c_row = c8.reshape(8, 1, t)
    y_fox, lse_fox = _fox_fwd(proj, c_col, c_row, tag + "fox")
    onehot, bias = _swa_tables(sm["rel_bias"])
    sink_rep = jnp.broadcast_to(sm["sink"][l].reshape(2, SWA_GROUP, 1), (2, SWA_GROUP, LANE))
    y_swa, lse_swa = _swa_fwd(proj, bias, sink_rep, tag + "swa")
    ys = (y_conv, y_fox, y_swa)
    pb = tuple(_matmul(ys[b], wt["w_branch"][l][b], "nn", F32, tag + f"branch{b}") for b in range(3))
    merged = _gate_fwd(proj, pb, tag + "gate")
    x1 = _matmul(merged, wt["w_mix_out"][l], "nn", F32, tag + "mix_out", residual=x)
    xn2 = _rms_fwd(x1, sm["xattn_norm_g"][l], tag + "xattn_norm")
    q = _matmul(xn2, wt["w_xq"][l], "nn", BF16, tag + "xq")
    mem_n = _rms_fwd(mem, sm["mem_norm_g"][l], tag + "mem_norm")
    kv = _matmul(mem_n, wt["w_xkv"][l], "nn", BF16, tag + "xkv")
    o = _xattn_fwd(q, kv, tag + "xattn")
    x2 = _matmul(o, wt["w_xo"][l], "nn", F32, tag + "xo", residual=x1)
    xn3 = _rms_fwd(x2, sm["ffn_norm_g"][l], tag + "ffn_norm")
    ab = _matmul(xn3, wt["w_gu"][l], "nn", F32, tag + "ffn_gu")
    h1 = _swiglu_fwd(ab, tag + "swiglu")
    x3 = _matmul(h1, wt["w_ffn_down"][l], "nn", F32, tag + "ffn_down", residual=x2)
    saved = dict(x=x, h=h, proj=proj, fbias_row=fbias_row, c_col=c_col, c_row=c_row, ys=ys, lse_fox=lse_fox,
                 onehot=onehot, bias=bias, sink_rep=sink_rep, lse_swa=lse_swa, pb=pb, merged=merged, x1=x1,
                 xn2=xn2, q=q, mem_n=mem_n, kv=kv, o=o, x2=x2, xn3=xn3, ab=ab, h1=h1)
    return x3, saved


def _layer_bwd(l, dx3, dx3_b, mem, wt, sm, sv):
    t = dx3.shape[0]
    nb = t // SWA_BLOCK
    tag = f"l{l}_b_"
    gw, gs = {}, {}
    dh1 = _matmul(dx3_b, wt["w_ffn_down"][l], "nt", F32, tag + "d_h1")
    gw["w_ffn_down"] = _matmul(sv["h1"], dx3_b, "tn", F32, tag + "dw_down")
    dab = _swiglu_bwd(sv["ab"], dh1, tag + "swiglu")
    dxn3 = _matmul(dab, wt["w_gu"][l], "nt", F32, tag + "d_xn3")
    dw_gu = _matmul(sv["xn3"], dab, "tn", F32, tag + "dw_gu")
    gw["w_ffn_gate"], gw["w_ffn_up"] = dw_gu[:, :D_FF], dw_gu[:, D_FF:]
    dx2, dx2_b, gs["ffn_norm_g"] = _rms_bwd(sv["x2"], sm["ffn_norm_g"][l], dxn3, dx3, tag + "ffn_norm")
    do = _matmul(dx2_b, wt["w_xo"][l], "nt", BF16, tag + "d_o")
    gw["w_xo"] = _matmul(sv["o"], dx2_b, "tn", F32, tag + "dw_xo")
    dq, dkv = _xattn_bwd(sv["q"], sv["kv"], do, tag + "xattn")
    gw["w_xkv"] = _matmul(sv["mem_n"], dkv, "tn", F32, tag + "dw_xkv")
    dmem_n = _matmul(dkv, wt["w_xkv"][l], "nt", F32, tag + "d_memn")
    _, _, gs["mem_norm_g"] = _rms_bwd(mem, sm["mem_norm_g"][l], dmem_n, None, tag + "mem_norm")
    gw["w_xq"] = _matmul(sv["xn2"], dq, "tn", F32, tag + "dw_xq")
    dxn2 = _matmul(dq, wt["w_xq"][l], "nt", F32, tag + "d_xn2")
    dx1, dx1_b, gs["xattn_norm_g"] = _rms_bwd(sv["x1"], sm["xattn_norm_g"][l], dxn2, dx2, tag + "xattn_norm")
    dmerged = _matmul(dx1_b, wt["w_mix_out"][l], "nt", F32, tag + "d_merged")
    gw["w_mix_out"] = _matmul(sv["merged"], dx1_b, "tn", F32, tag + "dw_mix_out")
    dp0, dp1, dp2, dgate = _gate_bwd(sv["proj"], sv["pb"], dmerged, tag + "gate")
    dps = (dp0, dp1, dp2)
    dy_dtypes = (F32, BF16, BF16)
    dys = [_matmul(dps[b], wt["w_branch"][l][b], "nt", dy_dtypes[b], tag + f"d_y{b}") for b in range(3)]
    gw["w_branch"] = jnp.stack(
        [_matmul(sv["ys"][b], dps[b], "tn", F32, tag + f"dw_branch{b}") for b in range(3)])
    dsq, dkp, dvp, dbias, dsink = _swa_bwd(sv["proj"], sv["bias"], sv["sink_rep"], sv["lse_swa"], sv["ys"][2],
                                           dys[2], tag + "swa")

    def band_add(part):
        tot = part[0] + part[1]
        cur = tot[:, SWA_BLOCK:, :]
        nxt = jnp.concatenate([tot[1:, :SWA_BLOCK, :], jnp.zeros((1, SWA_BLOCK, LANE), F32)], axis=0)
        return (cur + nxt).reshape(t, LANE).astype(BF16)

    dsk, dsv = band_add(dkp), band_add(dvp)
    gs["rel_bias_l"] = jnp.einsum("hts,tsb->bh", dbias.reshape(8, SWA_BLOCK, 2 * SWA_BLOCK), sv["onehot"],
                                  precision=lax.Precision.HIGHEST)
    gs["sink"] = dsink[:, :, 0].reshape(8)
    lse_row = sv["lse_fox"][:, :, 0].reshape(8, 1, t)
    dfq, dfk, dfv, dck, dcq_row = _fox_bwd(sv["proj"], sv["c_col"], sv["c_row"], lse_row, sv["ys"][1], dys[1],
                                           tag + "fox_bwd")
    dcq = jnp.pad(dcq_row.reshape(8, t).T, ((0, 0), (0, LANE - 8))).reshape(1, t, LANE)
    dfg, dfb = _logf_cumsum_bwd(sv["proj"], sv["fbias_row"], [dck, dcq], tag + "logf_cumsum")
    gs["forget_bias"] = dfb[0, :8]
    dcb, dcc, dcu, dconv = _conv_bwd(sv["proj"], wt["conv_w"][l], dys[0], tag + "conv")
    gw["conv_w"] = dconv[:3]
    dproj = jnp.concatenate([dgate, dcb, dcc, dcu, dfq, dfk, dfv, dsq, dsk, dsv, dfg], axis=1)
    dh = _matmul(dproj, wt["w_in"][l], "nt", F32, tag + "d_h")
    gw["w_in"] = _unperm_in(_matmul(sv["h"], dproj, "tn", F32, tag + "dw_in"))
    dx, dx_b, gs["mix_norm_g"] = _rms_bwd(sv["x"], sm["mix_norm_g"][l], dh, dx1, tag + "mix_norm")
    return dx, dx_b, gw, gs


def kernel(x, mem, mix_norm_g, w_in, forget_bias, conv_w, sink, w_branch, w_mix_out, rel_bias, xattn_norm_g, mem_norm_g, w_xq, w_xkv, w_xo, ffn_norm_g, w_ffn_gate, w_ffn_up, w_ffn_down, final_norm_g, loss_target, m_mix_norm_g, m_w_in, m_forget_bias, m_conv_w, m_sink, m_w_branch, m_w_mix_out, m_rel_bias, m_xattn_norm_g, m_mem_norm_g, m_w_xq, m_w_xkv, m_w_xo, m_ffn_norm_g, m_w_ffn_gate, m_w_ffn_up, m_w_ffn_down, m_final_norm_g, v_mix_norm_g, v_w_in, v_forget_bias, v_conv_w, v_sink, v_w_branch, v_w_mix_out, v_rel_bias, v_xattn_norm_g, v_mem_norm_g, v_w_xq, v_w_xkv, v_w_xo, v_ffn_norm_g, v_w_ffn_gate, v_w_ffn_up, v_w_ffn_down, v_final_norm_g):
    args = dict(locals())
    names = [n for n, _ in MATRIX_WEIGHTS] + list(SMALL_PARAMS)
    w = {n: args[n] for n in names}
    mo = {n: args["m_" + n] for n in names}
    vo = {n: args["v_" + n] for n in names}
    x2d, mem2d, tgt = x[0], mem[0], loss_target[0]

    wire = {n: (F32 if n == "conv_w" else BF16) for n, _ in MATRIX_WEIGHTS}
    wt = {n: [None] * DEPTH for n, _ in MATRIX_WEIGHTS}
    wt["w_gu"] = [None] * DEPTH

    def place_weights(l, gathered):
        for (n, ax), g in zip(MATRIX_WEIGHTS, gathered):
            wt[n][l] = _to_full(g, ax - 1)
        wt["w_in"][l] = _perm_in(wt["w_in"][l])
        wt["w_gu"][l] = jnp.concatenate([wt["w_ffn_gate"][l], wt["w_ffn_up"][l]], axis=1)

    shards = [[w[n][l].astype(wire[n]) for n, _ in MATRIX_WEIGHTS] for l in range(DEPTH)]
    place_weights(0, _all_gather(shards[0], "weights_gather_l0"))
    lands = [lax.empty((N_DEV,) + s.shape, s.dtype) for s in shards[1]]
    lands = _own_slots(shards[1], lands, None, True, "weights_gather_l1_own")
    w_send, w_recv, w_srcs, lands, token = _swap_start(shards[1], lands, None, True, "weights_gather_l1_start")
    sm = {n: w[n] for n in SMALL_PARAMS}
    sm["mix_norm_g"] = w["mix_norm_g"].at[0].add(token[0, 0])

    saved = []
    xc = x2d
    for l in range(DEPTH):
        if l == 1:
            place_weights(1, _swap_wait(w_send, w_recv, w_srcs, lands, xc, None, True, "weights_gather_l1_wait"))
        xc, sv = _layer_fwd(l, xc, mem2d, wt, sm)
        saved.append(sv)
    loss_row, dx, dx_b, dg_final = _loss_head(xc, sm["final_norm_g"], tgt, "loss_head")
    loss = lax.psum(loss_row[0, 0], ("x", "y", "c"))

    def grad_parts(gw):
        return [_to_blocks(gw[n], ax - 1).astype(wire[n]) for n, ax in MATRIX_WEIGHTS]

    gw_all, gs_all = [None] * DEPTH, [None] * DEPTH
    dx, dx_b, gw_all[1], gs_all[1] = _layer_bwd(1, dx, dx_b, mem2d, wt, sm, saved[1])
    parts1 = grad_parts(gw_all[1])
    zones = [lax.empty((N_DEV, DEPTH) + p.shape[1:], p.dtype) for p in parts1]
    zones = _own_slots(parts1, zones, 1, False, "grads_exchange_l1_own")
    g_send, g_recv, g_srcs, zones, token = _swap_start(parts1, zones, 1, False, "grads_exchange_l1_start")
    sm_b = dict(sm)
    sm_b["ffn_norm_g"] = sm["ffn_norm_g"].at[0].add(token[0, 0])
    dx, dx_b, gw_all[0], gs_all[0] = _layer_bwd(0, dx, dx_b, mem2d, wt, sm_b, saved[0])
    grad_x = dx[None]
    zones = _swap_wait(g_send, g_recv, g_srcs, zones, dx, 1, False, "grads_exchange_l1_wait")
    recv = _exchange(grad_parts(gw_all[0]), zones, 0, "grads_exchange_l0")

    outs = {}
    for (n, _), r in zip(MATRIX_WEIGHTS, recv):
        res = _adamw(r.reshape((N_DEV,) + _rows(w[n]).shape), _rows(w[n]), _rows(mo[n]), _rows(vo[n]), "adamw_" + n)
        outs[n] = [o.reshape(w[n].shape) for o in res]

    gsm = {n: jnp.stack([gs_all[l][n] for l in range(DEPTH)])
           for n in ("mix_norm_g", "xattn_norm_g", "mem_norm_g", "ffn_norm_g", "forget_bias", "sink")}
    gsm["final_norm_g"] = dg_final
    gsm["rel_bias"] = gs_all[0]["rel_bias_l"] + gs_all[1]["rel_bias_l"]
    (small_parts,) = _all_gather([_pack_small([gsm[n] for n in SMALL_PARAMS])], "small_grads_all_gather")
    outs_small = _adamw(small_parts, *[_pack_small([d[n] for n in SMALL_PARAMS]) for d in (w, mo, vo)], "adamw_small")
    for kind in range(4):
        flat, o = outs_small[kind].reshape(-1), 0
        for n in SMALL_PARAMS:
            sz = int(np.prod(w[n].shape))
            outs.setdefault(n, []).append(flat[o:o + sz].reshape(w[n].shape))
            o += sz

    order = ["mix_norm_g", "w_in", "forget_bias", "conv_w", "sink", "w_branch", "w_mix_out", "rel_bias",
             "xattn_norm_g", "mem_norm_g", "w_xq", "w_xkv", "w_xo", "ffn_norm_g", "w_ffn_gate", "w_ffn_up",
             "w_ffn_down", "final_norm_g"]
    result = [loss, grad_x]
    for kind in range(4):
        result += [outs[n][kind] for n in order]
    return tuple(result)
```

```python
import math

import numpy as np
import jax
import jax.numpy as jnp
from jax import lax
from jax.experimental import pallas as pl
from jax.experimental.pallas import tpu as pltpu

F32 = jnp.float32
BF16 = jnp.bfloat16
MESH = pl.DeviceIdType.MESH

LANE = 128
BF16_SUBLANE = 16
V7X_VMEM_REQUEST_CAP = 56 * 2 ** 20
N_DEV = 8

D_MODEL = 1024
DEPTH = 2
HEAD = 64
BRANCH = 512
SWA_BLOCK = 128
SWA_GROUP = 4
N_BUCKETS = 32
X_HEADS = 4
X_HEAD = 256
D_FF = 2816
RMS_EPS = 1e-6
NEG = -1e30
ADAM_LR, ADAM_B1, ADAM_B2, ADAM_EPS, ADAM_WD, ADAM_STEP = 0.001, 0.9, 0.999, 1e-08, 0.01, 10

IN_COLS = 6920
PROJ_COLS = 7040
COL_GATE, COL_CONV, COL_FOX, COL_SQ, COL_SK, COL_SV, COL_FG = 0, 3072, 4608, 6144, 6656, 6784, 6912

ROW_TILE = 512
FOX_TILE = 512
MM_TM, MM_TN, MM_TK = 1024, 1536, 1024


def _pick(n, cap, mult):
    best = None
    for d in range(mult, min(n, cap) + 1, mult):
        if n % d == 0:
            best = d
    return n if best is None else best


def _params(semantics, block_bytes):
    limit = int(min(max(2 * block_bytes + (8 << 20), 24 << 20), V7X_VMEM_REQUEST_CAP))
    return pltpu.CompilerParams(dimension_semantics=semantics, vmem_limit_bytes=limit)


def _nbytes(shape, dtype):
    return int(np.prod(shape)) * jnp.dtype(dtype).itemsize


def _dot(a, b, dims):
    return lax.dot_general(a, b, (dims, ((), ())), preferred_element_type=F32)


NN = ((1,), (0,))
NT = ((1,), (1,))
TN = ((0,), (0,))


def _matmul(a, b, mode, out_dtype, name, residual=None):
    if mode == "nn":
        (m, k), (k2, n) = a.shape, b.shape
    elif mode == "nt":
        (m, k), (n, k2) = a.shape, b.shape
    else:
        (k, m), (k2, n) = a.shape, b.shape
    assert k == k2, (name, a.shape, b.shape)
    tm, tn, tk = _pick(m, MM_TM, LANE), _pick(n, MM_TN, LANE), _pick(k, MM_TK, LANE)
    nk = k // tk
    dims = {"nn": NN, "nt": NT, "tn": TN}[mode]
    has_res = residual is not None

    def body(*refs):
        a_ref, b_ref = refs[0], refs[1]
        r_ref = refs[2] if has_res else None
        o_ref = refs[3] if has_res else refs[2]
        kk = pl.program_id(2)
        p = _dot(a_ref[...].astype(BF16), b_ref[...].astype(BF16), dims)
        if nk == 1:
            if has_res:
                p = p + r_ref[...]
            o_ref[...] = p.astype(out_dtype)
        else:
            acc_ref = refs[-1]

            @pl.when(kk == 0)
            def _():
                acc_ref[...] = p

            @pl.when(kk > 0)
            def _():
                acc_ref[...] += p

            @pl.when(kk == nk - 1)
            def _():
                res = acc_ref[...]
                if has_res:
                    res = res + r_ref[...]
                o_ref[...] = res.astype(out_dtype)

    if mode == "nn":
        a_spec = pl.BlockSpec((tm, tk), lambda i, j, kk: (i, kk))
        b_spec = pl.BlockSpec((tk, tn), lambda i, j, kk: (kk, j))
    elif mode == "nt":
        a_spec = pl.BlockSpec((tm, tk), lambda i, j, kk: (i, kk))
        b_spec = pl.BlockSpec((tn, tk), lambda i, j, kk: (j, kk))
    else:
        a_spec = pl.BlockSpec((tk, tm), lambda i, j, kk: (kk, i))
        b_spec = pl.BlockSpec((tk, tn), lambda i, j, kk: (kk, j))
    o_spec = pl.BlockSpec((tm, tn), lambda i, j, kk: (i, j))
    in_specs, args = [a_spec, b_spec], [a, b]
    if has_res:
        in_specs.append(o_spec)
        args.append(residual)
    blk = (_nbytes((tm, tk), a.dtype) + _nbytes((tk, tn), b.dtype) + _nbytes((tm, tn), out_dtype)
           + (_nbytes((tm, tn), F32) if has_res else 0))
    scratch = [pltpu.VMEM((tm, tn), F32)] if nk > 1 else []
    return pl.pallas_call(
        body, name=name, grid=(m // tm, n // tn, nk),
        out_shape=jax.ShapeDtypeStruct((m, n), out_dtype),
        in_specs=in_specs, out_specs=o_spec, scratch_shapes=scratch,
        compiler_params=_params(("parallel", "parallel", "arbitrary"), blk + _nbytes((tm, tn), F32)),
    )(*args)


def _rms_fwd(x, g, name):
    t, d = x.shape
    tr = _pick(t, ROW_TILE, BF16_SUBLANE)

    def body(x_ref, g_ref, y_ref):
        xv = x_ref[...]
        r = lax.rsqrt(jnp.mean(xv * xv, axis=-1, keepdims=True) + RMS_EPS)
        y_ref[...] = ((xv * r) * g_ref[...]).astype(BF16)

    return pl.pallas_call(
        body, name=name, grid=(t // tr,),
        out_shape=jax.ShapeDtypeStruct((t, d), BF16),
        in_specs=[pl.BlockSpec((tr, d), lambda i: (i, 0)), pl.BlockSpec((1, d), lambda i: (0, 0))],
        out_specs=pl.BlockSpec((tr, d), lambda i: (i, 0)),
        compiler_params=_params(("parallel",), 2 * _nbytes((tr, d), F32)),
    )(x, g.reshape(1, d))


def _rms_bwd(x, g, dy, dres, name):
    t, d = x.shape
    tr = _pick(t, ROW_TILE, BF16_SUBLANE)
    has_res = dres is not None

    def body(*refs):
        x_ref, g_ref, dy_ref = refs[:3]
        r_ref = refs[3] if has_res else None
        dx_ref, dxb_ref, dg_ref = refs[-3:]
        i = pl.program_id(0)
        xv = x_ref[...]
        r = lax.rsqrt(jnp.mean(xv * xv, axis=-1, keepdims=True) + RMS_EPS)
        xh = xv * r
        dyv = dy_ref[...].astype(F32)
        dxh = dyv * g_ref[...]
        dx = r * (dxh - xh * jnp.mean(dxh * xh, axis=-1, keepdims=True))
        if has_res:
            dx = dx + r_ref[...]
        dx_ref[...] = dx
        dxb_ref[...] = dx.astype(BF16)

        @pl.when(i == 0)
        def _():
            dg_ref[...] = jnp.zeros_like(dg_ref)

        dg_ref[...] += jnp.sum(dyv * xh, axis=0, keepdims=True)

    row = pl.BlockSpec((tr, d), lambda i: (i, 0))
    vec = pl.BlockSpec((1, d), lambda i: (0, 0))
    in_specs, args = [row, vec, row], [x, g.reshape(1, d), dy]
    if has_res:
        in_specs.append(row)
        args.append(dres)
    return pl.pallas_call(
        body, name=name, grid=(t // tr,),
        out_shape=(jax.ShapeDtypeStruct((t, d), F32), jax.ShapeDtypeStruct((t, d), BF16),
                   jax.ShapeDtypeStruct((1, d), F32)),
        in_specs=in_specs, out_specs=(row, row, vec),
        compiler_params=_params(("arbitrary",), 5 * _nbytes((tr, d), F32)),
    )(*args)


def _loss_head(x, g, target, name):
    t, d = x.shape
    tr = _pick(t, ROW_TILE, BF16_SUBLANE)

    def body(x_ref, g_ref, t_ref, loss_ref, dx_ref, dxb_ref, dg_ref):
        i = pl.program_id(0)
        xv = x_ref[...]
        gv = g_ref[...]
        r = lax.rsqrt(jnp.mean(xv * xv, axis=-1, keepdims=True) + RMS_EPS)
        xh = xv * r
        diff = xh * gv - t_ref[...]
        part = 0.5 * jnp.sum(jnp.mean(diff * diff, axis=-1, keepdims=True), axis=0, keepdims=True)
        dyv = diff * (1.0 / d)
        dxh = dyv * gv
        dx = r * (dxh - xh * jnp.mean(dxh * xh, axis=-1, keepdims=True))
        dx_ref[...] = dx
        dxb_ref[...] = dx.astype(BF16)

        @pl.when(i == 0)
        def _():
            dg_ref[...] = jnp.zeros_like(dg_ref)
            loss_ref[...] = jnp.zeros_like(loss_ref)

        dg_ref[...] += jnp.sum(dyv * xh, axis=0, keepdims=True)
        loss_ref[...] += jnp.broadcast_to(part, loss_ref.shape)

    row = pl.BlockSpec((tr, d), lambda i: (i, 0))
    vec = pl.BlockSpec((1, d), lambda i: (0, 0))
    return pl.pallas_call(
        body, name=name, grid=(t // tr,),
        out_shape=(jax.ShapeDtypeStruct((1, LANE), F32), jax.ShapeDtypeStruct((t, d), F32),
                   jax.ShapeDtypeStruct((t, d), BF16), jax.ShapeDtypeStruct((1, d), F32)),
        in_specs=[row, vec, row],
        out_specs=(pl.BlockSpec((1, LANE), lambda i: (0, 0)), row, row, vec),
        compiler_params=_params(("arbitrary",), 5 * _nbytes((tr, d), F32)),
    )(x, g.reshape(1, d), target)


HALO = 8


def _conv_fwd(proj, conv_w, name):
    t = proj.shape[0]
    tr = _pick(t, ROW_TILE, BF16_SUBLANE)
    c0 = COL_CONV // BRANCH
    hb = tr // HALO

    def body(cb_ref, cc_ref, cu_ref, hc_ref, hu_ref, w_ref, y_ref):
        i = pl.program_id(0)
        z = cc_ref[...] * cu_ref[...]
        hz = jnp.where(i > 0, hc_ref[...] * hu_ref[...], 0.0)
        zf = jnp.concatenate([hz, z], axis=0)
        z1 = pltpu.roll(zf, 1, 0)[HALO:]
        z2 = pltpu.roll(zf, 2, 0)[HALO:]
        y = w_ref[2:3, :] * z + w_ref[1:2, :] * z1 + w_ref[0:1, :] * z2
        y_ref[...] = (cb_ref[...] * y).astype(BF16)

    def col(c):
        return pl.BlockSpec((tr, BRANCH), lambda i, c=c: (i, c0 + c))

    def prev(c):
        return pl.BlockSpec((HALO, BRANCH), lambda i, c=c: (jnp.maximum(i * hb - 1, 0), c0 + c))

    return pl.pallas_call(
        body, name=name, grid=(t // tr,),
        out_shape=jax.ShapeDtypeStruct((t, BRANCH), BF16),
        in_specs=[col(0), col(1), col(2), prev(1), prev(2), pl.BlockSpec((3, BRANCH), lambda i: (0, 0))],
        out_specs=pl.BlockSpec((tr, BRANCH), lambda i: (i, 0)),
        compiler_params=_params(("parallel",), 6 * _nbytes((tr, BRANCH), F32)),
    )(proj, proj, proj, proj, proj, conv_w)


def _conv_bwd(proj, conv_w, dout, name):
    t = proj.shape[0]
    tr = _pick(t, ROW_TILE, BF16_SUBLANE)
    nblk = t // tr
    c0 = COL_CONV // BRANCH
    hb = tr // HALO
    last_halo = t // HALO - 1

    def body(cb_ref, cc_ref, cu_ref, hc_ref, hu_ref, do_ref, ndo_ref, ncb_ref, w_ref,
             dcb_ref, dcc_ref, dcu_ref, dw_ref):
        i = pl.program_id(0)
        cb, cc, cu = cb_ref[...], cc_ref[...], cu_ref[...]
        w0, w1, w2 = w_ref[0:1, :], w_ref[1:2, :], w_ref[2:3, :]
        z = cc * cu
        hz = jnp.where(i > 0, hc_ref[...] * hu_ref[...], 0.0)
        zf = jnp.concatenate([hz, z], axis=0)
        z1 = pltpu.roll(zf, 1, 0)[HALO:]
        z2 = pltpu.roll(zf, 2, 0)[HALO:]
        y = w2 * z + w1 * z1 + w0 * z2
        dout_v = do_ref[...]
        dyc = dout_v * cb
        hdy = jnp.where(i < nblk - 1, ndo_ref[...] * ncb_ref[...], 0.0)
        dyf = jnp.concatenate([dyc, hdy], axis=0)
        dy1 = pltpu.roll(dyf, tr + HALO - 1, 0)[:tr]
        dy2 = pltpu.roll(dyf, tr + HALO - 2, 0)[:tr]
        dz = w2 * dyc + w1 * dy1 + w0 * dy2
        dcb_ref[...] = (dout_v * y).astype(BF16)
        dcc_ref[...] = (dz * cu).astype(BF16)
        dcu_ref[...] = (dz * cc).astype(BF16)

        @pl.when(i == 0)
        def _():
            dw_ref[...] = jnp.zeros_like(dw_ref)

        dw_ref[0:1, :] += jnp.sum(dyc * z2, axis=0, keepdims=True)
        dw_ref[1:2, :] += jnp.sum(dyc * z1, axis=0, keepdims=True)
        dw_ref[2:3, :] += jnp.sum(dyc * z, axis=0, keepdims=True)

    def col(c):
        return pl.BlockSpec((tr, BRANCH), lambda i, c=c: (i, c0 + c))

    def prev(c):
        return pl.BlockSpec((HALO, BRANCH), lambda i, c=c: (jnp.maximum(i * hb - 1, 0), c0 + c))

    def nxt(c):
        return pl.BlockSpec((HALO, BRANCH), lambda i, c=c: (jnp.minimum((i + 1) * hb, last_halo), c))

    row = pl.BlockSpec((tr, BRANCH), lambda i: (i, 0))
    return pl.pallas_call(
        body, name=name, grid=(nblk,),
        out_shape=(jax.ShapeDtypeStruct((t, BRANCH), BF16),) * 3 + (jax.ShapeDtypeStruct((HALO, BRANCH), F32),),
        in_specs=[col(0), col(1), col(2), prev(1), prev(2), row, nxt(0), nxt(c0),
                  pl.BlockSpec((3, BRANCH), lambda i: (0, 0))],
        out_specs=(row, row, row, pl.BlockSpec((HALO, BRANCH), lambda i: (0, 0))),
        compiler_params=_params(("arbitrary",), 8 * _nbytes((tr, BRANCH), F32)),
    )(proj, proj, proj, proj, proj, dout, dout, proj, conv_w)


def _tri(lower):
    r = lax.broadcasted_iota(jnp.int32, (LANE, LANE), 0)
    c = lax.broadcasted_iota(jnp.int32, (LANE, LANE), 1)
    return jnp.where((c <= r) if lower else (c >= r), 1.0, 0.0).astype(F32)


def _logf_cumsum(proj, fbias_row, name):
    t = proj.shape[0]
    nchunk = t // LANE

    def body(f_ref, b_ref, c_ref, run_sc):
        tri = _tri(True)
        run_sc[...] = jnp.zeros_like(run_sc)

        @pl.loop(0, nchunk)
        def _(i):
            rows = pl.ds(pl.multiple_of(i * LANE, LANE), LANE)
            z = f_ref[rows, :] + b_ref[...]
            logf = jnp.minimum(z, 0.0) - jnp.log(1.0 + jnp.exp(-jnp.abs(z)))
            cs = lax.dot_general(tri, logf, (NN, ((), ())), precision=lax.Precision.HIGHEST,
                                 preferred_element_type=F32) + run_sc[0:1, :]
            c_ref[rows, :] = cs
            run_sc[0:1, :] = cs[LANE - 1:LANE, :]

    return pl.pallas_call(
        body, name=name, grid=(1,),
        out_shape=jax.ShapeDtypeStruct((t, LANE), F32),
        in_specs=[pl.BlockSpec((t, LANE), lambda i: (0, COL_FG // LANE)), pl.BlockSpec((1, LANE), lambda i: (0, 0))],
        out_specs=pl.BlockSpec((t, LANE), lambda i: (0, 0)),
        scratch_shapes=[pltpu.VMEM((8, LANE), F32)],
        compiler_params=_params(("arbitrary",), 2 * _nbytes((t, LANE), F32)),
    )(proj, fbias_row)


def _logf_cumsum_bwd(proj, fbias_row, pieces, name):
    t = proj.shape[0]
    tb = _pick(t, 2 * ROW_TILE, LANE)
    nblk = t // tb
    npiece = len(pieces)

    def body(*refs):
        f_ref, b_ref = refs[:2]
        piece_refs = refs[2:2 + npiece]
        df_ref, db_ref, run_sc = refs[2 + npiece:]
        i = pl.program_id(0)
        tri = _tri(False)

        @pl.when(i == 0)
        def _():
            run_sc[...] = jnp.zeros_like(run_sc)
            db_ref[...] = jnp.zeros_like(db_ref)

        for c in reversed(range(tb // LANE)):
            rows = slice(c * LANE, (c + 1) * LANE)
            slabs = [p_ref[n, rows, :] for p_ref in piece_refs for n in range(p_ref.shape[0])]
            dcc = slabs[0]
            for slab in slabs[1:]:
                dcc = dcc + slab
            ss = lax.dot_general(tri, dcc, (NN, ((), ())), precision=lax.Precision.HIGHEST,
                                 preferred_element_type=F32) + run_sc[0:1, :]
            z = f_ref[rows, :] + b_ref[...]
            dz = ss * (1.0 / (1.0 + jnp.exp(z)))
            df_ref[rows, :] = dz.astype(BF16)
            run_sc[0:1, :] = ss[0:1, :]
            db_ref[...] += jnp.sum(dz, axis=0, keepdims=True)

    piece_specs = [pl.BlockSpec((p.shape[0], tb, LANE), lambda i: (0, nblk - 1 - i, 0)) for p in pieces]
    nslab = sum(p.shape[0] for p in pieces)
    return pl.pallas_call(
        body, name=name, grid=(nblk,),
        out_shape=(jax.ShapeDtypeStruct((t, LANE), BF16), jax.ShapeDtypeStruct((1, LANE), F32)),
        in_specs=[pl.BlockSpec((tb, LANE), lambda i: (nblk - 1 - i, COL_FG // LANE)),
                  pl.BlockSpec((1, LANE), lambda i: (0, 0))] + piece_specs,
        out_specs=(pl.BlockSpec((tb, LANE), lambda i: (nblk - 1 - i, 0)), pl.BlockSpec((1, LANE), lambda i: (0, 0))),
        scratch_shapes=[pltpu.VMEM((8, LANE), F32)],
        compiler_params=_params(("arbitrary",), (4 + nslab) * _nbytes((tb, LANE), F32)),
    )(proj, fbias_row, *pieces)


def _lo_mask():
    return lax.broadcasted_iota(jnp.int32, (1, LANE), 1) < HEAD


def _causal_steps(n, key_major):
    if key_major:
        pairs = [(iq, ik) for ik in range(n) for iq in range(ik, n)]
    else:
        pairs = [(iq, ik) for iq in range(n) for ik in range(iq + 1)]
    return (jnp.asarray([p[0] for p in pairs], jnp.int32), jnp.asarray([p[1] for p in pairs], jnp.int32))


def _head_lanes(j, pair_vals):
    lane = lax.broadcasted_iota(jnp.int32, (1, LANE), 1)
    return jnp.where(lane == 2 * j, pair_vals[0], 0.0) + jnp.where(lane == 2 * j + 1, pair_vals[1], 0.0)


def _fox_fwd(proj, c_col, c_row, name):
    t = proj.shape[0]
    tq = _pick(t, FOX_TILE, LANE)
    nq = t // tq
    rep = tq // LANE
    scale = HEAD ** -0.5
    cq, ck, cv = COL_FOX // LANE, COL_FOX // LANE + 4, COL_FOX // LANE + 8
    q_tab, k_tab = _causal_steps(nq, False)

    def body(qt_ref, kt_ref, q_ref, k_ref, v_ref, cc_ref, cr_ref, y_ref, lse_ref, m_sc, l_sc, acc_sc):
        step_id = pl.program_id(1)
        iq, ik = qt_ref[step_id], kt_ref[step_id]
        lo = _lo_mask()

        @pl.when(ik == 0)
        def _():
            m_sc[...] = jnp.full(m_sc.shape, NEG, F32)
            l_sc[...] = jnp.zeros_like(l_sc)
            acc_sc[...] = jnp.zeros_like(acc_sc)

        def step(diag):
            q2 = q_ref[...].astype(BF16)
            k2 = k_ref[...].astype(BF16)
            v2 = v_ref[...].astype(BF16)
            for h in range(2):
                msk = lo if h == 0 else jnp.logical_not(lo)
                qh = jnp.where(msk, q2, jnp.zeros_like(q2))
                s = _dot(qh, k2, NT) * scale
                s = s + jnp.tile(cc_ref[h], (1, rep)) - cr_ref[h]
                if diag:
                    row = lax.broadcasted_iota(jnp.int32, (tq, tq), 0)
                    col = lax.broadcasted_iota(jnp.int32, (tq, tq), 1)
                    s = jnp.where(col <= row, s, NEG)
                m_prev = m_sc[h]
                m_new = jnp.maximum(m_prev, jnp.max(s, axis=1, keepdims=True))
                alpha = jnp.exp(m_prev - m_new)
                p = jnp.exp(s - jnp.tile(m_new, (1, rep)))
                l_sc[h] = alpha * l_sc[h] + jnp.sum(p, axis=1, keepdims=True)
                acc_sc[h] = alpha * acc_sc[h] + _dot(p.astype(BF16), v2, NN)
                m_sc[h] = m_new

        @pl.when(ik < iq)
        def _():
            step(False)

        @pl.when(ik == iq)
        def _():
            step(True)
            y = jnp.where(lo, acc_sc[0] / l_sc[0], acc_sc[1] / l_sc[1])
            y_ref[...] = y.astype(BF16)
            lse_ref[...] = m_sc[...] + jnp.log(l_sc[...])

    def kv(c):
        return pl.BlockSpec((tq, LANE), lambda j, s, qt, kt, c=c: (kt[s], c + j))

    grid_spec = pltpu.PrefetchScalarGridSpec(
        num_scalar_prefetch=2, grid=(4, int(q_tab.shape[0])),
        in_specs=[pl.BlockSpec((tq, LANE), lambda j, s, qt, kt: (qt[s], cq + j)), kv(ck), kv(cv),
                  pl.BlockSpec((2, tq, LANE), lambda j, s, qt, kt: (j, qt[s], 0)),
                  pl.BlockSpec((2, 1, tq), lambda j, s, qt, kt: (j, 0, kt[s]))],
        out_specs=(pl.BlockSpec((tq, LANE), lambda j, s, qt, kt: (qt[s], j)),
                   pl.BlockSpec((2, tq, LANE), lambda j, s, qt, kt: (j, qt[s], 0))),
        scratch_shapes=[pltpu.VMEM((2, tq, LANE), F32)] * 3)
    return pl.pallas_call(
        body, name=name, grid_spec=grid_spec,
        out_shape=(jax.ShapeDtypeStruct((t, BRANCH), BF16), jax.ShapeDtypeStruct((8, t, LANE), F32)),
        compiler_params=_params(("parallel", "arbitrary"),
                                16 * _nbytes((tq, LANE), F32) + 6 * _nbytes((tq, tq), F32)),
    )(q_tab, k_tab, proj, proj, proj, c_col, c_row)


def _fox_bwd(proj, c_col, c_row, lse_row, y, dy, name):
    t = proj.shape[0]
    tb = _pick(t, FOX_TILE, LANE)
    nb = t // tb
    rep = tb // LANE
    scale = HEAD ** -0.5
    cq, ck, cv = COL_FOX // LANE, COL_FOX // LANE + 4, COL_FOX // LANE + 8
    q_tab, k_tab = _causal_steps(nb, True)
    nsteps = int(q_tab.shape[0])

    def body(qt_ref, kt_ref, k_ref, v_ref, q_ref, y_ref, dy_ref, ck_ref, cqr_ref, lser_ref,
             dq_ref, dk_ref, dv_ref, dck_ref, dcq_ref, dk_sc, dv_sc, dc_sc, dqt_sc, dcq_sc, d_sc):
        j, step_id = pl.program_id(0), pl.program_id(1)
        iq, ik = qt_ref[step_id], kt_ref[step_id]
        lo = _lo_mask()

        @pl.when(step_id == 0)
        def _():
            dqt_sc[...] = jnp.zeros_like(dqt_sc)
            dcq_sc[...] = jnp.zeros_like(dcq_sc)

        @pl.when(iq == ik)
        def _():
            dk_sc[...] = jnp.zeros_like(dk_sc)
            dv_sc[...] = jnp.zeros_like(dv_sc)
            dc_sc[...] = jnp.zeros_like(dc_sc)

        @pl.when(ik == 0)
        def _():
            prod = y_ref[...].astype(F32) * dy_ref[...].astype(F32)
            row = lax.broadcasted_iota(jnp.int32, (8, LANE), 0)
            sel = jnp.logical_or(jnp.logical_and(row == 0, lo), jnp.logical_and(row == 1, jnp.logical_not(lo)))
            d_sc[iq] = lax.dot_general(jnp.where(sel, 1.0, 0.0).astype(F32), prod, (NT, ((), ())),
                                       precision=lax.Precision.HIGHEST, preferred_element_type=F32)

        def step(diag):
            k2 = k_ref[...].astype(BF16)
            v2 = v_ref[...].astype(BF16)
            q2 = q_ref[...].astype(BF16)
            do2 = dy_ref[...]
            d_rows = d_sc[iq]
            for h in range(2):
                msk = lo if h == 0 else jnp.logical_not(lo)
                kh = jnp.where(msk, k2, jnp.zeros_like(k2))
                vh = jnp.where(msk, v2, jnp.zeros_like(v2))
                st = _dot(kh, q2, NT) * scale
                st = st + cqr_ref[h] - jnp.tile(ck_ref[h], (1, rep))
                if diag:
                    krow = lax.broadcasted_iota(jnp.int32, (tb, tb), 0)
                    qcol = lax.broadcasted_iota(jnp.int32, (tb, tb), 1)
                    st = jnp.where(krow <= qcol, st, NEG)
                pt = jnp.exp(st - lser_ref[h])
                dpt = _dot(vh, do2, NT)
                dst = pt * (dpt - d_rows[h:h + 1, :])
                dsb = dst.astype(BF16)
                dv_sc[h] += _dot(pt.astype(BF16), do2, NN)
                dk_sc[h] += _dot(dsb, q2, NN)
                dc_sc[h] -= jnp.sum(dst, axis=1, keepdims=True)
                dqt_sc[iq] += _dot(kh, dsb, TN)
                dcq_sc[h, iq] += jnp.sum(dst, axis=0, keepdims=True)

        @pl.when(iq > ik)
        def _():
            step(False)

        @pl.when(iq == ik)
        def _():
            step(True)

        @pl.when(iq == nb - 1)
        def _():
            dk_ref[...] = (jnp.where(lo, dk_sc[0], dk_sc[1]) * scale).astype(BF16)
            dv_ref[...] = jnp.where(lo, dv_sc[0], dv_sc[1]).astype(BF16)
            dck_ref[...] = _head_lanes(j, dc_sc)

        @pl.when(step_id == nsteps - 1)
        def _():
            for i in range(nb):
                dq_ref[i * tb:(i + 1) * tb, :] = (dqt_sc[i].T * scale).astype(BF16)
                for h in range(2):
                    dcq_ref[h, :, i * tb:(i + 1) * tb] = dcq_sc[h, i]

    def kcol(c):
        return pl.BlockSpec((tb, LANE), lambda j, s, qt, kt, c=c: (kt[s], c + j))

    qrow = pl.BlockSpec((2, 1, tb), lambda j, s, qt, kt: (j, 0, qt[s]))
    pair_q = pl.BlockSpec((tb, LANE), lambda j, s, qt, kt: (qt[s], j))
    pair_k = pl.BlockSpec((tb, LANE), lambda j, s, qt, kt: (kt[s], j))
    grid_spec = pltpu.PrefetchScalarGridSpec(
        num_scalar_prefetch=2, grid=(4, nsteps),
        in_specs=[kcol(ck), kcol(cv), pl.BlockSpec((tb, LANE), lambda j, s, qt, kt: (qt[s], cq + j)), pair_q, pair_q,
                  pl.BlockSpec((2, tb, LANE), lambda j, s, qt, kt: (j, kt[s], 0)), qrow, qrow],
        out_specs=(pl.BlockSpec((t, LANE), lambda j, s, qt, kt: (0, j)), pair_k, pair_k,
                   pl.BlockSpec((None, tb, LANE), lambda j, s, qt, kt: (j, kt[s], 0)),
                   pl.BlockSpec((2, 1, t), lambda j, s, qt, kt: (j, 0, 0))),
        scratch_shapes=[pltpu.VMEM((2, tb, LANE), F32)] * 3
        + [pltpu.VMEM((nb, LANE, tb), F32), pltpu.VMEM((2, nb, 1, tb), F32), pltpu.VMEM((nb, 8, tb), F32)])
    return pl.pallas_call(
        body, name=name, grid_spec=grid_spec,
        out_shape=(jax.ShapeDtypeStruct((t, BRANCH), BF16), jax.ShapeDtypeStruct((t, BRANCH), BF16),
                   jax.ShapeDtypeStruct((t, BRANCH), BF16), jax.ShapeDtypeStruct((4, t, LANE), F32),
                   jax.ShapeDtypeStruct((8, 1, t), F32)),
        compiler_params=_params(("parallel", "arbitrary"),
                                24 * _nbytes((tb, LANE), F32) + 8 * _nbytes((tb, tb), F32)
                                + 2 * _nbytes((t, LANE), F32)),
    )(q_tab, k_tab, proj, proj, proj, y, dy, c_col, c_row, lse_row)


def _swa_tables(rel_bias):
    tq = np.arange(SWA_BLOCK)[:, None]
    sk = np.arange(2 * SWA_BLOCK)[None, :]
    dist = SWA_BLOCK + tq - sk
    inwin = (dist >= 0) & (dist < SWA_BLOCK)
    n = np.maximum(dist, 0)
    max_exact = N_BUCKETS // 2
    large = max_exact + (np.log(np.maximum(n, 1).astype(np.float32) / max_exact)
                         / math.log(SWA_BLOCK / max_exact) * (N_BUCKETS - max_exact)).astype(np.int32)
    bucket = np.where(n < max_exact, n, np.minimum(large, N_BUCKETS - 1))
    onehot = (bucket[..., None] == np.arange(N_BUCKETS)) & inwin[..., None]
    onehot = jnp.asarray(onehot.astype(np.float32))
    bias = jnp.einsum("tsb,bh->hts", onehot, rel_bias, precision=lax.Precision.HIGHEST)
    bias = jnp.where(jnp.asarray(inwin)[None], bias, NEG)
    return onehot, bias


def _swa_fwd(proj, bias, sink_rep, name):
    t = proj.shape[0]
    nb = t // SWA_BLOCK
    scale = HEAD ** -0.5
    csq, csk, csv = COL_SQ // 256, COL_SK // LANE, COL_SV // LANE

    def body(q_ref, kp_ref, kc_ref, vp_ref, vc_ref, b_ref, sk_ref, y_ref, lse_ref):
        kvh, n = pl.program_id(0), pl.program_id(1)
        lane = lax.broadcasted_iota(jnp.int32, (1, LANE), 1)
        lo = lane < HEAD
        kvm = jnp.logical_and(lane >= kvh * HEAD, lane < (kvh + 1) * HEAD)

        def both(prev_ref, cur_ref):
            band = jnp.concatenate([prev_ref[...], cur_ref[...]], axis=0)
            band = jnp.where(kvm, band, 0.0)
            return (band + pltpu.roll(band, HEAD, 1)).astype(BF16)

        kb, vb = both(kp_ref, kc_ref), both(vp_ref, vc_ref)
        col = lax.broadcasted_iota(jnp.int32, (SWA_BLOCK, 2 * SWA_BLOCK), 1)
        first = jnp.logical_and(n == 0, col < SWA_BLOCK)
        outs = []
        for g in range(SWA_GROUP):
            half = q_ref[:, (g // 2) * LANE:(g // 2 + 1) * LANE]
            hm = lo if g % 2 == 0 else jnp.logical_not(lo)
            qg = jnp.where(hm, half, 0.0).astype(BF16)
            s = _dot(qg, kb, NT) * scale + b_ref[g]
            s = jnp.where(first, NEG, s)
            snk = sk_ref[g:g + 1, :]
            m = jnp.maximum(jnp.max(s, axis=1, keepdims=True), snk)
            p = jnp.exp(s - jnp.tile(m, (1, 2)))
            denom = jnp.sum(p, axis=1, keepdims=True) + jnp.exp(snk - m)
            outs.append(_dot(p.astype(BF16), vb, NN) / denom)
            lse_ref[g] = m + jnp.log(denom)
        y_ref[:, 0:LANE] = jnp.where(lo, outs[0], outs[1]).astype(BF16)
        y_ref[:, LANE:2 * LANE] = jnp.where(lo, outs[2], outs[3]).astype(BF16)

    def blk(c, shift):
        return pl.BlockSpec((SWA_BLOCK, LANE), lambda kvh, n, c=c, s=shift: (jnp.maximum(n - s, 0), c))

    return pl.pallas_call(
        body, name=name, grid=(2, nb),
        out_shape=(jax.ShapeDtypeStruct((t, BRANCH), BF16), jax.ShapeDtypeStruct((8, t, LANE), F32)),
        in_specs=[pl.BlockSpec((SWA_BLOCK, 256), lambda kvh, n: (n, csq + kvh)),
                  blk(csk, 1), blk(csk, 0), blk(csv, 1), blk(csv, 0),
                  pl.BlockSpec((None, SWA_GROUP, SWA_BLOCK, 256), lambda kvh, n: (kvh, 0, 0, 0)),
                  pl.BlockSpec((None, SWA_GROUP, LANE), lambda kvh, n: (kvh, 0, 0))],
        out_specs=(pl.BlockSpec((SWA_BLOCK, 256), lambda kvh, n: (n, kvh)),
                   pl.BlockSpec((SWA_GROUP, SWA_BLOCK, LANE), lambda kvh, n: (kvh, n, 0))),
        compiler_params=_params(("parallel", "arbitrary"), 4 << 20),
    )(proj, proj, proj, proj, proj, bias.reshape(2, SWA_GROUP, SWA_BLOCK, 256), sink_rep)


def _swa_bwd(proj, bias, sink_rep, lse, y, dy, name):
    t = proj.shape[0]
    nb = t // SWA_BLOCK
    scale = HEAD ** -0.5
    csq, csk, csv = COL_SQ // 256, COL_SK // LANE, COL_SV // LANE

    def body(q_ref, kp_ref, kc_ref, vp_ref, vc_ref, b_ref, sk_ref, lse_ref, y_ref, dy_ref,
             dq_ref, dkp_ref, dvp_ref, db_ref, dsk_ref):
        kvh, n = pl.program_id(0), pl.program_id(1)
        lane = lax.broadcasted_iota(jnp.int32, (1, LANE), 1)
        lo = lane < HEAD
        kvm = jnp.logical_and(lane >= kvh * HEAD, lane < (kvh + 1) * HEAD)

        def both(prev_ref, cur_ref):
            band = jnp.concatenate([prev_ref[...], cur_ref[...]], axis=0)
            band = jnp.where(kvm, band, 0.0)
            return (band + pltpu.roll(band, HEAD, 1)).astype(BF16)

        kb, vb = both(kp_ref, kc_ref), both(vp_ref, vc_ref)
        col = lax.broadcasted_iota(jnp.int32, (SWA_BLOCK, 2 * SWA_BLOCK), 1)
        first = jnp.logical_and(n == 0, col < SWA_BLOCK)

        @pl.when(n == 0)
        def _():
            db_ref[...] = jnp.zeros_like(db_ref)
            dsk_ref[...] = jnp.zeros_like(dsk_ref)

        dk_full = jnp.zeros((2 * SWA_BLOCK, LANE), F32)
        dv_full = jnp.zeros((2 * SWA_BLOCK, LANE), F32)
        dqs = []
        for g in range(SWA_GROUP):
            sl = slice((g // 2) * LANE, (g // 2 + 1) * LANE)
            hm = lo if g % 2 == 0 else jnp.logical_not(lo)
            qg = jnp.where(hm, q_ref[:, sl], 0.0).astype(BF16)
            dog = jnp.where(hm, dy_ref[:, sl], jnp.zeros((SWA_BLOCK, LANE), BF16))
            dmat = jnp.where(hm, y_ref[:, sl].astype(F32) * dy_ref[:, sl].astype(F32), 0.0)
            dg = jnp.sum(dmat, axis=1, keepdims=True)
            s = _dot(qg, kb, NT) * scale + b_ref[g]
            s = jnp.where(first, NEG, s)
            lse_g = lse_ref[g]
            p = jnp.exp(s - jnp.tile(lse_g, (1, 2)))
            dp = _dot(dog, vb, NT)
            ds = p * (dp - dg)
            dsb = ds.astype(BF16)
            dqs.append(_dot(dsb, kb, NN) * scale)
            dk_full = dk_full + _dot(dsb, qg, TN)
            dv_full = dv_full + _dot(p.astype(BF16), dog, TN)
            db_ref[g] += ds
            psink = jnp.exp(sk_ref[g:g + 1, :] - lse_g)
            dsk_ref[g:g + 1, :] -= jnp.sum(psink * dg, axis=0, keepdims=True)
        dq_ref[:, 0:LANE] = jnp.where(lo, dqs[0], dqs[1]).astype(BF16)
        dq_ref[:, LANE:2 * LANE] = jnp.where(lo, dqs[2], dqs[3]).astype(BF16)
        dkp_ref[...] = jnp.where(kvm, (dk_full + pltpu.roll(dk_full, HEAD, 1)) * scale, 0.0)
        dvp_ref[...] = jnp.where(kvm, dv_full + pltpu.roll(dv_full, HEAD, 1), 0.0)

    def blk(c, shift):
        return pl.BlockSpec((SWA_BLOCK, LANE), lambda kvh, n, c=c, s=shift: (jnp.maximum(n - s, 0), c))

    qblk = pl.BlockSpec((SWA_BLOCK, 256), lambda kvh, n: (n, kvh))
    part = pl.BlockSpec((None, None, 2 * SWA_BLOCK, LANE), lambda kvh, n: (kvh, n, 0, 0))
    bspec = pl.BlockSpec((None, SWA_GROUP, SWA_BLOCK, 256), lambda kvh, n: (kvh, 0, 0, 0))
    sspec = pl.BlockSpec((None, SWA_GROUP, LANE), lambda kvh, n: (kvh, 0, 0))
    return pl.pallas_call(
        body, name=name, grid=(2, nb),
        out_shape=(jax.ShapeDtypeStruct((t, BRANCH), BF16),
                   jax.ShapeDtypeStruct((2, nb, 2 * SWA_BLOCK, LANE), F32),
                   jax.ShapeDtypeStruct((2, nb, 2 * SWA_BLOCK, LANE), F32),
                   jax.ShapeDtypeStruct((2, SWA_GROUP, SWA_BLOCK, 256), F32),
                   jax.ShapeDtypeStruct((2, SWA_GROUP, LANE), F32)),
        in_specs=[pl.BlockSpec((SWA_BLOCK, 256), lambda kvh, n: (n, csq + kvh)),
                  blk(csk, 1), blk(csk, 0), blk(csv, 1), blk(csv, 0), bspec, sspec,
                  pl.BlockSpec((SWA_GROUP, SWA_BLOCK, LANE), lambda kvh, n: (kvh, n, 0)), qblk, qblk],
        out_specs=(qblk, part, part, bspec, sspec),
        compiler_params=_params(("parallel", "arbitrary"), 6 << 20),
    )(proj, proj, proj, proj, proj, bias.reshape(2, SWA_GROUP, SWA_BLOCK, 256), sink_rep, lse, y, dy)


def _gate_fwd(proj, pb, name):
    t = proj.shape[0]
    tr = _pick(t, ROW_TILE // 2, BF16_SUBLANE)

    def body(g0, g1, g2, p0, p1, p2, o_ref):
        acc = jax.nn.sigmoid(g0[...]) * p0[...]
        acc = acc + jax.nn.sigmoid(g1[...]) * p1[...]
        acc = acc + jax.nn.sigmoid(g2[...]) * p2[...]
        o_ref[...] = acc.astype(BF16)

    row = pl.BlockSpec((tr, D_MODEL), lambda i: (i, 0))
    gates = [pl.BlockSpec((tr, D_MODEL), lambda i, b=b: (i, b)) for b in range(3)]
    return pl.pallas_call(
        body, name=name, grid=(t // tr,),
        out_shape=jax.ShapeDtypeStruct((t, D_MODEL), BF16),
        in_specs=gates + [row] * 3, out_specs=row,
        compiler_params=_params(("parallel",), 7 * _nbytes((tr, D_MODEL), F32)),
    )(proj, proj, proj, *pb)


def _gate_bwd(proj, pb, dmerged, name):
    t = proj.shape[0]
    tr = _pick(t, ROW_TILE // 2, BF16_SUBLANE)

    def body(g0, g1, g2, p0, p1, p2, dm_ref, dp0, dp1, dp2, dg_ref):
        dm = dm_ref[...]
        for b, (g_ref, p_ref, dp_ref) in enumerate(((g0, p0, dp0), (g1, p1, dp1), (g2, p2, dp2))):
            sg = jax.nn.sigmoid(g_ref[...])
            dp_ref[...] = (dm * sg).astype(BF16)
            dg_ref[:, b * D_MODEL:(b + 1) * D_MODEL] = (dm * p_ref[...] * sg * (1.0 - sg)).astype(BF16)

    row = pl.BlockSpec((tr, D_MODEL), lambda i: (i, 0))
    gates = [pl.BlockSpec((tr, D_MODEL), lambda i, b=b: (i, b)) for b in range(3)]
    return pl.pallas_call(
        body, name=name, grid=(t // tr,),
        out_shape=(jax.ShapeDtypeStruct((t, D_MODEL), BF16),) * 3 + (jax.ShapeDtypeStruct((t, 3 * D_MODEL), BF16),),
        in_specs=gates + [row] * 4,
        out_specs=(row, row, row, pl.BlockSpec((tr, 3 * D_MODEL), lambda i: (i, 0))),
        compiler_params=_params(("parallel",), 11 * _nbytes((tr, D_MODEL), F32)),
    )(proj, proj, proj, *pb, dmerged)


def _swiglu_fwd(ab, name):
    t = ab.shape[0]
    tr = _pick(t, ROW_TILE, BF16_SUBLANE)
    tc = D_FF // 2

    def body(a_ref, b_ref, o_ref):
        a = a_ref[...]
        o_ref[...] = (a * jax.nn.sigmoid(a) * b_ref[...]).astype(BF16)

    return pl.pallas_call(
        body, name=name, grid=(t // tr, 2),
        out_shape=jax.ShapeDtypeStruct((t, D_FF), BF16),
        in_specs=[pl.BlockSpec((tr, tc), lambda i, j: (i, j)), pl.BlockSpec((tr, tc), lambda i, j: (i, j + 2))],
        out_specs=pl.BlockSpec((tr, tc), lambda i, j: (i, j)),
        compiler_params=_params(("parallel", "parallel"), 3 * _nbytes((tr, tc), F32)),
    )(ab, ab)


def _swiglu_bwd(ab, dh, name):
    t = ab.shape[0]
    tr = _pick(t, ROW_TILE, BF16_SUBLANE)
    tc = D_FF // 2

    def body(a_ref, b_ref, dh_ref, o_ref):
        jj = pl.program_id(1)
        a, b, d = a_ref[...], b_ref[...], dh_ref[...]
        sg = jax.nn.sigmoid(a)
        da = d * b * (sg * (1.0 + a * (1.0 - sg)))
        db = d * (a * sg)
        o_ref[...] = jnp.where(jj < 2, da, db).astype(BF16)

    return pl.pallas_call(
        body, name=name, grid=(t // tr, 4),
        out_shape=jax.ShapeDtypeStruct((t, 2 * D_FF), BF16),
        in_specs=[pl.BlockSpec((tr, tc), lambda i, j: (i, j % 2)),
                  pl.BlockSpec((tr, tc), lambda i, j: (i, j % 2 + 2)),
                  pl.BlockSpec((tr, tc), lambda i, j: (i, j % 2))],
        out_specs=pl.BlockSpec((tr, tc), lambda i, j: (i, j)),
        compiler_params=_params(("parallel", "parallel"), 4 * _nbytes((tr, tc), F32)),
    )(ab, ab, dh)


def _xattn_fwd(q, kv, name):
    t = q.shape[0]
    tq = _pick(t, ROW_TILE, BF16_SUBLANE)
    mlen = kv.shape[0]
    scale = X_HEAD ** -0.5

    def body(q_ref, kv_ref, o_ref):
        for h in range(X_HEADS):
            sl = slice(h * X_HEAD, (h + 1) * X_HEAD)
            kh = kv_ref[:, sl]
            vh = kv_ref[:, D_MODEL + h * X_HEAD:D_MODEL + (h + 1) * X_HEAD]
            s = _dot(q_ref[:, sl], kh, NT) * scale
            p = jnp.exp(s - jnp.max(s, axis=1, keepdims=True))
            l = jnp.sum(p, axis=1, keepdims=True)
            o_ref[:, sl] = (_dot(p.astype(BF16), vh, NN) / l).astype(BF16)

    return pl.pallas_call(
        body, name=name, grid=(t // tq,),
        out_shape=jax.ShapeDtypeStruct((t, D_MODEL), BF16),
        in_specs=[pl.BlockSpec((tq, D_MODEL), lambda i: (i, 0)), pl.BlockSpec((mlen, 2 * D_MODEL), lambda i: (0, 0))],
        out_specs=pl.BlockSpec((tq, D_MODEL), lambda i: (i, 0)),
        compiler_params=_params(("parallel",), 4 * _nbytes((tq, D_MODEL), F32)),
    )(q, kv)


def _xattn_bwd(q, kv, do, name):
    t = q.shape[0]
    tq = _pick(t, ROW_TILE, BF16_SUBLANE)
    mlen = kv.shape[0]
    scale = X_HEAD ** -0.5

    def body(q_ref, kv_ref, do_ref, dq_ref, dkv_ref):
        i = pl.program_id(0)

        @pl.when(i == 0)
        def _():
            dkv_ref[...] = jnp.zeros_like(dkv_ref)

        for h in range(X_HEADS):
            sl = slice(h * X_HEAD, (h + 1) * X_HEAD)
            vsl = slice(D_MODEL + h * X_HEAD, D_MODEL + (h + 1) * X_HEAD)
            qh, kh, vh, doh = q_ref[:, sl], kv_ref[:, sl], kv_ref[:, vsl], do_ref[:, sl]
            s = _dot(qh, kh, NT) * scale
            p = jnp.exp(s - jnp.max(s, axis=1, keepdims=True))
            p = p / jnp.sum(p, axis=1, keepdims=True)
            dp = _dot(doh, vh, NT)
            ds = p * (dp - jnp.sum(p * dp, axis=1, keepdims=True))
            dsb = ds.astype(BF16)
            dq_ref[:, sl] = (_dot(dsb, kh, NN) * scale).astype(BF16)
            dkv_ref[:, sl] += _dot(dsb, qh, TN) * scale
            dkv_ref[:, vsl] += _dot(p.astype(BF16), doh, TN)

    row = pl.BlockSpec((tq, D_MODEL), lambda i: (i, 0))
    whole = pl.BlockSpec((mlen, 2 * D_MODEL), lambda i: (0, 0))
    return pl.pallas_call(
        body, name=name, grid=(t // tq,),
        out_shape=(jax.ShapeDtypeStruct((t, D_MODEL), BF16), jax.ShapeDtypeStruct((mlen, 2 * D_MODEL), F32)),
        in_specs=[row, whole, row], out_specs=(row, whole),
        compiler_params=_params(("arbitrary",), 6 * _nbytes((tq, D_MODEL), F32)),
    )(q, kv, do)


def _position():
    return lax.axis_index("x"), lax.axis_index("y"), lax.axis_index("c")


N_PEER = N_DEV - 1


def _all_gather(xs, name):
    n = len(xs)

    def body(*refs):
        x_refs, out_refs = refs[:n], refs[n:2 * n]
        send_sems, recv_sems, local_sems = refs[2 * n:]
        mx, my, mc = _position()
        me, sib = (mx, my, mc), (mx, my, 1 - mc)
        chips = [(1 - mx, my), (mx, 1 - my), (1 - mx, 1 - my)]

        def slot(i, p):
            return out_refs[i].at[4 * p[0] + 2 * p[1] + p[2]]

        def copy(i, k, block, to, src=None):
            return pltpu.make_async_remote_copy(
                src_ref=slot(i, block) if src is None else src, dst_ref=slot(i, block),
                send_sem=send_sems.at[i * N_PEER + k], recv_sem=recv_sems.at[i * N_PEER + k],
                device_id=to, device_id_type=MESH)

        mine = [pltpu.make_async_copy(x_refs[i], slot(i, me), local_sems.at[i]) for i in range(n)]
        for cp in mine:
            cp.start()
        first = [copy(i, 1 + j, me, (*chip, mc), src=x_refs[i]) for j, chip in enumerate(chips) for i in range(n)]
        first += [copy(i, 0, me, sib, src=x_refs[i]) for i in range(n)]
        for cp in first:
            cp.start()
        passed = []
        for j, chip in enumerate(chips):
            for i in range(n):
                copy(i, 1 + j, (*chip, mc), me).wait_recv()
                passed.append(copy(i, 4 + j, (*chip, mc), sib))
                passed[-1].start()
        for i in range(n):
            copy(i, 0, sib, me).wait_recv()
        for j, chip in enumerate(chips):
            for i in range(n):
                copy(i, 4 + j, (*chip, 1 - mc), me).wait_recv()
        for cp in first + passed:
            cp.wait_send()
        for cp in mine:
            cp.wait()

    return pl.pallas_call(
        body, name=name,
        out_shape=tuple(jax.ShapeDtypeStruct((N_DEV,) + x.shape, x.dtype) for x in xs),
        in_specs=[pl.BlockSpec(memory_space=pl.ANY)] * n, out_specs=(pl.BlockSpec(memory_space=pl.ANY),) * n,
        scratch_shapes=[pltpu.SemaphoreType.DMA((n * N_PEER,)), pltpu.SemaphoreType.DMA((n * N_PEER,)),
                        pltpu.SemaphoreType.DMA((n,))],
    )(*xs)


PEER_RELS = [(dx, dy, dc) for dx in (0, 1) for dy in (0, 1) for dc in (0, 1)][1:]


def _peer_copy(rel_k, i, src_refs, land_refs, send_sems, recv_sems, layer, gather, arriving):
    mx, my, mc = _position()
    me_idx = 4 * mx + 2 * my + mc
    p = tuple((1 - v) if f else v for f, v in zip(PEER_RELS[rel_k], (mx, my, mc)))
    p_idx = 4 * p[0] + 2 * p[1] + p[2]
    src_slot, dst_slot = (me_idx, p_idx) if arriving else (p_idx, me_idx)
    src = src_refs[i] if gather else src_refs[i].at[src_slot]
    dst = land_refs[i].at[dst_slot] if layer is None else land_refs[i].at[dst_slot, layer]
    return pltpu.make_async_remote_copy(
        src_ref=src, dst_ref=dst, send_sem=send_sems.at[i * N_PEER + rel_k], recv_sem=recv_sems.at[i * N_PEER + rel_k],
        device_id=p, device_id_type=MESH)


def _own_copy(i, src_refs, land_refs, sem, layer, gather):
    mx, my, mc = _position()
    me_idx = 4 * mx + 2 * my + mc
    src = src_refs[i] if gather else src_refs[i].at[me_idx]
    dst = land_refs[i].at[me_idx] if layer is None else land_refs[i].at[me_idx, layer]
    return pltpu.make_async_copy(src, dst, sem)


def _exchange(parts, lands, layer, name):
    n = len(parts)

    def body(*refs):
        g_refs, land_refs = refs[:n], refs[2 * n:3 * n]
        send_sems, recv_sems, local_sems = refs[3 * n:]
        args = (g_refs, land_refs, send_sems, recv_sems, layer, False)
        mine = [_own_copy(i, g_refs, land_refs, local_sems.at[i], layer, False) for i in range(n)]
        for cp in mine:
            cp.start()
        sends = [_peer_copy(k, i, *args, False) for i in range(n) for k in range(N_PEER)]
        for cp in sends:
            cp.start()
        for i in range(n):
            for k in range(N_PEER):
                _peer_copy(k, i, *args, True).wait_recv()
        for cp in sends:
            cp.wait_send()
        for cp in mine:
            cp.wait()

    return pl.pallas_call(
        body, name=name,
        out_shape=tuple(jax.ShapeDtypeStruct(l.shape, l.dtype) for l in lands),
        in_specs=[pl.BlockSpec(memory_space=pl.ANY)] * (2 * n), out_specs=(pl.BlockSpec(memory_space=pl.ANY),) * n,
        input_output_aliases={n + i: i for i in range(n)},
        scratch_shapes=[pltpu.SemaphoreType.DMA((n * N_PEER,)), pltpu.SemaphoreType.DMA((n * N_PEER,)),
                        pltpu.SemaphoreType.DMA((n,))],
    )(*parts, *lands)


HBM_SPEC = pl.BlockSpec(memory_space=pltpu.HBM)
SEM_SPEC = pl.BlockSpec(memory_space=pltpu.SEMAPHORE)
SIDE_EFFECT = pltpu.SideEffectType.DATAFLOW_SIDE_EFFECTING


def _own_slots(srcs, lands, layer, gather):
    mx, my, mc = _position()
    me_idx = 4 * mx + 2 * my + mc
    out = []
    for s, land in zip(srcs, lands):
        piece = s[None] if gather else lax.dynamic_index_in_dim(s, me_idx, 0, keepdims=True)
        if layer is None:
            start = (me_idx,) + (0,) * (land.ndim - 1)
        else:
            piece, start = piece[:, None], (me_idx, layer) + (0,) * (land.ndim - 2)
        out.append(lax.dynamic_update_slice(land, piece, start))
    return out


def _swap_start(srcs, lands, layer, gather, name):
    n = len(srcs)

    def body(*refs):
        src_refs, land_refs = refs[:n], refs[n:2 * n]
        send_sems, recv_sems = refs[2 * n], refs[2 * n + 1]
        token = refs[4 * n + 2]
        for i in range(n):
            for k in range(N_PEER):
                _peer_copy(k, i, src_refs, land_refs, send_sems, recv_sems, layer, gather, False).start()
        token[...] = jnp.zeros_like(token)

    hbm = [pltpu.with_memory_space_constraint(a, pltpu.HBM) for a in list(srcs) + list(lands)]
    out = pl.pallas_call(
        body, name=name,
        out_shape=(pltpu.SemaphoreType.DMA((n * N_PEER,)), pltpu.SemaphoreType.DMA((n * N_PEER,)))
        + tuple(pltpu.HBM(a.shape, a.dtype) for a in hbm) + (jax.ShapeDtypeStruct((8, LANE), F32),),
        in_specs=[HBM_SPEC] * (2 * n),
        out_specs=(SEM_SPEC, SEM_SPEC) + (HBM_SPEC,) * (2 * n) + (pl.BlockSpec(memory_space=pltpu.VMEM),),
        input_output_aliases={i: 2 + i for i in range(2 * n)},
        compiler_params=pltpu.CompilerParams(has_side_effects=SIDE_EFFECT),
    )(*hbm)
    return out[0], out[1], list(out[2:2 + n]), list(out[2 + n:2 + 2 * n]), out[2 + 2 * n]


def _swap_wait(send_sems, recv_sems, srcs, lands, after, layer, gather, name):
    n = len(srcs)

    def body(*refs):
        src_refs, land_refs = refs[:n], refs[n:2 * n]
        send_sems_ref, recv_sems_ref = refs[2 * n], refs[2 * n + 1]
        for i in range(n):
            for k in range(N_PEER):
                args = (src_refs, land_refs, send_sems_ref, recv_sems_ref, layer, gather)
                _peer_copy(k, i, *args, False).wait_send()
                _peer_copy(k, i, *args, True).wait_recv()

    out = pl.pallas_call(
        body, name=name,
        out_shape=tuple(pltpu.HBM(a.shape, a.dtype) for a in list(srcs) + list(lands)),
        in_specs=[HBM_SPEC] * (2 * n) + [SEM_SPEC, SEM_SPEC, pl.BlockSpec(memory_space=pl.ANY)],
        out_specs=(HBM_SPEC,) * (2 * n),
        input_output_aliases={i: i for i in range(2 * n)},
        compiler_params=pltpu.CompilerParams(has_side_effects=SIDE_EFFECT),
    )(*srcs, *lands, send_sems, recv_sems, after)
    return list(out[n:])


ADAMW_BLOCK_BYTES = 1 << 20


def _adamw(parts, w, m, v, name):
    r, l = w.shape
    tr = _pick(r, max(ADAMW_BLOCK_BYTES // (4 * l), BF16_SUBLANE), BF16_SUBLANE)
    c1 = 1.0 - ADAM_B1 ** ADAM_STEP
    c2 = 1.0 - ADAM_B2 ** ADAM_STEP

    def body(p_ref, w_ref, m_ref, v_ref, g_ref, d_ref, nm_ref, nv_ref):
        g = p_ref[0].astype(F32)
        for s in range(1, N_DEV):
            g = g + p_ref[s].astype(F32)
        nm = ADAM_B1 * m_ref[...] + (1.0 - ADAM_B1) * g
        nv = ADAM_B2 * v_ref[...] + (1.0 - ADAM_B2) * (g * g)
        m_hat = nm / c1
        v_hat = nv / c2
        g_ref[...] = g
        d_ref[...] = -ADAM_LR * (m_hat / (jnp.sqrt(v_hat) + ADAM_EPS) + ADAM_WD * w_ref[...])
        nm_ref[...] = nm
        nv_ref[...] = nv

    row = pl.BlockSpec((tr, l), lambda i: (i, 0))
    return pl.pallas_call(
        body, name=name, grid=(r // tr,),
        out_shape=(jax.ShapeDtypeStruct((r, l), F32),) * 4,
        in_specs=[pl.BlockSpec((N_DEV, tr, l), lambda i: (0, i, 0)), row, row, row],
        out_specs=(row,) * 4,
        compiler_params=_params(("parallel",), 12 * _nbytes((tr, l), F32)),
    )(parts, w, m, v)


MATRIX_WEIGHTS = (("w_in", 2), ("conv_w", 2), ("w_branch", 3), ("w_mix_out", 1), ("w_xq", 1), ("w_xkv", 2),
                  ("w_xo", 1), ("w_ffn_gate", 2), ("w_ffn_up", 2), ("w_ffn_down", 1))
SMALL_PARAMS = ("mix_norm_g", "xattn_norm_g", "mem_norm_g", "ffn_norm_g", "final_norm_g", "forget_bias", "sink",
                "rel_bias")


def _pack_small(pieces):
    flat = jnp.concatenate([p.astype(F32).reshape(-1) for p in pieces])
    total = -(-flat.shape[0] // (8 * LANE)) * (8 * LANE)
    return jnp.pad(flat, (0, total - flat.shape[0])).reshape(total // LANE, LANE)


def _rows(a):
    return a.reshape(-1, a.shape[-1])


def _to_full(gathered, axis):
    moved = jnp.moveaxis(gathered, 0, axis)
    shape = list(moved.shape)
    shape[axis:axis + 2] = [shape[axis] * shape[axis + 1]]
    return moved.reshape(shape)


def _to_blocks(full, axis):
    shape = list(full.shape)
    shape[axis:axis + 1] = [N_DEV, shape[axis] // N_DEV]
    return jnp.moveaxis(full.reshape(shape), axis, 0)


def _perm_in(w_in):
    pad = jnp.zeros((w_in.shape[0], PROJ_COLS - IN_COLS), w_in.dtype)
    return jnp.concatenate([w_in[:, 3848:6920], w_in[:, 0:3072], w_in[:, 3080:3848], w_in[:, 3072:3080], pad], axis=1)


def _unperm_in(dw):
    return jnp.concatenate([dw[:, 3072:6144], dw[:, 6912:6920], dw[:, 6144:6912], dw[:, 0:3072]], axis=1)


def _layer_fwd(l, x, mem, wt, sm):
    t = x.shape[0]
    tag = f"l{l}_"
    h = _rms_fwd(x, sm["mix_norm_g"][l], tag + "mix_norm")
    proj = _matmul(h, wt["w_in"][l], "nn", F32, tag + "in_proj")
    y_conv = _conv_fwd(proj, wt["conv_w"][l], tag + "conv")
    fbias_row = jnp.pad(sm["forget_bias"][l], (0, LANE - 8)).reshape(1, LANE)
    c = _logf_cumsum(proj, fbias_row, tag + "logf_cumsum")
    c8 = c[:, :8].T
    c_col = jnp.broadcast_to(c8[:, :, None], (8, t, LANE))
    c_row = c8.reshape(8, 1, t)
    y_fox, lse_fox = _fox_fwd(proj, c_col, c_row, tag + "fox")
    onehot, bias = _swa_tables(sm["rel_bias"])
    sink_rep = jnp.broadcast_to(sm["sink"][l].reshape(2, SWA_GROUP, 1), (2, SWA_GROUP, LANE))
    y_swa, lse_swa = _swa_fwd(proj, bias, sink_rep, tag + "swa")
    ys = (y_conv, y_fox, y_swa)
    pb = tuple(_matmul(ys[b], wt["w_branch"][l][b], "nn", F32, tag + f"branch{b}") for b in range(3))
    merged = _gate_fwd(proj, pb, tag + "gate")
    x1 = _matmul(merged, wt["w_mix_out"][l], "nn", F32, tag + "mix_out", residual=x)
    xn2 = _rms_fwd(x1, sm["xattn_norm_g"][l], tag + "xattn_norm")
    q = _matmul(xn2, wt["w_xq"][l], "nn", BF16, tag + "xq")
    mem_n = _rms_fwd(mem, sm["mem_norm_g"][l], tag + "mem_norm")
    kv = _matmul(mem_n, wt["w_xkv"][l], "nn", BF16, tag + "xkv")
    o = _xattn_fwd(q, kv, tag + "xattn")
    x2 = _matmul(o, wt["w_xo"][l], "nn", F32, tag + "xo", residual=x1)
    xn3 = _rms_fwd(x2, sm["ffn_norm_g"][l], tag + "ffn_norm")
    ab = _matmul(xn3, wt["w_gu"][l], "nn", F32, tag + "ffn_gu")
    h1 = _swiglu_fwd(ab, tag + "swiglu")
    x3 = _matmul(h1, wt["w_ffn_down"][l], "nn", F32, tag + "ffn_down", residual=x2)
    saved = dict(x=x, h=h, proj=proj, fbias_row=fbias_row, c_col=c_col, c_row=c_row, ys=ys, lse_fox=lse_fox,
                 onehot=onehot, bias=bias, sink_rep=sink_rep, lse_swa=lse_swa, pb=pb, merged=merged, x1=x1,
                 xn2=xn2, q=q, mem_n=mem_n, kv=kv, o=o, x2=x2, xn3=xn3, ab=ab, h1=h1)
    return x3, saved


def _layer_bwd(l, dx3, dx3_b, mem, wt, sm, sv, mid_hook=None):
    t = dx3.shape[0]
    nb = t // SWA_BLOCK
    tag = f"l{l}_b_"
    gw, gs = {}, {}
    dh1 = _matmul(dx3_b, wt["w_ffn_down"][l], "nt", F32, tag + "d_h1")
    gw["w_ffn_down"] = _matmul(sv["h1"], dx3_b, "tn", F32, tag + "dw_down")
    dab = _swiglu_bwd(sv["ab"], dh1, tag + "swiglu")
    dxn3 = _matmul(dab, wt["w_gu"][l], "nt", F32, tag + "d_xn3")
    dw_gu = _matmul(sv["xn3"], dab, "tn", F32, tag + "dw_gu")
    gw["w_ffn_gate"], gw["w_ffn_up"] = dw_gu[:, :D_FF], dw_gu[:, D_FF:]
    dx2, dx2_b, gs["ffn_norm_g"] = _rms_bwd(sv["x2"], sm["ffn_norm_g"][l], dxn3, dx3, tag + "ffn_norm")
    do = _matmul(dx2_b, wt["w_xo"][l], "nt", BF16, tag + "d_o")
    gw["w_xo"] = _matmul(sv["o"], dx2_b, "tn", F32, tag + "dw_xo")
    dq, dkv = _xattn_bwd(sv["q"], sv["kv"], do, tag + "xattn")
    gw["w_xkv"] = _matmul(sv["mem_n"], dkv, "tn", F32, tag + "dw_xkv")
    dmem_n = _matmul(dkv, wt["w_xkv"][l], "nt", F32, tag + "d_memn")
    _, _, gs["mem_norm_g"] = _rms_bwd(mem, sm["mem_norm_g"][l], dmem_n, None, tag + "mem_norm")
    gw["w_xq"] = _matmul(sv["xn2"], dq, "tn", F32, tag + "dw_xq")
    dxn2 = _matmul(dq, wt["w_xq"][l], "nt", F32, tag + "d_xn2")
    dx1, dx1_b, gs["xattn_norm_g"] = _rms_bwd(sv["x1"], sm["xattn_norm_g"][l], dxn2, dx2, tag + "xattn_norm")
    dmerged = _matmul(dx1_b, wt["w_mix_out"][l], "nt", F32, tag + "d_merged")
    gw["w_mix_out"] = _matmul(sv["merged"], dx1_b, "tn", F32, tag + "dw_mix_out")
    dp0, dp1, dp2, dgate = _gate_bwd(sv["proj"], sv["pb"], dmerged, tag + "gate")
    dps = (dp0, dp1, dp2)
    dy_dtypes = (F32, BF16, BF16)
    dys = [_matmul(dps[b], wt["w_branch"][l][b], "nt", dy_dtypes[b], tag + f"d_y{b}") for b in range(3)]
    gw["w_branch"] = jnp.stack(
        [_matmul(sv["ys"][b], dps[b], "tn", F32, tag + f"dw_branch{b}") for b in range(3)])
    sink_rep = sv["sink_rep"] if mid_hook is None else sv["sink_rep"] + mid_hook(gw)
    dsq, dkp, dvp, dbias, dsink = _swa_bwd(sv["proj"], sv["bias"], sink_rep, sv["lse_swa"], sv["ys"][2],
                                           dys[2], tag + "swa")

    def band_add(part):
        tot = part[0] + part[1]
        cur = tot[:, SWA_BLOCK:, :]
        nxt = jnp.concatenate([tot[1:, :SWA_BLOCK, :], jnp.zeros((1, SWA_BLOCK, LANE), F32)], axis=0)
        return (cur + nxt).reshape(t, LANE).astype(BF16)

    dsk, dsv = band_add(dkp), band_add(dvp)
    gs["rel_bias_l"] = jnp.einsum("hts,tsb->bh", dbias.reshape(8, SWA_BLOCK, 2 * SWA_BLOCK), sv["onehot"],
                                  precision=lax.Precision.HIGHEST)
    gs["sink"] = dsink[:, :, 0].reshape(8)
    lse_row = sv["lse_fox"][:, :, 0].reshape(8, 1, t)
    dfq, dfk, dfv, dck, dcq_row = _fox_bwd(sv["proj"], sv["c_col"], sv["c_row"], lse_row, sv["ys"][1], dys[1],
                                           tag + "fox_bwd")
    dcq = jnp.pad(dcq_row.reshape(8, t).T, ((0, 0), (0, LANE - 8))).reshape(1, t, LANE)
    dfg, dfb = _logf_cumsum_bwd(sv["proj"], sv["fbias_row"], [dck, dcq], tag + "logf_cumsum")
    gs["forget_bias"] = dfb[0, :8]
    dcb, dcc, dcu, dconv = _conv_bwd(sv["proj"], wt["conv_w"][l], dys[0], tag + "conv")
    gw["conv_w"] = dconv[:3]
    dproj = jnp.concatenate([dgate, dcb, dcc, dcu, dfq, dfk, dfv, dsq, dsk, dsv, dfg], axis=1)
    dh = _matmul(dproj, wt["w_in"][l], "nt", F32, tag + "d_h")
    gw["w_in"] = _unperm_in(_matmul(sv["h"], dproj, "tn", F32, tag + "dw_in"))
    dx, dx_b, gs["mix_norm_g"] = _rms_bwd(sv["x"], sm["mix_norm_g"][l], dh, dx1, tag + "mix_norm")
    return dx, dx_b, gw, gs


def kernel(x, mem, mix_norm_g, w_in, forget_bias, conv_w, sink, w_branch, w_mix_out, rel_bias, xattn_norm_g, mem_norm_g, w_xq, w_xkv, w_xo, ffn_norm_g, w_ffn_gate, w_ffn_up, w_ffn_down, final_norm_g, loss_target, m_mix_norm_g, m_w_in, m_forget_bias, m_conv_w, m_sink, m_w_branch, m_w_mix_out, m_rel_bias, m_xattn_norm_g, m_mem_norm_g, m_w_xq, m_w_xkv, m_w_xo, m_ffn_norm_g, m_w_ffn_gate, m_w_ffn_up, m_w_ffn_down, m_final_norm_g, v_mix_norm_g, v_w_in, v_forget_bias, v_conv_w, v_sink, v_w_branch, v_w_mix_out, v_rel_bias, v_xattn_norm_g, v_mem_norm_g, v_w_xq, v_w_xkv, v_w_xo, v_ffn_norm_g, v_w_ffn_gate, v_w_ffn_up, v_w_ffn_down, v_final_norm_g):
    args = dict(locals())
    names = [n for n, _ in MATRIX_WEIGHTS] + list(SMALL_PARAMS)
    w = {n: args[n] for n in names}
    mo = {n: args["m_" + n] for n in names}
    vo = {n: args["v_" + n] for n in names}
    x2d, mem2d, tgt = x[0], mem[0], loss_target[0]

    wire = {n: (F32 if n == "conv_w" else BF16) for n, _ in MATRIX_WEIGHTS}
    wt = {n: [None] * DEPTH for n, _ in MATRIX_WEIGHTS}
    wt["w_gu"] = [None] * DEPTH

    def place_weights(l, gathered):
        for (n, ax), g in zip(MATRIX_WEIGHTS, gathered):
            wt[n][l] = _to_full(g, ax - 1)
        wt["w_in"][l] = _perm_in(wt["w_in"][l])
        wt["w_gu"][l] = jnp.concatenate([wt["w_ffn_gate"][l], wt["w_ffn_up"][l]], axis=1)

    shards = [[w[n][l].astype(wire[n]) for n, _ in MATRIX_WEIGHTS] for l in range(DEPTH)]
    place_weights(0, _all_gather(shards[0], "weights_gather_l0"))
    lands = _own_slots(shards[1], [lax.empty((N_DEV,) + s.shape, s.dtype) for s in shards[1]], None, True)
    w_send, w_recv, w_srcs, lands, token = _swap_start(shards[1], lands, None, True, "weights_gather_l1_start")
    sm = {n: w[n] for n in SMALL_PARAMS}
    sm["mix_norm_g"] = w["mix_norm_g"].at[0].add(token[0, 0])

    saved = []
    xc = x2d
    for l in range(DEPTH):
        if l == 1:
            place_weights(1, _swap_wait(w_send, w_recv, w_srcs, lands, xc, None, True, "weights_gather_l1_wait"))
        xc, sv = _layer_fwd(l, xc, mem2d, wt, sm)
        saved.append(sv)
    loss_row, dx, dx_b, dg_final = _loss_head(xc, sm["final_norm_g"], tgt, "loss_head")
    loss = lax.psum(loss_row[0, 0], ("x", "y", "c"))

    late = ("w_in", "conv_w")
    early_w = [(n, ax) for n, ax in MATRIX_WEIGHTS if n not in late]
    late_w = [(n, ax) for n, ax in MATRIX_WEIGHTS if n in late]

    def grad_parts(gw, which):
        return [_to_blocks(gw[n], ax - 1).astype(wire[n]) for n, ax in which]

    gw_all, gs_all = [None] * DEPTH, [None] * DEPTH
    dx, dx_b, gw_all[1], gs_all[1] = _layer_bwd(1, dx, dx_b, mem2d, wt, sm, saved[1])
    parts1 = grad_parts(gw_all[1], MATRIX_WEIGHTS)
    zones = _own_slots(parts1, [lax.empty((N_DEV, DEPTH) + p.shape[1:], p.dtype) for p in parts1], 1, False)
    g_send, g_recv, g_srcs, zones, token = _swap_start(parts1, zones, 1, False, "grads_exchange_l1_start")
    sm_b = dict(sm)
    sm_b["ffn_norm_g"] = sm["ffn_norm_g"].at[0].add(token[0, 0])
    mid = {}

    def mid_hook(gw):
        zone = dict(zip([n for n, _ in MATRIX_WEIGHTS],
                        _swap_wait(g_send, g_recv, g_srcs, zones, gw["w_mix_out"], 1, False, "grads_exchange_l1_wait")))
        parts0 = grad_parts(gw, early_w)
        early_zones = _own_slots(parts0, [zone[n] for n, _ in early_w], 0, False)
        mid["early"] = _swap_start(parts0, early_zones, 0, False, "grads_exchange_l0_early_start")
        mid["late_zones"] = [zone[n] for n, _ in late_w]
        return mid["early"][4][0, 0]

    dx, dx_b, gw_all[0], gs_all[0] = _layer_bwd(0, dx, dx_b, mem2d, wt, sm_b, saved[0], mid_hook)
    grad_x = dx[None]
    e_send, e_recv, e_srcs, e_zones, _ = mid["early"]
    recv_by_name = dict(zip([n for n, _ in early_w],
                            _swap_wait(e_send, e_recv, e_srcs, e_zones, dx, 0, False, "grads_exchange_l0_early_wait")))
    recv_by_name.update(zip([n for n, _ in late_w],
                            _exchange(grad_parts(gw_all[0], late_w), mid["late_zones"], 0, "grads_exchange_l0_late")))
    recv = [recv_by_name[n] for n, _ in MATRIX_WEIGHTS]

    outs = {}
    for (n, _), r in zip(MATRIX_WEIGHTS, recv):
        res = _adamw(r.reshape((N_DEV,) + _rows(w[n]).shape), _rows(w[n]), _rows(mo[n]), _rows(vo[n]), "adamw_" + n)
        outs[n] = [o.reshape(w[n].shape) for o in res]

    gsm = {n: jnp.stack([gs_all[l][n] for l in range(DEPTH)])
           for n in ("mix_norm_g", "xattn_norm_g", "mem_norm_g", "ffn_norm_g", "forget_bias", "sink")}
    gsm["final_norm_g"] = dg_final
    gsm["rel_bias"] = gs_all[0]["rel_bias_l"] + gs_all[1]["rel_bias_l"]
    (small_parts,) = _all_gather([_pack_small([gsm[n] for n in SMALL_PARAMS])], "small_grads_all_gather")
    outs_small = _adamw(small_parts, *[_pack_small([d[n] for n in SMALL_PARAMS]) for d in (w, mo, vo)], "adamw_small")
    for kind in range(4):
        flat, o = outs_small[kind].reshape(-1), 0
        for n in SMALL_PARAMS:
            sz = int(np.prod(w[n].shape))
            outs.setdefault(n, []).append(flat[o:o + sz].reshape(w[n].shape))
            o += sz

    order = ["mix_norm_g", "w_in", "forget_bias", "conv_w", "sink", "w_branch", "w_mix_out", "rel_bias",
             "xattn_norm_g", "mem_norm_g", "w_xq", "w_xkv", "w_xo", "ffn_norm_g", "w_ffn_gate", "w_ffn_up",
             "w_ffn_down", "final_norm_g"]
    result = [loss, grad_x]
    for kind in range(4):
        result += [outs[n][kind] for n in order]
    return tuple(result)
```

```python
import math

import numpy as np
import jax
import jax.numpy as jnp
from jax import lax
from jax.experimental import pallas as pl
from jax.experimental.pallas import tpu as pltpu

F32 = jnp.float32
BF16 = jnp.bfloat16
MESH = pl.DeviceIdType.MESH

LANE = 128
BF16_SUBLANE = 16
V7X_VMEM_REQUEST_CAP = 56 * 2 ** 20
N_DEV = 8

D_MODEL = 1024
DEPTH = 2
HEAD = 64
BRANCH = 512
SWA_BLOCK = 128
SWA_GROUP = 4
N_BUCKETS = 32
X_HEADS = 4
X_HEAD = 256
D_FF = 2816
RMS_EPS = 1e-6
NEG = -1e30
ADAM_LR, ADAM_B1, ADAM_B2, ADAM_EPS, ADAM_WD, ADAM_STEP = 0.001, 0.9, 0.999, 1e-08, 0.01, 10

IN_COLS = 6920
PROJ_COLS = 7040
COL_GATE, COL_CONV, COL_FOX, COL_SQ, COL_SK, COL_SV, COL_FG = 0, 3072, 4608, 6144, 6656, 6784, 6912

ROW_TILE = 512
FOX_TILE = 512
MM_TM, MM_TN, MM_TK = 1024, 1536, 1024


def _pick(n, cap, mult):
    best = None
    for d in range(mult, min(n, cap) + 1, mult):
        if n % d == 0:
            best = d
    return n if best is None else best


def _params(semantics, block_bytes):
    limit = int(min(max(2 * block_bytes + (8 << 20), 24 << 20), V7X_VMEM_REQUEST_CAP))
    return pltpu.CompilerParams(dimension_semantics=semantics, vmem_limit_bytes=limit)


def _nbytes(shape, dtype):
    return int(np.prod(shape)) * jnp.dtype(dtype).itemsize


def _dot(a, b, dims):
    return lax.dot_general(a, b, (dims, ((), ())), preferred_element_type=F32)


NN = ((1,), (0,))
NT = ((1,), (1,))
TN = ((0,), (0,))


def _matmul(a, b, mode, out_dtype, name, residual=None):
    if mode == "nn":
        (m, k), (k2, n) = a.shape, b.shape
    elif mode == "nt":
        (m, k), (n, k2) = a.shape, b.shape
    else:
        (k, m), (k2, n) = a.shape, b.shape
    assert k == k2, (name, a.shape, b.shape)
    tm, tn, tk = _pick(m, MM_TM, LANE), _pick(n, MM_TN, LANE), _pick(k, MM_TK, LANE)
    nk = k // tk
    dims = {"nn": NN, "nt": NT, "tn": TN}[mode]
    has_res = residual is not None

    def body(*refs):
        a_ref, b_ref = refs[0], refs[1]
        r_ref = refs[2] if has_res else None
        o_ref = refs[3] if has_res else refs[2]
        kk = pl.program_id(2)
        p = _dot(a_ref[...].astype(BF16), b_ref[...].astype(BF16), dims)
        if nk == 1:
            if has_res:
                p = p + r_ref[...]
            o_ref[...] = p.astype(out_dtype)
        else:
            acc_ref = refs[-1]

            @pl.when(kk == 0)
            def _():
                acc_ref[...] = p

            @pl.when(kk > 0)
            def _():
                acc_ref[...] += p

            @pl.when(kk == nk - 1)
            def _():
                res = acc_ref[...]
                if has_res:
                    res = res + r_ref[...]
                o_ref[...] = res.astype(out_dtype)

    if mode == "nn":
        a_spec = pl.BlockSpec((tm, tk), lambda i, j, kk: (i, kk))
        b_spec = pl.BlockSpec((tk, tn), lambda i, j, kk: (kk, j))
    elif mode == "nt":
        a_spec = pl.BlockSpec((tm, tk), lambda i, j, kk: (i, kk))
        b_spec = pl.BlockSpec((tn, tk), lambda i, j, kk: (j, kk))
    else:
        a_spec = pl.BlockSpec((tk, tm), lambda i, j, kk: (kk, i))
        b_spec = pl.BlockSpec((tk, tn), lambda i, j, kk: (kk, j))
    o_spec = pl.BlockSpec((tm, tn), lambda i, j, kk: (i, j))
    in_specs, args = [a_spec, b_spec], [a, b]
    if has_res:
        in_specs.append(o_spec)
        args.append(residual)
    blk = (_nbytes((tm, tk), a.dtype) + _nbytes((tk, tn), b.dtype) + _nbytes((tm, tn), out_dtype)
           + (_nbytes((tm, tn), F32) if has_res else 0))
    scratch = [pltpu.VMEM((tm, tn), F32)] if nk > 1 else []
    return pl.pallas_call(
        body, name=name, grid=(m // tm, n // tn, nk),
        out_shape=jax.ShapeDtypeStruct((m, n), out_dtype),
        in_specs=in_specs, out_specs=o_spec, scratch_shapes=scratch,
        compiler_params=_params(("parallel", "parallel", "arbitrary"), blk + _nbytes((tm, tn), F32)),
    )(*args)


def _rms_fwd(x, g, name):
    t, d = x.shape
    tr = _pick(t, ROW_TILE, BF16_SUBLANE)

    def body(x_ref, g_ref, y_ref):
        xv = x_ref[...]
        r = lax.rsqrt(jnp.mean(xv * xv, axis=-1, keepdims=True) + RMS_EPS)
        y_ref[...] = ((xv * r) * g_ref[...]).astype(BF16)

    return pl.pallas_call(
        body, name=name, grid=(t // tr,),
        out_shape=jax.ShapeDtypeStruct((t, d), BF16),
        in_specs=[pl.BlockSpec((tr, d), lambda i: (i, 0)), pl.BlockSpec((1, d), lambda i: (0, 0))],
        out_specs=pl.BlockSpec((tr, d), lambda i: (i, 0)),
        compiler_params=_params(("parallel",), 2 * _nbytes((tr, d), F32)),
    )(x, g.reshape(1, d))


def _rms_bwd(x, g, dy, dres, name):
    t, d = x.shape
    tr = _pick(t, ROW_TILE, BF16_SUBLANE)
    has_res = dres is not None

    def body(*refs):
        x_ref, g_ref, dy_ref = refs[:3]
        r_ref = refs[3] if has_res else None
        dx_ref, dxb_ref, dg_ref = refs[-3:]
        i = pl.program_id(0)
        xv = x_ref[...]
        r = lax.rsqrt(jnp.mean(xv * xv, axis=-1, keepdims=True) + RMS_EPS)
        xh = xv * r
        dyv = dy_ref[...].astype(F32)
        dxh = dyv * g_ref[...]
        dx = r * (dxh - xh * jnp.mean(dxh * xh, axis=-1, keepdims=True))
        if has_res:
            dx = dx + r_ref[...]
        dx_ref[...] = dx
        dxb_ref[...] = dx.astype(BF16)

        @pl.when(i == 0)
        def _():
            dg_ref[...] = jnp.zeros_like(dg_ref)

        dg_ref[...] += jnp.sum(dyv * xh, axis=0, keepdims=True)

    row = pl.BlockSpec((tr, d), lambda i: (i, 0))
    vec = pl.BlockSpec((1, d), lambda i: (0, 0))
    in_specs, args = [row, vec, row], [x, g.reshape(1, d), dy]
    if has_res:
        in_specs.append(row)
        args.append(dres)
    return pl.pallas_call(
        body, name=name, grid=(t // tr,),
        out_shape=(jax.ShapeDtypeStruct((t, d), F32), jax.ShapeDtypeStruct((t, d), BF16),
                   jax.ShapeDtypeStruct((1, d), F32)),
        in_specs=in_specs, out_specs=(row, row, vec),
        compiler_params=_params(("arbitrary",), 5 * _nbytes((tr, d), F32)),
    )(*args)


def _loss_head(x, g, target, name):
    t, d = x.shape
    tr = _pick(t, ROW_TILE, BF16_SUBLANE)

    def body(x_ref, g_ref, t_ref, loss_ref, dx_ref, dxb_ref, dg_ref):
        i = pl.program_id(0)
        xv = x_ref[...]
        gv = g_ref[...]
        r = lax.rsqrt(jnp.mean(xv * xv, axis=-1, keepdims=True) + RMS_EPS)
        xh = xv * r
        diff = xh * gv - t_ref[...]
        part = 0.5 * jnp.sum(jnp.mean(diff * diff, axis=-1, keepdims=True), axis=0, keepdims=True)
        dyv = diff * (1.0 / d)
        dxh = dyv * gv
        dx = r * (dxh - xh * jnp.mean(dxh * xh, axis=-1, keepdims=True))
        dx_ref[...] = dx
        dxb_ref[...] = dx.astype(BF16)

        @pl.when(i == 0)
        def _():
            dg_ref[...] = jnp.zeros_like(dg_ref)
            loss_ref[...] = jnp.zeros_like(loss_ref)

        dg_ref[...] += jnp.sum(dyv * xh, axis=0, keepdims=True)
        loss_ref[...] += jnp.broadcast_to(part, loss_ref.shape)

    row = pl.BlockSpec((tr, d), lambda i: (i, 0))
    vec = pl.BlockSpec((1, d), lambda i: (0, 0))
    return pl.pallas_call(
        body, name=name, grid=(t // tr,),
        out_shape=(jax.ShapeDtypeStruct((1, LANE), F32), jax.ShapeDtypeStruct((t, d), F32),
                   jax.ShapeDtypeStruct((t, d), BF16), jax.ShapeDtypeStruct((1, d), F32)),
        in_specs=[row, vec, row],
        out_specs=(pl.BlockSpec((1, LANE), lambda i: (0, 0)), row, row, vec),
        compiler_params=_params(("arbitrary",), 5 * _nbytes((tr, d), F32)),
    )(x, g.reshape(1, d), target)


HALO = 8


def _conv_fwd(proj, conv_w, name):
    t = proj.shape[0]
    tr = _pick(t, ROW_TILE, BF16_SUBLANE)
    c0 = COL_CONV // BRANCH
    hb = tr // HALO

    def body(cb_ref, cc_ref, cu_ref, hc_ref, hu_ref, w_ref, y_ref):
        i = pl.program_id(0)
        z = cc_ref[...] * cu_ref[...]
        hz = jnp.where(i > 0, hc_ref[...] * hu_ref[...], 0.0)
        zf = jnp.concatenate([hz, z], axis=0)
        z1 = pltpu.roll(zf, 1, 0)[HALO:]
        z2 = pltpu.roll(zf, 2, 0)[HALO:]
        y = w_ref[2:3, :] * z + w_ref[1:2, :] * z1 + w_ref[0:1, :] * z2
        y_ref[...] = (cb_ref[...] * y).astype(BF16)

    def col(c):
        return pl.BlockSpec((tr, BRANCH), lambda i, c=c: (i, c0 + c))

    def prev(c):
        return pl.BlockSpec((HALO, BRANCH), lambda i, c=c: (jnp.maximum(i * hb - 1, 0), c0 + c))

    return pl.pallas_call(
        body, name=name, grid=(t // tr,),
        out_shape=jax.ShapeDtypeStruct((t, BRANCH), BF16),
        in_specs=[col(0), col(1), col(2), prev(1), prev(2), pl.BlockSpec((3, BRANCH), lambda i: (0, 0))],
        out_specs=pl.BlockSpec((tr, BRANCH), lambda i: (i, 0)),
        compiler_params=_params(("parallel",), 6 * _nbytes((tr, BRANCH), F32)),
    )(proj, proj, proj, proj, proj, conv_w)


def _conv_bwd(proj, conv_w, dout, name):
    t = proj.shape[0]
    tr = _pick(t, ROW_TILE, BF16_SUBLANE)
    nblk = t // tr
    c0 = COL_CONV // BRANCH
    hb = tr // HALO
    last_halo = t // HALO - 1

    def body(cb_ref, cc_ref, cu_ref, hc_ref, hu_ref, do_ref, ndo_ref, ncb_ref, w_ref,
             dcb_ref, dcc_ref, dcu_ref, dw_ref):
        i = pl.program_id(0)
        cb, cc, cu = cb_ref[...], cc_ref[...], cu_ref[...]
        w0, w1, w2 = w_ref[0:1, :], w_ref[1:2, :], w_ref[2:3, :]
        z = cc * cu
        hz = jnp.where(i > 0, hc_ref[...] * hu_ref[...], 0.0)
        zf = jnp.concatenate([hz, z], axis=0)
        z1 = pltpu.roll(zf, 1, 0)[HALO:]
        z2 = pltpu.roll(zf, 2, 0)[HALO:]
        y = w2 * z + w1 * z1 + w0 * z2
        dout_v = do_ref[...]
        dyc = dout_v * cb
        hdy = jnp.where(i < nblk - 1, ndo_ref[...] * ncb_ref[...], 0.0)
        dyf = jnp.concatenate([dyc, hdy], axis=0)
        dy1 = pltpu.roll(dyf, tr + HALO - 1, 0)[:tr]
        dy2 = pltpu.roll(dyf, tr + HALO - 2, 0)[:tr]
        dz = w2 * dyc + w1 * dy1 + w0 * dy2
        dcb_ref[...] = (dout_v * y).astype(BF16)
        dcc_ref[...] = (dz * cu).astype(BF16)
        dcu_ref[...] = (dz * cc).astype(BF16)

        @pl.when(i == 0)
        def _():
            dw_ref[...] = jnp.zeros_like(dw_ref)

        dw_ref[0:1, :] += jnp.sum(dyc * z2, axis=0, keepdims=True)
        dw_ref[1:2, :] += jnp.sum(dyc * z1, axis=0, keepdims=True)
        dw_ref[2:3, :] += jnp.sum(dyc * z, axis=0, keepdims=True)

    def col(c):
        return pl.BlockSpec((tr, BRANCH), lambda i, c=c: (i, c0 + c))

    def prev(c):
        return pl.BlockSpec((HALO, BRANCH), lambda i, c=c: (jnp.maximum(i * hb - 1, 0), c0 + c))

    def nxt(c):
        return pl.BlockSpec((HALO, BRANCH), lambda i, c=c: (jnp.minimum((i + 1) * hb, last_halo), c))

    row = pl.BlockSpec((tr, BRANCH), lambda i: (i, 0))
    return pl.pallas_call(
        body, name=name, grid=(nblk,),
        out_shape=(jax.ShapeDtypeStruct((t, BRANCH), BF16),) * 3 + (jax.ShapeDtypeStruct((HALO, BRANCH), F32),),
        in_specs=[col(0), col(1), col(2), prev(1), prev(2), row, nxt(0), nxt(c0),
                  pl.BlockSpec((3, BRANCH), lambda i: (0, 0))],
        out_specs=(row, row, row, pl.BlockSpec((HALO, BRANCH), lambda i: (0, 0))),
        compiler_params=_params(("arbitrary",), 8 * _nbytes((tr, BRANCH), F32)),
    )(proj, proj, proj, proj, proj, dout, dout, proj, conv_w)


def _tri(lower):
    r = lax.broadcasted_iota(jnp.int32, (LANE, LANE), 0)
    c = lax.broadcasted_iota(jnp.int32, (LANE, LANE), 1)
    return jnp.where((c <= r) if lower else (c >= r), 1.0, 0.0).astype(F32)


def _logf_cumsum(proj, fbias_row, name):
    t = proj.shape[0]
    nchunk = t // LANE

    def body(f_ref, b_ref, c_ref, run_sc):
        tri = _tri(True)
        run_sc[...] = jnp.zeros_like(run_sc)

        @pl.loop(0, nchunk)
        def _(i):
            rows = pl.ds(pl.multiple_of(i * LANE, LANE), LANE)
            z = f_ref[rows, :] + b_ref[...]
            logf = jnp.minimum(z, 0.0) - jnp.log(1.0 + jnp.exp(-jnp.abs(z)))
            cs = lax.dot_general(tri, logf, (NN, ((), ())), precision=lax.Precision.HIGHEST,
                                 preferred_element_type=F32) + run_sc[0:1, :]
            c_ref[rows, :] = cs
            run_sc[0:1, :] = cs[LANE - 1:LANE, :]

    return pl.pallas_call(
        body, name=name, grid=(1,),
        out_shape=jax.ShapeDtypeStruct((t, LANE), F32),
        in_specs=[pl.BlockSpec((t, LANE), lambda i: (0, COL_FG // LANE)), pl.BlockSpec((1, LANE), lambda i: (0, 0))],
        out_specs=pl.BlockSpec((t, LANE), lambda i: (0, 0)),
        scratch_shapes=[pltpu.VMEM((8, LANE), F32)],
        compiler_params=_params(("arbitrary",), 2 * _nbytes((t, LANE), F32)),
    )(proj, fbias_row)


def _logf_cumsum_bwd(proj, fbias_row, pieces, name):
    t = proj.shape[0]
    tb = _pick(t, 2 * ROW_TILE, LANE)
    nblk = t // tb
    npiece = len(pieces)

    def body(*refs):
        f_ref, b_ref = refs[:2]
        piece_refs = refs[2:2 + npiece]
        df_ref, db_ref, run_sc = refs[2 + npiece:]
        i = pl.program_id(0)
        tri = _tri(False)

        @pl.when(i == 0)
        def _():
            run_sc[...] = jnp.zeros_like(run_sc)
            db_ref[...] = jnp.zeros_like(db_ref)

        for c in reversed(range(tb // LANE)):
            rows = slice(c * LANE, (c + 1) * LANE)
            slabs = [p_ref[n, rows, :] for p_ref in piece_refs for n in range(p_ref.shape[0])]
            dcc = slabs[0]
            for slab in slabs[1:]:
                dcc = dcc + slab
            ss = lax.dot_general(tri, dcc, (NN, ((), ())), precision=lax.Precision.HIGHEST,
                                 preferred_element_type=F32) + run_sc[0:1, :]
            z = f_ref[rows, :] + b_ref[...]
            dz = ss * (1.0 / (1.0 + jnp.exp(z)))
            df_ref[rows, :] = dz.astype(BF16)
            run_sc[0:1, :] = ss[0:1, :]
            db_ref[...] += jnp.sum(dz, axis=0, keepdims=True)

    piece_specs = [pl.BlockSpec((p.shape[0], tb, LANE), lambda i: (0, nblk - 1 - i, 0)) for p in pieces]
    nslab = sum(p.shape[0] for p in pieces)
    return pl.pallas_call(
        body, name=name, grid=(nblk,),
        out_shape=(jax.ShapeDtypeStruct((t, LANE), BF16), jax.ShapeDtypeStruct((1, LANE), F32)),
        in_specs=[pl.BlockSpec((tb, LANE), lambda i: (nblk - 1 - i, COL_FG // LANE)),
                  pl.BlockSpec((1, LANE), lambda i: (0, 0))] + piece_specs,
        out_specs=(pl.BlockSpec((tb, LANE), lambda i: (nblk - 1 - i, 0)), pl.BlockSpec((1, LANE), lambda i: (0, 0))),
        scratch_shapes=[pltpu.VMEM((8, LANE), F32)],
        compiler_params=_params(("arbitrary",), (4 + nslab) * _nbytes((tb, LANE), F32)),
    )(proj, fbias_row, *pieces)


def _lo_mask():
    return lax.broadcasted_iota(jnp.int32, (1, LANE), 1) < HEAD


def _causal_steps(n, key_major):
    if key_major:
        pairs = [(iq, ik) for ik in range(n) for iq in range(ik, n)]
    else:
        pairs = [(iq, ik) for iq in range(n) for ik in range(iq + 1)]
    return (jnp.asarray([p[0] for p in pairs], jnp.int32), jnp.asarray([p[1] for p in pairs], jnp.int32))


def _head_lanes(j, pair_vals):
    lane = lax.broadcasted_iota(jnp.int32, (1, LANE), 1)
    return jnp.where(lane == 2 * j, pair_vals[0], 0.0) + jnp.where(lane == 2 * j + 1, pair_vals[1], 0.0)


def _fox_fwd(proj, c_col, c_row, name):
    t = proj.shape[0]
    tq = _pick(t, FOX_TILE, LANE)
    nq = t // tq
    rep = tq // LANE
    scale = HEAD ** -0.5
    cq, ck, cv = COL_FOX // LANE, COL_FOX // LANE + 4, COL_FOX // LANE + 8
    q_tab, k_tab = _causal_steps(nq, False)

    def body(qt_ref, kt_ref, q_ref, k_ref, v_ref, ck_ref, cqr_ref, y_ref, lse_ref, m_sc, l_sc, acc_sc):
        step_id = pl.program_id(1)
        iq, ik = qt_ref[step_id], kt_ref[step_id]
        lo = _lo_mask()
        lo_rows = lax.broadcasted_iota(jnp.int32, (LANE, 1), 0) < HEAD

        @pl.when(ik == 0)
        def _():
            m_sc[...] = jnp.full(m_sc.shape, NEG, F32)
            l_sc[...] = jnp.zeros_like(l_sc)
            acc_sc[...] = jnp.zeros_like(acc_sc)

        def step(diag):
            q2 = q_ref[...].astype(BF16)
            k2 = k_ref[...].astype(BF16)
            v2 = v_ref[...].astype(BF16)
            alphas, adds = [], []
            for h in range(2):
                msk = lo if h == 0 else jnp.logical_not(lo)
                kh = jnp.where(msk, k2, jnp.zeros_like(k2))
                vh = jnp.where(msk, v2, jnp.zeros_like(v2))
                st = _dot(kh, q2, NT) * scale
                st = st + cqr_ref[h] - jnp.tile(ck_ref[h], (1, rep))
                if diag:
                    krow = lax.broadcasted_iota(jnp.int32, (tq, tq), 0)
                    qcol = lax.broadcasted_iota(jnp.int32, (tq, tq), 1)
                    st = jnp.where(krow <= qcol, st, NEG)
                m_prev = m_sc[h]
                m_new = jnp.maximum(m_prev, jnp.max(st, axis=0, keepdims=True))
                alpha = jnp.exp(m_prev - m_new)
                pt = jnp.exp(st - m_new)
                l_sc[h] = alpha * l_sc[h] + jnp.sum(pt, axis=0, keepdims=True)
                m_sc[h] = m_new
                alphas.append(alpha)
                adds.append(_dot(vh, pt.astype(BF16), TN))
            acc_sc[...] = acc_sc[...] * jnp.where(lo_rows, alphas[0], alphas[1]) + (adds[0] + adds[1])

        @pl.when(ik < iq)
        def _():
            step(False)

        @pl.when(ik == iq)
        def _():
            step(True)
            yt = acc_sc[...] / jnp.where(lo_rows, l_sc[0], l_sc[1])
            y_ref[...] = yt.T.astype(BF16)
            lse_ref[...] = m_sc[...] + jnp.log(l_sc[...])

    def kv(c):
        return pl.BlockSpec((tq, LANE), lambda j, s, qt, kt, c=c: (kt[s], c + j))

    qrow = pl.BlockSpec((2, 1, tq), lambda j, s, qt, kt: (j, 0, qt[s]))
    grid_spec = pltpu.PrefetchScalarGridSpec(
        num_scalar_prefetch=2, grid=(4, int(q_tab.shape[0])),
        in_specs=[pl.BlockSpec((tq, LANE), lambda j, s, qt, kt: (qt[s], cq + j)), kv(ck), kv(cv),
                  pl.BlockSpec((2, tq, LANE), lambda j, s, qt, kt: (j, kt[s], 0)), qrow],
        out_specs=(pl.BlockSpec((tq, LANE), lambda j, s, qt, kt: (qt[s], j)), qrow),
        scratch_shapes=[pltpu.VMEM((2, 1, tq), F32), pltpu.VMEM((2, 1, tq), F32), pltpu.VMEM((LANE, tq), F32)])
    return pl.pallas_call(
        body, name=name, grid_spec=grid_spec,
        out_shape=(jax.ShapeDtypeStruct((t, BRANCH), BF16), jax.ShapeDtypeStruct((8, 1, t), F32)),
        compiler_params=_params(("parallel", "arbitrary"),
                                16 * _nbytes((tq, LANE), F32) + 6 * _nbytes((tq, tq), F32)),
    )(q_tab, k_tab, proj, proj, proj, c_col, c_row)


def _fox_bwd(proj, c_col, c_row, lse_row, y, dy, name):
    t = proj.shape[0]
    tb = _pick(t, FOX_TILE, LANE)
    nb = t // tb
    rep = tb // LANE
    scale = HEAD ** -0.5
    cq, ck, cv = COL_FOX // LANE, COL_FOX // LANE + 4, COL_FOX // LANE + 8
    q_tab, k_tab = _causal_steps(nb, True)
    nsteps = int(q_tab.shape[0])

    def body(qt_ref, kt_ref, k_ref, v_ref, q_ref, y_ref, dy_ref, ck_ref, cqr_ref, lser_ref,
             dq_ref, dk_ref, dv_ref, dck_ref, dcq_ref, dk_sc, dv_sc, dc_sc, dqt_sc, dcq_sc, d_sc):
        j, step_id = pl.program_id(0), pl.program_id(1)
        iq, ik = qt_ref[step_id], kt_ref[step_id]
        lo = _lo_mask()

        @pl.when(step_id == 0)
        def _():
            dqt_sc[...] = jnp.zeros_like(dqt_sc)
            dcq_sc[...] = jnp.zeros_like(dcq_sc)

        @pl.when(iq == ik)
        def _():
            dk_sc[...] = jnp.zeros_like(dk_sc)
            dv_sc[...] = jnp.zeros_like(dv_sc)
            dc_sc[...] = jnp.zeros_like(dc_sc)

        @pl.when(ik == 0)
        def _():
            prod = y_ref[...].astype(F32) * dy_ref[...].astype(F32)
            row = lax.broadcasted_iota(jnp.int32, (8, LANE), 0)
            sel = jnp.logical_or(jnp.logical_and(row == 0, lo), jnp.logical_and(row == 1, jnp.logical_not(lo)))
            d_sc[iq] = lax.dot_general(jnp.where(sel, 1.0, 0.0).astype(F32), prod, (NT, ((), ())),
                                       precision=lax.Precision.HIGHEST, preferred_element_type=F32)

        def step(diag):
            k2 = k_ref[...].astype(BF16)
            v2 = v_ref[...].astype(BF16)
            q2 = q_ref[...].astype(BF16)
            do2 = dy_ref[...]
            d_rows = d_sc[iq]
            for h in range(2):
                msk = lo if h == 0 else jnp.logical_not(lo)
                kh = jnp.where(msk, k2, jnp.zeros_like(k2))
                vh = jnp.where(msk, v2, jnp.zeros_like(v2))
                st = _dot(kh, q2, NT) * scale
                st = st + cqr_ref[h] - jnp.tile(ck_ref[h], (1, rep))
                if diag:
                    krow = lax.broadcasted_iota(jnp.int32, (tb, tb), 0)
                    qcol = lax.broadcasted_iota(jnp.int32, (tb, tb), 1)
                    st = jnp.where(krow <= qcol, st, NEG)
                pt = jnp.exp(st - lser_ref[h])
                dpt = _dot(vh, do2, NT)
                dst = pt * (dpt - d_rows[h:h + 1, :])
                dsb = dst.astype(BF16)
                dv_sc[h] += _dot(pt.astype(BF16), do2, NN)
                dk_sc[h] += _dot(dsb, q2, NN)
                dc_sc[h] -= jnp.sum(dst, axis=1, keepdims=True)
                dqt_sc[iq] += _dot(kh, dsb, TN)
                dcq_sc[h, iq] += jnp.sum(dst, axis=0, keepdims=True)

        @pl.when(iq > ik)
        def _():
            step(False)

        @pl.when(iq == ik)
        def _():
            step(True)

        @pl.when(iq == nb - 1)
        def _():
            dk_ref[...] = (jnp.where(lo, dk_sc[0], dk_sc[1]) * scale).astype(BF16)
            dv_ref[...] = jnp.where(lo, dv_sc[0], dv_sc[1]).astype(BF16)
            dck_ref[...] = _head_lanes(j, dc_sc)

        @pl.when(step_id == nsteps - 1)
        def _():
            for i in range(nb):
                dq_ref[i * tb:(i + 1) * tb, :] = (dqt_sc[i].T * scale).astype(BF16)
                for h in range(2):
                    dcq_ref[h, :, i * tb:(i + 1) * tb] = dcq_sc[h, i]

    def kcol(c):
        return pl.BlockSpec((tb, LANE), lambda j, s, qt, kt, c=c: (kt[s], c + j))

    qrow = pl.BlockSpec((2, 1, tb), lambda j, s, qt, kt: (j, 0, qt[s]))
    pair_q = pl.BlockSpec((tb, LANE), lambda j, s, qt, kt: (qt[s], j))
    pair_k = pl.BlockSpec((tb, LANE), lambda j, s, qt, kt: (kt[s], j))
    grid_spec = pltpu.PrefetchScalarGridSpec(
        num_scalar_prefetch=2, grid=(4, nsteps),
        in_specs=[kcol(ck), kcol(cv), pl.BlockSpec((tb, LANE), lambda j, s, qt, kt: (qt[s], cq + j)), pair_q, pair_q,
                  pl.BlockSpec((2, tb, LANE), lambda j, s, qt, kt: (j, kt[s], 0)), qrow, qrow],
        out_specs=(pl.BlockSpec((t, LANE), lambda j, s, qt, kt: (0, j)), pair_k, pair_k,
                   pl.BlockSpec((None, tb, LANE), lambda j, s, qt, kt: (j, kt[s], 0)),
                   pl.BlockSpec((2, 1, t), lambda j, s, qt, kt: (j, 0, 0))),
        scratch_shapes=[pltpu.VMEM((2, tb, LANE), F32)] * 3
        + [pltpu.VMEM((nb, LANE, tb), F32), pltpu.VMEM((2, nb, 1, tb), F32), pltpu.VMEM((nb, 8, tb), F32)])
    return pl.pallas_call(
        body, name=name, grid_spec=grid_spec,
        out_shape=(jax.ShapeDtypeStruct((t, BRANCH), BF16), jax.ShapeDtypeStruct((t, BRANCH), BF16),
                   jax.ShapeDtypeStruct((t, BRANCH), BF16), jax.ShapeDtypeStruct((4, t, LANE), F32),
                   jax.ShapeDtypeStruct((8, 1, t), F32)),
        compiler_params=_params(("parallel", "arbitrary"),
                                24 * _nbytes((tb, LANE), F32) + 8 * _nbytes((tb, tb), F32)
                                + 2 * _nbytes((t, LANE), F32)),
    )(q_tab, k_tab, proj, proj, proj, y, dy, c_col, c_row, lse_row)


def _swa_tables(rel_bias):
    tq = np.arange(SWA_BLOCK)[:, None]
    sk = np.arange(2 * SWA_BLOCK)[None, :]
    dist = SWA_BLOCK + tq - sk
    inwin = (dist >= 0) & (dist < SWA_BLOCK)
    n = np.maximum(dist, 0)
    max_exact = N_BUCKETS // 2
    large = max_exact + (np.log(np.maximum(n, 1).astype(np.float32) / max_exact)
                         / math.log(SWA_BLOCK / max_exact) * (N_BUCKETS - max_exact)).astype(np.int32)
    bucket = np.where(n < max_exact, n, np.minimum(large, N_BUCKETS - 1))
    onehot = (bucket[..., None] == np.arange(N_BUCKETS)) & inwin[..., None]
    onehot = jnp.asarray(onehot.astype(np.float32))
    bias = jnp.einsum("tsb,bh->hts", onehot, rel_bias, precision=lax.Precision.HIGHEST)
    bias = jnp.where(jnp.asarray(inwin)[None], bias, NEG)
    return onehot, bias


def _swa_fwd(proj, bias, sink_rep, name):
    t = proj.shape[0]
    nb = t // SWA_BLOCK
    scale = HEAD ** -0.5
    csq, csk, csv = COL_SQ // 256, COL_SK // LANE, COL_SV // LANE

    def body(q_ref, kp_ref, kc_ref, vp_ref, vc_ref, b_ref, sk_ref, y_ref, lse_ref):
        kvh, n = pl.program_id(0), pl.program_id(1)
        lane = lax.broadcasted_iota(jnp.int32, (1, LANE), 1)
        lo = lane < HEAD
        kvm = jnp.logical_and(lane >= kvh * HEAD, lane < (kvh + 1) * HEAD)

        def both(prev_ref, cur_ref):
            band = jnp.concatenate([prev_ref[...], cur_ref[...]], axis=0)
            band = jnp.where(kvm, band, 0.0)
            return (band + pltpu.roll(band, HEAD, 1)).astype(BF16)

        kb, vb = both(kp_ref, kc_ref), both(vp_ref, vc_ref)
        col = lax.broadcasted_iota(jnp.int32, (SWA_BLOCK, 2 * SWA_BLOCK), 1)
        first = jnp.logical_and(n == 0, col < SWA_BLOCK)
        outs = []
        for g in range(SWA_GROUP):
            half = q_ref[:, (g // 2) * LANE:(g // 2 + 1) * LANE]
            hm = lo if g % 2 == 0 else jnp.logical_not(lo)
            qg = jnp.where(hm, half, 0.0).astype(BF16)
            s = _dot(qg, kb, NT) * scale + b_ref[g]
            s = jnp.where(first, NEG, s)
            snk = sk_ref[g:g + 1, :]
            m = jnp.maximum(jnp.max(s, axis=1, keepdims=True), snk)
            p = jnp.exp(s - jnp.tile(m, (1, 2)))
            denom = jnp.sum(p, axis=1, keepdims=True) + jnp.exp(snk - m)
            outs.append(_dot(p.astype(BF16), vb, NN) / denom)
            lse_ref[g] = m + jnp.log(denom)
        y_ref[:, 0:LANE] = jnp.where(lo, outs[0], outs[1]).astype(BF16)
        y_ref[:, LANE:2 * LANE] = jnp.where(lo, outs[2], outs[3]).astype(BF16)

    def blk(c, shift):
        return pl.BlockSpec((SWA_BLOCK, LANE), lambda kvh, n, c=c, s=shift: (jnp.maximum(n - s, 0), c))

    return pl.pallas_call(
        body, name=name, grid=(2, nb),
        out_shape=(jax.ShapeDtypeStruct((t, BRANCH), BF16), jax.ShapeDtypeStruct((8, t, LANE), F32)),
        in_specs=[pl.BlockSpec((SWA_BLOCK, 256), lambda kvh, n: (n, csq + kvh)),
                  blk(csk, 1), blk(csk, 0), blk(csv, 1), blk(csv, 0),
                  pl.BlockSpec((None, SWA_GROUP, SWA_BLOCK, 256), lambda kvh, n: (kvh, 0, 0, 0)),
                  pl.BlockSpec((None, SWA_GROUP, LANE), lambda kvh, n: (kvh, 0, 0))],
        out_specs=(pl.BlockSpec((SWA_BLOCK, 256), lambda kvh, n: (n, kvh)),
                   pl.BlockSpec((SWA_GROUP, SWA_BLOCK, LANE), lambda kvh, n: (kvh, n, 0))),
        compiler_params=_params(("parallel", "arbitrary"), 4 << 20),
    )(proj, proj, proj, proj, proj, bias.reshape(2, SWA_GROUP, SWA_BLOCK, 256), sink_rep)


def _swa_bwd(proj, bias, sink_rep, lse, y, dy, name):
    t = proj.shape[0]
    nb = t // SWA_BLOCK
    scale = HEAD ** -0.5
    csq, csk, csv = COL_SQ // 256, COL_SK // LANE, COL_SV // LANE

    def body(q_ref, kp_ref, kc_ref, vp_ref, vc_ref, b_ref, sk_ref, lse_ref, y_ref, dy_ref,
             dq_ref, dkp_ref, dvp_ref, db_ref, dsk_ref):
        kvh, n = pl.program_id(0), pl.program_id(1)
        lane = lax.broadcasted_iota(jnp.int32, (1, LANE), 1)
        lo = lane < HEAD
        kvm = jnp.logical_and(lane >= kvh * HEAD, lane < (kvh + 1) * HEAD)

        def both(prev_ref, cur_ref):
            band = jnp.concatenate([prev_ref[...], cur_ref[...]], axis=0)
            band = jnp.where(kvm, band, 0.0)
            return (band + pltpu.roll(band, HEAD, 1)).astype(BF16)

        kb, vb = both(kp_ref, kc_ref), both(vp_ref, vc_ref)
        col = lax.broadcasted_iota(jnp.int32, (SWA_BLOCK, 2 * SWA_BLOCK), 1)
        first = jnp.logical_and(n == 0, col < SWA_BLOCK)

        @pl.when(n == 0)
        def _():
            db_ref[...] = jnp.zeros_like(db_ref)
            dsk_ref[...] = jnp.zeros_like(dsk_ref)

        dk_full = jnp.zeros((2 * SWA_BLOCK, LANE), F32)
        dv_full = jnp.zeros((2 * SWA_BLOCK, LANE), F32)
        dqs = []
        for g in range(SWA_GROUP):
            sl = slice((g // 2) * LANE, (g // 2 + 1) * LANE)
            hm = lo if g % 2 == 0 else jnp.logical_not(lo)
            qg = jnp.where(hm, q_ref[:, sl], 0.0).astype(BF16)
            dog = jnp.where(hm, dy_ref[:, sl], jnp.zeros((SWA_BLOCK, LANE), BF16))
            dmat = jnp.where(hm, y_ref[:, sl].astype(F32) * dy_ref[:, sl].astype(F32), 0.0)
            dg = jnp.sum(dmat, axis=1, keepdims=True)
            s = _dot(qg, kb, NT) * scale + b_ref[g]
            s = jnp.where(first, NEG, s)
            lse_g = lse_ref[g]
            p = jnp.exp(s - jnp.tile(lse_g, (1, 2)))
            dp = _dot(dog, vb, NT)
            ds = p * (dp - dg)
            dsb = ds.astype(BF16)
            dqs.append(_dot(dsb, kb, NN) * scale)
            dk_full = dk_full + _dot(dsb, qg, TN)
            dv_full = dv_full + _dot(p.astype(BF16), dog, TN)
            db_ref[g] += ds
            psink = jnp.exp(sk_ref[g:g + 1, :] - lse_g)
            dsk_ref[g:g + 1, :] -= jnp.sum(psink * dg, axis=0, keepdims=True)
        dq_ref[:, 0:LANE] = jnp.where(lo, dqs[0], dqs[1]).astype(BF16)
        dq_ref[:, LANE:2 * LANE] = jnp.where(lo, dqs[2], dqs[3]).astype(BF16)
        dkp_ref[...] = jnp.where(kvm, (dk_full + pltpu.roll(dk_full, HEAD, 1)) * scale, 0.0)
        dvp_ref[...] = jnp.where(kvm, dv_full + pltpu.roll(dv_full, HEAD, 1), 0.0)

    def blk(c, shift):
        return pl.BlockSpec((SWA_BLOCK, LANE), lambda kvh, n, c=c, s=shift: (jnp.maximum(n - s, 0), c))

    qblk = pl.BlockSpec((SWA_BLOCK, 256), lambda kvh, n: (n, kvh))
    part = pl.BlockSpec((None, None, 2 * SWA_BLOCK, LANE), lambda kvh, n: (kvh, n, 0, 0))
    bspec = pl.BlockSpec((None, SWA_GROUP, SWA_BLOCK, 256), lambda kvh, n: (kvh, 0, 0, 0))
    sspec = pl.BlockSpec((None, SWA_GROUP, LANE), lambda kvh, n: (kvh, 0, 0))
    return pl.pallas_call(
        body, name=name, grid=(2, nb),
        out_shape=(jax.ShapeDtypeStruct((t, BRANCH), BF16),
                   jax.ShapeDtypeStruct((2, nb, 2 * SWA_BLOCK, LANE), F32),
                   jax.ShapeDtypeStruct((2, nb, 2 * SWA_BLOCK, LANE), F32),
                   jax.ShapeDtypeStruct((2, SWA_GROUP, SWA_BLOCK, 256), F32),
                   jax.ShapeDtypeStruct((2, SWA_GROUP, LANE), F32)),
        in_specs=[pl.BlockSpec((SWA_BLOCK, 256), lambda kvh, n: (n, csq + kvh)),
                  blk(csk, 1), blk(csk, 0), blk(csv, 1), blk(csv, 0), bspec, sspec,
                  pl.BlockSpec((SWA_GROUP, SWA_BLOCK, LANE), lambda kvh, n: (kvh, n, 0)), qblk, qblk],
        out_specs=(qblk, part, part, bspec, sspec),
        compiler_params=_params(("parallel", "arbitrary"), 6 << 20),
    )(proj, proj, proj, proj, proj, bias.reshape(2, SWA_GROUP, SWA_BLOCK, 256), sink_rep, lse, y, dy)


def _gate_fwd(proj, pb, name):
    t = proj.shape[0]
    tr = _pick(t, ROW_TILE // 2, BF16_SUBLANE)

    def body(g0, g1, g2, p0, p1, p2, o_ref):
        acc = jax.nn.sigmoid(g0[...]) * p0[...]
        acc = acc + jax.nn.sigmoid(g1[...]) * p1[...]
        acc = acc + jax.nn.sigmoid(g2[...]) * p2[...]
        o_ref[...] = acc.astype(BF16)

    row = pl.BlockSpec((tr, D_MODEL), lambda i: (i, 0))
    gates = [pl.BlockSpec((tr, D_MODEL), lambda i, b=b: (i, b)) for b in range(3)]
    return pl.pallas_call(
        body, name=name, grid=(t // tr,),
        out_shape=jax.ShapeDtypeStruct((t, D_MODEL), BF16),
        in_specs=gates + [row] * 3, out_specs=row,
        compiler_params=_params(("parallel",), 7 * _nbytes((tr, D_MODEL), F32)),
    )(proj, proj, proj, *pb)


def _gate_bwd(proj, pb, dmerged, name):
    t = proj.shape[0]
    tr = _pick(t, ROW_TILE // 2, BF16_SUBLANE)

    def body(g0, g1, g2, p0, p1, p2, dm_ref, dp0, dp1, dp2, dg_ref):
        dm = dm_ref[...]
        for b, (g_ref, p_ref, dp_ref) in enumerate(((g0, p0, dp0), (g1, p1, dp1), (g2, p2, dp2))):
            sg = jax.nn.sigmoid(g_ref[...])
            dp_ref[...] = (dm * sg).astype(BF16)
            dg_ref[:, b * D_MODEL:(b + 1) * D_MODEL] = (dm * p_ref[...] * sg * (1.0 - sg)).astype(BF16)

    row = pl.BlockSpec((tr, D_MODEL), lambda i: (i, 0))
    gates = [pl.BlockSpec((tr, D_MODEL), lambda i, b=b: (i, b)) for b in range(3)]
    return pl.pallas_call(
        body, name=name, grid=(t // tr,),
        out_shape=(jax.ShapeDtypeStruct((t, D_MODEL), BF16),) * 3 + (jax.ShapeDtypeStruct((t, 3 * D_MODEL), BF16),),
        in_specs=gates + [row] * 4,
        out_specs=(row, row, row, pl.BlockSpec((tr, 3 * D_MODEL), lambda i: (i, 0))),
        compiler_params=_params(("parallel",), 11 * _nbytes((tr, D_MODEL), F32)),
    )(proj, proj, proj, *pb, dmerged)


def _swiglu_fwd(ab, name):
    t = ab.shape[0]
    tr = _pick(t, ROW_TILE, BF16_SUBLANE)
    tc = D_FF // 2

    def body(a_ref, b_ref, o_ref):
        a = a_ref[...]
        o_ref[...] = (a * jax.nn.sigmoid(a) * b_ref[...]).astype(BF16)

    return pl.pallas_call(
        body, name=name, grid=(t // tr, 2),
        out_shape=jax.ShapeDtypeStruct((t, D_FF), BF16),
        in_specs=[pl.BlockSpec((tr, tc), lambda i, j: (i, j)), pl.BlockSpec((tr, tc), lambda i, j: (i, j + 2))],
        out_specs=pl.BlockSpec((tr, tc), lambda i, j: (i, j)),
        compiler_params=_params(("parallel", "parallel"), 3 * _nbytes((tr, tc), F32)),
    )(ab, ab)


def _swiglu_bwd(ab, dh, name):
    t = ab.shape[0]
    tr = _pick(t, ROW_TILE, BF16_SUBLANE)
    tc = D_FF // 2

    def body(a_ref, b_ref, dh_ref, o_ref):
        jj = pl.program_id(1)
        a, b, d = a_ref[...], b_ref[...], dh_ref[...]
        sg = jax.nn.sigmoid(a)
        da = d * b * (sg * (1.0 + a * (1.0 - sg)))
        db = d * (a * sg)
        o_ref[...] = jnp.where(jj < 2, da, db).astype(BF16)

    return pl.pallas_call(
        body, name=name, grid=(t // tr, 4),
        out_shape=jax.ShapeDtypeStruct((t, 2 * D_FF), BF16),
        in_specs=[pl.BlockSpec((tr, tc), lambda i, j: (i, j % 2)),
                  pl.BlockSpec((tr, tc), lambda i, j: (i, j % 2 + 2)),
                  pl.BlockSpec((tr, tc), lambda i, j: (i, j % 2))],
        out_specs=pl.BlockSpec((tr, tc), lambda i, j: (i, j)),
        compiler_params=_params(("parallel", "parallel"), 4 * _nbytes((tr, tc), F32)),
    )(ab, ab, dh)


def _xattn_fwd(q, kv, name):
    t = q.shape[0]
    tq = _pick(t, ROW_TILE, BF16_SUBLANE)
    mlen = kv.shape[0]
    scale = X_HEAD ** -0.5

    def body(q_ref, kv_ref, o_ref):
        for h in range(X_HEADS):
            sl = slice(h * X_HEAD, (h + 1) * X_HEAD)
            kh = kv_ref[:, sl]
            vh = kv_ref[:, D_MODEL + h * X_HEAD:D_MODEL + (h + 1) * X_HEAD]
            s = _dot(q_ref[:, sl], kh, NT) * scale
            p = jnp.exp(s - jnp.max(s, axis=1, keepdims=True))
            l = jnp.sum(p, axis=1, keepdims=True)
            o_ref[:, sl] = (_dot(p.astype(BF16), vh, NN) / l).astype(BF16)

    return pl.pallas_call(
        body, name=name, grid=(t // tq,),
        out_shape=jax.ShapeDtypeStruct((t, D_MODEL), BF16),
        in_specs=[pl.BlockSpec((tq, D_MODEL), lambda i: (i, 0)), pl.BlockSpec((mlen, 2 * D_MODEL), lambda i: (0, 0))],
        out_specs=pl.BlockSpec((tq, D_MODEL), lambda i: (i, 0)),
        compiler_params=_params(("parallel",), 4 * _nbytes((tq, D_MODEL), F32)),
    )(q, kv)


def _xattn_bwd(q, kv, do, name):
    t = q.shape[0]
    tq = _pick(t, ROW_TILE, BF16_SUBLANE)
    mlen = kv.shape[0]
    scale = X_HEAD ** -0.5

    def body(q_ref, kv_ref, do_ref, dq_ref, dkv_ref):
        i = pl.program_id(0)

        @pl.when(i == 0)
        def _():
            dkv_ref[...] = jnp.zeros_like(dkv_ref)

        for h in range(X_HEADS):
            sl = slice(h * X_HEAD, (h + 1) * X_HEAD)
            vsl = slice(D_MODEL + h * X_HEAD, D_MODEL + (h + 1) * X_HEAD)
            qh, kh, vh, doh = q_ref[:, sl], kv_ref[:, sl], kv_ref[:, vsl], do_ref[:, sl]
            s = _dot(qh, kh, NT) * scale
            p = jnp.exp(s - jnp.max(s, axis=1, keepdims=True))
            p = p / jnp.sum(p, axis=1, keepdims=True)
            dp = _dot(doh, vh, NT)
            ds = p * (dp - jnp.sum(p * dp, axis=1, keepdims=True))
            dsb = ds.astype(BF16)
            dq_ref[:, sl] = (_dot(dsb, kh, NN) * scale).astype(BF16)
            dkv_ref[:, sl] += _dot(dsb, qh, TN) * scale
            dkv_ref[:, vsl] += _dot(p.astype(BF16), doh, TN)

    row = pl.BlockSpec((tq, D_MODEL), lambda i: (i, 0))
    whole = pl.BlockSpec((mlen, 2 * D_MODEL), lambda i: (0, 0))
    return pl.pallas_call(
        body, name=name, grid=(t // tq,),
        out_shape=(jax.ShapeDtypeStruct((t, D_MODEL), BF16), jax.ShapeDtypeStruct((mlen, 2 * D_MODEL), F32)),
        in_specs=[row, whole, row], out_specs=(row, whole),
        compiler_params=_params(("arbitrary",), 6 * _nbytes((tq, D_MODEL), F32)),
    )(q, kv, do)


def _position():
    return lax.axis_index("x"), lax.axis_index("y"), lax.axis_index("c")


N_PEER = N_DEV - 1


def _all_gather(xs, name):
    n = len(xs)

    def body(*refs):
        x_refs, out_refs = refs[:n], refs[n:2 * n]
        send_sems, recv_sems, local_sems = refs[2 * n:]
        mx, my, mc = _position()
        me, sib = (mx, my, mc), (mx, my, 1 - mc)
        chips = [(1 - mx, my), (mx, 1 - my), (1 - mx, 1 - my)]

        def slot(i, p):
            return out_refs[i].at[4 * p[0] + 2 * p[1] + p[2]]

        def copy(i, k, block, to, src=None):
            return pltpu.make_async_remote_copy(
                src_ref=slot(i, block) if src is None else src, dst_ref=slot(i, block),
                send_sem=send_sems.at[i * N_PEER + k], recv_sem=recv_sems.at[i * N_PEER + k],
                device_id=to, device_id_type=MESH)

        mine = [pltpu.make_async_copy(x_refs[i], slot(i, me), local_sems.at[i]) for i in range(n)]
        for cp in mine:
            cp.start()
        first = [copy(i, 1 + j, me, (*chip, mc), src=x_refs[i]) for j, chip in enumerate(chips) for i in range(n)]
        first += [copy(i, 0, me, sib, src=x_refs[i]) for i in range(n)]
        for cp in first:
            cp.start()
        passed = []
        for j, chip in enumerate(chips):
            for i in range(n):
                copy(i, 1 + j, (*chip, mc), me).wait_recv()
                passed.append(copy(i, 4 + j, (*chip, mc), sib))
                passed[-1].start()
        for i in range(n):
            copy(i, 0, sib, me).wait_recv()
        for j, chip in enumerate(chips):
            for i in range(n):
                copy(i, 4 + j, (*chip, 1 - mc), me).wait_recv()
        for cp in first + passed:
            cp.wait_send()
        for cp in mine:
            cp.wait()

    return pl.pallas_call(
        body, name=name,
        out_shape=tuple(jax.ShapeDtypeStruct((N_DEV,) + x.shape, x.dtype) for x in xs),
        in_specs=[pl.BlockSpec(memory_space=pl.ANY)] * n, out_specs=(pl.BlockSpec(memory_space=pl.ANY),) * n,
        scratch_shapes=[pltpu.SemaphoreType.DMA((n * N_PEER,)), pltpu.SemaphoreType.DMA((n * N_PEER,)),
                        pltpu.SemaphoreType.DMA((n,))],
    )(*xs)


PEER_RELS = [(dx, dy, dc) for dx in (0, 1) for dy in (0, 1) for dc in (0, 1)][1:]


def _peer_copy(rel_k, i, src_refs, land_refs, send_sems, recv_sems, layer, gather, arriving):
    mx, my, mc = _position()
    me_idx = 4 * mx + 2 * my + mc
    p = tuple((1 - v) if f else v for f, v in zip(PEER_RELS[rel_k], (mx, my, mc)))
    p_idx = 4 * p[0] + 2 * p[1] + p[2]
    src_slot, dst_slot = (me_idx, p_idx) if arriving else (p_idx, me_idx)
    src = src_refs[i] if gather else src_refs[i].at[src_slot]
    dst = land_refs[i].at[dst_slot] if layer is None else land_refs[i].at[dst_slot, layer]
    return pltpu.make_async_remote_copy(
        src_ref=src, dst_ref=dst, send_sem=send_sems.at[i * N_PEER + rel_k], recv_sem=recv_sems.at[i * N_PEER + rel_k],
        device_id=p, device_id_type=MESH)


def _own_copy(i, src_refs, land_refs, sem, layer, gather):
    mx, my, mc = _position()
    me_idx = 4 * mx + 2 * my + mc
    src = src_refs[i] if gather else src_refs[i].at[me_idx]
    dst = land_refs[i].at[me_idx] if layer is None else land_refs[i].at[me_idx, layer]
    return pltpu.make_async_copy(src, dst, sem)


def _exchange(parts, lands, layer, name):
    n = len(parts)

    def body(*refs):
        g_refs, land_refs = refs[:n], refs[2 * n:3 * n]
        send_sems, recv_sems, local_sems = refs[3 * n:]
        args = (g_refs, land_refs, send_sems, recv_sems, layer, False)
        mine = [_own_copy(i, g_refs, land_refs, local_sems.at[i], layer, False) for i in range(n)]
        for cp in mine:
            cp.start()
        sends = [_peer_copy(k, i, *args, False) for i in range(n) for k in range(N_PEER)]
        for cp in sends:
            cp.start()
        for i in range(n):
            for k in range(N_PEER):
                _peer_copy(k, i, *args, True).wait_recv()
        for cp in sends:
            cp.wait_send()
        for cp in mine:
            cp.wait()

    return pl.pallas_call(
        body, name=name,
        out_shape=tuple(jax.ShapeDtypeStruct(l.shape, l.dtype) for l in lands),
        in_specs=[pl.BlockSpec(memory_space=pl.ANY)] * (2 * n), out_specs=(pl.BlockSpec(memory_space=pl.ANY),) * n,
        input_output_aliases={n + i: i for i in range(n)},
        scratch_shapes=[pltpu.SemaphoreType.DMA((n * N_PEER,)), pltpu.SemaphoreType.DMA((n * N_PEER,)),
                        pltpu.SemaphoreType.DMA((n,))],
    )(*parts, *lands)


HBM_SPEC = pl.BlockSpec(memory_space=pltpu.HBM)
SEM_SPEC = pl.BlockSpec(memory_space=pltpu.SEMAPHORE)
SIDE_EFFECT = pltpu.SideEffectType.DATAFLOW_SIDE_EFFECTING


def _own_slots(srcs, lands, layer, gather):
    mx, my, mc = _position()
    me_idx = 4 * mx + 2 * my + mc
    out = []
    for s, land in zip(srcs, lands):
        piece = s[None] if gather else lax.dynamic_index_in_dim(s, me_idx, 0, keepdims=True)
        if layer is None:
            start = (me_idx,) + (0,) * (land.ndim - 1)
        else:
            piece, start = piece[:, None], (me_idx, layer) + (0,) * (land.ndim - 2)
        out.append(lax.dynamic_update_slice(land, piece, start))
    return out


def _swap_start(srcs, lands, layer, gather, name):
    n = len(srcs)

    def body(*refs):
        src_refs, land_refs = refs[:n], refs[n:2 * n]
        send_sems, recv_sems = refs[2 * n], refs[2 * n + 1]
        token = refs[4 * n + 2]
        for i in range(n):
            for k in range(N_PEER):
                _peer_copy(k, i, src_refs, land_refs, send_sems, recv_sems, layer, gather, False).start()
        token[...] = jnp.zeros_like(token)

    hbm = [pltpu.with_memory_space_constraint(a, pltpu.HBM) for a in list(srcs) + list(lands)]
    out = pl.pallas_call(
        body, name=name,
        out_shape=(pltpu.SemaphoreType.DMA((n * N_PEER,)), pltpu.SemaphoreType.DMA((n * N_PEER,)))
        + tuple(pltpu.HBM(a.shape, a.dtype) for a in hbm) + (jax.ShapeDtypeStruct((8, LANE), F32),),
        in_specs=[HBM_SPEC] * (2 * n),
        out_specs=(SEM_SPEC, SEM_SPEC) + (HBM_SPEC,) * (2 * n) + (pl.BlockSpec(memory_space=pltpu.VMEM),),
        input_output_aliases={i: 2 + i for i in range(2 * n)},
        compiler_params=pltpu.CompilerParams(has_side_effects=SIDE_EFFECT),
    )(*hbm)
    return out[0], out[1], list(out[2:2 + n]), list(out[2 + n:2 + 2 * n]), out[2 + 2 * n]


def _swap_wait(send_sems, recv_sems, srcs, lands, after, layer, gather, name):
    n = len(srcs)

    def body(*refs):
        src_refs, land_refs = refs[:n], refs[n:2 * n]
        send_sems_ref, recv_sems_ref = refs[2 * n], refs[2 * n + 1]
        for i in range(n):
            for k in range(N_PEER):
                args = (src_refs, land_refs, send_sems_ref, recv_sems_ref, layer, gather)
                _peer_copy(k, i, *args, False).wait_send()
                _peer_copy(k, i, *args, True).wait_recv()

    out = pl.pallas_call(
        body, name=name,
        out_shape=tuple(pltpu.HBM(a.shape, a.dtype) for a in list(srcs) + list(lands)),
        in_specs=[HBM_SPEC] * (2 * n) + [SEM_SPEC, SEM_SPEC, pl.BlockSpec(memory_space=pl.ANY)],
        out_specs=(HBM_SPEC,) * (2 * n),
        input_output_aliases={i: i for i in range(2 * n)},
        compiler_params=pltpu.CompilerParams(has_side_effects=SIDE_EFFECT),
    )(*srcs, *lands, send_sems, recv_sems, after)
    return list(out[n:])


ADAMW_BLOCK_BYTES = 1 << 20


def _adamw(parts, w, m, v, name):
    r, l = w.shape
    tr = _pick(r, max(ADAMW_BLOCK_BYTES // (4 * l), BF16_SUBLANE), BF16_SUBLANE)
    c1 = 1.0 - ADAM_B1 ** ADAM_STEP
    c2 = 1.0 - ADAM_B2 ** ADAM_STEP

    def body(p_ref, w_ref, m_ref, v_ref, g_ref, d_ref, nm_ref, nv_ref):
        g = p_ref[0].astype(F32)
        for s in range(1, N_DEV):
            g = g + p_ref[s].astype(F32)
        nm = ADAM_B1 * m_ref[...] + (1.0 - ADAM_B1) * g
        nv = ADAM_B2 * v_ref[...] + (1.0 - ADAM_B2) * (g * g)
        m_hat = nm / c1
        v_hat = nv / c2
        g_ref[...] = g
        d_ref[...] = -ADAM_LR * (m_hat / (jnp.sqrt(v_hat) + ADAM_EPS) + ADAM_WD * w_ref[...])
        nm_ref[...] = nm
        nv_ref[...] = nv

    row = pl.BlockSpec((tr, l), lambda i: (i, 0))
    return pl.pallas_call(
        body, name=name, grid=(r // tr,),
        out_shape=(jax.ShapeDtypeStruct((r, l), F32),) * 4,
        in_specs=[pl.BlockSpec((N_DEV, tr, l), lambda i: (0, i, 0)), row, row, row],
        out_specs=(row,) * 4,
        compiler_params=_params(("parallel",), 12 * _nbytes((tr, l), F32)),
    )(parts, w, m, v)


MATRIX_WEIGHTS = (("w_in", 2), ("conv_w", 2), ("w_branch", 3), ("w_mix_out", 1), ("w_xq", 1), ("w_xkv", 2),
                  ("w_xo", 1), ("w_ffn_gate", 2), ("w_ffn_up", 2), ("w_ffn_down", 1))
SMALL_PARAMS = ("mix_norm_g", "xattn_norm_g", "mem_norm_g", "ffn_norm_g", "final_norm_g", "forget_bias", "sink",
                "rel_bias")


def _pack_small(pieces):
    flat = jnp.concatenate([p.astype(F32).reshape(-1) for p in pieces])
    total = -(-flat.shape[0] // (8 * LANE)) * (8 * LANE)
    return jnp.pad(flat, (0, total - flat.shape[0])).reshape(total // LANE, LANE)


def _rows(a):
    return a.reshape(-1, a.shape[-1])


def _to_full(gathered, axis):
    moved = jnp.moveaxis(gathered, 0, axis)
    shape = list(moved.shape)
    shape[axis:axis + 2] = [shape[axis] * shape[axis + 1]]
    return moved.reshape(shape)


def _to_blocks(full, axis):
    shape = list(full.shape)
    shape[axis:axis + 1] = [N_DEV, shape[axis] // N_DEV]
    return jnp.moveaxis(full.reshape(shape), axis, 0)


def _perm_in(w_in):
    pad = jnp.zeros((w_in.shape[0], PROJ_COLS - IN_COLS), w_in.dtype)
    return jnp.concatenate([w_in[:, 3848:6920], w_in[:, 0:3072], w_in[:, 3080:3848], w_in[:, 3072:3080], pad], axis=1)


def _unperm_in(dw):
    return jnp.concatenate([dw[:, 3072:6144], dw[:, 6912:6920], dw[:, 6144:6912], dw[:, 0:3072]], axis=1)


def _layer_fwd(l, x, mem, wt, sm):
    t = x.shape[0]
    tag = f"l{l}_"
    h = _rms_fwd(x, sm["mix_norm_g"][l], tag + "mix_norm")
    proj = _matmul(h, wt["w_in"][l], "nn", F32, tag + "in_proj")
    y_conv = _conv_fwd(proj, wt["conv_w"][l], tag + "conv")
    fbias_row = jnp.pad(sm["forget_bias"][l], (0, LANE - 8)).reshape(1, LANE)
    c = _logf_cumsum(proj, fbias_row, tag + "logf_cumsum")
    c8 = c[:, :8].T
    c_col = jnp.broadcast_to(c8[:, :, None], (8, t, LANE))
    c_row = c8.reshape(8, 1, t)
    y_fox, lse_fox = _fox_fwd(proj, c_col, c_row, tag + "fox")
    onehot, bias = _swa_tables(sm["rel_bias"])
    sink_rep = jnp.broadcast_to(sm["sink"][l].reshape(2, SWA_GROUP, 1), (2, SWA_GROUP, LANE))
    y_swa, lse_swa = _swa_fwd(proj, bias, sink_rep, tag + "swa")
    ys = (y_conv, y_fox, y_swa)
    pb = tuple(_matmul(ys[b], wt["w_branch"][l][b], "nn", F32, tag + f"branch{b}") for b in range(3))
    merged = _gate_fwd(proj, pb, tag + "gate")
    x1 = _matmul(merged, wt["w_mix_out"][l], "nn", F32, tag + "mix_out", residual=x)
    xn2 = _rms_fwd(x1, sm["xattn_norm_g"][l], tag + "xattn_norm")
    q = _matmul(xn2, wt["w_xq"][l], "nn", BF16, tag + "xq")
    mem_n = _rms_fwd(mem, sm["mem_norm_g"][l], tag + "mem_norm")
    kv = _matmul(mem_n, wt["w_xkv"][l], "nn", BF16, tag + "xkv")
    o = _xattn_fwd(q, kv, tag + "xattn")
    x2 = _matmul(o, wt["w_xo"][l], "nn", F32, tag + "xo", residual=x1)
    xn3 = _rms_fwd(x2, sm["ffn_norm_g"][l], tag + "ffn_norm")
    ab = _matmul(xn3, wt["w_gu"][l], "nn", F32, tag + "ffn_gu")
    h1 = _swiglu_fwd(ab, tag + "swiglu")
    x3 = _matmul(h1, wt["w_ffn_down"][l], "nn", F32, tag + "ffn_down", residual=x2)
    saved = dict(x=x, h=h, proj=proj, fbias_row=fbias_row, c_col=c_col, c_row=c_row, ys=ys, lse_fox=lse_fox,
                 onehot=onehot, bias=bias, sink_rep=sink_rep, lse_swa=lse_swa, pb=pb, merged=merged, x1=x1,
                 xn2=xn2, q=q, mem_n=mem_n, kv=kv, o=o, x2=x2, xn3=xn3, ab=ab, h1=h1)
    return x3, saved


def _layer_bwd(l, dx3, dx3_b, mem, wt, sm, sv, mid_hook=None):
    t = dx3.shape[0]
    nb = t // SWA_BLOCK
    tag = f"l{l}_b_"
    gw, gs = {}, {}
    dh1 = _matmul(dx3_b, wt["w_ffn_down"][l], "nt", F32, tag + "d_h1")
    gw["w_ffn_down"] = _matmul(sv["h1"], dx3_b, "tn", F32, tag + "dw_down")
    dab = _swiglu_bwd(sv["ab"], dh1, tag + "swiglu")
    dxn3 = _matmul(dab, wt["w_gu"][l], "nt", F32, tag + "d_xn3")
    dw_gu = _matmul(sv["xn3"], dab, "tn", F32, tag + "dw_gu")
    gw["w_ffn_gate"], gw["w_ffn_up"] = dw_gu[:, :D_FF], dw_gu[:, D_FF:]
    dx2, dx2_b, gs["ffn_norm_g"] = _rms_bwd(sv["x2"], sm["ffn_norm_g"][l], dxn3, dx3, tag + "ffn_norm")
    do = _matmul(dx2_b, wt["w_xo"][l], "nt", BF16, tag + "d_o")
    gw["w_xo"] = _matmul(sv["o"], dx2_b, "tn", F32, tag + "dw_xo")
    dq, dkv = _xattn_bwd(sv["q"], sv["kv"], do, tag + "xattn")
    gw["w_xkv"] = _matmul(sv["mem_n"], dkv, "tn", F32, tag + "dw_xkv")
    dmem_n = _matmul(dkv, wt["w_xkv"][l], "nt", F32, tag + "d_memn")
    _, _, gs["mem_norm_g"] = _rms_bwd(mem, sm["mem_norm_g"][l], dmem_n, None, tag + "mem_norm")
    gw["w_xq"] = _matmul(sv["xn2"], dq, "tn", F32, tag + "dw_xq")
    dxn2 = _matmul(dq, wt["w_xq"][l], "nt", F32, tag + "d_xn2")
    dx1, dx1_b, gs["xattn_norm_g"] = _rms_bwd(sv["x1"], sm["xattn_norm_g"][l], dxn2, dx2, tag + "xattn_norm")
    dmerged = _matmul(dx1_b, wt["w_mix_out"][l], "nt", F32, tag + "d_merged")
    gw["w_mix_out"] = _matmul(sv["merged"], dx1_b, "tn", F32, tag + "dw_mix_out")
    dp0, dp1, dp2, dgate = _gate_bwd(sv["proj"], sv["pb"], dmerged, tag + "gate")
    dps = (dp0, dp1, dp2)
    dy_dtypes = (F32, BF16, BF16)
    dys = [_matmul(dps[b], wt["w_branch"][l][b], "nt", dy_dtypes[b], tag + f"d_y{b}") for b in range(3)]
    gw["w_branch"] = jnp.stack(
        [_matmul(sv["ys"][b], dps[b], "tn", F32, tag + f"dw_branch{b}") for b in range(3)])
    sink_rep = sv["sink_rep"] if mid_hook is None else sv["sink_rep"] + mid_hook(gw)
    dsq, dkp, dvp, dbias, dsink = _swa_bwd(sv["proj"], sv["bias"], sink_rep, sv["lse_swa"], sv["ys"][2],
                                           dys[2], tag + "swa")

    def band_add(part):
        tot = part[0] + part[1]
        cur = tot[:, SWA_BLOCK:, :]
        nxt = jnp.concatenate([tot[1:, :SWA_BLOCK, :], jnp.zeros((1, SWA_BLOCK, LANE), F32)], axis=0)
        return (cur + nxt).reshape(t, LANE).astype(BF16)

    dsk, dsv = band_add(dkp), band_add(dvp)
    gs["rel_bias_l"] = jnp.einsum("hts,tsb->bh", dbias.reshape(8, SWA_BLOCK, 2 * SWA_BLOCK), sv["onehot"],
                                  precision=lax.Precision.HIGHEST)
    gs["sink"] = dsink[:, :, 0].reshape(8)
    dfq, dfk, dfv, dck, dcq_row = _fox_bwd(sv["proj"], sv["c_col"], sv["c_row"], sv["lse_fox"], sv["ys"][1], dys[1],
                                           tag + "fox_bwd")
    dcq = jnp.pad(dcq_row.reshape(8, t).T, ((0, 0), (0, LANE - 8))).reshape(1, t, LANE)
    dfg, dfb = _logf_cumsum_bwd(sv["proj"], sv["fbias_row"], [dck, dcq], tag + "logf_cumsum")
    gs["forget_bias"] = dfb[0, :8]
    dcb, dcc, dcu, dconv = _conv_bwd(sv["proj"], wt["conv_w"][l], dys[0], tag + "conv")
    gw["conv_w"] = dconv[:3]
    dproj = jnp.concatenate([dgate, dcb, dcc, dcu, dfq, dfk, dfv, dsq, dsk, dsv, dfg], axis=1)
    dh = _matmul(dproj, wt["w_in"][l], "nt", F32, tag + "d_h")
    gw["w_in"] = _unperm_in(_matmul(sv["h"], dproj, "tn", F32, tag + "dw_in"))
    dx, dx_b, gs["mix_norm_g"] = _rms_bwd(sv["x"], sm["mix_norm_g"][l], dh, dx1, tag + "mix_norm")
    return dx, dx_b, gw, gs


def kernel(x, mem, mix_norm_g, w_in, forget_bias, conv_w, sink, w_branch, w_mix_out, rel_bias, xattn_norm_g, mem_norm_g, w_xq, w_xkv, w_xo, ffn_norm_g, w_ffn_gate, w_ffn_up, w_ffn_down, final_norm_g, loss_target, m_mix_norm_g, m_w_in, m_forget_bias, m_conv_w, m_sink, m_w_branch, m_w_mix_out, m_rel_bias, m_xattn_norm_g, m_mem_norm_g, m_w_xq, m_w_xkv, m_w_xo, m_ffn_norm_g, m_w_ffn_gate, m_w_ffn_up, m_w_ffn_down, m_final_norm_g, v_mix_norm_g, v_w_in, v_forget_bias, v_conv_w, v_sink, v_w_branch, v_w_mix_out, v_rel_bias, v_xattn_norm_g, v_mem_norm_g, v_w_xq, v_w_xkv, v_w_xo, v_ffn_norm_g, v_w_ffn_gate, v_w_ffn_up, v_w_ffn_down, v_final_norm_g):
    args = dict(locals())
    names = [n for n, _ in MATRIX_WEIGHTS] + list(SMALL_PARAMS)
    w = {n: args[n] for n in names}
    mo = {n: args["m_" + n] for n in names}
    vo = {n: args["v_" + n] for n in names}
    x2d, mem2d, tgt = x[0], mem[0], loss_target[0]

    wire = {n: (F32 if n == "conv_w" else BF16) for n, _ in MATRIX_WEIGHTS}
    wt = {n: [None] * DEPTH for n, _ in MATRIX_WEIGHTS}
    wt["w_gu"] = [None] * DEPTH

    def place_weights(l, gathered):
        for (n, ax), g in zip(MATRIX_WEIGHTS, gathered):
            wt[n][l] = _to_full(g, ax - 1)
        wt["w_in"][l] = _perm_in(wt["w_in"][l])
        wt["w_gu"][l] = jnp.concatenate([wt["w_ffn_gate"][l], wt["w_ffn_up"][l]], axis=1)

    shards = [[w[n][l].astype(wire[n]) for n, _ in MATRIX_WEIGHTS] for l in range(DEPTH)]
    place_weights(0, _all_gather(shards[0], "weights_gather_l0"))
    lands = _own_slots(shards[1], [lax.empty((N_DEV,) + s.shape, s.dtype) for s in shards[1]], None, True)
    w_send, w_recv, w_srcs, lands, token = _swap_start(shards[1], lands, None, True, "weights_gather_l1_start")
    sm = {n: w[n] for n in SMALL_PARAMS}
    sm["mix_norm_g"] = w["mix_norm_g"].at[0].add(token[0, 0])

    saved = []
    xc = x2d
    for l in range(DEPTH):
        if l == 1:
            place_weights(1, _swap_wait(w_send, w_recv, w_srcs, lands, xc, None, True, "weights_gather_l1_wait"))
        xc, sv = _layer_fwd(l, xc, mem2d, wt, sm)
        saved.append(sv)
    loss_row, dx, dx_b, dg_final = _loss_head(xc, sm["final_norm_g"], tgt, "loss_head")
    loss = lax.psum(loss_row[0, 0], ("x", "y", "c"))

    late = ("w_in", "conv_w")
    early_w = [(n, ax) for n, ax in MATRIX_WEIGHTS if n not in late]
    late_w = [(n, ax) for n, ax in MATRIX_WEIGHTS if n in late]

    def grad_parts(gw, which):
        return [_to_blocks(gw[n], ax - 1).astype(wire[n]) for n, ax in which]

    gw_all, gs_all = [None] * DEPTH, [None] * DEPTH
    dx, dx_b, gw_all[1], gs_all[1] = _layer_bwd(1, dx, dx_b, mem2d, wt, sm, saved[1])
    parts1 = grad_parts(gw_all[1], MATRIX_WEIGHTS)
    zones = _own_slots(parts1, [lax.empty((N_DEV, DEPTH) + p.shape[1:], p.dtype) for p in parts1], 1, False)
    g_send, g_recv, g_srcs, zones, token = _swap_start(parts1, zones, 1, False, "grads_exchange_l1_start")
    sm_b = dict(sm)
    sm_b["ffn_norm_g"] = sm["ffn_norm_g"].at[0].add(token[0, 0])
    mid = {}

    def mid_hook(gw):
        zone = dict(zip([n for n, _ in MATRIX_WEIGHTS],
                        _swap_wait(g_send, g_recv, g_srcs, zones, gw["w_mix_out"], 1, False, "grads_exchange_l1_wait")))
        parts0 = grad_parts(gw, early_w)
        early_zones = _own_slots(parts0, [zone[n] for n, _ in early_w], 0, False)
        mid["early"] = _swap_start(parts0, early_zones, 0, False, "grads_exchange_l0_early_start")
        mid["late_zones"] = [zone[n] for n, _ in late_w]
        return mid["early"][4][0, 0]

    dx, dx_b, gw_all[0], gs_all[0] = _layer_bwd(0, dx, dx_b, mem2d, wt, sm_b, saved[0], mid_hook)
    grad_x = dx[None]
    e_send, e_recv, e_srcs, e_zones, _ = mid["early"]
    recv_by_name = dict(zip([n for n, _ in early_w],
                            _swap_wait(e_send, e_recv, e_srcs, e_zones, dx, 0, False, "grads_exchange_l0_early_wait")))
    recv_by_name.update(zip([n for n, _ in late_w],
                            _exchange(grad_parts(gw_all[0], late_w), mid["late_zones"], 0, "grads_exchange_l0_late")))
    recv = [recv_by_name[n] for n, _ in MATRIX_WEIGHTS]

    outs = {}
    for (n, _), r in zip(MATRIX_WEIGHTS, recv):
        res = _adamw(r.reshape((N_DEV,) + _rows(w[n]).shape), _rows(w[n]), _rows(mo[n]), _rows(vo[n]), "adamw_" + n)
        outs[n] = [o.reshape(w[n].shape) for o in res]

    gsm = {n: jnp.stack([gs_all[l][n] for l in range(DEPTH)])
           for n in ("mix_norm_g", "xattn_norm_g", "mem_norm_g", "ffn_norm_g", "forget_bias", "sink")}
    gsm["final_norm_g"] = dg_final
    gsm["rel_bias"] = gs_all[0]["rel_bias_l"] + gs_all[1]["rel_bias_l"]
    (small_parts,) = _all_gather([_pack_small([gsm[n] for n in SMALL_PARAMS])], "small_grads_all_gather")
    outs_small = _adamw(small_parts, *[_pack_small([d[n] for n in SMALL_PARAMS]) for d in (w, mo, vo)], "adamw_small")
    for kind in range(4):
        flat, o = outs_small[kind].reshape(-1), 0
        for n in SMALL_PARAMS:
            sz = int(np.prod(w[n].shape))
            outs.setdefault(n, []).append(flat[o:o + sz].reshape(w[n].shape))
            o += sz

    order = ["mix_norm_g", "w_in", "forget_bias", "conv_w", "sink", "w_branch", "w_mix_out", "rel_bias",
             "xattn_norm_g", "mem_norm_g", "w_xq", "w_xkv", "w_xo", "ffn_norm_g", "w_ffn_gate", "w_ffn_up",
             "w_ffn_down", "final_norm_g"]
    result = [loss, grad_x]
    for kind in range(4):
        result += [outs[n][kind] for n in order]
    return tuple(result)
```

```python
import math

import numpy as np
import jax
import jax.numpy as jnp
from jax import lax
from jax.experimental import pallas as pl
from jax.experimental.pallas import tpu as pltpu

F32 = jnp.float32
BF16 = jnp.bfloat16
MESH = pl.DeviceIdType.MESH

LANE = 128
BF16_SUBLANE = 16
V7X_VMEM_REQUEST_CAP = 56 * 2 ** 20
N_DEV = 8

D_MODEL = 1024
DEPTH = 2
HEAD = 64
BRANCH = 512
SWA_BLOCK = 128
SWA_GROUP = 4
N_BUCKETS = 32
X_HEADS = 4
X_HEAD = 256
D_FF = 2816
RMS_EPS = 1e-6
NEG = -1e30
ADAM_LR, ADAM_B1, ADAM_B2, ADAM_EPS, ADAM_WD, ADAM_STEP = 0.001, 0.9, 0.999, 1e-08, 0.01, 10

IN_COLS = 6920
PROJ_COLS = 7040
COL_GATE, COL_CONV, COL_FOX, COL_SQ, COL_SK, COL_SV, COL_FG = 0, 3072, 4608, 6144, 6656, 6784, 6912

ROW_TILE = 512
FOX_TILE = 512
MM_TM, MM_TN, MM_TK = 1024, 1536, 1024


def _pick(n, cap, mult):
    best = None
    for d in range(mult, min(n, cap) + 1, mult):
        if n % d == 0:
            best = d
    return n if best is None else best


def _params(semantics, block_bytes):
    limit = int(min(max(2 * block_bytes + (8 << 20), 24 << 20), V7X_VMEM_REQUEST_CAP))
    return pltpu.CompilerParams(dimension_semantics=semantics, vmem_limit_bytes=limit)


def _nbytes(shape, dtype):
    return int(np.prod(shape)) * jnp.dtype(dtype).itemsize


def _dot(a, b, dims):
    return lax.dot_general(a, b, (dims, ((), ())), preferred_element_type=F32)


NN = ((1,), (0,))
NT = ((1,), (1,))
TN = ((0,), (0,))


def _matmul(a, b, mode, out_dtype, name, residual=None):
    if mode == "nn":
        (m, k), (k2, n) = a.shape, b.shape
    elif mode == "nt":
        (m, k), (n, k2) = a.shape, b.shape
    else:
        (k, m), (k2, n) = a.shape, b.shape
    assert k == k2, (name, a.shape, b.shape)
    tm, tn, tk = _pick(m, MM_TM, LANE), _pick(n, MM_TN, LANE), _pick(k, MM_TK, LANE)
    nk = k // tk
    dims = {"nn": NN, "nt": NT, "tn": TN}[mode]
    has_res = residual is not None

    def body(*refs):
        a_ref, b_ref = refs[0], refs[1]
        r_ref = refs[2] if has_res else None
        o_ref = refs[3] if has_res else refs[2]
        kk = pl.program_id(2)
        p = _dot(a_ref[...].astype(BF16), b_ref[...].astype(BF16), dims)
        if nk == 1:
            if has_res:
                p = p + r_ref[...]
            o_ref[...] = p.astype(out_dtype)
        else:
            acc_ref = refs[-1]

            @pl.when(kk == 0)
            def _():
                acc_ref[...] = p

            @pl.when(kk > 0)
            def _():
                acc_ref[...] += p

            @pl.when(kk == nk - 1)
            def _():
                res = acc_ref[...]
                if has_res:
                    res = res + r_ref[...]
                o_ref[...] = res.astype(out_dtype)

    if mode == "nn":
        a_spec = pl.BlockSpec((tm, tk), lambda i, j, kk: (i, kk))
        b_spec = pl.BlockSpec((tk, tn), lambda i, j, kk: (kk, j))
    elif mode == "nt":
        a_spec = pl.BlockSpec((tm, tk), lambda i, j, kk: (i, kk))
        b_spec = pl.BlockSpec((tn, tk), lambda i, j, kk: (j, kk))
    else:
        a_spec = pl.BlockSpec((tk, tm), lambda i, j, kk: (kk, i))
        b_spec = pl.BlockSpec((tk, tn), lambda i, j, kk: (kk, j))
    o_spec = pl.BlockSpec((tm, tn), lambda i, j, kk: (i, j))
    in_specs, args = [a_spec, b_spec], [a, b]
    if has_res:
        in_specs.append(o_spec)
        args.append(residual)
    blk = (_nbytes((tm, tk), a.dtype) + _nbytes((tk, tn), b.dtype) + _nbytes((tm, tn), out_dtype)
           + (_nbytes((tm, tn), F32) if has_res else 0))
    scratch = [pltpu.VMEM((tm, tn), F32)] if nk > 1 else []
    return pl.pallas_call(
        body, name=name, grid=(m // tm, n // tn, nk),
        out_shape=jax.ShapeDtypeStruct((m, n), out_dtype),
        in_specs=in_specs, out_specs=o_spec, scratch_shapes=scratch,
        compiler_params=_params(("parallel", "parallel", "arbitrary"), blk + _nbytes((tm, tn), F32)),
    )(*args)


def _rms_fwd(x, g, name):
    t, d = x.shape
    tr = _pick(t, ROW_TILE, BF16_SUBLANE)

    def body(x_ref, g_ref, y_ref):
        xv = x_ref[...]
        r = lax.rsqrt(jnp.mean(xv * xv, axis=-1, keepdims=True) + RMS_EPS)
        y_ref[...] = ((xv * r) * g_ref[...]).astype(BF16)

    return pl.pallas_call(
        body, name=name, grid=(t // tr,),
        out_shape=jax.ShapeDtypeStruct((t, d), BF16),
        in_specs=[pl.BlockSpec((tr, d), lambda i: (i, 0)), pl.BlockSpec((1, d), lambda i: (0, 0))],
        out_specs=pl.BlockSpec((tr, d), lambda i: (i, 0)),
        compiler_params=_params(("parallel",), 2 * _nbytes((tr, d), F32)),
    )(x, g.reshape(1, d))


def _rms_bwd(x, g, dy, dres, name):
    t, d = x.shape
    tr = _pick(t, ROW_TILE, BF16_SUBLANE)
    has_res = dres is not None

    def body(*refs):
        x_ref, g_ref, dy_ref = refs[:3]
        r_ref = refs[3] if has_res else None
        dx_ref, dxb_ref, dg_ref = refs[-3:]
        i = pl.program_id(0)
        xv = x_ref[...]
        r = lax.rsqrt(jnp.mean(xv * xv, axis=-1, keepdims=True) + RMS_EPS)
        xh = xv * r
        dyv = dy_ref[...].astype(F32)
        dxh = dyv * g_ref[...]
        dx = r * (dxh - xh * jnp.mean(dxh * xh, axis=-1, keepdims=True))
        if has_res:
            dx = dx + r_ref[...]
        dx_ref[...] = dx
        dxb_ref[...] = dx.astype(BF16)

        @pl.when(i == 0)
        def _():
            dg_ref[...] = jnp.zeros_like(dg_ref)

        dg_ref[...] += jnp.sum(dyv * xh, axis=0, keepdims=True)

    row = pl.BlockSpec((tr, d), lambda i: (i, 0))
    vec = pl.BlockSpec((1, d), lambda i: (0, 0))
    in_specs, args = [row, vec, row], [x, g.reshape(1, d), dy]
    if has_res:
        in_specs.append(row)
        args.append(dres)
    return pl.pallas_call(
        body, name=name, grid=(t // tr,),
        out_shape=(jax.ShapeDtypeStruct((t, d), F32), jax.ShapeDtypeStruct((t, d), BF16),
                   jax.ShapeDtypeStruct((1, d), F32)),
        in_specs=in_specs, out_specs=(row, row, vec),
        compiler_params=_params(("arbitrary",), 5 * _nbytes((tr, d), F32)),
    )(*args)


def _loss_head(x, g, target, name):
    t, d = x.shape
    tr = _pick(t, ROW_TILE, BF16_SUBLANE)

    def body(x_ref, g_ref, t_ref, loss_ref, dx_ref, dxb_ref, dg_ref):
        i = pl.program_id(0)
        xv = x_ref[...]
        gv = g_ref[...]
        r = lax.rsqrt(jnp.mean(xv * xv, axis=-1, keepdims=True) + RMS_EPS)
        xh = xv * r
        diff = xh * gv - t_ref[...]
        part = 0.5 * jnp.sum(jnp.mean(diff * diff, axis=-1, keepdims=True), axis=0, keepdims=True)
        dyv = diff * (1.0 / d)
        dxh = dyv * gv
        dx = r * (dxh - xh * jnp.mean(dxh * xh, axis=-1, keepdims=True))
        dx_ref[...] = dx
        dxb_ref[...] = dx.astype(BF16)

        @pl.when(i == 0)
        def _():
            dg_ref[...] = jnp.zeros_like(dg_ref)
            loss_ref[...] = jnp.zeros_like(loss_ref)

        dg_ref[...] += jnp.sum(dyv * xh, axis=0, keepdims=True)
        loss_ref[...] += jnp.broadcast_to(part, loss_ref.shape)

    row = pl.BlockSpec((tr, d), lambda i: (i, 0))
    vec = pl.BlockSpec((1, d), lambda i: (0, 0))
    return pl.pallas_call(
        body, name=name, grid=(t // tr,),
        out_shape=(jax.ShapeDtypeStruct((1, LANE), F32), jax.ShapeDtypeStruct((t, d), F32),
                   jax.ShapeDtypeStruct((t, d), BF16), jax.ShapeDtypeStruct((1, d), F32)),
        in_specs=[row, vec, row],
        out_specs=(pl.BlockSpec((1, LANE), lambda i: (0, 0)), row, row, vec),
        compiler_params=_params(("arbitrary",), 5 * _nbytes((tr, d), F32)),
    )(x, g.reshape(1, d), target)


HALO = 8


def _conv_fwd(proj, conv_w, name):
    t = proj.shape[0]
    tr = _pick(t, ROW_TILE, BF16_SUBLANE)
    c0 = COL_CONV // BRANCH
    hb = tr // HALO

    def body(cb_ref, cc_ref, cu_ref, hc_ref, hu_ref, w_ref, y_ref):
        i = pl.program_id(0)
        z = cc_ref[...] * cu_ref[...]
        hz = jnp.where(i > 0, hc_ref[...] * hu_ref[...], 0.0)
        zf = jnp.concatenate([hz, z], axis=0)
        z1 = pltpu.roll(zf, 1, 0)[HALO:]
        z2 = pltpu.roll(zf, 2, 0)[HALO:]
        y = w_ref[2:3, :] * z + w_ref[1:2, :] * z1 + w_ref[0:1, :] * z2
        y_ref[...] = (cb_ref[...] * y).astype(BF16)

    def col(c):
        return pl.BlockSpec((tr, BRANCH), lambda i, c=c: (i, c0 + c))

    def prev(c):
        return pl.BlockSpec((HALO, BRANCH), lambda i, c=c: (jnp.maximum(i * hb - 1, 0), c0 + c))

    return pl.pallas_call(
        body, name=name, grid=(t // tr,),
        out_shape=jax.ShapeDtypeStruct((t, BRANCH), BF16),
        in_specs=[col(0), col(1), col(2), prev(1), prev(2), pl.BlockSpec((3, BRANCH), lambda i: (0, 0))],
        out_specs=pl.BlockSpec((tr, BRANCH), lambda i: (i, 0)),
        compiler_params=_params(("parallel",), 6 * _nbytes((tr, BRANCH), F32)),
    )(proj, proj, proj, proj, proj, conv_w)


def _conv_bwd(proj, conv_w, dout, name):
    t = proj.shape[0]
    tr = _pick(t, ROW_TILE, BF16_SUBLANE)
    nblk = t // tr
    c0 = COL_CONV // BRANCH
    hb = tr // HALO
    last_halo = t // HALO - 1

    def body(cb_ref, cc_ref, cu_ref, hc_ref, hu_ref, do_ref, ndo_ref, ncb_ref, w_ref,
             dcb_ref, dcc_ref, dcu_ref, dw_ref):
        i = pl.program_id(0)
        cb, cc, cu = cb_ref[...], cc_ref[...], cu_ref[...]
        w0, w1, w2 = w_ref[0:1, :], w_ref[1:2, :], w_ref[2:3, :]
        z = cc * cu
        hz = jnp.where(i > 0, hc_ref[...] * hu_ref[...], 0.0)
        zf = jnp.concatenate([hz, z], axis=0)
        z1 = pltpu.roll(zf, 1, 0)[HALO:]
        z2 = pltpu.roll(zf, 2, 0)[HALO:]
        y = w2 * z + w1 * z1 + w0 * z2
        dout_v = do_ref[...]
        dyc = dout_v * cb
        hdy = jnp.where(i < nblk - 1, ndo_ref[...] * ncb_ref[...], 0.0)
        dyf = jnp.concatenate([dyc, hdy], axis=0)
        dy1 = pltpu.roll(dyf, tr + HALO - 1, 0)[:tr]
        dy2 = pltpu.roll(dyf, tr + HALO - 2, 0)[:tr]
        dz = w2 * dyc + w1 * dy1 + w0 * dy2
        dcb_ref[...] = (dout_v * y).astype(BF16)
        dcc_ref[...] = (dz * cu).astype(BF16)
        dcu_ref[...] = (dz * cc).astype(BF16)

        @pl.when(i == 0)
        def _():
            dw_ref[...] = jnp.zeros_like(dw_ref)

        dw_ref[0:1, :] += jnp.sum(dyc * z2, axis=0, keepdims=True)
        dw_ref[1:2, :] += jnp.sum(dyc * z1, axis=0, keepdims=True)
        dw_ref[2:3, :] += jnp.sum(dyc * z, axis=0, keepdims=True)

    def col(c):
        return pl.BlockSpec((tr, BRANCH), lambda i, c=c: (i, c0 + c))

    def prev(c):
        return pl.BlockSpec((HALO, BRANCH), lambda i, c=c: (jnp.maximum(i * hb - 1, 0), c0 + c))

    def nxt(c):
        return pl.BlockSpec((HALO, BRANCH), lambda i, c=c: (jnp.minimum((i + 1) * hb, last_halo), c))

    row = pl.BlockSpec((tr, BRANCH), lambda i: (i, 0))
    return pl.pallas_call(
        body, name=name, grid=(nblk,),
        out_shape=(jax.ShapeDtypeStruct((t, BRANCH), BF16),) * 3 + (jax.ShapeDtypeStruct((HALO, BRANCH), F32),),
        in_specs=[col(0), col(1), col(2), prev(1), prev(2), row, nxt(0), nxt(c0),
                  pl.BlockSpec((3, BRANCH), lambda i: (0, 0))],
        out_specs=(row, row, row, pl.BlockSpec((HALO, BRANCH), lambda i: (0, 0))),
        compiler_params=_params(("arbitrary",), 8 * _nbytes((tr, BRANCH), F32)),
    )(proj, proj, proj, proj, proj, dout, dout, proj, conv_w)


def _tri(lower):
    r = lax.broadcasted_iota(jnp.int32, (LANE, LANE), 0)
    c = lax.broadcasted_iota(jnp.int32, (LANE, LANE), 1)
    return jnp.where((c <= r) if lower else (c >= r), 1.0, 0.0).astype(F32)


def _logf_cumsum(proj, fbias_row, name):
    t = proj.shape[0]
    nchunk = t // LANE

    def body(f_ref, b_ref, c_ref, run_sc):
        tri = _tri(True)
        run_sc[...] = jnp.zeros_like(run_sc)

        @pl.loop(0, nchunk)
        def _(i):
            rows = pl.ds(pl.multiple_of(i * LANE, LANE), LANE)
            z = f_ref[rows, :] + b_ref[...]
            logf = jnp.minimum(z, 0.0) - jnp.log(1.0 + jnp.exp(-jnp.abs(z)))
            cs = lax.dot_general(tri, logf, (NN, ((), ())), precision=lax.Precision.HIGHEST,
                                 preferred_element_type=F32) + run_sc[0:1, :]
            c_ref[rows, :] = cs
            run_sc[0:1, :] = cs[LANE - 1:LANE, :]

    return pl.pallas_call(
        body, name=name, grid=(1,),
        out_shape=jax.ShapeDtypeStruct((t, LANE), F32),
        in_specs=[pl.BlockSpec((t, LANE), lambda i: (0, COL_FG // LANE)), pl.BlockSpec((1, LANE), lambda i: (0, 0))],
        out_specs=pl.BlockSpec((t, LANE), lambda i: (0, 0)),
        scratch_shapes=[pltpu.VMEM((8, LANE), F32)],
        compiler_params=_params(("arbitrary",), 2 * _nbytes((t, LANE), F32)),
    )(proj, fbias_row)


def _logf_cumsum_bwd(proj, fbias_row, pieces, name):
    t = proj.shape[0]
    tb = _pick(t, 2 * ROW_TILE, LANE)
    nblk = t // tb
    npiece = len(pieces)

    def body(*refs):
        f_ref, b_ref = refs[:2]
        piece_refs = refs[2:2 + npiece]
        df_ref, db_ref, run_sc = refs[2 + npiece:]
        i = pl.program_id(0)
        tri = _tri(False)

        @pl.when(i == 0)
        def _():
            run_sc[...] = jnp.zeros_like(run_sc)
            db_ref[...] = jnp.zeros_like(db_ref)

        for c in reversed(range(tb // LANE)):
            rows = slice(c * LANE, (c + 1) * LANE)
            slabs = [p_ref[n, rows, :] for p_ref in piece_refs for n in range(p_ref.shape[0])]
            dcc = slabs[0]
            for slab in slabs[1:]:
                dcc = dcc + slab
            ss = lax.dot_general(tri, dcc, (NN, ((), ())), precision=lax.Precision.HIGHEST,
                                 preferred_element_type=F32) + run_sc[0:1, :]
            z = f_ref[rows, :] + b_ref[...]
            dz = ss * (1.0 / (1.0 + jnp.exp(z)))
            df_ref[rows, :] = dz.astype(BF16)
            run_sc[0:1, :] = ss[0:1, :]
            db_ref[...] += jnp.sum(dz, axis=0, keepdims=True)

    piece_specs = [pl.BlockSpec((p.shape[0], tb, LANE), lambda i: (0, nblk - 1 - i, 0)) for p in pieces]
    nslab = sum(p.shape[0] for p in pieces)
    return pl.pallas_call(
        body, name=name, grid=(nblk,),
        out_shape=(jax.ShapeDtypeStruct((t, LANE), BF16), jax.ShapeDtypeStruct((1, LANE), F32)),
        in_specs=[pl.BlockSpec((tb, LANE), lambda i: (nblk - 1 - i, COL_FG // LANE)),
                  pl.BlockSpec((1, LANE), lambda i: (0, 0))] + piece_specs,
        out_specs=(pl.BlockSpec((tb, LANE), lambda i: (nblk - 1 - i, 0)), pl.BlockSpec((1, LANE), lambda i: (0, 0))),
        scratch_shapes=[pltpu.VMEM((8, LANE), F32)],
        compiler_params=_params(("arbitrary",), (4 + nslab) * _nbytes((tb, LANE), F32)),
    )(proj, fbias_row, *pieces)


def _lo_mask():
    return lax.broadcasted_iota(jnp.int32, (1, LANE), 1) < HEAD


def _causal_steps(n, key_major):
    if key_major:
        pairs = [(iq, ik) for ik in range(n) for iq in range(ik, n)]
    else:
        pairs = [(iq, ik) for iq in range(n) for ik in range(iq + 1)]
    return (jnp.asarray([p[0] for p in pairs], jnp.int32), jnp.asarray([p[1] for p in pairs], jnp.int32))


def _head_lanes(j, pair_vals):
    lane = lax.broadcasted_iota(jnp.int32, (1, LANE), 1)
    return jnp.where(lane == 2 * j, pair_vals[0], 0.0) + jnp.where(lane == 2 * j + 1, pair_vals[1], 0.0)


def _fox_fwd(proj, c_col, c_row, name):
    t = proj.shape[0]
    tq = _pick(t, FOX_TILE, LANE)
    nq = t // tq
    rep = tq // LANE
    scale = HEAD ** -0.5
    cq, ck, cv = COL_FOX // LANE, COL_FOX // LANE + 4, COL_FOX // LANE + 8
    q_tab, k_tab = _causal_steps(nq, False)

    def body(qt_ref, kt_ref, q_ref, k_ref, v_ref, ck_ref, cqr_ref, y_ref, lse_ref, m_sc, l_sc, acc_sc):
        step_id = pl.program_id(1)
        iq, ik = qt_ref[step_id], kt_ref[step_id]
        lo = _lo_mask()
        lo_rows = lax.broadcasted_iota(jnp.int32, (LANE, 1), 0) < HEAD

        @pl.when(ik == 0)
        def _():
            m_sc[...] = jnp.full(m_sc.shape, NEG, F32)
            l_sc[...] = jnp.zeros_like(l_sc)
            acc_sc[...] = jnp.zeros_like(acc_sc)

        def step(diag):
            q2 = (q_ref[...] * scale).astype(BF16)
            k2 = k_ref[...].astype(BF16)
            v2 = v_ref[...].astype(BF16)
            alphas, adds = [], []
            for h in range(2):
                msk = lo if h == 0 else jnp.logical_not(lo)
                kh = jnp.where(msk, k2, jnp.zeros_like(k2))
                vh = jnp.where(msk, v2, jnp.zeros_like(v2))
                st = _dot(kh, q2, NT) + cqr_ref[h] - jnp.tile(ck_ref[h], (1, rep))
                if diag:
                    krow = lax.broadcasted_iota(jnp.int32, (tq, tq), 0)
                    qcol = lax.broadcasted_iota(jnp.int32, (tq, tq), 1)
                    st = jnp.where(krow <= qcol, st, NEG)
                m_prev = m_sc[h]
                m_new = jnp.maximum(m_prev, jnp.max(st, axis=0, keepdims=True))
                alpha = jnp.exp(m_prev - m_new)
                pt = jnp.exp(st - m_new)
                l_sc[h] = alpha * l_sc[h] + jnp.sum(pt, axis=0, keepdims=True)
                m_sc[h] = m_new
                alphas.append(alpha)
                adds.append(_dot(vh, pt.astype(BF16), TN))
            acc_sc[...] = acc_sc[...] * jnp.where(lo_rows, alphas[0], alphas[1]) + (adds[0] + adds[1])

        @pl.when(ik < iq)
        def _():
            step(False)

        @pl.when(ik == iq)
        def _():
            step(True)
            yt = acc_sc[...] / jnp.where(lo_rows, l_sc[0], l_sc[1])
            y_ref[...] = yt.T.astype(BF16)
            lse_ref[...] = m_sc[...] + jnp.log(l_sc[...])

    def kv(c):
        return pl.BlockSpec((tq, LANE), lambda j, s, qt, kt, c=c: (kt[s], c + j))

    qrow = pl.BlockSpec((2, 1, tq), lambda j, s, qt, kt: (j, 0, qt[s]))
    grid_spec = pltpu.PrefetchScalarGridSpec(
        num_scalar_prefetch=2, grid=(4, int(q_tab.shape[0])),
        in_specs=[pl.BlockSpec((tq, LANE), lambda j, s, qt, kt: (qt[s], cq + j)), kv(ck), kv(cv),
                  pl.BlockSpec((2, tq, LANE), lambda j, s, qt, kt: (j, kt[s], 0)), qrow],
        out_specs=(pl.BlockSpec((tq, LANE), lambda j, s, qt, kt: (qt[s], j)), qrow),
        scratch_shapes=[pltpu.VMEM((2, 1, tq), F32), pltpu.VMEM((2, 1, tq), F32), pltpu.VMEM((LANE, tq), F32)])
    return pl.pallas_call(
        body, name=name, grid_spec=grid_spec,
        out_shape=(jax.ShapeDtypeStruct((t, BRANCH), BF16), jax.ShapeDtypeStruct((8, 1, t), F32)),
        compiler_params=_params(("parallel", "arbitrary"),
                                16 * _nbytes((tq, LANE), F32) + 6 * _nbytes((tq, tq), F32)),
    )(q_tab, k_tab, proj, proj, proj, c_col, c_row)


def _fox_bwd(proj, c_col, c_row, lse_row, y, dy, name):
    t = proj.shape[0]
    tb = _pick(t, FOX_TILE, LANE)
    nb = t // tb
    rep = tb // LANE
    scale = HEAD ** -0.5
    cq, ck, cv = COL_FOX // LANE, COL_FOX // LANE + 4, COL_FOX // LANE + 8
    q_tab, k_tab = _causal_steps(nb, True)
    nsteps = int(q_tab.shape[0])

    def body(qt_ref, kt_ref, k_ref, v_ref, q_ref, y_ref, dy_ref, ck_ref, cqr_ref, lser_ref,
             dq_ref, dk_ref, dv_ref, dck_ref, dcq_ref, dk_sc, dv_sc, dc_sc, dqt_sc, dcq_sc, d_sc):
        j, step_id = pl.program_id(0), pl.program_id(1)
        iq, ik = qt_ref[step_id], kt_ref[step_id]
        lo = _lo_mask()

        @pl.when(step_id == 0)
        def _():
            dqt_sc[...] = jnp.zeros_like(dqt_sc)
            dcq_sc[...] = jnp.zeros_like(dcq_sc)

        @pl.when(iq == ik)
        def _():
            dk_sc[...] = jnp.zeros_like(dk_sc)
            dv_sc[...] = jnp.zeros_like(dv_sc)
            dc_sc[...] = jnp.zeros_like(dc_sc)

        @pl.when(ik == 0)
        def _():
            prod = y_ref[...].astype(F32) * dy_ref[...].astype(F32)
            row = lax.broadcasted_iota(jnp.int32, (8, LANE), 0)
            sel = jnp.logical_or(jnp.logical_and(row == 0, lo), jnp.logical_and(row == 1, jnp.logical_not(lo)))
            d_sc[iq] = lax.dot_general(jnp.where(sel, 1.0, 0.0).astype(F32), prod, (NT, ((), ())),
                                       precision=lax.Precision.HIGHEST, preferred_element_type=F32)

        def step(diag):
            k2 = k_ref[...].astype(BF16)
            v2 = v_ref[...].astype(BF16)
            q2 = (q_ref[...] * scale).astype(BF16)
            do2 = dy_ref[...]
            d_rows = d_sc[iq]
            for h in range(2):
                msk = lo if h == 0 else jnp.logical_not(lo)
                kh = jnp.where(msk, k2, jnp.zeros_like(k2))
                vh = jnp.where(msk, v2, jnp.zeros_like(v2))
                st = _dot(kh, q2, NT) + (cqr_ref[h] - lser_ref[h]) - jnp.tile(ck_ref[h], (1, rep))
                if diag:
                    krow = lax.broadcasted_iota(jnp.int32, (tb, tb), 0)
                    qcol = lax.broadcasted_iota(jnp.int32, (tb, tb), 1)
                    st = jnp.where(krow <= qcol, st, NEG)
                pt = jnp.exp(st)
                dpt = _dot(vh, do2, NT)
                dst = pt * (dpt - d_rows[h:h + 1, :])
                dsb = dst.astype(BF16)
                dv_sc[h] += _dot(pt.astype(BF16), do2, NN)
                dk_sc[h] += _dot(dsb, q2, NN)
                dc_sc[h] -= jnp.sum(dst, axis=1, keepdims=True)
                dqt_sc[iq] += _dot(kh, dsb, TN)
                dcq_sc[h, iq] += jnp.sum(dst, axis=0, keepdims=True)

        @pl.when(iq > ik)
        def _():
            step(False)

        @pl.when(iq == ik)
        def _():
            step(True)

        @pl.when(iq == nb - 1)
        def _():
            dk_ref[...] = jnp.where(lo, dk_sc[0], dk_sc[1]).astype(BF16)
            dv_ref[...] = jnp.where(lo, dv_sc[0], dv_sc[1]).astype(BF16)
            dck_ref[...] = _head_lanes(j, dc_sc)

        @pl.when(step_id == nsteps - 1)
        def _():
            for i in range(nb):
                dq_ref[i * tb:(i + 1) * tb, :] = (dqt_sc[i].T * scale).astype(BF16)
                for h in range(2):
                    dcq_ref[h, :, i * tb:(i + 1) * tb] = dcq_sc[h, i]

    def kcol(c):
        return pl.BlockSpec((tb, LANE), lambda j, s, qt, kt, c=c: (kt[s], c + j))

    qrow = pl.BlockSpec((2, 1, tb), lambda j, s, qt, kt: (j, 0, qt[s]))
    pair_q = pl.BlockSpec((tb, LANE), lambda j, s, qt, kt: (qt[s], j))
    pair_k = pl.BlockSpec((tb, LANE), lambda j, s, qt, kt: (kt[s], j))
    grid_spec = pltpu.PrefetchScalarGridSpec(
        num_scalar_prefetch=2, grid=(4, nsteps),
        in_specs=[kcol(ck), kcol(cv), pl.BlockSpec((tb, LANE), lambda j, s, qt, kt: (qt[s], cq + j)), pair_q, pair_q,
                  pl.BlockSpec((2, tb, LANE), lambda j, s, qt, kt: (j, kt[s], 0)), qrow, qrow],
        out_specs=(pl.BlockSpec((t, LANE), lambda j, s, qt, kt: (0, j)), pair_k, pair_k,
                   pl.BlockSpec((None, tb, LANE), lambda j, s, qt, kt: (j, kt[s], 0)),
                   pl.BlockSpec((2, 1, t), lambda j, s, qt, kt: (j, 0, 0))),
        scratch_shapes=[pltpu.VMEM((2, tb, LANE), F32)] * 3
        + [pltpu.VMEM((nb, LANE, tb), F32), pltpu.VMEM((2, nb, 1, tb), F32), pltpu.VMEM((nb, 8, tb), F32)])
    return pl.pallas_call(
        body, name=name, grid_spec=grid_spec,
        out_shape=(jax.ShapeDtypeStruct((t, BRANCH), BF16), jax.ShapeDtypeStruct((t, BRANCH), BF16),
                   jax.ShapeDtypeStruct((t, BRANCH), BF16), jax.ShapeDtypeStruct((4, t, LANE), F32),
                   jax.ShapeDtypeStruct((8, 1, t), F32)),
        compiler_params=_params(("parallel", "arbitrary"),
                                24 * _nbytes((tb, LANE), F32) + 8 * _nbytes((tb, tb), F32)
                                + 2 * _nbytes((t, LANE), F32)),
    )(q_tab, k_tab, proj, proj, proj, y, dy, c_col, c_row, lse_row)


def _swa_tables(rel_bias):
    tq = np.arange(SWA_BLOCK)[:, None]
    sk = np.arange(2 * SWA_BLOCK)[None, :]
    dist = SWA_BLOCK + tq - sk
    inwin = (dist >= 0) & (dist < SWA_BLOCK)
    n = np.maximum(dist, 0)
    max_exact = N_BUCKETS // 2
    large = max_exact + (np.log(np.maximum(n, 1).astype(np.float32) / max_exact)
                         / math.log(SWA_BLOCK / max_exact) * (N_BUCKETS - max_exact)).astype(np.int32)
    bucket = np.where(n < max_exact, n, np.minimum(large, N_BUCKETS - 1))
    onehot = (bucket[..., None] == np.arange(N_BUCKETS)) & inwin[..., None]
    onehot = jnp.asarray(onehot.astype(np.float32))
    bias = jnp.einsum("tsb,bh->hts", onehot, rel_bias, precision=lax.Precision.HIGHEST)
    bias = jnp.where(jnp.asarray(inwin)[None], bias, NEG)
    return onehot, bias


def _swa_fwd(proj, bias, sink_rep, name):
    t = proj.shape[0]
    nb = t // SWA_BLOCK
    scale = HEAD ** -0.5
    csq, csk, csv = COL_SQ // 256, COL_SK // LANE, COL_SV // LANE

    def body(q_ref, kp_ref, kc_ref, vp_ref, vc_ref, b_ref, sk_ref, y_ref, lse_ref):
        kvh, n = pl.program_id(0), pl.program_id(1)
        lane = lax.broadcasted_iota(jnp.int32, (1, LANE), 1)
        lo = lane < HEAD
        kvm = jnp.logical_and(lane >= kvh * HEAD, lane < (kvh + 1) * HEAD)

        def both(prev_ref, cur_ref):
            band = jnp.concatenate([prev_ref[...], cur_ref[...]], axis=0)
            band = jnp.where(kvm, band, 0.0)
            return (band + pltpu.roll(band, HEAD, 1)).astype(BF16)

        kb, vb = both(kp_ref, kc_ref), both(vp_ref, vc_ref)
        col = lax.broadcasted_iota(jnp.int32, (SWA_BLOCK, 2 * SWA_BLOCK), 1)
        first = jnp.logical_and(n == 0, col < SWA_BLOCK)
        outs = []
        for g in range(SWA_GROUP):
            half = q_ref[:, (g // 2) * LANE:(g // 2 + 1) * LANE]
            hm = lo if g % 2 == 0 else jnp.logical_not(lo)
            qg = jnp.where(hm, half, 0.0).astype(BF16)
            s = _dot(qg, kb, NT) * scale + b_ref[g]
            s = jnp.where(first, NEG, s)
            snk = sk_ref[g:g + 1, :]
            m = jnp.maximum(jnp.max(s, axis=1, keepdims=True), snk)
            p = jnp.exp(s - jnp.tile(m, (1, 2)))
            denom = jnp.sum(p, axis=1, keepdims=True) + jnp.exp(snk - m)
            outs.append(_dot(p.astype(BF16), vb, NN) / denom)
            lse_ref[g] = m + jnp.log(denom)
        y_ref[:, 0:LANE] = jnp.where(lo, outs[0], outs[1]).astype(BF16)
        y_ref[:, LANE:2 * LANE] = jnp.where(lo, outs[2], outs[3]).astype(BF16)

    def blk(c, shift):
        return pl.BlockSpec((SWA_BLOCK, LANE), lambda kvh, n, c=c, s=shift: (jnp.maximum(n - s, 0), c))

    return pl.pallas_call(
        body, name=name, grid=(2, nb),
        out_shape=(jax.ShapeDtypeStruct((t, BRANCH), BF16), jax.ShapeDtypeStruct((8, t, LANE), F32)),
        in_specs=[pl.BlockSpec((SWA_BLOCK, 256), lambda kvh, n: (n, csq + kvh)),
                  blk(csk, 1), blk(csk, 0), blk(csv, 1), blk(csv, 0),
                  pl.BlockSpec((None, SWA_GROUP, SWA_BLOCK, 256), lambda kvh, n: (kvh, 0, 0, 0)),
                  pl.BlockSpec((None, SWA_GROUP, LANE), lambda kvh, n: (kvh, 0, 0))],
        out_specs=(pl.BlockSpec((SWA_BLOCK, 256), lambda kvh, n: (n, kvh)),
                   pl.BlockSpec((SWA_GROUP, SWA_BLOCK, LANE), lambda kvh, n: (kvh, n, 0))),
        compiler_params=_params(("parallel", "arbitrary"), 4 << 20),
    )(proj, proj, proj, proj, proj, bias.reshape(2, SWA_GROUP, SWA_BLOCK, 256), sink_rep)


def _swa_bwd(proj, bias, sink_rep, lse, y, dy, name):
    t = proj.shape[0]
    nb = t // SWA_BLOCK
    scale = HEAD ** -0.5
    csq, csk, csv = COL_SQ // 256, COL_SK // LANE, COL_SV // LANE

    def body(q_ref, kp_ref, kc_ref, vp_ref, vc_ref, b_ref, sk_ref, lse_ref, y_ref, dy_ref,
             dq_ref, dkp_ref, dvp_ref, db_ref, dsk_ref):
        kvh, n = pl.program_id(0), pl.program_id(1)
        lane = lax.broadcasted_iota(jnp.int32, (1, LANE), 1)
        lo = lane < HEAD
        kvm = jnp.logical_and(lane >= kvh * HEAD, lane < (kvh + 1) * HEAD)

        def both(prev_ref, cur_ref):
            band = jnp.concatenate([prev_ref[...], cur_ref[...]], axis=0)
            band = jnp.where(kvm, band, 0.0)
            return (band + pltpu.roll(band, HEAD, 1)).astype(BF16)

        kb, vb = both(kp_ref, kc_ref), both(vp_ref, vc_ref)
        col = lax.broadcasted_iota(jnp.int32, (SWA_BLOCK, 2 * SWA_BLOCK), 1)
        first = jnp.logical_and(n == 0, col < SWA_BLOCK)

        @pl.when(n == 0)
        def _():
            db_ref[...] = jnp.zeros_like(db_ref)
            dsk_ref[...] = jnp.zeros_like(dsk_ref)

        dk_full = jnp.zeros((2 * SWA_BLOCK, LANE), F32)
        dv_full = jnp.zeros((2 * SWA_BLOCK, LANE), F32)
        dqs = []
        for g in range(SWA_GROUP):
            sl = slice((g // 2) * LANE, (g // 2 + 1) * LANE)
            hm = lo if g % 2 == 0 else jnp.logical_not(lo)
            qg = jnp.where(hm, q_ref[:, sl], 0.0).astype(BF16)
            dog = jnp.where(hm, dy_ref[:, sl], jnp.zeros((SWA_BLOCK, LANE), BF16))
            dmat = jnp.where(hm, y_ref[:, sl].astype(F32) * dy_ref[:, sl].astype(F32), 0.0)
            dg = jnp.sum(dmat, axis=1, keepdims=True)
            s = _dot(qg, kb, NT) * scale + b_ref[g]
            s = jnp.where(first, NEG, s)
            lse_g = lse_ref[g]
            p = jnp.exp(s - jnp.tile(lse_g, (1, 2)))
            dp = _dot(dog, vb, NT)
            ds = p * (dp - dg)
            dsb = ds.astype(BF16)
            dqs.append(_dot(dsb, kb, NN) * scale)
            dk_full = dk_full + _dot(dsb, qg, TN)
            dv_full = dv_full + _dot(p.astype(BF16), dog, TN)
            db_ref[g] += ds
            psink = jnp.exp(sk_ref[g:g + 1, :] - lse_g)
            dsk_ref[g:g + 1, :] -= jnp.sum(psink * dg, axis=0, keepdims=True)
        dq_ref[:, 0:LANE] = jnp.where(lo, dqs[0], dqs[1]).astype(BF16)
        dq_ref[:, LANE:2 * LANE] = jnp.where(lo, dqs[2], dqs[3]).astype(BF16)
        dkp_ref[...] = jnp.where(kvm, (dk_full + pltpu.roll(dk_full, HEAD, 1)) * scale, 0.0)
        dvp_ref[...] = jnp.where(kvm, dv_full + pltpu.roll(dv_full, HEAD, 1), 0.0)

    def blk(c, shift):
        return pl.BlockSpec((SWA_BLOCK, LANE), lambda kvh, n, c=c, s=shift: (jnp.maximum(n - s, 0), c))

    qblk = pl.BlockSpec((SWA_BLOCK, 256), lambda kvh, n: (n, kvh))
    part = pl.BlockSpec((None, None, 2 * SWA_BLOCK, LANE), lambda kvh, n: (kvh, n, 0, 0))
    bspec = pl.BlockSpec((None, SWA_GROUP, SWA_BLOCK, 256), lambda kvh, n: (kvh, 0, 0, 0))
    sspec = pl.BlockSpec((None, SWA_GROUP, LANE), lambda kvh, n: (kvh, 0, 0))
    return pl.pallas_call(
        body, name=name, grid=(2, nb),
        out_shape=(jax.ShapeDtypeStruct((t, BRANCH), BF16),
                   jax.ShapeDtypeStruct((2, nb, 2 * SWA_BLOCK, LANE), F32),
                   jax.ShapeDtypeStruct((2, nb, 2 * SWA_BLOCK, LANE), F32),
                   jax.ShapeDtypeStruct((2, SWA_GROUP, SWA_BLOCK, 256), F32),
                   jax.ShapeDtypeStruct((2, SWA_GROUP, LANE), F32)),
        in_specs=[pl.BlockSpec((SWA_BLOCK, 256), lambda kvh, n: (n, csq + kvh)),
                  blk(csk, 1), blk(csk, 0), blk(csv, 1), blk(csv, 0), bspec, sspec,
                  pl.BlockSpec((SWA_GROUP, SWA_BLOCK, LANE), lambda kvh, n: (kvh, n, 0)), qblk, qblk],
        out_specs=(qblk, part, part, bspec, sspec),
        compiler_params=_params(("parallel", "arbitrary"), 6 << 20),
    )(proj, proj, proj, proj, proj, bias.reshape(2, SWA_GROUP, SWA_BLOCK, 256), sink_rep, lse, y, dy)


def _gate_fwd(proj, pb, name):
    t = proj.shape[0]
    tr = _pick(t, ROW_TILE // 2, BF16_SUBLANE)

    def body(g0, g1, g2, p0, p1, p2, o_ref):
        acc = jax.nn.sigmoid(g0[...]) * p0[...]
        acc = acc + jax.nn.sigmoid(g1[...]) * p1[...]
        acc = acc + jax.nn.sigmoid(g2[...]) * p2[...]
        o_ref[...] = acc.astype(BF16)

    row = pl.BlockSpec((tr, D_MODEL), lambda i: (i, 0))
    gates = [pl.BlockSpec((tr, D_MODEL), lambda i, b=b: (i, b)) for b in range(3)]
    return pl.pallas_call(
        body, name=name, grid=(t // tr,),
        out_shape=jax.ShapeDtypeStruct((t, D_MODEL), BF16),
        in_specs=gates + [row] * 3, out_specs=row,
        compiler_params=_params(("parallel",), 7 * _nbytes((tr, D_MODEL), F32)),
    )(proj, proj, proj, *pb)


def _gate_bwd(proj, pb, dmerged, name):
    t = proj.shape[0]
    tr = _pick(t, ROW_TILE // 2, BF16_SUBLANE)

    def body(g0, g1, g2, p0, p1, p2, dm_ref, dp0, dp1, dp2, dg_ref):
        dm = dm_ref[...]
        for b, (g_ref, p_ref, dp_ref) in enumerate(((g0, p0, dp0), (g1, p1, dp1), (g2, p2, dp2))):
            sg = jax.nn.sigmoid(g_ref[...])
            dp_ref[...] = (dm * sg).astype(BF16)
            dg_ref[:, b * D_MODEL:(b + 1) * D_MODEL] = (dm * p_ref[...] * sg * (1.0 - sg)).astype(BF16)

    row = pl.BlockSpec((tr, D_MODEL), lambda i: (i, 0))
    gates = [pl.BlockSpec((tr, D_MODEL), lambda i, b=b: (i, b)) for b in range(3)]
    return pl.pallas_call(
        body, name=name, grid=(t // tr,),
        out_shape=(jax.ShapeDtypeStruct((t, D_MODEL), BF16),) * 3 + (jax.ShapeDtypeStruct((t, 3 * D_MODEL), BF16),),
        in_specs=gates + [row] * 4,
        out_specs=(row, row, row, pl.BlockSpec((tr, 3 * D_MODEL), lambda i: (i, 0))),
        compiler_params=_params(("parallel",), 11 * _nbytes((tr, D_MODEL), F32)),
    )(proj, proj, proj, *pb, dmerged)


def _swiglu_fwd(ab, name):
    t = ab.shape[0]
    tr = _pick(t, ROW_TILE, BF16_SUBLANE)
    tc = D_FF // 2

    def body(a_ref, b_ref, o_ref):
        a = a_ref[...]
        o_ref[...] = (a * jax.nn.sigmoid(a) * b_ref[...]).astype(BF16)

    return pl.pallas_call(
        body, name=name, grid=(t // tr, 2),
        out_shape=jax.ShapeDtypeStruct((t, D_FF), BF16),
        in_specs=[pl.BlockSpec((tr, tc), lambda i, j: (i, j)), pl.BlockSpec((tr, tc), lambda i, j: (i, j + 2))],
        out_specs=pl.BlockSpec((tr, tc), lambda i, j: (i, j)),
        compiler_params=_params(("parallel", "parallel"), 3 * _nbytes((tr, tc), F32)),
    )(ab, ab)


def _swiglu_bwd(ab, dh, name):
    t = ab.shape[0]
    tr = _pick(t, ROW_TILE, BF16_SUBLANE)
    tc = D_FF // 2

    def body(a_ref, b_ref, dh_ref, o_ref):
        jj = pl.program_id(1)
        a, b, d = a_ref[...], b_ref[...], dh_ref[...]
        sg = jax.nn.sigmoid(a)
        da = d * b * (sg * (1.0 + a * (1.0 - sg)))
        db = d * (a * sg)
        o_ref[...] = jnp.where(jj < 2, da, db).astype(BF16)

    return pl.pallas_call(
        body, name=name, grid=(t // tr, 4),
        out_shape=jax.ShapeDtypeStruct((t, 2 * D_FF), BF16),
        in_specs=[pl.BlockSpec((tr, tc), lambda i, j: (i, j % 2)),
                  pl.BlockSpec((tr, tc), lambda i, j: (i, j % 2 + 2)),
                  pl.BlockSpec((tr, tc), lambda i, j: (i, j % 2))],
        out_specs=pl.BlockSpec((tr, tc), lambda i, j: (i, j)),
        compiler_params=_params(("parallel", "parallel"), 4 * _nbytes((tr, tc), F32)),
    )(ab, ab, dh)


def _xattn_fwd(q, kv, name):
    t = q.shape[0]
    tq = _pick(t, ROW_TILE, BF16_SUBLANE)
    mlen = kv.shape[0]
    scale = X_HEAD ** -0.5

    def body(q_ref, kv_ref, o_ref):
        for h in range(X_HEADS):
            sl = slice(h * X_HEAD, (h + 1) * X_HEAD)
            kh = kv_ref[:, sl]
            vh = kv_ref[:, D_MODEL + h * X_HEAD:D_MODEL + (h + 1) * X_HEAD]
            s = _dot(q_ref[:, sl], kh, NT) * scale
            p = jnp.exp(s - jnp.max(s, axis=1, keepdims=True))
            l = jnp.sum(p, axis=1, keepdims=True)
            o_ref[:, sl] = (_dot(p.astype(BF16), vh, NN) / l).astype(BF16)

    return pl.pallas_call(
        body, name=name, grid=(t // tq,),
        out_shape=jax.ShapeDtypeStruct((t, D_MODEL), BF16),
        in_specs=[pl.BlockSpec((tq, D_MODEL), lambda i: (i, 0)), pl.BlockSpec((mlen, 2 * D_MODEL), lambda i: (0, 0))],
        out_specs=pl.BlockSpec((tq, D_MODEL), lambda i: (i, 0)),
        compiler_params=_params(("parallel",), 4 * _nbytes((tq, D_MODEL), F32)),
    )(q, kv)


def _xattn_bwd(q, kv, do, name):
    t = q.shape[0]
    tq = _pick(t, ROW_TILE, BF16_SUBLANE)
    mlen = kv.shape[0]
    scale = X_HEAD ** -0.5

    def body(q_ref, kv_ref, do_ref, dq_ref, dkv_ref):
        i = pl.program_id(0)

        @pl.when(i == 0)
        def _():
            dkv_ref[...] = jnp.zeros_like(dkv_ref)

        for h in range(X_HEADS):
            sl = slice(h * X_HEAD, (h + 1) * X_HEAD)
            vsl = slice(D_MODEL + h * X_HEAD, D_MODEL + (h + 1) * X_HEAD)
            qh, kh, vh, doh = q_ref[:, sl], kv_ref[:, sl], kv_ref[:, vsl], do_ref[:, sl]
            s = _dot(qh, kh, NT) * scale
            p = jnp.exp(s - jnp.max(s, axis=1, keepdims=True))
            p = p / jnp.sum(p, axis=1, keepdims=True)
            dp = _dot(doh, vh, NT)
            ds = p * (dp - jnp.sum(p * dp, axis=1, keepdims=True))
            dsb = ds.astype(BF16)
            dq_ref[:, sl] = (_dot(dsb, kh, NN) * scale).astype(BF16)
            dkv_ref[:, sl] += _dot(dsb, qh, TN) * scale
            dkv_ref[:, vsl] += _dot(p.astype(BF16), doh, TN)

    row = pl.BlockSpec((tq, D_MODEL), lambda i: (i, 0))
    whole = pl.BlockSpec((mlen, 2 * D_MODEL), lambda i: (0, 0))
    return pl.pallas_call(
        body, name=name, grid=(t // tq,),
        out_shape=(jax.ShapeDtypeStruct((t, D_MODEL), BF16), jax.ShapeDtypeStruct((mlen, 2 * D_MODEL), F32)),
        in_specs=[row, whole, row], out_specs=(row, whole),
        compiler_params=_params(("arbitrary",), 6 * _nbytes((tq, D_MODEL), F32)),
    )(q, kv, do)


def _position():
    return lax.axis_index("x"), lax.axis_index("y"), lax.axis_index("c")


N_PEER = N_DEV - 1


def _all_gather(xs, name):
    n = len(xs)

    def body(*refs):
        x_refs, out_refs = refs[:n], refs[n:2 * n]
        send_sems, recv_sems, local_sems = refs[2 * n:]
        mx, my, mc = _position()
        me, sib = (mx, my, mc), (mx, my, 1 - mc)
        chips = [(1 - mx, my), (mx, 1 - my), (1 - mx, 1 - my)]

        def slot(i, p):
            return out_refs[i].at[4 * p[0] + 2 * p[1] + p[2]]

        def copy(i, k, block, to, src=None):
            return pltpu.make_async_remote_copy(
                src_ref=slot(i, block) if src is None else src, dst_ref=slot(i, block),
                send_sem=send_sems.at[i * N_PEER + k], recv_sem=recv_sems.at[i * N_PEER + k],
                device_id=to, device_id_type=MESH)

        mine = [pltpu.make_async_copy(x_refs[i], slot(i, me), local_sems.at[i]) for i in range(n)]
        for cp in mine:
            cp.start()
        first = [copy(i, 1 + j, me, (*chip, mc), src=x_refs[i]) for j, chip in enumerate(chips) for i in range(n)]
        first += [copy(i, 0, me, sib, src=x_refs[i]) for i in range(n)]
        for cp in first:
            cp.start()
        passed = []
        for j, chip in enumerate(chips):
            for i in range(n):
                copy(i, 1 + j, (*chip, mc), me).wait_recv()
                passed.append(copy(i, 4 + j, (*chip, mc), sib))
                passed[-1].start()
        for i in range(n):
            copy(i, 0, sib, me).wait_recv()
        for j, chip in enumerate(chips):
            for i in range(n):
                copy(i, 4 + j, (*chip, 1 - mc), me).wait_recv()
        for cp in first + passed:
            cp.wait_send()
        for cp in mine:
            cp.wait()

    return pl.pallas_call(
        body, name=name,
        out_shape=tuple(jax.ShapeDtypeStruct((N_DEV,) + x.shape, x.dtype) for x in xs),
        in_specs=[pl.BlockSpec(memory_space=pl.ANY)] * n, out_specs=(pl.BlockSpec(memory_space=pl.ANY),) * n,
        scratch_shapes=[pltpu.SemaphoreType.DMA((n * N_PEER,)), pltpu.SemaphoreType.DMA((n * N_PEER,)),
                        pltpu.SemaphoreType.DMA((n,))],
    )(*xs)


PEER_RELS = [(dx, dy, dc) for dx in (0, 1) for dy in (0, 1) for dc in (0, 1)][1:]


def _peer_copy(rel_k, i, src_refs, land_refs, send_sems, recv_sems, layer, gather, arriving):
    mx, my, mc = _position()
    me_idx = 4 * mx + 2 * my + mc
    p = tuple((1 - v) if f else v for f, v in zip(PEER_RELS[rel_k], (mx, my, mc)))
    p_idx = 4 * p[0] + 2 * p[1] + p[2]
    src_slot, dst_slot = (me_idx, p_idx) if arriving else (p_idx, me_idx)
    src = src_refs[i] if gather else src_refs[i].at[src_slot]
    dst = land_refs[i].at[dst_slot] if layer is None else land_refs[i].at[dst_slot, layer]
    return pltpu.make_async_remote_copy(
        src_ref=src, dst_ref=dst, send_sem=send_sems.at[i * N_PEER + rel_k], recv_sem=recv_sems.at[i * N_PEER + rel_k],
        device_id=p, device_id_type=MESH)


def _own_copy(i, src_refs, land_refs, sem, layer, gather):
    mx, my, mc = _position()
    me_idx = 4 * mx + 2 * my + mc
    src = src_refs[i] if gather else src_refs[i].at[me_idx]
    dst = land_refs[i].at[me_idx] if layer is None else land_refs[i].at[me_idx, layer]
    return pltpu.make_async_copy(src, dst, sem)


def _exchange(parts, lands, layer, name):
    n = len(parts)

    def body(*refs):
        g_refs, land_refs = refs[:n], refs[2 * n:3 * n]
        send_sems, recv_sems, local_sems = refs[3 * n:]
        args = (g_refs, land_refs, send_sems, recv_sems, layer, False)
        mine = [_own_copy(i, g_refs, land_refs, local_sems.at[i], layer, False) for i in range(n)]
        for cp in mine:
            cp.start()
        sends = [_peer_copy(k, i, *args, False) for i in range(n) for k in range(N_PEER)]
        for cp in sends:
            cp.start()
        for i in range(n):
            for k in range(N_PEER):
                _peer_copy(k, i, *args, True).wait_recv()
        for cp in sends:
            cp.wait_send()
        for cp in mine:
            cp.wait()

    return pl.pallas_call(
        body, name=name,
        out_shape=tuple(jax.ShapeDtypeStruct(l.shape, l.dtype) for l in lands),
        in_specs=[pl.BlockSpec(memory_space=pl.ANY)] * (2 * n), out_specs=(pl.BlockSpec(memory_space=pl.ANY),) * n,
        input_output_aliases={n + i: i for i in range(n)},
        scratch_shapes=[pltpu.SemaphoreType.DMA((n * N_PEER,)), pltpu.SemaphoreType.DMA((n * N_PEER,)),
                        pltpu.SemaphoreType.DMA((n,))],
    )(*parts, *lands)


HBM_SPEC = pl.BlockSpec(memory_space=pltpu.HBM)
SEM_SPEC = pl.BlockSpec(memory_space=pltpu.SEMAPHORE)
SIDE_EFFECT = pltpu.SideEffectType.DATAFLOW_SIDE_EFFECTING


def _own_slots(srcs, lands, layer, gather):
    mx, my, mc = _position()
    me_idx = 4 * mx + 2 * my + mc
    out = []
    for s, land in zip(srcs, lands):
        piece = s[None] if gather else lax.dynamic_index_in_dim(s, me_idx, 0, keepdims=True)
        if layer is None:
            start = (me_idx,) + (0,) * (land.ndim - 1)
        else:
            piece, start = piece[:, None], (me_idx, layer) + (0,) * (land.ndim - 2)
        out.append(lax.dynamic_update_slice(land, piece, start))
    return out


def _swap_start(srcs, lands, layer, gather, name):
    n = len(srcs)

    def body(*refs):
        src_refs, land_refs = refs[:n], refs[n:2 * n]
        send_sems, recv_sems = refs[2 * n], refs[2 * n + 1]
        token = refs[4 * n + 2]
        for i in range(n):
            for k in range(N_PEER):
                _peer_copy(k, i, src_refs, land_refs, send_sems, recv_sems, layer, gather, False).start()
        token[...] = jnp.zeros_like(token)

    hbm = [pltpu.with_memory_space_constraint(a, pltpu.HBM) for a in list(srcs) + list(lands)]
    out = pl.pallas_call(
        body, name=name,
        out_shape=(pltpu.SemaphoreType.DMA((n * N_PEER,)), pltpu.SemaphoreType.DMA((n * N_PEER,)))
        + tuple(pltpu.HBM(a.shape, a.dtype) for a in hbm) + (jax.ShapeDtypeStruct((8, LANE), F32),),
        in_specs=[HBM_SPEC] * (2 * n),
        out_specs=(SEM_SPEC, SEM_SPEC) + (HBM_SPEC,) * (2 * n) + (pl.BlockSpec(memory_space=pltpu.VMEM),),
        input_output_aliases={i: 2 + i for i in range(2 * n)},
        compiler_params=pltpu.CompilerParams(has_side_effects=SIDE_EFFECT),
    )(*hbm)
    return out[0], out[1], list(out[2:2 + n]), list(out[2 + n:2 + 2 * n]), out[2 + 2 * n]


def _swap_wait(send_sems, recv_sems, srcs, lands, after, layer, gather, name):
    n = len(srcs)

    def body(*refs):
        src_refs, land_refs = refs[:n], refs[n:2 * n]
        send_sems_ref, recv_sems_ref = refs[2 * n], refs[2 * n + 1]
        for i in range(n):
            for k in range(N_PEER):
                args = (src_refs, land_refs, send_sems_ref, recv_sems_ref, layer, gather)
                _peer_copy(k, i, *args, False).wait_send()
                _peer_copy(k, i, *args, True).wait_recv()

    out = pl.pallas_call(
        body, name=name,
        out_shape=tuple(pltpu.HBM(a.shape, a.dtype) for a in list(srcs) + list(lands)),
        in_specs=[HBM_SPEC] * (2 * n) + [SEM_SPEC, SEM_SPEC, pl.BlockSpec(memory_space=pl.ANY)],
        out_specs=(HBM_SPEC,) * (2 * n),
        input_output_aliases={i: i for i in range(2 * n)},
        compiler_params=pltpu.CompilerParams(has_side_effects=SIDE_EFFECT),
    )(*srcs, *lands, send_sems, recv_sems, after)
    return list(out[n:])


ADAMW_BLOCK_BYTES = 1 << 20


def _adamw(parts, w, m, v, name):
    r, l = w.shape
    tr = _pick(r, max(ADAMW_BLOCK_BYTES // (4 * l), BF16_SUBLANE), BF16_SUBLANE)
    c1 = 1.0 - ADAM_B1 ** ADAM_STEP
    c2 = 1.0 - ADAM_B2 ** ADAM_STEP

    def body(p_ref, w_ref, m_ref, v_ref, g_ref, d_ref, nm_ref, nv_ref):
        g = p_ref[0].astype(F32)
        for s in range(1, N_DEV):
            g = g + p_ref[s].astype(F32)
        nm = ADAM_B1 * m_ref[...] + (1.0 - ADAM_B1) * g
        nv = ADAM_B2 * v_ref[...] + (1.0 - ADAM_B2) * (g * g)
        m_hat = nm / c1
        v_hat = nv / c2
        g_ref[...] = g
        d_ref[...] = -ADAM_LR * (m_hat / (jnp.sqrt(v_hat) + ADAM_EPS) + ADAM_WD * w_ref[...])
        nm_ref[...] = nm
        nv_ref[...] = nv

    row = pl.BlockSpec((tr, l), lambda i: (i, 0))
    return pl.pallas_call(
        body, name=name, grid=(r // tr,),
        out_shape=(jax.ShapeDtypeStruct((r, l), F32),) * 4,
        in_specs=[pl.BlockSpec((N_DEV, tr, l), lambda i: (0, i, 0)), row, row, row],
        out_specs=(row,) * 4,
        compiler_params=_params(("parallel",), 12 * _nbytes((tr, l), F32)),
    )(parts, w, m, v)


MATRIX_WEIGHTS = (("w_in", 2), ("conv_w", 2), ("w_branch", 3), ("w_mix_out", 1), ("w_xq", 1), ("w_xkv", 2),
                  ("w_xo", 1), ("w_ffn_gate", 2), ("w_ffn_up", 2), ("w_ffn_down", 1))
SMALL_PARAMS = ("mix_norm_g", "xattn_norm_g", "mem_norm_g", "ffn_norm_g", "final_norm_g", "forget_bias", "sink",
                "rel_bias")


def _pack_small(pieces):
    flat = jnp.concatenate([p.astype(F32).reshape(-1) for p in pieces])
    total = -(-flat.shape[0] // (8 * LANE)) * (8 * LANE)
    return jnp.pad(flat, (0, total - flat.shape[0])).reshape(total // LANE, LANE)


def _rows(a):
    return a.reshape(-1, a.shape[-1])


def _to_full(gathered, axis):
    moved = jnp.moveaxis(gathered, 0, axis)
    shape = list(moved.shape)
    shape[axis:axis + 2] = [shape[axis] * shape[axis + 1]]
    return moved.reshape(shape)


def _to_blocks(full, axis):
    shape = list(full.shape)
    shape[axis:axis + 1] = [N_DEV, shape[axis] // N_DEV]
    return jnp.moveaxis(full.reshape(shape), axis, 0)


def _perm_in(w_in):
    pad = jnp.zeros((w_in.shape[0], PROJ_COLS - IN_COLS), w_in.dtype)
    return jnp.concatenate([w_in[:, 3848:6920], w_in[:, 0:3072], w_in[:, 3080:3848], w_in[:, 3072:3080], pad], axis=1)


def _unperm_in(dw):
    return jnp.concatenate([dw[:, 3072:6144], dw[:, 6912:6920], dw[:, 6144:6912], dw[:, 0:3072]], axis=1)


def _layer_fwd(l, x, mem, wt, sm):
    t = x.shape[0]
    tag = f"l{l}_"
    h = _rms_fwd(x, sm["mix_norm_g"][l], tag + "mix_norm")
    proj = _matmul(h, wt["w_in"][l], "nn", F32, tag + "in_proj")
    y_conv = _conv_fwd(proj, wt["conv_w"][l], tag + "conv")
    fbias_row = jnp.pad(sm["forget_bias"][l], (0, LANE - 8)).reshape(1, LANE)
    c = _logf_cumsum(proj, fbias_row, tag + "logf_cumsum")
    c8 = c[:, :8].T
    c_col = jnp.broadcast_to(c8[:, :, None], (8, t, LANE))
    c_row = c8.reshape(8, 1, t)
    y_fox, lse_fox = _fox_fwd(proj, c_col, c_row, tag + "fox")
    onehot, bias = _swa_tables(sm["rel_bias"])
    sink_rep = jnp.broadcast_to(sm["sink"][l].reshape(2, SWA_GROUP, 1), (2, SWA_GROUP, LANE))
    y_swa, lse_swa = _swa_fwd(proj, bias, sink_rep, tag + "swa")
    ys = (y_conv, y_fox, y_swa)
    pb = tuple(_matmul(ys[b], wt["w_branch"][l][b], "nn", F32, tag + f"branch{b}") for b in range(3))
    merged = _gate_fwd(proj, pb, tag + "gate")
    x1 = _matmul(merged, wt["w_mix_out"][l], "nn", F32, tag + "mix_out", residual=x)
    xn2 = _rms_fwd(x1, sm["xattn_norm_g"][l], tag + "xattn_norm")
    q = _matmul(xn2, wt["w_xq"][l], "nn", BF16, tag + "xq")
    mem_n = _rms_fwd(mem, sm["mem_norm_g"][l], tag + "mem_norm")
    kv = _matmul(mem_n, wt["w_xkv"][l], "nn", BF16, tag + "xkv")
    o = _xattn_fwd(q, kv, tag + "xattn")
    x2 = _matmul(o, wt["w_xo"][l], "nn", F32, tag + "xo", residual=x1)
    xn3 = _rms_fwd(x2, sm["ffn_norm_g"][l], tag + "ffn_norm")
    ab = _matmul(xn3, wt["w_gu"][l], "nn", F32, tag + "ffn_gu")
    h1 = _swiglu_fwd(ab, tag + "swiglu")
    x3 = _matmul(h1, wt["w_ffn_down"][l], "nn", F32, tag + "ffn_down", residual=x2)
    saved = dict(x=x, h=h, proj=proj, fbias_row=fbias_row, c_col=c_col, c_row=c_row, ys=ys, lse_fox=lse_fox,
                 onehot=onehot, bias=bias, sink_rep=sink_rep, lse_swa=lse_swa, pb=pb, merged=merged, x1=x1,
                 xn2=xn2, q=q, mem_n=mem_n, kv=kv, o=o, x2=x2, xn3=xn3, ab=ab, h1=h1)
    return x3, saved


def _layer_bwd(l, dx3, dx3_b, mem, wt, sm, sv, mid_hook=None):
    t = dx3.shape[0]
    nb = t // SWA_BLOCK
    tag = f"l{l}_b_"
    gw, gs = {}, {}
    dh1 = _matmul(dx3_b, wt["w_ffn_down"][l], "nt", F32, tag + "d_h1")
    gw["w_ffn_down"] = _matmul(sv["h1"], dx3_b, "tn", F32, tag + "dw_down")
    dab = _swiglu_bwd(sv["ab"], dh1, tag + "swiglu")
    dxn3 = _matmul(dab, wt["w_gu"][l], "nt", F32, tag + "d_xn3")
    dw_gu = _matmul(sv["xn3"], dab, "tn", F32, tag + "dw_gu")
    gw["w_ffn_gate"], gw["w_ffn_up"] = dw_gu[:, :D_FF], dw_gu[:, D_FF:]
    dx2, dx2_b, gs["ffn_norm_g"] = _rms_bwd(sv["x2"], sm["ffn_norm_g"][l], dxn3, dx3, tag + "ffn_norm")
    do = _matmul(dx2_b, wt["w_xo"][l], "nt", BF16, tag + "d_o")
    gw["w_xo"] = _matmul(sv["o"], dx2_b, "tn", F32, tag + "dw_xo")
    dq, dkv = _xattn_bwd(sv["q"], sv["kv"], do, tag + "xattn")
    gw["w_xkv"] = _matmul(sv["mem_n"], dkv, "tn", F32, tag + "dw_xkv")
    dmem_n = _matmul(dkv, wt["w_xkv"][l], "nt", F32, tag + "d_memn")
    _, _, gs["mem_norm_g"] = _rms_bwd(mem, sm["mem_norm_g"][l], dmem_n, None, tag + "mem_norm")
    gw["w_xq"] = _matmul(sv["xn2"], dq, "tn", F32, tag + "dw_xq")
    dxn2 = _matmul(dq, wt["w_xq"][l], "nt", F32, tag + "d_xn2")
    dx1, dx1_b, gs["xattn_norm_g"] = _rms_bwd(sv["x1"], sm["xattn_norm_g"][l], dxn2, dx2, tag + "xattn_norm")
    dmerged = _matmul(dx1_b, wt["w_mix_out"][l], "nt", F32, tag + "d_merged")
    gw["w_mix_out"] = _matmul(sv["merged"], dx1_b, "tn", F32, tag + "dw_mix_out")
    dp0, dp1, dp2, dgate = _gate_bwd(sv["proj"], sv["pb"], dmerged, tag + "gate")
    dps = (dp0, dp1, dp2)
    dy_dtypes = (F32, BF16, BF16)
    dys = [_matmul(dps[b], wt["w_branch"][l][b], "nt", dy_dtypes[b], tag + f"d_y{b}") for b in range(3)]
    gw["w_branch"] = jnp.stack(
        [_matmul(sv["ys"][b], dps[b], "tn", F32, tag + f"dw_branch{b}") for b in range(3)])
    sink_rep = sv["sink_rep"] if mid_hook is None else sv["sink_rep"] + mid_hook(gw)
    dsq, dkp, dvp, dbias, dsink = _swa_bwd(sv["proj"], sv["bias"], sink_rep, sv["lse_swa"], sv["ys"][2],
                                           dys[2], tag + "swa")

    def band_add(part):
        tot = part[0] + part[1]
        cur = tot[:, SWA_BLOCK:, :]
        nxt = jnp.concatenate([tot[1:, :SWA_BLOCK, :], jnp.zeros((1, SWA_BLOCK, LANE), F32)], axis=0)
        return (cur + nxt).reshape(t, LANE).astype(BF16)

    dsk, dsv = band_add(dkp), band_add(dvp)
    gs["rel_bias_l"] = jnp.einsum("hts,tsb->bh", dbias.reshape(8, SWA_BLOCK, 2 * SWA_BLOCK), sv["onehot"],
                                  precision=lax.Precision.HIGHEST)
    gs["sink"] = dsink[:, :, 0].reshape(8)
    dfq, dfk, dfv, dck, dcq_row = _fox_bwd(sv["proj"], sv["c_col"], sv["c_row"], sv["lse_fox"], sv["ys"][1], dys[1],
                                           tag + "fox_bwd")
    dcq = jnp.pad(dcq_row.reshape(8, t).T, ((0, 0), (0, LANE - 8))).reshape(1, t, LANE)
    dfg, dfb = _logf_cumsum_bwd(sv["proj"], sv["fbias_row"], [dck, dcq], tag + "logf_cumsum")
    gs["forget_bias"] = dfb[0, :8]
    dcb, dcc, dcu, dconv = _conv_bwd(sv["proj"], wt["conv_w"][l], dys[0], tag + "conv")
    gw["conv_w"] = dconv[:3]
    dproj = jnp.concatenate([dgate, dcb, dcc, dcu, dfq, dfk, dfv, dsq, dsk, dsv, dfg], axis=1)
    dh = _matmul(dproj, wt["w_in"][l], "nt", F32, tag + "d_h")
    gw["w_in"] = _unperm_in(_matmul(sv["h"], dproj, "tn", F32, tag + "dw_in"))
    dx, dx_b, gs["mix_norm_g"] = _rms_bwd(sv["x"], sm["mix_norm_g"][l], dh, dx1, tag + "mix_norm")
    return dx, dx_b, gw, gs


def kernel(x, mem, mix_norm_g, w_in, forget_bias, conv_w, sink, w_branch, w_mix_out, rel_bias, xattn_norm_g, mem_norm_g, w_xq, w_xkv, w_xo, ffn_norm_g, w_ffn_gate, w_ffn_up, w_ffn_down, final_norm_g, loss_target, m_mix_norm_g, m_w_in, m_forget_bias, m_conv_w, m_sink, m_w_branch, m_w_mix_out, m_rel_bias, m_xattn_norm_g, m_mem_norm_g, m_w_xq, m_w_xkv, m_w_xo, m_ffn_norm_g, m_w_ffn_gate, m_w_ffn_up, m_w_ffn_down, m_final_norm_g, v_mix_norm_g, v_w_in, v_forget_bias, v_conv_w, v_sink, v_w_branch, v_w_mix_out, v_rel_bias, v_xattn_norm_g, v_mem_norm_g, v_w_xq, v_w_xkv, v_w_xo, v_ffn_norm_g, v_w_ffn_gate, v_w_ffn_up, v_w_ffn_down, v_final_norm_g):
    args = dict(locals())
    names = [n for n, _ in MATRIX_WEIGHTS] + list(SMALL_PARAMS)
    w = {n: args[n] for n in names}
    mo = {n: args["m_" + n] for n in names}
    vo = {n: args["v_" + n] for n in names}
    x2d, mem2d, tgt = x[0], mem[0], loss_target[0]

    wire = {n: (F32 if n == "conv_w" else BF16) for n, _ in MATRIX_WEIGHTS}
    wt = {n: [None] * DEPTH for n, _ in MATRIX_WEIGHTS}
    wt["w_gu"] = [None] * DEPTH

    def place_weights(l, gathered):
        for (n, ax), g in zip(MATRIX_WEIGHTS, gathered):
            wt[n][l] = _to_full(g, ax - 1)
        wt["w_in"][l] = _perm_in(wt["w_in"][l])
        wt["w_gu"][l] = jnp.concatenate([wt["w_ffn_gate"][l], wt["w_ffn_up"][l]], axis=1)

    shards = [[w[n][l].astype(wire[n]) for n, _ in MATRIX_WEIGHTS] for l in range(DEPTH)]
    place_weights(0, _all_gather(shards[0], "weights_gather_l0"))
    lands = _own_slots(shards[1], [lax.empty((N_DEV,) + s.shape, s.dtype) for s in shards[1]], None, True)
    w_send, w_recv, w_srcs, lands, token = _swap_start(shards[1], lands, None, True, "weights_gather_l1_start")
    sm = {n: w[n] for n in SMALL_PARAMS}
    sm["mix_norm_g"] = w["mix_norm_g"].at[0].add(token[0, 0])

    saved = []
    xc = x2d
    for l in range(DEPTH):
        if l == 1:
            place_weights(1, _swap_wait(w_send, w_recv, w_srcs, lands, xc, None, True, "weights_gather_l1_wait"))
        xc, sv = _layer_fwd(l, xc, mem2d, wt, sm)
        saved.append(sv)
    loss_row, dx, dx_b, dg_final = _loss_head(xc, sm["final_norm_g"], tgt, "loss_head")
    loss = lax.psum(loss_row[0, 0], ("x", "y", "c"))

    late = ("w_in", "conv_w")
    early_w = [(n, ax) for n, ax in MATRIX_WEIGHTS if n not in late]
    late_w = [(n, ax) for n, ax in MATRIX_WEIGHTS if n in late]

    def grad_parts(gw, which):
        return [_to_blocks(gw[n], ax - 1).astype(wire[n]) for n, ax in which]

    gw_all, gs_all = [None] * DEPTH, [None] * DEPTH
    dx, dx_b, gw_all[1], gs_all[1] = _layer_bwd(1, dx, dx_b, mem2d, wt, sm, saved[1])
    parts1 = grad_parts(gw_all[1], MATRIX_WEIGHTS)
    zones = _own_slots(parts1, [lax.empty((N_DEV, DEPTH) + p.shape[1:], p.dtype) for p in parts1], 1, False)
    g_send, g_recv, g_srcs, zones, token = _swap_start(parts1, zones, 1, False, "grads_exchange_l1_start")
    sm_b = dict(sm)
    sm_b["ffn_norm_g"] = sm["ffn_norm_g"].at[0].add(token[0, 0])
    mid = {}

    def mid_hook(gw):
        zone = dict(zip([n for n, _ in MATRIX_WEIGHTS],
                        _swap_wait(g_send, g_recv, g_srcs, zones, gw["w_mix_out"], 1, False, "grads_exchange_l1_wait")))
        parts0 = grad_parts(gw, early_w)
        early_zones = _own_slots(parts0, [zone[n] for n, _ in early_w], 0, False)
        mid["early"] = _swap_start(parts0, early_zones, 0, False, "grads_exchange_l0_early_start")
        mid["late_zones"] = [zone[n] for n, _ in late_w]
        return mid["early"][4][0, 0]

    dx, dx_b, gw_all[0], gs_all[0] = _layer_bwd(0, dx, dx_b, mem2d, wt, sm_b, saved[0], mid_hook)
    grad_x = dx[None]
    e_send, e_recv, e_srcs, e_zones, _ = mid["early"]
    recv_by_name = dict(zip([n for n, _ in early_w],
                            _swap_wait(e_send, e_recv, e_srcs, e_zones, dx, 0, False, "grads_exchange_l0_early_wait")))
    recv_by_name.update(zip([n for n, _ in late_w],
                            _exchange(grad_parts(gw_all[0], late_w), mid["late_zones"], 0, "grads_exchange_l0_late")))
    recv = [recv_by_name[n] for n, _ in MATRIX_WEIGHTS]

    outs = {}
    for (n, _), r in zip(MATRIX_WEIGHTS, recv):
        res = _adamw(r.reshape((N_DEV,) + _rows(w[n]).shape), _rows(w[n]), _rows(mo[n]), _rows(vo[n]), "adamw_" + n)
        outs[n] = [o.reshape(w[n].shape) for o in res]

    gsm = {n: jnp.stack([gs_all[l][n] for l in range(DEPTH)])
           for n in ("mix_norm_g", "xattn_norm_g", "mem_norm_g", "ffn_norm_g", "forget_bias", "sink")}
    gsm["final_norm_g"] = dg_final
    gsm["rel_bias"] = gs_all[0]["rel_bias_l"] + gs_all[1]["rel_bias_l"]
    (small_parts,) = _all_gather([_pack_small([gsm[n] for n in SMALL_PARAMS])], "small_grads_all_gather")
    outs_small = _adamw(small_parts, *[_pack_small([d[n] for n in SMALL_PARAMS]) for d in (w, mo, vo)], "adamw_small")
    for kind in range(4):
        flat, o = outs_small[kind].reshape(-1), 0
        for n in SMALL_PARAMS:
            sz = int(np.prod(w[n].shape))
            outs.setdefault(n, []).append(flat[o:o + sz].reshape(w[n].shape))
            o += sz

    order = ["mix_norm_g", "w_in", "forget_bias", "conv_w", "sink", "w_branch", "w_mix_out", "rel_bias",
             "xattn_norm_g", "mem_norm_g", "w_xq", "w_xkv", "w_xo", "ffn_norm_g", "w_ffn_gate", "w_ffn_up",
             "w_ffn_down", "final_norm_g"]
    result = [loss, grad_x]
    for kind in range(4):
        result += [outs[n][kind] for n in order]
    return tuple(result)
```

```python
import math

import numpy as np
import jax
import jax.numpy as jnp
from jax import lax
from jax.experimental import pallas as pl
from jax.experimental.pallas import tpu as pltpu

F32 = jnp.float32
BF16 = jnp.bfloat16
MESH = pl.DeviceIdType.MESH

LANE = 128
BF16_SUBLANE = 16
V7X_VMEM_REQUEST_CAP = 56 * 2 ** 20
N_DEV = 8

D_MODEL = 1024
DEPTH = 2
HEAD = 64
BRANCH = 512
SWA_BLOCK = 128
SWA_GROUP = 4
N_BUCKETS = 32
X_HEADS = 4
X_HEAD = 256
D_FF = 2816
FF_SHARD = D_FF // N_DEV
FF_SHARD_P = -(-FF_SHARD // LANE) * LANE
D_FF_P = N_DEV * FF_SHARD_P
RMS_EPS = 1e-6
NEG = -1e30
ADAM_LR, ADAM_B1, ADAM_B2, ADAM_EPS, ADAM_WD, ADAM_STEP = 0.001, 0.9, 0.999, 1e-08, 0.01, 10

IN_COLS = 6920
PROJ_COLS = 7040
COL_GATE, COL_CONV, COL_FOX, COL_SQ, COL_SK, COL_SV, COL_FG = 0, 3072, 4608, 6144, 6656, 6784, 6912

ROW_TILE = 512
FOX_TILE = 512
MM_TM, MM_TN, MM_TK = 1024, 1536, 1024


def _pick(n, cap, mult):
    best = None
    for d in range(mult, min(n, cap) + 1, mult):
        if n % d == 0:
            best = d
    return n if best is None else best


def _params(semantics, block_bytes):
    limit = int(min(max(2 * block_bytes + (8 << 20), 24 << 20), V7X_VMEM_REQUEST_CAP))
    return pltpu.CompilerParams(dimension_semantics=semantics, vmem_limit_bytes=limit)


def _nbytes(shape, dtype):
    return int(np.prod(shape)) * jnp.dtype(dtype).itemsize


def _dot(a, b, dims):
    return lax.dot_general(a, b, (dims, ((), ())), preferred_element_type=F32)


NN = ((1,), (0,))
NT = ((1,), (1,))
TN = ((0,), (0,))


def _matmul(a, b, mode, out_dtype, name, residual=None):
    if mode == "nn":
        (m, k), (k2, n) = a.shape, b.shape
    elif mode == "nt":
        (m, k), (n, k2) = a.shape, b.shape
    else:
        (k, m), (k2, n) = a.shape, b.shape
    assert k == k2, (name, a.shape, b.shape)
    tm, tn, tk = _pick(m, MM_TM, LANE), _pick(n, MM_TN, LANE), _pick(k, MM_TK, LANE)
    nk = k // tk
    dims = {"nn": NN, "nt": NT, "tn": TN}[mode]
    has_res = residual is not None

    def body(*refs):
        a_ref, b_ref = refs[0], refs[1]
        r_ref = refs[2] if has_res else None
        o_ref = refs[3] if has_res else refs[2]
        kk = pl.program_id(2)
        p = _dot(a_ref[...].astype(BF16), b_ref[...].astype(BF16), dims)
        if nk == 1:
            if has_res:
                p = p + r_ref[...]
            o_ref[...] = p.astype(out_dtype)
        else:
            acc_ref = refs[-1]

            @pl.when(kk == 0)
            def _():
                acc_ref[...] = p

            @pl.when(kk > 0)
            def _():
                acc_ref[...] += p

            @pl.when(kk == nk - 1)
            def _():
                res = acc_ref[...]
                if has_res:
                    res = res + r_ref[...]
                o_ref[...] = res.astype(out_dtype)

    if mode == "nn":
        a_spec = pl.BlockSpec((tm, tk), lambda i, j, kk: (i, kk))
        b_spec = pl.BlockSpec((tk, tn), lambda i, j, kk: (kk, j))
    elif mode == "nt":
        a_spec = pl.BlockSpec((tm, tk), lambda i, j, kk: (i, kk))
        b_spec = pl.BlockSpec((tn, tk), lambda i, j, kk: (j, kk))
    else:
        a_spec = pl.BlockSpec((tk, tm), lambda i, j, kk: (kk, i))
        b_spec = pl.BlockSpec((tk, tn), lambda i, j, kk: (kk, j))
    o_spec = pl.BlockSpec((tm, tn), lambda i, j, kk: (i, j))
    in_specs, args = [a_spec, b_spec], [a, b]
    if has_res:
        in_specs.append(o_spec)
        args.append(residual)
    blk = (_nbytes((tm, tk), a.dtype) + _nbytes((tk, tn), b.dtype) + _nbytes((tm, tn), out_dtype)
           + (_nbytes((tm, tn), F32) if has_res else 0))
    scratch = [pltpu.VMEM((tm, tn), F32)] if nk > 1 else []
    return pl.pallas_call(
        body, name=name, grid=(m // tm, n // tn, nk),
        out_shape=jax.ShapeDtypeStruct((m, n), out_dtype),
        in_specs=in_specs, out_specs=o_spec, scratch_shapes=scratch,
        compiler_params=_params(("parallel", "parallel", "arbitrary"), blk + _nbytes((tm, tn), F32)),
    )(*args)


def _rms_fwd(x, g, name):
    t, d = x.shape
    tr = _pick(t, ROW_TILE, BF16_SUBLANE)

    def body(x_ref, g_ref, y_ref):
        xv = x_ref[...]
        r = lax.rsqrt(jnp.mean(xv * xv, axis=-1, keepdims=True) + RMS_EPS)
        y_ref[...] = ((xv * r) * g_ref[...]).astype(BF16)

    return pl.pallas_call(
        body, name=name, grid=(t // tr,),
        out_shape=jax.ShapeDtypeStruct((t, d), BF16),
        in_specs=[pl.BlockSpec((tr, d), lambda i: (i, 0)), pl.BlockSpec((1, d), lambda i: (0, 0))],
        out_specs=pl.BlockSpec((tr, d), lambda i: (i, 0)),
        compiler_params=_params(("parallel",), 2 * _nbytes((tr, d), F32)),
    )(x, g.reshape(1, d))


def _rms_bwd(x, g, dy, dres, name):
    t, d = x.shape
    tr = _pick(t, ROW_TILE, BF16_SUBLANE)
    has_res = dres is not None

    def body(*refs):
        x_ref, g_ref, dy_ref = refs[:3]
        r_ref = refs[3] if has_res else None
        dx_ref, dxb_ref, dg_ref = refs[-3:]
        i = pl.program_id(0)
        xv = x_ref[...]
        r = lax.rsqrt(jnp.mean(xv * xv, axis=-1, keepdims=True) + RMS_EPS)
        xh = xv * r
        dyv = dy_ref[...].astype(F32)
        dxh = dyv * g_ref[...]
        dx = r * (dxh - xh * jnp.mean(dxh * xh, axis=-1, keepdims=True))
        if has_res:
            dx = dx + r_ref[...]
        dx_ref[...] = dx
        dxb_ref[...] = dx.astype(BF16)

        @pl.when(i == 0)
        def _():
            dg_ref[...] = jnp.zeros_like(dg_ref)

        dg_ref[...] += jnp.sum(dyv * xh, axis=0, keepdims=True)

    row = pl.BlockSpec((tr, d), lambda i: (i, 0))
    vec = pl.BlockSpec((1, d), lambda i: (0, 0))
    in_specs, args = [row, vec, row], [x, g.reshape(1, d), dy]
    if has_res:
        in_specs.append(row)
        args.append(dres)
    return pl.pallas_call(
        body, name=name, grid=(t // tr,),
        out_shape=(jax.ShapeDtypeStruct((t, d), F32), jax.ShapeDtypeStruct((t, d), BF16),
                   jax.ShapeDtypeStruct((1, d), F32)),
        in_specs=in_specs, out_specs=(row, row, vec),
        compiler_params=_params(("arbitrary",), 5 * _nbytes((tr, d), F32)),
    )(*args)


def _loss_head(x, g, target, name):
    t, d = x.shape
    tr = _pick(t, ROW_TILE, BF16_SUBLANE)

    def body(x_ref, g_ref, t_ref, loss_ref, dx_ref, dxb_ref, dg_ref):
        i = pl.program_id(0)
        xv = x_ref[...]
        gv = g_ref[...]
        r = lax.rsqrt(jnp.mean(xv * xv, axis=-1, keepdims=True) + RMS_EPS)
        xh = xv * r
        diff = xh * gv - t_ref[...]
        part = 0.5 * jnp.sum(jnp.mean(diff * diff, axis=-1, keepdims=True), axis=0, keepdims=True)
        dyv = diff * (1.0 / d)
        dxh = dyv * gv
        dx = r * (dxh - xh * jnp.mean(dxh * xh, axis=-1, keepdims=True))
        dx_ref[...] = dx
        dxb_ref[...] = dx.astype(BF16)

        @pl.when(i == 0)
        def _():
            dg_ref[...] = jnp.zeros_like(dg_ref)
            loss_ref[...] = jnp.zeros_like(loss_ref)

        dg_ref[...] += jnp.sum(dyv * xh, axis=0, keepdims=True)
        loss_ref[...] += jnp.broadcast_to(part, loss_ref.shape)

    row = pl.BlockSpec((tr, d), lambda i: (i, 0))
    vec = pl.BlockSpec((1, d), lambda i: (0, 0))
    return pl.pallas_call(
        body, name=name, grid=(t // tr,),
        out_shape=(jax.ShapeDtypeStruct((1, LANE), F32), jax.ShapeDtypeStruct((t, d), F32),
                   jax.ShapeDtypeStruct((t, d), BF16), jax.ShapeDtypeStruct((1, d), F32)),
        in_specs=[row, vec, row],
        out_specs=(pl.BlockSpec((1, LANE), lambda i: (0, 0)), row, row, vec),
        compiler_params=_params(("arbitrary",), 5 * _nbytes((tr, d), F32)),
    )(x, g.reshape(1, d), target)


HALO = 8


def _conv_fwd(proj, conv_w, name):
    t = proj.shape[0]
    tr = _pick(t, ROW_TILE, BF16_SUBLANE)
    c0 = COL_CONV // BRANCH
    hb = tr // HALO

    def body(cb_ref, cc_ref, cu_ref, hc_ref, hu_ref, w_ref, y_ref):
        i = pl.program_id(0)
        z = cc_ref[...] * cu_ref[...]
        hz = jnp.where(i > 0, hc_ref[...] * hu_ref[...], 0.0)
        zf = jnp.concatenate([hz, z], axis=0)
        z1 = pltpu.roll(zf, 1, 0)[HALO:]
        z2 = pltpu.roll(zf, 2, 0)[HALO:]
        y = w_ref[2:3, :] * z + w_ref[1:2, :] * z1 + w_ref[0:1, :] * z2
        y_ref[...] = (cb_ref[...] * y).astype(BF16)

    def col(c):
        return pl.BlockSpec((tr, BRANCH), lambda i, c=c: (i, c0 + c))

    def prev(c):
        return pl.BlockSpec((HALO, BRANCH), lambda i, c=c: (jnp.maximum(i * hb - 1, 0), c0 + c))

    return pl.pallas_call(
        body, name=name, grid=(t // tr,),
        out_shape=jax.ShapeDtypeStruct((t, BRANCH), BF16),
        in_specs=[col(0), col(1), col(2), prev(1), prev(2), pl.BlockSpec((3, BRANCH), lambda i: (0, 0))],
        out_specs=pl.BlockSpec((tr, BRANCH), lambda i: (i, 0)),
        compiler_params=_params(("parallel",), 6 * _nbytes((tr, BRANCH), F32)),
    )(proj, proj, proj, proj, proj, conv_w)


def _conv_bwd(proj, conv_w, dout, name):
    t = proj.shape[0]
    tr = _pick(t, ROW_TILE, BF16_SUBLANE)
    nblk = t // tr
    c0 = COL_CONV // BRANCH
    hb = tr // HALO
    last_halo = t // HALO - 1

    def body(cb_ref, cc_ref, cu_ref, hc_ref, hu_ref, do_ref, ndo_ref, ncb_ref, w_ref,
             dcb_ref, dcc_ref, dcu_ref, dw_ref):
        i = pl.program_id(0)
        cb, cc, cu = cb_ref[...], cc_ref[...], cu_ref[...]
        w0, w1, w2 = w_ref[0:1, :], w_ref[1:2, :], w_ref[2:3, :]
        z = cc * cu
        hz = jnp.where(i > 0, hc_ref[...] * hu_ref[...], 0.0)
        zf = jnp.concatenate([hz, z], axis=0)
        z1 = pltpu.roll(zf, 1, 0)[HALO:]
        z2 = pltpu.roll(zf, 2, 0)[HALO:]
        y = w2 * z + w1 * z1 + w0 * z2
        dout_v = do_ref[...]
        dyc = dout_v * cb
        hdy = jnp.where(i < nblk - 1, ndo_ref[...] * ncb_ref[...], 0.0)
        dyf = jnp.concatenate([dyc, hdy], axis=0)
        dy1 = pltpu.roll(dyf, tr + HALO - 1, 0)[:tr]
        dy2 = pltpu.roll(dyf, tr + HALO - 2, 0)[:tr]
        dz = w2 * dyc + w1 * dy1 + w0 * dy2
        dcb_ref[...] = (dout_v * y).astype(BF16)
        dcc_ref[...] = (dz * cu).astype(BF16)
        dcu_ref[...] = (dz * cc).astype(BF16)

        @pl.when(i == 0)
        def _():
            dw_ref[...] = jnp.zeros_like(dw_ref)

        dw_ref[0:1, :] += jnp.sum(dyc * z2, axis=0, keepdims=True)
        dw_ref[1:2, :] += jnp.sum(dyc * z1, axis=0, keepdims=True)
        dw_ref[2:3, :] += jnp.sum(dyc * z, axis=0, keepdims=True)

    def col(c):
        return pl.BlockSpec((tr, BRANCH), lambda i, c=c: (i, c0 + c))

    def prev(c):
        return pl.BlockSpec((HALO, BRANCH), lambda i, c=c: (jnp.maximum(i * hb - 1, 0), c0 + c))

    def nxt(c):
        return pl.BlockSpec((HALO, BRANCH), lambda i, c=c: (jnp.minimum((i + 1) * hb, last_halo), c))

    row = pl.BlockSpec((tr, BRANCH), lambda i: (i, 0))
    return pl.pallas_call(
        body, name=name, grid=(nblk,),
        out_shape=(jax.ShapeDtypeStruct((t, BRANCH), BF16),) * 3 + (jax.ShapeDtypeStruct((HALO, BRANCH), F32),),
        in_specs=[col(0), col(1), col(2), prev(1), prev(2), row, nxt(0), nxt(c0),
                  pl.BlockSpec((3, BRANCH), lambda i: (0, 0))],
        out_specs=(row, row, row, pl.BlockSpec((HALO, BRANCH), lambda i: (0, 0))),
        compiler_params=_params(("arbitrary",), 8 * _nbytes((tr, BRANCH), F32)),
    )(proj, proj, proj, proj, proj, dout, dout, proj, conv_w)


def _tri(lower):
    r = lax.broadcasted_iota(jnp.int32, (LANE, LANE), 0)
    c = lax.broadcasted_iota(jnp.int32, (LANE, LANE), 1)
    return jnp.where((c <= r) if lower else (c >= r), 1.0, 0.0).astype(F32)


def _logf_cumsum(proj, fbias_row, name):
    t = proj.shape[0]
    nchunk = t // LANE

    def body(f_ref, b_ref, c_ref, run_sc):
        tri = _tri(True)
        run_sc[...] = jnp.zeros_like(run_sc)

        @pl.loop(0, nchunk)
        def _(i):
            rows = pl.ds(pl.multiple_of(i * LANE, LANE), LANE)
            z = f_ref[rows, :] + b_ref[...]
            logf = jnp.minimum(z, 0.0) - jnp.log(1.0 + jnp.exp(-jnp.abs(z)))
            cs = lax.dot_general(tri, logf, (NN, ((), ())), precision=lax.Precision.HIGHEST,
                                 preferred_element_type=F32) + run_sc[0:1, :]
            c_ref[rows, :] = cs
            run_sc[0:1, :] = cs[LANE - 1:LANE, :]

    return pl.pallas_call(
        body, name=name, grid=(1,),
        out_shape=jax.ShapeDtypeStruct((t, LANE), F32),
        in_specs=[pl.BlockSpec((t, LANE), lambda i: (0, COL_FG // LANE)), pl.BlockSpec((1, LANE), lambda i: (0, 0))],
        out_specs=pl.BlockSpec((t, LANE), lambda i: (0, 0)),
        scratch_shapes=[pltpu.VMEM((8, LANE), F32)],
        compiler_params=_params(("arbitrary",), 2 * _nbytes((t, LANE), F32)),
    )(proj, fbias_row)


def _logf_cumsum_bwd(proj, fbias_row, pieces, name):
    t = proj.shape[0]
    tb = _pick(t, 2 * ROW_TILE, LANE)
    nblk = t // tb
    npiece = len(pieces)

    def body(*refs):
        f_ref, b_ref = refs[:2]
        piece_refs = refs[2:2 + npiece]
        df_ref, db_ref, run_sc = refs[2 + npiece:]
        i = pl.program_id(0)
        tri = _tri(False)

        @pl.when(i == 0)
        def _():
            run_sc[...] = jnp.zeros_like(run_sc)
            db_ref[...] = jnp.zeros_like(db_ref)

        for c in reversed(range(tb // LANE)):
            rows = slice(c * LANE, (c + 1) * LANE)
            slabs = [p_ref[n, rows, :] for p_ref in piece_refs for n in range(p_ref.shape[0])]
            dcc = slabs[0]
            for slab in slabs[1:]:
                dcc = dcc + slab
            ss = lax.dot_general(tri, dcc, (NN, ((), ())), precision=lax.Precision.HIGHEST,
                                 preferred_element_type=F32) + run_sc[0:1, :]
            z = f_ref[rows, :] + b_ref[...]
            dz = ss * (1.0 / (1.0 + jnp.exp(z)))
            df_ref[rows, :] = dz.astype(BF16)
            run_sc[0:1, :] = ss[0:1, :]
            db_ref[...] += jnp.sum(dz, axis=0, keepdims=True)

    piece_specs = [pl.BlockSpec((p.shape[0], tb, LANE), lambda i: (0, nblk - 1 - i, 0)) for p in pieces]
    nslab = sum(p.shape[0] for p in pieces)
    return pl.pallas_call(
        body, name=name, grid=(nblk,),
        out_shape=(jax.ShapeDtypeStruct((t, LANE), BF16), jax.ShapeDtypeStruct((1, LANE), F32)),
        in_specs=[pl.BlockSpec((tb, LANE), lambda i: (nblk - 1 - i, COL_FG // LANE)),
                  pl.BlockSpec((1, LANE), lambda i: (0, 0))] + piece_specs,
        out_specs=(pl.BlockSpec((tb, LANE), lambda i: (nblk - 1 - i, 0)), pl.BlockSpec((1, LANE), lambda i: (0, 0))),
        scratch_shapes=[pltpu.VMEM((8, LANE), F32)],
        compiler_params=_params(("arbitrary",), (4 + nslab) * _nbytes((tb, LANE), F32)),
    )(proj, fbias_row, *pieces)


def _lo_mask():
    return lax.broadcasted_iota(jnp.int32, (1, LANE), 1) < HEAD


def _causal_steps(n, key_major):
    if key_major:
        pairs = [(iq, ik) for ik in range(n) for iq in range(ik, n)]
    else:
        pairs = [(iq, ik) for iq in range(n) for ik in range(iq + 1)]
    return (jnp.asarray([p[0] for p in pairs], jnp.int32), jnp.asarray([p[1] for p in pairs], jnp.int32))


def _head_lanes(j, pair_vals):
    lane = lax.broadcasted_iota(jnp.int32, (1, LANE), 1)
    return jnp.where(lane == 2 * j, pair_vals[0], 0.0) + jnp.where(lane == 2 * j + 1, pair_vals[1], 0.0)


def _fox_fwd(proj, c_col, c_row, name):
    t = proj.shape[0]
    tq = _pick(t, FOX_TILE, LANE)
    nq = t // tq
    rep = tq // LANE
    scale = HEAD ** -0.5
    cq, ck, cv = COL_FOX // LANE, COL_FOX // LANE + 4, COL_FOX // LANE + 8
    q_tab, k_tab = _causal_steps(nq, False)

    def body(qt_ref, kt_ref, q_ref, k_ref, v_ref, ck_ref, cqr_ref, y_ref, lse_ref, m_sc, l_sc, acc_sc):
        step_id = pl.program_id(1)
        iq, ik = qt_ref[step_id], kt_ref[step_id]
        lo = _lo_mask()
        lo_rows = lax.broadcasted_iota(jnp.int32, (LANE, 1), 0) < HEAD

        @pl.when(ik == 0)
        def _():
            m_sc[...] = jnp.full(m_sc.shape, NEG, F32)
            l_sc[...] = jnp.zeros_like(l_sc)
            acc_sc[...] = jnp.zeros_like(acc_sc)

        def step(diag):
            q2 = (q_ref[...] * scale).astype(BF16)
            k2 = k_ref[...].astype(BF16)
            v2 = v_ref[...].astype(BF16)
            alphas, adds = [], []
            for h in range(2):
                msk = lo if h == 0 else jnp.logical_not(lo)
                kh = jnp.where(msk, k2, jnp.zeros_like(k2))
                vh = jnp.where(msk, v2, jnp.zeros_like(v2))
                st = _dot(kh, q2, NT) + cqr_ref[h] - jnp.tile(ck_ref[h], (1, rep))
                if diag:
                    krow = lax.broadcasted_iota(jnp.int32, (tq, tq), 0)
                    qcol = lax.broadcasted_iota(jnp.int32, (tq, tq), 1)
                    st = jnp.where(krow <= qcol, st, NEG)
                m_prev = m_sc[h]
                m_new = jnp.maximum(m_prev, jnp.max(st, axis=0, keepdims=True))
                alpha = jnp.exp(m_prev - m_new)
                pt = jnp.exp(st - m_new)
                l_sc[h] = alpha * l_sc[h] + jnp.sum(pt, axis=0, keepdims=True)
                m_sc[h] = m_new
                alphas.append(alpha)
                adds.append(_dot(vh, pt.astype(BF16), TN))
            acc_sc[...] = acc_sc[...] * jnp.where(lo_rows, alphas[0], alphas[1]) + (adds[0] + adds[1])

        @pl.when(ik < iq)
        def _():
            step(False)

        @pl.when(ik == iq)
        def _():
            step(True)
            yt = acc_sc[...] / jnp.where(lo_rows, l_sc[0], l_sc[1])
            y_ref[...] = yt.T.astype(BF16)
            lse_ref[...] = m_sc[...] + jnp.log(l_sc[...])

    def kv(c):
        return pl.BlockSpec((tq, LANE), lambda j, s, qt, kt, c=c: (kt[s], c + j))

    qrow = pl.BlockSpec((2, 1, tq), lambda j, s, qt, kt: (j, 0, qt[s]))
    grid_spec = pltpu.PrefetchScalarGridSpec(
        num_scalar_prefetch=2, grid=(4, int(q_tab.shape[0])),
        in_specs=[pl.BlockSpec((tq, LANE), lambda j, s, qt, kt: (qt[s], cq + j)), kv(ck), kv(cv),
                  pl.BlockSpec((2, tq, LANE), lambda j, s, qt, kt: (j, kt[s], 0)), qrow],
        out_specs=(pl.BlockSpec((tq, LANE), lambda j, s, qt, kt: (qt[s], j)), qrow),
        scratch_shapes=[pltpu.VMEM((2, 1, tq), F32), pltpu.VMEM((2, 1, tq), F32), pltpu.VMEM((LANE, tq), F32)])
    return pl.pallas_call(
        body, name=name, grid_spec=grid_spec,
        out_shape=(jax.ShapeDtypeStruct((t, BRANCH), BF16), jax.ShapeDtypeStruct((8, 1, t), F32)),
        compiler_params=_params(("parallel", "arbitrary"),
                                16 * _nbytes((tq, LANE), F32) + 6 * _nbytes((tq, tq), F32)),
    )(q_tab, k_tab, proj, proj, proj, c_col, c_row)


def _fox_bwd(proj, c_col, c_row, lse_row, y, dy, name):
    t = proj.shape[0]
    tb = _pick(t, FOX_TILE, LANE)
    nb = t // tb
    rep = tb // LANE
    scale = HEAD ** -0.5
    cq, ck, cv = COL_FOX // LANE, COL_FOX // LANE + 4, COL_FOX // LANE + 8
    q_tab, k_tab = _causal_steps(nb, True)
    nsteps = int(q_tab.shape[0])

    def body(qt_ref, kt_ref, k_ref, v_ref, q_ref, y_ref, dy_ref, ck_ref, cqr_ref, lser_ref,
             dq_ref, dk_ref, dv_ref, dck_ref, dcq_ref, dk_sc, dv_sc, dc_sc, dqt_sc, dcq_sc, d_sc):
        j, step_id = pl.program_id(0), pl.program_id(1)
        iq, ik = qt_ref[step_id], kt_ref[step_id]
        lo = _lo_mask()

        @pl.when(step_id == 0)
        def _():
            dqt_sc[...] = jnp.zeros_like(dqt_sc)
            dcq_sc[...] = jnp.zeros_like(dcq_sc)

        @pl.when(iq == ik)
        def _():
            dk_sc[...] = jnp.zeros_like(dk_sc)
            dv_sc[...] = jnp.zeros_like(dv_sc)
            dc_sc[...] = jnp.zeros_like(dc_sc)

        @pl.when(ik == 0)
        def _():
            prod = y_ref[...].astype(F32) * dy_ref[...].astype(F32)
            row = lax.broadcasted_iota(jnp.int32, (8, LANE), 0)
            sel = jnp.logical_or(jnp.logical_and(row == 0, lo), jnp.logical_and(row == 1, jnp.logical_not(lo)))
            d_sc[iq] = lax.dot_general(jnp.where(sel, 1.0, 0.0).astype(F32), prod, (NT, ((), ())),
                                       precision=lax.Precision.HIGHEST, preferred_element_type=F32)

        def step(diag):
            k2 = k_ref[...].astype(BF16)
            v2 = v_ref[...].astype(BF16)
            q2 = (q_ref[...] * scale).astype(BF16)
            do2 = dy_ref[...]
            d_rows = d_sc[iq]
            for h in range(2):
                msk = lo if h == 0 else jnp.logical_not(lo)
                kh = jnp.where(msk, k2, jnp.zeros_like(k2))
                vh = jnp.where(msk, v2, jnp.zeros_like(v2))
                st = _dot(kh, q2, NT) + (cqr_ref[h] - lser_ref[h]) - jnp.tile(ck_ref[h], (1, rep))
                if diag:
                    krow = lax.broadcasted_iota(jnp.int32, (tb, tb), 0)
                    qcol = lax.broadcasted_iota(jnp.int32, (tb, tb), 1)
                    st = jnp.where(krow <= qcol, st, NEG)
                pt = jnp.exp(st)
                dpt = _dot(vh, do2, NT)
                dst = pt * (dpt - d_rows[h:h + 1, :])
                dsb = dst.astype(BF16)
                dv_sc[h] += _dot(pt.astype(BF16), do2, NN)
                dk_sc[h] += _dot(dsb, q2, NN)
                dc_sc[h] -= jnp.sum(dst, axis=1, keepdims=True)
                dqt_sc[iq] += _dot(kh, dsb, TN)
                dcq_sc[h, iq] += jnp.sum(dst, axis=0, keepdims=True)

        @pl.when(iq > ik)
        def _():
            step(False)

        @pl.when(iq == ik)
        def _():
            step(True)

        @pl.when(iq == nb - 1)
        def _():
            dk_ref[...] = jnp.where(lo, dk_sc[0], dk_sc[1]).astype(BF16)
            dv_ref[...] = jnp.where(lo, dv_sc[0], dv_sc[1]).astype(BF16)
            dck_ref[...] = _head_lanes(j, dc_sc)

        @pl.when(step_id == nsteps - 1)
        def _():
            for i in range(nb):
                dq_ref[i * tb:(i + 1) * tb, :] = (dqt_sc[i].T * scale).astype(BF16)
                for h in range(2):
                    dcq_ref[h, :, i * tb:(i + 1) * tb] = dcq_sc[h, i]

    def kcol(c):
        return pl.BlockSpec((tb, LANE), lambda j, s, qt, kt, c=c: (kt[s], c + j))

    qrow = pl.BlockSpec((2, 1, tb), lambda j, s, qt, kt: (j, 0, qt[s]))
    pair_q = pl.BlockSpec((tb, LANE), lambda j, s, qt, kt: (qt[s], j))
    pair_k = pl.BlockSpec((tb, LANE), lambda j, s, qt, kt: (kt[s], j))
    grid_spec = pltpu.PrefetchScalarGridSpec(
        num_scalar_prefetch=2, grid=(4, nsteps),
        in_specs=[kcol(ck), kcol(cv), pl.BlockSpec((tb, LANE), lambda j, s, qt, kt: (qt[s], cq + j)), pair_q, pair_q,
                  pl.BlockSpec((2, tb, LANE), lambda j, s, qt, kt: (j, kt[s], 0)), qrow, qrow],
        out_specs=(pl.BlockSpec((t, LANE), lambda j, s, qt, kt: (0, j)), pair_k, pair_k,
                   pl.BlockSpec((None, tb, LANE), lambda j, s, qt, kt: (j, kt[s], 0)),
                   pl.BlockSpec((2, 1, t), lambda j, s, qt, kt: (j, 0, 0))),
        scratch_shapes=[pltpu.VMEM((2, tb, LANE), F32)] * 3
        + [pltpu.VMEM((nb, LANE, tb), F32), pltpu.VMEM((2, nb, 1, tb), F32), pltpu.VMEM((nb, 8, tb), F32)])
    return pl.pallas_call(
        body, name=name, grid_spec=grid_spec,
        out_shape=(jax.ShapeDtypeStruct((t, BRANCH), BF16), jax.ShapeDtypeStruct((t, BRANCH), BF16),
                   jax.ShapeDtypeStruct((t, BRANCH), BF16), jax.ShapeDtypeStruct((4, t, LANE), F32),
                   jax.ShapeDtypeStruct((8, 1, t), F32)),
        compiler_params=_params(("parallel", "arbitrary"),
                                24 * _nbytes((tb, LANE), F32) + 8 * _nbytes((tb, tb), F32)
                                + 2 * _nbytes((t, LANE), F32)),
    )(q_tab, k_tab, proj, proj, proj, y, dy, c_col, c_row, lse_row)


def _swa_tables(rel_bias):
    tq = np.arange(SWA_BLOCK)[:, None]
    sk = np.arange(2 * SWA_BLOCK)[None, :]
    dist = SWA_BLOCK + tq - sk
    inwin = (dist >= 0) & (dist < SWA_BLOCK)
    n = np.maximum(dist, 0)
    max_exact = N_BUCKETS // 2
    large = max_exact + (np.log(np.maximum(n, 1).astype(np.float32) / max_exact)
                         / math.log(SWA_BLOCK / max_exact) * (N_BUCKETS - max_exact)).astype(np.int32)
    bucket = np.where(n < max_exact, n, np.minimum(large, N_BUCKETS - 1))
    onehot = (bucket[..., None] == np.arange(N_BUCKETS)) & inwin[..., None]
    onehot = jnp.asarray(onehot.astype(np.float32))
    bias = jnp.einsum("tsb,bh->hts", onehot, rel_bias, precision=lax.Precision.HIGHEST)
    bias = jnp.where(jnp.asarray(inwin)[None], bias, NEG)
    return onehot, bias


def _swa_fwd(proj, bias, sink_rep, name):
    t = proj.shape[0]
    nb = t // SWA_BLOCK
    scale = HEAD ** -0.5
    csq, csk, csv = COL_SQ // 256, COL_SK // LANE, COL_SV // LANE

    def body(q_ref, kp_ref, kc_ref, vp_ref, vc_ref, b_ref, sk_ref, y_ref, lse_ref):
        kvh, n = pl.program_id(0), pl.program_id(1)
        lane = lax.broadcasted_iota(jnp.int32, (1, LANE), 1)
        lo = lane < HEAD
        kvm = jnp.logical_and(lane >= kvh * HEAD, lane < (kvh + 1) * HEAD)

        def both(prev_ref, cur_ref):
            band = jnp.concatenate([prev_ref[...], cur_ref[...]], axis=0)
            band = jnp.where(kvm, band, 0.0)
            return (band + pltpu.roll(band, HEAD, 1)).astype(BF16)

        kb, vb = both(kp_ref, kc_ref), both(vp_ref, vc_ref)
        col = lax.broadcasted_iota(jnp.int32, (SWA_BLOCK, 2 * SWA_BLOCK), 1)
        first = jnp.logical_and(n == 0, col < SWA_BLOCK)
        outs = []
        for g in range(SWA_GROUP):
            half = q_ref[:, (g // 2) * LANE:(g // 2 + 1) * LANE]
            hm = lo if g % 2 == 0 else jnp.logical_not(lo)
            qg = jnp.where(hm, half, 0.0).astype(BF16)
            s = _dot(qg, kb, NT) * scale + b_ref[g]
            s = jnp.where(first, NEG, s)
            snk = sk_ref[g:g + 1, :]
            m = jnp.maximum(jnp.max(s, axis=1, keepdims=True), snk)
            p = jnp.exp(s - jnp.tile(m, (1, 2)))
            denom = jnp.sum(p, axis=1, keepdims=True) + jnp.exp(snk - m)
            outs.append(_dot(p.astype(BF16), vb, NN) / denom)
            lse_ref[g] = m + jnp.log(denom)
        y_ref[:, 0:LANE] = jnp.where(lo, outs[0], outs[1]).astype(BF16)
        y_ref[:, LANE:2 * LANE] = jnp.where(lo, outs[2], outs[3]).astype(BF16)

    def blk(c, shift):
        return pl.BlockSpec((SWA_BLOCK, LANE), lambda kvh, n, c=c, s=shift: (jnp.maximum(n - s, 0), c))

    return pl.pallas_call(
        body, name=name, grid=(2, nb),
        out_shape=(jax.ShapeDtypeStruct((t, BRANCH), BF16), jax.ShapeDtypeStruct((8, t, LANE), F32)),
        in_specs=[pl.BlockSpec((SWA_BLOCK, 256), lambda kvh, n: (n, csq + kvh)),
                  blk(csk, 1), blk(csk, 0), blk(csv, 1), blk(csv, 0),
                  pl.BlockSpec((None, SWA_GROUP, SWA_BLOCK, 256), lambda kvh, n: (kvh, 0, 0, 0)),
                  pl.BlockSpec((None, SWA_GROUP, LANE), lambda kvh, n: (kvh, 0, 0))],
        out_specs=(pl.BlockSpec((SWA_BLOCK, 256), lambda kvh, n: (n, kvh)),
                   pl.BlockSpec((SWA_GROUP, SWA_BLOCK, LANE), lambda kvh, n: (kvh, n, 0))),
        compiler_params=_params(("parallel", "arbitrary"), 4 << 20),
    )(proj, proj, proj, proj, proj, bias.reshape(2, SWA_GROUP, SWA_BLOCK, 256), sink_rep)


def _swa_bwd(proj, bias, sink_rep, lse, y, dy, name):
    t = proj.shape[0]
    nb = t // SWA_BLOCK
    scale = HEAD ** -0.5
    csq, csk, csv = COL_SQ // 256, COL_SK // LANE, COL_SV // LANE

    def body(q_ref, kp_ref, kc_ref, vp_ref, vc_ref, b_ref, sk_ref, lse_ref, y_ref, dy_ref,
             dq_ref, dkp_ref, dvp_ref, db_ref, dsk_ref):
        kvh, n = pl.program_id(0), pl.program_id(1)
        lane = lax.broadcasted_iota(jnp.int32, (1, LANE), 1)
        lo = lane < HEAD
        kvm = jnp.logical_and(lane >= kvh * HEAD, lane < (kvh + 1) * HEAD)

        def both(prev_ref, cur_ref):
            band = jnp.concatenate([prev_ref[...], cur_ref[...]], axis=0)
            band = jnp.where(kvm, band, 0.0)
            return (band + pltpu.roll(band, HEAD, 1)).astype(BF16)

        kb, vb = both(kp_ref, kc_ref), both(vp_ref, vc_ref)
        col = lax.broadcasted_iota(jnp.int32, (SWA_BLOCK, 2 * SWA_BLOCK), 1)
        first = jnp.logical_and(n == 0, col < SWA_BLOCK)

        @pl.when(n == 0)
        def _():
            db_ref[...] = jnp.zeros_like(db_ref)
            dsk_ref[...] = jnp.zeros_like(dsk_ref)

        dk_full = jnp.zeros((2 * SWA_BLOCK, LANE), F32)
        dv_full = jnp.zeros((2 * SWA_BLOCK, LANE), F32)
        dqs = []
        for g in range(SWA_GROUP):
            sl = slice((g // 2) * LANE, (g // 2 + 1) * LANE)
            hm = lo if g % 2 == 0 else jnp.logical_not(lo)
            qg = jnp.where(hm, q_ref[:, sl], 0.0).astype(BF16)
            dog = jnp.where(hm, dy_ref[:, sl], jnp.zeros((SWA_BLOCK, LANE), BF16))
            dmat = jnp.where(hm, y_ref[:, sl].astype(F32) * dy_ref[:, sl].astype(F32), 0.0)
            dg = jnp.sum(dmat, axis=1, keepdims=True)
            s = _dot(qg, kb, NT) * scale + b_ref[g]
            s = jnp.where(first, NEG, s)
            lse_g = lse_ref[g]
            p = jnp.exp(s - jnp.tile(lse_g, (1, 2)))
            dp = _dot(dog, vb, NT)
            ds = p * (dp - dg)
            dsb = ds.astype(BF16)
            dqs.append(_dot(dsb, kb, NN) * scale)
            dk_full = dk_full + _dot(dsb, qg, TN)
            dv_full = dv_full + _dot(p.astype(BF16), dog, TN)
            db_ref[g] += ds
            psink = jnp.exp(sk_ref[g:g + 1, :] - lse_g)
            dsk_ref[g:g + 1, :] -= jnp.sum(psink * dg, axis=0, keepdims=True)
        dq_ref[:, 0:LANE] = jnp.where(lo, dqs[0], dqs[1]).astype(BF16)
        dq_ref[:, LANE:2 * LANE] = jnp.where(lo, dqs[2], dqs[3]).astype(BF16)
        dkp_ref[...] = jnp.where(kvm, (dk_full + pltpu.roll(dk_full, HEAD, 1)) * scale, 0.0)
        dvp_ref[...] = jnp.where(kvm, dv_full + pltpu.roll(dv_full, HEAD, 1), 0.0)

    def blk(c, shift):
        return pl.BlockSpec((SWA_BLOCK, LANE), lambda kvh, n, c=c, s=shift: (jnp.maximum(n - s, 0), c))

    qblk = pl.BlockSpec((SWA_BLOCK, 256), lambda kvh, n: (n, kvh))
    part = pl.BlockSpec((None, None, 2 * SWA_BLOCK, LANE), lambda kvh, n: (kvh, n, 0, 0))
    bspec = pl.BlockSpec((None, SWA_GROUP, SWA_BLOCK, 256), lambda kvh, n: (kvh, 0, 0, 0))
    sspec = pl.BlockSpec((None, SWA_GROUP, LANE), lambda kvh, n: (kvh, 0, 0))
    return pl.pallas_call(
        body, name=name, grid=(2, nb),
        out_shape=(jax.ShapeDtypeStruct((t, BRANCH), BF16),
                   jax.ShapeDtypeStruct((2, nb, 2 * SWA_BLOCK, LANE), F32),
                   jax.ShapeDtypeStruct((2, nb, 2 * SWA_BLOCK, LANE), F32),
                   jax.ShapeDtypeStruct((2, SWA_GROUP, SWA_BLOCK, 256), F32),
                   jax.ShapeDtypeStruct((2, SWA_GROUP, LANE), F32)),
        in_specs=[pl.BlockSpec((SWA_BLOCK, 256), lambda kvh, n: (n, csq + kvh)),
                  blk(csk, 1), blk(csk, 0), blk(csv, 1), blk(csv, 0), bspec, sspec,
                  pl.BlockSpec((SWA_GROUP, SWA_BLOCK, LANE), lambda kvh, n: (kvh, n, 0)), qblk, qblk],
        out_specs=(qblk, part, part, bspec, sspec),
        compiler_params=_params(("parallel", "arbitrary"), 6 << 20),
    )(proj, proj, proj, proj, proj, bias.reshape(2, SWA_GROUP, SWA_BLOCK, 256), sink_rep, lse, y, dy)


def _gate_fwd(proj, pb, name):
    t = proj.shape[0]
    tr = _pick(t, ROW_TILE // 2, BF16_SUBLANE)

    def body(g0, g1, g2, p0, p1, p2, o_ref):
        acc = jax.nn.sigmoid(g0[...]) * p0[...]
        acc = acc + jax.nn.sigmoid(g1[...]) * p1[...]
        acc = acc + jax.nn.sigmoid(g2[...]) * p2[...]
        o_ref[...] = acc.astype(BF16)

    row = pl.BlockSpec((tr, D_MODEL), lambda i: (i, 0))
    gates = [pl.BlockSpec((tr, D_MODEL), lambda i, b=b: (i, b)) for b in range(3)]
    return pl.pallas_call(
        body, name=name, grid=(t // tr,),
        out_shape=jax.ShapeDtypeStruct((t, D_MODEL), BF16),
        in_specs=gates + [row] * 3, out_specs=row,
        compiler_params=_params(("parallel",), 7 * _nbytes((tr, D_MODEL), F32)),
    )(proj, proj, proj, *pb)


def _gate_bwd(proj, pb, dmerged, name):
    t = proj.shape[0]
    tr = _pick(t, ROW_TILE // 2, BF16_SUBLANE)

    def body(g0, g1, g2, p0, p1, p2, dm_ref, dp0, dp1, dp2, dg_ref):
        dm = dm_ref[...]
        for b, (g_ref, p_ref, dp_ref) in enumerate(((g0, p0, dp0), (g1, p1, dp1), (g2, p2, dp2))):
            sg = jax.nn.sigmoid(g_ref[...])
            dp_ref[...] = (dm * sg).astype(BF16)
            dg_ref[:, b * D_MODEL:(b + 1) * D_MODEL] = (dm * p_ref[...] * sg * (1.0 - sg)).astype(BF16)

    row = pl.BlockSpec((tr, D_MODEL), lambda i: (i, 0))
    gates = [pl.BlockSpec((tr, D_MODEL), lambda i, b=b: (i, b)) for b in range(3)]
    return pl.pallas_call(
        body, name=name, grid=(t // tr,),
        out_shape=(jax.ShapeDtypeStruct((t, D_MODEL), BF16),) * 3 + (jax.ShapeDtypeStruct((t, 3 * D_MODEL), BF16),),
        in_specs=gates + [row] * 4,
        out_specs=(row, row, row, pl.BlockSpec((tr, 3 * D_MODEL), lambda i: (i, 0))),
        compiler_params=_params(("parallel",), 11 * _nbytes((tr, D_MODEL), F32)),
    )(proj, proj, proj, *pb, dmerged)


def _swiglu_fwd(ab, name):
    t = ab.shape[0]
    tr = _pick(t, ROW_TILE, BF16_SUBLANE)
    tc = D_FF_P // 2

    def body(a_ref, b_ref, o_ref):
        a = a_ref[...]
        o_ref[...] = (a * jax.nn.sigmoid(a) * b_ref[...]).astype(BF16)

    return pl.pallas_call(
        body, name=name, grid=(t // tr, 2),
        out_shape=jax.ShapeDtypeStruct((t, D_FF_P), BF16),
        in_specs=[pl.BlockSpec((tr, tc), lambda i, j: (i, j)), pl.BlockSpec((tr, tc), lambda i, j: (i, j + 2))],
        out_specs=pl.BlockSpec((tr, tc), lambda i, j: (i, j)),
        compiler_params=_params(("parallel", "parallel"), 3 * _nbytes((tr, tc), F32)),
    )(ab, ab)


def _swiglu_bwd(ab, dh, name):
    t = ab.shape[0]
    tr = _pick(t, ROW_TILE, BF16_SUBLANE)
    tc = D_FF_P // 2

    def body(a_ref, b_ref, dh_ref, o_ref):
        jj = pl.program_id(1)
        a, b, d = a_ref[...], b_ref[...], dh_ref[...]
        sg = jax.nn.sigmoid(a)
        da = d * b * (sg * (1.0 + a * (1.0 - sg)))
        db = d * (a * sg)
        o_ref[...] = jnp.where(jj < 2, da, db).astype(BF16)

    return pl.pallas_call(
        body, name=name, grid=(t // tr, 4),
        out_shape=jax.ShapeDtypeStruct((t, 2 * D_FF_P), BF16),
        in_specs=[pl.BlockSpec((tr, tc), lambda i, j: (i, j % 2)),
                  pl.BlockSpec((tr, tc), lambda i, j: (i, j % 2 + 2)),
                  pl.BlockSpec((tr, tc), lambda i, j: (i, j % 2))],
        out_specs=pl.BlockSpec((tr, tc), lambda i, j: (i, j)),
        compiler_params=_params(("parallel", "parallel"), 4 * _nbytes((tr, tc), F32)),
    )(ab, ab, dh)


def _xattn_fwd(q, kv, name):
    t = q.shape[0]
    tq = _pick(t, ROW_TILE, BF16_SUBLANE)
    mlen = kv.shape[0]
    scale = X_HEAD ** -0.5

    def body(q_ref, kv_ref, o_ref):
        for h in range(X_HEADS):
            sl = slice(h * X_HEAD, (h + 1) * X_HEAD)
            kh = kv_ref[:, sl]
            vh = kv_ref[:, D_MODEL + h * X_HEAD:D_MODEL + (h + 1) * X_HEAD]
            s = _dot(q_ref[:, sl], kh, NT) * scale
            p = jnp.exp(s - jnp.max(s, axis=1, keepdims=True))
            l = jnp.sum(p, axis=1, keepdims=True)
            o_ref[:, sl] = (_dot(p.astype(BF16), vh, NN) / l).astype(BF16)

    return pl.pallas_call(
        body, name=name, grid=(t // tq,),
        out_shape=jax.ShapeDtypeStruct((t, D_MODEL), BF16),
        in_specs=[pl.BlockSpec((tq, D_MODEL), lambda i: (i, 0)), pl.BlockSpec((mlen, 2 * D_MODEL), lambda i: (0, 0))],
        out_specs=pl.BlockSpec((tq, D_MODEL), lambda i: (i, 0)),
        compiler_params=_params(("parallel",), 4 * _nbytes((tq, D_MODEL), F32)),
    )(q, kv)


def _xattn_bwd(q, kv, do, name):
    t = q.shape[0]
    tq = _pick(t, ROW_TILE, BF16_SUBLANE)
    mlen = kv.shape[0]
    scale = X_HEAD ** -0.5

    def body(q_ref, kv_ref, do_ref, dq_ref, dkv_ref):
        i = pl.program_id(0)

        @pl.when(i == 0)
        def _():
            dkv_ref[...] = jnp.zeros_like(dkv_ref)

        for h in range(X_HEADS):
            sl = slice(h * X_HEAD, (h + 1) * X_HEAD)
            vsl = slice(D_MODEL + h * X_HEAD, D_MODEL + (h + 1) * X_HEAD)
            qh, kh, vh, doh = q_ref[:, sl], kv_ref[:, sl], kv_ref[:, vsl], do_ref[:, sl]
            s = _dot(qh, kh, NT) * scale
            p = jnp.exp(s - jnp.max(s, axis=1, keepdims=True))
            p = p / jnp.sum(p, axis=1, keepdims=True)
            dp = _dot(doh, vh, NT)
            ds = p * (dp - jnp.sum(p * dp, axis=1, keepdims=True))
            dsb = ds.astype(BF16)
            dq_ref[:, sl] = (_dot(dsb, kh, NN) * scale).astype(BF16)
            dkv_ref[:, sl] += _dot(dsb, qh, TN) * scale
            dkv_ref[:, vsl] += _dot(p.astype(BF16), doh, TN)

    row = pl.BlockSpec((tq, D_MODEL), lambda i: (i, 0))
    whole = pl.BlockSpec((mlen, 2 * D_MODEL), lambda i: (0, 0))
    return pl.pallas_call(
        body, name=name, grid=(t // tq,),
        out_shape=(jax.ShapeDtypeStruct((t, D_MODEL), BF16), jax.ShapeDtypeStruct((mlen, 2 * D_MODEL), F32)),
        in_specs=[row, whole, row], out_specs=(row, whole),
        compiler_params=_params(("arbitrary",), 6 * _nbytes((tq, D_MODEL), F32)),
    )(q, kv, do)


def _position():
    return lax.axis_index("x"), lax.axis_index("y"), lax.axis_index("c")


N_PEER = N_DEV - 1


def _all_gather(xs, name):
    n = len(xs)

    def body(*refs):
        x_refs, out_refs = refs[:n], refs[n:2 * n]
        send_sems, recv_sems, local_sems = refs[2 * n:]
        mx, my, mc = _position()
        me, sib = (mx, my, mc), (mx, my, 1 - mc)
        chips = [(1 - mx, my), (mx, 1 - my), (1 - mx, 1 - my)]

        def slot(i, p):
            return out_refs[i].at[4 * p[0] + 2 * p[1] + p[2]]

        def copy(i, k, block, to, src=None):
            return pltpu.make_async_remote_copy(
                src_ref=slot(i, block) if src is None else src, dst_ref=slot(i, block),
                send_sem=send_sems.at[i * N_PEER + k], recv_sem=recv_sems.at[i * N_PEER + k],
                device_id=to, device_id_type=MESH)

        mine = [pltpu.make_async_copy(x_refs[i], slot(i, me), local_sems.at[i]) for i in range(n)]
        for cp in mine:
            cp.start()
        first = [copy(i, 1 + j, me, (*chip, mc), src=x_refs[i]) for j, chip in enumerate(chips) for i in range(n)]
        first += [copy(i, 0, me, sib, src=x_refs[i]) for i in range(n)]
        for cp in first:
            cp.start()
        passed = []
        for j, chip in enumerate(chips):
            for i in range(n):
                copy(i, 1 + j, (*chip, mc), me).wait_recv()
                passed.append(copy(i, 4 + j, (*chip, mc), sib))
                passed[-1].start()
        for i in range(n):
            copy(i, 0, sib, me).wait_recv()
        for j, chip in enumerate(chips):
            for i in range(n):
                copy(i, 4 + j, (*chip, 1 - mc), me).wait_recv()
        for cp in first + passed:
            cp.wait_send()
        for cp in mine:
            cp.wait()

    return pl.pallas_call(
        body, name=name,
        out_shape=tuple(jax.ShapeDtypeStruct((N_DEV,) + x.shape, x.dtype) for x in xs),
        in_specs=[pl.BlockSpec(memory_space=pl.ANY)] * n, out_specs=(pl.BlockSpec(memory_space=pl.ANY),) * n,
        scratch_shapes=[pltpu.SemaphoreType.DMA((n * N_PEER,)), pltpu.SemaphoreType.DMA((n * N_PEER,)),
                        pltpu.SemaphoreType.DMA((n,))],
    )(*xs)


PEER_RELS = [(dx, dy, dc) for dx in (0, 1) for dy in (0, 1) for dc in (0, 1)][1:]


def _peer_copy(rel_k, i, src_refs, land_refs, send_sems, recv_sems, layer, gather, arriving):
    mx, my, mc = _position()
    me_idx = 4 * mx + 2 * my + mc
    p = tuple((1 - v) if f else v for f, v in zip(PEER_RELS[rel_k], (mx, my, mc)))
    p_idx = 4 * p[0] + 2 * p[1] + p[2]
    src_slot, dst_slot = (me_idx, p_idx) if arriving else (p_idx, me_idx)
    src = src_refs[i] if gather else src_refs[i].at[src_slot]
    dst = land_refs[i].at[dst_slot] if layer is None else land_refs[i].at[dst_slot, layer]
    return pltpu.make_async_remote_copy(
        src_ref=src, dst_ref=dst, send_sem=send_sems.at[i * N_PEER + rel_k], recv_sem=recv_sems.at[i * N_PEER + rel_k],
        device_id=p, device_id_type=MESH)


def _own_copy(i, src_refs, land_refs, sem, layer, gather):
    mx, my, mc = _position()
    me_idx = 4 * mx + 2 * my + mc
    src = src_refs[i] if gather else src_refs[i].at[me_idx]
    dst = land_refs[i].at[me_idx] if layer is None else land_refs[i].at[me_idx, layer]
    return pltpu.make_async_copy(src, dst, sem)


def _exchange(parts, lands, layer, name):
    n = len(parts)

    def body(*refs):
        g_refs, land_refs = refs[:n], refs[2 * n:3 * n]
        send_sems, recv_sems, local_sems = refs[3 * n:]
        args = (g_refs, land_refs, send_sems, recv_sems, layer, False)
        mine = [_own_copy(i, g_refs, land_refs, local_sems.at[i], layer, False) for i in range(n)]
        for cp in mine:
            cp.start()
        sends = [_peer_copy(k, i, *args, False) for i in range(n) for k in range(N_PEER)]
        for cp in sends:
            cp.start()
        for i in range(n):
            for k in range(N_PEER):
                _peer_copy(k, i, *args, True).wait_recv()
        for cp in sends:
            cp.wait_send()
        for cp in mine:
            cp.wait()

    return pl.pallas_call(
        body, name=name,
        out_shape=tuple(jax.ShapeDtypeStruct(l.shape, l.dtype) for l in lands),
        in_specs=[pl.BlockSpec(memory_space=pl.ANY)] * (2 * n), out_specs=(pl.BlockSpec(memory_space=pl.ANY),) * n,
        input_output_aliases={n + i: i for i in range(n)},
        scratch_shapes=[pltpu.SemaphoreType.DMA((n * N_PEER,)), pltpu.SemaphoreType.DMA((n * N_PEER,)),
                        pltpu.SemaphoreType.DMA((n,))],
    )(*parts, *lands)


HBM_SPEC = pl.BlockSpec(memory_space=pltpu.HBM)
SEM_SPEC = pl.BlockSpec(memory_space=pltpu.SEMAPHORE)
SIDE_EFFECT = pltpu.SideEffectType.DATAFLOW_SIDE_EFFECTING


def _own_slots(srcs, lands, layer, gather):
    mx, my, mc = _position()
    me_idx = 4 * mx + 2 * my + mc
    out = []
    for s, land in zip(srcs, lands):
        piece = s[None] if gather else lax.dynamic_index_in_dim(s, me_idx, 0, keepdims=True)
        if layer is None:
            start = (me_idx,) + (0,) * (land.ndim - 1)
        else:
            piece, start = piece[:, None], (me_idx, layer) + (0,) * (land.ndim - 2)
        out.append(lax.dynamic_update_slice(land, piece, start))
    return out


def _swap_start(srcs, lands, layer, gather, name):
    n = len(srcs)

    def body(*refs):
        src_refs, land_refs = refs[:n], refs[n:2 * n]
        send_sems, recv_sems = refs[2 * n], refs[2 * n + 1]
        token = refs[4 * n + 2]
        for i in range(n):
            for k in range(N_PEER):
                _peer_copy(k, i, src_refs, land_refs, send_sems, recv_sems, layer, gather, False).start()
        token[...] = jnp.zeros_like(token)

    hbm = [pltpu.with_memory_space_constraint(a, pltpu.HBM) for a in list(srcs) + list(lands)]
    out = pl.pallas_call(
        body, name=name,
        out_shape=(pltpu.SemaphoreType.DMA((n * N_PEER,)), pltpu.SemaphoreType.DMA((n * N_PEER,)))
        + tuple(pltpu.HBM(a.shape, a.dtype) for a in hbm) + (jax.ShapeDtypeStruct((8, LANE), F32),),
        in_specs=[HBM_SPEC] * (2 * n),
        out_specs=(SEM_SPEC, SEM_SPEC) + (HBM_SPEC,) * (2 * n) + (pl.BlockSpec(memory_space=pltpu.VMEM),),
        input_output_aliases={i: 2 + i for i in range(2 * n)},
        compiler_params=pltpu.CompilerParams(has_side_effects=SIDE_EFFECT),
    )(*hbm)
    return out[0], out[1], list(out[2:2 + n]), list(out[2 + n:2 + 2 * n]), out[2 + 2 * n]


def _swap_wait(send_sems, recv_sems, srcs, lands, after, layer, gather, name):
    n = len(srcs)

    def body(*refs):
        src_refs, land_refs = refs[:n], refs[n:2 * n]
        send_sems_ref, recv_sems_ref = refs[2 * n], refs[2 * n + 1]
        for i in range(n):
            for k in range(N_PEER):
                args = (src_refs, land_refs, send_sems_ref, recv_sems_ref, layer, gather)
                _peer_copy(k, i, *args, False).wait_send()
                _peer_copy(k, i, *args, True).wait_recv()

    out = pl.pallas_call(
        body, name=name,
        out_shape=tuple(pltpu.HBM(a.shape, a.dtype) for a in list(srcs) + list(lands)),
        in_specs=[HBM_SPEC] * (2 * n) + [SEM_SPEC, SEM_SPEC, pl.BlockSpec(memory_space=pl.ANY)],
        out_specs=(HBM_SPEC,) * (2 * n),
        input_output_aliases={i: i for i in range(2 * n)},
        compiler_params=pltpu.CompilerParams(has_side_effects=SIDE_EFFECT),
    )(*srcs, *lands, send_sems, recv_sems, after)
    return list(out[n:])


ADAMW_BLOCK_BYTES = 1 << 20


def _adamw(parts, w, m, v, name):
    r, l = w.shape
    tr = _pick(r, max(ADAMW_BLOCK_BYTES // (4 * l), BF16_SUBLANE), BF16_SUBLANE)
    c1 = 1.0 - ADAM_B1 ** ADAM_STEP
    c2 = 1.0 - ADAM_B2 ** ADAM_STEP

    def body(p_ref, w_ref, m_ref, v_ref, g_ref, d_ref, nm_ref, nv_ref):
        g = p_ref[0].astype(F32)
        for s in range(1, N_DEV):
            g = g + p_ref[s].astype(F32)
        nm = ADAM_B1 * m_ref[...] + (1.0 - ADAM_B1) * g
        nv = ADAM_B2 * v_ref[...] + (1.0 - ADAM_B2) * (g * g)
        m_hat = nm / c1
        v_hat = nv / c2
        g_ref[...] = g
        d_ref[...] = -ADAM_LR * (m_hat / (jnp.sqrt(v_hat) + ADAM_EPS) + ADAM_WD * w_ref[...])
        nm_ref[...] = nm
        nv_ref[...] = nv

    row = pl.BlockSpec((tr, l), lambda i: (i, 0))
    return pl.pallas_call(
        body, name=name, grid=(r // tr,),
        out_shape=(jax.ShapeDtypeStruct((r, l), F32),) * 4,
        in_specs=[pl.BlockSpec((N_DEV, tr, l), lambda i: (0, i, 0)), row, row, row],
        out_specs=(row,) * 4,
        compiler_params=_params(("parallel",), 12 * _nbytes((tr, l), F32)),
    )(parts, w, m, v)


MATRIX_WEIGHTS = (("w_in", 2), ("conv_w", 2), ("w_branch", 3), ("w_mix_out", 1), ("w_xq", 1), ("w_xkv", 2),
                  ("w_xo", 1), ("w_ffn_gate", 2), ("w_ffn_up", 2), ("w_ffn_down", 1))
SMALL_PARAMS = ("mix_norm_g", "xattn_norm_g", "mem_norm_g", "ffn_norm_g", "final_norm_g", "forget_bias", "sink",
                "rel_bias")


def _pack_small(pieces):
    flat = jnp.concatenate([p.astype(F32).reshape(-1) for p in pieces])
    total = -(-flat.shape[0] // (8 * LANE)) * (8 * LANE)
    return jnp.pad(flat, (0, total - flat.shape[0])).reshape(total // LANE, LANE)


def _rows(a):
    return a.reshape(-1, a.shape[-1])


def _to_full(gathered, axis):
    moved = jnp.moveaxis(gathered, 0, axis)
    shape = list(moved.shape)
    shape[axis:axis + 2] = [shape[axis] * shape[axis + 1]]
    return moved.reshape(shape)


def _to_blocks(full, axis):
    shape = list(full.shape)
    shape[axis:axis + 1] = [N_DEV, shape[axis] // N_DEV]
    return jnp.moveaxis(full.reshape(shape), axis, 0)


def _perm_in(w_in):
    pad = jnp.zeros((w_in.shape[0], PROJ_COLS - IN_COLS), w_in.dtype)
    return jnp.concatenate([w_in[:, 3848:6920], w_in[:, 0:3072], w_in[:, 3080:3848], w_in[:, 3072:3080], pad], axis=1)


def _unperm_in(dw):
    return jnp.concatenate([dw[:, 3072:6144], dw[:, 6912:6920], dw[:, 6144:6912], dw[:, 0:3072]], axis=1)


def _layer_fwd(l, x, mem, wt, sm):
    t = x.shape[0]
    tag = f"l{l}_"
    h = _rms_fwd(x, sm["mix_norm_g"][l], tag + "mix_norm")
    proj = _matmul(h, wt["w_in"][l], "nn", F32, tag + "in_proj")
    y_conv = _conv_fwd(proj, wt["conv_w"][l], tag + "conv")
    fbias_row = jnp.pad(sm["forget_bias"][l], (0, LANE - 8)).reshape(1, LANE)
    c = _logf_cumsum(proj, fbias_row, tag + "logf_cumsum")
    c8 = c[:, :8].T
    c_col = jnp.broadcast_to(c8[:, :, None], (8, t, LANE))
    c_row = c8.reshape(8, 1, t)
    y_fox, lse_fox = _fox_fwd(proj, c_col, c_row, tag + "fox")
    onehot, bias = sm["swa_tables"]
    sink_rep = jnp.broadcast_to(sm["sink"][l].reshape(2, SWA_GROUP, 1), (2, SWA_GROUP, LANE))
    y_swa, lse_swa = _swa_fwd(proj, bias, sink_rep, tag + "swa")
    ys = (y_conv, y_fox, y_swa)
    pb = tuple(_matmul(ys[b], wt["w_branch"][l][b], "nn", F32, tag + f"branch{b}") for b in range(3))
    merged = _gate_fwd(proj, pb, tag + "gate")
    x1 = _matmul(merged, wt["w_mix_out"][l], "nn", F32, tag + "mix_out", residual=x)
    xn2 = _rms_fwd(x1, sm["xattn_norm_g"][l], tag + "xattn_norm")
    q = _matmul(xn2, wt["w_xq"][l], "nn", BF16, tag + "xq")
    mem_n = _rms_fwd(mem, sm["mem_norm_g"][l], tag + "mem_norm")
    kv = _matmul(mem_n, wt["w_xkv"][l], "nn", BF16, tag + "xkv")
    o = _xattn_fwd(q, kv, tag + "xattn")
    x2 = _matmul(o, wt["w_xo"][l], "nn", F32, tag + "xo", residual=x1)
    xn3 = _rms_fwd(x2, sm["ffn_norm_g"][l], tag + "ffn_norm")
    ab = _matmul(xn3, wt["w_gu"][l], "nn", F32, tag + "ffn_gu")
    h1 = _swiglu_fwd(ab, tag + "swiglu")
    x3 = _matmul(h1, wt["w_ffn_down"][l], "nn", F32, tag + "ffn_down", residual=x2)
    saved = dict(x=x, h=h, proj=proj, fbias_row=fbias_row, c_col=c_col, c_row=c_row, ys=ys, lse_fox=lse_fox,
                 onehot=onehot, bias=bias, sink_rep=sink_rep, lse_swa=lse_swa, pb=pb, merged=merged, x1=x1,
                 xn2=xn2, q=q, mem_n=mem_n, kv=kv, o=o, x2=x2, xn3=xn3, ab=ab, h1=h1)
    return x3, saved


def _layer_bwd(l, dx3, dx3_b, mem, wt, sm, sv, mid_hook=None):
    t = dx3.shape[0]
    nb = t // SWA_BLOCK
    tag = f"l{l}_b_"
    gw, gs = {}, {}
    dh1 = _matmul(dx3_b, wt["w_ffn_down"][l], "nt", F32, tag + "d_h1")
    gw["w_ffn_down"] = _matmul(sv["h1"], dx3_b, "tn", F32, tag + "dw_down")
    dab = _swiglu_bwd(sv["ab"], dh1, tag + "swiglu")
    dxn3 = _matmul(dab, wt["w_gu"][l], "nt", F32, tag + "d_xn3")
    dw_gu = _matmul(sv["xn3"], dab, "tn", F32, tag + "dw_gu")
    gw["w_ffn_gate"], gw["w_ffn_up"] = dw_gu[:, :D_FF_P], dw_gu[:, D_FF_P:]
    dx2, dx2_b, gs["ffn_norm_g"] = _rms_bwd(sv["x2"], sm["ffn_norm_g"][l], dxn3, dx3, tag + "ffn_norm")
    do = _matmul(dx2_b, wt["w_xo"][l], "nt", BF16, tag + "d_o")
    gw["w_xo"] = _matmul(sv["o"], dx2_b, "tn", F32, tag + "dw_xo")
    dq, dkv = _xattn_bwd(sv["q"], sv["kv"], do, tag + "xattn")
    gw["w_xkv"] = _matmul(sv["mem_n"], dkv, "tn", F32, tag + "dw_xkv")
    dmem_n = _matmul(dkv, wt["w_xkv"][l], "nt", F32, tag + "d_memn")
    _, _, gs["mem_norm_g"] = _rms_bwd(mem, sm["mem_norm_g"][l], dmem_n, None, tag + "mem_norm")
    gw["w_xq"] = _matmul(sv["xn2"], dq, "tn", F32, tag + "dw_xq")
    dxn2 = _matmul(dq, wt["w_xq"][l], "nt", F32, tag + "d_xn2")
    dx1, dx1_b, gs["xattn_norm_g"] = _rms_bwd(sv["x1"], sm["xattn_norm_g"][l], dxn2, dx2, tag + "xattn_norm")
    dmerged = _matmul(dx1_b, wt["w_mix_out"][l], "nt", F32, tag + "d_merged")
    gw["w_mix_out"] = _matmul(sv["merged"], dx1_b, "tn", F32, tag + "dw_mix_out")
    dp0, dp1, dp2, dgate = _gate_bwd(sv["proj"], sv["pb"], dmerged, tag + "gate")
    dps = (dp0, dp1, dp2)
    dy_dtypes = (F32, BF16, BF16)
    dys = [_matmul(dps[b], wt["w_branch"][l][b], "nt", dy_dtypes[b], tag + f"d_y{b}") for b in range(3)]
    gw["w_branch"] = jnp.stack(
        [_matmul(sv["ys"][b], dps[b], "tn", F32, tag + f"dw_branch{b}") for b in range(3)])
    sink_rep = sv["sink_rep"] if mid_hook is None else sv["sink_rep"] + mid_hook(gw)
    dsq, dkp, dvp, dbias, dsink = _swa_bwd(sv["proj"], sv["bias"], sink_rep, sv["lse_swa"], sv["ys"][2],
                                           dys[2], tag + "swa")

    def band_add(part):
        tot = part[0] + part[1]
        cur = tot[:, SWA_BLOCK:, :]
        nxt = jnp.concatenate([tot[1:, :SWA_BLOCK, :], jnp.zeros((1, SWA_BLOCK, LANE), F32)], axis=0)
        return (cur + nxt).reshape(t, LANE).astype(BF16)

    dsk, dsv = band_add(dkp), band_add(dvp)
    gs["rel_bias_l"] = jnp.einsum("hts,tsb->bh", dbias.reshape(8, SWA_BLOCK, 2 * SWA_BLOCK), sv["onehot"],
                                  precision=lax.Precision.HIGHEST)
    gs["sink"] = dsink[:, :, 0].reshape(8)
    dfq, dfk, dfv, dck, dcq_row = _fox_bwd(sv["proj"], sv["c_col"], sv["c_row"], sv["lse_fox"], sv["ys"][1], dys[1],
                                           tag + "fox_bwd")
    dcq = jnp.pad(dcq_row.reshape(8, t).T, ((0, 0), (0, LANE - 8))).reshape(1, t, LANE)
    dfg, dfb = _logf_cumsum_bwd(sv["proj"], sv["fbias_row"], [dck, dcq], tag + "logf_cumsum")
    gs["forget_bias"] = dfb[0, :8]
    dcb, dcc, dcu, dconv = _conv_bwd(sv["proj"], wt["conv_w"][l], dys[0], tag + "conv")
    gw["conv_w"] = dconv[:3]
    dproj = jnp.concatenate([dgate, dcb, dcc, dcu, dfq, dfk, dfv, dsq, dsk, dsv, dfg], axis=1)
    dh = _matmul(dproj, wt["w_in"][l], "nt", F32, tag + "d_h")
    gw["w_in"] = _unperm_in(_matmul(sv["h"], dproj, "tn", F32, tag + "dw_in"))
    dx, dx_b, gs["mix_norm_g"] = _rms_bwd(sv["x"], sm["mix_norm_g"][l], dh, dx1, tag + "mix_norm")
    return dx, dx_b, gw, gs


def kernel(x, mem, mix_norm_g, w_in, forget_bias, conv_w, sink, w_branch, w_mix_out, rel_bias, xattn_norm_g, mem_norm_g, w_xq, w_xkv, w_xo, ffn_norm_g, w_ffn_gate, w_ffn_up, w_ffn_down, final_norm_g, loss_target, m_mix_norm_g, m_w_in, m_forget_bias, m_conv_w, m_sink, m_w_branch, m_w_mix_out, m_rel_bias, m_xattn_norm_g, m_mem_norm_g, m_w_xq, m_w_xkv, m_w_xo, m_ffn_norm_g, m_w_ffn_gate, m_w_ffn_up, m_w_ffn_down, m_final_norm_g, v_mix_norm_g, v_w_in, v_forget_bias, v_conv_w, v_sink, v_w_branch, v_w_mix_out, v_rel_bias, v_xattn_norm_g, v_mem_norm_g, v_w_xq, v_w_xkv, v_w_xo, v_ffn_norm_g, v_w_ffn_gate, v_w_ffn_up, v_w_ffn_down, v_final_norm_g):
    args = dict(locals())
    names = [n for n, _ in MATRIX_WEIGHTS] + list(SMALL_PARAMS)
    w = {n: args[n] for n in names}
    mo = {n: args["m_" + n] for n in names}
    vo = {n: args["v_" + n] for n in names}
    x2d, mem2d, tgt = x[0], mem[0], loss_target[0]

    wire = {n: (F32 if n == "conv_w" else BF16) for n, _ in MATRIX_WEIGHTS}
    wt = {n: [None] * DEPTH for n, _ in MATRIX_WEIGHTS}
    wt["w_gu"] = [None] * DEPTH

    ff_pad = FF_SHARD_P - FF_SHARD
    ff_axis = {"w_ffn_gate": 2, "w_ffn_up": 2, "w_ffn_down": 1}

    def pad_ffn(n, blocks):
        if n not in ff_axis:
            return blocks
        return jnp.pad(blocks, [(0, ff_pad if d == ff_axis[n] else 0) for d in range(blocks.ndim)])

    def unpad_ffn(n, blocks):
        return lax.slice_in_dim(blocks, 0, FF_SHARD, axis=ff_axis[n]) if n in ff_axis else blocks

    def place_weights(l, gathered):
        for (n, ax), g in zip(MATRIX_WEIGHTS, gathered):
            wt[n][l] = _to_full(pad_ffn(n, g), ax - 1)
        wt["w_in"][l] = _perm_in(wt["w_in"][l])
        wt["w_gu"][l] = jnp.concatenate([wt["w_ffn_gate"][l], wt["w_ffn_up"][l]], axis=1)

    shards = [[w[n][l].astype(wire[n]) for n, _ in MATRIX_WEIGHTS] for l in range(DEPTH)]
    place_weights(0, _all_gather(shards[0], "weights_gather_l0"))
    lands = _own_slots(shards[1], [lax.empty((N_DEV,) + s.shape, s.dtype) for s in shards[1]], None, True)
    w_send, w_recv, w_srcs, lands, token = _swap_start(shards[1], lands, None, True, "weights_gather_l1_start")
    sm = {n: w[n] for n in SMALL_PARAMS}
    sm["mix_norm_g"] = w["mix_norm_g"].at[0].add(token[0, 0])
    sm["swa_tables"] = _swa_tables(w["rel_bias"])

    saved = []
    xc = x2d
    for l in range(DEPTH):
        if l == 1:
            place_weights(1, _swap_wait(w_send, w_recv, w_srcs, lands, xc, None, True, "weights_gather_l1_wait"))
        xc, sv = _layer_fwd(l, xc, mem2d, wt, sm)
        saved.append(sv)
    loss_row, dx, dx_b, dg_final = _loss_head(xc, sm["final_norm_g"], tgt, "loss_head")
    loss = lax.psum(loss_row[0, 0], ("x", "y", "c"))

    late = ("w_in", "conv_w")
    early_w = [(n, ax) for n, ax in MATRIX_WEIGHTS if n not in late]
    late_w = [(n, ax) for n, ax in MATRIX_WEIGHTS if n in late]

    def grad_parts(gw, which):
        return [unpad_ffn(n, _to_blocks(gw[n], ax - 1)).astype(wire[n]) for n, ax in which]

    gw_all, gs_all = [None] * DEPTH, [None] * DEPTH
    dx, dx_b, gw_all[1], gs_all[1] = _layer_bwd(1, dx, dx_b, mem2d, wt, sm, saved[1])
    parts1 = grad_parts(gw_all[1], MATRIX_WEIGHTS)
    zones = _own_slots(parts1, [lax.empty((N_DEV, DEPTH) + p.shape[1:], p.dtype) for p in parts1], 1, False)
    g_send, g_recv, g_srcs, zones, token = _swap_start(parts1, zones, 1, False, "grads_exchange_l1_start")
    sm_b = dict(sm)
    sm_b["ffn_norm_g"] = sm["ffn_norm_g"].at[0].add(token[0, 0])
    mid = {}

    def mid_hook(gw):
        zone = dict(zip([n for n, _ in MATRIX_WEIGHTS],
                        _swap_wait(g_send, g_recv, g_srcs, zones, gw["w_mix_out"], 1, False, "grads_exchange_l1_wait")))
        parts0 = grad_parts(gw, early_w)
        early_zones = _own_slots(parts0, [zone[n] for n, _ in early_w], 0, False)
        mid["early"] = _swap_start(parts0, early_zones, 0, False, "grads_exchange_l0_early_start")
        mid["late_zones"] = [zone[n] for n, _ in late_w]
        return mid["early"][4][0, 0]

    dx, dx_b, gw_all[0], gs_all[0] = _layer_bwd(0, dx, dx_b, mem2d, wt, sm_b, saved[0], mid_hook)
    grad_x = dx[None]
    e_send, e_recv, e_srcs, e_zones, _ = mid["early"]
    recv_by_name = dict(zip([n for n, _ in early_w],
                            _swap_wait(e_send, e_recv, e_srcs, e_zones, dx, 0, False, "grads_exchange_l0_early_wait")))
    recv_by_name.update(zip([n for n, _ in late_w],
                            _exchange(grad_parts(gw_all[0], late_w), mid["late_zones"], 0, "grads_exchange_l0_late")))
    recv = [recv_by_name[n] for n, _ in MATRIX_WEIGHTS]

    outs = {}
    for (n, _), r in zip(MATRIX_WEIGHTS, recv):
        res = _adamw(r.reshape((N_DEV,) + _rows(w[n]).shape), _rows(w[n]), _rows(mo[n]), _rows(vo[n]), "adamw_" + n)
        outs[n] = [o.reshape(w[n].shape) for o in res]

    gsm = {n: jnp.stack([gs_all[l][n] for l in range(DEPTH)])
           for n in ("mix_norm_g", "xattn_norm_g", "mem_norm_g", "ffn_norm_g", "forget_bias", "sink")}
    gsm["final_norm_g"] = dg_final
    gsm["rel_bias"] = gs_all[0]["rel_bias_l"] + gs_all[1]["rel_bias_l"]
    (small_parts,) = _all_gather([_pack_small([gsm[n] for n in SMALL_PARAMS])], "small_grads_all_gather")
    outs_small = _adamw(small_parts, *[_pack_small([d[n] for n in SMALL_PARAMS]) for d in (w, mo, vo)], "adamw_small")
    for kind in range(4):
        flat, o = outs_small[kind].reshape(-1), 0
        for n in SMALL_PARAMS:
            sz = int(np.prod(w[n].shape))
            outs.setdefault(n, []).append(flat[o:o + sz].reshape(w[n].shape))
            o += sz

    order = ["mix_norm_g", "w_in", "forget_bias", "conv_w", "sink", "w_branch", "w_mix_out", "rel_bias",
             "xattn_norm_g", "mem_norm_g", "w_xq", "w_xkv", "w_xo", "ffn_norm_g", "w_ffn_gate", "w_ffn_up",
             "w_ffn_down", "final_norm_g"]
    result = [loss, grad_x]
    for kind in range(4):
        result += [outs[n][kind] for n in order]
    return tuple(result)
```

```python
import math

import numpy as np
import jax
import jax.numpy as jnp
from jax import lax
from jax.experimental import pallas as pl
from jax.experimental.pallas import tpu as pltpu

F32 = jnp.float32
BF16 = jnp.bfloat16
MESH = pl.DeviceIdType.MESH

LANE = 128
BF16_SUBLANE = 16
V7X_VMEM_REQUEST_CAP = 56 * 2 ** 20
VMEM_REQUEST_FLOOR = V7X_VMEM_REQUEST_CAP
N_DEV = 8

D_MODEL = 1024
DEPTH = 2
HEAD = 64
BRANCH = 512
SWA_BLOCK = 128
SWA_GROUP = 4
N_BUCKETS = 32
X_HEADS = 4
X_HEAD = 256
D_FF = 2816
FF_SHARD = D_FF // N_DEV
FF_SHARD_P = -(-FF_SHARD // LANE) * LANE
D_FF_P = N_DEV * FF_SHARD_P
RMS_EPS = 1e-6
NEG = -1e30
ADAM_LR, ADAM_B1, ADAM_B2, ADAM_EPS, ADAM_WD, ADAM_STEP = 0.001, 0.9, 0.999, 1e-08, 0.01, 10

IN_COLS = 6920
PROJ_COLS = 7040
COL_GATE, COL_CONV, COL_FOX, COL_SQ, COL_SK, COL_SV, COL_FG = 0, 3072, 4608, 6144, 6656, 6784, 6912

ROW_TILE = 512
FOX_TILE = 512
MM_TM, MM_TN, MM_TK = 1024, 1536, 1024


def _pick(n, cap, mult):
    best = None
    for d in range(mult, min(n, cap) + 1, mult):
        if n % d == 0:
            best = d
    return n if best is None else best


def _params(semantics, block_bytes):
    limit = int(min(max(2 * block_bytes + (8 << 20), VMEM_REQUEST_FLOOR), V7X_VMEM_REQUEST_CAP))
    return pltpu.CompilerParams(dimension_semantics=semantics, vmem_limit_bytes=limit)


def _nbytes(shape, dtype):
    return int(np.prod(shape)) * jnp.dtype(dtype).itemsize


def _dot(a, b, dims):
    return lax.dot_general(a, b, (dims, ((), ())), preferred_element_type=F32)


NN = ((1,), (0,))
NT = ((1,), (1,))
TN = ((0,), (0,))


def _matmul(a, b, mode, out_dtype, name, residual=None):
    if mode == "nn":
        (m, k), (k2, n) = a.shape, b.shape
    elif mode == "nt":
        (m, k), (n, k2) = a.shape, b.shape
    else:
        (k, m), (k2, n) = a.shape, b.shape
    assert k == k2, (name, a.shape, b.shape)
    tm, tn, tk = _pick(m, MM_TM, LANE), _pick(n, MM_TN, LANE), _pick(k, MM_TK, LANE)
    nk = k // tk
    dims = {"nn": NN, "nt": NT, "tn": TN}[mode]
    has_res = residual is not None

    def body(*refs):
        a_ref, b_ref = refs[0], refs[1]
        r_ref = refs[2] if has_res else None
        o_ref = refs[3] if has_res else refs[2]
        kk = pl.program_id(2)
        p = _dot(a_ref[...].astype(BF16), b_ref[...].astype(BF16), dims)
        if nk == 1:
            if has_res:
                p = p + r_ref[...]
            o_ref[...] = p.astype(out_dtype)
        else:
            acc_ref = refs[-1]

            @pl.when(kk == 0)
            def _():
                acc_ref[...] = p

            @pl.when(kk > 0)
            def _():
                acc_ref[...] += p

            @pl.when(kk == nk - 1)
            def _():
                res = acc_ref[...]
                if has_res:
                    res = res + r_ref[...]
                o_ref[...] = res.astype(out_dtype)

    if mode == "nn":
        a_spec = pl.BlockSpec((tm, tk), lambda i, j, kk: (i, kk))
        b_spec = pl.BlockSpec((tk, tn), lambda i, j, kk: (kk, j))
    elif mode == "nt":
        a_spec = pl.BlockSpec((tm, tk), lambda i, j, kk: (i, kk))
        b_spec = pl.BlockSpec((tn, tk), lambda i, j, kk: (j, kk))
    else:
        a_spec = pl.BlockSpec((tk, tm), lambda i, j, kk: (kk, i))
        b_spec = pl.BlockSpec((tk, tn), lambda i, j, kk: (kk, j))
    o_spec = pl.BlockSpec((tm, tn), lambda i, j, kk: (i, j))
    in_specs, args = [a_spec, b_spec], [a, b]
    if has_res:
        in_specs.append(o_spec)
        args.append(residual)
    blk = (_nbytes((tm, tk), a.dtype) + _nbytes((tk, tn), b.dtype) + _nbytes((tm, tn), out_dtype)
           + (_nbytes((tm, tn), F32) if has_res else 0))
    scratch = [pltpu.VMEM((tm, tn), F32)] if nk > 1 else []
    return pl.pallas_call(
        body, name=name, grid=(m // tm, n // tn, nk),
        out_shape=jax.ShapeDtypeStruct((m, n), out_dtype),
        in_specs=in_specs, out_specs=o_spec, scratch_shapes=scratch,
        compiler_params=_params(("parallel", "parallel", "arbitrary"), blk + _nbytes((tm, tn), F32)),
    )(*args)


def _rms_fwd(x, g, name):
    t, d = x.shape
    tr = _pick(t, ROW_TILE, BF16_SUBLANE)

    def body(x_ref, g_ref, y_ref):
        xv = x_ref[...]
        r = lax.rsqrt(jnp.mean(xv * xv, axis=-1, keepdims=True) + RMS_EPS)
        y_ref[...] = ((xv * r) * g_ref[...]).astype(BF16)

    return pl.pallas_call(
        body, name=name, grid=(t // tr,),
        out_shape=jax.ShapeDtypeStruct((t, d), BF16),
        in_specs=[pl.BlockSpec((tr, d), lambda i: (i, 0)), pl.BlockSpec((1, d), lambda i: (0, 0))],
        out_specs=pl.BlockSpec((tr, d), lambda i: (i, 0)),
        compiler_params=_params(("parallel",), 2 * _nbytes((tr, d), F32)),
    )(x, g.reshape(1, d))


def _rms_bwd(x, g, dy, dres, name):
    t, d = x.shape
    tr = _pick(t, ROW_TILE, BF16_SUBLANE)
    has_res = dres is not None

    def body(*refs):
        x_ref, g_ref, dy_ref = refs[:3]
        r_ref = refs[3] if has_res else None
        dx_ref, dxb_ref, dg_ref = refs[-3:]
        i = pl.program_id(0)
        xv = x_ref[...]
        r = lax.rsqrt(jnp.mean(xv * xv, axis=-1, keepdims=True) + RMS_EPS)
        xh = xv * r
        dyv = dy_ref[...].astype(F32)
        dxh = dyv * g_ref[...]
        dx = r * (dxh - xh * jnp.mean(dxh * xh, axis=-1, keepdims=True))
        if has_res:
            dx = dx + r_ref[...]
        dx_ref[...] = dx
        dxb_ref[...] = dx.astype(BF16)

        @pl.when(i == 0)
        def _():
            dg_ref[...] = jnp.zeros_like(dg_ref)

        dg_ref[...] += jnp.sum(dyv * xh, axis=0, keepdims=True)

    row = pl.BlockSpec((tr, d), lambda i: (i, 0))
    vec = pl.BlockSpec((1, d), lambda i: (0, 0))
    in_specs, args = [row, vec, row], [x, g.reshape(1, d), dy]
    if has_res:
        in_specs.append(row)
        args.append(dres)
    return pl.pallas_call(
        body, name=name, grid=(t // tr,),
        out_shape=(jax.ShapeDtypeStruct((t, d), F32), jax.ShapeDtypeStruct((t, d), BF16),
                   jax.ShapeDtypeStruct((1, d), F32)),
        in_specs=in_specs, out_specs=(row, row, vec),
        compiler_params=_params(("arbitrary",), 5 * _nbytes((tr, d), F32)),
    )(*args)


def _loss_head(x, g, target, name):
    t, d = x.shape
    tr = _pick(t, ROW_TILE, BF16_SUBLANE)

    def body(x_ref, g_ref, t_ref, loss_ref, dx_ref, dxb_ref, dg_ref):
        i = pl.program_id(0)
        xv = x_ref[...]
        gv = g_ref[...]
        r = lax.rsqrt(jnp.mean(xv * xv, axis=-1, keepdims=True) + RMS_EPS)
        xh = xv * r
        diff = xh * gv - t_ref[...]
        part = 0.5 * jnp.sum(jnp.mean(diff * diff, axis=-1, keepdims=True), axis=0, keepdims=True)
        dyv = diff * (1.0 / d)
        dxh = dyv * gv
        dx = r * (dxh - xh * jnp.mean(dxh * xh, axis=-1, keepdims=True))
        dx_ref[...] = dx
        dxb_ref[...] = dx.astype(BF16)

        @pl.when(i == 0)
        def _():
            dg_ref[...] = jnp.zeros_like(dg_ref)
            loss_ref[...] = jnp.zeros_like(loss_ref)

        dg_ref[...] += jnp.sum(dyv * xh, axis=0, keepdims=True)
        loss_ref[...] += jnp.broadcast_to(part, loss_ref.shape)

    row = pl.BlockSpec((tr, d), lambda i: (i, 0))
    vec = pl.BlockSpec((1, d), lambda i: (0, 0))
    return pl.pallas_call(
        body, name=name, grid=(t // tr,),
        out_shape=(jax.ShapeDtypeStruct((1, LANE), F32), jax.ShapeDtypeStruct((t, d), F32),
                   jax.ShapeDtypeStruct((t, d), BF16), jax.ShapeDtypeStruct((1, d), F32)),
        in_specs=[row, vec, row],
        out_specs=(pl.BlockSpec((1, LANE), lambda i: (0, 0)), row, row, vec),
        compiler_params=_params(("arbitrary",), 5 * _nbytes((tr, d), F32)),
    )(x, g.reshape(1, d), target)


HALO = 8


def _conv_fwd(proj, conv_w, name):
    t = proj.shape[0]
    tr = _pick(t, ROW_TILE, BF16_SUBLANE)
    c0 = COL_CONV // BRANCH
    hb = tr // HALO

    def body(cb_ref, cc_ref, cu_ref, hc_ref, hu_ref, w_ref, y_ref):
        i = pl.program_id(0)
        z = cc_ref[...] * cu_ref[...]
        hz = jnp.where(i > 0, hc_ref[...] * hu_ref[...], 0.0)
        zf = jnp.concatenate([hz, z], axis=0)
        z1 = pltpu.roll(zf, 1, 0)[HALO:]
        z2 = pltpu.roll(zf, 2, 0)[HALO:]
        y = w_ref[2:3, :] * z + w_ref[1:2, :] * z1 + w_ref[0:1, :] * z2
        y_ref[...] = (cb_ref[...] * y).astype(BF16)

    def col(c):
        return pl.BlockSpec((tr, BRANCH), lambda i, c=c: (i, c0 + c))

    def prev(c):
        return pl.BlockSpec((HALO, BRANCH), lambda i, c=c: (jnp.maximum(i * hb - 1, 0), c0 + c))

    return pl.pallas_call(
        body, name=name, grid=(t // tr,),
        out_shape=jax.ShapeDtypeStruct((t, BRANCH), BF16),
        in_specs=[col(0), col(1), col(2), prev(1), prev(2), pl.BlockSpec((3, BRANCH), lambda i: (0, 0))],
        out_specs=pl.BlockSpec((tr, BRANCH), lambda i: (i, 0)),
        compiler_params=_params(("parallel",), 6 * _nbytes((tr, BRANCH), F32)),
    )(proj, proj, proj, proj, proj, conv_w)


def _conv_bwd(proj, conv_w, dout, name):
    t = proj.shape[0]
    tr = _pick(t, ROW_TILE, BF16_SUBLANE)
    nblk = t // tr
    c0 = COL_CONV // BRANCH
    hb = tr // HALO
    last_halo = t // HALO - 1

    def body(cb_ref, cc_ref, cu_ref, hc_ref, hu_ref, do_ref, ndo_ref, ncb_ref, w_ref,
             dcb_ref, dcc_ref, dcu_ref, dw_ref):
        i = pl.program_id(0)
        cb, cc, cu = cb_ref[...], cc_ref[...], cu_ref[...]
        w0, w1, w2 = w_ref[0:1, :], w_ref[1:2, :], w_ref[2:3, :]
        z = cc * cu
        hz = jnp.where(i > 0, hc_ref[...] * hu_ref[...], 0.0)
        zf = jnp.concatenate([hz, z], axis=0)
        z1 = pltpu.roll(zf, 1, 0)[HALO:]
        z2 = pltpu.roll(zf, 2, 0)[HALO:]
        y = w2 * z + w1 * z1 + w0 * z2
        dout_v = do_ref[...]
        dyc = dout_v * cb
        hdy = jnp.where(i < nblk - 1, ndo_ref[...] * ncb_ref[...], 0.0)
        dyf = jnp.concatenate([dyc, hdy], axis=0)
        dy1 = pltpu.roll(dyf, tr + HALO - 1, 0)[:tr]
        dy2 = pltpu.roll(dyf, tr + HALO - 2, 0)[:tr]
        dz = w2 * dyc + w1 * dy1 + w0 * dy2
        dcb_ref[...] = (dout_v * y).astype(BF16)
        dcc_ref[...] = (dz * cu).astype(BF16)
        dcu_ref[...] = (dz * cc).astype(BF16)

        @pl.when(i == 0)
        def _():
            dw_ref[...] = jnp.zeros_like(dw_ref)

        dw_ref[0:1, :] += jnp.sum(dyc * z2, axis=0, keepdims=True)
        dw_ref[1:2, :] += jnp.sum(dyc * z1, axis=0, keepdims=True)
        dw_ref[2:3, :] += jnp.sum(dyc * z, axis=0, keepdims=True)

    def col(c):
        return pl.BlockSpec((tr, BRANCH), lambda i, c=c: (i, c0 + c))

    def prev(c):
        return pl.BlockSpec((HALO, BRANCH), lambda i, c=c: (jnp.maximum(i * hb - 1, 0), c0 + c))

    def nxt(c):
        return pl.BlockSpec((HALO, BRANCH), lambda i, c=c: (jnp.minimum((i + 1) * hb, last_halo), c))

    row = pl.BlockSpec((tr, BRANCH), lambda i: (i, 0))
    return pl.pallas_call(
        body, name=name, grid=(nblk,),
        out_shape=(jax.ShapeDtypeStruct((t, BRANCH), BF16),) * 3 + (jax.ShapeDtypeStruct((HALO, BRANCH), F32),),
        in_specs=[col(0), col(1), col(2), prev(1), prev(2), row, nxt(0), nxt(c0),
                  pl.BlockSpec((3, BRANCH), lambda i: (0, 0))],
        out_specs=(row, row, row, pl.BlockSpec((HALO, BRANCH), lambda i: (0, 0))),
        compiler_params=_params(("arbitrary",), 8 * _nbytes((tr, BRANCH), F32)),
    )(proj, proj, proj, proj, proj, dout, dout, proj, conv_w)


def _tri(lower):
    r = lax.broadcasted_iota(jnp.int32, (LANE, LANE), 0)
    c = lax.broadcasted_iota(jnp.int32, (LANE, LANE), 1)
    return jnp.where((c <= r) if lower else (c >= r), 1.0, 0.0).astype(F32)


def _logf_cumsum(proj, fbias_row, name):
    t = proj.shape[0]
    nchunk = t // LANE

    def body(f_ref, b_ref, c_ref, run_sc):
        tri = _tri(True)
        run_sc[...] = jnp.zeros_like(run_sc)

        @pl.loop(0, nchunk)
        def _(i):
            rows = pl.ds(pl.multiple_of(i * LANE, LANE), LANE)
            z = f_ref[rows, :] + b_ref[...]
            logf = jnp.minimum(z, 0.0) - jnp.log(1.0 + jnp.exp(-jnp.abs(z)))
            cs = lax.dot_general(tri, logf, (NN, ((), ())), precision=lax.Precision.HIGHEST,
                                 preferred_element_type=F32) + run_sc[0:1, :]
            c_ref[rows, :] = cs
            run_sc[0:1, :] = cs[LANE - 1:LANE, :]

    return pl.pallas_call(
        body, name=name, grid=(1,),
        out_shape=jax.ShapeDtypeStruct((t, LANE), F32),
        in_specs=[pl.BlockSpec((t, LANE), lambda i: (0, COL_FG // LANE)), pl.BlockSpec((1, LANE), lambda i: (0, 0))],
        out_specs=pl.BlockSpec((t, LANE), lambda i: (0, 0)),
        scratch_shapes=[pltpu.VMEM((8, LANE), F32)],
        compiler_params=_params(("arbitrary",), 2 * _nbytes((t, LANE), F32)),
    )(proj, fbias_row)


def _logf_cumsum_bwd(proj, fbias_row, pieces, name):
    t = proj.shape[0]
    tb = _pick(t, 2 * ROW_TILE, LANE)
    nblk = t // tb
    npiece = len(pieces)

    def body(*refs):
        f_ref, b_ref = refs[:2]
        piece_refs = refs[2:2 + npiece]
        df_ref, db_ref, run_sc = refs[2 + npiece:]
        i = pl.program_id(0)
        tri = _tri(False)

        @pl.when(i == 0)
        def _():
            run_sc[...] = jnp.zeros_like(run_sc)
            db_ref[...] = jnp.zeros_like(db_ref)

        for c in reversed(range(tb // LANE)):
            rows = slice(c * LANE, (c + 1) * LANE)
            slabs = [p_ref[n, rows, :] for p_ref in piece_refs for n in range(p_ref.shape[0])]
            dcc = slabs[0]
            for slab in slabs[1:]:
                dcc = dcc + slab
            ss = lax.dot_general(tri, dcc, (NN, ((), ())), precision=lax.Precision.HIGHEST,
                                 preferred_element_type=F32) + run_sc[0:1, :]
            z = f_ref[rows, :] + b_ref[...]
            dz = ss * (1.0 / (1.0 + jnp.exp(z)))
            df_ref[rows, :] = dz.astype(BF16)
            run_sc[0:1, :] = ss[0:1, :]
            db_ref[...] += jnp.sum(dz, axis=0, keepdims=True)

    piece_specs = [pl.BlockSpec((p.shape[0], tb, LANE), lambda i: (0, nblk - 1 - i, 0)) for p in pieces]
    nslab = sum(p.shape[0] for p in pieces)
    return pl.pallas_call(
        body, name=name, grid=(nblk,),
        out_shape=(jax.ShapeDtypeStruct((t, LANE), BF16), jax.ShapeDtypeStruct((1, LANE), F32)),
        in_specs=[pl.BlockSpec((tb, LANE), lambda i: (nblk - 1 - i, COL_FG // LANE)),
                  pl.BlockSpec((1, LANE), lambda i: (0, 0))] + piece_specs,
        out_specs=(pl.BlockSpec((tb, LANE), lambda i: (nblk - 1 - i, 0)), pl.BlockSpec((1, LANE), lambda i: (0, 0))),
        scratch_shapes=[pltpu.VMEM((8, LANE), F32)],
        compiler_params=_params(("arbitrary",), (4 + nslab) * _nbytes((tb, LANE), F32)),
    )(proj, fbias_row, *pieces)


def _lo_mask():
    return lax.broadcasted_iota(jnp.int32, (1, LANE), 1) < HEAD


def _causal_steps(n, key_major):
    if key_major:
        pairs = [(iq, ik) for ik in range(n) for iq in range(ik, n)]
    else:
        pairs = [(iq, ik) for iq in range(n) for ik in range(iq + 1)]
    return (jnp.asarray([p[0] for p in pairs], jnp.int32), jnp.asarray([p[1] for p in pairs], jnp.int32))


def _head_lanes(j, pair_vals):
    lane = lax.broadcasted_iota(jnp.int32, (1, LANE), 1)
    return jnp.where(lane == 2 * j, pair_vals[0], 0.0) + jnp.where(lane == 2 * j + 1, pair_vals[1], 0.0)


def _fox_fwd(proj, c_col, c_row, name):
    t = proj.shape[0]
    tq = _pick(t, FOX_TILE, LANE)
    nq = t // tq
    rep = tq // LANE
    scale = HEAD ** -0.5
    cq, ck, cv = COL_FOX // LANE, COL_FOX // LANE + 4, COL_FOX // LANE + 8
    q_tab, k_tab = _causal_steps(nq, False)

    def body(qt_ref, kt_ref, q_ref, k_ref, v_ref, ck_ref, cqr_ref, y_ref, lse_ref, m_sc, l_sc, acc_sc):
        step_id = pl.program_id(1)
        iq, ik = qt_ref[step_id], kt_ref[step_id]
        lo = _lo_mask()
        lo_rows = lax.broadcasted_iota(jnp.int32, (LANE, 1), 0) < HEAD

        @pl.when(ik == 0)
        def _():
            m_sc[...] = jnp.full(m_sc.shape, NEG, F32)
            l_sc[...] = jnp.zeros_like(l_sc)
            acc_sc[...] = jnp.zeros_like(acc_sc)

        def step(diag):
            q2 = (q_ref[...] * scale).astype(BF16)
            k2 = k_ref[...].astype(BF16)
            v2 = v_ref[...].astype(BF16)
            alphas, adds = [], []
            for h in range(2):
                msk = lo if h == 0 else jnp.logical_not(lo)
                kh = jnp.where(msk, k2, jnp.zeros_like(k2))
                vh = jnp.where(msk, v2, jnp.zeros_like(v2))
                st = _dot(kh, q2, NT) + cqr_ref[h] - jnp.tile(ck_ref[h], (1, rep))
                if diag:
                    krow = lax.broadcasted_iota(jnp.int32, (tq, tq), 0)
                    qcol = lax.broadcasted_iota(jnp.int32, (tq, tq), 1)
                    st = jnp.where(krow <= qcol, st, NEG)
                m_prev = m_sc[h]
                m_new = jnp.maximum(m_prev, jnp.max(st, axis=0, keepdims=True))
                alpha = jnp.exp(m_prev - m_new)
                pt = jnp.exp(st - m_new)
                l_sc[h] = alpha * l_sc[h] + jnp.sum(pt, axis=0, keepdims=True)
                m_sc[h] = m_new
                alphas.append(alpha)
                adds.append(_dot(vh, pt.astype(BF16), TN))
            acc_sc[...] = acc_sc[...] * jnp.where(lo_rows, alphas[0], alphas[1]) + (adds[0] + adds[1])

        @pl.when(ik < iq)
        def _():
            step(False)

        @pl.when(ik == iq)
        def _():
            step(True)
            yt = acc_sc[...] / jnp.where(lo_rows, l_sc[0], l_sc[1])
            y_ref[...] = yt.T.astype(BF16)
            lse_ref[...] = m_sc[...] + jnp.log(l_sc[...])

    def kv(c):
        return pl.BlockSpec((tq, LANE), lambda j, s, qt, kt, c=c: (kt[s], c + j))

    qrow = pl.BlockSpec((2, 1, tq), lambda j, s, qt, kt: (j, 0, qt[s]))
    grid_spec = pltpu.PrefetchScalarGridSpec(
        num_scalar_prefetch=2, grid=(4, int(q_tab.shape[0])),
        in_specs=[pl.BlockSpec((tq, LANE), lambda j, s, qt, kt: (qt[s], cq + j)), kv(ck), kv(cv),
                  pl.BlockSpec((2, tq, LANE), lambda j, s, qt, kt: (j, kt[s], 0)), qrow],
        out_specs=(pl.BlockSpec((tq, LANE), lambda j, s, qt, kt: (qt[s], j)), qrow),
        scratch_shapes=[pltpu.VMEM((2, 1, tq), F32), pltpu.VMEM((2, 1, tq), F32), pltpu.VMEM((LANE, tq), F32)])
    return pl.pallas_call(
        body, name=name, grid_spec=grid_spec,
        out_shape=(jax.ShapeDtypeStruct((t, BRANCH), BF16), jax.ShapeDtypeStruct((8, 1, t), F32)),
        compiler_params=_params(("parallel", "arbitrary"),
                                16 * _nbytes((tq, LANE), F32) + 6 * _nbytes((tq, tq), F32)),
    )(q_tab, k_tab, proj, proj, proj, c_col, c_row)


def _fox_bwd(proj, c_col, c_row, lse_row, y, dy, name):
    t = proj.shape[0]
    tb = _pick(t, FOX_TILE, LANE)
    nb = t // tb
    rep = tb // LANE
    scale = HEAD ** -0.5
    cq, ck, cv = COL_FOX // LANE, COL_FOX // LANE + 4, COL_FOX // LANE + 8
    q_tab, k_tab = _causal_steps(nb, True)
    nsteps = int(q_tab.shape[0])

    def body(qt_ref, kt_ref, k_ref, v_ref, q_ref, y_ref, dy_ref, ck_ref, cqr_ref, lser_ref,
             dq_ref, dk_ref, dv_ref, dck_ref, dcq_ref, dk_sc, dv_sc, dc_sc, dqt_sc, dcq_sc, d_sc):
        j, step_id = pl.program_id(0), pl.program_id(1)
        iq, ik = qt_ref[step_id], kt_ref[step_id]
        lo = _lo_mask()

        @pl.when(step_id == 0)
        def _():
            dqt_sc[...] = jnp.zeros_like(dqt_sc)
            dcq_sc[...] = jnp.zeros_like(dcq_sc)

        @pl.when(iq == ik)
        def _():
            dk_sc[...] = jnp.zeros_like(dk_sc)
            dv_sc[...] = jnp.zeros_like(dv_sc)
            dc_sc[...] = jnp.zeros_like(dc_sc)

        @pl.when(ik == 0)
        def _():
            prod = y_ref[...].astype(F32) * dy_ref[...].astype(F32)
            row = lax.broadcasted_iota(jnp.int32, (8, LANE), 0)
            sel = jnp.logical_or(jnp.logical_and(row == 0, lo), jnp.logical_and(row == 1, jnp.logical_not(lo)))
            d_sc[iq] = lax.dot_general(jnp.where(sel, 1.0, 0.0).astype(F32), prod, (NT, ((), ())),
                                       precision=lax.Precision.HIGHEST, preferred_element_type=F32)

        def step(diag):
            k2 = k_ref[...].astype(BF16)
            v2 = v_ref[...].astype(BF16)
            q2 = (q_ref[...] * scale).astype(BF16)
            do2 = dy_ref[...]
            d_rows = d_sc[iq]
            for h in range(2):
                msk = lo if h == 0 else jnp.logical_not(lo)
                kh = jnp.where(msk, k2, jnp.zeros_like(k2))
                vh = jnp.where(msk, v2, jnp.zeros_like(v2))
                st = _dot(kh, q2, NT) + (cqr_ref[h] - lser_ref[h]) - jnp.tile(ck_ref[h], (1, rep))
                if diag:
                    krow = lax.broadcasted_iota(jnp.int32, (tb, tb), 0)
                    qcol = lax.broadcasted_iota(jnp.int32, (tb, tb), 1)
                    st = jnp.where(krow <= qcol, st, NEG)
                pt = jnp.exp(st)
                dpt = _dot(vh, do2, NT)
                dst = pt * (dpt - d_rows[h:h + 1, :])
                dsb = dst.astype(BF16)
                dv_sc[h] += _dot(pt.astype(BF16), do2, NN)
                dk_sc[h] += _dot(dsb, q2, NN)
                dc_sc[h] -= jnp.sum(dst, axis=1, keepdims=True)
                dqt_sc[iq] += _dot(kh, dsb, TN)
                dcq_sc[h, iq] += jnp.sum(dst, axis=0, keepdims=True)

        @pl.when(iq > ik)
        def _():
            step(False)

        @pl.when(iq == ik)
        def _():
            step(True)

        @pl.when(iq == nb - 1)
        def _():
            dk_ref[...] = jnp.where(lo, dk_sc[0], dk_sc[1]).astype(BF16)
            dv_ref[...] = jnp.where(lo, dv_sc[0], dv_sc[1]).astype(BF16)
            dck_ref[...] = _head_lanes(j, dc_sc)

        @pl.when(step_id == nsteps - 1)
        def _():
            for i in range(nb):
                dq_ref[i * tb:(i + 1) * tb, :] = (dqt_sc[i].T * scale).astype(BF16)
                for h in range(2):
                    dcq_ref[h, :, i * tb:(i + 1) * tb] = dcq_sc[h, i]

    def kcol(c):
        return pl.BlockSpec((tb, LANE), lambda j, s, qt, kt, c=c: (kt[s], c + j))

    qrow = pl.BlockSpec((2, 1, tb), lambda j, s, qt, kt: (j, 0, qt[s]))
    pair_q = pl.BlockSpec((tb, LANE), lambda j, s, qt, kt: (qt[s], j))
    pair_k = pl.BlockSpec((tb, LANE), lambda j, s, qt, kt: (kt[s], j))
    grid_spec = pltpu.PrefetchScalarGridSpec(
        num_scalar_prefetch=2, grid=(4, nsteps),
        in_specs=[kcol(ck), kcol(cv), pl.BlockSpec((tb, LANE), lambda j, s, qt, kt: (qt[s], cq + j)), pair_q, pair_q,
                  pl.BlockSpec((2, tb, LANE), lambda j, s, qt, kt: (j, kt[s], 0)), qrow, qrow],
        out_specs=(pl.BlockSpec((t, LANE), lambda j, s, qt, kt: (0, j)), pair_k, pair_k,
                   pl.BlockSpec((None, tb, LANE), lambda j, s, qt, kt: (j, kt[s], 0)),
                   pl.BlockSpec((2, 1, t), lambda j, s, qt, kt: (j, 0, 0))),
        scratch_shapes=[pltpu.VMEM((2, tb, LANE), F32)] * 3
        + [pltpu.VMEM((nb, LANE, tb), F32), pltpu.VMEM((2, nb, 1, tb), F32), pltpu.VMEM((nb, 8, tb), F32)])
    return pl.pallas_call(
        body, name=name, grid_spec=grid_spec,
        out_shape=(jax.ShapeDtypeStruct((t, BRANCH), BF16), jax.ShapeDtypeStruct((t, BRANCH), BF16),
                   jax.ShapeDtypeStruct((t, BRANCH), BF16), jax.ShapeDtypeStruct((4, t, LANE), F32),
                   jax.ShapeDtypeStruct((8, 1, t), F32)),
        compiler_params=_params(("parallel", "arbitrary"),
                                24 * _nbytes((tb, LANE), F32) + 8 * _nbytes((tb, tb), F32)
                                + 2 * _nbytes((t, LANE), F32)),
    )(q_tab, k_tab, proj, proj, proj, y, dy, c_col, c_row, lse_row)


def _swa_tables(rel_bias):
    tq = np.arange(SWA_BLOCK)[:, None]
    sk = np.arange(2 * SWA_BLOCK)[None, :]
    dist = SWA_BLOCK + tq - sk
    inwin = (dist >= 0) & (dist < SWA_BLOCK)
    n = np.maximum(dist, 0)
    max_exact = N_BUCKETS // 2
    large = max_exact + (np.log(np.maximum(n, 1).astype(np.float32) / max_exact)
                         / math.log(SWA_BLOCK / max_exact) * (N_BUCKETS - max_exact)).astype(np.int32)
    bucket = np.where(n < max_exact, n, np.minimum(large, N_BUCKETS - 1))
    onehot = (bucket[..., None] == np.arange(N_BUCKETS)) & inwin[..., None]
    onehot = jnp.asarray(onehot.astype(np.float32))
    bias = jnp.einsum("tsb,bh->hts", onehot, rel_bias, precision=lax.Precision.HIGHEST)
    bias = jnp.where(jnp.asarray(inwin)[None], bias, NEG)
    return onehot, bias


def _swa_fwd(proj, bias, sink_rep, name):
    t = proj.shape[0]
    nb = t // SWA_BLOCK
    scale = HEAD ** -0.5
    csq, csk, csv = COL_SQ // 256, COL_SK // LANE, COL_SV // LANE

    def body(q_ref, kp_ref, kc_ref, vp_ref, vc_ref, b_ref, sk_ref, y_ref, lse_ref):
        kvh, n = pl.program_id(0), pl.program_id(1)
        lane = lax.broadcasted_iota(jnp.int32, (1, LANE), 1)
        lo = lane < HEAD
        kvm = jnp.logical_and(lane >= kvh * HEAD, lane < (kvh + 1) * HEAD)

        def both(prev_ref, cur_ref):
            band = jnp.concatenate([prev_ref[...], cur_ref[...]], axis=0)
            band = jnp.where(kvm, band, 0.0)
            return (band + pltpu.roll(band, HEAD, 1)).astype(BF16)

        kb, vb = both(kp_ref, kc_ref), both(vp_ref, vc_ref)
        col = lax.broadcasted_iota(jnp.int32, (SWA_BLOCK, 2 * SWA_BLOCK), 1)
        first = jnp.logical_and(n == 0, col < SWA_BLOCK)
        outs = []
        for g in range(SWA_GROUP):
            half = q_ref[:, (g // 2) * LANE:(g // 2 + 1) * LANE]
            hm = lo if g % 2 == 0 else jnp.logical_not(lo)
            qg = jnp.where(hm, half, 0.0).astype(BF16)
            s = _dot(qg, kb, NT) * scale + b_ref[g]
            s = jnp.where(first, NEG, s)
            snk = sk_ref[g:g + 1, :]
            m = jnp.maximum(jnp.max(s, axis=1, keepdims=True), snk)
            p = jnp.exp(s - jnp.tile(m, (1, 2)))
            denom = jnp.sum(p, axis=1, keepdims=True) + jnp.exp(snk - m)
            outs.append(_dot(p.astype(BF16), vb, NN) / denom)
            lse_ref[g] = m + jnp.log(denom)
        y_ref[:, 0:LANE] = jnp.where(lo, outs[0], outs[1]).astype(BF16)
        y_ref[:, LANE:2 * LANE] = jnp.where(lo, outs[2], outs[3]).astype(BF16)

    def blk(c, shift):
        return pl.BlockSpec((SWA_BLOCK, LANE), lambda kvh, n, c=c, s=shift: (jnp.maximum(n - s, 0), c))

    return pl.pallas_call(
        body, name=name, grid=(2, nb),
        out_shape=(jax.ShapeDtypeStruct((t, BRANCH), BF16), jax.ShapeDtypeStruct((8, t, LANE), F32)),
        in_specs=[pl.BlockSpec((SWA_BLOCK, 256), lambda kvh, n: (n, csq + kvh)),
                  blk(csk, 1), blk(csk, 0), blk(csv, 1), blk(csv, 0),
                  pl.BlockSpec((None, SWA_GROUP, SWA_BLOCK, 256), lambda kvh, n: (kvh, 0, 0, 0)),
                  pl.BlockSpec((None, SWA_GROUP, LANE), lambda kvh, n: (kvh, 0, 0))],
        out_specs=(pl.BlockSpec((SWA_BLOCK, 256), lambda kvh, n: (n, kvh)),
                   pl.BlockSpec((SWA_GROUP, SWA_BLOCK, LANE), lambda kvh, n: (kvh, n, 0))),
        compiler_params=_params(("parallel", "arbitrary"), 4 << 20),
    )(proj, proj, proj, proj, proj, bias.reshape(2, SWA_GROUP, SWA_BLOCK, 256), sink_rep)


def _swa_bwd(proj, bias, sink_rep, lse, y, dy, name):
    t = proj.shape[0]
    nb = t // SWA_BLOCK
    scale = HEAD ** -0.5
    csq, csk, csv = COL_SQ // 256, COL_SK // LANE, COL_SV // LANE

    def body(q_ref, kp_ref, kc_ref, vp_ref, vc_ref, b_ref, sk_ref, lse_ref, y_ref, dy_ref,
             dq_ref, dkp_ref, dvp_ref, db_ref, dsk_ref):
        kvh, n = pl.program_id(0), pl.program_id(1)
        lane = lax.broadcasted_iota(jnp.int32, (1, LANE), 1)
        lo = lane < HEAD
        kvm = jnp.logical_and(lane >= kvh * HEAD, lane < (kvh + 1) * HEAD)

        def both(prev_ref, cur_ref):
            band = jnp.concatenate([prev_ref[...], cur_ref[...]], axis=0)
            band = jnp.where(kvm, band, 0.0)
            return (band + pltpu.roll(band, HEAD, 1)).astype(BF16)

        kb, vb = both(kp_ref, kc_ref), both(vp_ref, vc_ref)
        col = lax.broadcasted_iota(jnp.int32, (SWA_BLOCK, 2 * SWA_BLOCK), 1)
        first = jnp.logical_and(n == 0, col < SWA_BLOCK)

        @pl.when(n == 0)
        def _():
            db_ref[...] = jnp.zeros_like(db_ref)
            dsk_ref[...] = jnp.zeros_like(dsk_ref)

        dk_full = jnp.zeros((2 * SWA_BLOCK, LANE), F32)
        dv_full = jnp.zeros((2 * SWA_BLOCK, LANE), F32)
        dqs = []
        for g in range(SWA_GROUP):
            sl = slice((g // 2) * LANE, (g // 2 + 1) * LANE)
            hm = lo if g % 2 == 0 else jnp.logical_not(lo)
            qg = jnp.where(hm, q_ref[:, sl], 0.0).astype(BF16)
            dog = jnp.where(hm, dy_ref[:, sl], jnp.zeros((SWA_BLOCK, LANE), BF16))
            dmat = jnp.where(hm, y_ref[:, sl].astype(F32) * dy_ref[:, sl].astype(F32), 0.0)
            dg = jnp.sum(dmat, axis=1, keepdims=True)
            s = _dot(qg, kb, NT) * scale + b_ref[g]
            s = jnp.where(first, NEG, s)
            lse_g = lse_ref[g]
            p = jnp.exp(s - jnp.tile(lse_g, (1, 2)))
            dp = _dot(dog, vb, NT)
            ds = p * (dp - dg)
            dsb = ds.astype(BF16)
            dqs.append(_dot(dsb, kb, NN) * scale)
            dk_full = dk_full + _dot(dsb, qg, TN)
            dv_full = dv_full + _dot(p.astype(BF16), dog, TN)
            db_ref[g] += ds
            psink = jnp.exp(sk_ref[g:g + 1, :] - lse_g)
            dsk_ref[g:g + 1, :] -= jnp.sum(psink * dg, axis=0, keepdims=True)
        dq_ref[:, 0:LANE] = jnp.where(lo, dqs[0], dqs[1]).astype(BF16)
        dq_ref[:, LANE:2 * LANE] = jnp.where(lo, dqs[2], dqs[3]).astype(BF16)
        dkp_ref[...] = jnp.where(kvm, (dk_full + pltpu.roll(dk_full, HEAD, 1)) * scale, 0.0)
        dvp_ref[...] = jnp.where(kvm, dv_full + pltpu.roll(dv_full, HEAD, 1), 0.0)

    def blk(c, shift):
        return pl.BlockSpec((SWA_BLOCK, LANE), lambda kvh, n, c=c, s=shift: (jnp.maximum(n - s, 0), c))

    qblk = pl.BlockSpec((SWA_BLOCK, 256), lambda kvh, n: (n, kvh))
    part = pl.BlockSpec((None, None, 2 * SWA_BLOCK, LANE), lambda kvh, n: (kvh, n, 0, 0))
    bspec = pl.BlockSpec((None, SWA_GROUP, SWA_BLOCK, 256), lambda kvh, n: (kvh, 0, 0, 0))
    sspec = pl.BlockSpec((None, SWA_GROUP, LANE), lambda kvh, n: (kvh, 0, 0))
    return pl.pallas_call(
        body, name=name, grid=(2, nb),
        out_shape=(jax.ShapeDtypeStruct((t, BRANCH), BF16),
                   jax.ShapeDtypeStruct((2, nb, 2 * SWA_BLOCK, LANE), F32),
                   jax.ShapeDtypeStruct((2, nb, 2 * SWA_BLOCK, LANE), F32),
                   jax.ShapeDtypeStruct((2, SWA_GROUP, SWA_BLOCK, 256), F32),
                   jax.ShapeDtypeStruct((2, SWA_GROUP, LANE), F32)),
        in_specs=[pl.BlockSpec((SWA_BLOCK, 256), lambda kvh, n: (n, csq + kvh)),
                  blk(csk, 1), blk(csk, 0), blk(csv, 1), blk(csv, 0), bspec, sspec,
                  pl.BlockSpec((SWA_GROUP, SWA_BLOCK, LANE), lambda kvh, n: (kvh, n, 0)), qblk, qblk],
        out_specs=(qblk, part, part, bspec, sspec),
        compiler_params=_params(("parallel", "arbitrary"), 6 << 20),
    )(proj, proj, proj, proj, proj, bias.reshape(2, SWA_GROUP, SWA_BLOCK, 256), sink_rep, lse, y, dy)


def _gate_fwd(proj, pb, name):
    t = proj.shape[0]
    tr = _pick(t, ROW_TILE // 2, BF16_SUBLANE)

    def body(g0, g1, g2, p0, p1, p2, o_ref):
        acc = jax.nn.sigmoid(g0[...]) * p0[...]
        acc = acc + jax.nn.sigmoid(g1[...]) * p1[...]
        acc = acc + jax.nn.sigmoid(g2[...]) * p2[...]
        o_ref[...] = acc.astype(BF16)

    row = pl.BlockSpec((tr, D_MODEL), lambda i: (i, 0))
    gates = [pl.BlockSpec((tr, D_MODEL), lambda i, b=b: (i, b)) for b in range(3)]
    return pl.pallas_call(
        body, name=name, grid=(t // tr,),
        out_shape=jax.ShapeDtypeStruct((t, D_MODEL), BF16),
        in_specs=gates + [row] * 3, out_specs=row,
        compiler_params=_params(("parallel",), 7 * _nbytes((tr, D_MODEL), F32)),
    )(proj, proj, proj, *pb)


def _gate_bwd(proj, pb, dmerged, name):
    t = proj.shape[0]
    tr = _pick(t, ROW_TILE // 2, BF16_SUBLANE)

    def body(g0, g1, g2, p0, p1, p2, dm_ref, dp0, dp1, dp2, dg_ref):
        dm = dm_ref[...]
        for b, (g_ref, p_ref, dp_ref) in enumerate(((g0, p0, dp0), (g1, p1, dp1), (g2, p2, dp2))):
            sg = jax.nn.sigmoid(g_ref[...])
            dp_ref[...] = (dm * sg).astype(BF16)
            dg_ref[:, b * D_MODEL:(b + 1) * D_MODEL] = (dm * p_ref[...] * sg * (1.0 - sg)).astype(BF16)

    row = pl.BlockSpec((tr, D_MODEL), lambda i: (i, 0))
    gates = [pl.BlockSpec((tr, D_MODEL), lambda i, b=b: (i, b)) for b in range(3)]
    return pl.pallas_call(
        body, name=name, grid=(t // tr,),
        out_shape=(jax.ShapeDtypeStruct((t, D_MODEL), BF16),) * 3 + (jax.ShapeDtypeStruct((t, 3 * D_MODEL), BF16),),
        in_specs=gates + [row] * 4,
        out_specs=(row, row, row, pl.BlockSpec((tr, 3 * D_MODEL), lambda i: (i, 0))),
        compiler_params=_params(("parallel",), 11 * _nbytes((tr, D_MODEL), F32)),
    )(proj, proj, proj, *pb, dmerged)


def _swiglu_fwd(ab, name):
    t = ab.shape[0]
    tr = _pick(t, ROW_TILE, BF16_SUBLANE)
    tc = D_FF_P // 2

    def body(a_ref, b_ref, o_ref):
        a = a_ref[...]
        o_ref[...] = (a * jax.nn.sigmoid(a) * b_ref[...]).astype(BF16)

    return pl.pallas_call(
        body, name=name, grid=(t // tr, 2),
        out_shape=jax.ShapeDtypeStruct((t, D_FF_P), BF16),
        in_specs=[pl.BlockSpec((tr, tc), lambda i, j: (i, j)), pl.BlockSpec((tr, tc), lambda i, j: (i, j + 2))],
        out_specs=pl.BlockSpec((tr, tc), lambda i, j: (i, j)),
        compiler_params=_params(("parallel", "parallel"), 3 * _nbytes((tr, tc), F32)),
    )(ab, ab)


def _swiglu_bwd(ab, dh, name):
    t = ab.shape[0]
    tr = _pick(t, ROW_TILE, BF16_SUBLANE)
    tc = D_FF_P // 2

    def body(a_ref, b_ref, dh_ref, o_ref):
        jj = pl.program_id(1)
        a, b, d = a_ref[...], b_ref[...], dh_ref[...]
        sg = jax.nn.sigmoid(a)
        da = d * b * (sg * (1.0 + a * (1.0 - sg)))
        db = d * (a * sg)
        o_ref[...] = jnp.where(jj < 2, da, db).astype(BF16)

    return pl.pallas_call(
        body, name=name, grid=(t // tr, 4),
        out_shape=jax.ShapeDtypeStruct((t, 2 * D_FF_P), BF16),
        in_specs=[pl.BlockSpec((tr, tc), lambda i, j: (i, j % 2)),
                  pl.BlockSpec((tr, tc), lambda i, j: (i, j % 2 + 2)),
                  pl.BlockSpec((tr, tc), lambda i, j: (i, j % 2))],
        out_specs=pl.BlockSpec((tr, tc), lambda i, j: (i, j)),
        compiler_params=_params(("parallel", "parallel"), 4 * _nbytes((tr, tc), F32)),
    )(ab, ab, dh)


def _xattn_fwd(q, kv, name):
    t = q.shape[0]
    tq = _pick(t, ROW_TILE, BF16_SUBLANE)
    mlen = kv.shape[0]
    scale = X_HEAD ** -0.5

    def body(q_ref, kv_ref, o_ref):
        for h in range(X_HEADS):
            sl = slice(h * X_HEAD, (h + 1) * X_HEAD)
            kh = kv_ref[:, sl]
            vh = kv_ref[:, D_MODEL + h * X_HEAD:D_MODEL + (h + 1) * X_HEAD]
            s = _dot(q_ref[:, sl], kh, NT) * scale
            p = jnp.exp(s - jnp.max(s, axis=1, keepdims=True))
            l = jnp.sum(p, axis=1, keepdims=True)
            o_ref[:, sl] = (_dot(p.astype(BF16), vh, NN) / l).astype(BF16)

    return pl.pallas_call(
        body, name=name, grid=(t // tq,),
        out_shape=jax.ShapeDtypeStruct((t, D_MODEL), BF16),
        in_specs=[pl.BlockSpec((tq, D_MODEL), lambda i: (i, 0)), pl.BlockSpec((mlen, 2 * D_MODEL), lambda i: (0, 0))],
        out_specs=pl.BlockSpec((tq, D_MODEL), lambda i: (i, 0)),
        compiler_params=_params(("parallel",), 4 * _nbytes((tq, D_MODEL), F32)),
    )(q, kv)


def _xattn_bwd(q, kv, do, name):
    t = q.shape[0]
    tq = _pick(t, ROW_TILE, BF16_SUBLANE)
    mlen = kv.shape[0]
    scale = X_HEAD ** -0.5

    def body(q_ref, kv_ref, do_ref, dq_ref, dkv_ref):
        i = pl.program_id(0)

        @pl.when(i == 0)
        def _():
            dkv_ref[...] = jnp.zeros_like(dkv_ref)

        for h in range(X_HEADS):
            sl = slice(h * X_HEAD, (h + 1) * X_HEAD)
            vsl = slice(D_MODEL + h * X_HEAD, D_MODEL + (h + 1) * X_HEAD)
            qh, kh, vh, doh = q_ref[:, sl], kv_ref[:, sl], kv_ref[:, vsl], do_ref[:, sl]
            s = _dot(qh, kh, NT) * scale
            p = jnp.exp(s - jnp.max(s, axis=1, keepdims=True))
            p = p / jnp.sum(p, axis=1, keepdims=True)
            dp = _dot(doh, vh, NT)
            ds = p * (dp - jnp.sum(p * dp, axis=1, keepdims=True))
            dsb = ds.astype(BF16)
            dq_ref[:, sl] = (_dot(dsb, kh, NN) * scale).astype(BF16)
            dkv_ref[:, sl] += _dot(dsb, qh, TN) * scale
            dkv_ref[:, vsl] += _dot(p.astype(BF16), doh, TN)

    row = pl.BlockSpec((tq, D_MODEL), lambda i: (i, 0))
    whole = pl.BlockSpec((mlen, 2 * D_MODEL), lambda i: (0, 0))
    return pl.pallas_call(
        body, name=name, grid=(t // tq,),
        out_shape=(jax.ShapeDtypeStruct((t, D_MODEL), BF16), jax.ShapeDtypeStruct((mlen, 2 * D_MODEL), F32)),
        in_specs=[row, whole, row], out_specs=(row, whole),
        compiler_params=_params(("arbitrary",), 6 * _nbytes((tq, D_MODEL), F32)),
    )(q, kv, do)


def _position():
    return lax.axis_index("x"), lax.axis_index("y"), lax.axis_index("c")


N_PEER = N_DEV - 1


def _all_gather(xs, name):
    n = len(xs)

    def body(*refs):
        x_refs, out_refs = refs[:n], refs[n:2 * n]
        send_sems, recv_sems, local_sems = refs[2 * n:]
        mx, my, mc = _position()
        me, sib = (mx, my, mc), (mx, my, 1 - mc)
        chips = [(1 - mx, my), (mx, 1 - my), (1 - mx, 1 - my)]

        def slot(i, p):
            return out_refs[i].at[4 * p[0] + 2 * p[1] + p[2]]

        def copy(i, k, block, to, src=None):
            return pltpu.make_async_remote_copy(
                src_ref=slot(i, block) if src is None else src, dst_ref=slot(i, block),
                send_sem=send_sems.at[i * N_PEER + k], recv_sem=recv_sems.at[i * N_PEER + k],
                device_id=to, device_id_type=MESH)

        mine = [pltpu.make_async_copy(x_refs[i], slot(i, me), local_sems.at[i]) for i in range(n)]
        for cp in mine:
            cp.start()
        first = [copy(i, 1 + j, me, (*chip, mc), src=x_refs[i]) for j, chip in enumerate(chips) for i in range(n)]
        first += [copy(i, 0, me, sib, src=x_refs[i]) for i in range(n)]
        for cp in first:
            cp.start()
        passed = []
        for j, chip in enumerate(chips):
            for i in range(n):
                copy(i, 1 + j, (*chip, mc), me).wait_recv()
                passed.append(copy(i, 4 + j, (*chip, mc), sib))
                passed[-1].start()
        for i in range(n):
            copy(i, 0, sib, me).wait_recv()
        for j, chip in enumerate(chips):
            for i in range(n):
                copy(i, 4 + j, (*chip, 1 - mc), me).wait_recv()
        for cp in first + passed:
            cp.wait_send()
        for cp in mine:
            cp.wait()

    return pl.pallas_call(
        body, name=name,
        out_shape=tuple(jax.ShapeDtypeStruct((N_DEV,) + x.shape, x.dtype) for x in xs),
        in_specs=[pl.BlockSpec(memory_space=pl.ANY)] * n, out_specs=(pl.BlockSpec(memory_space=pl.ANY),) * n,
        scratch_shapes=[pltpu.SemaphoreType.DMA((n * N_PEER,)), pltpu.SemaphoreType.DMA((n * N_PEER,)),
                        pltpu.SemaphoreType.DMA((n,))],
    )(*xs)


PEER_RELS = [(dx, dy, dc) for dx in (0, 1) for dy in (0, 1) for dc in (0, 1)][1:]


def _peer_copy(rel_k, i, src_refs, land_refs, send_sems, recv_sems, layer, gather, arriving):
    mx, my, mc = _position()
    me_idx = 4 * mx + 2 * my + mc
    p = tuple((1 - v) if f else v for f, v in zip(PEER_RELS[rel_k], (mx, my, mc)))
    p_idx = 4 * p[0] + 2 * p[1] + p[2]
    src_slot, dst_slot = (me_idx, p_idx) if arriving else (p_idx, me_idx)
    src = src_refs[i] if gather else src_refs[i].at[src_slot]
    dst = land_refs[i].at[dst_slot] if layer is None else land_refs[i].at[dst_slot, layer]
    return pltpu.make_async_remote_copy(
        src_ref=src, dst_ref=dst, send_sem=send_sems.at[i * N_PEER + rel_k], recv_sem=recv_sems.at[i * N_PEER + rel_k],
        device_id=p, device_id_type=MESH)


def _own_copy(i, src_refs, land_refs, sem, layer, gather):
    mx, my, mc = _position()
    me_idx = 4 * mx + 2 * my + mc
    src = src_refs[i] if gather else src_refs[i].at[me_idx]
    dst = land_refs[i].at[me_idx] if layer is None else land_refs[i].at[me_idx, layer]
    return pltpu.make_async_copy(src, dst, sem)


def _exchange(parts, lands, layer, name):
    n = len(parts)

    def body(*refs):
        g_refs, land_refs = refs[:n], refs[2 * n:3 * n]
        send_sems, recv_sems, local_sems = refs[3 * n:]
        args = (g_refs, land_refs, send_sems, recv_sems, layer, False)
        mine = [_own_copy(i, g_refs, land_refs, local_sems.at[i], layer, False) for i in range(n)]
        for cp in mine:
            cp.start()
        sends = [_peer_copy(k, i, *args, False) for i in range(n) for k in range(N_PEER)]
        for cp in sends:
            cp.start()
        for i in range(n):
            for k in range(N_PEER):
                _peer_copy(k, i, *args, True).wait_recv()
        for cp in sends:
            cp.wait_send()
        for cp in mine:
            cp.wait()

    return pl.pallas_call(
        body, name=name,
        out_shape=tuple(jax.ShapeDtypeStruct(l.shape, l.dtype) for l in lands),
        in_specs=[pl.BlockSpec(memory_space=pl.ANY)] * (2 * n), out_specs=(pl.BlockSpec(memory_space=pl.ANY),) * n,
        input_output_aliases={n + i: i for i in range(n)},
        scratch_shapes=[pltpu.SemaphoreType.DMA((n * N_PEER,)), pltpu.SemaphoreType.DMA((n * N_PEER,)),
                        pltpu.SemaphoreType.DMA((n,))],
    )(*parts, *lands)


HBM_SPEC = pl.BlockSpec(memory_space=pltpu.HBM)
SEM_SPEC = pl.BlockSpec(memory_space=pltpu.SEMAPHORE)
SIDE_EFFECT = pltpu.SideEffectType.DATAFLOW_SIDE_EFFECTING


def _own_slots(srcs, lands, layer, gather):
    mx, my, mc = _position()
    me_idx = 4 * mx + 2 * my + mc
    out = []
    for s, land in zip(srcs, lands):
        piece = s[None] if gather else lax.dynamic_index_in_dim(s, me_idx, 0, keepdims=True)
        if layer is None:
            start = (me_idx,) + (0,) * (land.ndim - 1)
        else:
            piece, start = piece[:, None], (me_idx, layer) + (0,) * (land.ndim - 2)
        out.append(lax.dynamic_update_slice(land, piece, start))
    return out


def _swap_start(srcs, lands, layer, gather, name):
    n = len(srcs)

    def body(*refs):
        src_refs, land_refs = refs[:n], refs[n:2 * n]
        send_sems, recv_sems = refs[2 * n], refs[2 * n + 1]
        token = refs[4 * n + 2]
        for i in range(n):
            for k in range(N_PEER):
                _peer_copy(k, i, src_refs, land_refs, send_sems, recv_sems, layer, gather, False).start()
        token[...] = jnp.zeros_like(token)

    hbm = [pltpu.with_memory_space_constraint(a, pltpu.HBM) for a in list(srcs) + list(lands)]
    out = pl.pallas_call(
        body, name=name,
        out_shape=(pltpu.SemaphoreType.DMA((n * N_PEER,)), pltpu.SemaphoreType.DMA((n * N_PEER,)))
        + tuple(pltpu.HBM(a.shape, a.dtype) for a in hbm) + (jax.ShapeDtypeStruct((8, LANE), F32),),
        in_specs=[HBM_SPEC] * (2 * n),
        out_specs=(SEM_SPEC, SEM_SPEC) + (HBM_SPEC,) * (2 * n) + (pl.BlockSpec(memory_space=pltpu.VMEM),),
        input_output_aliases={i: 2 + i for i in range(2 * n)},
        compiler_params=pltpu.CompilerParams(has_side_effects=SIDE_EFFECT),
    )(*hbm)
    return out[0], out[1], list(out[2:2 + n]), list(out[2 + n:2 + 2 * n]), out[2 + 2 * n]


def _swap_wait(send_sems, recv_sems, srcs, lands, after, layer, gather, name):
    n = len(srcs)

    def body(*refs):
        src_refs, land_refs = refs[:n], refs[n:2 * n]
        send_sems_ref, recv_sems_ref = refs[2 * n], refs[2 * n + 1]
        for i in range(n):
            for k in range(N_PEER):
                args = (src_refs, land_refs, send_sems_ref, recv_sems_ref, layer, gather)
                _peer_copy(k, i, *args, False).wait_send()
                _peer_copy(k, i, *args, True).wait_recv()

    out = pl.pallas_call(
        body, name=name,
        out_shape=tuple(pltpu.HBM(a.shape, a.dtype) for a in list(srcs) + list(lands)),
        in_specs=[HBM_SPEC] * (2 * n) + [SEM_SPEC, SEM_SPEC, pl.BlockSpec(memory_space=pl.ANY)],
        out_specs=(HBM_SPEC,) * (2 * n),
        input_output_aliases={i: i for i in range(2 * n)},
        compiler_params=pltpu.CompilerParams(has_side_effects=SIDE_EFFECT),
    )(*srcs, *lands, send_sems, recv_sems, after)
    return list(out[n:])


ADAMW_BLOCK_BYTES = 1 << 20


def _adamw(parts, w, m, v, name):
    r, l = w.shape
    tr = _pick(r, max(ADAMW_BLOCK_BYTES // (4 * l), BF16_SUBLANE), BF16_SUBLANE)
    c1 = 1.0 - ADAM_B1 ** ADAM_STEP
    c2 = 1.0 - ADAM_B2 ** ADAM_STEP

    def body(p_ref, w_ref, m_ref, v_ref, g_ref, d_ref, nm_ref, nv_ref):
        g = p_ref[0].astype(F32)
        for s in range(1, N_DEV):
            g = g + p_ref[s].astype(F32)
        nm = ADAM_B1 * m_ref[...] + (1.0 - ADAM_B1) * g
        nv = ADAM_B2 * v_ref[...] + (1.0 - ADAM_B2) * (g * g)
        m_hat = nm / c1
        v_hat = nv / c2
        g_ref[...] = g
        d_ref[...] = -ADAM_LR * (m_hat / (jnp.sqrt(v_hat) + ADAM_EPS) + ADAM_WD * w_ref[...])
        nm_ref[...] = nm
        nv_ref[...] = nv

    row = pl.BlockSpec((tr, l), lambda i: (i, 0))
    return pl.pallas_call(
        body, name=name, grid=(r // tr,),
        out_shape=(jax.ShapeDtypeStruct((r, l), F32),) * 4,
        in_specs=[pl.BlockSpec((N_DEV, tr, l), lambda i: (0, i, 0)), row, row, row],
        out_specs=(row,) * 4,
        compiler_params=_params(("parallel",), 12 * _nbytes((tr, l), F32)),
    )(parts, w, m, v)


MATRIX_WEIGHTS = (("w_in", 2), ("conv_w", 2), ("w_branch", 3), ("w_mix_out", 1), ("w_xq", 1), ("w_xkv", 2),
                  ("w_xo", 1), ("w_ffn_gate", 2), ("w_ffn_up", 2), ("w_ffn_down", 1))
SMALL_PARAMS = ("mix_norm_g", "xattn_norm_g", "mem_norm_g", "ffn_norm_g", "final_norm_g", "forget_bias", "sink",
                "rel_bias")


def _pack_small(pieces):
    flat = jnp.concatenate([p.astype(F32).reshape(-1) for p in pieces])
    total = -(-flat.shape[0] // (8 * LANE)) * (8 * LANE)
    return jnp.pad(flat, (0, total - flat.shape[0])).reshape(total // LANE, LANE)


def _rows(a):
    return a.reshape(-1, a.shape[-1])


def _to_full(gathered, axis):
    moved = jnp.moveaxis(gathered, 0, axis)
    shape = list(moved.shape)
    shape[axis:axis + 2] = [shape[axis] * shape[axis + 1]]
    return moved.reshape(shape)


def _to_blocks(full, axis):
    shape = list(full.shape)
    shape[axis:axis + 1] = [N_DEV, shape[axis] // N_DEV]
    return jnp.moveaxis(full.reshape(shape), axis, 0)


def _perm_in(w_in):
    pad = jnp.zeros((w_in.shape[0], PROJ_COLS - IN_COLS), w_in.dtype)
    return jnp.concatenate([w_in[:, 3848:6920], w_in[:, 0:3072], w_in[:, 3080:3848], w_in[:, 3072:3080], pad], axis=1)


def _unperm_in(dw):
    return jnp.concatenate([dw[:, 3072:6144], dw[:, 6912:6920], dw[:, 6144:6912], dw[:, 0:3072]], axis=1)


def _layer_fwd(l, x, mem, wt, sm):
    t = x.shape[0]
    tag = f"l{l}_"
    h = _rms_fwd(x, sm["mix_norm_g"][l], tag + "mix_norm")
    proj = _matmul(h, wt["w_in"][l], "nn", F32, tag + "in_proj")
    y_conv = _conv_fwd(proj, wt["conv_w"][l], tag + "conv")
    fbias_row = jnp.pad(sm["forget_bias"][l], (0, LANE - 8)).reshape(1, LANE)
    c = _logf_cumsum(proj, fbias_row, tag + "logf_cumsum")
    c8 = c[:, :8].T
    c_col = jnp.broadcast_to(c8[:, :, None], (8, t, LANE))
    c_row = c8.reshape(8, 1, t)
    y_fox, lse_fox = _fox_fwd(proj, c_col, c_row, tag + "fox")
    onehot, bias = sm["swa_tables"]
    sink_rep = jnp.broadcast_to(sm["sink"][l].reshape(2, SWA_GROUP, 1), (2, SWA_GROUP, LANE))
    y_swa, lse_swa = _swa_fwd(proj, bias, sink_rep, tag + "swa")
    ys = (y_conv, y_fox, y_swa)
    pb = tuple(_matmul(ys[b], wt["w_branch"][l][b], "nn", F32, tag + f"branch{b}") for b in range(3))
    merged = _gate_fwd(proj, pb, tag + "gate")
    x1 = _matmul(merged, wt["w_mix_out"][l], "nn", F32, tag + "mix_out", residual=x)
    xn2 = _rms_fwd(x1, sm["xattn_norm_g"][l], tag + "xattn_norm")
    q = _matmul(xn2, wt["w_xq"][l], "nn", BF16, tag + "xq")
    mem_n = _rms_fwd(mem, sm["mem_norm_g"][l], tag + "mem_norm")
    kv = _matmul(mem_n, wt["w_xkv"][l], "nn", BF16, tag + "xkv")
    o = _xattn_fwd(q, kv, tag + "xattn")
    x2 = _matmul(o, wt["w_xo"][l], "nn", F32, tag + "xo", residual=x1)
    xn3 = _rms_fwd(x2, sm["ffn_norm_g"][l], tag + "ffn_norm")
    ab = _matmul(xn3, wt["w_gu"][l], "nn", F32, tag + "ffn_gu")
    h1 = _swiglu_fwd(ab, tag + "swiglu")
    x3 = _matmul(h1, wt["w_ffn_down"][l], "nn", F32, tag + "ffn_down", residual=x2)
    saved = dict(x=x, h=h, proj=proj, fbias_row=fbias_row, c_col=c_col, c_row=c_row, ys=ys, lse_fox=lse_fox,
                 onehot=onehot, bias=bias, sink_rep=sink_rep, lse_swa=lse_swa, pb=pb, merged=merged, x1=x1,
                 xn2=xn2, q=q, mem_n=mem_n, kv=kv, o=o, x2=x2, xn3=xn3, ab=ab, h1=h1)
    return x3, saved


def _layer_bwd(l, dx3, dx3_b, mem, wt, sm, sv, mid_hook=None):
    t = dx3.shape[0]
    nb = t // SWA_BLOCK
    tag = f"l{l}_b_"
    gw, gs = {}, {}
    dh1 = _matmul(dx3_b, wt["w_ffn_down"][l], "nt", F32, tag + "d_h1")
    gw["w_ffn_down"] = _matmul(sv["h1"], dx3_b, "tn", F32, tag + "dw_down")
    dab = _swiglu_bwd(sv["ab"], dh1, tag + "swiglu")
    dxn3 = _matmul(dab, wt["w_gu"][l], "nt", F32, tag + "d_xn3")
    dw_gu = _matmul(sv["xn3"], dab, "tn", F32, tag + "dw_gu")
    gw["w_ffn_gate"], gw["w_ffn_up"] = dw_gu[:, :D_FF_P], dw_gu[:, D_FF_P:]
    dx2, dx2_b, gs["ffn_norm_g"] = _rms_bwd(sv["x2"], sm["ffn_norm_g"][l], dxn3, dx3, tag + "ffn_norm")
    do = _matmul(dx2_b, wt["w_xo"][l], "nt", BF16, tag + "d_o")
    gw["w_xo"] = _matmul(sv["o"], dx2_b, "tn", F32, tag + "dw_xo")
    dq, dkv = _xattn_bwd(sv["q"], sv["kv"], do, tag + "xattn")
    gw["w_xkv"] = _matmul(sv["mem_n"], dkv, "tn", F32, tag + "dw_xkv")
    dmem_n = _matmul(dkv, wt["w_xkv"][l], "nt", F32, tag + "d_memn")
    _, _, gs["mem_norm_g"] = _rms_bwd(mem, sm["mem_norm_g"][l], dmem_n, None, tag + "mem_norm")
    gw["w_xq"] = _matmul(sv["xn2"], dq, "tn", F32, tag + "dw_xq")
    dxn2 = _matmul(dq, wt["w_xq"][l], "nt", F32, tag + "d_xn2")
    dx1, dx1_b, gs["xattn_norm_g"] = _rms_bwd(sv["x1"], sm["xattn_norm_g"][l], dxn2, dx2, tag + "xattn_norm")
    dmerged = _matmul(dx1_b, wt["w_mix_out"][l], "nt", F32, tag + "d_merged")
    gw["w_mix_out"] = _matmul(sv["merged"], dx1_b, "tn", F32, tag + "dw_mix_out")
    dp0, dp1, dp2, dgate = _gate_bwd(sv["proj"], sv["pb"], dmerged, tag + "gate")
    dps = (dp0, dp1, dp2)
    dy_dtypes = (F32, BF16, BF16)
    dys = [_matmul(dps[b], wt["w_branch"][l][b], "nt", dy_dtypes[b], tag + f"d_y{b}") for b in range(3)]
    gw["w_branch"] = jnp.stack(
        [_matmul(sv["ys"][b], dps[b], "tn", F32, tag + f"dw_branch{b}") for b in range(3)])
    sink_rep = sv["sink_rep"] if mid_hook is None else sv["sink_rep"] + mid_hook(gw)
    dsq, dkp, dvp, dbias, dsink = _swa_bwd(sv["proj"], sv["bias"], sink_rep, sv["lse_swa"], sv["ys"][2],
                                           dys[2], tag + "swa")

    def band_add(part):
        tot = part[0] + part[1]
        cur = tot[:, SWA_BLOCK:, :]
        nxt = jnp.concatenate([tot[1:, :SWA_BLOCK, :], jnp.zeros((1, SWA_BLOCK, LANE), F32)], axis=0)
        return (cur + nxt).reshape(t, LANE).astype(BF16)

    dsk, dsv = band_add(dkp), band_add(dvp)
    gs["rel_bias_l"] = jnp.einsum("hts,tsb->bh", dbias.reshape(8, SWA_BLOCK, 2 * SWA_BLOCK), sv["onehot"],
                                  precision=lax.Precision.HIGHEST)
    gs["sink"] = dsink[:, :, 0].reshape(8)
    dfq, dfk, dfv, dck, dcq_row = _fox_bwd(sv["proj"], sv["c_col"], sv["c_row"], sv["lse_fox"], sv["ys"][1], dys[1],
                                           tag + "fox_bwd")
    dcq = jnp.pad(dcq_row.reshape(8, t).T, ((0, 0), (0, LANE - 8))).reshape(1, t, LANE)
    dfg, dfb = _logf_cumsum_bwd(sv["proj"], sv["fbias_row"], [dck, dcq], tag + "logf_cumsum")
    gs["forget_bias"] = dfb[0, :8]
    dcb, dcc, dcu, dconv = _conv_bwd(sv["proj"], wt["conv_w"][l], dys[0], tag + "conv")
    gw["conv_w"] = dconv[:3]
    dproj = jnp.concatenate([dgate, dcb, dcc, dcu, dfq, dfk, dfv, dsq, dsk, dsv, dfg], axis=1)
    dh = _matmul(dproj, wt["w_in"][l], "nt", F32, tag + "d_h")
    gw["w_in"] = _unperm_in(_matmul(sv["h"], dproj, "tn", F32, tag + "dw_in"))
    dx, dx_b, gs["mix_norm_g"] = _rms_bwd(sv["x"], sm["mix_norm_g"][l], dh, dx1, tag + "mix_norm")
    return dx, dx_b, gw, gs


def kernel(x, mem, mix_norm_g, w_in, forget_bias, conv_w, sink, w_branch, w_mix_out, rel_bias, xattn_norm_g, mem_norm_g, w_xq, w_xkv, w_xo, ffn_norm_g, w_ffn_gate, w_ffn_up, w_ffn_down, final_norm_g, loss_target, m_mix_norm_g, m_w_in, m_forget_bias, m_conv_w, m_sink, m_w_branch, m_w_mix_out, m_rel_bias, m_xattn_norm_g, m_mem_norm_g, m_w_xq, m_w_xkv, m_w_xo, m_ffn_norm_g, m_w_ffn_gate, m_w_ffn_up, m_w_ffn_down, m_final_norm_g, v_mix_norm_g, v_w_in, v_forget_bias, v_conv_w, v_sink, v_w_branch, v_w_mix_out, v_rel_bias, v_xattn_norm_g, v_mem_norm_g, v_w_xq, v_w_xkv, v_w_xo, v_ffn_norm_g, v_w_ffn_gate, v_w_ffn_up, v_w_ffn_down, v_final_norm_g):
    args = dict(locals())
    names = [n for n, _ in MATRIX_WEIGHTS] + list(SMALL_PARAMS)
    w = {n: args[n] for n in names}
    mo = {n: args["m_" + n] for n in names}
    vo = {n: args["v_" + n] for n in names}
    x2d, mem2d, tgt = x[0], mem[0], loss_target[0]

    wire = {n: (F32 if n == "conv_w" else BF16) for n, _ in MATRIX_WEIGHTS}
    wt = {n: [None] * DEPTH for n, _ in MATRIX_WEIGHTS}
    wt["w_gu"] = [None] * DEPTH

    ff_pad = FF_SHARD_P - FF_SHARD
    ff_axis = {"w_ffn_gate": 2, "w_ffn_up": 2, "w_ffn_down": 1}

    def pad_ffn(n, blocks):
        if n not in ff_axis:
            return blocks
        return jnp.pad(blocks, [(0, ff_pad if d == ff_axis[n] else 0) for d in range(blocks.ndim)])

    def unpad_ffn(n, blocks):
        return lax.slice_in_dim(blocks, 0, FF_SHARD, axis=ff_axis[n]) if n in ff_axis else blocks

    def place_weights(l, gathered):
        for (n, ax), g in zip(MATRIX_WEIGHTS, gathered):
            wt[n][l] = _to_full(pad_ffn(n, g), ax - 1)
        wt["w_in"][l] = _perm_in(wt["w_in"][l])
        wt["w_gu"][l] = jnp.concatenate([wt["w_ffn_gate"][l], wt["w_ffn_up"][l]], axis=1)

    shards = [[w[n][l].astype(wire[n]) for n, _ in MATRIX_WEIGHTS] for l in range(DEPTH)]
    place_weights(0, _all_gather(shards[0], "weights_gather_l0"))
    lands = _own_slots(shards[1], [lax.empty((N_DEV,) + s.shape, s.dtype) for s in shards[1]], None, True)
    w_send, w_recv, w_srcs, lands, token = _swap_start(shards[1], lands, None, True, "weights_gather_l1_start")
    sm = {n: w[n] for n in SMALL_PARAMS}
    sm["mix_norm_g"] = w["mix_norm_g"].at[0].add(token[0, 0])
    sm["swa_tables"] = _swa_tables(w["rel_bias"])

    saved = []
    xc = x2d
    for l in range(DEPTH):
        if l == 1:
            place_weights(1, _swap_wait(w_send, w_recv, w_srcs, lands, xc, None, True, "weights_gather_l1_wait"))
        xc, sv = _layer_fwd(l, xc, mem2d, wt, sm)
        saved.append(sv)
    loss_row, dx, dx_b, dg_final = _loss_head(xc, sm["final_norm_g"], tgt, "loss_head")
    loss = lax.psum(loss_row[0, 0], ("x", "y", "c"))

    late = ("w_in", "conv_w")
    early_w = [(n, ax) for n, ax in MATRIX_WEIGHTS if n not in late]
    late_w = [(n, ax) for n, ax in MATRIX_WEIGHTS if n in late]

    def grad_parts(gw, which):
        return [unpad_ffn(n, _to_blocks(gw[n], ax - 1)).astype(wire[n]) for n, ax in which]

    gw_all, gs_all = [None] * DEPTH, [None] * DEPTH
    dx, dx_b, gw_all[1], gs_all[1] = _layer_bwd(1, dx, dx_b, mem2d, wt, sm, saved[1])
    parts1 = grad_parts(gw_all[1], MATRIX_WEIGHTS)
    zones = _own_slots(parts1, [lax.empty((N_DEV, DEPTH) + p.shape[1:], p.dtype) for p in parts1], 1, False)
    g_send, g_recv, g_srcs, zones, token = _swap_start(parts1, zones, 1, False, "grads_exchange_l1_start")
    sm_b = dict(sm)
    sm_b["ffn_norm_g"] = sm["ffn_norm_g"].at[0].add(token[0, 0])
    mid = {}

    def mid_hook(gw):
        zone = dict(zip([n for n, _ in MATRIX_WEIGHTS],
                        _swap_wait(g_send, g_recv, g_srcs, zones, gw["w_mix_out"], 1, False, "grads_exchange_l1_wait")))
        parts0 = grad_parts(gw, early_w)
        early_zones = _own_slots(parts0, [zone[n] for n, _ in early_w], 0, False)
        mid["early"] = _swap_start(parts0, early_zones, 0, False, "grads_exchange_l0_early_start")
        mid["late_zones"] = [zone[n] for n, _ in late_w]
        return mid["early"][4][0, 0]

    dx, dx_b, gw_all[0], gs_all[0] = _layer_bwd(0, dx, dx_b, mem2d, wt, sm_b, saved[0], mid_hook)
    grad_x = dx[None]
    e_send, e_recv, e_srcs, e_zones, _ = mid["early"]
    recv_by_name = dict(zip([n for n, _ in early_w],
                            _swap_wait(e_send, e_recv, e_srcs, e_zones, dx, 0, False, "grads_exchange_l0_early_wait")))
    recv_by_name.update(zip([n for n, _ in late_w],
                            _exchange(grad_parts(gw_all[0], late_w), mid["late_zones"], 0, "grads_exchange_l0_late")))
    recv = [recv_by_name[n] for n, _ in MATRIX_WEIGHTS]

    outs = {}
    for (n, _), r in zip(MATRIX_WEIGHTS, recv):
        res = _adamw(r.reshape((N_DEV,) + _rows(w[n]).shape), _rows(w[n]), _rows(mo[n]), _rows(vo[n]), "adamw_" + n)
        outs[n] = [o.reshape(w[n].shape) for o in res]

    gsm = {n: jnp.stack([gs_all[l][n] for l in range(DEPTH)])
           for n in ("mix_norm_g", "xattn_norm_g", "mem_norm_g", "ffn_norm_g", "forget_bias", "sink")}
    gsm["final_norm_g"] = dg_final
    gsm["rel_bias"] = gs_all[0]["rel_bias_l"] + gs_all[1]["rel_bias_l"]
    (small_parts,) = _all_gather([_pack_small([gsm[n] for n in SMALL_PARAMS])], "small_grads_all_gather")
    outs_small = _adamw(small_parts, *[_pack_small([d[n] for n in SMALL_PARAMS]) for d in (w, mo, vo)], "adamw_small")
    for kind in range(4):
        flat, o = outs_small[kind].reshape(-1), 0
        for n in SMALL_PARAMS:
            sz = int(np.prod(w[n].shape))
            outs.setdefault(n, []).append(flat[o:o + sz].reshape(w[n].shape))
            o += sz

    order = ["mix_norm_g", "w_in", "forget_bias", "conv_w", "sink", "w_branch", "w_mix_out", "rel_bias",
             "xattn_norm_g", "mem_norm_g", "w_xq", "w_xkv", "w_xo", "ffn_norm_g", "w_ffn_gate", "w_ffn_up",
             "w_ffn_down", "final_norm_g"]
    result = [loss, grad_x]
    for kind in range(4):
        result += [outs[n][kind] for n in order]
    return tuple(result)
```

```python
import math

import numpy as np
import jax
import jax.numpy as jnp
from jax import lax
from jax.experimental import pallas as pl
from jax.experimental.pallas import tpu as pltpu

F32 = jnp.float32
BF16 = jnp.bfloat16
MESH = pl.DeviceIdType.MESH

LANE = 128
BF16_SUBLANE = 16
V7X_VMEM_REQUEST_CAP = 56 * 2 ** 20
N_DEV = 8

D_MODEL = 1024
DEPTH = 2
HEAD = 64
BRANCH = 512
SWA_BLOCK = 128
SWA_GROUP = 4
N_BUCKETS = 32
X_HEADS = 4
X_HEAD = 256
D_FF = 2816
FF_SHARD = D_FF // N_DEV
FF_SHARD_P = -(-FF_SHARD // LANE) * LANE
D_FF_P = N_DEV * FF_SHARD_P
RMS_EPS = 1e-6
NEG = -1e30
ADAM_LR, ADAM_B1, ADAM_B2, ADAM_EPS, ADAM_WD, ADAM_STEP = 0.001, 0.9, 0.999, 1e-08, 0.01, 10

IN_COLS = 6920
PROJ_COLS = 7040
COL_GATE, COL_CONV, COL_FOX, COL_SQ, COL_SK, COL_SV, COL_FG = 0, 3072, 4608, 6144, 6656, 6784, 6912

ROW_TILE = 512
FOX_TILE = 512
MM_TM, MM_TN, MM_TK = 1024, 1536, 2048


def _pick(n, cap, mult):
    best = None
    for d in range(mult, min(n, cap) + 1, mult):
        if n % d == 0:
            best = d
    return n if best is None else best


def _params(semantics, block_bytes):
    limit = int(min(max(2 * block_bytes + (8 << 20), 24 << 20), V7X_VMEM_REQUEST_CAP))
    return pltpu.CompilerParams(dimension_semantics=semantics, vmem_limit_bytes=limit)


def _nbytes(shape, dtype):
    return int(np.prod(shape)) * jnp.dtype(dtype).itemsize


def _dot(a, b, dims):
    return lax.dot_general(a, b, (dims, ((), ())), preferred_element_type=F32)


NN = ((1,), (0,))
NT = ((1,), (1,))
TN = ((0,), (0,))


def _matmul(a, b, mode, out_dtype, name, residual=None):
    if mode == "nn":
        (m, k), (k2, n) = a.shape, b.shape
    elif mode == "nt":
        (m, k), (n, k2) = a.shape, b.shape
    else:
        (k, m), (k2, n) = a.shape, b.shape
    assert k == k2, (name, a.shape, b.shape)
    tm, tn, tk = _pick(m, MM_TM, LANE), _pick(n, MM_TN, LANE), _pick(k, MM_TK, LANE)
    nk = k // tk
    dims = {"nn": NN, "nt": NT, "tn": TN}[mode]
    has_res = residual is not None

    def body(*refs):
        a_ref, b_ref = refs[0], refs[1]
        r_ref = refs[2] if has_res else None
        o_ref = refs[3] if has_res else refs[2]
        kk = pl.program_id(2)
        p = _dot(a_ref[...].astype(BF16), b_ref[...].astype(BF16), dims)
        if nk == 1:
            if has_res:
                p = p + r_ref[...]
            o_ref[...] = p.astype(out_dtype)
        else:
            acc_ref = refs[-1]

            @pl.when(kk == 0)
            def _():
                acc_ref[...] = p

            @pl.when(kk > 0)
            def _():
                acc_ref[...] += p

            @pl.when(kk == nk - 1)
            def _():
                res = acc_ref[...]
                if has_res:
                    res = res + r_ref[...]
                o_ref[...] = res.astype(out_dtype)

    if mode == "nn":
        a_spec = pl.BlockSpec((tm, tk), lambda i, j, kk: (i, kk))
        b_spec = pl.BlockSpec((tk, tn), lambda i, j, kk: (kk, j))
    elif mode == "nt":
        a_spec = pl.BlockSpec((tm, tk), lambda i, j, kk: (i, kk))
        b_spec = pl.BlockSpec((tn, tk), lambda i, j, kk: (j, kk))
    else:
        a_spec = pl.BlockSpec((tk, tm), lambda i, j, kk: (kk, i))
        b_spec = pl.BlockSpec((tk, tn), lambda i, j, kk: (kk, j))
    o_spec = pl.BlockSpec((tm, tn), lambda i, j, kk: (i, j))
    in_specs, args = [a_spec, b_spec], [a, b]
    if has_res:
        in_specs.append(o_spec)
        args.append(residual)
    blk = (_nbytes((tm, tk), a.dtype) + _nbytes((tk, tn), b.dtype) + _nbytes((tm, tn), out_dtype)
           + (_nbytes((tm, tn), F32) if has_res else 0))
    scratch = [pltpu.VMEM((tm, tn), F32)] if nk > 1 else []
    return pl.pallas_call(
        body, name=name, grid=(m // tm, n // tn, nk),
        out_shape=jax.ShapeDtypeStruct((m, n), out_dtype),
        in_specs=in_specs, out_specs=o_spec, scratch_shapes=scratch,
        compiler_params=_params(("parallel", "parallel", "arbitrary"), blk + _nbytes((tm, tn), F32)),
    )(*args)


def _rms_fwd(x, g, name):
    t, d = x.shape
    tr = _pick(t, ROW_TILE, BF16_SUBLANE)

    def body(x_ref, g_ref, y_ref):
        xv = x_ref[...]
        r = lax.rsqrt(jnp.mean(xv * xv, axis=-1, keepdims=True) + RMS_EPS)
        y_ref[...] = ((xv * r) * g_ref[...]).astype(BF16)

    return pl.pallas_call(
        body, name=name, grid=(t // tr,),
        out_shape=jax.ShapeDtypeStruct((t, d), BF16),
        in_specs=[pl.BlockSpec((tr, d), lambda i: (i, 0)), pl.BlockSpec((1, d), lambda i: (0, 0))],
        out_specs=pl.BlockSpec((tr, d), lambda i: (i, 0)),
        compiler_params=_params(("parallel",), 2 * _nbytes((tr, d), F32)),
    )(x, g.reshape(1, d))


def _rms_bwd(x, g, dy, dres, name):
    t, d = x.shape
    tr = _pick(t, ROW_TILE, BF16_SUBLANE)
    has_res = dres is not None

    def body(*refs):
        x_ref, g_ref, dy_ref = refs[:3]
        r_ref = refs[3] if has_res else None
        dx_ref, dxb_ref, dg_ref = refs[-3:]
        i = pl.program_id(0)
        xv = x_ref[...]
        r = lax.rsqrt(jnp.mean(xv * xv, axis=-1, keepdims=True) + RMS_EPS)
        xh = xv * r
        dyv = dy_ref[...].astype(F32)
        dxh = dyv * g_ref[...]
        dx = r * (dxh - xh * jnp.mean(dxh * xh, axis=-1, keepdims=True))
        if has_res:
            dx = dx + r_ref[...]
        dx_ref[...] = dx
        dxb_ref[...] = dx.astype(BF16)

        @pl.when(i == 0)
        def _():
            dg_ref[...] = jnp.zeros_like(dg_ref)

        dg_ref[...] += jnp.sum(dyv * xh, axis=0, keepdims=True)

    row = pl.BlockSpec((tr, d), lambda i: (i, 0))
    vec = pl.BlockSpec((1, d), lambda i: (0, 0))
    in_specs, args = [row, vec, row], [x, g.reshape(1, d), dy]
    if has_res:
        in_specs.append(row)
        args.append(dres)
    return pl.pallas_call(
        body, name=name, grid=(t // tr,),
        out_shape=(jax.ShapeDtypeStruct((t, d), F32), jax.ShapeDtypeStruct((t, d), BF16),
                   jax.ShapeDtypeStruct((1, d), F32)),
        in_specs=in_specs, out_specs=(row, row, vec),
        compiler_params=_params(("arbitrary",), 5 * _nbytes((tr, d), F32)),
    )(*args)


def _loss_head(x, g, target, name):
    t, d = x.shape
    tr = _pick(t, ROW_TILE, BF16_SUBLANE)

    def body(x_ref, g_ref, t_ref, loss_ref, dx_ref, dxb_ref, dg_ref):
        i = pl.program_id(0)
        xv = x_ref[...]
        gv = g_ref[...]
        r = lax.rsqrt(jnp.mean(xv * xv, axis=-1, keepdims=True) + RMS_EPS)
        xh = xv * r
        diff = xh * gv - t_ref[...]
        part = 0.5 * jnp.sum(jnp.mean(diff * diff, axis=-1, keepdims=True), axis=0, keepdims=True)
        dyv = diff * (1.0 / d)
        dxh = dyv * gv
        dx = r * (dxh - xh * jnp.mean(dxh * xh, axis=-1, keepdims=True))
        dx_ref[...] = dx
        dxb_ref[...] = dx.astype(BF16)

        @pl.when(i == 0)
        def _():
            dg_ref[...] = jnp.zeros_like(dg_ref)
            loss_ref[...] = jnp.zeros_like(loss_ref)

        dg_ref[...] += jnp.sum(dyv * xh, axis=0, keepdims=True)
        loss_ref[...] += jnp.broadcast_to(part, loss_ref.shape)

    row = pl.BlockSpec((tr, d), lambda i: (i, 0))
    vec = pl.BlockSpec((1, d), lambda i: (0, 0))
    return pl.pallas_call(
        body, name=name, grid=(t // tr,),
        out_shape=(jax.ShapeDtypeStruct((1, LANE), F32), jax.ShapeDtypeStruct((t, d), F32),
                   jax.ShapeDtypeStruct((t, d), BF16), jax.ShapeDtypeStruct((1, d), F32)),
        in_specs=[row, vec, row],
        out_specs=(pl.BlockSpec((1, LANE), lambda i: (0, 0)), row, row, vec),
        compiler_params=_params(("arbitrary",), 5 * _nbytes((tr, d), F32)),
    )(x, g.reshape(1, d), target)


HALO = 8


def _conv_fwd(proj, conv_w, name):
    t = proj.shape[0]
    tr = _pick(t, ROW_TILE, BF16_SUBLANE)
    c0 = COL_CONV // BRANCH
    hb = tr // HALO

    def body(cb_ref, cc_ref, cu_ref, hc_ref, hu_ref, w_ref, y_ref):
        i = pl.program_id(0)
        z = cc_ref[...] * cu_ref[...]
        hz = jnp.where(i > 0, hc_ref[...] * hu_ref[...], 0.0)
        zf = jnp.concatenate([hz, z], axis=0)
        z1 = pltpu.roll(zf, 1, 0)[HALO:]
        z2 = pltpu.roll(zf, 2, 0)[HALO:]
        y = w_ref[2:3, :] * z + w_ref[1:2, :] * z1 + w_ref[0:1, :] * z2
        y_ref[...] = (cb_ref[...] * y).astype(BF16)

    def col(c):
        return pl.BlockSpec((tr, BRANCH), lambda i, c=c: (i, c0 + c))

    def prev(c):
        return pl.BlockSpec((HALO, BRANCH), lambda i, c=c: (jnp.maximum(i * hb - 1, 0), c0 + c))

    return pl.pallas_call(
        body, name=name, grid=(t // tr,),
        out_shape=jax.ShapeDtypeStruct((t, BRANCH), BF16),
        in_specs=[col(0), col(1), col(2), prev(1), prev(2), pl.BlockSpec((3, BRANCH), lambda i: (0, 0))],
        out_specs=pl.BlockSpec((tr, BRANCH), lambda i: (i, 0)),
        compiler_params=_params(("parallel",), 6 * _nbytes((tr, BRANCH), F32)),
    )(proj, proj, proj, proj, proj, conv_w)


def _conv_bwd(proj, conv_w, dout, name):
    t = proj.shape[0]
    tr = _pick(t, ROW_TILE, BF16_SUBLANE)
    nblk = t // tr
    c0 = COL_CONV // BRANCH
    hb = tr // HALO
    last_halo = t // HALO - 1

    def body(cb_ref, cc_ref, cu_ref, hc_ref, hu_ref, do_ref, ndo_ref, ncb_ref, w_ref,
             dcb_ref, dcc_ref, dcu_ref, dw_ref):
        i = pl.program_id(0)
        cb, cc, cu = cb_ref[...], cc_ref[...], cu_ref[...]
        w0, w1, w2 = w_ref[0:1, :], w_ref[1:2, :], w_ref[2:3, :]
        z = cc * cu
        hz = jnp.where(i > 0, hc_ref[...] * hu_ref[...], 0.0)
        zf = jnp.concatenate([hz, z], axis=0)
        z1 = pltpu.roll(zf, 1, 0)[HALO:]
        z2 = pltpu.roll(zf, 2, 0)[HALO:]
        y = w2 * z + w1 * z1 + w0 * z2
        dout_v = do_ref[...]
        dyc = dout_v * cb
        hdy = jnp.where(i < nblk - 1, ndo_ref[...] * ncb_ref[...], 0.0)
        dyf = jnp.concatenate([dyc, hdy], axis=0)
        dy1 = pltpu.roll(dyf, tr + HALO - 1, 0)[:tr]
        dy2 = pltpu.roll(dyf, tr + HALO - 2, 0)[:tr]
        dz = w2 * dyc + w1 * dy1 + w0 * dy2
        dcb_ref[...] = (dout_v * y).astype(BF16)
        dcc_ref[...] = (dz * cu).astype(BF16)
        dcu_ref[...] = (dz * cc).astype(BF16)

        @pl.when(i == 0)
        def _():
            dw_ref[...] = jnp.zeros_like(dw_ref)

        dw_ref[0:1, :] += jnp.sum(dyc * z2, axis=0, keepdims=True)
        dw_ref[1:2, :] += jnp.sum(dyc * z1, axis=0, keepdims=True)
        dw_ref[2:3, :] += jnp.sum(dyc * z, axis=0, keepdims=True)

    def col(c):
        return pl.BlockSpec((tr, BRANCH), lambda i, c=c: (i, c0 + c))

    def prev(c):
        return pl.BlockSpec((HALO, BRANCH), lambda i, c=c: (jnp.maximum(i * hb - 1, 0), c0 + c))

    def nxt(c):
        return pl.BlockSpec((HALO, BRANCH), lambda i, c=c: (jnp.minimum((i + 1) * hb, last_halo), c))

    row = pl.BlockSpec((tr, BRANCH), lambda i: (i, 0))
    return pl.pallas_call(
        body, name=name, grid=(nblk,),
        out_shape=(jax.ShapeDtypeStruct((t, BRANCH), BF16),) * 3 + (jax.ShapeDtypeStruct((HALO, BRANCH), F32),),
        in_specs=[col(0), col(1), col(2), prev(1), prev(2), row, nxt(0), nxt(c0),
                  pl.BlockSpec((3, BRANCH), lambda i: (0, 0))],
        out_specs=(row, row, row, pl.BlockSpec((HALO, BRANCH), lambda i: (0, 0))),
        compiler_params=_params(("arbitrary",), 8 * _nbytes((tr, BRANCH), F32)),
    )(proj, proj, proj, proj, proj, dout, dout, proj, conv_w)


def _tri(lower):
    r = lax.broadcasted_iota(jnp.int32, (LANE, LANE), 0)
    c = lax.broadcasted_iota(jnp.int32, (LANE, LANE), 1)
    return jnp.where((c <= r) if lower else (c >= r), 1.0, 0.0).astype(F32)


def _logf_cumsum(proj, fbias_row, name):
    t = proj.shape[0]
    nchunk = t // LANE

    def body(f_ref, b_ref, c_ref, run_sc):
        tri = _tri(True)
        run_sc[...] = jnp.zeros_like(run_sc)

        @pl.loop(0, nchunk)
        def _(i):
            rows = pl.ds(pl.multiple_of(i * LANE, LANE), LANE)
            z = f_ref[rows, :] + b_ref[...]
            logf = jnp.minimum(z, 0.0) - jnp.log(1.0 + jnp.exp(-jnp.abs(z)))
            cs = lax.dot_general(tri, logf, (NN, ((), ())), precision=lax.Precision.HIGHEST,
                                 preferred_element_type=F32) + run_sc[0:1, :]
            c_ref[rows, :] = cs
            run_sc[0:1, :] = cs[LANE - 1:LANE, :]

    return pl.pallas_call(
        body, name=name, grid=(1,),
        out_shape=jax.ShapeDtypeStruct((t, LANE), F32),
        in_specs=[pl.BlockSpec((t, LANE), lambda i: (0, COL_FG // LANE)), pl.BlockSpec((1, LANE), lambda i: (0, 0))],
        out_specs=pl.BlockSpec((t, LANE), lambda i: (0, 0)),
        scratch_shapes=[pltpu.VMEM((8, LANE), F32)],
        compiler_params=_params(("arbitrary",), 2 * _nbytes((t, LANE), F32)),
    )(proj, fbias_row)


def _logf_cumsum_bwd(proj, fbias_row, pieces, name):
    t = proj.shape[0]
    tb = _pick(t, 2 * ROW_TILE, LANE)
    nblk = t // tb
    npiece = len(pieces)

    def body(*refs):
        f_ref, b_ref = refs[:2]
        piece_refs = refs[2:2 + npiece]
        df_ref, db_ref, run_sc = refs[2 + npiece:]
        i = pl.program_id(0)
        tri = _tri(False)

        @pl.when(i == 0)
        def _():
            run_sc[...] = jnp.zeros_like(run_sc)
            db_ref[...] = jnp.zeros_like(db_ref)

        for c in reversed(range(tb // LANE)):
            rows = slice(c * LANE, (c + 1) * LANE)
            slabs = [p_ref[n, rows, :] for p_ref in piece_refs for n in range(p_ref.shape[0])]
            dcc = slabs[0]
            for slab in slabs[1:]:
                dcc = dcc + slab
            ss = lax.dot_general(tri, dcc, (NN, ((), ())), precision=lax.Precision.HIGHEST,
                                 preferred_element_type=F32) + run_sc[0:1, :]
            z = f_ref[rows, :] + b_ref[...]
            dz = ss * (1.0 / (1.0 + jnp.exp(z)))
            df_ref[rows, :] = dz.astype(BF16)
            run_sc[0:1, :] = ss[0:1, :]
            db_ref[...] += jnp.sum(dz, axis=0, keepdims=True)

    piece_specs = [pl.BlockSpec((p.shape[0], tb, LANE), lambda i: (0, nblk - 1 - i, 0)) for p in pieces]
    nslab = sum(p.shape[0] for p in pieces)
    return pl.pallas_call(
        body, name=name, grid=(nblk,),
        out_shape=(jax.ShapeDtypeStruct((t, LANE), BF16), jax.ShapeDtypeStruct((1, LANE), F32)),
        in_specs=[pl.BlockSpec((tb, LANE), lambda i: (nblk - 1 - i, COL_FG // LANE)),
                  pl.BlockSpec((1, LANE), lambda i: (0, 0))] + piece_specs,
        out_specs=(pl.BlockSpec((tb, LANE), lambda i: (nblk - 1 - i, 0)), pl.BlockSpec((1, LANE), lambda i: (0, 0))),
        scratch_shapes=[pltpu.VMEM((8, LANE), F32)],
        compiler_params=_params(("arbitrary",), (4 + nslab) * _nbytes((tb, LANE), F32)),
    )(proj, fbias_row, *pieces)


def _lo_mask():
    return lax.broadcasted_iota(jnp.int32, (1, LANE), 1) < HEAD


def _causal_steps(n, key_major):
    if key_major:
        pairs = [(iq, ik) for ik in range(n) for iq in range(ik, n)]
    else:
        pairs = [(iq, ik) for iq in range(n) for ik in range(iq + 1)]
    return (jnp.asarray([p[0] for p in pairs], jnp.int32), jnp.asarray([p[1] for p in pairs], jnp.int32))


def _head_lanes(j, pair_vals):
    lane = lax.broadcasted_iota(jnp.int32, (1, LANE), 1)
    return jnp.where(lane == 2 * j, pair_vals[0], 0.0) + jnp.where(lane == 2 * j + 1, pair_vals[1], 0.0)


def _fox_fwd(proj, c_col, c_row, name):
    t = proj.shape[0]
    tq = _pick(t, FOX_TILE, LANE)
    nq = t // tq
    rep = tq // LANE
    scale = HEAD ** -0.5
    cq, ck, cv = COL_FOX // LANE, COL_FOX // LANE + 4, COL_FOX // LANE + 8
    q_tab, k_tab = _causal_steps(nq, False)

    def body(qt_ref, kt_ref, q_ref, k_ref, v_ref, ck_ref, cqr_ref, y_ref, lse_ref, m_sc, l_sc, acc_sc):
        step_id = pl.program_id(1)
        iq, ik = qt_ref[step_id], kt_ref[step_id]
        lo = _lo_mask()
        lo_rows = lax.broadcasted_iota(jnp.int32, (LANE, 1), 0) < HEAD

        @pl.when(ik == 0)
        def _():
            m_sc[...] = jnp.full(m_sc.shape, NEG, F32)
            l_sc[...] = jnp.zeros_like(l_sc)
            acc_sc[...] = jnp.zeros_like(acc_sc)

        def step(diag):
            q2 = (q_ref[...] * scale).astype(BF16)
            k2 = k_ref[...].astype(BF16)
            v2 = v_ref[...].astype(BF16)
            alphas, adds = [], []
            for h in range(2):
                msk = lo if h == 0 else jnp.logical_not(lo)
                kh = jnp.where(msk, k2, jnp.zeros_like(k2))
                vh = jnp.where(msk, v2, jnp.zeros_like(v2))
                st = _dot(kh, q2, NT) + cqr_ref[h] - jnp.tile(ck_ref[h], (1, rep))
                if diag:
                    krow = lax.broadcasted_iota(jnp.int32, (tq, tq), 0)
                    qcol = lax.broadcasted_iota(jnp.int32, (tq, tq), 1)
                    st = jnp.where(krow <= qcol, st, NEG)
                m_prev = m_sc[h]
                m_new = jnp.maximum(m_prev, jnp.max(st, axis=0, keepdims=True))
                alpha = jnp.exp(m_prev - m_new)
                pt = jnp.exp(st - m_new)
                l_sc[h] = alpha * l_sc[h] + jnp.sum(pt, axis=0, keepdims=True)
                m_sc[h] = m_new
                alphas.append(alpha)
                adds.append(_dot(vh, pt.astype(BF16), TN))
            acc_sc[...] = acc_sc[...] * jnp.where(lo_rows, alphas[0], alphas[1]) + (adds[0] + adds[1])

        @pl.when(ik < iq)
        def _():
            step(False)

        @pl.when(ik == iq)
        def _():
            step(True)
            yt = acc_sc[...] / jnp.where(lo_rows, l_sc[0], l_sc[1])
            y_ref[...] = yt.T.astype(BF16)
            lse_ref[...] = m_sc[...] + jnp.log(l_sc[...])

    def kv(c):
        return pl.BlockSpec((tq, LANE), lambda j, s, qt, kt, c=c: (kt[s], c + j))

    qrow = pl.BlockSpec((2, 1, tq), lambda j, s, qt, kt: (j, 0, qt[s]))
    grid_spec = pltpu.PrefetchScalarGridSpec(
        num_scalar_prefetch=2, grid=(4, int(q_tab.shape[0])),
        in_specs=[pl.BlockSpec((tq, LANE), lambda j, s, qt, kt: (qt[s], cq + j)), kv(ck), kv(cv),
                  pl.BlockSpec((2, tq, LANE), lambda j, s, qt, kt: (j, kt[s], 0)), qrow],
        out_specs=(pl.BlockSpec((tq, LANE), lambda j, s, qt, kt: (qt[s], j)), qrow),
        scratch_shapes=[pltpu.VMEM((2, 1, tq), F32), pltpu.VMEM((2, 1, tq), F32), pltpu.VMEM((LANE, tq), F32)])
    return pl.pallas_call(
        body, name=name, grid_spec=grid_spec,
        out_shape=(jax.ShapeDtypeStruct((t, BRANCH), BF16), jax.ShapeDtypeStruct((8, 1, t), F32)),
        compiler_params=_params(("parallel", "arbitrary"),
                                16 * _nbytes((tq, LANE), F32) + 6 * _nbytes((tq, tq), F32)),
    )(q_tab, k_tab, proj, proj, proj, c_col, c_row)


def _fox_bwd(proj, c_col, c_row, lse_row, y, dy, name):
    t = proj.shape[0]
    tb = _pick(t, FOX_TILE, LANE)
    nb = t // tb
    rep = tb // LANE
    scale = HEAD ** -0.5
    cq, ck, cv = COL_FOX // LANE, COL_FOX // LANE + 4, COL_FOX // LANE + 8
    q_tab, k_tab = _causal_steps(nb, True)
    nsteps = int(q_tab.shape[0])

    def body(qt_ref, kt_ref, k_ref, v_ref, q_ref, y_ref, dy_ref, ck_ref, cqr_ref, lser_ref,
             dq_ref, dk_ref, dv_ref, dck_ref, dcq_ref, dk_sc, dv_sc, dc_sc, dqt_sc, dcq_sc, d_sc):
        j, step_id = pl.program_id(0), pl.program_id(1)
        iq, ik = qt_ref[step_id], kt_ref[step_id]
        lo = _lo_mask()

        @pl.when(step_id == 0)
        def _():
            dqt_sc[...] = jnp.zeros_like(dqt_sc)
            dcq_sc[...] = jnp.zeros_like(dcq_sc)

        @pl.when(iq == ik)
        def _():
            dk_sc[...] = jnp.zeros_like(dk_sc)
            dv_sc[...] = jnp.zeros_like(dv_sc)
            dc_sc[...] = jnp.zeros_like(dc_sc)

        @pl.when(ik == 0)
        def _():
            prod = y_ref[...].astype(F32) * dy_ref[...].astype(F32)
            row = lax.broadcasted_iota(jnp.int32, (8, LANE), 0)
            sel = jnp.logical_or(jnp.logical_and(row == 0, lo), jnp.logical_and(row == 1, jnp.logical_not(lo)))
            d_sc[iq] = lax.dot_general(jnp.where(sel, 1.0, 0.0).astype(F32), prod, (NT, ((), ())),
                                       precision=lax.Precision.HIGHEST, preferred_element_type=F32)

        def step(diag):
            k2 = k_ref[...].astype(BF16)
            v2 = v_ref[...].astype(BF16)
            q2 = (q_ref[...] * scale).astype(BF16)
            do2 = dy_ref[...]
            d_rows = d_sc[iq]
            for h in range(2):
                msk = lo if h == 0 else jnp.logical_not(lo)
                kh = jnp.where(msk, k2, jnp.zeros_like(k2))
                vh = jnp.where(msk, v2, jnp.zeros_like(v2))
                st = _dot(kh, q2, NT) + (cqr_ref[h] - lser_ref[h]) - jnp.tile(ck_ref[h], (1, rep))
                if diag:
                    krow = lax.broadcasted_iota(jnp.int32, (tb, tb), 0)
                    qcol = lax.broadcasted_iota(jnp.int32, (tb, tb), 1)
                    st = jnp.where(krow <= qcol, st, NEG)
                pt = jnp.exp(st)
                dpt = _dot(vh, do2, NT)
                dst = pt * (dpt - d_rows[h:h + 1, :])
                dsb = dst.astype(BF16)
                dv_sc[h] += _dot(pt.astype(BF16), do2, NN)
                dk_sc[h] += _dot(dsb, q2, NN)
                dc_sc[h] -= jnp.sum(dst, axis=1, keepdims=True)
                dqt_sc[iq] += _dot(kh, dsb, TN)
                dcq_sc[h, iq] += jnp.sum(dst, axis=0, keepdims=True)

        @pl.when(iq > ik)
        def _():
            step(False)

        @pl.when(iq == ik)
        def _():
            step(True)

        @pl.when(iq == nb - 1)
        def _():
            dk_ref[...] = jnp.where(lo, dk_sc[0], dk_sc[1]).astype(BF16)
            dv_ref[...] = jnp.where(lo, dv_sc[0], dv_sc[1]).astype(BF16)
            dck_ref[...] = _head_lanes(j, dc_sc)

        @pl.when(step_id == nsteps - 1)
        def _():
            for i in range(nb):
                dq_ref[i * tb:(i + 1) * tb, :] = (dqt_sc[i].T * scale).astype(BF16)
                for h in range(2):
                    dcq_ref[h, :, i * tb:(i + 1) * tb] = dcq_sc[h, i]

    def kcol(c):
        return pl.BlockSpec((tb, LANE), lambda j, s, qt, kt, c=c: (kt[s], c + j))

    qrow = pl.BlockSpec((2, 1, tb), lambda j, s, qt, kt: (j, 0, qt[s]))
    pair_q = pl.BlockSpec((tb, LANE), lambda j, s, qt, kt: (qt[s], j))
    pair_k = pl.BlockSpec((tb, LANE), lambda j, s, qt, kt: (kt[s], j))
    grid_spec = pltpu.PrefetchScalarGridSpec(
        num_scalar_prefetch=2, grid=(4, nsteps),
        in_specs=[kcol(ck), kcol(cv), pl.BlockSpec((tb, LANE), lambda j, s, qt, kt: (qt[s], cq + j)), pair_q, pair_q,
                  pl.BlockSpec((2, tb, LANE), lambda j, s, qt, kt: (j, kt[s], 0)), qrow, qrow],
        out_specs=(pl.BlockSpec((t, LANE), lambda j, s, qt, kt: (0, j)), pair_k, pair_k,
                   pl.BlockSpec((None, tb, LANE), lambda j, s, qt, kt: (j, kt[s], 0)),
                   pl.BlockSpec((2, 1, t), lambda j, s, qt, kt: (j, 0, 0))),
        scratch_shapes=[pltpu.VMEM((2, tb, LANE), F32)] * 3
        + [pltpu.VMEM((nb, LANE, tb), F32), pltpu.VMEM((2, nb, 1, tb), F32), pltpu.VMEM((nb, 8, tb), F32)])
    return pl.pallas_call(
        body, name=name, grid_spec=grid_spec,
        out_shape=(jax.ShapeDtypeStruct((t, BRANCH), BF16), jax.ShapeDtypeStruct((t, BRANCH), BF16),
                   jax.ShapeDtypeStruct((t, BRANCH), BF16), jax.ShapeDtypeStruct((4, t, LANE), F32),
                   jax.ShapeDtypeStruct((8, 1, t), F32)),
        compiler_params=_params(("parallel", "arbitrary"),
                                24 * _nbytes((tb, LANE), F32) + 8 * _nbytes((tb, tb), F32)
                                + 2 * _nbytes((t, LANE), F32)),
    )(q_tab, k_tab, proj, proj, proj, y, dy, c_col, c_row, lse_row)


def _swa_tables(rel_bias):
    tq = np.arange(SWA_BLOCK)[:, None]
    sk = np.arange(2 * SWA_BLOCK)[None, :]
    dist = SWA_BLOCK + tq - sk
    inwin = (dist >= 0) & (dist < SWA_BLOCK)
    n = np.maximum(dist, 0)
    max_exact = N_BUCKETS // 2
    large = max_exact + (np.log(np.maximum(n, 1).astype(np.float32) / max_exact)
                         / math.log(SWA_BLOCK / max_exact) * (N_BUCKETS - max_exact)).astype(np.int32)
    bucket = np.where(n < max_exact, n, np.minimum(large, N_BUCKETS - 1))
    onehot = (bucket[..., None] == np.arange(N_BUCKETS)) & inwin[..., None]
    onehot = jnp.asarray(onehot.astype(np.float32))
    bias = jnp.einsum("tsb,bh->hts", onehot, rel_bias, precision=lax.Precision.HIGHEST)
    bias = jnp.where(jnp.asarray(inwin)[None], bias, NEG)
    return onehot, bias


def _swa_fwd(proj, bias, sink_rep, name):
    t = proj.shape[0]
    nb = t // SWA_BLOCK
    scale = HEAD ** -0.5
    csq, csk, csv = COL_SQ // 256, COL_SK // LANE, COL_SV // LANE

    def body(q_ref, kp_ref, kc_ref, vp_ref, vc_ref, b_ref, sk_ref, y_ref, lse_ref):
        kvh, n = pl.program_id(0), pl.program_id(1)
        lane = lax.broadcasted_iota(jnp.int32, (1, LANE), 1)
        lo = lane < HEAD
        kvm = jnp.logical_and(lane >= kvh * HEAD, lane < (kvh + 1) * HEAD)

        def both(prev_ref, cur_ref):
            band = jnp.concatenate([prev_ref[...], cur_ref[...]], axis=0)
            band = jnp.where(kvm, band, 0.0)
            return (band + pltpu.roll(band, HEAD, 1)).astype(BF16)

        kb, vb = both(kp_ref, kc_ref), both(vp_ref, vc_ref)
        col = lax.broadcasted_iota(jnp.int32, (SWA_BLOCK, 2 * SWA_BLOCK), 1)
        first = jnp.logical_and(n == 0, col < SWA_BLOCK)
        outs = []
        for g in range(SWA_GROUP):
            half = q_ref[:, (g // 2) * LANE:(g // 2 + 1) * LANE]
            hm = lo if g % 2 == 0 else jnp.logical_not(lo)
            qg = jnp.where(hm, half, 0.0).astype(BF16)
            s = _dot(qg, kb, NT) * scale + b_ref[g]
            s = jnp.where(first, NEG, s)
            snk = sk_ref[g:g + 1, :]
            m = jnp.maximum(jnp.max(s, axis=1, keepdims=True), snk)
            p = jnp.exp(s - jnp.tile(m, (1, 2)))
            denom = jnp.sum(p, axis=1, keepdims=True) + jnp.exp(snk - m)
            outs.append(_dot(p.astype(BF16), vb, NN) / denom)
            lse_ref[g] = m + jnp.log(denom)
        y_ref[:, 0:LANE] = jnp.where(lo, outs[0], outs[1]).astype(BF16)
        y_ref[:, LANE:2 * LANE] = jnp.where(lo, outs[2], outs[3]).astype(BF16)

    def blk(c, shift):
        return pl.BlockSpec((SWA_BLOCK, LANE), lambda kvh, n, c=c, s=shift: (jnp.maximum(n - s, 0), c))

    return pl.pallas_call(
        body, name=name, grid=(2, nb),
        out_shape=(jax.ShapeDtypeStruct((t, BRANCH), BF16), jax.ShapeDtypeStruct((8, t, LANE), F32)),
        in_specs=[pl.BlockSpec((SWA_BLOCK, 256), lambda kvh, n: (n, csq + kvh)),
                  blk(csk, 1), blk(csk, 0), blk(csv, 1), blk(csv, 0),
                  pl.BlockSpec((None, SWA_GROUP, SWA_BLOCK, 256), lambda kvh, n: (kvh, 0, 0, 0)),
                  pl.BlockSpec((None, SWA_GROUP, LANE), lambda kvh, n: (kvh, 0, 0))],
        out_specs=(pl.BlockSpec((SWA_BLOCK, 256), lambda kvh, n: (n, kvh)),
                   pl.BlockSpec((SWA_GROUP, SWA_BLOCK, LANE), lambda kvh, n: (kvh, n, 0))),
        compiler_params=_params(("parallel", "arbitrary"), 4 << 20),
    )(proj, proj, proj, proj, proj, bias.reshape(2, SWA_GROUP, SWA_BLOCK, 256), sink_rep)


def _swa_bwd(proj, bias, sink_rep, lse, y, dy, name):
    t = proj.shape[0]
    nb = t // SWA_BLOCK
    scale = HEAD ** -0.5
    csq, csk, csv = COL_SQ // 256, COL_SK // LANE, COL_SV // LANE

    def body(q_ref, kp_ref, kc_ref, vp_ref, vc_ref, b_ref, sk_ref, lse_ref, y_ref, dy_ref,
             dq_ref, dkp_ref, dvp_ref, db_ref, dsk_ref):
        kvh, n = pl.program_id(0), pl.program_id(1)
        lane = lax.broadcasted_iota(jnp.int32, (1, LANE), 1)
        lo = lane < HEAD
        kvm = jnp.logical_and(lane >= kvh * HEAD, lane < (kvh + 1) * HEAD)

        def both(prev_ref, cur_ref):
            band = jnp.concatenate([prev_ref[...], cur_ref[...]], axis=0)
            band = jnp.where(kvm, band, 0.0)
            return (band + pltpu.roll(band, HEAD, 1)).astype(BF16)

        kb, vb = both(kp_ref, kc_ref), both(vp_ref, vc_ref)
        col = lax.broadcasted_iota(jnp.int32, (SWA_BLOCK, 2 * SWA_BLOCK), 1)
        first = jnp.logical_and(n == 0, col < SWA_BLOCK)

        @pl.when(n == 0)
        def _():
            db_ref[...] = jnp.zeros_like(db_ref)
            dsk_ref[...] = jnp.zeros_like(dsk_ref)

        dk_full = jnp.zeros((2 * SWA_BLOCK, LANE), F32)
        dv_full = jnp.zeros((2 * SWA_BLOCK, LANE), F32)
        dqs = []
        for g in range(SWA_GROUP):
            sl = slice((g // 2) * LANE, (g // 2 + 1) * LANE)
            hm = lo if g % 2 == 0 else jnp.logical_not(lo)
            qg = jnp.where(hm, q_ref[:, sl], 0.0).astype(BF16)
            dog = jnp.where(hm, dy_ref[:, sl], jnp.zeros((SWA_BLOCK, LANE), BF16))
            dmat = jnp.where(hm, y_ref[:, sl].astype(F32) * dy_ref[:, sl].astype(F32), 0.0)
            dg = jnp.sum(dmat, axis=1, keepdims=True)
            s = _dot(qg, kb, NT) * scale + b_ref[g]
            s = jnp.where(first, NEG, s)
            lse_g = lse_ref[g]
            p = jnp.exp(s - jnp.tile(lse_g, (1, 2)))
            dp = _dot(dog, vb, NT)
            ds = p * (dp - dg)
            dsb = ds.astype(BF16)
            dqs.append(_dot(dsb, kb, NN) * scale)
            dk_full = dk_full + _dot(dsb, qg, TN)
            dv_full = dv_full + _dot(p.astype(BF16), dog, TN)
            db_ref[g] += ds
            psink = jnp.exp(sk_ref[g:g + 1, :] - lse_g)
            dsk_ref[g:g + 1, :] -= jnp.sum(psink * dg, axis=0, keepdims=True)
        dq_ref[:, 0:LANE] = jnp.where(lo, dqs[0], dqs[1]).astype(BF16)
        dq_ref[:, LANE:2 * LANE] = jnp.where(lo, dqs[2], dqs[3]).astype(BF16)
        dkp_ref[...] = jnp.where(kvm, (dk_full + pltpu.roll(dk_full, HEAD, 1)) * scale, 0.0)
        dvp_ref[...] = jnp.where(kvm, dv_full + pltpu.roll(dv_full, HEAD, 1), 0.0)

    def blk(c, shift):
        return pl.BlockSpec((SWA_BLOCK, LANE), lambda kvh, n, c=c, s=shift: (jnp.maximum(n - s, 0), c))

    qblk = pl.BlockSpec((SWA_BLOCK, 256), lambda kvh, n: (n, kvh))
    part = pl.BlockSpec((None, None, 2 * SWA_BLOCK, LANE), lambda kvh, n: (kvh, n, 0, 0))
    bspec = pl.BlockSpec((None, SWA_GROUP, SWA_BLOCK, 256), lambda kvh, n: (kvh, 0, 0, 0))
    sspec = pl.BlockSpec((None, SWA_GROUP, LANE), lambda kvh, n: (kvh, 0, 0))
    return pl.pallas_call(
        body, name=name, grid=(2, nb),
        out_shape=(jax.ShapeDtypeStruct((t, BRANCH), BF16),
                   jax.ShapeDtypeStruct((2, nb, 2 * SWA_BLOCK, LANE), F32),
                   jax.ShapeDtypeStruct((2, nb, 2 * SWA_BLOCK, LANE), F32),
                   jax.ShapeDtypeStruct((2, SWA_GROUP, SWA_BLOCK, 256), F32),
                   jax.ShapeDtypeStruct((2, SWA_GROUP, LANE), F32)),
        in_specs=[pl.BlockSpec((SWA_BLOCK, 256), lambda kvh, n: (n, csq + kvh)),
                  blk(csk, 1), blk(csk, 0), blk(csv, 1), blk(csv, 0), bspec, sspec,
                  pl.BlockSpec((SWA_GROUP, SWA_BLOCK, LANE), lambda kvh, n: (kvh, n, 0)), qblk, qblk],
        out_specs=(qblk, part, part, bspec, sspec),
        compiler_params=_params(("parallel", "arbitrary"), 6 << 20),
    )(proj, proj, proj, proj, proj, bias.reshape(2, SWA_GROUP, SWA_BLOCK, 256), sink_rep, lse, y, dy)


def _gate_fwd(proj, pb, name):
    t = proj.shape[0]
    tr = _pick(t, ROW_TILE // 2, BF16_SUBLANE)

    def body(g0, g1, g2, p0, p1, p2, o_ref):
        acc = jax.nn.sigmoid(g0[...]) * p0[...]
        acc = acc + jax.nn.sigmoid(g1[...]) * p1[...]
        acc = acc + jax.nn.sigmoid(g2[...]) * p2[...]
        o_ref[...] = acc.astype(BF16)

    row = pl.BlockSpec((tr, D_MODEL), lambda i: (i, 0))
    gates = [pl.BlockSpec((tr, D_MODEL), lambda i, b=b: (i, b)) for b in range(3)]
    return pl.pallas_call(
        body, name=name, grid=(t // tr,),
        out_shape=jax.ShapeDtypeStruct((t, D_MODEL), BF16),
        in_specs=gates + [row] * 3, out_specs=row,
        compiler_params=_params(("parallel",), 7 * _nbytes((tr, D_MODEL), F32)),
    )(proj, proj, proj, *pb)


def _gate_bwd(proj, pb, dmerged, name):
    t = proj.shape[0]
    tr = _pick(t, ROW_TILE // 2, BF16_SUBLANE)

    def body(g0, g1, g2, p0, p1, p2, dm_ref, dp0, dp1, dp2, dg_ref):
        dm = dm_ref[...]
        for b, (g_ref, p_ref, dp_ref) in enumerate(((g0, p0, dp0), (g1, p1, dp1), (g2, p2, dp2))):
            sg = jax.nn.sigmoid(g_ref[...])
            dp_ref[...] = (dm * sg).astype(BF16)
            dg_ref[:, b * D_MODEL:(b + 1) * D_MODEL] = (dm * p_ref[...] * sg * (1.0 - sg)).astype(BF16)

    row = pl.BlockSpec((tr, D_MODEL), lambda i: (i, 0))
    gates = [pl.BlockSpec((tr, D_MODEL), lambda i, b=b: (i, b)) for b in range(3)]
    return pl.pallas_call(
        body, name=name, grid=(t // tr,),
        out_shape=(jax.ShapeDtypeStruct((t, D_MODEL), BF16),) * 3 + (jax.ShapeDtypeStruct((t, 3 * D_MODEL), BF16),),
        in_specs=gates + [row] * 4,
        out_specs=(row, row, row, pl.BlockSpec((tr, 3 * D_MODEL), lambda i: (i, 0))),
        compiler_params=_params(("parallel",), 11 * _nbytes((tr, D_MODEL), F32)),
    )(proj, proj, proj, *pb, dmerged)


def _swiglu_fwd(ab, name):
    t = ab.shape[0]
    tr = _pick(t, ROW_TILE, BF16_SUBLANE)
    tc = D_FF_P // 2

    def body(a_ref, b_ref, o_ref):
        a = a_ref[...]
        o_ref[...] = (a * jax.nn.sigmoid(a) * b_ref[...]).astype(BF16)

    return pl.pallas_call(
        body, name=name, grid=(t // tr, 2),
        out_shape=jax.ShapeDtypeStruct((t, D_FF_P), BF16),
        in_specs=[pl.BlockSpec((tr, tc), lambda i, j: (i, j)), pl.BlockSpec((tr, tc), lambda i, j: (i, j + 2))],
        out_specs=pl.BlockSpec((tr, tc), lambda i, j: (i, j)),
        compiler_params=_params(("parallel", "parallel"), 3 * _nbytes((tr, tc), F32)),
    )(ab, ab)


def _swiglu_bwd(ab, dh, name):
    t = ab.shape[0]
    tr = _pick(t, ROW_TILE, BF16_SUBLANE)
    tc = D_FF_P // 2

    def body(a_ref, b_ref, dh_ref, o_ref):
        jj = pl.program_id(1)
        a, b, d = a_ref[...], b_ref[...], dh_ref[...]
        sg = jax.nn.sigmoid(a)
        da = d * b * (sg * (1.0 + a * (1.0 - sg)))
        db = d * (a * sg)
        o_ref[...] = jnp.where(jj < 2, da, db).astype(BF16)

    return pl.pallas_call(
        body, name=name, grid=(t // tr, 4),
        out_shape=jax.ShapeDtypeStruct((t, 2 * D_FF_P), BF16),
        in_specs=[pl.BlockSpec((tr, tc), lambda i, j: (i, j % 2)),
                  pl.BlockSpec((tr, tc), lambda i, j: (i, j % 2 + 2)),
                  pl.BlockSpec((tr, tc), lambda i, j: (i, j % 2))],
        out_specs=pl.BlockSpec((tr, tc), lambda i, j: (i, j)),
        compiler_params=_params(("parallel", "parallel"), 4 * _nbytes((tr, tc), F32)),
    )(ab, ab, dh)


def _xattn_fwd(q, kv, name):
    t = q.shape[0]
    tq = _pick(t, ROW_TILE, BF16_SUBLANE)
    mlen = kv.shape[0]
    scale = X_HEAD ** -0.5

    def body(q_ref, kv_ref, o_ref):
        for h in range(X_HEADS):
            sl = slice(h * X_HEAD, (h + 1) * X_HEAD)
            kh = kv_ref[:, sl]
            vh = kv_ref[:, D_MODEL + h * X_HEAD:D_MODEL + (h + 1) * X_HEAD]
            s = _dot(q_ref[:, sl], kh, NT) * scale
            p = jnp.exp(s - jnp.max(s, axis=1, keepdims=True))
            l = jnp.sum(p, axis=1, keepdims=True)
            o_ref[:, sl] = (_dot(p.astype(BF16), vh, NN) / l).astype(BF16)

    return pl.pallas_call(
        body, name=name, grid=(t // tq,),
        out_shape=jax.ShapeDtypeStruct((t, D_MODEL), BF16),
        in_specs=[pl.BlockSpec((tq, D_MODEL), lambda i: (i, 0)), pl.BlockSpec((mlen, 2 * D_MODEL), lambda i: (0, 0))],
        out_specs=pl.BlockSpec((tq, D_MODEL), lambda i: (i, 0)),
        compiler_params=_params(("parallel",), 4 * _nbytes((tq, D_MODEL), F32)),
    )(q, kv)


def _xattn_bwd(q, kv, do, name):
    t = q.shape[0]
    tq = _pick(t, ROW_TILE, BF16_SUBLANE)
    mlen = kv.shape[0]
    scale = X_HEAD ** -0.5

    def body(q_ref, kv_ref, do_ref, dq_ref, dkv_ref):
        i = pl.program_id(0)

        @pl.when(i == 0)
        def _():
            dkv_ref[...] = jnp.zeros_like(dkv_ref)

        for h in range(X_HEADS):
            sl = slice(h * X_HEAD, (h + 1) * X_HEAD)
            vsl = slice(D_MODEL + h * X_HEAD, D_MODEL + (h + 1) * X_HEAD)
            qh, kh, vh, doh = q_ref[:, sl], kv_ref[:, sl], kv_ref[:, vsl], do_ref[:, sl]
            s = _dot(qh, kh, NT) * scale
            p = jnp.exp(s - jnp.max(s, axis=1, keepdims=True))
            p = p / jnp.sum(p, axis=1, keepdims=True)
            dp = _dot(doh, vh, NT)
            ds = p * (dp - jnp.sum(p * dp, axis=1, keepdims=True))
            dsb = ds.astype(BF16)
            dq_ref[:, sl] = (_dot(dsb, kh, NN) * scale).astype(BF16)
            dkv_ref[:, sl] += _dot(dsb, qh, TN) * scale
            dkv_ref[:, vsl] += _dot(p.astype(BF16), doh, TN)

    row = pl.BlockSpec((tq, D_MODEL), lambda i: (i, 0))
    whole = pl.BlockSpec((mlen, 2 * D_MODEL), lambda i: (0, 0))
    return pl.pallas_call(
        body, name=name, grid=(t // tq,),
        out_shape=(jax.ShapeDtypeStruct((t, D_MODEL), BF16), jax.ShapeDtypeStruct((mlen, 2 * D_MODEL), F32)),
        in_specs=[row, whole, row], out_specs=(row, whole),
        compiler_params=_params(("arbitrary",), 6 * _nbytes((tq, D_MODEL), F32)),
    )(q, kv, do)


def _position():
    return lax.axis_index("x"), lax.axis_index("y"), lax.axis_index("c")


N_PEER = N_DEV - 1


def _all_gather(xs, name):
    n = len(xs)

    def body(*refs):
        x_refs, out_refs = refs[:n], refs[n:2 * n]
        send_sems, recv_sems, local_sems = refs[2 * n:]
        mx, my, mc = _position()
        me, sib = (mx, my, mc), (mx, my, 1 - mc)
        chips = [(1 - mx, my), (mx, 1 - my), (1 - mx, 1 - my)]

        def slot(i, p):
            return out_refs[i].at[4 * p[0] + 2 * p[1] + p[2]]

        def copy(i, k, block, to, src=None):
            return pltpu.make_async_remote_copy(
                src_ref=slot(i, block) if src is None else src, dst_ref=slot(i, block),
                send_sem=send_sems.at[i * N_PEER + k], recv_sem=recv_sems.at[i * N_PEER + k],
                device_id=to, device_id_type=MESH)

        mine = [pltpu.make_async_copy(x_refs[i], slot(i, me), local_sems.at[i]) for i in range(n)]
        for cp in mine:
            cp.start()
        first = [copy(i, 1 + j, me, (*chip, mc), src=x_refs[i]) for j, chip in enumerate(chips) for i in range(n)]
        first += [copy(i, 0, me, sib, src=x_refs[i]) for i in range(n)]
        for cp in first:
            cp.start()
        passed = []
        for j, chip in enumerate(chips):
            for i in range(n):
                copy(i, 1 + j, (*chip, mc), me).wait_recv()
                passed.append(copy(i, 4 + j, (*chip, mc), sib))
                passed[-1].start()
        for i in range(n):
            copy(i, 0, sib, me).wait_recv()
        for j, chip in enumerate(chips):
            for i in range(n):
                copy(i, 4 + j, (*chip, 1 - mc), me).wait_recv()
        for cp in first + passed:
            cp.wait_send()
        for cp in mine:
            cp.wait()

    return pl.pallas_call(
        body, name=name,
        out_shape=tuple(jax.ShapeDtypeStruct((N_DEV,) + x.shape, x.dtype) for x in xs),
        in_specs=[pl.BlockSpec(memory_space=pl.ANY)] * n, out_specs=(pl.BlockSpec(memory_space=pl.ANY),) * n,
        scratch_shapes=[pltpu.SemaphoreType.DMA((n * N_PEER,)), pltpu.SemaphoreType.DMA((n * N_PEER,)),
                        pltpu.SemaphoreType.DMA((n,))],
    )(*xs)


PEER_RELS = [(dx, dy, dc) for dx in (0, 1) for dy in (0, 1) for dc in (0, 1)][1:]


def _peer_copy(rel_k, i, src_refs, land_refs, send_sems, recv_sems, layer, gather, arriving):
    mx, my, mc = _position()
    me_idx = 4 * mx + 2 * my + mc
    p = tuple((1 - v) if f else v for f, v in zip(PEER_RELS[rel_k], (mx, my, mc)))
    p_idx = 4 * p[0] + 2 * p[1] + p[2]
    src_slot, dst_slot = (me_idx, p_idx) if arriving else (p_idx, me_idx)
    src = src_refs[i] if gather else src_refs[i].at[src_slot]
    dst = land_refs[i].at[dst_slot] if layer is None else land_refs[i].at[dst_slot, layer]
    return pltpu.make_async_remote_copy(
        src_ref=src, dst_ref=dst, send_sem=send_sems.at[i * N_PEER + rel_k], recv_sem=recv_sems.at[i * N_PEER + rel_k],
        device_id=p, device_id_type=MESH)


HBM_SPEC = pl.BlockSpec(memory_space=pltpu.HBM)
SEM_SPEC = pl.BlockSpec(memory_space=pltpu.SEMAPHORE)
SIDE_EFFECT = pltpu.SideEffectType.DATAFLOW_SIDE_EFFECTING


def _own_slots(srcs, lands, layer, gather):
    mx, my, mc = _position()
    me_idx = 4 * mx + 2 * my + mc
    out = []
    for s, land in zip(srcs, lands):
        piece = s[None] if gather else lax.dynamic_index_in_dim(s, me_idx, 0, keepdims=True)
        if layer is None:
            start = (me_idx,) + (0,) * (land.ndim - 1)
        else:
            piece, start = piece[:, None], (me_idx, layer) + (0,) * (land.ndim - 2)
        out.append(lax.dynamic_update_slice(land, piece, start))
    return out


def _swap_start(srcs, lands, layer, gather, name):
    n = len(srcs)

    def body(*refs):
        src_refs, land_refs = refs[:n], refs[n:2 * n]
        send_sems, recv_sems = refs[2 * n], refs[2 * n + 1]
        token = refs[4 * n + 2]
        for i in range(n):
            for k in range(N_PEER):
                _peer_copy(k, i, src_refs, land_refs, send_sems, recv_sems, layer, gather, False).start()
        token[...] = jnp.zeros_like(token)

    hbm = [pltpu.with_memory_space_constraint(a, pltpu.HBM) for a in list(srcs) + list(lands)]
    out = pl.pallas_call(
        body, name=name,
        out_shape=(pltpu.SemaphoreType.DMA((n * N_PEER,)), pltpu.SemaphoreType.DMA((n * N_PEER,)))
        + tuple(pltpu.HBM(a.shape, a.dtype) for a in hbm) + (jax.ShapeDtypeStruct((8, LANE), F32),),
        in_specs=[HBM_SPEC] * (2 * n),
        out_specs=(SEM_SPEC, SEM_SPEC) + (HBM_SPEC,) * (2 * n) + (pl.BlockSpec(memory_space=pltpu.VMEM),),
        input_output_aliases={i: 2 + i for i in range(2 * n)},
        compiler_params=pltpu.CompilerParams(has_side_effects=SIDE_EFFECT),
    )(*hbm)
    return out[0], out[1], list(out[2:2 + n]), list(out[2 + n:2 + 2 * n]), out[2 + 2 * n]


def _swap_wait(send_sems, recv_sems, srcs, lands, after, layer, gather, name):
    n = len(srcs)

    def body(*refs):
        src_refs, land_refs = refs[:n], refs[n:2 * n]
        send_sems_ref, recv_sems_ref = refs[2 * n], refs[2 * n + 1]
        for i in range(n):
            for k in range(N_PEER):
                args = (src_refs, land_refs, send_sems_ref, recv_sems_ref, layer, gather)
                _peer_copy(k, i, *args, False).wait_send()
                _peer_copy(k, i, *args, True).wait_recv()

    out = pl.pallas_call(
        body, name=name,
        out_shape=tuple(pltpu.HBM(a.shape, a.dtype) for a in list(srcs) + list(lands)),
        in_specs=[HBM_SPEC] * (2 * n) + [SEM_SPEC, SEM_SPEC, pl.BlockSpec(memory_space=pl.ANY)],
        out_specs=(HBM_SPEC,) * (2 * n),
        input_output_aliases={i: i for i in range(2 * n)},
        compiler_params=pltpu.CompilerParams(has_side_effects=SIDE_EFFECT),
    )(*srcs, *lands, send_sems, recv_sems, after)
    return list(out[n:])


ADAMW_BLOCK_BYTES = 1 << 20


def _adamw(parts, w, m, v, name):
    r, l = w.shape
    tr = _pick(r, max(ADAMW_BLOCK_BYTES // (4 * l), BF16_SUBLANE), BF16_SUBLANE)
    c1 = 1.0 - ADAM_B1 ** ADAM_STEP
    c2 = 1.0 - ADAM_B2 ** ADAM_STEP

    def body(p_ref, w_ref, m_ref, v_ref, g_ref, d_ref, nm_ref, nv_ref):
        g = p_ref[0].astype(F32)
        for s in range(1, N_DEV):
            g = g + p_ref[s].astype(F32)
        nm = ADAM_B1 * m_ref[...] + (1.0 - ADAM_B1) * g
        nv = ADAM_B2 * v_ref[...] + (1.0 - ADAM_B2) * (g * g)
        m_hat = nm / c1
        v_hat = nv / c2
        g_ref[...] = g
        d_ref[...] = -ADAM_LR * (m_hat / (jnp.sqrt(v_hat) + ADAM_EPS) + ADAM_WD * w_ref[...])
        nm_ref[...] = nm
        nv_ref[...] = nv

    row = pl.BlockSpec((tr, l), lambda i: (i, 0))
    return pl.pallas_call(
        body, name=name, grid=(r // tr,),
        out_shape=(jax.ShapeDtypeStruct((r, l), F32),) * 4,
        in_specs=[pl.BlockSpec((N_DEV, tr, l), lambda i: (0, i, 0)), row, row, row],
        out_specs=(row,) * 4,
        compiler_params=_params(("parallel",), 12 * _nbytes((tr, l), F32)),
    )(parts, w, m, v)


MATRIX_WEIGHTS = (("w_in", 2), ("conv_w", 2), ("w_branch", 3), ("w_mix_out", 1), ("w_xq", 1), ("w_xkv", 2),
                  ("w_xo", 1), ("w_ffn_gate", 2), ("w_ffn_up", 2), ("w_ffn_down", 1))
SMALL_PARAMS = ("mix_norm_g", "xattn_norm_g", "mem_norm_g", "ffn_norm_g", "final_norm_g", "forget_bias", "sink",
                "rel_bias")


def _pack_small(pieces):
    flat = jnp.concatenate([p.astype(F32).reshape(-1) for p in pieces])
    total = -(-flat.shape[0] // (8 * LANE)) * (8 * LANE)
    return jnp.pad(flat, (0, total - flat.shape[0])).reshape(total // LANE, LANE)


def _rows(a):
    return a.reshape(-1, a.shape[-1])


def _to_full(gathered, axis):
    moved = jnp.moveaxis(gathered, 0, axis)
    shape = list(moved.shape)
    shape[axis:axis + 2] = [shape[axis] * shape[axis + 1]]
    return moved.reshape(shape)


def _to_blocks(full, axis):
    shape = list(full.shape)
    shape[axis:axis + 1] = [N_DEV, shape[axis] // N_DEV]
    return jnp.moveaxis(full.reshape(shape), axis, 0)


def _perm_in(w_in):
    pad = jnp.zeros((w_in.shape[0], PROJ_COLS - IN_COLS), w_in.dtype)
    return jnp.concatenate([w_in[:, 3848:6920], w_in[:, 0:3072], w_in[:, 3080:3848], w_in[:, 3072:3080], pad], axis=1)


def _unperm_in(dw):
    return jnp.concatenate([dw[:, 3072:6144], dw[:, 6912:6920], dw[:, 6144:6912], dw[:, 0:3072]], axis=1)


def _layer_fwd(l, x, mem, wt, sm, pre_branch=None):
    t = x.shape[0]
    tag = f"l{l}_"
    h = _rms_fwd(x, sm["mix_norm_g"][l], tag + "mix_norm")
    proj = _matmul(h, wt["w_in"][l], "nn", F32, tag + "in_proj")
    y_conv = _conv_fwd(proj, wt["conv_w"][l], tag + "conv")
    fbias_row = jnp.pad(sm["forget_bias"][l], (0, LANE - 8)).reshape(1, LANE)
    c = _logf_cumsum(proj, fbias_row, tag + "logf_cumsum")
    c8 = c[:, :8].T
    c_col = jnp.broadcast_to(c8[:, :, None], (8, t, LANE))
    c_row = c8.reshape(8, 1, t)
    y_fox, lse_fox = _fox_fwd(proj, c_col, c_row, tag + "fox")
    onehot, bias = sm["swa_tables"]
    sink_rep = jnp.broadcast_to(sm["sink"][l].reshape(2, SWA_GROUP, 1), (2, SWA_GROUP, LANE))
    y_swa, lse_swa = _swa_fwd(proj, bias, sink_rep, tag + "swa")
    ys = (y_conv, y_fox, y_swa)
    if pre_branch is not None:
        pre_branch(y_swa)
    pb = tuple(_matmul(ys[b], wt["w_branch"][l][b], "nn", F32, tag + f"branch{b}") for b in range(3))
    merged = _gate_fwd(proj, pb, tag + "gate")
    x1 = _matmul(merged, wt["w_mix_out"][l], "nn", F32, tag + "mix_out", residual=x)
    xn2 = _rms_fwd(x1, sm["xattn_norm_g"][l], tag + "xattn_norm")
    q = _matmul(xn2, wt["w_xq"][l], "nn", BF16, tag + "xq")
    mem_n = _rms_fwd(mem, sm["mem_norm_g"][l], tag + "mem_norm")
    kv = _matmul(mem_n, wt["w_xkv"][l], "nn", BF16, tag + "xkv")
    o = _xattn_fwd(q, kv, tag + "xattn")
    x2 = _matmul(o, wt["w_xo"][l], "nn", F32, tag + "xo", residual=x1)
    xn3 = _rms_fwd(x2, sm["ffn_norm_g"][l], tag + "ffn_norm")
    ab = _matmul(xn3, wt["w_gu"][l], "nn", F32, tag + "ffn_gu")
    h1 = _swiglu_fwd(ab, tag + "swiglu")
    x3 = _matmul(h1, wt["w_ffn_down"][l], "nn", F32, tag + "ffn_down", residual=x2)
    saved = dict(x=x, h=h, proj=proj, fbias_row=fbias_row, c_col=c_col, c_row=c_row, ys=ys, lse_fox=lse_fox,
                 onehot=onehot, bias=bias, sink_rep=sink_rep, lse_swa=lse_swa, pb=pb, merged=merged, x1=x1,
                 xn2=xn2, q=q, mem_n=mem_n, kv=kv, o=o, x2=x2, xn3=xn3, ab=ab, h1=h1)
    return x3, saved


def _layer_bwd(l, dx3, dx3_b, mem, wt, sm, sv, mid_hook=None, late_hook=None):
    t = dx3.shape[0]
    nb = t // SWA_BLOCK
    tag = f"l{l}_b_"
    gw, gs = {}, {}
    dh1 = _matmul(dx3_b, wt["w_ffn_down"][l], "nt", F32, tag + "d_h1")
    gw["w_ffn_down"] = _matmul(sv["h1"], dx3_b, "tn", F32, tag + "dw_down")
    dab = _swiglu_bwd(sv["ab"], dh1, tag + "swiglu")
    dxn3 = _matmul(dab, wt["w_gu"][l], "nt", F32, tag + "d_xn3")
    dw_gu = _matmul(sv["xn3"], dab, "tn", F32, tag + "dw_gu")
    gw["w_ffn_gate"], gw["w_ffn_up"] = dw_gu[:, :D_FF_P], dw_gu[:, D_FF_P:]
    dx2, dx2_b, gs["ffn_norm_g"] = _rms_bwd(sv["x2"], sm["ffn_norm_g"][l], dxn3, dx3, tag + "ffn_norm")
    do = _matmul(dx2_b, wt["w_xo"][l], "nt", BF16, tag + "d_o")
    gw["w_xo"] = _matmul(sv["o"], dx2_b, "tn", F32, tag + "dw_xo")
    dq, dkv = _xattn_bwd(sv["q"], sv["kv"], do, tag + "xattn")
    gw["w_xkv"] = _matmul(sv["mem_n"], dkv, "tn", F32, tag + "dw_xkv")
    dmem_n = _matmul(dkv, wt["w_xkv"][l], "nt", F32, tag + "d_memn")
    _, _, gs["mem_norm_g"] = _rms_bwd(mem, sm["mem_norm_g"][l], dmem_n, None, tag + "mem_norm")
    gw["w_xq"] = _matmul(sv["xn2"], dq, "tn", F32, tag + "dw_xq")
    dxn2 = _matmul(dq, wt["w_xq"][l], "nt", F32, tag + "d_xn2")
    dx1, dx1_b, gs["xattn_norm_g"] = _rms_bwd(sv["x1"], sm["xattn_norm_g"][l], dxn2, dx2, tag + "xattn_norm")
    dmerged = _matmul(dx1_b, wt["w_mix_out"][l], "nt", F32, tag + "d_merged")
    gw["w_mix_out"] = _matmul(sv["merged"], dx1_b, "tn", F32, tag + "dw_mix_out")
    dp0, dp1, dp2, dgate = _gate_bwd(sv["proj"], sv["pb"], dmerged, tag + "gate")
    dps = (dp0, dp1, dp2)
    dy_dtypes = (F32, BF16, BF16)
    dys = [_matmul(dps[b], wt["w_branch"][l][b], "nt", dy_dtypes[b], tag + f"d_y{b}") for b in range(3)]
    gw["w_branch"] = jnp.stack(
        [_matmul(sv["ys"][b], dps[b], "tn", F32, tag + f"dw_branch{b}") for b in range(3)])
    sink_rep = sv["sink_rep"] if mid_hook is None else sv["sink_rep"] + mid_hook(gw)
    dsq, dkp, dvp, dbias, dsink = _swa_bwd(sv["proj"], sv["bias"], sink_rep, sv["lse_swa"], sv["ys"][2],
                                           dys[2], tag + "swa")

    def band_add(part):
        tot = part[0] + part[1]
        cur = tot[:, SWA_BLOCK:, :]
        nxt = jnp.concatenate([tot[1:, :SWA_BLOCK, :], jnp.zeros((1, SWA_BLOCK, LANE), F32)], axis=0)
        return (cur + nxt).reshape(t, LANE).astype(BF16)

    dsk, dsv = band_add(dkp), band_add(dvp)
    gs["rel_bias_l"] = jnp.einsum("hts,tsb->bh", dbias.reshape(8, SWA_BLOCK, 2 * SWA_BLOCK), sv["onehot"],
                                  precision=lax.Precision.HIGHEST)
    gs["sink"] = dsink[:, :, 0].reshape(8)
    dfq, dfk, dfv, dck, dcq_row = _fox_bwd(sv["proj"], sv["c_col"], sv["c_row"], sv["lse_fox"], sv["ys"][1], dys[1],
                                           tag + "fox_bwd")
    dcq = jnp.pad(dcq_row.reshape(8, t).T, ((0, 0), (0, LANE - 8))).reshape(1, t, LANE)
    dfg, dfb = _logf_cumsum_bwd(sv["proj"], sv["fbias_row"], [dck, dcq], tag + "logf_cumsum")
    gs["forget_bias"] = dfb[0, :8]
    dcb, dcc, dcu, dconv = _conv_bwd(sv["proj"], wt["conv_w"][l], dys[0], tag + "conv")
    gw["conv_w"] = dconv[:3]
    dproj = jnp.concatenate([dgate, dcb, dcc, dcu, dfq, dfk, dfv, dsq, dsk, dsv, dfg], axis=1)
    gw["w_in"] = _unperm_in(_matmul(sv["h"], dproj, "tn", F32, tag + "dw_in"))
    g_mix = sm["mix_norm_g"][l] if late_hook is None else sm["mix_norm_g"][l] + late_hook(gw)
    dh = _matmul(dproj, wt["w_in"][l], "nt", F32, tag + "d_h")
    dx, dx_b, gs["mix_norm_g"] = _rms_bwd(sv["x"], g_mix, dh, dx1, tag + "mix_norm")
    return dx, dx_b, gw, gs


def kernel(x, mem, mix_norm_g, w_in, forget_bias, conv_w, sink, w_branch, w_mix_out, rel_bias, xattn_norm_g, mem_norm_g, w_xq, w_xkv, w_xo, ffn_norm_g, w_ffn_gate, w_ffn_up, w_ffn_down, final_norm_g, loss_target, m_mix_norm_g, m_w_in, m_forget_bias, m_conv_w, m_sink, m_w_branch, m_w_mix_out, m_rel_bias, m_xattn_norm_g, m_mem_norm_g, m_w_xq, m_w_xkv, m_w_xo, m_ffn_norm_g, m_w_ffn_gate, m_w_ffn_up, m_w_ffn_down, m_final_norm_g, v_mix_norm_g, v_w_in, v_forget_bias, v_conv_w, v_sink, v_w_branch, v_w_mix_out, v_rel_bias, v_xattn_norm_g, v_mem_norm_g, v_w_xq, v_w_xkv, v_w_xo, v_ffn_norm_g, v_w_ffn_gate, v_w_ffn_up, v_w_ffn_down, v_final_norm_g):
    args = dict(locals())
    names = [n for n, _ in MATRIX_WEIGHTS] + list(SMALL_PARAMS)
    w = {n: args[n] for n in names}
    mo = {n: args["m_" + n] for n in names}
    vo = {n: args["v_" + n] for n in names}
    x2d, mem2d, tgt = x[0], mem[0], loss_target[0]

    wire = {n: (F32 if n == "conv_w" else BF16) for n, _ in MATRIX_WEIGHTS}
    late = ("w_in", "conv_w")
    early_w = [(n, ax) for n, ax in MATRIX_WEIGHTS if n not in late]
    late_w = [(n, ax) for n, ax in MATRIX_WEIGHTS if n in late]
    wt = {n: [None] * DEPTH for n, _ in MATRIX_WEIGHTS}
    wt["w_gu"] = [None] * DEPTH

    ff_pad = FF_SHARD_P - FF_SHARD
    ff_axis = {"w_ffn_gate": 2, "w_ffn_up": 2, "w_ffn_down": 1}

    def pad_ffn(n, blocks):
        if n not in ff_axis:
            return blocks
        return jnp.pad(blocks, [(0, ff_pad if d == ff_axis[n] else 0) for d in range(blocks.ndim)])

    def unpad_ffn(n, blocks):
        return lax.slice_in_dim(blocks, 0, FF_SHARD, axis=ff_axis[n]) if n in ff_axis else blocks

    def place_weights(l, which, gathered):
        for (n, ax), g in zip(which, gathered):
            wt[n][l] = _to_full(pad_ffn(n, g), ax - 1)
        if "w_in" in dict(which):
            wt["w_in"][l] = _perm_in(wt["w_in"][l])
        if "w_ffn_gate" in dict(which):
            wt["w_gu"][l] = jnp.concatenate([wt["w_ffn_gate"][l], wt["w_ffn_up"][l]], axis=1)

    def shards_of(l, which):
        return [w[n][l].astype(wire[n]) for n, _ in which]

    def start_gather(srcs, name):
        lands = _own_slots(srcs, [lax.empty((N_DEV,) + s.shape, s.dtype) for s in srcs], None, True)
        return _swap_start(srcs, lands, None, True, name)

    place_weights(0, late_w, _all_gather(shards_of(0, late_w), "weights_gather_l0_first"))
    r_send, r_recv, r_srcs, r_lands, token = start_gather(shards_of(0, early_w), "weights_gather_l0_rest_start")
    shards1 = shards_of(1, MATRIX_WEIGHTS)
    shards1[0] = shards1[0] + token[0, 0].astype(shards1[0].dtype)
    w_send, w_recv, w_srcs, lands, token = start_gather(shards1, "weights_gather_l1_start")
    sm = {n: w[n] for n in SMALL_PARAMS}
    sm["mix_norm_g"] = w["mix_norm_g"].at[0].add(token[0, 0])
    sm["swa_tables"] = _swa_tables(w["rel_bias"])

    def rest_of_layer0(after):
        place_weights(0, early_w, _swap_wait(r_send, r_recv, r_srcs, r_lands, after, None, True,
                                             "weights_gather_l0_rest_wait"))

    saved = []
    xc = x2d
    for l in range(DEPTH):
        if l == 1:
            place_weights(1, MATRIX_WEIGHTS,
                          _swap_wait(w_send, w_recv, w_srcs, lands, xc, None, True, "weights_gather_l1_wait"))
        xc, sv = _layer_fwd(l, xc, mem2d, wt, sm, rest_of_layer0 if l == 0 else None)
        saved.append(sv)
    loss_row, dx, dx_b, dg_final = _loss_head(xc, sm["final_norm_g"], tgt, "loss_head")
    loss = lax.psum(loss_row[0, 0], ("x", "y", "c"))

    def grad_parts(gw, which):
        return [unpad_ffn(n, _to_blocks(gw[n], ax - 1)).astype(wire[n]) for n, ax in which]

    gw_all, gs_all = [None] * DEPTH, [None] * DEPTH
    dx, dx_b, gw_all[1], gs_all[1] = _layer_bwd(1, dx, dx_b, mem2d, wt, sm, saved[1])
    parts1 = grad_parts(gw_all[1], MATRIX_WEIGHTS)
    zones = _own_slots(parts1, [lax.empty((N_DEV, DEPTH) + p.shape[1:], p.dtype) for p in parts1], 1, False)
    g_send, g_recv, g_srcs, zones, token = _swap_start(parts1, zones, 1, False, "grads_exchange_l1_start")
    sm_b = dict(sm)
    sm_b["ffn_norm_g"] = sm["ffn_norm_g"].at[0].add(token[0, 0])
    mid = {}

    def mid_hook(gw):
        zone = dict(zip([n for n, _ in MATRIX_WEIGHTS],
                        _swap_wait(g_send, g_recv, g_srcs, zones, gw["w_mix_out"], 1, False, "grads_exchange_l1_wait")))
        parts0 = grad_parts(gw, early_w)
        early_zones = _own_slots(parts0, [zone[n] for n, _ in early_w], 0, False)
        mid["early"] = _swap_start(parts0, early_zones, 0, False, "grads_exchange_l0_early_start")
        mid["late_zones"] = [zone[n] for n, _ in late_w]
        return mid["early"][4][0, 0]

    def late_hook(gw):
        parts0 = grad_parts(gw, late_w)
        late_zones = _own_slots(parts0, mid["late_zones"], 0, False)
        mid["late"] = _swap_start(parts0, late_zones, 0, False, "grads_exchange_l0_late_start")
        return mid["late"][4][0, 0]

    dx, dx_b, gw_all[0], gs_all[0] = _layer_bwd(0, dx, dx_b, mem2d, wt, sm_b, saved[0], mid_hook, late_hook)
    grad_x = dx[None]
    recv_by_name = {}
    for key, which in (("early", early_w), ("late", late_w)):
        s_send, s_recv, s_srcs, s_zones, _ = mid[key]
        recv_by_name.update(zip([n for n, _ in which],
                                _swap_wait(s_send, s_recv, s_srcs, s_zones, dx, 0, False,
                                           "grads_exchange_l0_" + key + "_wait")))
    recv = [recv_by_name[n] for n, _ in MATRIX_WEIGHTS]

    outs = {}
    for (n, _), r in zip(MATRIX_WEIGHTS, recv):
        res = _adamw(r.reshape((N_DEV,) + _rows(w[n]).shape), _rows(w[n]), _rows(mo[n]), _rows(vo[n]), "adamw_" + n)
        outs[n] = [o.reshape(w[n].shape) for o in res]

    gsm = {n: jnp.stack([gs_all[l][n] for l in range(DEPTH)])
           for n in ("mix_norm_g", "xattn_norm_g", "mem_norm_g", "ffn_norm_g", "forget_bias", "sink")}
    gsm["final_norm_g"] = dg_final
    gsm["rel_bias"] = gs_all[0]["rel_bias_l"] + gs_all[1]["rel_bias_l"]
    (small_parts,) = _all_gather([_pack_small([gsm[n] for n in SMALL_PARAMS])], "small_grads_all_gather")
    outs_small = _adamw(small_parts, *[_pack_small([d[n] for n in SMALL_PARAMS]) for d in (w, mo, vo)], "adamw_small")
    for kind in range(4):
        flat, o = outs_small[kind].reshape(-1), 0
        for n in SMALL_PARAMS:
            sz = int(np.prod(w[n].shape))
            outs.setdefault(n, []).append(flat[o:o + sz].reshape(w[n].shape))
            o += sz

    order = ["mix_norm_g", "w_in", "forget_bias", "conv_w", "sink", "w_branch", "w_mix_out", "rel_bias",
             "xattn_norm_g", "mem_norm_g", "w_xq", "w_xkv", "w_xo", "ffn_norm_g", "w_ffn_gate", "w_ffn_up",
             "w_ffn_down", "final_norm_g"]
    result = [loss, grad_x]
    for kind in range(4):
        result += [outs[n][kind] for n in order]
    return tuple(result)
```

```python
import math

import numpy as np
import jax
import jax.numpy as jnp
from jax import lax
from jax.experimental import pallas as pl
from jax.experimental.pallas import tpu as pltpu

F32 = jnp.float32
BF16 = jnp.bfloat16
MESH = pl.DeviceIdType.MESH

LANE = 128
BF16_SUBLANE = 16
V7X_VMEM_REQUEST_CAP = 56 * 2 ** 20
N_DEV = 8

D_MODEL = 1024
DEPTH = 2
HEAD = 64
BRANCH = 512
SWA_BLOCK = 128
SWA_GROUP = 4
N_BUCKETS = 32
X_HEADS = 4
X_HEAD = 256
D_FF = 2816
FF_SHARD = D_FF // N_DEV
FF_SHARD_P = -(-FF_SHARD // LANE) * LANE
D_FF_P = N_DEV * FF_SHARD_P
RMS_EPS = 1e-6
NEG = -1e30
ADAM_LR, ADAM_B1, ADAM_B2, ADAM_EPS, ADAM_WD, ADAM_STEP = 0.001, 0.9, 0.999, 1e-08, 0.01, 10

IN_COLS = 6920
PROJ_COLS = 7040
COL_GATE, COL_CONV, COL_FOX, COL_SQ, COL_SK, COL_SV, COL_FG = 0, 3072, 4608, 6144, 6656, 6784, 6912

ROW_TILE = 512
FOX_TILE = 512
MM_TM, MM_TN, MM_TK = 1024, 1536, 2048


def _pick(n, cap, mult):
    best = None
    for d in range(mult, min(n, cap) + 1, mult):
        if n % d == 0:
            best = d
    return n if best is None else best


def _params(semantics, block_bytes):
    limit = int(min(max(2 * block_bytes + (8 << 20), 24 << 20), V7X_VMEM_REQUEST_CAP))
    return pltpu.CompilerParams(dimension_semantics=semantics, vmem_limit_bytes=limit)


def _nbytes(shape, dtype):
    return int(np.prod(shape)) * jnp.dtype(dtype).itemsize


def _dot(a, b, dims):
    return lax.dot_general(a, b, (dims, ((), ())), preferred_element_type=F32)


NN = ((1,), (0,))
NT = ((1,), (1,))
TN = ((0,), (0,))


def _matmul(a, b, mode, out_dtype, name, residual=None):
    if mode == "nn":
        (m, k), (k2, n) = a.shape, b.shape
    elif mode == "nt":
        (m, k), (n, k2) = a.shape, b.shape
    else:
        (k, m), (k2, n) = a.shape, b.shape
    assert k == k2, (name, a.shape, b.shape)
    tm, tn, tk = _pick(m, MM_TM, LANE), _pick(n, MM_TN, LANE), _pick(k, MM_TK, LANE)
    nk = k // tk
    dims = {"nn": NN, "nt": NT, "tn": TN}[mode]
    has_res = residual is not None

    def body(*refs):
        a_ref, b_ref = refs[0], refs[1]
        r_ref = refs[2] if has_res else None
        o_ref = refs[3] if has_res else refs[2]
        kk = pl.program_id(2)
        p = _dot(a_ref[...].astype(BF16), b_ref[...].astype(BF16), dims)
        if nk == 1:
            if has_res:
                p = p + r_ref[...]
            o_ref[...] = p.astype(out_dtype)
        else:
            acc_ref = refs[-1]

            @pl.when(kk == 0)
            def _():
                acc_ref[...] = p

            @pl.when(kk > 0)
            def _():
                acc_ref[...] += p

            @pl.when(kk == nk - 1)
            def _():
                res = acc_ref[...]
                if has_res:
                    res = res + r_ref[...]
                o_ref[...] = res.astype(out_dtype)

    if mode == "nn":
        a_spec = pl.BlockSpec((tm, tk), lambda i, j, kk: (i, kk))
        b_spec = pl.BlockSpec((tk, tn), lambda i, j, kk: (kk, j))
    elif mode == "nt":
        a_spec = pl.BlockSpec((tm, tk), lambda i, j, kk: (i, kk))
        b_spec = pl.BlockSpec((tn, tk), lambda i, j, kk: (j, kk))
    else:
        a_spec = pl.BlockSpec((tk, tm), lambda i, j, kk: (kk, i))
        b_spec = pl.BlockSpec((tk, tn), lambda i, j, kk: (kk, j))
    o_spec = pl.BlockSpec((tm, tn), lambda i, j, kk: (i, j))
    in_specs, args = [a_spec, b_spec], [a, b]
    if has_res:
        in_specs.append(o_spec)
        args.append(residual)
    blk = (_nbytes((tm, tk), a.dtype) + _nbytes((tk, tn), b.dtype) + _nbytes((tm, tn), out_dtype)
           + (_nbytes((tm, tn), F32) if has_res else 0))
    scratch = [pltpu.VMEM((tm, tn), F32)] if nk > 1 else []
    return pl.pallas_call(
        body, name=name, grid=(m // tm, n // tn, nk),
        out_shape=jax.ShapeDtypeStruct((m, n), out_dtype),
        in_specs=in_specs, out_specs=o_spec, scratch_shapes=scratch,
        compiler_params=_params(("parallel", "parallel", "arbitrary"), blk + _nbytes((tm, tn), F32)),
    )(*args)


def _rms_fwd(x, g, name):
    t, d = x.shape
    tr = _pick(t, ROW_TILE, BF16_SUBLANE)

    def body(x_ref, g_ref, y_ref):
        xv = x_ref[...]
        r = lax.rsqrt(jnp.mean(xv * xv, axis=-1, keepdims=True) + RMS_EPS)
        y_ref[...] = ((xv * r) * g_ref[...]).astype(BF16)

    return pl.pallas_call(
        body, name=name, grid=(t // tr,),
        out_shape=jax.ShapeDtypeStruct((t, d), BF16),
        in_specs=[pl.BlockSpec((tr, d), lambda i: (i, 0)), pl.BlockSpec((1, d), lambda i: (0, 0))],
        out_specs=pl.BlockSpec((tr, d), lambda i: (i, 0)),
        compiler_params=_params(("parallel",), 2 * _nbytes((tr, d), F32)),
    )(x, g.reshape(1, d))


def _rms_bwd(x, g, dy, dres, name):
    t, d = x.shape
    tr = _pick(t, ROW_TILE, BF16_SUBLANE)
    has_res = dres is not None

    def body(*refs):
        x_ref, g_ref, dy_ref = refs[:3]
        r_ref = refs[3] if has_res else None
        dx_ref, dxb_ref, dg_ref = refs[-3:]
        i = pl.program_id(0)
        xv = x_ref[...]
        r = lax.rsqrt(jnp.mean(xv * xv, axis=-1, keepdims=True) + RMS_EPS)
        xh = xv * r
        dyv = dy_ref[...].astype(F32)
        dxh = dyv * g_ref[...]
        dx = r * (dxh - xh * jnp.mean(dxh * xh, axis=-1, keepdims=True))
        if has_res:
            dx = dx + r_ref[...]
        dx_ref[...] = dx
        dxb_ref[...] = dx.astype(BF16)

        @pl.when(i == 0)
        def _():
            dg_ref[...] = jnp.zeros_like(dg_ref)

        dg_ref[...] += jnp.sum(dyv * xh, axis=0, keepdims=True)

    row = pl.BlockSpec((tr, d), lambda i: (i, 0))
    vec = pl.BlockSpec((1, d), lambda i: (0, 0))
    in_specs, args = [row, vec, row], [x, g.reshape(1, d), dy]
    if has_res:
        in_specs.append(row)
        args.append(dres)
    return pl.pallas_call(
        body, name=name, grid=(t // tr,),
        out_shape=(jax.ShapeDtypeStruct((t, d), F32), jax.ShapeDtypeStruct((t, d), BF16),
                   jax.ShapeDtypeStruct((1, d), F32)),
        in_specs=in_specs, out_specs=(row, row, vec),
        compiler_params=_params(("arbitrary",), 5 * _nbytes((tr, d), F32)),
    )(*args)


def _loss_head(x, g, target, name):
    t, d = x.shape
    tr = _pick(t, ROW_TILE, BF16_SUBLANE)

    def body(x_ref, g_ref, t_ref, loss_ref, dx_ref, dxb_ref, dg_ref):
        i = pl.program_id(0)
        xv = x_ref[...]
        gv = g_ref[...]
        r = lax.rsqrt(jnp.mean(xv * xv, axis=-1, keepdims=True) + RMS_EPS)
        xh = xv * r
        diff = xh * gv - t_ref[...]
        part = 0.5 * jnp.sum(jnp.mean(diff * diff, axis=-1, keepdims=True), axis=0, keepdims=True)
        dyv = diff * (1.0 / d)
        dxh = dyv * gv
        dx = r * (dxh - xh * jnp.mean(dxh * xh, axis=-1, keepdims=True))
        dx_ref[...] = dx
        dxb_ref[...] = dx.astype(BF16)

        @pl.when(i == 0)
        def _():
            dg_ref[...] = jnp.zeros_like(dg_ref)
            loss_ref[...] = jnp.zeros_like(loss_ref)

        dg_ref[...] += jnp.sum(dyv * xh, axis=0, keepdims=True)
        loss_ref[...] += jnp.broadcast_to(part, loss_ref.shape)

    row = pl.BlockSpec((tr, d), lambda i: (i, 0))
    vec = pl.BlockSpec((1, d), lambda i: (0, 0))
    return pl.pallas_call(
        body, name=name, grid=(t // tr,),
        out_shape=(jax.ShapeDtypeStruct((1, LANE), F32), jax.ShapeDtypeStruct((t, d), F32),
                   jax.ShapeDtypeStruct((t, d), BF16), jax.ShapeDtypeStruct((1, d), F32)),
        in_specs=[row, vec, row],
        out_specs=(pl.BlockSpec((1, LANE), lambda i: (0, 0)), row, row, vec),
        compiler_params=_params(("arbitrary",), 5 * _nbytes((tr, d), F32)),
    )(x, g.reshape(1, d), target)


HALO = 8


def _conv_fwd(proj, conv_w, name):
    t = proj.shape[0]
    tr = _pick(t, ROW_TILE, BF16_SUBLANE)
    c0 = COL_CONV // BRANCH
    hb = tr // HALO

    def body(cb_ref, cc_ref, cu_ref, hc_ref, hu_ref, w_ref, y_ref):
        i = pl.program_id(0)
        z = cc_ref[...] * cu_ref[...]
        hz = jnp.where(i > 0, hc_ref[...] * hu_ref[...], 0.0)
        zf = jnp.concatenate([hz, z], axis=0)
        z1 = pltpu.roll(zf, 1, 0)[HALO:]
        z2 = pltpu.roll(zf, 2, 0)[HALO:]
        y = w_ref[2:3, :] * z + w_ref[1:2, :] * z1 + w_ref[0:1, :] * z2
        y_ref[...] = (cb_ref[...] * y).astype(BF16)

    def col(c):
        return pl.BlockSpec((tr, BRANCH), lambda i, c=c: (i, c0 + c))

    def prev(c):
        return pl.BlockSpec((HALO, BRANCH), lambda i, c=c: (jnp.maximum(i * hb - 1, 0), c0 + c))

    return pl.pallas_call(
        body, name=name, grid=(t // tr,),
        out_shape=jax.ShapeDtypeStruct((t, BRANCH), BF16),
        in_specs=[col(0), col(1), col(2), prev(1), prev(2), pl.BlockSpec((3, BRANCH), lambda i: (0, 0))],
        out_specs=pl.BlockSpec((tr, BRANCH), lambda i: (i, 0)),
        compiler_params=_params(("parallel",), 6 * _nbytes((tr, BRANCH), F32)),
    )(proj, proj, proj, proj, proj, conv_w)


def _conv_bwd(proj, conv_w, dout, name):
    t = proj.shape[0]
    tr = _pick(t, ROW_TILE, BF16_SUBLANE)
    nblk = t // tr
    c0 = COL_CONV // BRANCH
    hb = tr // HALO
    last_halo = t // HALO - 1

    def body(cb_ref, cc_ref, cu_ref, hc_ref, hu_ref, do_ref, ndo_ref, ncb_ref, w_ref,
             dcb_ref, dcc_ref, dcu_ref, dw_ref):
        i = pl.program_id(0)
        cb, cc, cu = cb_ref[...], cc_ref[...], cu_ref[...]
        w0, w1, w2 = w_ref[0:1, :], w_ref[1:2, :], w_ref[2:3, :]
        z = cc * cu
        hz = jnp.where(i > 0, hc_ref[...] * hu_ref[...], 0.0)
        zf = jnp.concatenate([hz, z], axis=0)
        z1 = pltpu.roll(zf, 1, 0)[HALO:]
        z2 = pltpu.roll(zf, 2, 0)[HALO:]
        y = w2 * z + w1 * z1 + w0 * z2
        dout_v = do_ref[...]
        dyc = dout_v * cb
        hdy = jnp.where(i < nblk - 1, ndo_ref[...] * ncb_ref[...], 0.0)
        dyf = jnp.concatenate([dyc, hdy], axis=0)
        dy1 = pltpu.roll(dyf, tr + HALO - 1, 0)[:tr]
        dy2 = pltpu.roll(dyf, tr + HALO - 2, 0)[:tr]
        dz = w2 * dyc + w1 * dy1 + w0 * dy2
        dcb_ref[...] = (dout_v * y).astype(BF16)
        dcc_ref[...] = (dz * cu).astype(BF16)
        dcu_ref[...] = (dz * cc).astype(BF16)

        @pl.when(i == 0)
        def _():
            dw_ref[...] = jnp.zeros_like(dw_ref)

        dw_ref[0:1, :] += jnp.sum(dyc * z2, axis=0, keepdims=True)
        dw_ref[1:2, :] += jnp.sum(dyc * z1, axis=0, keepdims=True)
        dw_ref[2:3, :] += jnp.sum(dyc * z, axis=0, keepdims=True)

    def col(c):
        return pl.BlockSpec((tr, BRANCH), lambda i, c=c: (i, c0 + c))

    def prev(c):
        return pl.BlockSpec((HALO, BRANCH), lambda i, c=c: (jnp.maximum(i * hb - 1, 0), c0 + c))

    def nxt(c):
        return pl.BlockSpec((HALO, BRANCH), lambda i, c=c: (jnp.minimum((i + 1) * hb, last_halo), c))

    row = pl.BlockSpec((tr, BRANCH), lambda i: (i, 0))
    return pl.pallas_call(
        body, name=name, grid=(nblk,),
        out_shape=(jax.ShapeDtypeStruct((t, BRANCH), BF16),) * 3 + (jax.ShapeDtypeStruct((HALO, BRANCH), F32),),
        in_specs=[col(0), col(1), col(2), prev(1), prev(2), row, nxt(0), nxt(c0),
                  pl.BlockSpec((3, BRANCH), lambda i: (0, 0))],
        out_specs=(row, row, row, pl.BlockSpec((HALO, BRANCH), lambda i: (0, 0))),
        compiler_params=_params(("arbitrary",), 8 * _nbytes((tr, BRANCH), F32)),
    )(proj, proj, proj, proj, proj, dout, dout, proj, conv_w)


def _tri(lower):
    r = lax.broadcasted_iota(jnp.int32, (LANE, LANE), 0)
    c = lax.broadcasted_iota(jnp.int32, (LANE, LANE), 1)
    return jnp.where((c <= r) if lower else (c >= r), 1.0, 0.0).astype(F32)


def _logf_cumsum(proj, fbias_row, name):
    t = proj.shape[0]
    nchunk = t // LANE

    def body(f_ref, b_ref, c_ref, run_sc):
        tri = _tri(True)
        run_sc[...] = jnp.zeros_like(run_sc)

        @pl.loop(0, nchunk)
        def _(i):
            rows = pl.ds(pl.multiple_of(i * LANE, LANE), LANE)
            z = f_ref[rows, :] + b_ref[...]
            logf = jnp.minimum(z, 0.0) - jnp.log(1.0 + jnp.exp(-jnp.abs(z)))
            cs = lax.dot_general(tri, logf, (NN, ((), ())), precision=lax.Precision.HIGHEST,
                                 preferred_element_type=F32) + run_sc[0:1, :]
            c_ref[rows, :] = cs
            run_sc[0:1, :] = cs[LANE - 1:LANE, :]

    return pl.pallas_call(
        body, name=name, grid=(1,),
        out_shape=jax.ShapeDtypeStruct((t, LANE), F32),
        in_specs=[pl.BlockSpec((t, LANE), lambda i: (0, COL_FG // LANE)), pl.BlockSpec((1, LANE), lambda i: (0, 0))],
        out_specs=pl.BlockSpec((t, LANE), lambda i: (0, 0)),
        scratch_shapes=[pltpu.VMEM((8, LANE), F32)],
        compiler_params=_params(("arbitrary",), 2 * _nbytes((t, LANE), F32)),
    )(proj, fbias_row)


def _logf_cumsum_bwd(proj, fbias_row, pieces, name):
    t = proj.shape[0]
    tb = _pick(t, 2 * ROW_TILE, LANE)
    nblk = t // tb
    npiece = len(pieces)

    def body(*refs):
        f_ref, b_ref = refs[:2]
        piece_refs = refs[2:2 + npiece]
        df_ref, db_ref, run_sc = refs[2 + npiece:]
        i = pl.program_id(0)
        tri = _tri(False)

        @pl.when(i == 0)
        def _():
            run_sc[...] = jnp.zeros_like(run_sc)
            db_ref[...] = jnp.zeros_like(db_ref)

        for c in reversed(range(tb // LANE)):
            rows = slice(c * LANE, (c + 1) * LANE)
            slabs = [p_ref[n, rows, :] for p_ref in piece_refs for n in range(p_ref.shape[0])]
            dcc = slabs[0]
            for slab in slabs[1:]:
                dcc = dcc + slab
            ss = lax.dot_general(tri, dcc, (NN, ((), ())), precision=lax.Precision.HIGHEST,
                                 preferred_element_type=F32) + run_sc[0:1, :]
            z = f_ref[rows, :] + b_ref[...]
            dz = ss * (1.0 / (1.0 + jnp.exp(z)))
            df_ref[rows, :] = dz.astype(BF16)
            run_sc[0:1, :] = ss[0:1, :]
            db_ref[...] += jnp.sum(dz, axis=0, keepdims=True)

    piece_specs = [pl.BlockSpec((p.shape[0], tb, LANE), lambda i: (0, nblk - 1 - i, 0)) for p in pieces]
    nslab = sum(p.shape[0] for p in pieces)
    return pl.pallas_call(
        body, name=name, grid=(nblk,),
        out_shape=(jax.ShapeDtypeStruct((t, LANE), BF16), jax.ShapeDtypeStruct((1, LANE), F32)),
        in_specs=[pl.BlockSpec((tb, LANE), lambda i: (nblk - 1 - i, COL_FG // LANE)),
                  pl.BlockSpec((1, LANE), lambda i: (0, 0))] + piece_specs,
        out_specs=(pl.BlockSpec((tb, LANE), lambda i: (nblk - 1 - i, 0)), pl.BlockSpec((1, LANE), lambda i: (0, 0))),
        scratch_shapes=[pltpu.VMEM((8, LANE), F32)],
        compiler_params=_params(("arbitrary",), (4 + nslab) * _nbytes((tb, LANE), F32)),
    )(proj, fbias_row, *pieces)


def _lo_mask():
    return lax.broadcasted_iota(jnp.int32, (1, LANE), 1) < HEAD


def _causal_steps(n, key_major):
    if key_major:
        pairs = [(iq, ik) for ik in range(n) for iq in range(ik, n)]
    else:
        pairs = [(iq, ik) for iq in range(n) for ik in range(iq + 1)]
    return (jnp.asarray([p[0] for p in pairs], jnp.int32), jnp.asarray([p[1] for p in pairs], jnp.int32))


def _head_lanes(j, pair_vals):
    lane = lax.broadcasted_iota(jnp.int32, (1, LANE), 1)
    return jnp.where(lane == 2 * j, pair_vals[0], 0.0) + jnp.where(lane == 2 * j + 1, pair_vals[1], 0.0)


def _fox_fwd(proj, c_col, c_row, name):
    t = proj.shape[0]
    tq = _pick(t, FOX_TILE, LANE)
    nq = t // tq
    rep = tq // LANE
    scale = HEAD ** -0.5
    cq, ck, cv = COL_FOX // LANE, COL_FOX // LANE + 4, COL_FOX // LANE + 8
    q_tab, k_tab = _causal_steps(nq, False)

    def body(qt_ref, kt_ref, q_ref, k_ref, v_ref, ck_ref, cqr_ref, y_ref, lse_ref, m_sc, l_sc, acc_sc):
        step_id = pl.program_id(1)
        iq, ik = qt_ref[step_id], kt_ref[step_id]
        lo = _lo_mask()
        lo_rows = lax.broadcasted_iota(jnp.int32, (LANE, 1), 0) < HEAD

        @pl.when(ik == 0)
        def _():
            m_sc[...] = jnp.full(m_sc.shape, NEG, F32)
            l_sc[...] = jnp.zeros_like(l_sc)
            acc_sc[...] = jnp.zeros_like(acc_sc)

        def step(diag):
            q2 = (q_ref[...] * scale).astype(BF16)
            k2 = k_ref[...].astype(BF16)
            v2 = v_ref[...].astype(BF16)
            alphas, adds = [], []
            for h in range(2):
                msk = lo if h == 0 else jnp.logical_not(lo)
                kh = jnp.where(msk, k2, jnp.zeros_like(k2))
                vh = jnp.where(msk, v2, jnp.zeros_like(v2))
                st = _dot(kh, q2, NT) + cqr_ref[h] - jnp.tile(ck_ref[h], (1, rep))
                if diag:
                    krow = lax.broadcasted_iota(jnp.int32, (tq, tq), 0)
                    qcol = lax.broadcasted_iota(jnp.int32, (tq, tq), 1)
                    st = jnp.where(krow <= qcol, st, NEG)
                m_prev = m_sc[h]
                m_new = jnp.maximum(m_prev, jnp.max(st, axis=0, keepdims=True))
                alpha = jnp.exp(m_prev - m_new)
                pt = jnp.exp(st - m_new)
                l_sc[h] = alpha * l_sc[h] + jnp.sum(pt, axis=0, keepdims=True)
                m_sc[h] = m_new
                alphas.append(alpha)
                adds.append(_dot(vh, pt.astype(BF16), TN))
            acc_sc[...] = acc_sc[...] * jnp.where(lo_rows, alphas[0], alphas[1]) + (adds[0] + adds[1])

        @pl.when(ik < iq)
        def _():
            step(False)

        @pl.when(ik == iq)
        def _():
            step(True)
            yt = acc_sc[...] / jnp.where(lo_rows, l_sc[0], l_sc[1])
            y_ref[...] = yt.T.astype(BF16)
            lse_ref[...] = m_sc[...] + jnp.log(l_sc[...])

    def kv(c):
        return pl.BlockSpec((tq, LANE), lambda j, s, qt, kt, c=c: (kt[s], c + j))

    qrow = pl.BlockSpec((2, 1, tq), lambda j, s, qt, kt: (j, 0, qt[s]))
    grid_spec = pltpu.PrefetchScalarGridSpec(
        num_scalar_prefetch=2, grid=(4, int(q_tab.shape[0])),
        in_specs=[pl.BlockSpec((tq, LANE), lambda j, s, qt, kt: (qt[s], cq + j)), kv(ck), kv(cv),
                  pl.BlockSpec((2, tq, LANE), lambda j, s, qt, kt: (j, kt[s], 0)), qrow],
        out_specs=(pl.BlockSpec((tq, LANE), lambda j, s, qt, kt: (qt[s], j)), qrow),
        scratch_shapes=[pltpu.VMEM((2, 1, tq), F32), pltpu.VMEM((2, 1, tq), F32), pltpu.VMEM((LANE, tq), F32)])
    return pl.pallas_call(
        body, name=name, grid_spec=grid_spec,
        out_shape=(jax.ShapeDtypeStruct((t, BRANCH), BF16), jax.ShapeDtypeStruct((8, 1, t), F32)),
        compiler_params=_params(("parallel", "arbitrary"),
                                16 * _nbytes((tq, LANE), F32) + 6 * _nbytes((tq, tq), F32)),
    )(q_tab, k_tab, proj, proj, proj, c_col, c_row)


def _fox_bwd(proj, c_col, c_row, lse_row, y, dy, name):
    t = proj.shape[0]
    tb = _pick(t, FOX_TILE, LANE)
    nb = t // tb
    rep = tb // LANE
    scale = HEAD ** -0.5
    cq, ck, cv = COL_FOX // LANE, COL_FOX // LANE + 4, COL_FOX // LANE + 8
    q_tab, k_tab = _causal_steps(nb, True)
    nsteps = int(q_tab.shape[0])

    def body(qt_ref, kt_ref, k_ref, v_ref, q_ref, y_ref, dy_ref, ck_ref, cqr_ref, lser_ref,
             dq_ref, dk_ref, dv_ref, dck_ref, dcq_ref, dk_sc, dv_sc, dc_sc, dqt_sc, dcq_sc, d_sc):
        j, step_id = pl.program_id(0), pl.program_id(1)
        iq, ik = qt_ref[step_id], kt_ref[step_id]
        lo = _lo_mask()

        @pl.when(step_id == 0)
        def _():
            dqt_sc[...] = jnp.zeros_like(dqt_sc)
            dcq_sc[...] = jnp.zeros_like(dcq_sc)

        @pl.when(iq == ik)
        def _():
            dk_sc[...] = jnp.zeros_like(dk_sc)
            dv_sc[...] = jnp.zeros_like(dv_sc)
            dc_sc[...] = jnp.zeros_like(dc_sc)

        @pl.when(ik == 0)
        def _():
            prod = y_ref[...].astype(F32) * dy_ref[...].astype(F32)
            row = lax.broadcasted_iota(jnp.int32, (8, LANE), 0)
            sel = jnp.logical_or(jnp.logical_and(row == 0, lo), jnp.logical_and(row == 1, jnp.logical_not(lo)))
            d_sc[iq] = lax.dot_general(jnp.where(sel, 1.0, 0.0).astype(F32), prod, (NT, ((), ())),
                                       precision=lax.Precision.HIGHEST, preferred_element_type=F32)

        def step(diag):
            k2 = k_ref[...].astype(BF16)
            v2 = v_ref[...].astype(BF16)
            q2 = (q_ref[...] * scale).astype(BF16)
            do2 = dy_ref[...]
            d_rows = d_sc[iq]
            for h in range(2):
                msk = lo if h == 0 else jnp.logical_not(lo)
                kh = jnp.where(msk, k2, jnp.zeros_like(k2))
                vh = jnp.where(msk, v2, jnp.zeros_like(v2))
                st = _dot(kh, q2, NT) + (cqr_ref[h] - lser_ref[h]) - jnp.tile(ck_ref[h], (1, rep))
                if diag:
                    krow = lax.broadcasted_iota(jnp.int32, (tb, tb), 0)
                    qcol = lax.broadcasted_iota(jnp.int32, (tb, tb), 1)
                    st = jnp.where(krow <= qcol, st, NEG)
                pt = jnp.exp(st)
                dpt = _dot(vh, do2, NT)
                dst = pt * (dpt - d_rows[h:h + 1, :])
                dsb = dst.astype(BF16)
                dv_sc[h] += _dot(pt.astype(BF16), do2, NN)
                dk_sc[h] += _dot(dsb, q2, NN)
                dc_sc[h] -= jnp.sum(dst, axis=1, keepdims=True)
                dqt_sc[iq] += _dot(kh, dsb, TN)
                dcq_sc[h, iq] += jnp.sum(dst, axis=0, keepdims=True)

        @pl.when(iq > ik)
        def _():
            step(False)

        @pl.when(iq == ik)
        def _():
            step(True)

        @pl.when(iq == nb - 1)
        def _():
            dk_ref[...] = jnp.where(lo, dk_sc[0], dk_sc[1]).astype(BF16)
            dv_ref[...] = jnp.where(lo, dv_sc[0], dv_sc[1]).astype(BF16)
            dck_ref[...] = _head_lanes(j, dc_sc)

        @pl.when(step_id == nsteps - 1)
        def _():
            for i in range(nb):
                dq_ref[i * tb:(i + 1) * tb, :] = (dqt_sc[i].T * scale).astype(BF16)
                for h in range(2):
                    dcq_ref[h, :, i * tb:(i + 1) * tb] = dcq_sc[h, i]

    def kcol(c):
        return pl.BlockSpec((tb, LANE), lambda j, s, qt, kt, c=c: (kt[s], c + j))

    qrow = pl.BlockSpec((2, 1, tb), lambda j, s, qt, kt: (j, 0, qt[s]))
    pair_q = pl.BlockSpec((tb, LANE), lambda j, s, qt, kt: (qt[s], j))
    pair_k = pl.BlockSpec((tb, LANE), lambda j, s, qt, kt: (kt[s], j))
    grid_spec = pltpu.PrefetchScalarGridSpec(
        num_scalar_prefetch=2, grid=(4, nsteps),
        in_specs=[kcol(ck), kcol(cv), pl.BlockSpec((tb, LANE), lambda j, s, qt, kt: (qt[s], cq + j)), pair_q, pair_q,
                  pl.BlockSpec((2, tb, LANE), lambda j, s, qt, kt: (j, kt[s], 0)), qrow, qrow],
        out_specs=(pl.BlockSpec((t, LANE), lambda j, s, qt, kt: (0, j)), pair_k, pair_k,
                   pl.BlockSpec((None, tb, LANE), lambda j, s, qt, kt: (j, kt[s], 0)),
                   pl.BlockSpec((2, 1, t), lambda j, s, qt, kt: (j, 0, 0))),
        scratch_shapes=[pltpu.VMEM((2, tb, LANE), F32)] * 3
        + [pltpu.VMEM((nb, LANE, tb), F32), pltpu.VMEM((2, nb, 1, tb), F32), pltpu.VMEM((nb, 8, tb), F32)])
    return pl.pallas_call(
        body, name=name, grid_spec=grid_spec,
        out_shape=(jax.ShapeDtypeStruct((t, BRANCH), BF16), jax.ShapeDtypeStruct((t, BRANCH), BF16),
                   jax.ShapeDtypeStruct((t, BRANCH), BF16), jax.ShapeDtypeStruct((4, t, LANE), F32),
                   jax.ShapeDtypeStruct((8, 1, t), F32)),
        compiler_params=_params(("parallel", "arbitrary"),
                                24 * _nbytes((tb, LANE), F32) + 8 * _nbytes((tb, tb), F32)
                                + 2 * _nbytes((t, LANE), F32)),
    )(q_tab, k_tab, proj, proj, proj, y, dy, c_col, c_row, lse_row)


def _swa_tables(rel_bias):
    tq = np.arange(SWA_BLOCK)[:, None]
    sk = np.arange(2 * SWA_BLOCK)[None, :]
    dist = SWA_BLOCK + tq - sk
    inwin = (dist >= 0) & (dist < SWA_BLOCK)
    n = np.maximum(dist, 0)
    max_exact = N_BUCKETS // 2
    large = max_exact + (np.log(np.maximum(n, 1).astype(np.float32) / max_exact)
                         / math.log(SWA_BLOCK / max_exact) * (N_BUCKETS - max_exact)).astype(np.int32)
    bucket = np.where(n < max_exact, n, np.minimum(large, N_BUCKETS - 1))
    onehot = (bucket[..., None] == np.arange(N_BUCKETS)) & inwin[..., None]
    onehot = jnp.asarray(onehot.astype(np.float32))
    bias = jnp.einsum("tsb,bh->hts", onehot, rel_bias, precision=lax.Precision.HIGHEST)
    bias = jnp.where(jnp.asarray(inwin)[None], bias, NEG)
    return onehot, bias


SWA_BLOCKS_PER_STEP = 4


def _swa_specs(nb, r_blk, csk, csv):
    def prev(c):
        return pl.BlockSpec((SWA_BLOCK, LANE), lambda kvh, n, c=c: (jnp.maximum(n * r_blk - 1, 0), c))

    def cur(c):
        return pl.BlockSpec((r_blk * SWA_BLOCK, LANE), lambda kvh, n, c=c: (n, c))

    return [prev(csk), cur(csk), prev(csv), cur(csv)]


def _swa_fwd(proj, bias, sink_rep, name):
    t = proj.shape[0]
    nb = t // SWA_BLOCK
    r_blk = _pick(nb, SWA_BLOCKS_PER_STEP, 1)
    rows = r_blk * SWA_BLOCK
    scale = HEAD ** -0.5
    csq, csk, csv = COL_SQ // 256, COL_SK // LANE, COL_SV // LANE

    def body(q_ref, kp_ref, kc_ref, vp_ref, vc_ref, b_ref, sk_ref, y_ref, lse_ref):
        kvh, n = pl.program_id(0), pl.program_id(1)
        lane = lax.broadcasted_iota(jnp.int32, (1, LANE), 1)
        lo = lane < HEAD
        kvm = jnp.logical_and(lane >= kvh * HEAD, lane < (kvh + 1) * HEAD)

        def both(prev_ref, cur_ref):
            band = jnp.concatenate([prev_ref[...], cur_ref[...]], axis=0)
            band = jnp.where(kvm, band, 0.0)
            return (band + pltpu.roll(band, HEAD, 1)).astype(BF16)

        kb_all, vb_all = both(kp_ref, kc_ref), both(vp_ref, vc_ref)
        col = lax.broadcasted_iota(jnp.int32, (SWA_BLOCK, 2 * SWA_BLOCK), 1)
        first = jnp.logical_and(n == 0, col < SWA_BLOCK)
        for r in range(r_blk):
            rs = slice(r * SWA_BLOCK, (r + 1) * SWA_BLOCK)
            kb = kb_all[r * SWA_BLOCK:(r + 2) * SWA_BLOCK]
            vb = vb_all[r * SWA_BLOCK:(r + 2) * SWA_BLOCK]
            outs = []
            for g in range(SWA_GROUP):
                half = q_ref[rs, (g // 2) * LANE:(g // 2 + 1) * LANE]
                hm = lo if g % 2 == 0 else jnp.logical_not(lo)
                qg = jnp.where(hm, half, 0.0).astype(BF16)
                s = _dot(qg, kb, NT) * scale + b_ref[g]
                if r == 0:
                    s = jnp.where(first, NEG, s)
                snk = sk_ref[g:g + 1, :]
                m = jnp.maximum(jnp.max(s, axis=1, keepdims=True), snk)
                p = jnp.exp(s - jnp.tile(m, (1, 2)))
                denom = jnp.sum(p, axis=1, keepdims=True) + jnp.exp(snk - m)
                outs.append(_dot(p.astype(BF16), vb, NN) / denom)
                lse_ref[g, rs, :] = m + jnp.log(denom)
            y_ref[rs, 0:LANE] = jnp.where(lo, outs[0], outs[1]).astype(BF16)
            y_ref[rs, LANE:2 * LANE] = jnp.where(lo, outs[2], outs[3]).astype(BF16)

    return pl.pallas_call(
        body, name=name, grid=(2, nb // r_blk),
        out_shape=(jax.ShapeDtypeStruct((t, BRANCH), BF16), jax.ShapeDtypeStruct((8, t, LANE), F32)),
        in_specs=[pl.BlockSpec((rows, 256), lambda kvh, n: (n, csq + kvh))] + _swa_specs(nb, r_blk, csk, csv)
        + [pl.BlockSpec((None, SWA_GROUP, SWA_BLOCK, 256), lambda kvh, n: (kvh, 0, 0, 0)),
           pl.BlockSpec((None, SWA_GROUP, LANE), lambda kvh, n: (kvh, 0, 0))],
        out_specs=(pl.BlockSpec((rows, 256), lambda kvh, n: (n, kvh)),
                   pl.BlockSpec((SWA_GROUP, rows, LANE), lambda kvh, n: (kvh, n, 0))),
        compiler_params=_params(("parallel", "arbitrary"), 8 << 20),
    )(proj, proj, proj, proj, proj, bias.reshape(2, SWA_GROUP, SWA_BLOCK, 256), sink_rep)


def _swa_bwd(proj, bias, sink_rep, lse, y, dy, name):
    t = proj.shape[0]
    nb = t // SWA_BLOCK
    r_blk = _pick(nb, SWA_BLOCKS_PER_STEP, 1)
    rows = r_blk * SWA_BLOCK
    scale = HEAD ** -0.5
    csq, csk, csv = COL_SQ // 256, COL_SK // LANE, COL_SV // LANE

    def body(q_ref, kp_ref, kc_ref, vp_ref, vc_ref, b_ref, sk_ref, lse_ref, y_ref, dy_ref,
             dq_ref, dkp_ref, dvp_ref, db_ref, dsk_ref):
        kvh, n = pl.program_id(0), pl.program_id(1)
        lane = lax.broadcasted_iota(jnp.int32, (1, LANE), 1)
        lo = lane < HEAD
        kvm = jnp.logical_and(lane >= kvh * HEAD, lane < (kvh + 1) * HEAD)

        def both(prev_ref, cur_ref):
            band = jnp.concatenate([prev_ref[...], cur_ref[...]], axis=0)
            band = jnp.where(kvm, band, 0.0)
            return (band + pltpu.roll(band, HEAD, 1)).astype(BF16)

        kb_all, vb_all = both(kp_ref, kc_ref), both(vp_ref, vc_ref)
        col = lax.broadcasted_iota(jnp.int32, (SWA_BLOCK, 2 * SWA_BLOCK), 1)
        first = jnp.logical_and(n == 0, col < SWA_BLOCK)

        @pl.when(n == 0)
        def _():
            db_ref[...] = jnp.zeros_like(db_ref)
            dsk_ref[...] = jnp.zeros_like(dsk_ref)

        for r in range(r_blk):
            rs = slice(r * SWA_BLOCK, (r + 1) * SWA_BLOCK)
            kb = kb_all[r * SWA_BLOCK:(r + 2) * SWA_BLOCK]
            vb = vb_all[r * SWA_BLOCK:(r + 2) * SWA_BLOCK]
            dk_full = jnp.zeros((2 * SWA_BLOCK, LANE), F32)
            dv_full = jnp.zeros((2 * SWA_BLOCK, LANE), F32)
            dqs = []
            for g in range(SWA_GROUP):
                sl = slice((g // 2) * LANE, (g // 2 + 1) * LANE)
                hm = lo if g % 2 == 0 else jnp.logical_not(lo)
                qg = jnp.where(hm, q_ref[rs, sl], 0.0).astype(BF16)
                dog = jnp.where(hm, dy_ref[rs, sl], jnp.zeros((SWA_BLOCK, LANE), BF16))
                dmat = jnp.where(hm, y_ref[rs, sl].astype(F32) * dy_ref[rs, sl].astype(F32), 0.0)
                dg = jnp.sum(dmat, axis=1, keepdims=True)
                s = _dot(qg, kb, NT) * scale + b_ref[g]
                if r == 0:
                    s = jnp.where(first, NEG, s)
                lse_g = lse_ref[g, rs, :]
                p = jnp.exp(s - jnp.tile(lse_g, (1, 2)))
                dp = _dot(dog, vb, NT)
                ds = p * (dp - dg)
                dsb = ds.astype(BF16)
                dqs.append(_dot(dsb, kb, NN) * scale)
                dk_full = dk_full + _dot(dsb, qg, TN)
                dv_full = dv_full + _dot(p.astype(BF16), dog, TN)
                db_ref[g] += ds
                psink = jnp.exp(sk_ref[g:g + 1, :] - lse_g)
                dsk_ref[g:g + 1, :] -= jnp.sum(psink * dg, axis=0, keepdims=True)
            dq_ref[rs, 0:LANE] = jnp.where(lo, dqs[0], dqs[1]).astype(BF16)
            dq_ref[rs, LANE:2 * LANE] = jnp.where(lo, dqs[2], dqs[3]).astype(BF16)
            dkp_ref[r] = jnp.where(kvm, (dk_full + pltpu.roll(dk_full, HEAD, 1)) * scale, 0.0)
            dvp_ref[r] = jnp.where(kvm, dv_full + pltpu.roll(dv_full, HEAD, 1), 0.0)

    qblk = pl.BlockSpec((rows, 256), lambda kvh, n: (n, kvh))
    part = pl.BlockSpec((None, r_blk, 2 * SWA_BLOCK, LANE), lambda kvh, n: (kvh, n, 0, 0))
    bspec = pl.BlockSpec((None, SWA_GROUP, SWA_BLOCK, 256), lambda kvh, n: (kvh, 0, 0, 0))
    sspec = pl.BlockSpec((None, SWA_GROUP, LANE), lambda kvh, n: (kvh, 0, 0))
    return pl.pallas_call(
        body, name=name, grid=(2, nb // r_blk),
        out_shape=(jax.ShapeDtypeStruct((t, BRANCH), BF16),
                   jax.ShapeDtypeStruct((2, nb, 2 * SWA_BLOCK, LANE), F32),
                   jax.ShapeDtypeStruct((2, nb, 2 * SWA_BLOCK, LANE), F32),
                   jax.ShapeDtypeStruct((2, SWA_GROUP, SWA_BLOCK, 256), F32),
                   jax.ShapeDtypeStruct((2, SWA_GROUP, LANE), F32)),
        in_specs=[pl.BlockSpec((rows, 256), lambda kvh, n: (n, csq + kvh))] + _swa_specs(nb, r_blk, csk, csv)
        + [bspec, sspec, pl.BlockSpec((SWA_GROUP, rows, LANE), lambda kvh, n: (kvh, n, 0)), qblk, qblk],
        out_specs=(qblk, part, part, bspec, sspec),
        compiler_params=_params(("parallel", "arbitrary"), 12 << 20),
    )(proj, proj, proj, proj, proj, bias.reshape(2, SWA_GROUP, SWA_BLOCK, 256), sink_rep, lse, y, dy)


def _gate_fwd(proj, pb, name):
    t = proj.shape[0]
    tr = _pick(t, ROW_TILE // 2, BF16_SUBLANE)

    def body(g0, g1, g2, p0, p1, p2, o_ref):
        acc = jax.nn.sigmoid(g0[...]) * p0[...]
        acc = acc + jax.nn.sigmoid(g1[...]) * p1[...]
        acc = acc + jax.nn.sigmoid(g2[...]) * p2[...]
        o_ref[...] = acc.astype(BF16)

    row = pl.BlockSpec((tr, D_MODEL), lambda i: (i, 0))
    gates = [pl.BlockSpec((tr, D_MODEL), lambda i, b=b: (i, b)) for b in range(3)]
    return pl.pallas_call(
        body, name=name, grid=(t // tr,),
        out_shape=jax.ShapeDtypeStruct((t, D_MODEL), BF16),
        in_specs=gates + [row] * 3, out_specs=row,
        compiler_params=_params(("parallel",), 7 * _nbytes((tr, D_MODEL), F32)),
    )(proj, proj, proj, *pb)


def _gate_bwd(proj, pb, dmerged, name):
    t = proj.shape[0]
    tr = _pick(t, ROW_TILE // 2, BF16_SUBLANE)

    def body(g0, g1, g2, p0, p1, p2, dm_ref, dp0, dp1, dp2, dg_ref):
        dm = dm_ref[...]
        for b, (g_ref, p_ref, dp_ref) in enumerate(((g0, p0, dp0), (g1, p1, dp1), (g2, p2, dp2))):
            sg = jax.nn.sigmoid(g_ref[...])
            dp_ref[...] = (dm * sg).astype(BF16)
            dg_ref[:, b * D_MODEL:(b + 1) * D_MODEL] = (dm * p_ref[...] * sg * (1.0 - sg)).astype(BF16)

    row = pl.BlockSpec((tr, D_MODEL), lambda i: (i, 0))
    gates = [pl.BlockSpec((tr, D_MODEL), lambda i, b=b: (i, b)) for b in range(3)]
    return pl.pallas_call(
        body, name=name, grid=(t // tr,),
        out_shape=(jax.ShapeDtypeStruct((t, D_MODEL), BF16),) * 3 + (jax.ShapeDtypeStruct((t, 3 * D_MODEL), BF16),),
        in_specs=gates + [row] * 4,
        out_specs=(row, row, row, pl.BlockSpec((tr, 3 * D_MODEL), lambda i: (i, 0))),
        compiler_params=_params(("parallel",), 11 * _nbytes((tr, D_MODEL), F32)),
    )(proj, proj, proj, *pb, dmerged)


def _swiglu_fwd(ab, name):
    t = ab.shape[0]
    tr = _pick(t, ROW_TILE, BF16_SUBLANE)
    tc = D_FF_P // 2

    def body(a_ref, b_ref, o_ref):
        a = a_ref[...]
        o_ref[...] = (a * jax.nn.sigmoid(a) * b_ref[...]).astype(BF16)

    return pl.pallas_call(
        body, name=name, grid=(t // tr, 2),
        out_shape=jax.ShapeDtypeStruct((t, D_FF_P), BF16),
        in_specs=[pl.BlockSpec((tr, tc), lambda i, j: (i, j)), pl.BlockSpec((tr, tc), lambda i, j: (i, j + 2))],
        out_specs=pl.BlockSpec((tr, tc), lambda i, j: (i, j)),
        compiler_params=_params(("parallel", "parallel"), 3 * _nbytes((tr, tc), F32)),
    )(ab, ab)


def _swiglu_bwd(ab, dh, name):
    t = ab.shape[0]
    tr = _pick(t, ROW_TILE, BF16_SUBLANE)
    tc = D_FF_P // 2

    def body(a_ref, b_ref, dh_ref, da_ref, db_ref):
        a, b, d = a_ref[...], b_ref[...], dh_ref[...]
        sg = jax.nn.sigmoid(a)
        da_ref[...] = (d * b * (sg * (1.0 + a * (1.0 - sg)))).astype(BF16)
        db_ref[...] = (d * (a * sg)).astype(BF16)

    blk = pl.BlockSpec((tr, tc), lambda i, j: (i, j))
    return pl.pallas_call(
        body, name=name, grid=(t // tr, 2),
        out_shape=(jax.ShapeDtypeStruct((t, D_FF_P), BF16),) * 2,
        in_specs=[blk, pl.BlockSpec((tr, tc), lambda i, j: (i, j + 2)), blk],
        out_specs=(blk, blk),
        compiler_params=_params(("parallel", "parallel"), 5 * _nbytes((tr, tc), F32)),
    )(ab, ab, dh)


def _xattn_fwd(q, kv, name):
    t = q.shape[0]
    tq = _pick(t, ROW_TILE, BF16_SUBLANE)
    mlen = kv.shape[0]
    scale = X_HEAD ** -0.5

    def body(q_ref, kv_ref, o_ref):
        for h in range(X_HEADS):
            sl = slice(h * X_HEAD, (h + 1) * X_HEAD)
            kh = kv_ref[:, sl]
            vh = kv_ref[:, D_MODEL + h * X_HEAD:D_MODEL + (h + 1) * X_HEAD]
            s = _dot(q_ref[:, sl], kh, NT) * scale
            p = jnp.exp(s - jnp.max(s, axis=1, keepdims=True))
            l = jnp.sum(p, axis=1, keepdims=True)
            o_ref[:, sl] = (_dot(p.astype(BF16), vh, NN) / l).astype(BF16)

    return pl.pallas_call(
        body, name=name, grid=(t // tq,),
        out_shape=jax.ShapeDtypeStruct((t, D_MODEL), BF16),
        in_specs=[pl.BlockSpec((tq, D_MODEL), lambda i: (i, 0)), pl.BlockSpec((mlen, 2 * D_MODEL), lambda i: (0, 0))],
        out_specs=pl.BlockSpec((tq, D_MODEL), lambda i: (i, 0)),
        compiler_params=_params(("parallel",), 4 * _nbytes((tq, D_MODEL), F32)),
    )(q, kv)


def _xattn_bwd(q, kv, do, name):
    t = q.shape[0]
    tq = _pick(t, ROW_TILE, BF16_SUBLANE)
    mlen = kv.shape[0]
    scale = X_HEAD ** -0.5

    def body(q_ref, kv_ref, do_ref, dq_ref, dkv_ref):
        i = pl.program_id(0)

        @pl.when(i == 0)
        def _():
            dkv_ref[...] = jnp.zeros_like(dkv_ref)

        for h in range(X_HEADS):
            sl = slice(h * X_HEAD, (h + 1) * X_HEAD)
            vsl = slice(D_MODEL + h * X_HEAD, D_MODEL + (h + 1) * X_HEAD)
            qh, kh, vh, doh = q_ref[:, sl], kv_ref[:, sl], kv_ref[:, vsl], do_ref[:, sl]
            s = _dot(qh, kh, NT) * scale
            p = jnp.exp(s - jnp.max(s, axis=1, keepdims=True))
            p = p / jnp.sum(p, axis=1, keepdims=True)
            dp = _dot(doh, vh, NT)
            ds = p * (dp - jnp.sum(p * dp, axis=1, keepdims=True))
            dsb = ds.astype(BF16)
            dq_ref[:, sl] = (_dot(dsb, kh, NN) * scale).astype(BF16)
            dkv_ref[:, sl] += _dot(dsb, qh, TN) * scale
            dkv_ref[:, vsl] += _dot(p.astype(BF16), doh, TN)

    row = pl.BlockSpec((tq, D_MODEL), lambda i: (i, 0))
    whole = pl.BlockSpec((mlen, 2 * D_MODEL), lambda i: (0, 0))
    return pl.pallas_call(
        body, name=name, grid=(t // tq,),
        out_shape=(jax.ShapeDtypeStruct((t, D_MODEL), BF16), jax.ShapeDtypeStruct((mlen, 2 * D_MODEL), F32)),
        in_specs=[row, whole, row], out_specs=(row, whole),
        compiler_params=_params(("arbitrary",), 6 * _nbytes((tq, D_MODEL), F32)),
    )(q, kv, do)


def _position():
    return lax.axis_index("x"), lax.axis_index("y"), lax.axis_index("c")


N_PEER = N_DEV - 1


def _all_gather(xs, name):
    n = len(xs)

    def body(*refs):
        x_refs, out_refs = refs[:n], refs[n:2 * n]
        send_sems, recv_sems, local_sems = refs[2 * n:]
        mx, my, mc = _position()
        me, sib = (mx, my, mc), (mx, my, 1 - mc)
        chips = [(1 - mx, my), (mx, 1 - my), (1 - mx, 1 - my)]

        def slot(i, p):
            return out_refs[i].at[4 * p[0] + 2 * p[1] + p[2]]

        def copy(i, k, block, to, src=None):
            return pltpu.make_async_remote_copy(
                src_ref=slot(i, block) if src is None else src, dst_ref=slot(i, block),
                send_sem=send_sems.at[i * N_PEER + k], recv_sem=recv_sems.at[i * N_PEER + k],
                device_id=to, device_id_type=MESH)

        mine = [pltpu.make_async_copy(x_refs[i], slot(i, me), local_sems.at[i]) for i in range(n)]
        for cp in mine:
            cp.start()
        first = [copy(i, 1 + j, me, (*chip, mc), src=x_refs[i]) for j, chip in enumerate(chips) for i in range(n)]
        first += [copy(i, 0, me, sib, src=x_refs[i]) for i in range(n)]
        for cp in first:
            cp.start()
        passed = []
        for j, chip in enumerate(chips):
            for i in range(n):
                copy(i, 1 + j, (*chip, mc), me).wait_recv()
                passed.append(copy(i, 4 + j, (*chip, mc), sib))
                passed[-1].start()
        for i in range(n):
            copy(i, 0, sib, me).wait_recv()
        for j, chip in enumerate(chips):
            for i in range(n):
                copy(i, 4 + j, (*chip, 1 - mc), me).wait_recv()
        for cp in first + passed:
            cp.wait_send()
        for cp in mine:
            cp.wait()

    return pl.pallas_call(
        body, name=name,
        out_shape=tuple(jax.ShapeDtypeStruct((N_DEV,) + x.shape, x.dtype) for x in xs),
        in_specs=[pl.BlockSpec(memory_space=pl.ANY)] * n, out_specs=(pl.BlockSpec(memory_space=pl.ANY),) * n,
        scratch_shapes=[pltpu.SemaphoreType.DMA((n * N_PEER,)), pltpu.SemaphoreType.DMA((n * N_PEER,)),
                        pltpu.SemaphoreType.DMA((n,))],
    )(*xs)


PEER_RELS = [(dx, dy, dc) for dx in (0, 1) for dy in (0, 1) for dc in (0, 1)][1:]


def _peer_copy(rel_k, i, src_refs, land_refs, send_sems, recv_sems, layer, gather, arriving):
    mx, my, mc = _position()
    me_idx = 4 * mx + 2 * my + mc
    p = tuple((1 - v) if f else v for f, v in zip(PEER_RELS[rel_k], (mx, my, mc)))
    p_idx = 4 * p[0] + 2 * p[1] + p[2]
    src_slot, dst_slot = (me_idx, p_idx) if arriving else (p_idx, me_idx)
    src = src_refs[i] if gather else src_refs[i].at[src_slot]
    dst = land_refs[i].at[dst_slot] if layer is None else land_refs[i].at[dst_slot, layer]
    return pltpu.make_async_remote_copy(
        src_ref=src, dst_ref=dst, send_sem=send_sems.at[i * N_PEER + rel_k], recv_sem=recv_sems.at[i * N_PEER + rel_k],
        device_id=p, device_id_type=MESH)


HBM_SPEC = pl.BlockSpec(memory_space=pltpu.HBM)
SEM_SPEC = pl.BlockSpec(memory_space=pltpu.SEMAPHORE)
SIDE_EFFECT = pltpu.SideEffectType.DATAFLOW_SIDE_EFFECTING


def _own_slots(srcs, lands, layer, gather):
    mx, my, mc = _position()
    me_idx = 4 * mx + 2 * my + mc
    out = []
    for s, land in zip(srcs, lands):
        piece = s[None] if gather else lax.dynamic_index_in_dim(s, me_idx, 0, keepdims=True)
        if layer is None:
            start = (me_idx,) + (0,) * (land.ndim - 1)
        else:
            piece, start = piece[:, None], (me_idx, layer) + (0,) * (land.ndim - 2)
        out.append(lax.dynamic_update_slice(land, piece, start))
    return out


def _swap_start(srcs, lands, layer, gather, name):
    n = len(srcs)

    def body(*refs):
        src_refs, land_refs = refs[:n], refs[n:2 * n]
        send_sems, recv_sems = refs[2 * n], refs[2 * n + 1]
        token = refs[4 * n + 2]
        for i in range(n):
            for k in range(N_PEER):
                _peer_copy(k, i, src_refs, land_refs, send_sems, recv_sems, layer, gather, False).start()
        token[...] = jnp.zeros_like(token)

    hbm = [pltpu.with_memory_space_constraint(a, pltpu.HBM) for a in list(srcs) + list(lands)]
    out = pl.pallas_call(
        body, name=name,
        out_shape=(pltpu.SemaphoreType.DMA((n * N_PEER,)), pltpu.SemaphoreType.DMA((n * N_PEER,)))
        + tuple(pltpu.HBM(a.shape, a.dtype) for a in hbm) + (jax.ShapeDtypeStruct((8, LANE), F32),),
        in_specs=[HBM_SPEC] * (2 * n),
        out_specs=(SEM_SPEC, SEM_SPEC) + (HBM_SPEC,) * (2 * n) + (pl.BlockSpec(memory_space=pltpu.VMEM),),
        input_output_aliases={i: 2 + i for i in range(2 * n)},
        compiler_params=pltpu.CompilerParams(has_side_effects=SIDE_EFFECT),
    )(*hbm)
    return out[0], out[1], list(out[2:2 + n]), list(out[2 + n:2 + 2 * n]), out[2 + 2 * n]


def _swap_wait(send_sems, recv_sems, srcs, lands, after, layer, gather, name):
    n = len(srcs)

    def body(*refs):
        src_refs, land_refs = refs[:n], refs[n:2 * n]
        send_sems_ref, recv_sems_ref = refs[2 * n], refs[2 * n + 1]
        for i in range(n):
            for k in range(N_PEER):
                args = (src_refs, land_refs, send_sems_ref, recv_sems_ref, layer, gather)
                _peer_copy(k, i, *args, False).wait_send()
                _peer_copy(k, i, *args, True).wait_recv()

    out = pl.pallas_call(
        body, name=name,
        out_shape=tuple(pltpu.HBM(a.shape, a.dtype) for a in list(srcs) + list(lands)),
        in_specs=[HBM_SPEC] * (2 * n) + [SEM_SPEC, SEM_SPEC, pl.BlockSpec(memory_space=pl.ANY)],
        out_specs=(HBM_SPEC,) * (2 * n),
        input_output_aliases={i: i for i in range(2 * n)},
        compiler_params=pltpu.CompilerParams(has_side_effects=SIDE_EFFECT),
    )(*srcs, *lands, send_sems, recv_sems, after)
    return list(out[n:])


ADAMW_BLOCK_BYTES = 1 << 20


def _adamw(parts, w, m, v, name):
    r, l = w.shape
    tr = _pick(r, max(ADAMW_BLOCK_BYTES // (4 * l), BF16_SUBLANE), BF16_SUBLANE)
    c1 = 1.0 - ADAM_B1 ** ADAM_STEP
    c2 = 1.0 - ADAM_B2 ** ADAM_STEP

    def body(p_ref, w_ref, m_ref, v_ref, g_ref, d_ref, nm_ref, nv_ref):
        g = p_ref[0].astype(F32)
        for s in range(1, N_DEV):
            g = g + p_ref[s].astype(F32)
        nm = ADAM_B1 * m_ref[...] + (1.0 - ADAM_B1) * g
        nv = ADAM_B2 * v_ref[...] + (1.0 - ADAM_B2) * (g * g)
        m_hat = nm / c1
        v_hat = nv / c2
        g_ref[...] = g
        d_ref[...] = -ADAM_LR * (m_hat / (jnp.sqrt(v_hat) + ADAM_EPS) + ADAM_WD * w_ref[...])
        nm_ref[...] = nm
        nv_ref[...] = nv

    row = pl.BlockSpec((tr, l), lambda i: (i, 0))
    return pl.pallas_call(
        body, name=name, grid=(r // tr,),
        out_shape=(jax.ShapeDtypeStruct((r, l), F32),) * 4,
        in_specs=[pl.BlockSpec((N_DEV, tr, l), lambda i: (0, i, 0)), row, row, row],
        out_specs=(row,) * 4,
        compiler_params=_params(("parallel",), 12 * _nbytes((tr, l), F32)),
    )(parts, w, m, v)


MATRIX_WEIGHTS = (("w_in", 2), ("conv_w", 2), ("w_branch", 3), ("w_mix_out", 1), ("w_xq", 1), ("w_xkv", 2),
                  ("w_xo", 1), ("w_ffn_gate", 2), ("w_ffn_up", 2), ("w_ffn_down", 1))
SMALL_PARAMS = ("mix_norm_g", "xattn_norm_g", "mem_norm_g", "ffn_norm_g", "final_norm_g", "forget_bias", "sink",
                "rel_bias")


def _pack_small(pieces):
    flat = jnp.concatenate([p.astype(F32).reshape(-1) for p in pieces])
    total = -(-flat.shape[0] // (8 * LANE)) * (8 * LANE)
    return jnp.pad(flat, (0, total - flat.shape[0])).reshape(total // LANE, LANE)


def _rows(a):
    return a.reshape(-1, a.shape[-1])


def _to_full(gathered, axis):
    moved = jnp.moveaxis(gathered, 0, axis)
    shape = list(moved.shape)
    shape[axis:axis + 2] = [shape[axis] * shape[axis + 1]]
    return moved.reshape(shape)


def _to_blocks(full, axis):
    shape = list(full.shape)
    shape[axis:axis + 1] = [N_DEV, shape[axis] // N_DEV]
    return jnp.moveaxis(full.reshape(shape), axis, 0)


def _perm_in(w_in):
    pad = jnp.zeros((w_in.shape[0], PROJ_COLS - IN_COLS), w_in.dtype)
    return jnp.concatenate([w_in[:, 3848:6920], w_in[:, 0:3072], w_in[:, 3080:3848], w_in[:, 3072:3080], pad], axis=1)


def _unperm_in(dw):
    return jnp.concatenate([dw[:, 3072:6144], dw[:, 6912:6920], dw[:, 6144:6912], dw[:, 0:3072]], axis=1)


def _layer_fwd(l, x, mem, wt, sm, pre_branch=None):
    t = x.shape[0]
    tag = f"l{l}_"
    h = _rms_fwd(x, sm["mix_norm_g"][l], tag + "mix_norm")
    proj = _matmul(h, wt["w_in"][l], "nn", F32, tag + "in_proj")
    y_conv = _conv_fwd(proj, wt["conv_w"][l], tag + "conv")
    fbias_row = jnp.pad(sm["forget_bias"][l], (0, LANE - 8)).reshape(1, LANE)
    c = _logf_cumsum(proj, fbias_row, tag + "logf_cumsum")
    c8 = c[:, :8].T
    c_col = jnp.broadcast_to(c8[:, :, None], (8, t, LANE))
    c_row = c8.reshape(8, 1, t)
    y_fox, lse_fox = _fox_fwd(proj, c_col, c_row, tag + "fox")
    onehot, bias = sm["swa_tables"]
    sink_rep = jnp.broadcast_to(sm["sink"][l].reshape(2, SWA_GROUP, 1), (2, SWA_GROUP, LANE))
    y_swa, lse_swa = _swa_fwd(proj, bias, sink_rep, tag + "swa")
    ys = (y_conv, y_fox, y_swa)
    if pre_branch is not None:
        pre_branch(y_swa)
    pb = tuple(_matmul(ys[b], wt["w_branch"][l][b], "nn", F32, tag + f"branch{b}") for b in range(3))
    merged = _gate_fwd(proj, pb, tag + "gate")
    x1 = _matmul(merged, wt["w_mix_out"][l], "nn", F32, tag + "mix_out", residual=x)
    xn2 = _rms_fwd(x1, sm["xattn_norm_g"][l], tag + "xattn_norm")
    q = _matmul(xn2, wt["w_xq"][l], "nn", BF16, tag + "xq")
    mem_n = _rms_fwd(mem, sm["mem_norm_g"][l], tag + "mem_norm")
    kv = _matmul(mem_n, wt["w_xkv"][l], "nn", BF16, tag + "xkv")
    o = _xattn_fwd(q, kv, tag + "xattn")
    x2 = _matmul(o, wt["w_xo"][l], "nn", F32, tag + "xo", residual=x1)
    xn3 = _rms_fwd(x2, sm["ffn_norm_g"][l], tag + "ffn_norm")
    ab = _matmul(xn3, wt["w_gu"][l], "nn", F32, tag + "ffn_gu")
    h1 = _swiglu_fwd(ab, tag + "swiglu")
    x3 = _matmul(h1, wt["w_ffn_down"][l], "nn", F32, tag + "ffn_down", residual=x2)
    saved = dict(x=x, h=h, proj=proj, fbias_row=fbias_row, c_col=c_col, c_row=c_row, ys=ys, lse_fox=lse_fox,
                 onehot=onehot, bias=bias, sink_rep=sink_rep, lse_swa=lse_swa, pb=pb, merged=merged, x1=x1,
                 xn2=xn2, q=q, mem_n=mem_n, kv=kv, o=o, x2=x2, xn3=xn3, ab=ab, h1=h1)
    return x3, saved


def _layer_bwd(l, dx3, dx3_b, mem, wt, sm, sv, mid_hook=None, late_hook=None):
    t = dx3.shape[0]
    nb = t // SWA_BLOCK
    tag = f"l{l}_b_"
    gw, gs = {}, {}
    dh1 = _matmul(dx3_b, wt["w_ffn_down"][l], "nt", F32, tag + "d_h1")
    gw["w_ffn_down"] = _matmul(sv["h1"], dx3_b, "tn", F32, tag + "dw_down")
    da, db = _swiglu_bwd(sv["ab"], dh1, tag + "swiglu")
    dxn3 = _matmul(da, wt["w_ffn_gate"][l], "nt", F32, tag + "d_xn3_gate")
    dxn3 = _matmul(db, wt["w_ffn_up"][l], "nt", F32, tag + "d_xn3_up", residual=dxn3)
    gw["w_ffn_gate"] = _matmul(sv["xn3"], da, "tn", F32, tag + "dw_gate")
    gw["w_ffn_up"] = _matmul(sv["xn3"], db, "tn", F32, tag + "dw_up")
    dx2, dx2_b, gs["ffn_norm_g"] = _rms_bwd(sv["x2"], sm["ffn_norm_g"][l], dxn3, dx3, tag + "ffn_norm")
    do = _matmul(dx2_b, wt["w_xo"][l], "nt", BF16, tag + "d_o")
    gw["w_xo"] = _matmul(sv["o"], dx2_b, "tn", F32, tag + "dw_xo")
    dq, dkv = _xattn_bwd(sv["q"], sv["kv"], do, tag + "xattn")
    gw["w_xkv"] = _matmul(sv["mem_n"], dkv, "tn", F32, tag + "dw_xkv")
    dmem_n = _matmul(dkv, wt["w_xkv"][l], "nt", F32, tag + "d_memn")
    _, _, gs["mem_norm_g"] = _rms_bwd(mem, sm["mem_norm_g"][l], dmem_n, None, tag + "mem_norm")
    gw["w_xq"] = _matmul(sv["xn2"], dq, "tn", F32, tag + "dw_xq")
    dxn2 = _matmul(dq, wt["w_xq"][l], "nt", F32, tag + "d_xn2")
    dx1, dx1_b, gs["xattn_norm_g"] = _rms_bwd(sv["x1"], sm["xattn_norm_g"][l], dxn2, dx2, tag + "xattn_norm")
    dmerged = _matmul(dx1_b, wt["w_mix_out"][l], "nt", F32, tag + "d_merged")
    gw["w_mix_out"] = _matmul(sv["merged"], dx1_b, "tn", F32, tag + "dw_mix_out")
    dp0, dp1, dp2, dgate = _gate_bwd(sv["proj"], sv["pb"], dmerged, tag + "gate")
    dps = (dp0, dp1, dp2)
    dy_dtypes = (F32, BF16, BF16)
    dys = [_matmul(dps[b], wt["w_branch"][l][b], "nt", dy_dtypes[b], tag + f"d_y{b}") for b in range(3)]
    gw["w_branch"] = jnp.stack(
        [_matmul(sv["ys"][b], dps[b], "tn", F32, tag + f"dw_branch{b}") for b in range(3)])
    sink_rep = sv["sink_rep"] if mid_hook is None else sv["sink_rep"] + mid_hook(gw)
    dsq, dkp, dvp, dbias, dsink = _swa_bwd(sv["proj"], sv["bias"], sink_rep, sv["lse_swa"], sv["ys"][2],
                                           dys[2], tag + "swa")

    def band_add(part):
        tot = part[0] + part[1]
        cur = tot[:, SWA_BLOCK:, :]
        nxt = jnp.concatenate([tot[1:, :SWA_BLOCK, :], jnp.zeros((1, SWA_BLOCK, LANE), F32)], axis=0)
        return (cur + nxt).reshape(t, LANE).astype(BF16)

    dsk, dsv = band_add(dkp), band_add(dvp)
    gs["rel_bias_l"] = jnp.einsum("hts,tsb->bh", dbias.reshape(8, SWA_BLOCK, 2 * SWA_BLOCK), sv["onehot"],
                                  precision=lax.Precision.HIGHEST)
    gs["sink"] = dsink[:, :, 0].reshape(8)
    dfq, dfk, dfv, dck, dcq_row = _fox_bwd(sv["proj"], sv["c_col"], sv["c_row"], sv["lse_fox"], sv["ys"][1], dys[1],
                                           tag + "fox_bwd")
    dcq = jnp.pad(dcq_row.reshape(8, t).T, ((0, 0), (0, LANE - 8))).reshape(1, t, LANE)
    dfg, dfb = _logf_cumsum_bwd(sv["proj"], sv["fbias_row"], [dck, dcq], tag + "logf_cumsum")
    gs["forget_bias"] = dfb[0, :8]
    dcb, dcc, dcu, dconv = _conv_bwd(sv["proj"], wt["conv_w"][l], dys[0], tag + "conv")
    gw["conv_w"] = dconv[:3]
    dproj = jnp.concatenate([dgate, dcb, dcc, dcu, dfq, dfk, dfv, dsq, dsk, dsv, dfg], axis=1)
    gw["w_in"] = _unperm_in(_matmul(sv["h"], dproj, "tn", F32, tag + "dw_in"))
    g_mix = sm["mix_norm_g"][l] if late_hook is None else sm["mix_norm_g"][l] + late_hook(gw)
    dh = _matmul(dproj, wt["w_in"][l], "nt", F32, tag + "d_h")
    dx, dx_b, gs["mix_norm_g"] = _rms_bwd(sv["x"], g_mix, dh, dx1, tag + "mix_norm")
    return dx, dx_b, gw, gs


def kernel(x, mem, mix_norm_g, w_in, forget_bias, conv_w, sink, w_branch, w_mix_out, rel_bias, xattn_norm_g, mem_norm_g, w_xq, w_xkv, w_xo, ffn_norm_g, w_ffn_gate, w_ffn_up, w_ffn_down, final_norm_g, loss_target, m_mix_norm_g, m_w_in, m_forget_bias, m_conv_w, m_sink, m_w_branch, m_w_mix_out, m_rel_bias, m_xattn_norm_g, m_mem_norm_g, m_w_xq, m_w_xkv, m_w_xo, m_ffn_norm_g, m_w_ffn_gate, m_w_ffn_up, m_w_ffn_down, m_final_norm_g, v_mix_norm_g, v_w_in, v_forget_bias, v_conv_w, v_sink, v_w_branch, v_w_mix_out, v_rel_bias, v_xattn_norm_g, v_mem_norm_g, v_w_xq, v_w_xkv, v_w_xo, v_ffn_norm_g, v_w_ffn_gate, v_w_ffn_up, v_w_ffn_down, v_final_norm_g):
    args = dict(locals())
    names = [n for n, _ in MATRIX_WEIGHTS] + list(SMALL_PARAMS)
    w = {n: args[n] for n in names}
    mo = {n: args["m_" + n] for n in names}
    vo = {n: args["v_" + n] for n in names}
    x2d, mem2d, tgt = x[0], mem[0], loss_target[0]

    wire = {n: (F32 if n == "conv_w" else BF16) for n, _ in MATRIX_WEIGHTS}
    late = ("w_in", "conv_w")
    early_w = [(n, ax) for n, ax in MATRIX_WEIGHTS if n not in late]
    late_w = [(n, ax) for n, ax in MATRIX_WEIGHTS if n in late]
    wt = {n: [None] * DEPTH for n, _ in MATRIX_WEIGHTS}
    wt["w_gu"] = [None] * DEPTH

    ff_pad = FF_SHARD_P - FF_SHARD
    ff_axis = {"w_ffn_gate": 2, "w_ffn_up": 2, "w_ffn_down": 1}

    def pad_ffn(n, blocks):
        if n not in ff_axis:
            return blocks
        return jnp.pad(blocks, [(0, ff_pad if d == ff_axis[n] else 0) for d in range(blocks.ndim)])

    def unpad_ffn(n, blocks):
        return lax.slice_in_dim(blocks, 0, FF_SHARD, axis=ff_axis[n]) if n in ff_axis else blocks

    def place_weights(l, which, gathered):
        for (n, ax), g in zip(which, gathered):
            wt[n][l] = _to_full(pad_ffn(n, g), ax - 1)
        if "w_in" in dict(which):
            wt["w_in"][l] = _perm_in(wt["w_in"][l])
        if "w_ffn_gate" in dict(which):
            wt["w_gu"][l] = jnp.concatenate([wt["w_ffn_gate"][l], wt["w_ffn_up"][l]], axis=1)

    def shards_of(l, which):
        return [w[n][l].astype(wire[n]) for n, _ in which]

    def start_gather(srcs, name):
        lands = _own_slots(srcs, [lax.empty((N_DEV,) + s.shape, s.dtype) for s in srcs], None, True)
        return _swap_start(srcs, lands, None, True, name)

    place_weights(0, late_w, _all_gather(shards_of(0, late_w), "weights_gather_l0_first"))
    r_send, r_recv, r_srcs, r_lands, token = start_gather(shards_of(0, early_w), "weights_gather_l0_rest_start")
    shards1 = shards_of(1, MATRIX_WEIGHTS)
    shards1[0] = shards1[0] + token[0, 0].astype(shards1[0].dtype)
    w_send, w_recv, w_srcs, lands, token = start_gather(shards1, "weights_gather_l1_start")
    sm = {n: w[n] for n in SMALL_PARAMS}
    sm["mix_norm_g"] = w["mix_norm_g"].at[0].add(token[0, 0])
    sm["swa_tables"] = _swa_tables(w["rel_bias"])

    def rest_of_layer0(after):
        place_weights(0, early_w, _swap_wait(r_send, r_recv, r_srcs, r_lands, after, None, True,
                                             "weights_gather_l0_rest_wait"))

    saved = []
    xc = x2d
    for l in range(DEPTH):
        if l == 1:
            place_weights(1, MATRIX_WEIGHTS,
                          _swap_wait(w_send, w_recv, w_srcs, lands, xc, None, True, "weights_gather_l1_wait"))
        xc, sv = _layer_fwd(l, xc, mem2d, wt, sm, rest_of_layer0 if l == 0 else None)
        saved.append(sv)
    loss_row, dx, dx_b, dg_final = _loss_head(xc, sm["final_norm_g"], tgt, "loss_head")
    loss = lax.psum(loss_row[0, 0], ("x", "y", "c"))

    def grad_parts(gw, which):
        return [unpad_ffn(n, _to_blocks(gw[n], ax - 1)).astype(wire[n]) for n, ax in which]

    gw_all, gs_all = [None] * DEPTH, [None] * DEPTH
    dx, dx_b, gw_all[1], gs_all[1] = _layer_bwd(1, dx, dx_b, mem2d, wt, sm, saved[1])
    parts1 = grad_parts(gw_all[1], MATRIX_WEIGHTS)
    zones = _own_slots(parts1, [lax.empty((N_DEV, DEPTH) + p.shape[1:], p.dtype) for p in parts1], 1, False)
    g_send, g_recv, g_srcs, zones, token = _swap_start(parts1, zones, 1, False, "grads_exchange_l1_start")
    sm_b = dict(sm)
    sm_b["ffn_norm_g"] = sm["ffn_norm_g"].at[0].add(token[0, 0])
    mid = {}

    def mid_hook(gw):
        zone = dict(zip([n for n, _ in MATRIX_WEIGHTS],
                        _swap_wait(g_send, g_recv, g_srcs, zones, gw["w_mix_out"], 1, False, "grads_exchange_l1_wait")))
        parts0 = grad_parts(gw, early_w)
        early_zones = _own_slots(parts0, [zone[n] for n, _ in early_w], 0, False)
        mid["early"] = _swap_start(parts0, early_zones, 0, False, "grads_exchange_l0_early_start")
        mid["late_zones"] = [zone[n] for n, _ in late_w]
        return mid["early"][4][0, 0]

    def late_hook(gw):
        parts0 = grad_parts(gw, late_w)
        late_zones = _own_slots(parts0, mid["late_zones"], 0, False)
        mid["late"] = _swap_start(parts0, late_zones, 0, False, "grads_exchange_l0_late_start")
        return mid["late"][4][0, 0]

    dx, dx_b, gw_all[0], gs_all[0] = _layer_bwd(0, dx, dx_b, mem2d, wt, sm_b, saved[0], mid_hook, late_hook)
    grad_x = dx[None]
    recv_by_name = {}
    for key, which in (("early", early_w), ("late", late_w)):
        s_send, s_recv, s_srcs, s_zones, _ = mid[key]
        recv_by_name.update(zip([n for n, _ in which],
                                _swap_wait(s_send, s_recv, s_srcs, s_zones, dx, 0, False,
                                           "grads_exchange_l0_" + key + "_wait")))
    recv = [recv_by_name[n] for n, _ in MATRIX_WEIGHTS]

    outs = {}
    for (n, _), r in zip(MATRIX_WEIGHTS, recv):
        res = _adamw(r.reshape((N_DEV,) + _rows(w[n]).shape), _rows(w[n]), _rows(mo[n]), _rows(vo[n]), "adamw_" + n)
        outs[n] = [o.reshape(w[n].shape) for o in res]

    gsm = {n: jnp.stack([gs_all[l][n] for l in range(DEPTH)])
           for n in ("mix_norm_g", "xattn_norm_g", "mem_norm_g", "ffn_norm_g", "forget_bias", "sink")}
    gsm["final_norm_g"] = dg_final
    gsm["rel_bias"] = gs_all[0]["rel_bias_l"] + gs_all[1]["rel_bias_l"]
    (small_parts,) = _all_gather([_pack_small([gsm[n] for n in SMALL_PARAMS])], "small_grads_all_gather")
    outs_small = _adamw(small_parts, *[_pack_small([d[n] for n in SMALL_PARAMS]) for d in (w, mo, vo)], "adamw_small")
    for kind in range(4):
        flat, o = outs_small[kind].reshape(-1), 0
        for n in SMALL_PARAMS:
            sz = int(np.prod(w[n].shape))
            outs.setdefault(n, []).append(flat[o:o + sz].reshape(w[n].shape))
            o += sz

    order = ["mix_norm_g", "w_in", "forget_bias", "conv_w", "sink", "w_branch", "w_mix_out", "rel_bias",
             "xattn_norm_g", "mem_norm_g", "w_xq", "w_xkv", "w_xo", "ffn_norm_g", "w_ffn_gate", "w_ffn_up",
             "w_ffn_down", "final_norm_g"]
    result = [loss, grad_x]
    for kind in range(4):
        result += [outs[n][kind] for n in order]
    return tuple(result)
```

```python
import math

import numpy as np
import jax
import jax.numpy as jnp
from jax import lax
from jax.experimental import pallas as pl
from jax.experimental.pallas import tpu as pltpu

F32 = jnp.float32
BF16 = jnp.bfloat16
MESH = pl.DeviceIdType.MESH

LANE = 128
BF16_SUBLANE = 16
V7X_VMEM_REQUEST_CAP = 56 * 2 ** 20
N_DEV = 8

D_MODEL = 1024
DEPTH = 2
HEAD = 64
BRANCH = 512
SWA_BLOCK = 128
SWA_GROUP = 4
N_BUCKETS = 32
X_HEADS = 4
X_HEAD = 256
D_FF = 2816
FF_SHARD = D_FF // N_DEV
FF_SHARD_P = -(-FF_SHARD // LANE) * LANE
D_FF_P = N_DEV * FF_SHARD_P
RMS_EPS = 1e-6
NEG = -1e30
ADAM_LR, ADAM_B1, ADAM_B2, ADAM_EPS, ADAM_WD, ADAM_STEP = 0.001, 0.9, 0.999, 1e-08, 0.01, 10

IN_COLS = 6920
PROJ_COLS = 7040
COL_GATE, COL_CONV, COL_FOX, COL_SQ, COL_SK, COL_SV, COL_FG = 0, 3072, 4608, 6144, 6656, 6784, 6912

ROW_TILE = 512
FOX_TILE = 1024
MM_TM, MM_TN, MM_TK = 1024, 1536, 2048


def _pick(n, cap, mult):
    best = None
    for d in range(mult, min(n, cap) + 1, mult):
        if n % d == 0:
            best = d
    return n if best is None else best


def _params(semantics, block_bytes):
    limit = int(min(max(2 * block_bytes + (8 << 20), 24 << 20), V7X_VMEM_REQUEST_CAP))
    return pltpu.CompilerParams(dimension_semantics=semantics, vmem_limit_bytes=limit)


def _nbytes(shape, dtype):
    return int(np.prod(shape)) * jnp.dtype(dtype).itemsize


def _dot(a, b, dims):
    return lax.dot_general(a, b, (dims, ((), ())), preferred_element_type=F32)


NN = ((1,), (0,))
NT = ((1,), (1,))
TN = ((0,), (0,))


def _matmul(a, b, mode, out_dtype, name, residual=None):
    if mode == "nn":
        (m, k), (k2, n) = a.shape, b.shape
    elif mode == "nt":
        (m, k), (n, k2) = a.shape, b.shape
    else:
        (k, m), (k2, n) = a.shape, b.shape
    assert k == k2, (name, a.shape, b.shape)
    tm, tn, tk = _pick(m, MM_TM, LANE), _pick(n, MM_TN, LANE), _pick(k, MM_TK, LANE)
    nk = k // tk
    dims = {"nn": NN, "nt": NT, "tn": TN}[mode]
    has_res = residual is not None

    def body(*refs):
        a_ref, b_ref = refs[0], refs[1]
        r_ref = refs[2] if has_res else None
        o_ref = refs[3] if has_res else refs[2]
        kk = pl.program_id(2)
        p = _dot(a_ref[...].astype(BF16), b_ref[...].astype(BF16), dims)
        if nk == 1:
            if has_res:
                p = p + r_ref[...]
            o_ref[...] = p.astype(out_dtype)
        else:
            acc_ref = refs[-1]

            @pl.when(kk == 0)
            def _():
                acc_ref[...] = p

            @pl.when(kk > 0)
            def _():
                acc_ref[...] += p

            @pl.when(kk == nk - 1)
            def _():
                res = acc_ref[...]
                if has_res:
                    res = res + r_ref[...]
                o_ref[...] = res.astype(out_dtype)

    if mode == "nn":
        a_spec = pl.BlockSpec((tm, tk), lambda i, j, kk: (i, kk))
        b_spec = pl.BlockSpec((tk, tn), lambda i, j, kk: (kk, j))
    elif mode == "nt":
        a_spec = pl.BlockSpec((tm, tk), lambda i, j, kk: (i, kk))
        b_spec = pl.BlockSpec((tn, tk), lambda i, j, kk: (j, kk))
    else:
        a_spec = pl.BlockSpec((tk, tm), lambda i, j, kk: (kk, i))
        b_spec = pl.BlockSpec((tk, tn), lambda i, j, kk: (kk, j))
    o_spec = pl.BlockSpec((tm, tn), lambda i, j, kk: (i, j))
    in_specs, args = [a_spec, b_spec], [a, b]
    if has_res:
        in_specs.append(o_spec)
        args.append(residual)
    blk = (_nbytes((tm, tk), a.dtype) + _nbytes((tk, tn), b.dtype) + _nbytes((tm, tn), out_dtype)
           + (_nbytes((tm, tn), F32) if has_res else 0))
    scratch = [pltpu.VMEM((tm, tn), F32)] if nk > 1 else []
    return pl.pallas_call(
        body, name=name, grid=(m // tm, n // tn, nk),
        out_shape=jax.ShapeDtypeStruct((m, n), out_dtype),
        in_specs=in_specs, out_specs=o_spec, scratch_shapes=scratch,
        compiler_params=_params(("parallel", "parallel", "arbitrary"), blk + _nbytes((tm, tn), F32)),
    )(*args)


def _rms_fwd(x, g, name):
    t, d = x.shape
    tr = _pick(t, ROW_TILE, BF16_SUBLANE)

    def body(x_ref, g_ref, y_ref):
        xv = x_ref[...]
        r = lax.rsqrt(jnp.mean(xv * xv, axis=-1, keepdims=True) + RMS_EPS)
        y_ref[...] = ((xv * r) * g_ref[...]).astype(BF16)

    return pl.pallas_call(
        body, name=name, grid=(t // tr,),
        out_shape=jax.ShapeDtypeStruct((t, d), BF16),
        in_specs=[pl.BlockSpec((tr, d), lambda i: (i, 0)), pl.BlockSpec((1, d), lambda i: (0, 0))],
        out_specs=pl.BlockSpec((tr, d), lambda i: (i, 0)),
        compiler_params=_params(("parallel",), 2 * _nbytes((tr, d), F32)),
    )(x, g.reshape(1, d))


def _rms_bwd(x, g, dy, dres, name):
    t, d = x.shape
    tr = _pick(t, ROW_TILE, BF16_SUBLANE)
    has_res = dres is not None

    def body(*refs):
        x_ref, g_ref, dy_ref = refs[:3]
        r_ref = refs[3] if has_res else None
        dx_ref, dxb_ref, dg_ref = refs[-3:]
        i = pl.program_id(0)
        xv = x_ref[...]
        r = lax.rsqrt(jnp.mean(xv * xv, axis=-1, keepdims=True) + RMS_EPS)
        xh = xv * r
        dyv = dy_ref[...].astype(F32)
        dxh = dyv * g_ref[...]
        dx = r * (dxh - xh * jnp.mean(dxh * xh, axis=-1, keepdims=True))
        if has_res:
            dx = dx + r_ref[...]
        dx_ref[...] = dx
        dxb_ref[...] = dx.astype(BF16)

        @pl.when(i == 0)
        def _():
            dg_ref[...] = jnp.zeros_like(dg_ref)

        dg_ref[...] += jnp.sum(dyv * xh, axis=0, keepdims=True)

    row = pl.BlockSpec((tr, d), lambda i: (i, 0))
    vec = pl.BlockSpec((1, d), lambda i: (0, 0))
    in_specs, args = [row, vec, row], [x, g.reshape(1, d), dy]
    if has_res:
        in_specs.append(row)
        args.append(dres)
    return pl.pallas_call(
        body, name=name, grid=(t // tr,),
        out_shape=(jax.ShapeDtypeStruct((t, d), F32), jax.ShapeDtypeStruct((t, d), BF16),
                   jax.ShapeDtypeStruct((1, d), F32)),
        in_specs=in_specs, out_specs=(row, row, vec),
        compiler_params=_params(("arbitrary",), 5 * _nbytes((tr, d), F32)),
    )(*args)


def _loss_head(x, g, target, name):
    t, d = x.shape
    tr = _pick(t, ROW_TILE, BF16_SUBLANE)

    def body(x_ref, g_ref, t_ref, loss_ref, dx_ref, dxb_ref, dg_ref):
        i = pl.program_id(0)
        xv = x_ref[...]
        gv = g_ref[...]
        r = lax.rsqrt(jnp.mean(xv * xv, axis=-1, keepdims=True) + RMS_EPS)
        xh = xv * r
        diff = xh * gv - t_ref[...]
        part = 0.5 * jnp.sum(jnp.mean(diff * diff, axis=-1, keepdims=True), axis=0, keepdims=True)
        dyv = diff * (1.0 / d)
        dxh = dyv * gv
        dx = r * (dxh - xh * jnp.mean(dxh * xh, axis=-1, keepdims=True))
        dx_ref[...] = dx
        dxb_ref[...] = dx.astype(BF16)

        @pl.when(i == 0)
        def _():
            dg_ref[...] = jnp.zeros_like(dg_ref)
            loss_ref[...] = jnp.zeros_like(loss_ref)

        dg_ref[...] += jnp.sum(dyv * xh, axis=0, keepdims=True)
        loss_ref[...] += jnp.broadcast_to(part, loss_ref.shape)

    row = pl.BlockSpec((tr, d), lambda i: (i, 0))
    vec = pl.BlockSpec((1, d), lambda i: (0, 0))
    return pl.pallas_call(
        body, name=name, grid=(t // tr,),
        out_shape=(jax.ShapeDtypeStruct((1, LANE), F32), jax.ShapeDtypeStruct((t, d), F32),
                   jax.ShapeDtypeStruct((t, d), BF16), jax.ShapeDtypeStruct((1, d), F32)),
        in_specs=[row, vec, row],
        out_specs=(pl.BlockSpec((1, LANE), lambda i: (0, 0)), row, row, vec),
        compiler_params=_params(("arbitrary",), 5 * _nbytes((tr, d), F32)),
    )(x, g.reshape(1, d), target)


HALO = 8


def _conv_fwd(proj, conv_w, name):
    t = proj.shape[0]
    tr = _pick(t, ROW_TILE, BF16_SUBLANE)
    c0 = COL_CONV // BRANCH
    hb = tr // HALO

    def body(cb_ref, cc_ref, cu_ref, hc_ref, hu_ref, w_ref, y_ref):
        i = pl.program_id(0)
        z = cc_ref[...] * cu_ref[...]
        hz = jnp.where(i > 0, hc_ref[...] * hu_ref[...], 0.0)
        zf = jnp.concatenate([hz, z], axis=0)
        z1 = pltpu.roll(zf, 1, 0)[HALO:]
        z2 = pltpu.roll(zf, 2, 0)[HALO:]
        y = w_ref[2:3, :] * z + w_ref[1:2, :] * z1 + w_ref[0:1, :] * z2
        y_ref[...] = (cb_ref[...] * y).astype(BF16)

    def col(c):
        return pl.BlockSpec((tr, BRANCH), lambda i, c=c: (i, c0 + c))

    def prev(c):
        return pl.BlockSpec((HALO, BRANCH), lambda i, c=c: (jnp.maximum(i * hb - 1, 0), c0 + c))

    return pl.pallas_call(
        body, name=name, grid=(t // tr,),
        out_shape=jax.ShapeDtypeStruct((t, BRANCH), BF16),
        in_specs=[col(0), col(1), col(2), prev(1), prev(2), pl.BlockSpec((3, BRANCH), lambda i: (0, 0))],
        out_specs=pl.BlockSpec((tr, BRANCH), lambda i: (i, 0)),
        compiler_params=_params(("parallel",), 6 * _nbytes((tr, BRANCH), F32)),
    )(proj, proj, proj, proj, proj, conv_w)


def _conv_bwd(proj, conv_w, dout, name):
    t = proj.shape[0]
    tr = _pick(t, ROW_TILE, BF16_SUBLANE)
    nblk = t // tr
    c0 = COL_CONV // BRANCH
    hb = tr // HALO
    last_halo = t // HALO - 1

    def body(cb_ref, cc_ref, cu_ref, hc_ref, hu_ref, do_ref, ndo_ref, ncb_ref, w_ref,
             dcb_ref, dcc_ref, dcu_ref, dw_ref):
        i = pl.program_id(0)
        cb, cc, cu = cb_ref[...], cc_ref[...], cu_ref[...]
        w0, w1, w2 = w_ref[0:1, :], w_ref[1:2, :], w_ref[2:3, :]
        z = cc * cu
        hz = jnp.where(i > 0, hc_ref[...] * hu_ref[...], 0.0)
        zf = jnp.concatenate([hz, z], axis=0)
        z1 = pltpu.roll(zf, 1, 0)[HALO:]
        z2 = pltpu.roll(zf, 2, 0)[HALO:]
        y = w2 * z + w1 * z1 + w0 * z2
        dout_v = do_ref[...]
        dyc = dout_v * cb
        hdy = jnp.where(i < nblk - 1, ndo_ref[...] * ncb_ref[...], 0.0)
        dyf = jnp.concatenate([dyc, hdy], axis=0)
        dy1 = pltpu.roll(dyf, tr + HALO - 1, 0)[:tr]
        dy2 = pltpu.roll(dyf, tr + HALO - 2, 0)[:tr]
        dz = w2 * dyc + w1 * dy1 + w0 * dy2
        dcb_ref[...] = (dout_v * y).astype(BF16)
        dcc_ref[...] = (dz * cu).astype(BF16)
        dcu_ref[...] = (dz * cc).astype(BF16)

        @pl.when(i == 0)
        def _():
            dw_ref[...] = jnp.zeros_like(dw_ref)

        dw_ref[0:1, :] += jnp.sum(dyc * z2, axis=0, keepdims=True)
        dw_ref[1:2, :] += jnp.sum(dyc * z1, axis=0, keepdims=True)
        dw_ref[2:3, :] += jnp.sum(dyc * z, axis=0, keepdims=True)

    def col(c):
        return pl.BlockSpec((tr, BRANCH), lambda i, c=c: (i, c0 + c))

    def prev(c):
        return pl.BlockSpec((HALO, BRANCH), lambda i, c=c: (jnp.maximum(i * hb - 1, 0), c0 + c))

    def nxt(c):
        return pl.BlockSpec((HALO, BRANCH), lambda i, c=c: (jnp.minimum((i + 1) * hb, last_halo), c))

    row = pl.BlockSpec((tr, BRANCH), lambda i: (i, 0))
    return pl.pallas_call(
        body, name=name, grid=(nblk,),
        out_shape=(jax.ShapeDtypeStruct((t, BRANCH), BF16),) * 3 + (jax.ShapeDtypeStruct((HALO, BRANCH), F32),),
        in_specs=[col(0), col(1), col(2), prev(1), prev(2), row, nxt(0), nxt(c0),
                  pl.BlockSpec((3, BRANCH), lambda i: (0, 0))],
        out_specs=(row, row, row, pl.BlockSpec((HALO, BRANCH), lambda i: (0, 0))),
        compiler_params=_params(("arbitrary",), 8 * _nbytes((tr, BRANCH), F32)),
    )(proj, proj, proj, proj, proj, dout, dout, proj, conv_w)


def _tri(lower):
    r = lax.broadcasted_iota(jnp.int32, (LANE, LANE), 0)
    c = lax.broadcasted_iota(jnp.int32, (LANE, LANE), 1)
    return jnp.where((c <= r) if lower else (c >= r), 1.0, 0.0).astype(F32)


def _logf_cumsum(proj, fbias_row, name):
    t = proj.shape[0]
    nchunk = t // LANE

    def body(f_ref, b_ref, c_ref, run_sc):
        tri = _tri(True)
        run_sc[...] = jnp.zeros_like(run_sc)

        @pl.loop(0, nchunk)
        def _(i):
            rows = pl.ds(pl.multiple_of(i * LANE, LANE), LANE)
            z = f_ref[rows, :] + b_ref[...]
            logf = jnp.minimum(z, 0.0) - jnp.log(1.0 + jnp.exp(-jnp.abs(z)))
            cs = lax.dot_general(tri, logf, (NN, ((), ())), precision=lax.Precision.HIGHEST,
                                 preferred_element_type=F32) + run_sc[0:1, :]
            c_ref[rows, :] = cs
            run_sc[0:1, :] = cs[LANE - 1:LANE, :]

    return pl.pallas_call(
        body, name=name, grid=(1,),
        out_shape=jax.ShapeDtypeStruct((t, LANE), F32),
        in_specs=[pl.BlockSpec((t, LANE), lambda i: (0, COL_FG // LANE)), pl.BlockSpec((1, LANE), lambda i: (0, 0))],
        out_specs=pl.BlockSpec((t, LANE), lambda i: (0, 0)),
        scratch_shapes=[pltpu.VMEM((8, LANE), F32)],
        compiler_params=_params(("arbitrary",), 2 * _nbytes((t, LANE), F32)),
    )(proj, fbias_row)


def _logf_cumsum_bwd(proj, fbias_row, pieces, name):
    t = proj.shape[0]
    tb = _pick(t, 2 * ROW_TILE, LANE)
    nblk = t // tb
    npiece = len(pieces)

    def body(*refs):
        f_ref, b_ref = refs[:2]
        piece_refs = refs[2:2 + npiece]
        df_ref, db_ref, run_sc = refs[2 + npiece:]
        i = pl.program_id(0)
        tri = _tri(False)

        @pl.when(i == 0)
        def _():
            run_sc[...] = jnp.zeros_like(run_sc)
            db_ref[...] = jnp.zeros_like(db_ref)

        for c in reversed(range(tb // LANE)):
            rows = slice(c * LANE, (c + 1) * LANE)
            slabs = [p_ref[n, rows, :] for p_ref in piece_refs for n in range(p_ref.shape[0])]
            dcc = slabs[0]
            for slab in slabs[1:]:
                dcc = dcc + slab
            ss = lax.dot_general(tri, dcc, (NN, ((), ())), precision=lax.Precision.HIGHEST,
                                 preferred_element_type=F32) + run_sc[0:1, :]
            z = f_ref[rows, :] + b_ref[...]
            dz = ss * (1.0 / (1.0 + jnp.exp(z)))
            df_ref[rows, :] = dz.astype(BF16)
            run_sc[0:1, :] = ss[0:1, :]
            db_ref[...] += jnp.sum(dz, axis=0, keepdims=True)

    piece_specs = [pl.BlockSpec((p.shape[0], tb, LANE), lambda i: (0, nblk - 1 - i, 0)) for p in pieces]
    nslab = sum(p.shape[0] for p in pieces)
    return pl.pallas_call(
        body, name=name, grid=(nblk,),
        out_shape=(jax.ShapeDtypeStruct((t, LANE), BF16), jax.ShapeDtypeStruct((1, LANE), F32)),
        in_specs=[pl.BlockSpec((tb, LANE), lambda i: (nblk - 1 - i, COL_FG // LANE)),
                  pl.BlockSpec((1, LANE), lambda i: (0, 0))] + piece_specs,
        out_specs=(pl.BlockSpec((tb, LANE), lambda i: (nblk - 1 - i, 0)), pl.BlockSpec((1, LANE), lambda i: (0, 0))),
        scratch_shapes=[pltpu.VMEM((8, LANE), F32)],
        compiler_params=_params(("arbitrary",), (4 + nslab) * _nbytes((tb, LANE), F32)),
    )(proj, fbias_row, *pieces)


def _lo_mask():
    return lax.broadcasted_iota(jnp.int32, (1, LANE), 1) < HEAD


def _causal_steps(n, key_major):
    if key_major:
        pairs = [(iq, ik) for ik in range(n) for iq in range(ik, n)]
    else:
        pairs = [(iq, ik) for iq in range(n) for ik in range(iq + 1)]
    return (jnp.asarray([p[0] for p in pairs], jnp.int32), jnp.asarray([p[1] for p in pairs], jnp.int32))


def _head_lanes(j, pair_vals):
    lane = lax.broadcasted_iota(jnp.int32, (1, LANE), 1)
    return jnp.where(lane == 2 * j, pair_vals[0], 0.0) + jnp.where(lane == 2 * j + 1, pair_vals[1], 0.0)


def _fox_fwd(proj, c_col, c_row, name):
    t = proj.shape[0]
    tq = _pick(t, FOX_TILE, LANE)
    nq = t // tq
    rep = tq // LANE
    scale = HEAD ** -0.5
    cq, ck, cv = COL_FOX // LANE, COL_FOX // LANE + 4, COL_FOX // LANE + 8
    q_tab, k_tab = _causal_steps(nq, False)

    def body(qt_ref, kt_ref, q_ref, k_ref, v_ref, ck_ref, cqr_ref, y_ref, lse_ref, m_sc, l_sc, acc_sc):
        step_id = pl.program_id(1)
        iq, ik = qt_ref[step_id], kt_ref[step_id]
        lo = _lo_mask()
        lo_rows = lax.broadcasted_iota(jnp.int32, (LANE, 1), 0) < HEAD

        @pl.when(ik == 0)
        def _():
            m_sc[...] = jnp.full(m_sc.shape, NEG, F32)
            l_sc[...] = jnp.zeros_like(l_sc)
            acc_sc[...] = jnp.zeros_like(acc_sc)

        def step(diag):
            q2 = (q_ref[...] * scale).astype(BF16)
            k2 = k_ref[...].astype(BF16)
            v2 = v_ref[...].astype(BF16)
            alphas, adds = [], []
            for h in range(2):
                msk = lo if h == 0 else jnp.logical_not(lo)
                kh = jnp.where(msk, k2, jnp.zeros_like(k2))
                vh = jnp.where(msk, v2, jnp.zeros_like(v2))
                st = _dot(kh, q2, NT) + cqr_ref[h] - jnp.tile(ck_ref[h], (1, rep))
                if diag:
                    krow = lax.broadcasted_iota(jnp.int32, (tq, tq), 0)
                    qcol = lax.broadcasted_iota(jnp.int32, (tq, tq), 1)
                    st = jnp.where(krow <= qcol, st, NEG)
                m_prev = m_sc[h]
                m_new = jnp.maximum(m_prev, jnp.max(st, axis=0, keepdims=True))
                alpha = jnp.exp(m_prev - m_new)
                pt = jnp.exp(st - m_new)
                l_sc[h] = alpha * l_sc[h] + jnp.sum(pt, axis=0, keepdims=True)
                m_sc[h] = m_new
                alphas.append(alpha)
                adds.append(_dot(vh, pt.astype(BF16), TN))
            acc_sc[...] = acc_sc[...] * jnp.where(lo_rows, alphas[0], alphas[1]) + (adds[0] + adds[1])

        @pl.when(ik < iq)
        def _():
            step(False)

        @pl.when(ik == iq)
        def _():
            step(True)
            yt = acc_sc[...] / jnp.where(lo_rows, l_sc[0], l_sc[1])
            y_ref[...] = yt.T.astype(BF16)
            lse_ref[...] = m_sc[...] + jnp.log(l_sc[...])

    def kv(c):
        return pl.BlockSpec((tq, LANE), lambda j, s, qt, kt, c=c: (kt[s], c + j))

    qrow = pl.BlockSpec((2, 1, tq), lambda j, s, qt, kt: (j, 0, qt[s]))
    grid_spec = pltpu.PrefetchScalarGridSpec(
        num_scalar_prefetch=2, grid=(4, int(q_tab.shape[0])),
        in_specs=[pl.BlockSpec((tq, LANE), lambda j, s, qt, kt: (qt[s], cq + j)), kv(ck), kv(cv),
                  pl.BlockSpec((2, tq, LANE), lambda j, s, qt, kt: (j, kt[s], 0)), qrow],
        out_specs=(pl.BlockSpec((tq, LANE), lambda j, s, qt, kt: (qt[s], j)), qrow),
        scratch_shapes=[pltpu.VMEM((2, 1, tq), F32), pltpu.VMEM((2, 1, tq), F32), pltpu.VMEM((LANE, tq), F32)])
    return pl.pallas_call(
        body, name=name, grid_spec=grid_spec,
        out_shape=(jax.ShapeDtypeStruct((t, BRANCH), BF16), jax.ShapeDtypeStruct((8, 1, t), F32)),
        compiler_params=_params(("parallel", "arbitrary"),
                                16 * _nbytes((tq, LANE), F32) + 6 * _nbytes((tq, tq), F32)),
    )(q_tab, k_tab, proj, proj, proj, c_col, c_row)


def _fox_bwd(proj, c_col, c_row, lse_row, y, dy, name):
    t = proj.shape[0]
    tb = _pick(t, FOX_TILE, LANE)
    nb = t // tb
    rep = tb // LANE
    scale = HEAD ** -0.5
    cq, ck, cv = COL_FOX // LANE, COL_FOX // LANE + 4, COL_FOX // LANE + 8
    q_tab, k_tab = _causal_steps(nb, True)
    nsteps = int(q_tab.shape[0])

    def body(qt_ref, kt_ref, k_ref, v_ref, q_ref, y_ref, dy_ref, ck_ref, cqr_ref, lser_ref,
             dq_ref, dk_ref, dv_ref, dck_ref, dcq_ref, dk_sc, dv_sc, dc_sc, dqt_sc, dcq_sc, d_sc):
        j, step_id = pl.program_id(0), pl.program_id(1)
        iq, ik = qt_ref[step_id], kt_ref[step_id]
        lo = _lo_mask()

        @pl.when(step_id == 0)
        def _():
            dqt_sc[...] = jnp.zeros_like(dqt_sc)
            dcq_sc[...] = jnp.zeros_like(dcq_sc)

        @pl.when(iq == ik)
        def _():
            dk_sc[...] = jnp.zeros_like(dk_sc)
            dv_sc[...] = jnp.zeros_like(dv_sc)
            dc_sc[...] = jnp.zeros_like(dc_sc)

        @pl.when(ik == 0)
        def _():
            prod = y_ref[...].astype(F32) * dy_ref[...].astype(F32)
            row = lax.broadcasted_iota(jnp.int32, (8, LANE), 0)
            sel = jnp.logical_or(jnp.logical_and(row == 0, lo), jnp.logical_and(row == 1, jnp.logical_not(lo)))
            d_sc[iq] = lax.dot_general(jnp.where(sel, 1.0, 0.0).astype(F32), prod, (NT, ((), ())),
                                       precision=lax.Precision.HIGHEST, preferred_element_type=F32)

        def step(diag):
            k2 = k_ref[...].astype(BF16)
            v2 = v_ref[...].astype(BF16)
            q2 = (q_ref[...] * scale).astype(BF16)
            do2 = dy_ref[...]
            d_rows = d_sc[iq]
            for h in range(2):
                msk = lo if h == 0 else jnp.logical_not(lo)
                kh = jnp.where(msk, k2, jnp.zeros_like(k2))
                vh = jnp.where(msk, v2, jnp.zeros_like(v2))
                st = _dot(kh, q2, NT) + (cqr_ref[h] - lser_ref[h]) - jnp.tile(ck_ref[h], (1, rep))
                if diag:
                    krow = lax.broadcasted_iota(jnp.int32, (tb, tb), 0)
                    qcol = lax.broadcasted_iota(jnp.int32, (tb, tb), 1)
                    st = jnp.where(krow <= qcol, st, NEG)
                pt = jnp.exp(st)
                dpt = _dot(vh, do2, NT)
                dst = pt * (dpt - d_rows[h:h + 1, :])
                dsb = dst.astype(BF16)
                dv_sc[h] += _dot(pt.astype(BF16), do2, NN)
                dk_sc[h] += _dot(dsb, q2, NN)
                dc_sc[h] -= jnp.sum(dst, axis=1, keepdims=True)
                dqt_sc[iq] += _dot(kh, dsb, TN)
                dcq_sc[h, iq] += jnp.sum(dst, axis=0, keepdims=True)

        @pl.when(iq > ik)
        def _():
            step(False)

        @pl.when(iq == ik)
        def _():
            step(True)

        @pl.when(iq == nb - 1)
        def _():
            dk_ref[...] = jnp.where(lo, dk_sc[0], dk_sc[1]).astype(BF16)
            dv_ref[...] = jnp.where(lo, dv_sc[0], dv_sc[1]).astype(BF16)
            dck_ref[...] = _head_lanes(j, dc_sc)

        @pl.when(step_id == nsteps - 1)
        def _():
            for i in range(nb):
                dq_ref[i * tb:(i + 1) * tb, :] = (dqt_sc[i].T * scale).astype(BF16)
                for h in range(2):
                    dcq_ref[h, :, i * tb:(i + 1) * tb] = dcq_sc[h, i]

    def kcol(c):
        return pl.BlockSpec((tb, LANE), lambda j, s, qt, kt, c=c: (kt[s], c + j))

    qrow = pl.BlockSpec((2, 1, tb), lambda j, s, qt, kt: (j, 0, qt[s]))
    pair_q = pl.BlockSpec((tb, LANE), lambda j, s, qt, kt: (qt[s], j))
    pair_k = pl.BlockSpec((tb, LANE), lambda j, s, qt, kt: (kt[s], j))
    grid_spec = pltpu.PrefetchScalarGridSpec(
        num_scalar_prefetch=2, grid=(4, nsteps),
        in_specs=[kcol(ck), kcol(cv), pl.BlockSpec((tb, LANE), lambda j, s, qt, kt: (qt[s], cq + j)), pair_q, pair_q,
                  pl.BlockSpec((2, tb, LANE), lambda j, s, qt, kt: (j, kt[s], 0)), qrow, qrow],
        out_specs=(pl.BlockSpec((t, LANE), lambda j, s, qt, kt: (0, j)), pair_k, pair_k,
                   pl.BlockSpec((None, tb, LANE), lambda j, s, qt, kt: (j, kt[s], 0)),
                   pl.BlockSpec((2, 1, t), lambda j, s, qt, kt: (j, 0, 0))),
        scratch_shapes=[pltpu.VMEM((2, tb, LANE), F32)] * 3
        + [pltpu.VMEM((nb, LANE, tb), F32), pltpu.VMEM((2, nb, 1, tb), F32), pltpu.VMEM((nb, 8, tb), F32)])
    return pl.pallas_call(
        body, name=name, grid_spec=grid_spec,
        out_shape=(jax.ShapeDtypeStruct((t, BRANCH), BF16), jax.ShapeDtypeStruct((t, BRANCH), BF16),
                   jax.ShapeDtypeStruct((t, BRANCH), BF16), jax.ShapeDtypeStruct((4, t, LANE), F32),
                   jax.ShapeDtypeStruct((8, 1, t), F32)),
        compiler_params=_params(("parallel", "arbitrary"),
                                24 * _nbytes((tb, LANE), F32) + 8 * _nbytes((tb, tb), F32)
                                + 2 * _nbytes((t, LANE), F32)),
    )(q_tab, k_tab, proj, proj, proj, y, dy, c_col, c_row, lse_row)


def _swa_tables(rel_bias):
    tq = np.arange(SWA_BLOCK)[:, None]
    sk = np.arange(2 * SWA_BLOCK)[None, :]
    dist = SWA_BLOCK + tq - sk
    inwin = (dist >= 0) & (dist < SWA_BLOCK)
    n = np.maximum(dist, 0)
    max_exact = N_BUCKETS // 2
    large = max_exact + (np.log(np.maximum(n, 1).astype(np.float32) / max_exact)
                         / math.log(SWA_BLOCK / max_exact) * (N_BUCKETS - max_exact)).astype(np.int32)
    bucket = np.where(n < max_exact, n, np.minimum(large, N_BUCKETS - 1))
    onehot = (bucket[..., None] == np.arange(N_BUCKETS)) & inwin[..., None]
    onehot = jnp.asarray(onehot.astype(np.float32))
    bias = jnp.einsum("tsb,bh->hts", onehot, rel_bias, precision=lax.Precision.HIGHEST)
    bias = jnp.where(jnp.asarray(inwin)[None], bias, NEG)
    return onehot, bias


SWA_BLOCKS_PER_STEP = 8


def _swa_specs(nb, r_blk, csk, csv):
    def prev(c):
        return pl.BlockSpec((SWA_BLOCK, LANE), lambda kvh, n, c=c: (jnp.maximum(n * r_blk - 1, 0), c))

    def cur(c):
        return pl.BlockSpec((r_blk * SWA_BLOCK, LANE), lambda kvh, n, c=c: (n, c))

    return [prev(csk), cur(csk), prev(csv), cur(csv)]


def _swa_fwd(proj, bias, sink_rep, name):
    t = proj.shape[0]
    nb = t // SWA_BLOCK
    r_blk = _pick(nb, SWA_BLOCKS_PER_STEP, 1)
    rows = r_blk * SWA_BLOCK
    scale = HEAD ** -0.5
    csq, csk, csv = COL_SQ // 256, COL_SK // LANE, COL_SV // LANE

    def body(q_ref, kp_ref, kc_ref, vp_ref, vc_ref, b_ref, sk_ref, y_ref, lse_ref):
        kvh, n = pl.program_id(0), pl.program_id(1)
        lane = lax.broadcasted_iota(jnp.int32, (1, LANE), 1)
        lo = lane < HEAD
        kvm = jnp.logical_and(lane >= kvh * HEAD, lane < (kvh + 1) * HEAD)

        def both(prev_ref, cur_ref):
            band = jnp.concatenate([prev_ref[...], cur_ref[...]], axis=0)
            band = jnp.where(kvm, band, 0.0)
            return (band + pltpu.roll(band, HEAD, 1)).astype(BF16)

        kb_all, vb_all = both(kp_ref, kc_ref), both(vp_ref, vc_ref)
        col = lax.broadcasted_iota(jnp.int32, (SWA_BLOCK, 2 * SWA_BLOCK), 1)
        first = jnp.logical_and(n == 0, col < SWA_BLOCK)
        for r in range(r_blk):
            rs = slice(r * SWA_BLOCK, (r + 1) * SWA_BLOCK)
            kb = kb_all[r * SWA_BLOCK:(r + 2) * SWA_BLOCK]
            vb = vb_all[r * SWA_BLOCK:(r + 2) * SWA_BLOCK]
            outs = []
            for g in range(SWA_GROUP):
                half = q_ref[rs, (g // 2) * LANE:(g // 2 + 1) * LANE]
                hm = lo if g % 2 == 0 else jnp.logical_not(lo)
                qg = jnp.where(hm, half, 0.0).astype(BF16)
                s = _dot(qg, kb, NT) * scale + b_ref[g]
                if r == 0:
                    s = jnp.where(first, NEG, s)
                snk = sk_ref[g:g + 1, :]
                m = jnp.maximum(jnp.max(s, axis=1, keepdims=True), snk)
                p = jnp.exp(s - jnp.tile(m, (1, 2)))
                denom = jnp.sum(p, axis=1, keepdims=True) + jnp.exp(snk - m)
                outs.append(_dot(p.astype(BF16), vb, NN) / denom)
                lse_ref[g, rs, :] = m + jnp.log(denom)
            y_ref[rs, 0:LANE] = jnp.where(lo, outs[0], outs[1]).astype(BF16)
            y_ref[rs, LANE:2 * LANE] = jnp.where(lo, outs[2], outs[3]).astype(BF16)

    return pl.pallas_call(
        body, name=name, grid=(2, nb // r_blk),
        out_shape=(jax.ShapeDtypeStruct((t, BRANCH), BF16), jax.ShapeDtypeStruct((8, t, LANE), F32)),
        in_specs=[pl.BlockSpec((rows, 256), lambda kvh, n: (n, csq + kvh))] + _swa_specs(nb, r_blk, csk, csv)
        + [pl.BlockSpec((None, SWA_GROUP, SWA_BLOCK, 256), lambda kvh, n: (kvh, 0, 0, 0)),
           pl.BlockSpec((None, SWA_GROUP, LANE), lambda kvh, n: (kvh, 0, 0))],
        out_specs=(pl.BlockSpec((rows, 256), lambda kvh, n: (n, kvh)),
                   pl.BlockSpec((SWA_GROUP, rows, LANE), lambda kvh, n: (kvh, n, 0))),
        compiler_params=_params(("parallel", "arbitrary"), 8 << 20),
    )(proj, proj, proj, proj, proj, bias.reshape(2, SWA_GROUP, SWA_BLOCK, 256), sink_rep)


def _swa_bwd(proj, bias, sink_rep, lse, y, dy, name):
    t = proj.shape[0]
    nb = t // SWA_BLOCK
    r_blk = _pick(nb, SWA_BLOCKS_PER_STEP, 1)
    rows = r_blk * SWA_BLOCK
    scale = HEAD ** -0.5
    csq, csk, csv = COL_SQ // 256, COL_SK // LANE, COL_SV // LANE

    def body(q_ref, kp_ref, kc_ref, vp_ref, vc_ref, b_ref, sk_ref, lse_ref, y_ref, dy_ref,
             dq_ref, dkp_ref, dvp_ref, db_ref, dsk_ref):
        kvh, n = pl.program_id(0), pl.program_id(1)
        lane = lax.broadcasted_iota(jnp.int32, (1, LANE), 1)
        lo = lane < HEAD
        kvm = jnp.logical_and(lane >= kvh * HEAD, lane < (kvh + 1) * HEAD)

        def both(prev_ref, cur_ref):
            band = jnp.concatenate([prev_ref[...], cur_ref[...]], axis=0)
            band = jnp.where(kvm, band, 0.0)
            return (band + pltpu.roll(band, HEAD, 1)).astype(BF16)

        kb_all, vb_all = both(kp_ref, kc_ref), both(vp_ref, vc_ref)
        col = lax.broadcasted_iota(jnp.int32, (SWA_BLOCK, 2 * SWA_BLOCK), 1)
        first = jnp.logical_and(n == 0, col < SWA_BLOCK)

        @pl.when(n == 0)
        def _():
            db_ref[...] = jnp.zeros_like(db_ref)
            dsk_ref[...] = jnp.zeros_like(dsk_ref)

        for r in range(r_blk):
            rs = slice(r * SWA_BLOCK, (r + 1) * SWA_BLOCK)
            kb = kb_all[r * SWA_BLOCK:(r + 2) * SWA_BLOCK]
            vb = vb_all[r * SWA_BLOCK:(r + 2) * SWA_BLOCK]
            dk_full = jnp.zeros((2 * SWA_BLOCK, LANE), F32)
            dv_full = jnp.zeros((2 * SWA_BLOCK, LANE), F32)
            dqs = []
            for g in range(SWA_GROUP):
                sl = slice((g // 2) * LANE, (g // 2 + 1) * LANE)
                hm = lo if g % 2 == 0 else jnp.logical_not(lo)
                qg = jnp.where(hm, q_ref[rs, sl], 0.0).astype(BF16)
                dog = jnp.where(hm, dy_ref[rs, sl], jnp.zeros((SWA_BLOCK, LANE), BF16))
                dmat = jnp.where(hm, y_ref[rs, sl].astype(F32) * dy_ref[rs, sl].astype(F32), 0.0)
                dg = jnp.sum(dmat, axis=1, keepdims=True)
                s = _dot(qg, kb, NT) * scale + b_ref[g]
                if r == 0:
                    s = jnp.where(first, NEG, s)
                lse_g = lse_ref[g, rs, :]
                p = jnp.exp(s - jnp.tile(lse_g, (1, 2)))
                dp = _dot(dog, vb, NT)
                ds = p * (dp - dg)
                dsb = ds.astype(BF16)
                dqs.append(_dot(dsb, kb, NN) * scale)
                dk_full = dk_full + _dot(dsb, qg, TN)
                dv_full = dv_full + _dot(p.astype(BF16), dog, TN)
                db_ref[g] += ds
                psink = jnp.exp(sk_ref[g:g + 1, :] - lse_g)
                dsk_ref[g:g + 1, :] -= jnp.sum(psink * dg, axis=0, keepdims=True)
            dq_ref[rs, 0:LANE] = jnp.where(lo, dqs[0], dqs[1]).astype(BF16)
            dq_ref[rs, LANE:2 * LANE] = jnp.where(lo, dqs[2], dqs[3]).astype(BF16)
            dkp_ref[r] = jnp.where(kvm, (dk_full + pltpu.roll(dk_full, HEAD, 1)) * scale, 0.0)
            dvp_ref[r] = jnp.where(kvm, dv_full + pltpu.roll(dv_full, HEAD, 1), 0.0)

    qblk = pl.BlockSpec((rows, 256), lambda kvh, n: (n, kvh))
    part = pl.BlockSpec((None, r_blk, 2 * SWA_BLOCK, LANE), lambda kvh, n: (kvh, n, 0, 0))
    bspec = pl.BlockSpec((None, SWA_GROUP, SWA_BLOCK, 256), lambda kvh, n: (kvh, 0, 0, 0))
    sspec = pl.BlockSpec((None, SWA_GROUP, LANE), lambda kvh, n: (kvh, 0, 0))
    return pl.pallas_call(
        body, name=name, grid=(2, nb // r_blk),
        out_shape=(jax.ShapeDtypeStruct((t, BRANCH), BF16),
                   jax.ShapeDtypeStruct((2, nb, 2 * SWA_BLOCK, LANE), F32),
                   jax.ShapeDtypeStruct((2, nb, 2 * SWA_BLOCK, LANE), F32),
                   jax.ShapeDtypeStruct((2, SWA_GROUP, SWA_BLOCK, 256), F32),
                   jax.ShapeDtypeStruct((2, SWA_GROUP, LANE), F32)),
        in_specs=[pl.BlockSpec((rows, 256), lambda kvh, n: (n, csq + kvh))] + _swa_specs(nb, r_blk, csk, csv)
        + [bspec, sspec, pl.BlockSpec((SWA_GROUP, rows, LANE), lambda kvh, n: (kvh, n, 0)), qblk, qblk],
        out_specs=(qblk, part, part, bspec, sspec),
        compiler_params=_params(("parallel", "arbitrary"), 12 << 20),
    )(proj, proj, proj, proj, proj, bias.reshape(2, SWA_GROUP, SWA_BLOCK, 256), sink_rep, lse, y, dy)


def _gate_fwd(proj, pb, name):
    t = proj.shape[0]
    tr = _pick(t, ROW_TILE // 2, BF16_SUBLANE)

    def body(g0, g1, g2, p0, p1, p2, o_ref):
        acc = jax.nn.sigmoid(g0[...]) * p0[...]
        acc = acc + jax.nn.sigmoid(g1[...]) * p1[...]
        acc = acc + jax.nn.sigmoid(g2[...]) * p2[...]
        o_ref[...] = acc.astype(BF16)

    row = pl.BlockSpec((tr, D_MODEL), lambda i: (i, 0))
    gates = [pl.BlockSpec((tr, D_MODEL), lambda i, b=b: (i, b)) for b in range(3)]
    return pl.pallas_call(
        body, name=name, grid=(t // tr,),
        out_shape=jax.ShapeDtypeStruct((t, D_MODEL), BF16),
        in_specs=gates + [row] * 3, out_specs=row,
        compiler_params=_params(("parallel",), 7 * _nbytes((tr, D_MODEL), F32)),
    )(proj, proj, proj, *pb)


def _gate_bwd(proj, pb, dmerged, name):
    t = proj.shape[0]
    tr = _pick(t, ROW_TILE // 2, BF16_SUBLANE)

    def body(g0, g1, g2, p0, p1, p2, dm_ref, dp0, dp1, dp2, dg_ref):
        dm = dm_ref[...]
        for b, (g_ref, p_ref, dp_ref) in enumerate(((g0, p0, dp0), (g1, p1, dp1), (g2, p2, dp2))):
            sg = jax.nn.sigmoid(g_ref[...])
            dp_ref[...] = (dm * sg).astype(BF16)
            dg_ref[:, b * D_MODEL:(b + 1) * D_MODEL] = (dm * p_ref[...] * sg * (1.0 - sg)).astype(BF16)

    row = pl.BlockSpec((tr, D_MODEL), lambda i: (i, 0))
    gates = [pl.BlockSpec((tr, D_MODEL), lambda i, b=b: (i, b)) for b in range(3)]
    return pl.pallas_call(
        body, name=name, grid=(t // tr,),
        out_shape=(jax.ShapeDtypeStruct((t, D_MODEL), BF16),) * 3 + (jax.ShapeDtypeStruct((t, 3 * D_MODEL), BF16),),
        in_specs=gates + [row] * 4,
        out_specs=(row, row, row, pl.BlockSpec((tr, 3 * D_MODEL), lambda i: (i, 0))),
        compiler_params=_params(("parallel",), 11 * _nbytes((tr, D_MODEL), F32)),
    )(proj, proj, proj, *pb, dmerged)


def _swiglu_fwd(ab, name):
    t = ab.shape[0]
    tr = _pick(t, ROW_TILE, BF16_SUBLANE)
    tc = D_FF_P // 2

    def body(a_ref, b_ref, o_ref):
        a = a_ref[...]
        o_ref[...] = (a * jax.nn.sigmoid(a) * b_ref[...]).astype(BF16)

    return pl.pallas_call(
        body, name=name, grid=(t // tr, 2),
        out_shape=jax.ShapeDtypeStruct((t, D_FF_P), BF16),
        in_specs=[pl.BlockSpec((tr, tc), lambda i, j: (i, j)), pl.BlockSpec((tr, tc), lambda i, j: (i, j + 2))],
        out_specs=pl.BlockSpec((tr, tc), lambda i, j: (i, j)),
        compiler_params=_params(("parallel", "parallel"), 3 * _nbytes((tr, tc), F32)),
    )(ab, ab)


def _swiglu_bwd(ab, dh, name):
    t = ab.shape[0]
    tr = _pick(t, ROW_TILE, BF16_SUBLANE)
    tc = D_FF_P // 2

    def body(a_ref, b_ref, dh_ref, da_ref, db_ref):
        a, b, d = a_ref[...], b_ref[...], dh_ref[...]
        sg = jax.nn.sigmoid(a)
        da_ref[...] = (d * b * (sg * (1.0 + a * (1.0 - sg)))).astype(BF16)
        db_ref[...] = (d * (a * sg)).astype(BF16)

    blk = pl.BlockSpec((tr, tc), lambda i, j: (i, j))
    return pl.pallas_call(
        body, name=name, grid=(t // tr, 2),
        out_shape=(jax.ShapeDtypeStruct((t, D_FF_P), BF16),) * 2,
        in_specs=[blk, pl.BlockSpec((tr, tc), lambda i, j: (i, j + 2)), blk],
        out_specs=(blk, blk),
        compiler_params=_params(("parallel", "parallel"), 5 * _nbytes((tr, tc), F32)),
    )(ab, ab, dh)


def _xattn_fwd(q, kv, name):
    t = q.shape[0]
    tq = _pick(t, ROW_TILE, BF16_SUBLANE)
    mlen = kv.shape[0]
    scale = X_HEAD ** -0.5

    def body(q_ref, kv_ref, o_ref):
        for h in range(X_HEADS):
            sl = slice(h * X_HEAD, (h + 1) * X_HEAD)
            kh = kv_ref[:, sl]
            vh = kv_ref[:, D_MODEL + h * X_HEAD:D_MODEL + (h + 1) * X_HEAD]
            s = _dot(q_ref[:, sl], kh, NT) * scale
            p = jnp.exp(s - jnp.max(s, axis=1, keepdims=True))
            l = jnp.sum(p, axis=1, keepdims=True)
            o_ref[:, sl] = (_dot(p.astype(BF16), vh, NN) / l).astype(BF16)

    return pl.pallas_call(
        body, name=name, grid=(t // tq,),
        out_shape=jax.ShapeDtypeStruct((t, D_MODEL), BF16),
        in_specs=[pl.BlockSpec((tq, D_MODEL), lambda i: (i, 0)), pl.BlockSpec((mlen, 2 * D_MODEL), lambda i: (0, 0))],
        out_specs=pl.BlockSpec((tq, D_MODEL), lambda i: (i, 0)),
        compiler_params=_params(("parallel",), 4 * _nbytes((tq, D_MODEL), F32)),
    )(q, kv)


def _xattn_bwd(q, kv, do, name):
    t = q.shape[0]
    tq = _pick(t, ROW_TILE, BF16_SUBLANE)
    mlen = kv.shape[0]
    scale = X_HEAD ** -0.5

    def body(q_ref, kv_ref, do_ref, dq_ref, dkv_ref):
        i = pl.program_id(0)

        @pl.when(i == 0)
        def _():
            dkv_ref[...] = jnp.zeros_like(dkv_ref)

        for h in range(X_HEADS):
            sl = slice(h * X_HEAD, (h + 1) * X_HEAD)
            vsl = slice(D_MODEL + h * X_HEAD, D_MODEL + (h + 1) * X_HEAD)
            qh, kh, vh, doh = q_ref[:, sl], kv_ref[:, sl], kv_ref[:, vsl], do_ref[:, sl]
            s = _dot(qh, kh, NT) * scale
            p = jnp.exp(s - jnp.max(s, axis=1, keepdims=True))
            p = p / jnp.sum(p, axis=1, keepdims=True)
            dp = _dot(doh, vh, NT)
            ds = p * (dp - jnp.sum(p * dp, axis=1, keepdims=True))
            dsb = ds.astype(BF16)
            dq_ref[:, sl] = (_dot(dsb, kh, NN) * scale).astype(BF16)
            dkv_ref[:, sl] += _dot(dsb, qh, TN) * scale
            dkv_ref[:, vsl] += _dot(p.astype(BF16), doh, TN)

    row = pl.BlockSpec((tq, D_MODEL), lambda i: (i, 0))
    whole = pl.BlockSpec((mlen, 2 * D_MODEL), lambda i: (0, 0))
    return pl.pallas_call(
        body, name=name, grid=(t // tq,),
        out_shape=(jax.ShapeDtypeStruct((t, D_MODEL), BF16), jax.ShapeDtypeStruct((mlen, 2 * D_MODEL), F32)),
        in_specs=[row, whole, row], out_specs=(row, whole),
        compiler_params=_params(("arbitrary",), 6 * _nbytes((tq, D_MODEL), F32)),
    )(q, kv, do)


def _position():
    return lax.axis_index("x"), lax.axis_index("y"), lax.axis_index("c")


N_PEER = N_DEV - 1


def _all_gather(xs, name):
    n = len(xs)

    def body(*refs):
        x_refs, out_refs = refs[:n], refs[n:2 * n]
        send_sems, recv_sems, local_sems = refs[2 * n:]
        mx, my, mc = _position()
        me, sib = (mx, my, mc), (mx, my, 1 - mc)
        chips = [(1 - mx, my), (mx, 1 - my), (1 - mx, 1 - my)]

        def slot(i, p):
            return out_refs[i].at[4 * p[0] + 2 * p[1] + p[2]]

        def copy(i, k, block, to, src=None):
            return pltpu.make_async_remote_copy(
                src_ref=slot(i, block) if src is None else src, dst_ref=slot(i, block),
                send_sem=send_sems.at[i * N_PEER + k], recv_sem=recv_sems.at[i * N_PEER + k],
                device_id=to, device_id_type=MESH)

        mine = [pltpu.make_async_copy(x_refs[i], slot(i, me), local_sems.at[i]) for i in range(n)]
        for cp in mine:
            cp.start()
        first = [copy(i, 1 + j, me, (*chip, mc), src=x_refs[i]) for j, chip in enumerate(chips) for i in range(n)]
        first += [copy(i, 0, me, sib, src=x_refs[i]) for i in range(n)]
        for cp in first:
            cp.start()
        passed = []
        for j, chip in enumerate(chips):
            for i in range(n):
                copy(i, 1 + j, (*chip, mc), me).wait_recv()
                passed.append(copy(i, 4 + j, (*chip, mc), sib))
                passed[-1].start()
        for i in range(n):
            copy(i, 0, sib, me).wait_recv()
        for j, chip in enumerate(chips):
            for i in range(n):
                copy(i, 4 + j, (*chip, 1 - mc), me).wait_recv()
        for cp in first + passed:
            cp.wait_send()
        for cp in mine:
            cp.wait()

    return pl.pallas_call(
        body, name=name,
        out_shape=tuple(jax.ShapeDtypeStruct((N_DEV,) + x.shape, x.dtype) for x in xs),
        in_specs=[pl.BlockSpec(memory_space=pl.ANY)] * n, out_specs=(pl.BlockSpec(memory_space=pl.ANY),) * n,
        scratch_shapes=[pltpu.SemaphoreType.DMA((n * N_PEER,)), pltpu.SemaphoreType.DMA((n * N_PEER,)),
                        pltpu.SemaphoreType.DMA((n,))],
    )(*xs)


PEER_RELS = [(dx, dy, dc) for dx in (0, 1) for dy in (0, 1) for dc in (0, 1)][1:]


def _peer_copy(rel_k, i, src_refs, land_refs, send_sems, recv_sems, layer, gather, arriving):
    mx, my, mc = _position()
    me_idx = 4 * mx + 2 * my + mc
    p = tuple((1 - v) if f else v for f, v in zip(PEER_RELS[rel_k], (mx, my, mc)))
    p_idx = 4 * p[0] + 2 * p[1] + p[2]
    src_slot, dst_slot = (me_idx, p_idx) if arriving else (p_idx, me_idx)
    src = src_refs[i] if gather else src_refs[i].at[src_slot]
    dst = land_refs[i].at[dst_slot] if layer is None else land_refs[i].at[dst_slot, layer]
    return pltpu.make_async_remote_copy(
        src_ref=src, dst_ref=dst, send_sem=send_sems.at[i * N_PEER + rel_k], recv_sem=recv_sems.at[i * N_PEER + rel_k],
        device_id=p, device_id_type=MESH)


HBM_SPEC = pl.BlockSpec(memory_space=pltpu.HBM)
SEM_SPEC = pl.BlockSpec(memory_space=pltpu.SEMAPHORE)
SIDE_EFFECT = pltpu.SideEffectType.DATAFLOW_SIDE_EFFECTING


def _own_slots(srcs, lands, layer, gather):
    mx, my, mc = _position()
    me_idx = 4 * mx + 2 * my + mc
    out = []
    for s, land in zip(srcs, lands):
        piece = s[None] if gather else lax.dynamic_index_in_dim(s, me_idx, 0, keepdims=True)
        if layer is None:
            start = (me_idx,) + (0,) * (land.ndim - 1)
        else:
            piece, start = piece[:, None], (me_idx, layer) + (0,) * (land.ndim - 2)
        out.append(lax.dynamic_update_slice(land, piece, start))
    return out


def _swap_start(srcs, lands, layer, gather, name):
    n = len(srcs)

    def body(*refs):
        src_refs, land_refs = refs[:n], refs[n:2 * n]
        send_sems, recv_sems = refs[2 * n], refs[2 * n + 1]
        token = refs[4 * n + 2]
        for i in range(n):
            for k in range(N_PEER):
                _peer_copy(k, i, src_refs, land_refs, send_sems, recv_sems, layer, gather, False).start()
        token[...] = jnp.zeros_like(token)

    hbm = [pltpu.with_memory_space_constraint(a, pltpu.HBM) for a in list(srcs) + list(lands)]
    out = pl.pallas_call(
        body, name=name,
        out_shape=(pltpu.SemaphoreType.DMA((n * N_PEER,)), pltpu.SemaphoreType.DMA((n * N_PEER,)))
        + tuple(pltpu.HBM(a.shape, a.dtype) for a in hbm) + (jax.ShapeDtypeStruct((8, LANE), F32),),
        in_specs=[HBM_SPEC] * (2 * n),
        out_specs=(SEM_SPEC, SEM_SPEC) + (HBM_SPEC,) * (2 * n) + (pl.BlockSpec(memory_space=pltpu.VMEM),),
        input_output_aliases={i: 2 + i for i in range(2 * n)},
        compiler_params=pltpu.CompilerParams(has_side_effects=SIDE_EFFECT),
    )(*hbm)
    return out[0], out[1], list(out[2:2 + n]), list(out[2 + n:2 + 2 * n]), out[2 + 2 * n]


def _swap_wait(send_sems, recv_sems, srcs, lands, after, layer, gather, name):
    n = len(srcs)

    def body(*refs):
        src_refs, land_refs = refs[:n], refs[n:2 * n]
        send_sems_ref, recv_sems_ref = refs[2 * n], refs[2 * n + 1]
        for i in range(n):
            for k in range(N_PEER):
                args = (src_refs, land_refs, send_sems_ref, recv_sems_ref, layer, gather)
                _peer_copy(k, i, *args, False).wait_send()
                _peer_copy(k, i, *args, True).wait_recv()

    out = pl.pallas_call(
        body, name=name,
        out_shape=tuple(pltpu.HBM(a.shape, a.dtype) for a in list(srcs) + list(lands)),
        in_specs=[HBM_SPEC] * (2 * n) + [SEM_SPEC, SEM_SPEC, pl.BlockSpec(memory_space=pl.ANY)],
        out_specs=(HBM_SPEC,) * (2 * n),
        input_output_aliases={i: i for i in range(2 * n)},
        compiler_params=pltpu.CompilerParams(has_side_effects=SIDE_EFFECT),
    )(*srcs, *lands, send_sems, recv_sems, after)
    return list(out[n:])


ADAMW_BLOCK_BYTES = 1 << 20


def _adamw(parts, w, m, v, name):
    r, l = w.shape
    tr = _pick(r, max(ADAMW_BLOCK_BYTES // (4 * l), BF16_SUBLANE), BF16_SUBLANE)
    c1 = 1.0 - ADAM_B1 ** ADAM_STEP
    c2 = 1.0 - ADAM_B2 ** ADAM_STEP

    def body(p_ref, w_ref, m_ref, v_ref, g_ref, d_ref, nm_ref, nv_ref):
        g = p_ref[0].astype(F32)
        for s in range(1, N_DEV):
            g = g + p_ref[s].astype(F32)
        nm = ADAM_B1 * m_ref[...] + (1.0 - ADAM_B1) * g
        nv = ADAM_B2 * v_ref[...] + (1.0 - ADAM_B2) * (g * g)
        m_hat = nm / c1
        v_hat = nv / c2
        g_ref[...] = g
        d_ref[...] = -ADAM_LR * (m_hat / (jnp.sqrt(v_hat) + ADAM_EPS) + ADAM_WD * w_ref[...])
        nm_ref[...] = nm
        nv_ref[...] = nv

    row = pl.BlockSpec((tr, l), lambda i: (i, 0))
    return pl.pallas_call(
        body, name=name, grid=(r // tr,),
        out_shape=(jax.ShapeDtypeStruct((r, l), F32),) * 4,
        in_specs=[pl.BlockSpec((N_DEV, tr, l), lambda i: (0, i, 0)), row, row, row],
        out_specs=(row,) * 4,
        compiler_params=_params(("parallel",), 12 * _nbytes((tr, l), F32)),
    )(parts, w, m, v)


MATRIX_WEIGHTS = (("w_in", 2), ("conv_w", 2), ("w_branch", 3), ("w_mix_out", 1), ("w_xq", 1), ("w_xkv", 2),
                  ("w_xo", 1), ("w_ffn_gate", 2), ("w_ffn_up", 2), ("w_ffn_down", 1))
SMALL_PARAMS = ("mix_norm_g", "xattn_norm_g", "mem_norm_g", "ffn_norm_g", "final_norm_g", "forget_bias", "sink",
                "rel_bias")


def _pack_small(pieces):
    flat = jnp.concatenate([p.astype(F32).reshape(-1) for p in pieces])
    total = -(-flat.shape[0] // (8 * LANE)) * (8 * LANE)
    return jnp.pad(flat, (0, total - flat.shape[0])).reshape(total // LANE, LANE)


def _rows(a):
    return a.reshape(-1, a.shape[-1])


def _to_full(gathered, axis):
    moved = jnp.moveaxis(gathered, 0, axis)
    shape = list(moved.shape)
    shape[axis:axis + 2] = [shape[axis] * shape[axis + 1]]
    return moved.reshape(shape)


def _to_blocks(full, axis):
    shape = list(full.shape)
    shape[axis:axis + 1] = [N_DEV, shape[axis] // N_DEV]
    return jnp.moveaxis(full.reshape(shape), axis, 0)


def _perm_in(w_in):
    pad = jnp.zeros((w_in.shape[0], PROJ_COLS - IN_COLS), w_in.dtype)
    return jnp.concatenate([w_in[:, 3848:6920], w_in[:, 0:3072], w_in[:, 3080:3848], w_in[:, 3072:3080], pad], axis=1)


def _unperm_in(dw):
    return jnp.concatenate([dw[:, 3072:6144], dw[:, 6912:6920], dw[:, 6144:6912], dw[:, 0:3072]], axis=1)


def _layer_fwd(l, x, mem, wt, sm, pre_branch=None):
    t = x.shape[0]
    tag = f"l{l}_"
    h = _rms_fwd(x, sm["mix_norm_g"][l], tag + "mix_norm")
    proj = _matmul(h, wt["w_in"][l], "nn", F32, tag + "in_proj")
    y_conv = _conv_fwd(proj, wt["conv_w"][l], tag + "conv")
    fbias_row = jnp.pad(sm["forget_bias"][l], (0, LANE - 8)).reshape(1, LANE)
    c = _logf_cumsum(proj, fbias_row, tag + "logf_cumsum")
    c8 = c[:, :8].T
    c_col = jnp.broadcast_to(c8[:, :, None], (8, t, LANE))
    c_row = c8.reshape(8, 1, t)
    y_fox, lse_fox = _fox_fwd(proj, c_col, c_row, tag + "fox")
    onehot, bias = sm["swa_tables"]
    sink_rep = jnp.broadcast_to(sm["sink"][l].reshape(2, SWA_GROUP, 1), (2, SWA_GROUP, LANE))
    y_swa, lse_swa = _swa_fwd(proj, bias, sink_rep, tag + "swa")
    ys = (y_conv, y_fox, y_swa)
    if pre_branch is not None:
        pre_branch(y_swa)
    pb = tuple(_matmul(ys[b], wt["w_branch"][l][b], "nn", F32, tag + f"branch{b}") for b in range(3))
    merged = _gate_fwd(proj, pb, tag + "gate")
    x1 = _matmul(merged, wt["w_mix_out"][l], "nn", F32, tag + "mix_out", residual=x)
    xn2 = _rms_fwd(x1, sm["xattn_norm_g"][l], tag + "xattn_norm")
    q = _matmul(xn2, wt["w_xq"][l], "nn", BF16, tag + "xq")
    mem_n = _rms_fwd(mem, sm["mem_norm_g"][l], tag + "mem_norm")
    kv = _matmul(mem_n, wt["w_xkv"][l], "nn", BF16, tag + "xkv")
    o = _xattn_fwd(q, kv, tag + "xattn")
    x2 = _matmul(o, wt["w_xo"][l], "nn", F32, tag + "xo", residual=x1)
    xn3 = _rms_fwd(x2, sm["ffn_norm_g"][l], tag + "ffn_norm")
    ab = _matmul(xn3, wt["w_gu"][l], "nn", F32, tag + "ffn_gu")
    h1 = _swiglu_fwd(ab, tag + "swiglu")
    x3 = _matmul(h1, wt["w_ffn_down"][l], "nn", F32, tag + "ffn_down", residual=x2)
    saved = dict(x=x, h=h, proj=proj, fbias_row=fbias_row, c_col=c_col, c_row=c_row, ys=ys, lse_fox=lse_fox,
                 onehot=onehot, bias=bias, sink_rep=sink_rep, lse_swa=lse_swa, pb=pb, merged=merged, x1=x1,
                 xn2=xn2, q=q, mem_n=mem_n, kv=kv, o=o, x2=x2, xn3=xn3, ab=ab, h1=h1)
    return x3, saved


def _layer_bwd(l, dx3, dx3_b, mem, wt, sm, sv, mid_hook=None, late_hook=None):
    t = dx3.shape[0]
    nb = t // SWA_BLOCK
    tag = f"l{l}_b_"
    gw, gs = {}, {}
    dh1 = _matmul(dx3_b, wt["w_ffn_down"][l], "nt", F32, tag + "d_h1")
    gw["w_ffn_down"] = _matmul(sv["h1"], dx3_b, "tn", F32, tag + "dw_down")
    da, db = _swiglu_bwd(sv["ab"], dh1, tag + "swiglu")
    dxn3 = _matmul(da, wt["w_ffn_gate"][l], "nt", F32, tag + "d_xn3_gate")
    dxn3 = _matmul(db, wt["w_ffn_up"][l], "nt", F32, tag + "d_xn3_up", residual=dxn3)
    gw["w_ffn_gate"] = _matmul(sv["xn3"], da, "tn", F32, tag + "dw_gate")
    gw["w_ffn_up"] = _matmul(sv["xn3"], db, "tn", F32, tag + "dw_up")
    dx2, dx2_b, gs["ffn_norm_g"] = _rms_bwd(sv["x2"], sm["ffn_norm_g"][l], dxn3, dx3, tag + "ffn_norm")
    do = _matmul(dx2_b, wt["w_xo"][l], "nt", BF16, tag + "d_o")
    gw["w_xo"] = _matmul(sv["o"], dx2_b, "tn", F32, tag + "dw_xo")
    dq, dkv = _xattn_bwd(sv["q"], sv["kv"], do, tag + "xattn")
    gw["w_xkv"] = _matmul(sv["mem_n"], dkv, "tn", F32, tag + "dw_xkv")
    dmem_n = _matmul(dkv, wt["w_xkv"][l], "nt", F32, tag + "d_memn")
    _, _, gs["mem_norm_g"] = _rms_bwd(mem, sm["mem_norm_g"][l], dmem_n, None, tag + "mem_norm")
    gw["w_xq"] = _matmul(sv["xn2"], dq, "tn", F32, tag + "dw_xq")
    dxn2 = _matmul(dq, wt["w_xq"][l], "nt", F32, tag + "d_xn2")
    dx1, dx1_b, gs["xattn_norm_g"] = _rms_bwd(sv["x1"], sm["xattn_norm_g"][l], dxn2, dx2, tag + "xattn_norm")
    dmerged = _matmul(dx1_b, wt["w_mix_out"][l], "nt", F32, tag + "d_merged")
    gw["w_mix_out"] = _matmul(sv["merged"], dx1_b, "tn", F32, tag + "dw_mix_out")
    dp0, dp1, dp2, dgate = _gate_bwd(sv["proj"], sv["pb"], dmerged, tag + "gate")
    dps = (dp0, dp1, dp2)
    dy_dtypes = (F32, BF16, BF16)
    dys = [_matmul(dps[b], wt["w_branch"][l][b], "nt", dy_dtypes[b], tag + f"d_y{b}") for b in range(3)]
    gw["w_branch"] = jnp.stack(
        [_matmul(sv["ys"][b], dps[b], "tn", F32, tag + f"dw_branch{b}") for b in range(3)])
    sink_rep = sv["sink_rep"] if mid_hook is None else sv["sink_rep"] + mid_hook(gw)
    dsq, dkp, dvp, dbias, dsink = _swa_bwd(sv["proj"], sv["bias"], sink_rep, sv["lse_swa"], sv["ys"][2],
                                           dys[2], tag + "swa")

    def band_add(part):
        tot = part[0] + part[1]
        cur = tot[:, SWA_BLOCK:, :]
        nxt = jnp.concatenate([tot[1:, :SWA_BLOCK, :], jnp.zeros((1, SWA_BLOCK, LANE), F32)], axis=0)
        return (cur + nxt).reshape(t, LANE).astype(BF16)

    dsk, dsv = band_add(dkp), band_add(dvp)
    gs["rel_bias_l"] = jnp.einsum("hts,tsb->bh", dbias.reshape(8, SWA_BLOCK, 2 * SWA_BLOCK), sv["onehot"],
                                  precision=lax.Precision.HIGHEST)
    gs["sink"] = dsink[:, :, 0].reshape(8)
    dfq, dfk, dfv, dck, dcq_row = _fox_bwd(sv["proj"], sv["c_col"], sv["c_row"], sv["lse_fox"], sv["ys"][1], dys[1],
                                           tag + "fox_bwd")
    dcq = jnp.pad(dcq_row.reshape(8, t).T, ((0, 0), (0, LANE - 8))).reshape(1, t, LANE)
    dfg, dfb = _logf_cumsum_bwd(sv["proj"], sv["fbias_row"], [dck, dcq], tag + "logf_cumsum")
    gs["forget_bias"] = dfb[0, :8]
    dcb, dcc, dcu, dconv = _conv_bwd(sv["proj"], wt["conv_w"][l], dys[0], tag + "conv")
    gw["conv_w"] = dconv[:3]
    dproj = jnp.concatenate([dgate, dcb, dcc, dcu, dfq, dfk, dfv, dsq, dsk, dsv, dfg], axis=1)
    gw["w_in"] = _unperm_in(_matmul(sv["h"], dproj, "tn", F32, tag + "dw_in"))
    g_mix = sm["mix_norm_g"][l] if late_hook is None else sm["mix_norm_g"][l] + late_hook(gw)
    dh = _matmul(dproj, wt["w_in"][l], "nt", F32, tag + "d_h")
    dx, dx_b, gs["mix_norm_g"] = _rms_bwd(sv["x"], g_mix, dh, dx1, tag + "mix_norm")
    return dx, dx_b, gw, gs


def kernel(x, mem, mix_norm_g, w_in, forget_bias, conv_w, sink, w_branch, w_mix_out, rel_bias, xattn_norm_g, mem_norm_g, w_xq, w_xkv, w_xo, ffn_norm_g, w_ffn_gate, w_ffn_up, w_ffn_down, final_norm_g, loss_target, m_mix_norm_g, m_w_in, m_forget_bias, m_conv_w, m_sink, m_w_branch, m_w_mix_out, m_rel_bias, m_xattn_norm_g, m_mem_norm_g, m_w_xq, m_w_xkv, m_w_xo, m_ffn_norm_g, m_w_ffn_gate, m_w_ffn_up, m_w_ffn_down, m_final_norm_g, v_mix_norm_g, v_w_in, v_forget_bias, v_conv_w, v_sink, v_w_branch, v_w_mix_out, v_rel_bias, v_xattn_norm_g, v_mem_norm_g, v_w_xq, v_w_xkv, v_w_xo, v_ffn_norm_g, v_w_ffn_gate, v_w_ffn_up, v_w_ffn_down, v_final_norm_g):
    args = dict(locals())
    names = [n for n, _ in MATRIX_WEIGHTS] + list(SMALL_PARAMS)
    w = {n: args[n] for n in names}
    mo = {n: args["m_" + n] for n in names}
    vo = {n: args["v_" + n] for n in names}
    x2d, mem2d, tgt = x[0], mem[0], loss_target[0]

    wire = {n: (F32 if n == "conv_w" else BF16) for n, _ in MATRIX_WEIGHTS}
    late = ("w_in", "conv_w")
    early_w = [(n, ax) for n, ax in MATRIX_WEIGHTS if n not in late]
    late_w = [(n, ax) for n, ax in MATRIX_WEIGHTS if n in late]
    wt = {n: [None] * DEPTH for n, _ in MATRIX_WEIGHTS}
    wt["w_gu"] = [None] * DEPTH

    ff_pad = FF_SHARD_P - FF_SHARD
    ff_axis = {"w_ffn_gate": 2, "w_ffn_up": 2, "w_ffn_down": 1}

    def pad_ffn(n, blocks):
        if n not in ff_axis:
            return blocks
        return jnp.pad(blocks, [(0, ff_pad if d == ff_axis[n] else 0) for d in range(blocks.ndim)])

    def unpad_ffn(n, blocks):
        return lax.slice_in_dim(blocks, 0, FF_SHARD, axis=ff_axis[n]) if n in ff_axis else blocks

    def place_weights(l, which, gathered):
        for (n, ax), g in zip(which, gathered):
            wt[n][l] = _to_full(pad_ffn(n, g), ax - 1)
        if "w_in" in dict(which):
            wt["w_in"][l] = _perm_in(wt["w_in"][l])
        if "w_ffn_gate" in dict(which):
            wt["w_gu"][l] = jnp.concatenate([wt["w_ffn_gate"][l], wt["w_ffn_up"][l]], axis=1)

    def shards_of(l, which):
        return [w[n][l].astype(wire[n]) for n, _ in which]

    def start_gather(srcs, name):
        lands = _own_slots(srcs, [lax.empty((N_DEV,) + s.shape, s.dtype) for s in srcs], None, True)
        return _swap_start(srcs, lands, None, True, name)

    place_weights(0, late_w, _all_gather(shards_of(0, late_w), "weights_gather_l0_first"))
    r_send, r_recv, r_srcs, r_lands, token = start_gather(shards_of(0, early_w), "weights_gather_l0_rest_start")
    shards1 = shards_of(1, MATRIX_WEIGHTS)
    shards1[0] = shards1[0] + token[0, 0].astype(shards1[0].dtype)
    w_send, w_recv, w_srcs, lands, token = start_gather(shards1, "weights_gather_l1_start")
    sm = {n: w[n] for n in SMALL_PARAMS}
    sm["mix_norm_g"] = w["mix_norm_g"].at[0].add(token[0, 0])
    sm["swa_tables"] = _swa_tables(w["rel_bias"])

    def rest_of_layer0(after):
        place_weights(0, early_w, _swap_wait(r_send, r_recv, r_srcs, r_lands, after, None, True,
                                             "weights_gather_l0_rest_wait"))

    saved = []
    xc = x2d
    for l in range(DEPTH):
        if l == 1:
            place_weights(1, MATRIX_WEIGHTS,
                          _swap_wait(w_send, w_recv, w_srcs, lands, xc, None, True, "weights_gather_l1_wait"))
        xc, sv = _layer_fwd(l, xc, mem2d, wt, sm, rest_of_layer0 if l == 0 else None)
        saved.append(sv)
    loss_row, dx, dx_b, dg_final = _loss_head(xc, sm["final_norm_g"], tgt, "loss_head")

    def grad_parts(gw, which):
        return [unpad_ffn(n, _to_blocks(gw[n], ax - 1)).astype(wire[n]) for n, ax in which]

    gw_all, gs_all = [None] * DEPTH, [None] * DEPTH
    dx, dx_b, gw_all[1], gs_all[1] = _layer_bwd(1, dx, dx_b, mem2d, wt, sm, saved[1])
    parts1 = grad_parts(gw_all[1], MATRIX_WEIGHTS)
    zones = _own_slots(parts1, [lax.empty((N_DEV, DEPTH) + p.shape[1:], p.dtype) for p in parts1], 1, False)
    g_send, g_recv, g_srcs, zones, token = _swap_start(parts1, zones, 1, False, "grads_exchange_l1_start")
    sm_b = dict(sm)
    sm_b["ffn_norm_g"] = sm["ffn_norm_g"].at[0].add(token[0, 0])
    mid = {}

    def mid_hook(gw):
        zone = dict(zip([n for n, _ in MATRIX_WEIGHTS],
                        _swap_wait(g_send, g_recv, g_srcs, zones, gw["w_mix_out"], 1, False, "grads_exchange_l1_wait")))
        parts0 = grad_parts(gw, early_w)
        early_zones = _own_slots(parts0, [zone[n] for n, _ in early_w], 0, False)
        mid["early"] = _swap_start(parts0, early_zones, 0, False, "grads_exchange_l0_early_start")
        mid["late_zones"] = [zone[n] for n, _ in late_w]
        return mid["early"][4][0, 0]

    def late_hook(gw):
        parts0 = grad_parts(gw, late_w)
        late_zones = _own_slots(parts0, mid["late_zones"], 0, False)
        mid["late"] = _swap_start(parts0, late_zones, 0, False, "grads_exchange_l0_late_start")
        return mid["late"][4][0, 0]

    dx, dx_b, gw_all[0], gs_all[0] = _layer_bwd(0, dx, dx_b, mem2d, wt, sm_b, saved[0], mid_hook, late_hook)
    grad_x = dx[None]
    recv_by_name = {}
    for key, which in (("early", early_w), ("late", late_w)):
        s_send, s_recv, s_srcs, s_zones, _ = mid[key]
        recv_by_name.update(zip([n for n, _ in which],
                                _swap_wait(s_send, s_recv, s_srcs, s_zones, dx, 0, False,
                                           "grads_exchange_l0_" + key + "_wait")))
    recv = [recv_by_name[n] for n, _ in MATRIX_WEIGHTS]

    outs = {}
    for (n, _), r in zip(MATRIX_WEIGHTS, recv):
        res = _adamw(r.reshape((N_DEV,) + _rows(w[n]).shape), _rows(w[n]), _rows(mo[n]), _rows(vo[n]), "adamw_" + n)
        outs[n] = [o.reshape(w[n].shape) for o in res]

    gsm = {n: jnp.stack([gs_all[l][n] for l in range(DEPTH)])
           for n in ("mix_norm_g", "xattn_norm_g", "mem_norm_g", "ffn_norm_g", "forget_bias", "sink")}
    gsm["final_norm_g"] = dg_final
    gsm["rel_bias"] = gs_all[0]["rel_bias_l"] + gs_all[1]["rel_bias_l"]
    zero = jnp.zeros((1,), F32)
    (small_parts,) = _all_gather([_pack_small([gsm[n] for n in SMALL_PARAMS] + [loss_row[0, :1]])],
                                 "small_grads_all_gather")
    outs_small = _adamw(small_parts, *[_pack_small([d[n] for n in SMALL_PARAMS] + [zero]) for d in (w, mo, vo)],
                        "adamw_small")
    for kind in range(4):
        flat, o = outs_small[kind].reshape(-1), 0
        for n in SMALL_PARAMS:
            sz = int(np.prod(w[n].shape))
            outs.setdefault(n, []).append(flat[o:o + sz].reshape(w[n].shape))
            o += sz
        if kind == 0:
            loss = flat[o]

    order = ["mix_norm_g", "w_in", "forget_bias", "conv_w", "sink", "w_branch", "w_mix_out", "rel_bias",
             "xattn_norm_g", "mem_norm_g", "w_xq", "w_xkv", "w_xo", "ffn_norm_g", "w_ffn_gate", "w_ffn_up",
             "w_ffn_down", "final_norm_g"]
    result = [loss, grad_x]
    for kind in range(4):
        result += [outs[n][kind] for n in order]
    return tuple(result)
```

```python
import math

import numpy as np
import jax
import jax.numpy as jnp
from jax import lax
from jax.experimental import pallas as pl
from jax.experimental.pallas import tpu as pltpu

F32 = jnp.float32
BF16 = jnp.bfloat16
MESH = pl.DeviceIdType.MESH

LANE = 128
BF16_SUBLANE = 16
V7X_VMEM_REQUEST_CAP = 56 * 2 ** 20
N_DEV = 8

D_MODEL = 1024
DEPTH = 2
HEAD = 64
BRANCH = 512
SWA_BLOCK = 128
SWA_GROUP = 4
N_BUCKETS = 32
X_HEADS = 4
X_HEAD = 256
D_FF = 2816
FF_SHARD = D_FF // N_DEV
FF_SHARD_P = -(-FF_SHARD // LANE) * LANE
D_FF_P = N_DEV * FF_SHARD_P
RMS_EPS = 1e-6
NEG = -1e30
ADAM_LR, ADAM_B1, ADAM_B2, ADAM_EPS, ADAM_WD, ADAM_STEP = 0.001, 0.9, 0.999, 1e-08, 0.01, 10

IN_COLS = 6920
PROJ_COLS = 7040
COL_GATE, COL_CONV, COL_FOX, COL_SQ, COL_SK, COL_SV, COL_FG = 0, 3072, 4608, 6144, 6656, 6784, 6912

ROW_TILE = 512
FOX_TILE = 1024
MM_TM, MM_TN, MM_TK = 1024, 1536, 2048


def _pick(n, cap, mult):
    best = None
    for d in range(mult, min(n, cap) + 1, mult):
        if n % d == 0:
            best = d
    return n if best is None else best


def _params(semantics, block_bytes):
    limit = int(min(max(2 * block_bytes + (8 << 20), 24 << 20), V7X_VMEM_REQUEST_CAP))
    return pltpu.CompilerParams(dimension_semantics=semantics, vmem_limit_bytes=limit)


def _nbytes(shape, dtype):
    return int(np.prod(shape)) * jnp.dtype(dtype).itemsize


def _dot(a, b, dims):
    return lax.dot_general(a, b, (dims, ((), ())), preferred_element_type=F32)


NN = ((1,), (0,))
NT = ((1,), (1,))
TN = ((0,), (0,))


def _matmul(a, b, mode, out_dtype, name, residual=None):
    if mode == "nn":
        (m, k), (k2, n) = a.shape, b.shape
    elif mode == "nt":
        (m, k), (n, k2) = a.shape, b.shape
    else:
        (k, m), (k2, n) = a.shape, b.shape
    assert k == k2, (name, a.shape, b.shape)
    tm, tn, tk = _pick(m, MM_TM, LANE), _pick(n, MM_TN, LANE), _pick(k, MM_TK, LANE)
    nk = k // tk
    dims = {"nn": NN, "nt": NT, "tn": TN}[mode]
    has_res = residual is not None

    def body(*refs):
        a_ref, b_ref = refs[0], refs[1]
        r_ref = refs[2] if has_res else None
        o_ref = refs[3] if has_res else refs[2]
        kk = pl.program_id(2)
        p = _dot(a_ref[...].astype(BF16), b_ref[...].astype(BF16), dims)
        if nk == 1:
            if has_res:
                p = p + r_ref[...]
            o_ref[...] = p.astype(out_dtype)
        else:
            acc_ref = refs[-1]

            @pl.when(kk == 0)
            def _():
                acc_ref[...] = p

            @pl.when(kk > 0)
            def _():
                acc_ref[...] += p

            @pl.when(kk == nk - 1)
            def _():
                res = acc_ref[...]
                if has_res:
                    res = res + r_ref[...]
                o_ref[...] = res.astype(out_dtype)

    if mode == "nn":
        a_spec = pl.BlockSpec((tm, tk), lambda i, j, kk: (i, kk))
        b_spec = pl.BlockSpec((tk, tn), lambda i, j, kk: (kk, j))
    elif mode == "nt":
        a_spec = pl.BlockSpec((tm, tk), lambda i, j, kk: (i, kk))
        b_spec = pl.BlockSpec((tn, tk), lambda i, j, kk: (j, kk))
    else:
        a_spec = pl.BlockSpec((tk, tm), lambda i, j, kk: (kk, i))
        b_spec = pl.BlockSpec((tk, tn), lambda i, j, kk: (kk, j))
    o_spec = pl.BlockSpec((tm, tn), lambda i, j, kk: (i, j))
    in_specs, args = [a_spec, b_spec], [a, b]
    if has_res:
        in_specs.append(o_spec)
        args.append(residual)
    blk = (_nbytes((tm, tk), a.dtype) + _nbytes((tk, tn), b.dtype) + _nbytes((tm, tn), out_dtype)
           + (_nbytes((tm, tn), F32) if has_res else 0))
    scratch = [pltpu.VMEM((tm, tn), F32)] if nk > 1 else []
    return pl.pallas_call(
        body, name=name, grid=(m // tm, n // tn, nk),
        out_shape=jax.ShapeDtypeStruct((m, n), out_dtype),
        in_specs=in_specs, out_specs=o_spec, scratch_shapes=scratch,
        compiler_params=_params(("parallel", "parallel", "arbitrary"), blk + _nbytes((tm, tn), F32)),
    )(*args)


def _rms_fwd(x, g, name):
    t, d = x.shape
    tr = _pick(t, ROW_TILE, BF16_SUBLANE)

    def body(x_ref, g_ref, y_ref):
        xv = x_ref[...]
        r = lax.rsqrt(jnp.mean(xv * xv, axis=-1, keepdims=True) + RMS_EPS)
        y_ref[...] = ((xv * r) * g_ref[...]).astype(BF16)

    return pl.pallas_call(
        body, name=name, grid=(t // tr,),
        out_shape=jax.ShapeDtypeStruct((t, d), BF16),
        in_specs=[pl.BlockSpec((tr, d), lambda i: (i, 0)), pl.BlockSpec((1, d), lambda i: (0, 0))],
        out_specs=pl.BlockSpec((tr, d), lambda i: (i, 0)),
        compiler_params=_params(("parallel",), 2 * _nbytes((tr, d), F32)),
    )(x, g.reshape(1, d))


def _rms_bwd(x, g, dy, dres, name):
    t, d = x.shape
    tr = _pick(t, ROW_TILE, BF16_SUBLANE)
    has_res = dres is not None

    def body(*refs):
        x_ref, g_ref, dy_ref = refs[:3]
        r_ref = refs[3] if has_res else None
        dx_ref, dxb_ref, dg_ref = refs[-3:]
        i = pl.program_id(0)
        xv = x_ref[...]
        r = lax.rsqrt(jnp.mean(xv * xv, axis=-1, keepdims=True) + RMS_EPS)
        xh = xv * r
        dyv = dy_ref[...].astype(F32)
        dxh = dyv * g_ref[...]
        dx = r * (dxh - xh * jnp.mean(dxh * xh, axis=-1, keepdims=True))
        if has_res:
            dx = dx + r_ref[...]
        dx_ref[...] = dx
        dxb_ref[...] = dx.astype(BF16)

        @pl.when(i == 0)
        def _():
            dg_ref[...] = jnp.zeros_like(dg_ref)

        dg_ref[...] += jnp.sum(dyv * xh, axis=0, keepdims=True)

    row = pl.BlockSpec((tr, d), lambda i: (i, 0))
    vec = pl.BlockSpec((1, d), lambda i: (0, 0))
    in_specs, args = [row, vec, row], [x, g.reshape(1, d), dy]
    if has_res:
        in_specs.append(row)
        args.append(dres)
    return pl.pallas_call(
        body, name=name, grid=(t // tr,),
        out_shape=(jax.ShapeDtypeStruct((t, d), F32), jax.ShapeDtypeStruct((t, d), BF16),
                   jax.ShapeDtypeStruct((1, d), F32)),
        in_specs=in_specs, out_specs=(row, row, vec),
        compiler_params=_params(("arbitrary",), 5 * _nbytes((tr, d), F32)),
    )(*args)


def _loss_head(x, g, target, name):
    t, d = x.shape
    tr = _pick(t, ROW_TILE, BF16_SUBLANE)

    def body(x_ref, g_ref, t_ref, loss_ref, dx_ref, dxb_ref, dg_ref):
        i = pl.program_id(0)
        xv = x_ref[...]
        gv = g_ref[...]
        r = lax.rsqrt(jnp.mean(xv * xv, axis=-1, keepdims=True) + RMS_EPS)
        xh = xv * r
        diff = xh * gv - t_ref[...]
        part = 0.5 * jnp.sum(jnp.mean(diff * diff, axis=-1, keepdims=True), axis=0, keepdims=True)
        dyv = diff * (1.0 / d)
        dxh = dyv * gv
        dx = r * (dxh - xh * jnp.mean(dxh * xh, axis=-1, keepdims=True))
        dx_ref[...] = dx
        dxb_ref[...] = dx.astype(BF16)

        @pl.when(i == 0)
        def _():
            dg_ref[...] = jnp.zeros_like(dg_ref)
            loss_ref[...] = jnp.zeros_like(loss_ref)

        dg_ref[...] += jnp.sum(dyv * xh, axis=0, keepdims=True)
        loss_ref[...] += jnp.broadcast_to(part, loss_ref.shape)

    row = pl.BlockSpec((tr, d), lambda i: (i, 0))
    vec = pl.BlockSpec((1, d), lambda i: (0, 0))
    return pl.pallas_call(
        body, name=name, grid=(t // tr,),
        out_shape=(jax.ShapeDtypeStruct((1, LANE), F32), jax.ShapeDtypeStruct((t, d), F32),
                   jax.ShapeDtypeStruct((t, d), BF16), jax.ShapeDtypeStruct((1, d), F32)),
        in_specs=[row, vec, row],
        out_specs=(pl.BlockSpec((1, LANE), lambda i: (0, 0)), row, row, vec),
        compiler_params=_params(("arbitrary",), 5 * _nbytes((tr, d), F32)),
    )(x, g.reshape(1, d), target)


HALO = 8


def _conv_fwd(proj, conv_w, name):
    t = proj.shape[0]
    tr = _pick(t, ROW_TILE, BF16_SUBLANE)
    c0 = COL_CONV // BRANCH
    hb = tr // HALO

    def body(cb_ref, cc_ref, cu_ref, hc_ref, hu_ref, w_ref, y_ref):
        i = pl.program_id(0)
        z = cc_ref[...] * cu_ref[...]
        hz = jnp.where(i > 0, hc_ref[...] * hu_ref[...], 0.0)
        zf = jnp.concatenate([hz, z], axis=0)
        z1 = pltpu.roll(zf, 1, 0)[HALO:]
        z2 = pltpu.roll(zf, 2, 0)[HALO:]
        y = w_ref[2:3, :] * z + w_ref[1:2, :] * z1 + w_ref[0:1, :] * z2
        y_ref[...] = (cb_ref[...] * y).astype(BF16)

    def col(c):
        return pl.BlockSpec((tr, BRANCH), lambda i, c=c: (i, c0 + c))

    def prev(c):
        return pl.BlockSpec((HALO, BRANCH), lambda i, c=c: (jnp.maximum(i * hb - 1, 0), c0 + c))

    return pl.pallas_call(
        body, name=name, grid=(t // tr,),
        out_shape=jax.ShapeDtypeStruct((t, BRANCH), BF16),
        in_specs=[col(0), col(1), col(2), prev(1), prev(2), pl.BlockSpec((3, BRANCH), lambda i: (0, 0))],
        out_specs=pl.BlockSpec((tr, BRANCH), lambda i: (i, 0)),
        compiler_params=_params(("parallel",), 6 * _nbytes((tr, BRANCH), F32)),
    )(proj, proj, proj, proj, proj, conv_w)


def _conv_bwd(proj, conv_w, dout, name):
    t = proj.shape[0]
    tr = _pick(t, ROW_TILE, BF16_SUBLANE)
    nblk = t // tr
    c0 = COL_CONV // BRANCH
    hb = tr // HALO
    last_halo = t // HALO - 1

    def body(cb_ref, cc_ref, cu_ref, hc_ref, hu_ref, do_ref, ndo_ref, ncb_ref, w_ref,
             dcb_ref, dcc_ref, dcu_ref, dw_ref):
        i = pl.program_id(0)
        cb, cc, cu = cb_ref[...], cc_ref[...], cu_ref[...]
        w0, w1, w2 = w_ref[0:1, :], w_ref[1:2, :], w_ref[2:3, :]
        z = cc * cu
        hz = jnp.where(i > 0, hc_ref[...] * hu_ref[...], 0.0)
        zf = jnp.concatenate([hz, z], axis=0)
        z1 = pltpu.roll(zf, 1, 0)[HALO:]
        z2 = pltpu.roll(zf, 2, 0)[HALO:]
        y = w2 * z + w1 * z1 + w0 * z2
        dout_v = do_ref[...]
        dyc = dout_v * cb
        hdy = jnp.where(i < nblk - 1, ndo_ref[...] * ncb_ref[...], 0.0)
        dyf = jnp.concatenate([dyc, hdy], axis=0)
        dy1 = pltpu.roll(dyf, tr + HALO - 1, 0)[:tr]
        dy2 = pltpu.roll(dyf, tr + HALO - 2, 0)[:tr]
        dz = w2 * dyc + w1 * dy1 + w0 * dy2
        dcb_ref[...] = (dout_v * y).astype(BF16)
        dcc_ref[...] = (dz * cu).astype(BF16)
        dcu_ref[...] = (dz * cc).astype(BF16)

        @pl.when(i == 0)
        def _():
            dw_ref[...] = jnp.zeros_like(dw_ref)

        dw_ref[0:1, :] += jnp.sum(dyc * z2, axis=0, keepdims=True)
        dw_ref[1:2, :] += jnp.sum(dyc * z1, axis=0, keepdims=True)
        dw_ref[2:3, :] += jnp.sum(dyc * z, axis=0, keepdims=True)

    def col(c):
        return pl.BlockSpec((tr, BRANCH), lambda i, c=c: (i, c0 + c))

    def prev(c):
        return pl.BlockSpec((HALO, BRANCH), lambda i, c=c: (jnp.maximum(i * hb - 1, 0), c0 + c))

    def nxt(c):
        return pl.BlockSpec((HALO, BRANCH), lambda i, c=c: (jnp.minimum((i + 1) * hb, last_halo), c))

    row = pl.BlockSpec((tr, BRANCH), lambda i: (i, 0))
    return pl.pallas_call(
        body, name=name, grid=(nblk,),
        out_shape=(jax.ShapeDtypeStruct((t, BRANCH), BF16),) * 3 + (jax.ShapeDtypeStruct((HALO, BRANCH), F32),),
        in_specs=[col(0), col(1), col(2), prev(1), prev(2), row, nxt(0), nxt(c0),
                  pl.BlockSpec((3, BRANCH), lambda i: (0, 0))],
        out_specs=(row, row, row, pl.BlockSpec((HALO, BRANCH), lambda i: (0, 0))),
        compiler_params=_params(("arbitrary",), 8 * _nbytes((tr, BRANCH), F32)),
    )(proj, proj, proj, proj, proj, dout, dout, proj, conv_w)


def _tri(lower):
    r = lax.broadcasted_iota(jnp.int32, (LANE, LANE), 0)
    c = lax.broadcasted_iota(jnp.int32, (LANE, LANE), 1)
    return jnp.where((c <= r) if lower else (c >= r), 1.0, 0.0).astype(F32)


def _logf_cumsum(proj, fbias_row, name):
    t = proj.shape[0]
    nchunk = t // LANE

    def body(f_ref, b_ref, c_ref, run_sc):
        tri = _tri(True)
        run_sc[...] = jnp.zeros_like(run_sc)

        @pl.loop(0, nchunk)
        def _(i):
            rows = pl.ds(pl.multiple_of(i * LANE, LANE), LANE)
            z = f_ref[rows, :] + b_ref[...]
            logf = jnp.minimum(z, 0.0) - jnp.log(1.0 + jnp.exp(-jnp.abs(z)))
            cs = lax.dot_general(tri, logf, (NN, ((), ())), precision=lax.Precision.HIGHEST,
                                 preferred_element_type=F32) + run_sc[0:1, :]
            c_ref[rows, :] = cs
            run_sc[0:1, :] = cs[LANE - 1:LANE, :]

    return pl.pallas_call(
        body, name=name, grid=(1,),
        out_shape=jax.ShapeDtypeStruct((t, LANE), F32),
        in_specs=[pl.BlockSpec((t, LANE), lambda i: (0, COL_FG // LANE)), pl.BlockSpec((1, LANE), lambda i: (0, 0))],
        out_specs=pl.BlockSpec((t, LANE), lambda i: (0, 0)),
        scratch_shapes=[pltpu.VMEM((8, LANE), F32)],
        compiler_params=_params(("arbitrary",), 2 * _nbytes((t, LANE), F32)),
    )(proj, fbias_row)


def _logf_cumsum_bwd(proj, fbias_row, pieces, name):
    t = proj.shape[0]
    tb = _pick(t, 2 * ROW_TILE, LANE)
    nblk = t // tb
    npiece = len(pieces)

    def body(*refs):
        f_ref, b_ref = refs[:2]
        piece_refs = refs[2:2 + npiece]
        df_ref, db_ref, run_sc = refs[2 + npiece:]
        i = pl.program_id(0)
        tri = _tri(False)

        @pl.when(i == 0)
        def _():
            run_sc[...] = jnp.zeros_like(run_sc)
            db_ref[...] = jnp.zeros_like(db_ref)

        for c in reversed(range(tb // LANE)):
            rows = slice(c * LANE, (c + 1) * LANE)
            slabs = [p_ref[n, rows, :] for p_ref in piece_refs for n in range(p_ref.shape[0])]
            dcc = slabs[0]
            for slab in slabs[1:]:
                dcc = dcc + slab
            ss = lax.dot_general(tri, dcc, (NN, ((), ())), precision=lax.Precision.HIGHEST,
                                 preferred_element_type=F32) + run_sc[0:1, :]
            z = f_ref[rows, :] + b_ref[...]
            dz = ss * (1.0 / (1.0 + jnp.exp(z)))
            df_ref[rows, :] = dz.astype(BF16)
            run_sc[0:1, :] = ss[0:1, :]
            db_ref[...] += jnp.sum(dz, axis=0, keepdims=True)

    piece_specs = [pl.BlockSpec((p.shape[0], tb, LANE), lambda i: (0, nblk - 1 - i, 0)) for p in pieces]
    nslab = sum(p.shape[0] for p in pieces)
    return pl.pallas_call(
        body, name=name, grid=(nblk,),
        out_shape=(jax.ShapeDtypeStruct((t, LANE), BF16), jax.ShapeDtypeStruct((1, LANE), F32)),
        in_specs=[pl.BlockSpec((tb, LANE), lambda i: (nblk - 1 - i, COL_FG // LANE)),
                  pl.BlockSpec((1, LANE), lambda i: (0, 0))] + piece_specs,
        out_specs=(pl.BlockSpec((tb, LANE), lambda i: (nblk - 1 - i, 0)), pl.BlockSpec((1, LANE), lambda i: (0, 0))),
        scratch_shapes=[pltpu.VMEM((8, LANE), F32)],
        compiler_params=_params(("arbitrary",), (4 + nslab) * _nbytes((tb, LANE), F32)),
    )(proj, fbias_row, *pieces)


def _lo_mask():
    return lax.broadcasted_iota(jnp.int32, (1, LANE), 1) < HEAD


def _causal_steps(n, key_major):
    if key_major:
        pairs = [(iq, ik) for ik in range(n) for iq in range(ik, n)]
    else:
        pairs = [(iq, ik) for iq in range(n) for ik in range(iq + 1)]
    return (jnp.asarray([p[0] for p in pairs], jnp.int32), jnp.asarray([p[1] for p in pairs], jnp.int32))


def _head_lanes(j, pair_vals):
    lane = lax.broadcasted_iota(jnp.int32, (1, LANE), 1)
    return jnp.where(lane == 2 * j, pair_vals[0], 0.0) + jnp.where(lane == 2 * j + 1, pair_vals[1], 0.0)


def _fox_fwd(proj, c_col, c_row, name):
    t = proj.shape[0]
    tq = _pick(t, FOX_TILE, LANE)
    nq = t // tq
    rep = tq // LANE
    scale = HEAD ** -0.5
    cq, ck, cv = COL_FOX // LANE, COL_FOX // LANE + 4, COL_FOX // LANE + 8
    q_tab, k_tab = _causal_steps(nq, False)

    def body(qt_ref, kt_ref, q_ref, k_ref, v_ref, ck_ref, cqr_ref, y_ref, lse_ref, m_sc, l_sc, acc_sc):
        step_id = pl.program_id(1)
        iq, ik = qt_ref[step_id], kt_ref[step_id]
        lo = _lo_mask()
        lo_rows = lax.broadcasted_iota(jnp.int32, (LANE, 1), 0) < HEAD

        @pl.when(ik == 0)
        def _():
            m_sc[...] = jnp.full(m_sc.shape, NEG, F32)
            l_sc[...] = jnp.zeros_like(l_sc)
            acc_sc[...] = jnp.zeros_like(acc_sc)

        def step(diag):
            q2 = (q_ref[...] * scale).astype(BF16)
            k2 = k_ref[...].astype(BF16)
            v2 = v_ref[...].astype(BF16)
            alphas, adds = [], []
            for h in range(2):
                msk = lo if h == 0 else jnp.logical_not(lo)
                kh = jnp.where(msk, k2, jnp.zeros_like(k2))
                vh = jnp.where(msk, v2, jnp.zeros_like(v2))
                st = _dot(kh, q2, NT) + cqr_ref[h] - jnp.tile(ck_ref[h], (1, rep))
                if diag:
                    krow = lax.broadcasted_iota(jnp.int32, (tq, tq), 0)
                    qcol = lax.broadcasted_iota(jnp.int32, (tq, tq), 1)
                    st = jnp.where(krow <= qcol, st, NEG)
                m_prev = m_sc[h]
                m_new = jnp.maximum(m_prev, jnp.max(st, axis=0, keepdims=True))
                alpha = jnp.exp(m_prev - m_new)
                pt = jnp.exp(st - m_new)
                l_sc[h] = alpha * l_sc[h] + jnp.sum(pt, axis=0, keepdims=True)
                m_sc[h] = m_new
                alphas.append(alpha)
                adds.append(_dot(vh, pt.astype(BF16), TN))
            acc_sc[...] = acc_sc[...] * jnp.where(lo_rows, alphas[0], alphas[1]) + (adds[0] + adds[1])

        @pl.when(ik < iq)
        def _():
            step(False)

        @pl.when(ik == iq)
        def _():
            step(True)
            yt = acc_sc[...] / jnp.where(lo_rows, l_sc[0], l_sc[1])
            y_ref[...] = yt.T.astype(BF16)
            lse_ref[...] = m_sc[...] + jnp.log(l_sc[...])

    def kv(c):
        return pl.BlockSpec((tq, LANE), lambda j, s, qt, kt, c=c: (kt[s], c + j))

    qrow = pl.BlockSpec((2, 1, tq), lambda j, s, qt, kt: (j, 0, qt[s]))
    grid_spec = pltpu.PrefetchScalarGridSpec(
        num_scalar_prefetch=2, grid=(4, int(q_tab.shape[0])),
        in_specs=[pl.BlockSpec((tq, LANE), lambda j, s, qt, kt: (qt[s], cq + j)), kv(ck), kv(cv),
                  pl.BlockSpec((2, tq, LANE), lambda j, s, qt, kt: (j, kt[s], 0)), qrow],
        out_specs=(pl.BlockSpec((tq, LANE), lambda j, s, qt, kt: (qt[s], j)), qrow),
        scratch_shapes=[pltpu.VMEM((2, 1, tq), F32), pltpu.VMEM((2, 1, tq), F32), pltpu.VMEM((LANE, tq), F32)])
    return pl.pallas_call(
        body, name=name, grid_spec=grid_spec,
        out_shape=(jax.ShapeDtypeStruct((t, BRANCH), BF16), jax.ShapeDtypeStruct((8, 1, t), F32)),
        compiler_params=_params(("parallel", "arbitrary"),
                                16 * _nbytes((tq, LANE), F32) + 6 * _nbytes((tq, tq), F32)),
    )(q_tab, k_tab, proj, proj, proj, c_col, c_row)


def _fox_bwd(proj, c_col, c_row, lse_row, y, dy, name):
    t = proj.shape[0]
    tb = _pick(t, FOX_TILE, LANE)
    nb = t // tb
    rep = tb // LANE
    scale = HEAD ** -0.5
    cq, ck, cv = COL_FOX // LANE, COL_FOX // LANE + 4, COL_FOX // LANE + 8
    q_tab, k_tab = _causal_steps(nb, True)
    nsteps = int(q_tab.shape[0])

    def body(qt_ref, kt_ref, k_ref, v_ref, q_ref, y_ref, dy_ref, ck_ref, cqr_ref, lser_ref,
             dq_ref, dk_ref, dv_ref, dck_ref, dcq_ref, dk_sc, dv_sc, dc_sc, dqt_sc, dcq_sc, d_sc):
        j, step_id = pl.program_id(0), pl.program_id(1)
        iq, ik = qt_ref[step_id], kt_ref[step_id]
        lo = _lo_mask()

        @pl.when(step_id == 0)
        def _():
            dqt_sc[...] = jnp.zeros_like(dqt_sc)
            dcq_sc[...] = jnp.zeros_like(dcq_sc)

        @pl.when(iq == ik)
        def _():
            dk_sc[...] = jnp.zeros_like(dk_sc)
            dv_sc[...] = jnp.zeros_like(dv_sc)
            dc_sc[...] = jnp.zeros_like(dc_sc)

        @pl.when(ik == 0)
        def _():
            prod = y_ref[...].astype(F32) * dy_ref[...].astype(F32)
            row = lax.broadcasted_iota(jnp.int32, (8, LANE), 0)
            sel = jnp.logical_or(jnp.logical_and(row == 0, lo), jnp.logical_and(row == 1, jnp.logical_not(lo)))
            d_sc[iq] = lax.dot_general(jnp.where(sel, 1.0, 0.0).astype(F32), prod, (NT, ((), ())),
                                       precision=lax.Precision.HIGHEST, preferred_element_type=F32)

        def step(diag):
            k2 = k_ref[...].astype(BF16)
            v2 = v_ref[...].astype(BF16)
            q2 = (q_ref[...] * scale).astype(BF16)
            do2 = dy_ref[...]
            d_rows = d_sc[iq]
            for h in range(2):
                msk = lo if h == 0 else jnp.logical_not(lo)
                kh = jnp.where(msk, k2, jnp.zeros_like(k2))
                vh = jnp.where(msk, v2, jnp.zeros_like(v2))
                st = _dot(kh, q2, NT) + (cqr_ref[h] - lser_ref[h]) - jnp.tile(ck_ref[h], (1, rep))
                if diag:
                    krow = lax.broadcasted_iota(jnp.int32, (tb, tb), 0)
                    qcol = lax.broadcasted_iota(jnp.int32, (tb, tb), 1)
                    st = jnp.where(krow <= qcol, st, NEG)
                pt = jnp.exp(st)
                dpt = _dot(vh, do2, NT)
                dst = pt * (dpt - d_rows[h:h + 1, :])
                dsb = dst.astype(BF16)
                dv_sc[h] += _dot(pt.astype(BF16), do2, NN)
                dk_sc[h] += _dot(dsb, q2, NN)
                dc_sc[h] -= jnp.sum(dst, axis=1, keepdims=True)
                dqt_sc[iq] += _dot(kh, dsb, TN)
                dcq_sc[h, iq] += jnp.sum(dst, axis=0, keepdims=True)

        @pl.when(iq > ik)
        def _():
            step(False)

        @pl.when(iq == ik)
        def _():
            step(True)

        @pl.when(iq == nb - 1)
        def _():
            dk_ref[...] = jnp.where(lo, dk_sc[0], dk_sc[1]).astype(BF16)
            dv_ref[...] = jnp.where(lo, dv_sc[0], dv_sc[1]).astype(BF16)
            dck_ref[...] = _head_lanes(j, dc_sc)

        @pl.when(step_id == nsteps - 1)
        def _():
            for i in range(nb):
                dq_ref[i * tb:(i + 1) * tb, :] = (dqt_sc[i].T * scale).astype(BF16)
                for h in range(2):
                    dcq_ref[h, :, i * tb:(i + 1) * tb] = dcq_sc[h, i]

    def kcol(c):
        return pl.BlockSpec((tb, LANE), lambda j, s, qt, kt, c=c: (kt[s], c + j))

    qrow = pl.BlockSpec((2, 1, tb), lambda j, s, qt, kt: (j, 0, qt[s]))
    pair_q = pl.BlockSpec((tb, LANE), lambda j, s, qt, kt: (qt[s], j))
    pair_k = pl.BlockSpec((tb, LANE), lambda j, s, qt, kt: (kt[s], j))
    grid_spec = pltpu.PrefetchScalarGridSpec(
        num_scalar_prefetch=2, grid=(4, nsteps),
        in_specs=[kcol(ck), kcol(cv), pl.BlockSpec((tb, LANE), lambda j, s, qt, kt: (qt[s], cq + j)), pair_q, pair_q,
                  pl.BlockSpec((2, tb, LANE), lambda j, s, qt, kt: (j, kt[s], 0)), qrow, qrow],
        out_specs=(pl.BlockSpec((t, LANE), lambda j, s, qt, kt: (0, j)), pair_k, pair_k,
                   pl.BlockSpec((None, tb, LANE), lambda j, s, qt, kt: (j, kt[s], 0)),
                   pl.BlockSpec((2, 1, t), lambda j, s, qt, kt: (j, 0, 0))),
        scratch_shapes=[pltpu.VMEM((2, tb, LANE), F32)] * 3
        + [pltpu.VMEM((nb, LANE, tb), F32), pltpu.VMEM((2, nb, 1, tb), F32), pltpu.VMEM((nb, 8, tb), F32)])
    return pl.pallas_call(
        body, name=name, grid_spec=grid_spec,
        out_shape=(jax.ShapeDtypeStruct((t, BRANCH), BF16), jax.ShapeDtypeStruct((t, BRANCH), BF16),
                   jax.ShapeDtypeStruct((t, BRANCH), BF16), jax.ShapeDtypeStruct((4, t, LANE), F32),
                   jax.ShapeDtypeStruct((8, 1, t), F32)),
        compiler_params=_params(("parallel", "arbitrary"),
                                24 * _nbytes((tb, LANE), F32) + 8 * _nbytes((tb, tb), F32)
                                + 2 * _nbytes((t, LANE), F32)),
    )(q_tab, k_tab, proj, proj, proj, y, dy, c_col, c_row, lse_row)


def _swa_tables(rel_bias):
    tq = np.arange(SWA_BLOCK)[:, None]
    sk = np.arange(2 * SWA_BLOCK)[None, :]
    dist = SWA_BLOCK + tq - sk
    inwin = (dist >= 0) & (dist < SWA_BLOCK)
    n = np.maximum(dist, 0)
    max_exact = N_BUCKETS // 2
    large = max_exact + (np.log(np.maximum(n, 1).astype(np.float32) / max_exact)
                         / math.log(SWA_BLOCK / max_exact) * (N_BUCKETS - max_exact)).astype(np.int32)
    bucket = np.where(n < max_exact, n, np.minimum(large, N_BUCKETS - 1))
    onehot = (bucket[..., None] == np.arange(N_BUCKETS)) & inwin[..., None]
    onehot = jnp.asarray(onehot.astype(np.float32))
    bias = jnp.einsum("tsb,bh->hts", onehot, rel_bias, precision=lax.Precision.HIGHEST)
    bias = jnp.where(jnp.asarray(inwin)[None], bias, NEG)
    return onehot, bias


SWA_BLOCKS_PER_STEP = 8


def _swa_specs(nb, r_blk, csk, csv):
    def prev(c):
        return pl.BlockSpec((SWA_BLOCK, LANE), lambda kvh, n, c=c: (jnp.maximum(n * r_blk - 1, 0), c))

    def cur(c):
        return pl.BlockSpec((r_blk * SWA_BLOCK, LANE), lambda kvh, n, c=c: (n, c))

    return [prev(csk), cur(csk), prev(csv), cur(csv)]


def _swa_fwd(proj, bias, sink_rep, name):
    t = proj.shape[0]
    nb = t // SWA_BLOCK
    r_blk = _pick(nb, SWA_BLOCKS_PER_STEP, 1)
    rows = r_blk * SWA_BLOCK
    scale = HEAD ** -0.5
    csq, csk, csv = COL_SQ // 256, COL_SK // LANE, COL_SV // LANE

    def body(q_ref, kp_ref, kc_ref, vp_ref, vc_ref, b_ref, sk_ref, y_ref, lse_ref):
        kvh, n = pl.program_id(0), pl.program_id(1)
        lane = lax.broadcasted_iota(jnp.int32, (1, LANE), 1)
        lo = lane < HEAD
        kvm = jnp.logical_and(lane >= kvh * HEAD, lane < (kvh + 1) * HEAD)

        def both(prev_ref, cur_ref):
            band = jnp.concatenate([prev_ref[...], cur_ref[...]], axis=0)
            band = jnp.where(kvm, band, 0.0)
            return (band + pltpu.roll(band, HEAD, 1)).astype(BF16)

        kb_all, vb_all = both(kp_ref, kc_ref), both(vp_ref, vc_ref)
        col = lax.broadcasted_iota(jnp.int32, (SWA_BLOCK, 2 * SWA_BLOCK), 1)
        first = jnp.logical_and(n == 0, col < SWA_BLOCK)
        for r in range(r_blk):
            rs = slice(r * SWA_BLOCK, (r + 1) * SWA_BLOCK)
            kb = kb_all[r * SWA_BLOCK:(r + 2) * SWA_BLOCK]
            vb = vb_all[r * SWA_BLOCK:(r + 2) * SWA_BLOCK]
            outs = []
            for g in range(SWA_GROUP):
                half = q_ref[rs, (g // 2) * LANE:(g // 2 + 1) * LANE]
                hm = lo if g % 2 == 0 else jnp.logical_not(lo)
                qg = jnp.where(hm, half, 0.0).astype(BF16)
                s = _dot(qg, kb, NT) * scale + b_ref[g]
                if r == 0:
                    s = jnp.where(first, NEG, s)
                snk = sk_ref[g:g + 1, :]
                m = jnp.maximum(jnp.max(s, axis=1, keepdims=True), snk)
                p = jnp.exp(s - jnp.tile(m, (1, 2)))
                denom = jnp.sum(p, axis=1, keepdims=True) + jnp.exp(snk - m)
                outs.append(_dot(p.astype(BF16), vb, NN) / denom)
                lse_ref[g, rs, :] = m + jnp.log(denom)
            y_ref[rs, 0:LANE] = jnp.where(lo, outs[0], outs[1]).astype(BF16)
            y_ref[rs, LANE:2 * LANE] = jnp.where(lo, outs[2], outs[3]).astype(BF16)

    return pl.pallas_call(
        body, name=name, grid=(2, nb // r_blk),
        out_shape=(jax.ShapeDtypeStruct((t, BRANCH), BF16), jax.ShapeDtypeStruct((8, t, LANE), F32)),
        in_specs=[pl.BlockSpec((rows, 256), lambda kvh, n: (n, csq + kvh))] + _swa_specs(nb, r_blk, csk, csv)
        + [pl.BlockSpec((None, SWA_GROUP, SWA_BLOCK, 256), lambda kvh, n: (kvh, 0, 0, 0)),
           pl.BlockSpec((None, SWA_GROUP, LANE), lambda kvh, n: (kvh, 0, 0))],
        out_specs=(pl.BlockSpec((rows, 256), lambda kvh, n: (n, kvh)),
                   pl.BlockSpec((SWA_GROUP, rows, LANE), lambda kvh, n: (kvh, n, 0))),
        compiler_params=_params(("parallel", "arbitrary"), 8 << 20),
    )(proj, proj, proj, proj, proj, bias.reshape(2, SWA_GROUP, SWA_BLOCK, 256), sink_rep)


def _swa_bwd(proj, bias, sink_rep, lse, y, dy, name):
    t = proj.shape[0]
    nb = t // SWA_BLOCK
    r_blk = _pick(nb, SWA_BLOCKS_PER_STEP, 1)
    rows = r_blk * SWA_BLOCK
    scale = HEAD ** -0.5
    csq, csk, csv = COL_SQ // 256, COL_SK // LANE, COL_SV // LANE

    def body(q_ref, kp_ref, kc_ref, vp_ref, vc_ref, b_ref, sk_ref, lse_ref, y_ref, dy_ref,
             dq_ref, dkp_ref, dvp_ref, db_ref, dsk_ref):
        kvh, n = pl.program_id(0), pl.program_id(1)
        lane = lax.broadcasted_iota(jnp.int32, (1, LANE), 1)
        lo = lane < HEAD
        kvm = jnp.logical_and(lane >= kvh * HEAD, lane < (kvh + 1) * HEAD)

        def both(prev_ref, cur_ref):
            band = jnp.concatenate([prev_ref[...], cur_ref[...]], axis=0)
            band = jnp.where(kvm, band, 0.0)
            return (band + pltpu.roll(band, HEAD, 1)).astype(BF16)

        kb_all, vb_all = both(kp_ref, kc_ref), both(vp_ref, vc_ref)
        col = lax.broadcasted_iota(jnp.int32, (SWA_BLOCK, 2 * SWA_BLOCK), 1)
        first = jnp.logical_and(n == 0, col < SWA_BLOCK)

        @pl.when(n == 0)
        def _():
            db_ref[...] = jnp.zeros_like(db_ref)
            dsk_ref[...] = jnp.zeros_like(dsk_ref)

        for r in range(r_blk):
            rs = slice(r * SWA_BLOCK, (r + 1) * SWA_BLOCK)
            kb = kb_all[r * SWA_BLOCK:(r + 2) * SWA_BLOCK]
            vb = vb_all[r * SWA_BLOCK:(r + 2) * SWA_BLOCK]
            dk_full = jnp.zeros((2 * SWA_BLOCK, LANE), F32)
            dv_full = jnp.zeros((2 * SWA_BLOCK, LANE), F32)
            dqs = []
            for g in range(SWA_GROUP):
                sl = slice((g // 2) * LANE, (g // 2 + 1) * LANE)
                hm = lo if g % 2 == 0 else jnp.logical_not(lo)
                qg = jnp.where(hm, q_ref[rs, sl], 0.0).astype(BF16)
                dog = jnp.where(hm, dy_ref[rs, sl], jnp.zeros((SWA_BLOCK, LANE), BF16))
                dmat = jnp.where(hm, y_ref[rs, sl].astype(F32) * dy_ref[rs, sl].astype(F32), 0.0)
                dg = jnp.sum(dmat, axis=1, keepdims=True)
                s = _dot(qg, kb, NT) * scale + b_ref[g]
                if r == 0:
                    s = jnp.where(first, NEG, s)
                lse_g = lse_ref[g, rs, :]
                p = jnp.exp(s - jnp.tile(lse_g, (1, 2)))
                dp = _dot(dog, vb, NT)
                ds = p * (dp - dg)
                dsb = ds.astype(BF16)
                dqs.append(_dot(dsb, kb, NN) * scale)
                dk_full = dk_full + _dot(dsb, qg, TN)
                dv_full = dv_full + _dot(p.astype(BF16), dog, TN)
                db_ref[g] += ds
                psink = jnp.exp(sk_ref[g:g + 1, :] - lse_g)
                dsk_ref[g:g + 1, :] -= jnp.sum(psink * dg, axis=0, keepdims=True)
            dq_ref[rs, 0:LANE] = jnp.where(lo, dqs[0], dqs[1]).astype(BF16)
            dq_ref[rs, LANE:2 * LANE] = jnp.where(lo, dqs[2], dqs[3]).astype(BF16)
            dkp_ref[r] = jnp.where(kvm, (dk_full + pltpu.roll(dk_full, HEAD, 1)) * scale, 0.0)
            dvp_ref[r] = jnp.where(kvm, dv_full + pltpu.roll(dv_full, HEAD, 1), 0.0)

    qblk = pl.BlockSpec((rows, 256), lambda kvh, n: (n, kvh))
    part = pl.BlockSpec((None, r_blk, 2 * SWA_BLOCK, LANE), lambda kvh, n: (kvh, n, 0, 0))
    bspec = pl.BlockSpec((None, SWA_GROUP, SWA_BLOCK, 256), lambda kvh, n: (kvh, 0, 0, 0))
    sspec = pl.BlockSpec((None, SWA_GROUP, LANE), lambda kvh, n: (kvh, 0, 0))
    return pl.pallas_call(
        body, name=name, grid=(2, nb // r_blk),
        out_shape=(jax.ShapeDtypeStruct((t, BRANCH), BF16),
                   jax.ShapeDtypeStruct((2, nb, 2 * SWA_BLOCK, LANE), F32),
                   jax.ShapeDtypeStruct((2, nb, 2 * SWA_BLOCK, LANE), F32),
                   jax.ShapeDtypeStruct((2, SWA_GROUP, SWA_BLOCK, 256), F32),
                   jax.ShapeDtypeStruct((2, SWA_GROUP, LANE), F32)),
        in_specs=[pl.BlockSpec((rows, 256), lambda kvh, n: (n, csq + kvh))] + _swa_specs(nb, r_blk, csk, csv)
        + [bspec, sspec, pl.BlockSpec((SWA_GROUP, rows, LANE), lambda kvh, n: (kvh, n, 0)), qblk, qblk],
        out_specs=(qblk, part, part, bspec, sspec),
        compiler_params=_params(("parallel", "arbitrary"), 12 << 20),
    )(proj, proj, proj, proj, proj, bias.reshape(2, SWA_GROUP, SWA_BLOCK, 256), sink_rep, lse, y, dy)


def _gate_fwd(proj, pb, name):
    t = proj.shape[0]
    tr = _pick(t, ROW_TILE // 2, BF16_SUBLANE)

    def body(g0, g1, g2, p0, p1, p2, o_ref):
        acc = jax.nn.sigmoid(g0[...]) * p0[...]
        acc = acc + jax.nn.sigmoid(g1[...]) * p1[...]
        acc = acc + jax.nn.sigmoid(g2[...]) * p2[...]
        o_ref[...] = acc.astype(BF16)

    row = pl.BlockSpec((tr, D_MODEL), lambda i: (i, 0))
    gates = [pl.BlockSpec((tr, D_MODEL), lambda i, b=b: (i, b)) for b in range(3)]
    return pl.pallas_call(
        body, name=name, grid=(t // tr,),
        out_shape=jax.ShapeDtypeStruct((t, D_MODEL), BF16),
        in_specs=gates + [row] * 3, out_specs=row,
        compiler_params=_params(("parallel",), 7 * _nbytes((tr, D_MODEL), F32)),
    )(proj, proj, proj, *pb)


def _gate_bwd(proj, pb, dmerged, name):
    t = proj.shape[0]
    tr = _pick(t, ROW_TILE // 2, BF16_SUBLANE)

    def body(g0, g1, g2, p0, p1, p2, dm_ref, dp0, dp1, dp2, dg_ref):
        dm = dm_ref[...]
        for b, (g_ref, p_ref, dp_ref) in enumerate(((g0, p0, dp0), (g1, p1, dp1), (g2, p2, dp2))):
            sg = jax.nn.sigmoid(g_ref[...])
            dp_ref[...] = (dm * sg).astype(BF16)
            dg_ref[:, b * D_MODEL:(b + 1) * D_MODEL] = (dm * p_ref[...] * sg * (1.0 - sg)).astype(BF16)

    row = pl.BlockSpec((tr, D_MODEL), lambda i: (i, 0))
    gates = [pl.BlockSpec((tr, D_MODEL), lambda i, b=b: (i, b)) for b in range(3)]
    return pl.pallas_call(
        body, name=name, grid=(t // tr,),
        out_shape=(jax.ShapeDtypeStruct((t, D_MODEL), BF16),) * 3 + (jax.ShapeDtypeStruct((t, 3 * D_MODEL), BF16),),
        in_specs=gates + [row] * 4,
        out_specs=(row, row, row, pl.BlockSpec((tr, 3 * D_MODEL), lambda i: (i, 0))),
        compiler_params=_params(("parallel",), 11 * _nbytes((tr, D_MODEL), F32)),
    )(proj, proj, proj, *pb, dmerged)


def _swiglu_fwd(ab, name):
    t = ab.shape[0]
    tr = _pick(t, ROW_TILE, BF16_SUBLANE)
    tc = D_FF_P // 2

    def body(a_ref, b_ref, o_ref):
        a = a_ref[...]
        o_ref[...] = (a * jax.nn.sigmoid(a) * b_ref[...]).astype(BF16)

    return pl.pallas_call(
        body, name=name, grid=(t // tr, 2),
        out_shape=jax.ShapeDtypeStruct((t, D_FF_P), BF16),
        in_specs=[pl.BlockSpec((tr, tc), lambda i, j: (i, j)), pl.BlockSpec((tr, tc), lambda i, j: (i, j + 2))],
        out_specs=pl.BlockSpec((tr, tc), lambda i, j: (i, j)),
        compiler_params=_params(("parallel", "parallel"), 3 * _nbytes((tr, tc), F32)),
    )(ab, ab)


def _swiglu_bwd(ab, dh, name):
    t = ab.shape[0]
    tr = _pick(t, ROW_TILE, BF16_SUBLANE)
    tc = D_FF_P // 2

    def body(a_ref, b_ref, dh_ref, da_ref, db_ref):
        a, b, d = a_ref[...], b_ref[...], dh_ref[...]
        sg = jax.nn.sigmoid(a)
        da_ref[...] = (d * b * (sg * (1.0 + a * (1.0 - sg)))).astype(BF16)
        db_ref[...] = (d * (a * sg)).astype(BF16)

    blk = pl.BlockSpec((tr, tc), lambda i, j: (i, j))
    return pl.pallas_call(
        body, name=name, grid=(t // tr, 2),
        out_shape=(jax.ShapeDtypeStruct((t, D_FF_P), BF16),) * 2,
        in_specs=[blk, pl.BlockSpec((tr, tc), lambda i, j: (i, j + 2)), blk],
        out_specs=(blk, blk),
        compiler_params=_params(("parallel", "parallel"), 5 * _nbytes((tr, tc), F32)),
    )(ab, ab, dh)


def _xattn_fwd(q, kv, name):
    t = q.shape[0]
    tq = _pick(t, ROW_TILE, BF16_SUBLANE)
    mlen = kv.shape[0]
    scale = X_HEAD ** -0.5

    def body(q_ref, kv_ref, o_ref):
        for h in range(X_HEADS):
            sl = slice(h * X_HEAD, (h + 1) * X_HEAD)
            kh = kv_ref[:, sl]
            vh = kv_ref[:, D_MODEL + h * X_HEAD:D_MODEL + (h + 1) * X_HEAD]
            s = _dot(q_ref[:, sl], kh, NT) * scale
            p = jnp.exp(s - jnp.max(s, axis=1, keepdims=True))
            l = jnp.sum(p, axis=1, keepdims=True)
            o_ref[:, sl] = (_dot(p.astype(BF16), vh, NN) / l).astype(BF16)

    return pl.pallas_call(
        body, name=name, grid=(t // tq,),
        out_shape=jax.ShapeDtypeStruct((t, D_MODEL), BF16),
        in_specs=[pl.BlockSpec((tq, D_MODEL), lambda i: (i, 0)), pl.BlockSpec((mlen, 2 * D_MODEL), lambda i: (0, 0))],
        out_specs=pl.BlockSpec((tq, D_MODEL), lambda i: (i, 0)),
        compiler_params=_params(("parallel",), 4 * _nbytes((tq, D_MODEL), F32)),
    )(q, kv)


def _xattn_bwd(q, kv, do, name):
    t = q.shape[0]
    tq = _pick(t, ROW_TILE, BF16_SUBLANE)
    mlen = kv.shape[0]
    scale = X_HEAD ** -0.5

    def body(q_ref, kv_ref, do_ref, dq_ref, dkv_ref):
        i = pl.program_id(0)

        @pl.when(i == 0)
        def _():
            dkv_ref[...] = jnp.zeros_like(dkv_ref)

        for h in range(X_HEADS):
            sl = slice(h * X_HEAD, (h + 1) * X_HEAD)
            vsl = slice(D_MODEL + h * X_HEAD, D_MODEL + (h + 1) * X_HEAD)
            qh, kh, vh, doh = q_ref[:, sl], kv_ref[:, sl], kv_ref[:, vsl], do_ref[:, sl]
            s = _dot(qh, kh, NT) * scale
            p = jnp.exp(s - jnp.max(s, axis=1, keepdims=True))
            p = p / jnp.sum(p, axis=1, keepdims=True)
            dp = _dot(doh, vh, NT)
            ds = p * (dp - jnp.sum(p * dp, axis=1, keepdims=True))
            dsb = ds.astype(BF16)
            dq_ref[:, sl] = (_dot(dsb, kh, NN) * scale).astype(BF16)
            dkv_ref[:, sl] += _dot(dsb, qh, TN) * scale
            dkv_ref[:, vsl] += _dot(p.astype(BF16), doh, TN)

    row = pl.BlockSpec((tq, D_MODEL), lambda i: (i, 0))
    whole = pl.BlockSpec((mlen, 2 * D_MODEL), lambda i: (0, 0))
    return pl.pallas_call(
        body, name=name, grid=(t // tq,),
        out_shape=(jax.ShapeDtypeStruct((t, D_MODEL), BF16), jax.ShapeDtypeStruct((mlen, 2 * D_MODEL), F32)),
        in_specs=[row, whole, row], out_specs=(row, whole),
        compiler_params=_params(("arbitrary",), 6 * _nbytes((tq, D_MODEL), F32)),
    )(q, kv, do)


def _position():
    return lax.axis_index("x"), lax.axis_index("y"), lax.axis_index("c")


N_PEER = N_DEV - 1


PEER_RELS = [(dx, dy, dc) for dx in (0, 1) for dy in (0, 1) for dc in (0, 1)][1:]


def _peer_copy(rel_k, i, src_refs, land_refs, send_sems, recv_sems, layer, gather, arriving):
    mx, my, mc = _position()
    me_idx = 4 * mx + 2 * my + mc
    p = tuple((1 - v) if f else v for f, v in zip(PEER_RELS[rel_k], (mx, my, mc)))
    p_idx = 4 * p[0] + 2 * p[1] + p[2]
    src_slot, dst_slot = (me_idx, p_idx) if arriving else (p_idx, me_idx)
    src = src_refs[i] if gather else src_refs[i].at[src_slot]
    dst = land_refs[i].at[dst_slot] if layer is None else land_refs[i].at[dst_slot, layer]
    return pltpu.make_async_remote_copy(
        src_ref=src, dst_ref=dst, send_sem=send_sems.at[i * N_PEER + rel_k], recv_sem=recv_sems.at[i * N_PEER + rel_k],
        device_id=p, device_id_type=MESH)


HBM_SPEC = pl.BlockSpec(memory_space=pltpu.HBM)
SEM_SPEC = pl.BlockSpec(memory_space=pltpu.SEMAPHORE)
SIDE_EFFECT = pltpu.SideEffectType.DATAFLOW_SIDE_EFFECTING


def _own_slots(srcs, lands, layer, gather):
    mx, my, mc = _position()
    me_idx = 4 * mx + 2 * my + mc
    out = []
    for s, land in zip(srcs, lands):
        piece = s[None] if gather else lax.dynamic_index_in_dim(s, me_idx, 0, keepdims=True)
        if layer is None:
            start = (me_idx,) + (0,) * (land.ndim - 1)
        else:
            piece, start = piece[:, None], (me_idx, layer) + (0,) * (land.ndim - 2)
        out.append(lax.dynamic_update_slice(land, piece, start))
    return out


def _swap_start(srcs, lands, layer, gather, name):
    n = len(srcs)

    def body(*refs):
        src_refs, land_refs = refs[:n], refs[n:2 * n]
        send_sems, recv_sems = refs[2 * n], refs[2 * n + 1]
        token = refs[4 * n + 2]
        for i in range(n):
            for k in range(N_PEER):
                _peer_copy(k, i, src_refs, land_refs, send_sems, recv_sems, layer, gather, False).start()
        token[...] = jnp.zeros_like(token)

    hbm = [pltpu.with_memory_space_constraint(a, pltpu.HBM) for a in list(srcs) + list(lands)]
    out = pl.pallas_call(
        body, name=name,
        out_shape=(pltpu.SemaphoreType.DMA((n * N_PEER,)), pltpu.SemaphoreType.DMA((n * N_PEER,)))
        + tuple(pltpu.HBM(a.shape, a.dtype) for a in hbm) + (jax.ShapeDtypeStruct((8, LANE), F32),),
        in_specs=[HBM_SPEC] * (2 * n),
        out_specs=(SEM_SPEC, SEM_SPEC) + (HBM_SPEC,) * (2 * n) + (pl.BlockSpec(memory_space=pltpu.VMEM),),
        input_output_aliases={i: 2 + i for i in range(2 * n)},
        compiler_params=pltpu.CompilerParams(has_side_effects=SIDE_EFFECT),
    )(*hbm)
    return out[0], out[1], list(out[2:2 + n]), list(out[2 + n:2 + 2 * n]), out[2 + 2 * n]


def _swap_wait(send_sems, recv_sems, srcs, lands, after, layer, gather, name):
    n = len(srcs)

    def body(*refs):
        src_refs, land_refs = refs[:n], refs[n:2 * n]
        send_sems_ref, recv_sems_ref = refs[2 * n], refs[2 * n + 1]
        for i in range(n):
            for k in range(N_PEER):
                args = (src_refs, land_refs, send_sems_ref, recv_sems_ref, layer, gather)
                _peer_copy(k, i, *args, False).wait_send()
                _peer_copy(k, i, *args, True).wait_recv()

    out = pl.pallas_call(
        body, name=name,
        out_shape=tuple(pltpu.HBM(a.shape, a.dtype) for a in list(srcs) + list(lands)),
        in_specs=[HBM_SPEC] * (2 * n) + [SEM_SPEC, SEM_SPEC, pl.BlockSpec(memory_space=pl.ANY)],
        out_specs=(HBM_SPEC,) * (2 * n),
        input_output_aliases={i: i for i in range(2 * n)},
        compiler_params=pltpu.CompilerParams(has_side_effects=SIDE_EFFECT),
    )(*srcs, *lands, send_sems, recv_sems, after)
    return list(out[n:])


ADAMW_BLOCK_BYTES = 1 << 20


def _adamw(parts, w, m, v, name):
    r, l = w.shape
    tr = _pick(r, max(ADAMW_BLOCK_BYTES // (4 * l), BF16_SUBLANE), BF16_SUBLANE)
    c1 = 1.0 - ADAM_B1 ** ADAM_STEP
    c2 = 1.0 - ADAM_B2 ** ADAM_STEP

    def body(p_ref, w_ref, m_ref, v_ref, g_ref, d_ref, nm_ref, nv_ref):
        g = p_ref[0].astype(F32)
        for s in range(1, N_DEV):
            g = g + p_ref[s].astype(F32)
        nm = ADAM_B1 * m_ref[...] + (1.0 - ADAM_B1) * g
        nv = ADAM_B2 * v_ref[...] + (1.0 - ADAM_B2) * (g * g)
        m_hat = nm / c1
        v_hat = nv / c2
        g_ref[...] = g
        d_ref[...] = -ADAM_LR * (m_hat / (jnp.sqrt(v_hat) + ADAM_EPS) + ADAM_WD * w_ref[...])
        nm_ref[...] = nm
        nv_ref[...] = nv

    row = pl.BlockSpec((tr, l), lambda i: (i, 0))
    return pl.pallas_call(
        body, name=name, grid=(r // tr,),
        out_shape=(jax.ShapeDtypeStruct((r, l), F32),) * 4,
        in_specs=[pl.BlockSpec((N_DEV, tr, l), lambda i: (0, i, 0)), row, row, row],
        out_specs=(row,) * 4,
        compiler_params=_params(("parallel",), 12 * _nbytes((tr, l), F32)),
    )(parts, w, m, v)


MATRIX_WEIGHTS = (("w_in", 2), ("conv_w", 2), ("w_branch", 3), ("w_mix_out", 1), ("w_xq", 1), ("w_xkv", 2),
                  ("w_xo", 1), ("w_ffn_gate", 2), ("w_ffn_up", 2), ("w_ffn_down", 1))
SMALL_PARAMS = ("mix_norm_g", "xattn_norm_g", "mem_norm_g", "ffn_norm_g", "final_norm_g", "forget_bias", "sink",
                "rel_bias")


def _pack_small(pieces):
    flat = jnp.concatenate([p.astype(F32).reshape(-1) for p in pieces])
    total = -(-flat.shape[0] // (8 * LANE)) * (8 * LANE)
    return jnp.pad(flat, (0, total - flat.shape[0])).reshape(total // LANE, LANE)


def _rows(a):
    return a.reshape(-1, a.shape[-1])


def _to_full(gathered, axis):
    moved = jnp.moveaxis(gathered, 0, axis)
    shape = list(moved.shape)
    shape[axis:axis + 2] = [shape[axis] * shape[axis + 1]]
    return moved.reshape(shape)


def _to_blocks(full, axis):
    shape = list(full.shape)
    shape[axis:axis + 1] = [N_DEV, shape[axis] // N_DEV]
    return jnp.moveaxis(full.reshape(shape), axis, 0)


def _perm_in(w_in):
    pad = jnp.zeros((w_in.shape[0], PROJ_COLS - IN_COLS), w_in.dtype)
    return jnp.concatenate([w_in[:, 3848:6920], w_in[:, 0:3072], w_in[:, 3080:3848], w_in[:, 3072:3080], pad], axis=1)


def _unperm_in(dw):
    return jnp.concatenate([dw[:, 3072:6144], dw[:, 6912:6920], dw[:, 6144:6912], dw[:, 0:3072]], axis=1)


def _layer_fwd(l, x, mem, wt, sm, pre_proj=None, pre_branch=None):
    t = x.shape[0]
    tag = f"l{l}_"
    h = _rms_fwd(x, sm["mix_norm_g"][l], tag + "mix_norm")
    if pre_proj is not None:
        pre_proj(h)
    proj = _matmul(h, wt["w_in"][l], "nn", F32, tag + "in_proj")
    y_conv = _conv_fwd(proj, wt["conv_w"][l], tag + "conv")
    fbias_row = jnp.pad(sm["forget_bias"][l], (0, LANE - 8)).reshape(1, LANE)
    c = _logf_cumsum(proj, fbias_row, tag + "logf_cumsum")
    c8 = c[:, :8].T
    c_col = jnp.broadcast_to(c8[:, :, None], (8, t, LANE))
    c_row = c8.reshape(8, 1, t)
    y_fox, lse_fox = _fox_fwd(proj, c_col, c_row, tag + "fox")
    onehot, bias = sm["swa_tables"]
    sink_rep = jnp.broadcast_to(sm["sink"][l].reshape(2, SWA_GROUP, 1), (2, SWA_GROUP, LANE))
    y_swa, lse_swa = _swa_fwd(proj, bias, sink_rep, tag + "swa")
    ys = (y_conv, y_fox, y_swa)
    if pre_branch is not None:
        pre_branch(y_swa)
    pb = tuple(_matmul(ys[b], wt["w_branch"][l][b], "nn", F32, tag + f"branch{b}") for b in range(3))
    merged = _gate_fwd(proj, pb, tag + "gate")
    x1 = _matmul(merged, wt["w_mix_out"][l], "nn", F32, tag + "mix_out", residual=x)
    xn2 = _rms_fwd(x1, sm["xattn_norm_g"][l], tag + "xattn_norm")
    q = _matmul(xn2, wt["w_xq"][l], "nn", BF16, tag + "xq")
    mem_n = _rms_fwd(mem, sm["mem_norm_g"][l], tag + "mem_norm")
    kv = _matmul(mem_n, wt["w_xkv"][l], "nn", BF16, tag + "xkv")
    o = _xattn_fwd(q, kv, tag + "xattn")
    x2 = _matmul(o, wt["w_xo"][l], "nn", F32, tag + "xo", residual=x1)
    xn3 = _rms_fwd(x2, sm["ffn_norm_g"][l], tag + "ffn_norm")
    ab = _matmul(xn3, wt["w_gu"][l], "nn", F32, tag + "ffn_gu")
    h1 = _swiglu_fwd(ab, tag + "swiglu")
    x3 = _matmul(h1, wt["w_ffn_down"][l], "nn", F32, tag + "ffn_down", residual=x2)
    saved = dict(x=x, h=h, proj=proj, fbias_row=fbias_row, c_col=c_col, c_row=c_row, ys=ys, lse_fox=lse_fox,
                 onehot=onehot, bias=bias, sink_rep=sink_rep, lse_swa=lse_swa, pb=pb, merged=merged, x1=x1,
                 xn2=xn2, q=q, mem_n=mem_n, kv=kv, o=o, x2=x2, xn3=xn3, ab=ab, h1=h1)
    return x3, saved


def _layer_bwd(l, dx3, dx3_b, mem, wt, sm, sv, mid_hook=None, late_hook=None):
    t = dx3.shape[0]
    nb = t // SWA_BLOCK
    tag = f"l{l}_b_"
    gw, gs = {}, {}
    dh1 = _matmul(dx3_b, wt["w_ffn_down"][l], "nt", F32, tag + "d_h1")
    gw["w_ffn_down"] = _matmul(sv["h1"], dx3_b, "tn", F32, tag + "dw_down")
    da, db = _swiglu_bwd(sv["ab"], dh1, tag + "swiglu")
    dxn3 = _matmul(da, wt["w_ffn_gate"][l], "nt", F32, tag + "d_xn3_gate")
    dxn3 = _matmul(db, wt["w_ffn_up"][l], "nt", F32, tag + "d_xn3_up", residual=dxn3)
    gw["w_ffn_gate"] = _matmul(sv["xn3"], da, "tn", F32, tag + "dw_gate")
    gw["w_ffn_up"] = _matmul(sv["xn3"], db, "tn", F32, tag + "dw_up")
    dx2, dx2_b, gs["ffn_norm_g"] = _rms_bwd(sv["x2"], sm["ffn_norm_g"][l], dxn3, dx3, tag + "ffn_norm")
    do = _matmul(dx2_b, wt["w_xo"][l], "nt", BF16, tag + "d_o")
    gw["w_xo"] = _matmul(sv["o"], dx2_b, "tn", F32, tag + "dw_xo")
    dq, dkv = _xattn_bwd(sv["q"], sv["kv"], do, tag + "xattn")
    gw["w_xkv"] = _matmul(sv["mem_n"], dkv, "tn", F32, tag + "dw_xkv")
    dmem_n = _matmul(dkv, wt["w_xkv"][l], "nt", F32, tag + "d_memn")
    _, _, gs["mem_norm_g"] = _rms_bwd(mem, sm["mem_norm_g"][l], dmem_n, None, tag + "mem_norm")
    gw["w_xq"] = _matmul(sv["xn2"], dq, "tn", F32, tag + "dw_xq")
    dxn2 = _matmul(dq, wt["w_xq"][l], "nt", F32, tag + "d_xn2")
    dx1, dx1_b, gs["xattn_norm_g"] = _rms_bwd(sv["x1"], sm["xattn_norm_g"][l], dxn2, dx2, tag + "xattn_norm")
    dmerged = _matmul(dx1_b, wt["w_mix_out"][l], "nt", F32, tag + "d_merged")
    gw["w_mix_out"] = _matmul(sv["merged"], dx1_b, "tn", F32, tag + "dw_mix_out")
    dp0, dp1, dp2, dgate = _gate_bwd(sv["proj"], sv["pb"], dmerged, tag + "gate")
    dps = (dp0, dp1, dp2)
    dy_dtypes = (F32, BF16, BF16)
    dys = [_matmul(dps[b], wt["w_branch"][l][b], "nt", dy_dtypes[b], tag + f"d_y{b}") for b in range(3)]
    gw["w_branch"] = jnp.stack(
        [_matmul(sv["ys"][b], dps[b], "tn", F32, tag + f"dw_branch{b}") for b in range(3)])
    sink_rep = sv["sink_rep"] if mid_hook is None else sv["sink_rep"] + mid_hook(gw)
    dsq, dkp, dvp, dbias, dsink = _swa_bwd(sv["proj"], sv["bias"], sink_rep, sv["lse_swa"], sv["ys"][2],
                                           dys[2], tag + "swa")

    def band_add(part):
        tot = part[0] + part[1]
        cur = tot[:, SWA_BLOCK:, :]
        nxt = jnp.concatenate([tot[1:, :SWA_BLOCK, :], jnp.zeros((1, SWA_BLOCK, LANE), F32)], axis=0)
        return (cur + nxt).reshape(t, LANE).astype(BF16)

    dsk, dsv = band_add(dkp), band_add(dvp)
    gs["rel_bias_l"] = jnp.einsum("hts,tsb->bh", dbias.reshape(8, SWA_BLOCK, 2 * SWA_BLOCK), sv["onehot"],
                                  precision=lax.Precision.HIGHEST)
    gs["sink"] = dsink[:, :, 0].reshape(8)
    dfq, dfk, dfv, dck, dcq_row = _fox_bwd(sv["proj"], sv["c_col"], sv["c_row"], sv["lse_fox"], sv["ys"][1], dys[1],
                                           tag + "fox_bwd")
    dcq = jnp.pad(dcq_row.reshape(8, t).T, ((0, 0), (0, LANE - 8))).reshape(1, t, LANE)
    dfg, dfb = _logf_cumsum_bwd(sv["proj"], sv["fbias_row"], [dck, dcq], tag + "logf_cumsum")
    gs["forget_bias"] = dfb[0, :8]
    dcb, dcc, dcu, dconv = _conv_bwd(sv["proj"], wt["conv_w"][l], dys[0], tag + "conv")
    gw["conv_w"] = dconv[:3]
    dproj = jnp.concatenate([dgate, dcb, dcc, dcu, dfq, dfk, dfv, dsq, dsk, dsv, dfg], axis=1)
    gw["w_in"] = _unperm_in(_matmul(sv["h"], dproj, "tn", F32, tag + "dw_in"))
    g_mix = sm["mix_norm_g"][l] if late_hook is None else sm["mix_norm_g"][l] + late_hook(gw)
    dh = _matmul(dproj, wt["w_in"][l], "nt", F32, tag + "d_h")
    dx, dx_b, gs["mix_norm_g"] = _rms_bwd(sv["x"], g_mix, dh, dx1, tag + "mix_norm")
    return dx, dx_b, gw, gs


def kernel(x, mem, mix_norm_g, w_in, forget_bias, conv_w, sink, w_branch, w_mix_out, rel_bias, xattn_norm_g, mem_norm_g, w_xq, w_xkv, w_xo, ffn_norm_g, w_ffn_gate, w_ffn_up, w_ffn_down, final_norm_g, loss_target, m_mix_norm_g, m_w_in, m_forget_bias, m_conv_w, m_sink, m_w_branch, m_w_mix_out, m_rel_bias, m_xattn_norm_g, m_mem_norm_g, m_w_xq, m_w_xkv, m_w_xo, m_ffn_norm_g, m_w_ffn_gate, m_w_ffn_up, m_w_ffn_down, m_final_norm_g, v_mix_norm_g, v_w_in, v_forget_bias, v_conv_w, v_sink, v_w_branch, v_w_mix_out, v_rel_bias, v_xattn_norm_g, v_mem_norm_g, v_w_xq, v_w_xkv, v_w_xo, v_ffn_norm_g, v_w_ffn_gate, v_w_ffn_up, v_w_ffn_down, v_final_norm_g):
    args = dict(locals())
    names = [n for n, _ in MATRIX_WEIGHTS] + list(SMALL_PARAMS)
    w = {n: args[n] for n in names}
    mo = {n: args["m_" + n] for n in names}
    vo = {n: args["v_" + n] for n in names}
    x2d, mem2d, tgt = x[0], mem[0], loss_target[0]

    wire = {n: (F32 if n == "conv_w" else BF16) for n, _ in MATRIX_WEIGHTS}
    late = ("w_in", "conv_w")
    early_w = [(n, ax) for n, ax in MATRIX_WEIGHTS if n not in late]
    late_w = [(n, ax) for n, ax in MATRIX_WEIGHTS if n in late]
    wt = {n: [None] * DEPTH for n, _ in MATRIX_WEIGHTS}
    wt["w_gu"] = [None] * DEPTH

    ff_pad = FF_SHARD_P - FF_SHARD
    ff_axis = {"w_ffn_gate": 2, "w_ffn_up": 2, "w_ffn_down": 1}

    def pad_ffn(n, blocks):
        if n not in ff_axis:
            return blocks
        return jnp.pad(blocks, [(0, ff_pad if d == ff_axis[n] else 0) for d in range(blocks.ndim)])

    def unpad_ffn(n, blocks):
        return lax.slice_in_dim(blocks, 0, FF_SHARD, axis=ff_axis[n]) if n in ff_axis else blocks

    def place_weights(l, which, gathered):
        for (n, ax), g in zip(which, gathered):
            wt[n][l] = _to_full(pad_ffn(n, g), ax - 1)
        if "w_in" in dict(which):
            wt["w_in"][l] = _perm_in(wt["w_in"][l])
        if "w_ffn_gate" in dict(which):
            wt["w_gu"][l] = jnp.concatenate([wt["w_ffn_gate"][l], wt["w_ffn_up"][l]], axis=1)

    def shards_of(l, which):
        return [w[n][l].astype(wire[n]) for n, _ in which]

    def start_gather(srcs, name):
        lands = _own_slots(srcs, [lax.empty((N_DEV,) + s.shape, s.dtype) for s in srcs], None, True)
        return _swap_start(srcs, lands, None, True, name)

    def after_token(srcs, token):
        return [srcs[0] + token[0, 0].astype(srcs[0].dtype)] + srcs[1:]

    f_send, f_recv, f_srcs, f_lands, token = start_gather(shards_of(0, late_w), "weights_gather_l0_first_start")
    r_send, r_recv, r_srcs, r_lands, token = start_gather(after_token(shards_of(0, early_w), token),
                                                          "weights_gather_l0_rest_start")
    w_send, w_recv, w_srcs, lands, token = start_gather(after_token(shards_of(1, MATRIX_WEIGHTS), token),
                                                        "weights_gather_l1_start")
    sm = {n: w[n] for n in SMALL_PARAMS}
    sm["mix_norm_g"] = w["mix_norm_g"].at[0].add(token[0, 0])
    sm["swa_tables"] = _swa_tables(w["rel_bias"])

    def first_of_layer0(after):
        place_weights(0, late_w, _swap_wait(f_send, f_recv, f_srcs, f_lands, after, None, True,
                                            "weights_gather_l0_first_wait"))

    def rest_of_layer0(after):
        place_weights(0, early_w, _swap_wait(r_send, r_recv, r_srcs, r_lands, after, None, True,
                                             "weights_gather_l0_rest_wait"))

    saved = []
    xc = x2d
    for l in range(DEPTH):
        if l == 1:
            place_weights(1, MATRIX_WEIGHTS,
                          _swap_wait(w_send, w_recv, w_srcs, lands, xc, None, True, "weights_gather_l1_wait"))
        hooks = (first_of_layer0, rest_of_layer0) if l == 0 else (None, None)
        xc, sv = _layer_fwd(l, xc, mem2d, wt, sm, *hooks)
        saved.append(sv)
    loss_row, dx, dx_b, dg_final = _loss_head(xc, sm["final_norm_g"], tgt, "loss_head")

    def grad_parts(gw, which):
        return [unpad_ffn(n, _to_blocks(gw[n], ax - 1)).astype(wire[n]) for n, ax in which]

    gw_all, gs_all = [None] * DEPTH, [None] * DEPTH
    dx, dx_b, gw_all[1], gs_all[1] = _layer_bwd(1, dx, dx_b, mem2d, wt, sm, saved[1])
    parts1 = grad_parts(gw_all[1], MATRIX_WEIGHTS)
    zones = _own_slots(parts1, [lax.empty((N_DEV, DEPTH) + p.shape[1:], p.dtype) for p in parts1], 1, False)
    g_send, g_recv, g_srcs, zones, token = _swap_start(parts1, zones, 1, False, "grads_exchange_l1_start")
    sm_b = dict(sm)
    sm_b["ffn_norm_g"] = sm["ffn_norm_g"].at[0].add(token[0, 0])
    mid = {}

    def mid_hook(gw):
        zone = dict(zip([n for n, _ in MATRIX_WEIGHTS],
                        _swap_wait(g_send, g_recv, g_srcs, zones, gw["w_mix_out"], 1, False, "grads_exchange_l1_wait")))
        parts0 = grad_parts(gw, early_w)
        early_zones = _own_slots(parts0, [zone[n] for n, _ in early_w], 0, False)
        mid["early"] = _swap_start(parts0, early_zones, 0, False, "grads_exchange_l0_early_start")
        mid["late_zones"] = [zone[n] for n, _ in late_w]
        return mid["early"][4][0, 0]

    def late_hook(gw):
        parts0 = grad_parts(gw, late_w)
        late_zones = _own_slots(parts0, mid["late_zones"], 0, False)
        mid["late"] = _swap_start(parts0, late_zones, 0, False, "grads_exchange_l0_late_start")
        return mid["late"][4][0, 0]

    dx, dx_b, gw_all[0], gs_all[0] = _layer_bwd(0, dx, dx_b, mem2d, wt, sm_b, saved[0], mid_hook, late_hook)
    grad_x = dx[None]
    recv_by_name = {}
    for key, which in (("early", early_w), ("late", late_w)):
        s_send, s_recv, s_srcs, s_zones, _ = mid[key]
        recv_by_name.update(zip([n for n, _ in which],
                                _swap_wait(s_send, s_recv, s_srcs, s_zones, dx, 0, False,
                                           "grads_exchange_l0_" + key + "_wait")))
    recv = [recv_by_name[n] for n, _ in MATRIX_WEIGHTS]

    gsm = {n: jnp.stack([gs_all[l][n] for l in range(DEPTH)])
           for n in ("mix_norm_g", "xattn_norm_g", "mem_norm_g", "ffn_norm_g", "forget_bias", "sink")}
    gsm["final_norm_g"] = dg_final
    gsm["rel_bias"] = gs_all[0]["rel_bias_l"] + gs_all[1]["rel_bias_l"]
    zero = jnp.zeros((1,), F32)
    s_send, s_recv, s_srcs, s_lands, token = start_gather(
        [_pack_small([gsm[n] for n in SMALL_PARAMS] + [loss_row[0, :1]])], "small_grads_gather_start")

    outs = {}
    for (n, _), r in zip(MATRIX_WEIGHTS, recv):
        w_rows = _rows(w[n]) + token[0, 0] if n == "conv_w" else _rows(w[n])
        res = _adamw(r.reshape((N_DEV,) + w_rows.shape), w_rows, _rows(mo[n]), _rows(vo[n]), "adamw_" + n)
        outs[n] = [o.reshape(w[n].shape) for o in res]

    (small_parts,) = _swap_wait(s_send, s_recv, s_srcs, s_lands, outs["w_ffn_down"][0], None, True,
                                "small_grads_gather_wait")
    outs_small = _adamw(small_parts, *[_pack_small([d[n] for n in SMALL_PARAMS] + [zero]) for d in (w, mo, vo)],
                        "adamw_small")
    for kind in range(4):
        flat, o = outs_small[kind].reshape(-1), 0
        for n in SMALL_PARAMS:
            sz = int(np.prod(w[n].shape))
            outs.setdefault(n, []).append(flat[o:o + sz].reshape(w[n].shape))
            o += sz
        if kind == 0:
            loss = flat[o]

    order = ["mix_norm_g", "w_in", "forget_bias", "conv_w", "sink", "w_branch", "w_mix_out", "rel_bias",
             "xattn_norm_g", "mem_norm_g", "w_xq", "w_xkv", "w_xo", "ffn_norm_g", "w_ffn_gate", "w_ffn_up",
             "w_ffn_down", "final_norm_g"]
    result = [loss, grad_x]
    for kind in range(4):
        result += [outs[n][kind] for n in order]
    return tuple(result)
```

```python
import math

import numpy as np
import jax
import jax.numpy as jnp
from jax import lax
from jax.experimental import pallas as pl
from jax.experimental.pallas import tpu as pltpu

F32 = jnp.float32
BF16 = jnp.bfloat16
MESH = pl.DeviceIdType.MESH

LANE = 128
BF16_SUBLANE = 16
V7X_VMEM_REQUEST_CAP = 56 * 2 ** 20
N_DEV = 8

D_MODEL = 1024
DEPTH = 2
HEAD = 64
BRANCH = 512
SWA_BLOCK = 128
SWA_GROUP = 4
N_BUCKETS = 32
X_HEADS = 4
X_HEAD = 256
D_FF = 2816
FF_SHARD = D_FF // N_DEV
FF_SHARD_P = -(-FF_SHARD // LANE) * LANE
D_FF_P = N_DEV * FF_SHARD_P
RMS_EPS = 1e-6
NEG = -1e30
ADAM_LR, ADAM_B1, ADAM_B2, ADAM_EPS, ADAM_WD, ADAM_STEP = 0.001, 0.9, 0.999, 1e-08, 0.01, 10

IN_COLS = 6920
PROJ_COLS = 7040
COL_GATE, COL_CONV, COL_FOX, COL_SQ, COL_SK, COL_SV, COL_FG = 0, 3072, 4608, 6144, 6656, 6784, 6912

ROW_TILE = 512
FOX_TILE = 1024
MM_TM, MM_TN, MM_TK = 2048, 1536, 2048


def _pick(n, cap, mult):
    best = None
    for d in range(mult, min(n, cap) + 1, mult):
        if n % d == 0:
            best = d
    return n if best is None else best


def _params(semantics, block_bytes):
    limit = int(min(max(2 * block_bytes + (8 << 20), 24 << 20), V7X_VMEM_REQUEST_CAP))
    return pltpu.CompilerParams(dimension_semantics=semantics, vmem_limit_bytes=limit)


def _nbytes(shape, dtype):
    return int(np.prod(shape)) * jnp.dtype(dtype).itemsize


def _dot(a, b, dims):
    return lax.dot_general(a, b, (dims, ((), ())), preferred_element_type=F32)


NN = ((1,), (0,))
NT = ((1,), (1,))
TN = ((0,), (0,))


def _matmul(a, b, mode, out_dtype, name, residual=None):
    if mode == "nn":
        (m, k), (k2, n) = a.shape, b.shape
    elif mode == "nt":
        (m, k), (n, k2) = a.shape, b.shape
    else:
        (k, m), (k2, n) = a.shape, b.shape
    assert k == k2, (name, a.shape, b.shape)
    tn, tk = _pick(n, MM_TN, LANE), _pick(k, MM_TK, LANE)
    nk = k // tk
    dims = {"nn": NN, "nt": NT, "tn": TN}[mode]
    has_res = residual is not None
    tm = _pick(m, MM_TM if (nk == 1 and not has_res) else MM_TM // 2, LANE)

    def body(*refs):
        a_ref, b_ref = refs[0], refs[1]
        r_ref = refs[2] if has_res else None
        o_ref = refs[3] if has_res else refs[2]
        kk = pl.program_id(2)
        p = _dot(a_ref[...].astype(BF16), b_ref[...].astype(BF16), dims)
        if nk == 1:
            if has_res:
                p = p + r_ref[...]
            o_ref[...] = p.astype(out_dtype)
        else:
            acc_ref = refs[-1]

            @pl.when(kk == 0)
            def _():
                acc_ref[...] = p

            @pl.when(kk > 0)
            def _():
                acc_ref[...] += p

            @pl.when(kk == nk - 1)
            def _():
                res = acc_ref[...]
                if has_res:
                    res = res + r_ref[...]
                o_ref[...] = res.astype(out_dtype)

    if mode == "nn":
        a_spec = pl.BlockSpec((tm, tk), lambda i, j, kk: (i, kk))
        b_spec = pl.BlockSpec((tk, tn), lambda i, j, kk: (kk, j))
    elif mode == "nt":
        a_spec = pl.BlockSpec((tm, tk), lambda i, j, kk: (i, kk))
        b_spec = pl.BlockSpec((tn, tk), lambda i, j, kk: (j, kk))
    else:
        a_spec = pl.BlockSpec((tk, tm), lambda i, j, kk: (kk, i))
        b_spec = pl.BlockSpec((tk, tn), lambda i, j, kk: (kk, j))
    o_spec = pl.BlockSpec((tm, tn), lambda i, j, kk: (i, j))
    in_specs, args = [a_spec, b_spec], [a, b]
    if has_res:
        in_specs.append(o_spec)
        args.append(residual)
    blk = (_nbytes((tm, tk), a.dtype) + _nbytes((tk, tn), b.dtype) + _nbytes((tm, tn), out_dtype)
           + (_nbytes((tm, tn), F32) if has_res else 0))
    scratch = [pltpu.VMEM((tm, tn), F32)] if nk > 1 else []
    return pl.pallas_call(
        body, name=name, grid=(m // tm, n // tn, nk),
        out_shape=jax.ShapeDtypeStruct((m, n), out_dtype),
        in_specs=in_specs, out_specs=o_spec, scratch_shapes=scratch,
        compiler_params=_params(("parallel", "parallel", "arbitrary"), blk + _nbytes((tm, tn), F32)),
    )(*args)


def _rms_fwd(x, g, name):
    t, d = x.shape
    tr = _pick(t, ROW_TILE, BF16_SUBLANE)

    def body(x_ref, g_ref, y_ref):
        xv = x_ref[...]
        r = lax.rsqrt(jnp.mean(xv * xv, axis=-1, keepdims=True) + RMS_EPS)
        y_ref[...] = ((xv * r) * g_ref[...]).astype(BF16)

    return pl.pallas_call(
        body, name=name, grid=(t // tr,),
        out_shape=jax.ShapeDtypeStruct((t, d), BF16),
        in_specs=[pl.BlockSpec((tr, d), lambda i: (i, 0)), pl.BlockSpec((1, d), lambda i: (0, 0))],
        out_specs=pl.BlockSpec((tr, d), lambda i: (i, 0)),
        compiler_params=_params(("parallel",), 2 * _nbytes((tr, d), F32)),
    )(x, g.reshape(1, d))


def _rms_bwd(x, g, dy, dres, name):
    t, d = x.shape
    tr = _pick(t, ROW_TILE, BF16_SUBLANE)
    has_res = dres is not None

    def body(*refs):
        x_ref, g_ref, dy_ref = refs[:3]
        r_ref = refs[3] if has_res else None
        dx_ref, dxb_ref, dg_ref = refs[-3:]
        i = pl.program_id(0)
        xv = x_ref[...]
        r = lax.rsqrt(jnp.mean(xv * xv, axis=-1, keepdims=True) + RMS_EPS)
        xh = xv * r
        dyv = dy_ref[...].astype(F32)
        dxh = dyv * g_ref[...]
        dx = r * (dxh - xh * jnp.mean(dxh * xh, axis=-1, keepdims=True))
        if has_res:
            dx = dx + r_ref[...]
        dx_ref[...] = dx
        dxb_ref[...] = dx.astype(BF16)

        @pl.when(i == 0)
        def _():
            dg_ref[...] = jnp.zeros_like(dg_ref)

        dg_ref[...] += jnp.sum(dyv * xh, axis=0, keepdims=True)

    row = pl.BlockSpec((tr, d), lambda i: (i, 0))
    vec = pl.BlockSpec((1, d), lambda i: (0, 0))
    in_specs, args = [row, vec, row], [x, g.reshape(1, d), dy]
    if has_res:
        in_specs.append(row)
        args.append(dres)
    return pl.pallas_call(
        body, name=name, grid=(t // tr,),
        out_shape=(jax.ShapeDtypeStruct((t, d), F32), jax.ShapeDtypeStruct((t, d), BF16),
                   jax.ShapeDtypeStruct((1, d), F32)),
        in_specs=in_specs, out_specs=(row, row, vec),
        compiler_params=_params(("arbitrary",), 5 * _nbytes((tr, d), F32)),
    )(*args)


def _loss_head(x, g, target, name):
    t, d = x.shape
    tr = _pick(t, ROW_TILE, BF16_SUBLANE)

    def body(x_ref, g_ref, t_ref, loss_ref, dx_ref, dxb_ref, dg_ref):
        i = pl.program_id(0)
        xv = x_ref[...]
        gv = g_ref[...]
        r = lax.rsqrt(jnp.mean(xv * xv, axis=-1, keepdims=True) + RMS_EPS)
        xh = xv * r
        diff = xh * gv - t_ref[...]
        part = 0.5 * jnp.sum(jnp.mean(diff * diff, axis=-1, keepdims=True), axis=0, keepdims=True)
        dyv = diff * (1.0 / d)
        dxh = dyv * gv
        dx = r * (dxh - xh * jnp.mean(dxh * xh, axis=-1, keepdims=True))
        dx_ref[...] = dx
        dxb_ref[...] = dx.astype(BF16)

        @pl.when(i == 0)
        def _():
            dg_ref[...] = jnp.zeros_like(dg_ref)
            loss_ref[...] = jnp.zeros_like(loss_ref)

        dg_ref[...] += jnp.sum(dyv * xh, axis=0, keepdims=True)
        loss_ref[...] += jnp.broadcast_to(part, loss_ref.shape)

    row = pl.BlockSpec((tr, d), lambda i: (i, 0))
    vec = pl.BlockSpec((1, d), lambda i: (0, 0))
    return pl.pallas_call(
        body, name=name, grid=(t // tr,),
        out_shape=(jax.ShapeDtypeStruct((1, LANE), F32), jax.ShapeDtypeStruct((t, d), F32),
                   jax.ShapeDtypeStruct((t, d), BF16), jax.ShapeDtypeStruct((1, d), F32)),
        in_specs=[row, vec, row],
        out_specs=(pl.BlockSpec((1, LANE), lambda i: (0, 0)), row, row, vec),
        compiler_params=_params(("arbitrary",), 5 * _nbytes((tr, d), F32)),
    )(x, g.reshape(1, d), target)


HALO = 8


def _conv_fwd(proj, conv_w, name):
    t = proj.shape[0]
    tr = _pick(t, ROW_TILE, BF16_SUBLANE)
    c0 = COL_CONV // BRANCH
    hb = tr // HALO

    def body(cb_ref, cc_ref, cu_ref, hc_ref, hu_ref, w_ref, y_ref):
        i = pl.program_id(0)
        z = cc_ref[...] * cu_ref[...]
        hz = jnp.where(i > 0, hc_ref[...] * hu_ref[...], 0.0)
        zf = jnp.concatenate([hz, z], axis=0)
        z1 = pltpu.roll(zf, 1, 0)[HALO:]
        z2 = pltpu.roll(zf, 2, 0)[HALO:]
        y = w_ref[2:3, :] * z + w_ref[1:2, :] * z1 + w_ref[0:1, :] * z2
        y_ref[...] = (cb_ref[...] * y).astype(BF16)

    def col(c):
        return pl.BlockSpec((tr, BRANCH), lambda i, c=c: (i, c0 + c))

    def prev(c):
        return pl.BlockSpec((HALO, BRANCH), lambda i, c=c: (jnp.maximum(i * hb - 1, 0), c0 + c))

    return pl.pallas_call(
        body, name=name, grid=(t // tr,),
        out_shape=jax.ShapeDtypeStruct((t, BRANCH), BF16),
        in_specs=[col(0), col(1), col(2), prev(1), prev(2), pl.BlockSpec((3, BRANCH), lambda i: (0, 0))],
        out_specs=pl.BlockSpec((tr, BRANCH), lambda i: (i, 0)),
        compiler_params=_params(("parallel",), 6 * _nbytes((tr, BRANCH), F32)),
    )(proj, proj, proj, proj, proj, conv_w)


def _conv_bwd(proj, conv_w, dout, name):
    t = proj.shape[0]
    tr = _pick(t, ROW_TILE, BF16_SUBLANE)
    nblk = t // tr
    c0 = COL_CONV // BRANCH
    hb = tr // HALO
    last_halo = t // HALO - 1

    def body(cb_ref, cc_ref, cu_ref, hc_ref, hu_ref, do_ref, ndo_ref, ncb_ref, w_ref,
             dcb_ref, dcc_ref, dcu_ref, dw_ref):
        i = pl.program_id(0)
        cb, cc, cu = cb_ref[...], cc_ref[...], cu_ref[...]
        w0, w1, w2 = w_ref[0:1, :], w_ref[1:2, :], w_ref[2:3, :]
        z = cc * cu
        hz = jnp.where(i > 0, hc_ref[...] * hu_ref[...], 0.0)
        zf = jnp.concatenate([hz, z], axis=0)
        z1 = pltpu.roll(zf, 1, 0)[HALO:]
        z2 = pltpu.roll(zf, 2, 0)[HALO:]
        y = w2 * z + w1 * z1 + w0 * z2
        dout_v = do_ref[...]
        dyc = dout_v * cb
        hdy = jnp.where(i < nblk - 1, ndo_ref[...] * ncb_ref[...], 0.0)
        dyf = jnp.concatenate([dyc, hdy], axis=0)
        dy1 = pltpu.roll(dyf, tr + HALO - 1, 0)[:tr]
        dy2 = pltpu.roll(dyf, tr + HALO - 2, 0)[:tr]
        dz = w2 * dyc + w1 * dy1 + w0 * dy2
        dcb_ref[...] = (dout_v * y).astype(BF16)
        dcc_ref[...] = (dz * cu).astype(BF16)
        dcu_ref[...] = (dz * cc).astype(BF16)

        @pl.when(i == 0)
        def _():
            dw_ref[...] = jnp.zeros_like(dw_ref)

        dw_ref[0:1, :] += jnp.sum(dyc * z2, axis=0, keepdims=True)
        dw_ref[1:2, :] += jnp.sum(dyc * z1, axis=0, keepdims=True)
        dw_ref[2:3, :] += jnp.sum(dyc * z, axis=0, keepdims=True)

    def col(c):
        return pl.BlockSpec((tr, BRANCH), lambda i, c=c: (i, c0 + c))

    def prev(c):
        return pl.BlockSpec((HALO, BRANCH), lambda i, c=c: (jnp.maximum(i * hb - 1, 0), c0 + c))

    def nxt(c):
        return pl.BlockSpec((HALO, BRANCH), lambda i, c=c: (jnp.minimum((i + 1) * hb, last_halo), c))

    row = pl.BlockSpec((tr, BRANCH), lambda i: (i, 0))
    return pl.pallas_call(
        body, name=name, grid=(nblk,),
        out_shape=(jax.ShapeDtypeStruct((t, BRANCH), BF16),) * 3 + (jax.ShapeDtypeStruct((HALO, BRANCH), F32),),
        in_specs=[col(0), col(1), col(2), prev(1), prev(2), row, nxt(0), nxt(c0),
                  pl.BlockSpec((3, BRANCH), lambda i: (0, 0))],
        out_specs=(row, row, row, pl.BlockSpec((HALO, BRANCH), lambda i: (0, 0))),
        compiler_params=_params(("arbitrary",), 8 * _nbytes((tr, BRANCH), F32)),
    )(proj, proj, proj, proj, proj, dout, dout, proj, conv_w)


def _tri(lower):
    r = lax.broadcasted_iota(jnp.int32, (LANE, LANE), 0)
    c = lax.broadcasted_iota(jnp.int32, (LANE, LANE), 1)
    return jnp.where((c <= r) if lower else (c >= r), 1.0, 0.0).astype(F32)


def _logf_cumsum(proj, fbias_row, name):
    t = proj.shape[0]
    nchunk = t // LANE

    def body(f_ref, b_ref, c_ref, run_sc):
        tri = _tri(True)
        run_sc[...] = jnp.zeros_like(run_sc)

        @pl.loop(0, nchunk)
        def _(i):
            rows = pl.ds(pl.multiple_of(i * LANE, LANE), LANE)
            z = f_ref[rows, :] + b_ref[...]
            logf = jnp.minimum(z, 0.0) - jnp.log(1.0 + jnp.exp(-jnp.abs(z)))
            cs = lax.dot_general(tri, logf, (NN, ((), ())), precision=lax.Precision.HIGHEST,
                                 preferred_element_type=F32) + run_sc[0:1, :]
            c_ref[rows, :] = cs
            run_sc[0:1, :] = cs[LANE - 1:LANE, :]

    return pl.pallas_call(
        body, name=name, grid=(1,),
        out_shape=jax.ShapeDtypeStruct((t, LANE), F32),
        in_specs=[pl.BlockSpec((t, LANE), lambda i: (0, COL_FG // LANE)), pl.BlockSpec((1, LANE), lambda i: (0, 0))],
        out_specs=pl.BlockSpec((t, LANE), lambda i: (0, 0)),
        scratch_shapes=[pltpu.VMEM((8, LANE), F32)],
        compiler_params=_params(("arbitrary",), 2 * _nbytes((t, LANE), F32)),
    )(proj, fbias_row)


def _logf_cumsum_bwd(proj, fbias_row, pieces, name):
    t = proj.shape[0]
    tb = _pick(t, 2 * ROW_TILE, LANE)
    nblk = t // tb
    npiece = len(pieces)

    def body(*refs):
        f_ref, b_ref = refs[:2]
        piece_refs = refs[2:2 + npiece]
        df_ref, db_ref, run_sc = refs[2 + npiece:]
        i = pl.program_id(0)
        tri = _tri(False)

        @pl.when(i == 0)
        def _():
            run_sc[...] = jnp.zeros_like(run_sc)
            db_ref[...] = jnp.zeros_like(db_ref)

        for c in reversed(range(tb // LANE)):
            rows = slice(c * LANE, (c + 1) * LANE)
            slabs = [p_ref[n, rows, :] for p_ref in piece_refs for n in range(p_ref.shape[0])]
            dcc = slabs[0]
            for slab in slabs[1:]:
                dcc = dcc + slab
            ss = lax.dot_general(tri, dcc, (NN, ((), ())), precision=lax.Precision.HIGHEST,
                                 preferred_element_type=F32) + run_sc[0:1, :]
            z = f_ref[rows, :] + b_ref[...]
            dz = ss * (1.0 / (1.0 + jnp.exp(z)))
            df_ref[rows, :] = dz.astype(BF16)
            run_sc[0:1, :] = ss[0:1, :]
            db_ref[...] += jnp.sum(dz, axis=0, keepdims=True)

    piece_specs = [pl.BlockSpec((p.shape[0], tb, LANE), lambda i: (0, nblk - 1 - i, 0)) for p in pieces]
    nslab = sum(p.shape[0] for p in pieces)
    return pl.pallas_call(
        body, name=name, grid=(nblk,),
        out_shape=(jax.ShapeDtypeStruct((t, LANE), BF16), jax.ShapeDtypeStruct((1, LANE), F32)),
        in_specs=[pl.BlockSpec((tb, LANE), lambda i: (nblk - 1 - i, COL_FG // LANE)),
                  pl.BlockSpec((1, LANE), lambda i: (0, 0))] + piece_specs,
        out_specs=(pl.BlockSpec((tb, LANE), lambda i: (nblk - 1 - i, 0)), pl.BlockSpec((1, LANE), lambda i: (0, 0))),
        scratch_shapes=[pltpu.VMEM((8, LANE), F32)],
        compiler_params=_params(("arbitrary",), (4 + nslab) * _nbytes((tb, LANE), F32)),
    )(proj, fbias_row, *pieces)


def _lo_mask():
    return lax.broadcasted_iota(jnp.int32, (1, LANE), 1) < HEAD


def _causal_steps(n, key_major):
    if key_major:
        pairs = [(iq, ik) for ik in range(n) for iq in range(ik, n)]
    else:
        pairs = [(iq, ik) for iq in range(n) for ik in range(iq + 1)]
    return (jnp.asarray([p[0] for p in pairs], jnp.int32), jnp.asarray([p[1] for p in pairs], jnp.int32))


def _head_lanes(j, pair_vals):
    lane = lax.broadcasted_iota(jnp.int32, (1, LANE), 1)
    return jnp.where(lane == 2 * j, pair_vals[0], 0.0) + jnp.where(lane == 2 * j + 1, pair_vals[1], 0.0)


def _fox_fwd(proj, c_col, c_row, name):
    t = proj.shape[0]
    tq = _pick(t, FOX_TILE, LANE)
    nq = t // tq
    rep = tq // LANE
    scale = HEAD ** -0.5
    cq, ck, cv = COL_FOX // LANE, COL_FOX // LANE + 4, COL_FOX // LANE + 8
    q_tab, k_tab = _causal_steps(nq, False)

    def body(qt_ref, kt_ref, q_ref, k_ref, v_ref, ck_ref, cqr_ref, y_ref, lse_ref, m_sc, l_sc, acc_sc):
        step_id = pl.program_id(1)
        iq, ik = qt_ref[step_id], kt_ref[step_id]
        lo = _lo_mask()
        lo_rows = lax.broadcasted_iota(jnp.int32, (LANE, 1), 0) < HEAD

        @pl.when(ik == 0)
        def _():
            m_sc[...] = jnp.full(m_sc.shape, NEG, F32)
            l_sc[...] = jnp.zeros_like(l_sc)
            acc_sc[...] = jnp.zeros_like(acc_sc)

        def step(diag):
            q2 = (q_ref[...] * scale).astype(BF16)
            k2 = k_ref[...].astype(BF16)
            v2 = v_ref[...].astype(BF16)
            alphas, adds = [], []
            for h in range(2):
                msk = lo if h == 0 else jnp.logical_not(lo)
                kh = jnp.where(msk, k2, jnp.zeros_like(k2))
                vh = jnp.where(msk, v2, jnp.zeros_like(v2))
                st = _dot(kh, q2, NT) + cqr_ref[h] - jnp.tile(ck_ref[h], (1, rep))
                if diag:
                    krow = lax.broadcasted_iota(jnp.int32, (tq, tq), 0)
                    qcol = lax.broadcasted_iota(jnp.int32, (tq, tq), 1)
                    st = jnp.where(krow <= qcol, st, NEG)
                m_prev = m_sc[h]
                m_new = jnp.maximum(m_prev, jnp.max(st, axis=0, keepdims=True))
                alpha = jnp.exp(m_prev - m_new)
                pt = jnp.exp(st - m_new)
                l_sc[h] = alpha * l_sc[h] + jnp.sum(pt, axis=0, keepdims=True)
                m_sc[h] = m_new
                alphas.append(alpha)
                adds.append(_dot(vh, pt.astype(BF16), TN))
            acc_sc[...] = acc_sc[...] * jnp.where(lo_rows, alphas[0], alphas[1]) + (adds[0] + adds[1])

        @pl.when(ik < iq)
        def _():
            step(False)

        @pl.when(ik == iq)
        def _():
            step(True)
            yt = acc_sc[...] / jnp.where(lo_rows, l_sc[0], l_sc[1])
            y_ref[...] = yt.T.astype(BF16)
            lse_ref[...] = m_sc[...] + jnp.log(l_sc[...])

    def kv(c):
        return pl.BlockSpec((tq, LANE), lambda j, s, qt, kt, c=c: (kt[s], c + j))

    qrow = pl.BlockSpec((2, 1, tq), lambda j, s, qt, kt: (j, 0, qt[s]))
    grid_spec = pltpu.PrefetchScalarGridSpec(
        num_scalar_prefetch=2, grid=(4, int(q_tab.shape[0])),
        in_specs=[pl.BlockSpec((tq, LANE), lambda j, s, qt, kt: (qt[s], cq + j)), kv(ck), kv(cv),
                  pl.BlockSpec((2, tq, LANE), lambda j, s, qt, kt: (j, kt[s], 0)), qrow],
        out_specs=(pl.BlockSpec((tq, LANE), lambda j, s, qt, kt: (qt[s], j)), qrow),
        scratch_shapes=[pltpu.VMEM((2, 1, tq), F32), pltpu.VMEM((2, 1, tq), F32), pltpu.VMEM((LANE, tq), F32)])
    return pl.pallas_call(
        body, name=name, grid_spec=grid_spec,
        out_shape=(jax.ShapeDtypeStruct((t, BRANCH), BF16), jax.ShapeDtypeStruct((8, 1, t), F32)),
        compiler_params=_params(("parallel", "arbitrary"),
                                16 * _nbytes((tq, LANE), F32) + 6 * _nbytes((tq, tq), F32)),
    )(q_tab, k_tab, proj, proj, proj, c_col, c_row)


def _fox_bwd(proj, c_col, c_row, lse_row, y, dy, name):
    t = proj.shape[0]
    tb = _pick(t, FOX_TILE, LANE)
    nb = t // tb
    rep = tb // LANE
    scale = HEAD ** -0.5
    cq, ck, cv = COL_FOX // LANE, COL_FOX // LANE + 4, COL_FOX // LANE + 8
    q_tab, k_tab = _causal_steps(nb, True)
    nsteps = int(q_tab.shape[0])

    def body(qt_ref, kt_ref, k_ref, v_ref, q_ref, y_ref, dy_ref, ck_ref, cqr_ref, lser_ref,
             dq_ref, dk_ref, dv_ref, dck_ref, dcq_ref, dk_sc, dv_sc, dc_sc, dqt_sc, dcq_sc, d_sc):
        j, step_id = pl.program_id(0), pl.program_id(1)
        iq, ik = qt_ref[step_id], kt_ref[step_id]
        lo = _lo_mask()

        @pl.when(step_id == 0)
        def _():
            dqt_sc[...] = jnp.zeros_like(dqt_sc)
            dcq_sc[...] = jnp.zeros_like(dcq_sc)

        @pl.when(iq == ik)
        def _():
            dk_sc[...] = jnp.zeros_like(dk_sc)
            dv_sc[...] = jnp.zeros_like(dv_sc)
            dc_sc[...] = jnp.zeros_like(dc_sc)

        @pl.when(ik == 0)
        def _():
            prod = y_ref[...].astype(F32) * dy_ref[...].astype(F32)
            row = lax.broadcasted_iota(jnp.int32, (8, LANE), 0)
            sel = jnp.logical_or(jnp.logical_and(row == 0, lo), jnp.logical_and(row == 1, jnp.logical_not(lo)))
            d_sc[iq] = lax.dot_general(jnp.where(sel, 1.0, 0.0).astype(F32), prod, (NT, ((), ())),
                                       precision=lax.Precision.HIGHEST, preferred_element_type=F32)

        def step(diag):
            k2 = k_ref[...].astype(BF16)
            v2 = v_ref[...].astype(BF16)
            q2 = (q_ref[...] * scale).astype(BF16)
            do2 = dy_ref[...]
            d_rows = d_sc[iq]
            for h in range(2):
                msk = lo if h == 0 else jnp.logical_not(lo)
                kh = jnp.where(msk, k2, jnp.zeros_like(k2))
                vh = jnp.where(msk, v2, jnp.zeros_like(v2))
                st = _dot(kh, q2, NT) + (cqr_ref[h] - lser_ref[h]) - jnp.tile(ck_ref[h], (1, rep))
                if diag:
                    krow = lax.broadcasted_iota(jnp.int32, (tb, tb), 0)
                    qcol = lax.broadcasted_iota(jnp.int32, (tb, tb), 1)
                    st = jnp.where(krow <= qcol, st, NEG)
                pt = jnp.exp(st)
                dpt = _dot(vh, do2, NT)
                dst = pt * (dpt - d_rows[h:h + 1, :])
                dsb = dst.astype(BF16)
                dv_sc[h] += _dot(pt.astype(BF16), do2, NN)
                dk_sc[h] += _dot(dsb, q2, NN)
                dc_sc[h] -= jnp.sum(dst, axis=1, keepdims=True)
                dqt_sc[iq] += _dot(kh, dsb, TN)
                dcq_sc[h, iq] += jnp.sum(dst, axis=0, keepdims=True)

        @pl.when(iq > ik)
        def _():
            step(False)

        @pl.when(iq == ik)
        def _():
            step(True)

        @pl.when(iq == nb - 1)
        def _():
            dk_ref[...] = jnp.where(lo, dk_sc[0], dk_sc[1]).astype(BF16)
            dv_ref[...] = jnp.where(lo, dv_sc[0], dv_sc[1]).astype(BF16)
            dck_ref[...] = _head_lanes(j, dc_sc)

        @pl.when(step_id == nsteps - 1)
        def _():
            for i in range(nb):
                dq_ref[i * tb:(i + 1) * tb, :] = (dqt_sc[i].T * scale).astype(BF16)
                for h in range(2):
                    dcq_ref[h, :, i * tb:(i + 1) * tb] = dcq_sc[h, i]

    def kcol(c):
        return pl.BlockSpec((tb, LANE), lambda j, s, qt, kt, c=c: (kt[s], c + j))

    qrow = pl.BlockSpec((2, 1, tb), lambda j, s, qt, kt: (j, 0, qt[s]))
    pair_q = pl.BlockSpec((tb, LANE), lambda j, s, qt, kt: (qt[s], j))
    pair_k = pl.BlockSpec((tb, LANE), lambda j, s, qt, kt: (kt[s], j))
    grid_spec = pltpu.PrefetchScalarGridSpec(
        num_scalar_prefetch=2, grid=(4, nsteps),
        in_specs=[kcol(ck), kcol(cv), pl.BlockSpec((tb, LANE), lambda j, s, qt, kt: (qt[s], cq + j)), pair_q, pair_q,
                  pl.BlockSpec((2, tb, LANE), lambda j, s, qt, kt: (j, kt[s], 0)), qrow, qrow],
        out_specs=(pl.BlockSpec((t, LANE), lambda j, s, qt, kt: (0, j)), pair_k, pair_k,
                   pl.BlockSpec((None, tb, LANE), lambda j, s, qt, kt: (j, kt[s], 0)),
                   pl.BlockSpec((2, 1, t), lambda j, s, qt, kt: (j, 0, 0))),
        scratch_shapes=[pltpu.VMEM((2, tb, LANE), F32)] * 3
        + [pltpu.VMEM((nb, LANE, tb), F32), pltpu.VMEM((2, nb, 1, tb), F32), pltpu.VMEM((nb, 8, tb), F32)])
    return pl.pallas_call(
        body, name=name, grid_spec=grid_spec,
        out_shape=(jax.ShapeDtypeStruct((t, BRANCH), BF16), jax.ShapeDtypeStruct((t, BRANCH), BF16),
                   jax.ShapeDtypeStruct((t, BRANCH), BF16), jax.ShapeDtypeStruct((4, t, LANE), F32),
                   jax.ShapeDtypeStruct((8, 1, t), F32)),
        compiler_params=_params(("parallel", "arbitrary"),
                                24 * _nbytes((tb, LANE), F32) + 8 * _nbytes((tb, tb), F32)
                                + 2 * _nbytes((t, LANE), F32)),
    )(q_tab, k_tab, proj, proj, proj, y, dy, c_col, c_row, lse_row)


def _swa_tables(rel_bias):
    tq = np.arange(SWA_BLOCK)[:, None]
    sk = np.arange(2 * SWA_BLOCK)[None, :]
    dist = SWA_BLOCK + tq - sk
    inwin = (dist >= 0) & (dist < SWA_BLOCK)
    n = np.maximum(dist, 0)
    max_exact = N_BUCKETS // 2
    large = max_exact + (np.log(np.maximum(n, 1).astype(np.float32) / max_exact)
                         / math.log(SWA_BLOCK / max_exact) * (N_BUCKETS - max_exact)).astype(np.int32)
    bucket = np.where(n < max_exact, n, np.minimum(large, N_BUCKETS - 1))
    onehot = (bucket[..., None] == np.arange(N_BUCKETS)) & inwin[..., None]
    onehot = jnp.asarray(onehot.astype(np.float32))
    bias = jnp.einsum("tsb,bh->hts", onehot, rel_bias, precision=lax.Precision.HIGHEST)
    bias = jnp.where(jnp.asarray(inwin)[None], bias, NEG)
    return onehot, bias


SWA_BLOCKS_PER_STEP = 8


def _swa_specs(nb, r_blk, csk, csv):
    def prev(c):
        return pl.BlockSpec((SWA_BLOCK, LANE), lambda kvh, n, c=c: (jnp.maximum(n * r_blk - 1, 0), c))

    def cur(c):
        return pl.BlockSpec((r_blk * SWA_BLOCK, LANE), lambda kvh, n, c=c: (n, c))

    return [prev(csk), cur(csk), prev(csv), cur(csv)]


def _swa_fwd(proj, bias, sink_rep, name):
    t = proj.shape[0]
    nb = t // SWA_BLOCK
    r_blk = _pick(nb, SWA_BLOCKS_PER_STEP, 1)
    rows = r_blk * SWA_BLOCK
    scale = HEAD ** -0.5
    csq, csk, csv = COL_SQ // 256, COL_SK // LANE, COL_SV // LANE

    def body(q_ref, kp_ref, kc_ref, vp_ref, vc_ref, b_ref, sk_ref, y_ref, lse_ref):
        kvh, n = pl.program_id(0), pl.program_id(1)
        lane = lax.broadcasted_iota(jnp.int32, (1, LANE), 1)
        lo = lane < HEAD
        kvm = jnp.logical_and(lane >= kvh * HEAD, lane < (kvh + 1) * HEAD)

        def both(prev_ref, cur_ref):
            band = jnp.concatenate([prev_ref[...], cur_ref[...]], axis=0)
            band = jnp.where(kvm, band, 0.0)
            return (band + pltpu.roll(band, HEAD, 1)).astype(BF16)

        kb_all, vb_all = both(kp_ref, kc_ref), both(vp_ref, vc_ref)
        col = lax.broadcasted_iota(jnp.int32, (SWA_BLOCK, 2 * SWA_BLOCK), 1)
        first = jnp.logical_and(n == 0, col < SWA_BLOCK)
        for r in range(r_blk):
            rs = slice(r * SWA_BLOCK, (r + 1) * SWA_BLOCK)
            kb = kb_all[r * SWA_BLOCK:(r + 2) * SWA_BLOCK]
            vb = vb_all[r * SWA_BLOCK:(r + 2) * SWA_BLOCK]
            outs = []
            for g in range(SWA_GROUP):
                half = q_ref[rs, (g // 2) * LANE:(g // 2 + 1) * LANE]
                hm = lo if g % 2 == 0 else jnp.logical_not(lo)
                qg = jnp.where(hm, half, 0.0).astype(BF16)
                s = _dot(qg, kb, NT) * scale + b_ref[g]
                if r == 0:
                    s = jnp.where(first, NEG, s)
                snk = sk_ref[g:g + 1, :]
                m = jnp.maximum(jnp.max(s, axis=1, keepdims=True), snk)
                p = jnp.exp(s - jnp.tile(m, (1, 2)))
                denom = jnp.sum(p, axis=1, keepdims=True) + jnp.exp(snk - m)
                outs.append(_dot(p.astype(BF16), vb, NN) / denom)
                lse_ref[g, rs, :] = m + jnp.log(denom)
            y_ref[rs, 0:LANE] = jnp.where(lo, outs[0], outs[1]).astype(BF16)
            y_ref[rs, LANE:2 * LANE] = jnp.where(lo, outs[2], outs[3]).astype(BF16)

    return pl.pallas_call(
        body, name=name, grid=(2, nb // r_blk),
        out_shape=(jax.ShapeDtypeStruct((t, BRANCH), BF16), jax.ShapeDtypeStruct((8, t, LANE), F32)),
        in_specs=[pl.BlockSpec((rows, 256), lambda kvh, n: (n, csq + kvh))] + _swa_specs(nb, r_blk, csk, csv)
        + [pl.BlockSpec((None, SWA_GROUP, SWA_BLOCK, 256), lambda kvh, n: (kvh, 0, 0, 0)),
           pl.BlockSpec((None, SWA_GROUP, LANE), lambda kvh, n: (kvh, 0, 0))],
        out_specs=(pl.BlockSpec((rows, 256), lambda kvh, n: (n, kvh)),
                   pl.BlockSpec((SWA_GROUP, rows, LANE), lambda kvh, n: (kvh, n, 0))),
        compiler_params=_params(("parallel", "arbitrary"), 8 << 20),
    )(proj, proj, proj, proj, proj, bias.reshape(2, SWA_GROUP, SWA_BLOCK, 256), sink_rep)


def _swa_bwd(proj, bias, sink_rep, lse, y, dy, name):
    t = proj.shape[0]
    nb = t // SWA_BLOCK
    r_blk = _pick(nb, SWA_BLOCKS_PER_STEP, 1)
    rows = r_blk * SWA_BLOCK
    scale = HEAD ** -0.5
    csq, csk, csv = COL_SQ // 256, COL_SK // LANE, COL_SV // LANE

    def body(q_ref, kp_ref, kc_ref, vp_ref, vc_ref, b_ref, sk_ref, lse_ref, y_ref, dy_ref,
             dq_ref, dkp_ref, dvp_ref, db_ref, dsk_ref):
        kvh, n = pl.program_id(0), pl.program_id(1)
        lane = lax.broadcasted_iota(jnp.int32, (1, LANE), 1)
        lo = lane < HEAD
        kvm = jnp.logical_and(lane >= kvh * HEAD, lane < (kvh + 1) * HEAD)

        def both(prev_ref, cur_ref):
            band = jnp.concatenate([prev_ref[...], cur_ref[...]], axis=0)
            band = jnp.where(kvm, band, 0.0)
            return (band + pltpu.roll(band, HEAD, 1)).astype(BF16)

        kb_all, vb_all = both(kp_ref, kc_ref), both(vp_ref, vc_ref)
        col = lax.broadcasted_iota(jnp.int32, (SWA_BLOCK, 2 * SWA_BLOCK), 1)
        first = jnp.logical_and(n == 0, col < SWA_BLOCK)

        @pl.when(n == 0)
        def _():
            db_ref[...] = jnp.zeros_like(db_ref)
            dsk_ref[...] = jnp.zeros_like(dsk_ref)

        for r in range(r_blk):
            rs = slice(r * SWA_BLOCK, (r + 1) * SWA_BLOCK)
            kb = kb_all[r * SWA_BLOCK:(r + 2) * SWA_BLOCK]
            vb = vb_all[r * SWA_BLOCK:(r + 2) * SWA_BLOCK]
            dk_full = jnp.zeros((2 * SWA_BLOCK, LANE), F32)
            dv_full = jnp.zeros((2 * SWA_BLOCK, LANE), F32)
            dqs = []
            for g in range(SWA_GROUP):
                sl = slice((g // 2) * LANE, (g // 2 + 1) * LANE)
                hm = lo if g % 2 == 0 else jnp.logical_not(lo)
                qg = jnp.where(hm, q_ref[rs, sl], 0.0).astype(BF16)
                dog = jnp.where(hm, dy_ref[rs, sl], jnp.zeros((SWA_BLOCK, LANE), BF16))
                dmat = jnp.where(hm, y_ref[rs, sl].astype(F32) * dy_ref[rs, sl].astype(F32), 0.0)
                dg = jnp.sum(dmat, axis=1, keepdims=True)
                s = _dot(qg, kb, NT) * scale + b_ref[g]
                if r == 0:
                    s = jnp.where(first, NEG, s)
                lse_g = lse_ref[g, rs, :]
                p = jnp.exp(s - jnp.tile(lse_g, (1, 2)))
                dp = _dot(dog, vb, NT)
                ds = p * (dp - dg)
                dsb = ds.astype(BF16)
                dqs.append(_dot(dsb, kb, NN) * scale)
                dk_full = dk_full + _dot(dsb, qg, TN)
                dv_full = dv_full + _dot(p.astype(BF16), dog, TN)
                db_ref[g] += ds
                psink = jnp.exp(sk_ref[g:g + 1, :] - lse_g)
                dsk_ref[g:g + 1, :] -= jnp.sum(psink * dg, axis=0, keepdims=True)
            dq_ref[rs, 0:LANE] = jnp.where(lo, dqs[0], dqs[1]).astype(BF16)
            dq_ref[rs, LANE:2 * LANE] = jnp.where(lo, dqs[2], dqs[3]).astype(BF16)
            dkp_ref[r] = jnp.where(kvm, (dk_full + pltpu.roll(dk_full, HEAD, 1)) * scale, 0.0)
            dvp_ref[r] = jnp.where(kvm, dv_full + pltpu.roll(dv_full, HEAD, 1), 0.0)

    qblk = pl.BlockSpec((rows, 256), lambda kvh, n: (n, kvh))
    part = pl.BlockSpec((None, r_blk, 2 * SWA_BLOCK, LANE), lambda kvh, n: (kvh, n, 0, 0))
    bspec = pl.BlockSpec((None, SWA_GROUP, SWA_BLOCK, 256), lambda kvh, n: (kvh, 0, 0, 0))
    sspec = pl.BlockSpec((None, SWA_GROUP, LANE), lambda kvh, n: (kvh, 0, 0))
    return pl.pallas_call(
        body, name=name, grid=(2, nb // r_blk),
        out_shape=(jax.ShapeDtypeStruct((t, BRANCH), BF16),
                   jax.ShapeDtypeStruct((2, nb, 2 * SWA_BLOCK, LANE), F32),
                   jax.ShapeDtypeStruct((2, nb, 2 * SWA_BLOCK, LANE), F32),
                   jax.ShapeDtypeStruct((2, SWA_GROUP, SWA_BLOCK, 256), F32),
                   jax.ShapeDtypeStruct((2, SWA_GROUP, LANE), F32)),
        in_specs=[pl.BlockSpec((rows, 256), lambda kvh, n: (n, csq + kvh))] + _swa_specs(nb, r_blk, csk, csv)
        + [bspec, sspec, pl.BlockSpec((SWA_GROUP, rows, LANE), lambda kvh, n: (kvh, n, 0)), qblk, qblk],
        out_specs=(qblk, part, part, bspec, sspec),
        compiler_params=_params(("parallel", "arbitrary"), 12 << 20),
    )(proj, proj, proj, proj, proj, bias.reshape(2, SWA_GROUP, SWA_BLOCK, 256), sink_rep, lse, y, dy)


def _gate_fwd(proj, pb, name):
    t = proj.shape[0]
    tr = _pick(t, ROW_TILE // 2, BF16_SUBLANE)

    def body(g0, g1, g2, p0, p1, p2, o_ref):
        acc = jax.nn.sigmoid(g0[...]) * p0[...]
        acc = acc + jax.nn.sigmoid(g1[...]) * p1[...]
        acc = acc + jax.nn.sigmoid(g2[...]) * p2[...]
        o_ref[...] = acc.astype(BF16)

    row = pl.BlockSpec((tr, D_MODEL), lambda i: (i, 0))
    gates = [pl.BlockSpec((tr, D_MODEL), lambda i, b=b: (i, b)) for b in range(3)]
    return pl.pallas_call(
        body, name=name, grid=(t // tr,),
        out_shape=jax.ShapeDtypeStruct((t, D_MODEL), BF16),
        in_specs=gates + [row] * 3, out_specs=row,
        compiler_params=_params(("parallel",), 7 * _nbytes((tr, D_MODEL), F32)),
    )(proj, proj, proj, *pb)


def _gate_bwd(proj, pb, dmerged, name):
    t = proj.shape[0]
    tr = _pick(t, ROW_TILE // 2, BF16_SUBLANE)

    def body(g0, g1, g2, p0, p1, p2, dm_ref, dp0, dp1, dp2, dg_ref):
        dm = dm_ref[...]
        for b, (g_ref, p_ref, dp_ref) in enumerate(((g0, p0, dp0), (g1, p1, dp1), (g2, p2, dp2))):
            sg = jax.nn.sigmoid(g_ref[...])
            dp_ref[...] = (dm * sg).astype(BF16)
            dg_ref[:, b * D_MODEL:(b + 1) * D_MODEL] = (dm * p_ref[...] * sg * (1.0 - sg)).astype(BF16)

    row = pl.BlockSpec((tr, D_MODEL), lambda i: (i, 0))
    gates = [pl.BlockSpec((tr, D_MODEL), lambda i, b=b: (i, b)) for b in range(3)]
    return pl.pallas_call(
        body, name=name, grid=(t // tr,),
        out_shape=(jax.ShapeDtypeStruct((t, D_MODEL), BF16),) * 3 + (jax.ShapeDtypeStruct((t, 3 * D_MODEL), BF16),),
        in_specs=gates + [row] * 4,
        out_specs=(row, row, row, pl.BlockSpec((tr, 3 * D_MODEL), lambda i: (i, 0))),
        compiler_params=_params(("parallel",), 11 * _nbytes((tr, D_MODEL), F32)),
    )(proj, proj, proj, *pb, dmerged)


def _swiglu_fwd(ab, name):
    t = ab.shape[0]
    tr = _pick(t, ROW_TILE, BF16_SUBLANE)
    tc = D_FF_P // 2

    def body(a_ref, b_ref, o_ref):
        a = a_ref[...]
        o_ref[...] = (a * jax.nn.sigmoid(a) * b_ref[...]).astype(BF16)

    return pl.pallas_call(
        body, name=name, grid=(t // tr, 2),
        out_shape=jax.ShapeDtypeStruct((t, D_FF_P), BF16),
        in_specs=[pl.BlockSpec((tr, tc), lambda i, j: (i, j)), pl.BlockSpec((tr, tc), lambda i, j: (i, j + 2))],
        out_specs=pl.BlockSpec((tr, tc), lambda i, j: (i, j)),
        compiler_params=_params(("parallel", "parallel"), 3 * _nbytes((tr, tc), F32)),
    )(ab, ab)


def _swiglu_bwd(ab, dh, name):
    t = ab.shape[0]
    tr = _pick(t, ROW_TILE, BF16_SUBLANE)
    tc = D_FF_P // 2

    def body(a_ref, b_ref, dh_ref, da_ref, db_ref):
        a, b, d = a_ref[...], b_ref[...], dh_ref[...]
        sg = jax.nn.sigmoid(a)
        da_ref[...] = (d * b * (sg * (1.0 + a * (1.0 - sg)))).astype(BF16)
        db_ref[...] = (d * (a * sg)).astype(BF16)

    blk = pl.BlockSpec((tr, tc), lambda i, j: (i, j))
    return pl.pallas_call(
        body, name=name, grid=(t // tr, 2),
        out_shape=(jax.ShapeDtypeStruct((t, D_FF_P), BF16),) * 2,
        in_specs=[blk, pl.BlockSpec((tr, tc), lambda i, j: (i, j + 2)), blk],
        out_specs=(blk, blk),
        compiler_params=_params(("parallel", "parallel"), 5 * _nbytes((tr, tc), F32)),
    )(ab, ab, dh)


def _xattn_fwd(q, kv, name):
    t = q.shape[0]
    tq = _pick(t, ROW_TILE, BF16_SUBLANE)
    mlen = kv.shape[0]
    scale = X_HEAD ** -0.5

    def body(q_ref, kv_ref, o_ref):
        for h in range(X_HEADS):
            sl = slice(h * X_HEAD, (h + 1) * X_HEAD)
            kh = kv_ref[:, sl]
            vh = kv_ref[:, D_MODEL + h * X_HEAD:D_MODEL + (h + 1) * X_HEAD]
            s = _dot(q_ref[:, sl], kh, NT) * scale
            p = jnp.exp(s - jnp.max(s, axis=1, keepdims=True))
            l = jnp.sum(p, axis=1, keepdims=True)
            o_ref[:, sl] = (_dot(p.astype(BF16), vh, NN) / l).astype(BF16)

    return pl.pallas_call(
        body, name=name, grid=(t // tq,),
        out_shape=jax.ShapeDtypeStruct((t, D_MODEL), BF16),
        in_specs=[pl.BlockSpec((tq, D_MODEL), lambda i: (i, 0)), pl.BlockSpec((mlen, 2 * D_MODEL), lambda i: (0, 0))],
        out_specs=pl.BlockSpec((tq, D_MODEL), lambda i: (i, 0)),
        compiler_params=_params(("parallel",), 4 * _nbytes((tq, D_MODEL), F32)),
    )(q, kv)


def _xattn_bwd(q, kv, do, name):
    t = q.shape[0]
    tq = _pick(t, ROW_TILE, BF16_SUBLANE)
    mlen = kv.shape[0]
    scale = X_HEAD ** -0.5

    def body(q_ref, kv_ref, do_ref, dq_ref, dkv_ref):
        i = pl.program_id(0)

        @pl.when(i == 0)
        def _():
            dkv_ref[...] = jnp.zeros_like(dkv_ref)

        for h in range(X_HEADS):
            sl = slice(h * X_HEAD, (h + 1) * X_HEAD)
            vsl = slice(D_MODEL + h * X_HEAD, D_MODEL + (h + 1) * X_HEAD)
            qh, kh, vh, doh = q_ref[:, sl], kv_ref[:, sl], kv_ref[:, vsl], do_ref[:, sl]
            s = _dot(qh, kh, NT) * scale
            p = jnp.exp(s - jnp.max(s, axis=1, keepdims=True))
            p = p / jnp.sum(p, axis=1, keepdims=True)
            dp = _dot(doh, vh, NT)
            ds = p * (dp - jnp.sum(p * dp, axis=1, keepdims=True))
            dsb = ds.astype(BF16)
            dq_ref[:, sl] = (_dot(dsb, kh, NN) * scale).astype(BF16)
            dkv_ref[:, sl] += _dot(dsb, qh, TN) * scale
            dkv_ref[:, vsl] += _dot(p.astype(BF16), doh, TN)

    row = pl.BlockSpec((tq, D_MODEL), lambda i: (i, 0))
    whole = pl.BlockSpec((mlen, 2 * D_MODEL), lambda i: (0, 0))
    return pl.pallas_call(
        body, name=name, grid=(t // tq,),
        out_shape=(jax.ShapeDtypeStruct((t, D_MODEL), BF16), jax.ShapeDtypeStruct((mlen, 2 * D_MODEL), F32)),
        in_specs=[row, whole, row], out_specs=(row, whole),
        compiler_params=_params(("arbitrary",), 6 * _nbytes((tq, D_MODEL), F32)),
    )(q, kv, do)


def _position():
    return lax.axis_index("x"), lax.axis_index("y"), lax.axis_index("c")


N_PEER = N_DEV - 1


def _all_gather(xs, name):
    n = len(xs)

    def body(*refs):
        x_refs, out_refs = refs[:n], refs[n:2 * n]
        send_sems, recv_sems, local_sems = refs[2 * n:]
        mx, my, mc = _position()
        me, sib = (mx, my, mc), (mx, my, 1 - mc)
        chips = [(1 - mx, my), (mx, 1 - my), (1 - mx, 1 - my)]

        def slot(i, p):
            return out_refs[i].at[4 * p[0] + 2 * p[1] + p[2]]

        def copy(i, k, block, to, src=None):
            return pltpu.make_async_remote_copy(
                src_ref=slot(i, block) if src is None else src, dst_ref=slot(i, block),
                send_sem=send_sems.at[i * N_PEER + k], recv_sem=recv_sems.at[i * N_PEER + k],
                device_id=to, device_id_type=MESH)

        mine = [pltpu.make_async_copy(x_refs[i], slot(i, me), local_sems.at[i]) for i in range(n)]
        for cp in mine:
            cp.start()
        first = [copy(i, 1 + j, me, (*chip, mc), src=x_refs[i]) for j, chip in enumerate(chips) for i in range(n)]
        first += [copy(i, 0, me, sib, src=x_refs[i]) for i in range(n)]
        for cp in first:
            cp.start()
        passed = []
        for j, chip in enumerate(chips):
            for i in range(n):
                copy(i, 1 + j, (*chip, mc), me).wait_recv()
                passed.append(copy(i, 4 + j, (*chip, mc), sib))
                passed[-1].start()
        for i in range(n):
            copy(i, 0, sib, me).wait_recv()
        for j, chip in enumerate(chips):
            for i in range(n):
                copy(i, 4 + j, (*chip, 1 - mc), me).wait_recv()
        for cp in first + passed:
            cp.wait_send()
        for cp in mine:
            cp.wait()

    return pl.pallas_call(
        body, name=name,
        out_shape=tuple(jax.ShapeDtypeStruct((N_DEV,) + x.shape, x.dtype) for x in xs),
        in_specs=[pl.BlockSpec(memory_space=pl.ANY)] * n, out_specs=(pl.BlockSpec(memory_space=pl.ANY),) * n,
        scratch_shapes=[pltpu.SemaphoreType.DMA((n * N_PEER,)), pltpu.SemaphoreType.DMA((n * N_PEER,)),
                        pltpu.SemaphoreType.DMA((n,))],
    )(*xs)


PEER_RELS = [(dx, dy, dc) for dx in (0, 1) for dy in (0, 1) for dc in (0, 1)][1:]


def _peer_copy(rel_k, i, src_refs, land_refs, send_sems, recv_sems, layer, gather, arriving):
    mx, my, mc = _position()
    me_idx = 4 * mx + 2 * my + mc
    p = tuple((1 - v) if f else v for f, v in zip(PEER_RELS[rel_k], (mx, my, mc)))
    p_idx = 4 * p[0] + 2 * p[1] + p[2]
    src_slot, dst_slot = (me_idx, p_idx) if arriving else (p_idx, me_idx)
    src = src_refs[i] if gather else src_refs[i].at[src_slot]
    dst = land_refs[i].at[dst_slot] if layer is None else land_refs[i].at[dst_slot, layer]
    return pltpu.make_async_remote_copy(
        src_ref=src, dst_ref=dst, send_sem=send_sems.at[i * N_PEER + rel_k], recv_sem=recv_sems.at[i * N_PEER + rel_k],
        device_id=p, device_id_type=MESH)


HBM_SPEC = pl.BlockSpec(memory_space=pltpu.HBM)
SEM_SPEC = pl.BlockSpec(memory_space=pltpu.SEMAPHORE)
SIDE_EFFECT = pltpu.SideEffectType.DATAFLOW_SIDE_EFFECTING


def _own_slots(srcs, lands, layer, gather):
    mx, my, mc = _position()
    me_idx = 4 * mx + 2 * my + mc
    out = []
    for s, land in zip(srcs, lands):
        piece = s[None] if gather else lax.dynamic_index_in_dim(s, me_idx, 0, keepdims=True)
        if layer is None:
            start = (me_idx,) + (0,) * (land.ndim - 1)
        else:
            piece, start = piece[:, None], (me_idx, layer) + (0,) * (land.ndim - 2)
        out.append(lax.dynamic_update_slice(land, piece, start))
    return out


def _swap_start(srcs, lands, layer, gather, name):
    n = len(srcs)

    def body(*refs):
        src_refs, land_refs = refs[:n], refs[n:2 * n]
        send_sems, recv_sems = refs[2 * n], refs[2 * n + 1]
        token = refs[4 * n + 2]
        for i in range(n):
            for k in range(N_PEER):
                _peer_copy(k, i, src_refs, land_refs, send_sems, recv_sems, layer, gather, False).start()
        token[...] = jnp.zeros_like(token)

    hbm = [pltpu.with_memory_space_constraint(a, pltpu.HBM) for a in list(srcs) + list(lands)]
    out = pl.pallas_call(
        body, name=name,
        out_shape=(pltpu.SemaphoreType.DMA((n * N_PEER,)), pltpu.SemaphoreType.DMA((n * N_PEER,)))
        + tuple(pltpu.HBM(a.shape, a.dtype) for a in hbm) + (jax.ShapeDtypeStruct((8, LANE), F32),),
        in_specs=[HBM_SPEC] * (2 * n),
        out_specs=(SEM_SPEC, SEM_SPEC) + (HBM_SPEC,) * (2 * n) + (pl.BlockSpec(memory_space=pltpu.VMEM),),
        input_output_aliases={i: 2 + i for i in range(2 * n)},
        compiler_params=pltpu.CompilerParams(has_side_effects=SIDE_EFFECT),
    )(*hbm)
    return out[0], out[1], list(out[2:2 + n]), list(out[2 + n:2 + 2 * n]), out[2 + 2 * n]


def _swap_wait(send_sems, recv_sems, srcs, lands, after, layer, gather, name):
    n = len(srcs)

    def body(*refs):
        src_refs, land_refs = refs[:n], refs[n:2 * n]
        send_sems_ref, recv_sems_ref = refs[2 * n], refs[2 * n + 1]
        for i in range(n):
            for k in range(N_PEER):
                args = (src_refs, land_refs, send_sems_ref, recv_sems_ref, layer, gather)
                _peer_copy(k, i, *args, False).wait_send()
                _peer_copy(k, i, *args, True).wait_recv()

    out = pl.pallas_call(
        body, name=name,
        out_shape=tuple(pltpu.HBM(a.shape, a.dtype) for a in list(srcs) + list(lands)),
        in_specs=[HBM_SPEC] * (2 * n) + [SEM_SPEC, SEM_SPEC, pl.BlockSpec(memory_space=pl.ANY)],
        out_specs=(HBM_SPEC,) * (2 * n),
        input_output_aliases={i: i for i in range(2 * n)},
        compiler_params=pltpu.CompilerParams(has_side_effects=SIDE_EFFECT),
    )(*srcs, *lands, send_sems, recv_sems, after)
    return list(out[n:])


ADAMW_BLOCK_BYTES = 1 << 20


def _adamw(parts, w, m, v, name):
    r, l = w.shape
    tr = _pick(r, max(ADAMW_BLOCK_BYTES // (4 * l), BF16_SUBLANE), BF16_SUBLANE)
    c1 = 1.0 - ADAM_B1 ** ADAM_STEP
    c2 = 1.0 - ADAM_B2 ** ADAM_STEP

    def body(p_ref, w_ref, m_ref, v_ref, g_ref, d_ref, nm_ref, nv_ref):
        g = p_ref[0].astype(F32)
        for s in range(1, N_DEV):
            g = g + p_ref[s].astype(F32)
        nm = ADAM_B1 * m_ref[...] + (1.0 - ADAM_B1) * g
        nv = ADAM_B2 * v_ref[...] + (1.0 - ADAM_B2) * (g * g)
        m_hat = nm / c1
        v_hat = nv / c2
        g_ref[...] = g
        d_ref[...] = -ADAM_LR * (m_hat / (jnp.sqrt(v_hat) + ADAM_EPS) + ADAM_WD * w_ref[...])
        nm_ref[...] = nm
        nv_ref[...] = nv

    row = pl.BlockSpec((tr, l), lambda i: (i, 0))
    return pl.pallas_call(
        body, name=name, grid=(r // tr,),
        out_shape=(jax.ShapeDtypeStruct((r, l), F32),) * 4,
        in_specs=[pl.BlockSpec((N_DEV, tr, l), lambda i: (0, i, 0)), row, row, row],
        out_specs=(row,) * 4,
        compiler_params=_params(("parallel",), 12 * _nbytes((tr, l), F32)),
    )(parts, w, m, v)


MATRIX_WEIGHTS = (("w_in", 2), ("conv_w", 2), ("w_branch", 3), ("w_mix_out", 1), ("w_xq", 1), ("w_xkv", 2),
                  ("w_xo", 1), ("w_ffn_gate", 2), ("w_ffn_up", 2), ("w_ffn_down", 1))
SMALL_PARAMS = ("mix_norm_g", "xattn_norm_g", "mem_norm_g", "ffn_norm_g", "final_norm_g", "forget_bias", "sink",
                "rel_bias")


def _pack_small(pieces):
    flat = jnp.concatenate([p.astype(F32).reshape(-1) for p in pieces])
    total = -(-flat.shape[0] // (8 * LANE)) * (8 * LANE)
    return jnp.pad(flat, (0, total - flat.shape[0])).reshape(total // LANE, LANE)


def _rows(a):
    return a.reshape(-1, a.shape[-1])


def _to_full(gathered, axis):
    moved = jnp.moveaxis(gathered, 0, axis)
    shape = list(moved.shape)
    shape[axis:axis + 2] = [shape[axis] * shape[axis + 1]]
    return moved.reshape(shape)


def _to_blocks(full, axis):
    shape = list(full.shape)
    shape[axis:axis + 1] = [N_DEV, shape[axis] // N_DEV]
    return jnp.moveaxis(full.reshape(shape), axis, 0)


def _perm_in(w_in):
    pad = jnp.zeros((w_in.shape[0], PROJ_COLS - IN_COLS), w_in.dtype)
    return jnp.concatenate([w_in[:, 3848:6920], w_in[:, 0:3072], w_in[:, 3080:3848], w_in[:, 3072:3080], pad], axis=1)


def _unperm_in(dw):
    return jnp.concatenate([dw[:, 3072:6144], dw[:, 6912:6920], dw[:, 6144:6912], dw[:, 0:3072]], axis=1)


def _layer_fwd(l, x, mem, wt, sm, pre_branch=None):
    t = x.shape[0]
    tag = f"l{l}_"
    h = _rms_fwd(x, sm["mix_norm_g"][l], tag + "mix_norm")
    proj = _matmul(h, wt["w_in"][l], "nn", F32, tag + "in_proj")
    y_conv = _conv_fwd(proj, wt["conv_w"][l], tag + "conv")
    fbias_row = jnp.pad(sm["forget_bias"][l], (0, LANE - 8)).reshape(1, LANE)
    c = _logf_cumsum(proj, fbias_row, tag + "logf_cumsum")
    c8 = c[:, :8].T
    c_col = jnp.broadcast_to(c8[:, :, None], (8, t, LANE))
    c_row = c8.reshape(8, 1, t)
    y_fox, lse_fox = _fox_fwd(proj, c_col, c_row, tag + "fox")
    onehot, bias = sm["swa_tables"]
    sink_rep = jnp.broadcast_to(sm["sink"][l].reshape(2, SWA_GROUP, 1), (2, SWA_GROUP, LANE))
    y_swa, lse_swa = _swa_fwd(proj, bias, sink_rep, tag + "swa")
    ys = (y_conv, y_fox, y_swa)
    if pre_branch is not None:
        pre_branch(y_swa)
    pb = tuple(_matmul(ys[b], wt["w_branch"][l][b], "nn", F32, tag + f"branch{b}") for b in range(3))
    merged = _gate_fwd(proj, pb, tag + "gate")
    x1 = _matmul(merged, wt["w_mix_out"][l], "nn", F32, tag + "mix_out", residual=x)
    xn2 = _rms_fwd(x1, sm["xattn_norm_g"][l], tag + "xattn_norm")
    q = _matmul(xn2, wt["w_xq"][l], "nn", BF16, tag + "xq")
    mem_n = _rms_fwd(mem, sm["mem_norm_g"][l], tag + "mem_norm")
    kv = _matmul(mem_n, wt["w_xkv"][l], "nn", BF16, tag + "xkv")
    o = _xattn_fwd(q, kv, tag + "xattn")
    x2 = _matmul(o, wt["w_xo"][l], "nn", F32, tag + "xo", residual=x1)
    xn3 = _rms_fwd(x2, sm["ffn_norm_g"][l], tag + "ffn_norm")
    ab = _matmul(xn3, wt["w_gu"][l], "nn", F32, tag + "ffn_gu")
    h1 = _swiglu_fwd(ab, tag + "swiglu")
    x3 = _matmul(h1, wt["w_ffn_down"][l], "nn", F32, tag + "ffn_down", residual=x2)
    saved = dict(x=x, h=h, proj=proj, fbias_row=fbias_row, c_col=c_col, c_row=c_row, ys=ys, lse_fox=lse_fox,
                 onehot=onehot, bias=bias, sink_rep=sink_rep, lse_swa=lse_swa, pb=pb, merged=merged, x1=x1,
                 xn2=xn2, q=q, mem_n=mem_n, kv=kv, o=o, x2=x2, xn3=xn3, ab=ab, h1=h1)
    return x3, saved


def _layer_bwd(l, dx3, dx3_b, mem, wt, sm, sv, mid_hook=None, late_hook=None):
    t = dx3.shape[0]
    nb = t // SWA_BLOCK
    tag = f"l{l}_b_"
    gw, gs = {}, {}
    dh1 = _matmul(dx3_b, wt["w_ffn_down"][l], "nt", F32, tag + "d_h1")
    gw["w_ffn_down"] = _matmul(sv["h1"], dx3_b, "tn", F32, tag + "dw_down")
    da, db = _swiglu_bwd(sv["ab"], dh1, tag + "swiglu")
    dxn3 = _matmul(da, wt["w_ffn_gate"][l], "nt", F32, tag + "d_xn3_gate")
    dxn3 = _matmul(db, wt["w_ffn_up"][l], "nt", F32, tag + "d_xn3_up", residual=dxn3)
    gw["w_ffn_gate"] = _matmul(sv["xn3"], da, "tn", F32, tag + "dw_gate")
    gw["w_ffn_up"] = _matmul(sv["xn3"], db, "tn", F32, tag + "dw_up")
    dx2, dx2_b, gs["ffn_norm_g"] = _rms_bwd(sv["x2"], sm["ffn_norm_g"][l], dxn3, dx3, tag + "ffn_norm")
    do = _matmul(dx2_b, wt["w_xo"][l], "nt", BF16, tag + "d_o")
    gw["w_xo"] = _matmul(sv["o"], dx2_b, "tn", F32, tag + "dw_xo")
    dq, dkv = _xattn_bwd(sv["q"], sv["kv"], do, tag + "xattn")
    gw["w_xkv"] = _matmul(sv["mem_n"], dkv, "tn", F32, tag + "dw_xkv")
    dmem_n = _matmul(dkv, wt["w_xkv"][l], "nt", F32, tag + "d_memn")
    _, _, gs["mem_norm_g"] = _rms_bwd(mem, sm["mem_norm_g"][l], dmem_n, None, tag + "mem_norm")
    gw["w_xq"] = _matmul(sv["xn2"], dq, "tn", F32, tag + "dw_xq")
    dxn2 = _matmul(dq, wt["w_xq"][l], "nt", F32, tag + "d_xn2")
    dx1, dx1_b, gs["xattn_norm_g"] = _rms_bwd(sv["x1"], sm["xattn_norm_g"][l], dxn2, dx2, tag + "xattn_norm")
    dmerged = _matmul(dx1_b, wt["w_mix_out"][l], "nt", F32, tag + "d_merged")
    gw["w_mix_out"] = _matmul(sv["merged"], dx1_b, "tn", F32, tag + "dw_mix_out")
    dp0, dp1, dp2, dgate = _gate_bwd(sv["proj"], sv["pb"], dmerged, tag + "gate")
    dps = (dp0, dp1, dp2)
    dy_dtypes = (F32, BF16, BF16)
    dys = [_matmul(dps[b], wt["w_branch"][l][b], "nt", dy_dtypes[b], tag + f"d_y{b}") for b in range(3)]
    gw["w_branch"] = jnp.stack(
        [_matmul(sv["ys"][b], dps[b], "tn", F32, tag + f"dw_branch{b}") for b in range(3)])
    sink_rep = sv["sink_rep"] if mid_hook is None else sv["sink_rep"] + mid_hook(gw)
    dsq, dkp, dvp, dbias, dsink = _swa_bwd(sv["proj"], sv["bias"], sink_rep, sv["lse_swa"], sv["ys"][2],
                                           dys[2], tag + "swa")

    def band_add(part):
        tot = part[0] + part[1]
        cur = tot[:, SWA_BLOCK:, :]
        nxt = jnp.concatenate([tot[1:, :SWA_BLOCK, :], jnp.zeros((1, SWA_BLOCK, LANE), F32)], axis=0)
        return (cur + nxt).reshape(t, LANE).astype(BF16)

    dsk, dsv = band_add(dkp), band_add(dvp)
    gs["rel_bias_l"] = jnp.einsum("hts,tsb->bh", dbias.reshape(8, SWA_BLOCK, 2 * SWA_BLOCK), sv["onehot"],
                                  precision=lax.Precision.HIGHEST)
    gs["sink"] = dsink[:, :, 0].reshape(8)
    dfq, dfk, dfv, dck, dcq_row = _fox_bwd(sv["proj"], sv["c_col"], sv["c_row"], sv["lse_fox"], sv["ys"][1], dys[1],
                                           tag + "fox_bwd")
    dcq = jnp.pad(dcq_row.reshape(8, t).T, ((0, 0), (0, LANE - 8))).reshape(1, t, LANE)
    dfg, dfb = _logf_cumsum_bwd(sv["proj"], sv["fbias_row"], [dck, dcq], tag + "logf_cumsum")
    gs["forget_bias"] = dfb[0, :8]
    dcb, dcc, dcu, dconv = _conv_bwd(sv["proj"], wt["conv_w"][l], dys[0], tag + "conv")
    gw["conv_w"] = dconv[:3]
    dproj = jnp.concatenate([dgate, dcb, dcc, dcu, dfq, dfk, dfv, dsq, dsk, dsv, dfg], axis=1)
    gw["w_in"] = _unperm_in(_matmul(sv["h"], dproj, "tn", F32, tag + "dw_in"))
    g_mix = sm["mix_norm_g"][l] if late_hook is None else sm["mix_norm_g"][l] + late_hook(gw)
    dh = _matmul(dproj, wt["w_in"][l], "nt", F32, tag + "d_h")
    dx, dx_b, gs["mix_norm_g"] = _rms_bwd(sv["x"], g_mix, dh, dx1, tag + "mix_norm")
    return dx, dx_b, gw, gs


def kernel(x, mem, mix_norm_g, w_in, forget_bias, conv_w, sink, w_branch, w_mix_out, rel_bias, xattn_norm_g, mem_norm_g, w_xq, w_xkv, w_xo, ffn_norm_g, w_ffn_gate, w_ffn_up, w_ffn_down, final_norm_g, loss_target, m_mix_norm_g, m_w_in, m_forget_bias, m_conv_w, m_sink, m_w_branch, m_w_mix_out, m_rel_bias, m_xattn_norm_g, m_mem_norm_g, m_w_xq, m_w_xkv, m_w_xo, m_ffn_norm_g, m_w_ffn_gate, m_w_ffn_up, m_w_ffn_down, m_final_norm_g, v_mix_norm_g, v_w_in, v_forget_bias, v_conv_w, v_sink, v_w_branch, v_w_mix_out, v_rel_bias, v_xattn_norm_g, v_mem_norm_g, v_w_xq, v_w_xkv, v_w_xo, v_ffn_norm_g, v_w_ffn_gate, v_w_ffn_up, v_w_ffn_down, v_final_norm_g):
    args = dict(locals())
    names = [n for n, _ in MATRIX_WEIGHTS] + list(SMALL_PARAMS)
    w = {n: args[n] for n in names}
    mo = {n: args["m_" + n] for n in names}
    vo = {n: args["v_" + n] for n in names}
    x2d, mem2d, tgt = x[0], mem[0], loss_target[0]

    wire = {n: (F32 if n == "conv_w" else BF16) for n, _ in MATRIX_WEIGHTS}
    late = ("w_in", "conv_w")
    early_w = [(n, ax) for n, ax in MATRIX_WEIGHTS if n not in late]
    late_w = [(n, ax) for n, ax in MATRIX_WEIGHTS if n in late]
    wt = {n: [None] * DEPTH for n, _ in MATRIX_WEIGHTS}
    wt["w_gu"] = [None] * DEPTH

    ff_pad = FF_SHARD_P - FF_SHARD
    ff_axis = {"w_ffn_gate": 2, "w_ffn_up": 2, "w_ffn_down": 1}

    def pad_ffn(n, blocks):
        if n not in ff_axis:
            return blocks
        return jnp.pad(blocks, [(0, ff_pad if d == ff_axis[n] else 0) for d in range(blocks.ndim)])

    def unpad_ffn(n, blocks):
        return lax.slice_in_dim(blocks, 0, FF_SHARD, axis=ff_axis[n]) if n in ff_axis else blocks

    def place_weights(l, which, gathered):
        for (n, ax), g in zip(which, gathered):
            wt[n][l] = _to_full(pad_ffn(n, g), ax - 1)
        if "w_in" in dict(which):
            wt["w_in"][l] = _perm_in(wt["w_in"][l])
        if "w_ffn_gate" in dict(which):
            wt["w_gu"][l] = jnp.concatenate([wt["w_ffn_gate"][l], wt["w_ffn_up"][l]], axis=1)

    def shards_of(l, which):
        return [w[n][l].astype(wire[n]) for n, _ in which]

    def start_gather(srcs, name):
        lands = _own_slots(srcs, [lax.empty((N_DEV,) + s.shape, s.dtype) for s in srcs], None, True)
        return _swap_start(srcs, lands, None, True, name)

    place_weights(0, late_w, _all_gather(shards_of(0, late_w), "weights_gather_l0_first"))
    r_send, r_recv, r_srcs, r_lands, token = start_gather(shards_of(0, early_w), "weights_gather_l0_rest_start")
    shards1 = shards_of(1, MATRIX_WEIGHTS)
    shards1[0] = shards1[0] + token[0, 0].astype(shards1[0].dtype)
    w_send, w_recv, w_srcs, lands, token = start_gather(shards1, "weights_gather_l1_start")
    sm = {n: w[n] for n in SMALL_PARAMS}
    sm["mix_norm_g"] = w["mix_norm_g"].at[0].add(token[0, 0])
    sm["swa_tables"] = _swa_tables(w["rel_bias"])

    def rest_of_layer0(after):
        place_weights(0, early_w, _swap_wait(r_send, r_recv, r_srcs, r_lands, after, None, True,
                                             "weights_gather_l0_rest_wait"))

    saved = []
    xc = x2d
    for l in range(DEPTH):
        if l == 1:
            place_weights(1, MATRIX_WEIGHTS,
                          _swap_wait(w_send, w_recv, w_srcs, lands, xc, None, True, "weights_gather_l1_wait"))
        xc, sv = _layer_fwd(l, xc, mem2d, wt, sm, rest_of_layer0 if l == 0 else None)
        saved.append(sv)
    loss_row, dx, dx_b, dg_final = _loss_head(xc, sm["final_norm_g"], tgt, "loss_head")

    def grad_parts(gw, which):
        return [unpad_ffn(n, _to_blocks(gw[n], ax - 1)).astype(wire[n]) for n, ax in which]

    gw_all, gs_all = [None] * DEPTH, [None] * DEPTH
    dx, dx_b, gw_all[1], gs_all[1] = _layer_bwd(1, dx, dx_b, mem2d, wt, sm, saved[1])
    parts1 = grad_parts(gw_all[1], MATRIX_WEIGHTS)
    zones = _own_slots(parts1, [lax.empty((N_DEV, DEPTH) + p.shape[1:], p.dtype) for p in parts1], 1, False)
    g_send, g_recv, g_srcs, zones, token = _swap_start(parts1, zones, 1, False, "grads_exchange_l1_start")
    sm_b = dict(sm)
    sm_b["ffn_norm_g"] = sm["ffn_norm_g"].at[0].add(token[0, 0])
    mid = {}

    def mid_hook(gw):
        zone = dict(zip([n for n, _ in MATRIX_WEIGHTS],
                        _swap_wait(g_send, g_recv, g_srcs, zones, gw["w_mix_out"], 1, False, "grads_exchange_l1_wait")))
        parts0 = grad_parts(gw, early_w)
        early_zones = _own_slots(parts0, [zone[n] for n, _ in early_w], 0, False)
        mid["early"] = _swap_start(parts0, early_zones, 0, False, "grads_exchange_l0_early_start")
        mid["late_zones"] = [zone[n] for n, _ in late_w]
        return mid["early"][4][0, 0]

    def late_hook(gw):
        parts0 = grad_parts(gw, late_w)
        late_zones = _own_slots(parts0, mid["late_zones"], 0, False)
        mid["late"] = _swap_start(parts0, late_zones, 0, False, "grads_exchange_l0_late_start")
        return mid["late"][4][0, 0]

    dx, dx_b, gw_all[0], gs_all[0] = _layer_bwd(0, dx, dx_b, mem2d, wt, sm_b, saved[0], mid_hook, late_hook)
    grad_x = dx[None]
    recv_by_name = {}
    for key, which in (("early", early_w), ("late", late_w)):
        s_send, s_recv, s_srcs, s_zones, _ = mid[key]
        recv_by_name.update(zip([n for n, _ in which],
                                _swap_wait(s_send, s_recv, s_srcs, s_zones, dx, 0, False,
                                           "grads_exchange_l0_" + key + "_wait")))
    recv = [recv_by_name[n] for n, _ in MATRIX_WEIGHTS]

    outs = {}
    for (n, _), r in zip(MATRIX_WEIGHTS, recv):
        res = _adamw(r.reshape((N_DEV,) + _rows(w[n]).shape), _rows(w[n]), _rows(mo[n]), _rows(vo[n]), "adamw_" + n)
        outs[n] = [o.reshape(w[n].shape) for o in res]

    gsm = {n: jnp.stack([gs_all[l][n] for l in range(DEPTH)])
           for n in ("mix_norm_g", "xattn_norm_g", "mem_norm_g", "ffn_norm_g", "forget_bias", "sink")}
    gsm["final_norm_g"] = dg_final
    gsm["rel_bias"] = gs_all[0]["rel_bias_l"] + gs_all[1]["rel_bias_l"]
    zero = jnp.zeros((1,), F32)
    (small_parts,) = _all_gather([_pack_small([gsm[n] for n in SMALL_PARAMS] + [loss_row[0, :1]])],
                                 "small_grads_all_gather")
    outs_small = _adamw(small_parts, *[_pack_small([d[n] for n in SMALL_PARAMS] + [zero]) for d in (w, mo, vo)],
                        "adamw_small")
    for kind in range(4):
        flat, o = outs_small[kind].reshape(-1), 0
        for n in SMALL_PARAMS:
            sz = int(np.prod(w[n].shape))
            outs.setdefault(n, []).append(flat[o:o + sz].reshape(w[n].shape))
            o += sz
        if kind == 0:
            loss = flat[o]

    order = ["mix_norm_g", "w_in", "forget_bias", "conv_w", "sink", "w_branch", "w_mix_out", "rel_bias",
             "xattn_norm_g", "mem_norm_g", "w_xq", "w_xkv", "w_xo", "ffn_norm_g", "w_ffn_gate", "w_ffn_up",
             "w_ffn_down", "final_norm_g"]
    result = [loss, grad_x]
    for kind in range(4):
        result += [outs[n][kind] for n in order]
    return tuple(result)
```

```python
import math

import numpy as np
import jax
import jax.numpy as jnp
from jax import lax
from jax.experimental import pallas as pl
from jax.experimental.pallas import tpu as pltpu

F32 = jnp.float32
BF16 = jnp.bfloat16
MESH = pl.DeviceIdType.MESH

LANE = 128
BF16_SUBLANE = 16
V7X_VMEM_REQUEST_CAP = 56 * 2 ** 20
N_DEV = 8

D_MODEL = 1024
DEPTH = 2
HEAD = 64
BRANCH = 512
SWA_BLOCK = 128
SWA_GROUP = 4
N_BUCKETS = 32
X_HEADS = 4
X_HEAD = 256
D_FF = 2816
FF_SHARD = D_FF // N_DEV
FF_SHARD_P = -(-FF_SHARD // LANE) * LANE
D_FF_P = N_DEV * FF_SHARD_P
RMS_EPS = 1e-6
NEG = -1e30
ADAM_LR, ADAM_B1, ADAM_B2, ADAM_EPS, ADAM_WD, ADAM_STEP = 0.001, 0.9, 0.999, 1e-08, 0.01, 10

IN_COLS = 6920
PROJ_COLS = 7040
COL_GATE, COL_CONV, COL_FOX, COL_SQ, COL_SK, COL_SV, COL_FG = 0, 3072, 4608, 6144, 6656, 6784, 6912

ROW_TILE = 512
FOX_TILE = 1024
MM_TM, MM_TN, MM_TK = 2048, 1536, 2048


def _pick(n, cap, mult):
    best = None
    for d in range(mult, min(n, cap) + 1, mult):
        if n % d == 0:
            best = d
    return n if best is None else best


def _params(semantics, block_bytes):
    limit = int(min(max(2 * block_bytes + (8 << 20), 24 << 20), V7X_VMEM_REQUEST_CAP))
    return pltpu.CompilerParams(dimension_semantics=semantics, vmem_limit_bytes=limit)


def _nbytes(shape, dtype):
    return int(np.prod(shape)) * jnp.dtype(dtype).itemsize


def _dot(a, b, dims):
    return lax.dot_general(a, b, (dims, ((), ())), preferred_element_type=F32)


NN = ((1,), (0,))
NT = ((1,), (1,))
TN = ((0,), (0,))


def _matmul(a, b, mode, out_dtype, name, residual=None):
    if mode == "nn":
        (m, k), (k2, n) = a.shape, b.shape
    elif mode == "nt":
        (m, k), (n, k2) = a.shape, b.shape
    else:
        (k, m), (k2, n) = a.shape, b.shape
    assert k == k2, (name, a.shape, b.shape)
    tn, tk = _pick(n, MM_TN, LANE), _pick(k, MM_TK, LANE)
    nk = k // tk
    dims = {"nn": NN, "nt": NT, "tn": TN}[mode]
    has_res = residual is not None
    tall = nk == 1 and not has_res and n > 2 * MM_TN
    tm = _pick(m, MM_TM if tall else MM_TM // 2, LANE)

    def body(*refs):
        a_ref, b_ref = refs[0], refs[1]
        r_ref = refs[2] if has_res else None
        o_ref = refs[3] if has_res else refs[2]
        kk = pl.program_id(2)
        p = _dot(a_ref[...].astype(BF16), b_ref[...].astype(BF16), dims)
        if nk == 1:
            if has_res:
                p = p + r_ref[...]
            o_ref[...] = p.astype(out_dtype)
        else:
            acc_ref = refs[-1]

            @pl.when(kk == 0)
            def _():
                acc_ref[...] = p

            @pl.when(kk > 0)
            def _():
                acc_ref[...] += p

            @pl.when(kk == nk - 1)
            def _():
                res = acc_ref[...]
                if has_res:
                    res = res + r_ref[...]
                o_ref[...] = res.astype(out_dtype)

    if mode == "nn":
        a_spec = pl.BlockSpec((tm, tk), lambda i, j, kk: (i, kk))
        b_spec = pl.BlockSpec((tk, tn), lambda i, j, kk: (kk, j))
    elif mode == "nt":
        a_spec = pl.BlockSpec((tm, tk), lambda i, j, kk: (i, kk))
        b_spec = pl.BlockSpec((tn, tk), lambda i, j, kk: (j, kk))
    else:
        a_spec = pl.BlockSpec((tk, tm), lambda i, j, kk: (kk, i))
        b_spec = pl.BlockSpec((tk, tn), lambda i, j, kk: (kk, j))
    o_spec = pl.BlockSpec((tm, tn), lambda i, j, kk: (i, j))
    in_specs, args = [a_spec, b_spec], [a, b]
    if has_res:
        in_specs.append(o_spec)
        args.append(residual)
    blk = (_nbytes((tm, tk), a.dtype) + _nbytes((tk, tn), b.dtype) + _nbytes((tm, tn), out_dtype)
           + (_nbytes((tm, tn), F32) if has_res else 0))
    scratch = [pltpu.VMEM((tm, tn), F32)] if nk > 1 else []
    return pl.pallas_call(
        body, name=name, grid=(m // tm, n // tn, nk),
        out_shape=jax.ShapeDtypeStruct((m, n), out_dtype),
        in_specs=in_specs, out_specs=o_spec, scratch_shapes=scratch,
        compiler_params=_params(("parallel", "parallel", "arbitrary"), blk + _nbytes((tm, tn), F32)),
    )(*args)


def _rms_fwd(x, g, name):
    t, d = x.shape
    tr = _pick(t, ROW_TILE, BF16_SUBLANE)

    def body(x_ref, g_ref, y_ref):
        xv = x_ref[...]
        r = lax.rsqrt(jnp.mean(xv * xv, axis=-1, keepdims=True) + RMS_EPS)
        y_ref[...] = ((xv * r) * g_ref[...]).astype(BF16)

    return pl.pallas_call(
        body, name=name, grid=(t // tr,),
        out_shape=jax.ShapeDtypeStruct((t, d), BF16),
        in_specs=[pl.BlockSpec((tr, d), lambda i: (i, 0)), pl.BlockSpec((1, d), lambda i: (0, 0))],
        out_specs=pl.BlockSpec((tr, d), lambda i: (i, 0)),
        compiler_params=_params(("parallel",), 2 * _nbytes((tr, d), F32)),
    )(x, g.reshape(1, d))


def _rms_bwd(x, g, dy, dres, name):
    t, d = x.shape
    tr = _pick(t, ROW_TILE, BF16_SUBLANE)
    has_res = dres is not None

    def body(*refs):
        x_ref, g_ref, dy_ref = refs[:3]
        r_ref = refs[3] if has_res else None
        dx_ref, dxb_ref, dg_ref = refs[-3:]
        i = pl.program_id(0)
        xv = x_ref[...]
        r = lax.rsqrt(jnp.mean(xv * xv, axis=-1, keepdims=True) + RMS_EPS)
        xh = xv * r
        dyv = dy_ref[...].astype(F32)
        dxh = dyv * g_ref[...]
        dx = r * (dxh - xh * jnp.mean(dxh * xh, axis=-1, keepdims=True))
        if has_res:
            dx = dx + r_ref[...]
        dx_ref[...] = dx
        dxb_ref[...] = dx.astype(BF16)

        @pl.when(i == 0)
        def _():
            dg_ref[...] = jnp.zeros_like(dg_ref)

        dg_ref[...] += jnp.sum(dyv * xh, axis=0, keepdims=True)

    row = pl.BlockSpec((tr, d), lambda i: (i, 0))
    vec = pl.BlockSpec((1, d), lambda i: (0, 0))
    in_specs, args = [row, vec, row], [x, g.reshape(1, d), dy]
    if has_res:
        in_specs.append(row)
        args.append(dres)
    return pl.pallas_call(
        body, name=name, grid=(t // tr,),
        out_shape=(jax.ShapeDtypeStruct((t, d), F32), jax.ShapeDtypeStruct((t, d), BF16),
                   jax.ShapeDtypeStruct((1, d), F32)),
        in_specs=in_specs, out_specs=(row, row, vec),
        compiler_params=_params(("arbitrary",), 5 * _nbytes((tr, d), F32)),
    )(*args)


def _loss_head(x, g, target, name):
    t, d = x.shape
    tr = _pick(t, ROW_TILE, BF16_SUBLANE)

    def body(x_ref, g_ref, t_ref, loss_ref, dx_ref, dxb_ref, dg_ref):
        i = pl.program_id(0)
        xv = x_ref[...]
        gv = g_ref[...]
        r = lax.rsqrt(jnp.mean(xv * xv, axis=-1, keepdims=True) + RMS_EPS)
        xh = xv * r
        diff = xh * gv - t_ref[...]
        part = 0.5 * jnp.sum(jnp.mean(diff * diff, axis=-1, keepdims=True), axis=0, keepdims=True)
        dyv = diff * (1.0 / d)
        dxh = dyv * gv
        dx = r * (dxh - xh * jnp.mean(dxh * xh, axis=-1, keepdims=True))
        dx_ref[...] = dx
        dxb_ref[...] = dx.astype(BF16)

        @pl.when(i == 0)
        def _():
            dg_ref[...] = jnp.zeros_like(dg_ref)
            loss_ref[...] = jnp.zeros_like(loss_ref)

        dg_ref[...] += jnp.sum(dyv * xh, axis=0, keepdims=True)
        loss_ref[...] += jnp.broadcast_to(part, loss_ref.shape)

    row = pl.BlockSpec((tr, d), lambda i: (i, 0))
    vec = pl.BlockSpec((1, d), lambda i: (0, 0))
    return pl.pallas_call(
        body, name=name, grid=(t // tr,),
        out_shape=(jax.ShapeDtypeStruct((1, LANE), F32), jax.ShapeDtypeStruct((t, d), F32),
                   jax.ShapeDtypeStruct((t, d), BF16), jax.ShapeDtypeStruct((1, d), F32)),
        in_specs=[row, vec, row],
        out_specs=(pl.BlockSpec((1, LANE), lambda i: (0, 0)), row, row, vec),
        compiler_params=_params(("arbitrary",), 5 * _nbytes((tr, d), F32)),
    )(x, g.reshape(1, d), target)


HALO = 8


def _conv_fwd(proj, conv_w, name):
    t = proj.shape[0]
    tr = _pick(t, ROW_TILE, BF16_SUBLANE)
    c0 = COL_CONV // BRANCH
    hb = tr // HALO

    def body(cb_ref, cc_ref, cu_ref, hc_ref, hu_ref, w_ref, y_ref):
        i = pl.program_id(0)
        z = cc_ref[...] * cu_ref[...]
        hz = jnp.where(i > 0, hc_ref[...] * hu_ref[...], 0.0)
        zf = jnp.concatenate([hz, z], axis=0)
        z1 = pltpu.roll(zf, 1, 0)[HALO:]
        z2 = pltpu.roll(zf, 2, 0)[HALO:]
        y = w_ref[2:3, :] * z + w_ref[1:2, :] * z1 + w_ref[0:1, :] * z2
        y_ref[...] = (cb_ref[...] * y).astype(BF16)

    def col(c):
        return pl.BlockSpec((tr, BRANCH), lambda i, c=c: (i, c0 + c))

    def prev(c):
        return pl.BlockSpec((HALO, BRANCH), lambda i, c=c: (jnp.maximum(i * hb - 1, 0), c0 + c))

    return pl.pallas_call(
        body, name=name, grid=(t // tr,),
        out_shape=jax.ShapeDtypeStruct((t, BRANCH), BF16),
        in_specs=[col(0), col(1), col(2), prev(1), prev(2), pl.BlockSpec((3, BRANCH), lambda i: (0, 0))],
        out_specs=pl.BlockSpec((tr, BRANCH), lambda i: (i, 0)),
        compiler_params=_params(("parallel",), 6 * _nbytes((tr, BRANCH), F32)),
    )(proj, proj, proj, proj, proj, conv_w)


def _conv_bwd(proj, conv_w, dout, name):
    t = proj.shape[0]
    tr = _pick(t, ROW_TILE, BF16_SUBLANE)
    nblk = t // tr
    c0 = COL_CONV // BRANCH
    hb = tr // HALO
    last_halo = t // HALO - 1

    def body(cb_ref, cc_ref, cu_ref, hc_ref, hu_ref, do_ref, ndo_ref, ncb_ref, w_ref,
             dcb_ref, dcc_ref, dcu_ref, dw_ref):
        i = pl.program_id(0)
        cb, cc, cu = cb_ref[...], cc_ref[...], cu_ref[...]
        w0, w1, w2 = w_ref[0:1, :], w_ref[1:2, :], w_ref[2:3, :]
        z = cc * cu
        hz = jnp.where(i > 0, hc_ref[...] * hu_ref[...], 0.0)
        zf = jnp.concatenate([hz, z], axis=0)
        z1 = pltpu.roll(zf, 1, 0)[HALO:]
        z2 = pltpu.roll(zf, 2, 0)[HALO:]
        y = w2 * z + w1 * z1 + w0 * z2
        dout_v = do_ref[...]
        dyc = dout_v * cb
        hdy = jnp.where(i < nblk - 1, ndo_ref[...] * ncb_ref[...], 0.0)
        dyf = jnp.concatenate([dyc, hdy], axis=0)
        dy1 = pltpu.roll(dyf, tr + HALO - 1, 0)[:tr]
        dy2 = pltpu.roll(dyf, tr + HALO - 2, 0)[:tr]
        dz = w2 * dyc + w1 * dy1 + w0 * dy2
        dcb_ref[...] = (dout_v * y).astype(BF16)
        dcc_ref[...] = (dz * cu).astype(BF16)
        dcu_ref[...] = (dz * cc).astype(BF16)

        @pl.when(i == 0)
        def _():
            dw_ref[...] = jnp.zeros_like(dw_ref)

        dw_ref[0:1, :] += jnp.sum(dyc * z2, axis=0, keepdims=True)
        dw_ref[1:2, :] += jnp.sum(dyc * z1, axis=0, keepdims=True)
        dw_ref[2:3, :] += jnp.sum(dyc * z, axis=0, keepdims=True)

    def col(c):
        return pl.BlockSpec((tr, BRANCH), lambda i, c=c: (i, c0 + c))

    def prev(c):
        return pl.BlockSpec((HALO, BRANCH), lambda i, c=c: (jnp.maximum(i * hb - 1, 0), c0 + c))

    def nxt(c):
        return pl.BlockSpec((HALO, BRANCH), lambda i, c=c: (jnp.minimum((i + 1) * hb, last_halo), c))

    row = pl.BlockSpec((tr, BRANCH), lambda i: (i, 0))
    return pl.pallas_call(
        body, name=name, grid=(nblk,),
        out_shape=(jax.ShapeDtypeStruct((t, BRANCH), BF16),) * 3 + (jax.ShapeDtypeStruct((HALO, BRANCH), F32),),
        in_specs=[col(0), col(1), col(2), prev(1), prev(2), row, nxt(0), nxt(c0),
                  pl.BlockSpec((3, BRANCH), lambda i: (0, 0))],
        out_specs=(row, row, row, pl.BlockSpec((HALO, BRANCH), lambda i: (0, 0))),
        compiler_params=_params(("arbitrary",), 8 * _nbytes((tr, BRANCH), F32)),
    )(proj, proj, proj, proj, proj, dout, dout, proj, conv_w)


def _tri(lower):
    r = lax.broadcasted_iota(jnp.int32, (LANE, LANE), 0)
    c = lax.broadcasted_iota(jnp.int32, (LANE, LANE), 1)
    return jnp.where((c <= r) if lower else (c >= r), 1.0, 0.0).astype(F32)


def _logf_cumsum(proj, fbias_row, name):
    t = proj.shape[0]
    nchunk = t // LANE

    def body(f_ref, b_ref, c_ref, run_sc):
        tri = _tri(True)
        run_sc[...] = jnp.zeros_like(run_sc)

        @pl.loop(0, nchunk)
        def _(i):
            rows = pl.ds(pl.multiple_of(i * LANE, LANE), LANE)
            z = f_ref[rows, :] + b_ref[...]
            logf = jnp.minimum(z, 0.0) - jnp.log(1.0 + jnp.exp(-jnp.abs(z)))
            cs = lax.dot_general(tri, logf, (NN, ((), ())), precision=lax.Precision.HIGHEST,
                                 preferred_element_type=F32) + run_sc[0:1, :]
            c_ref[rows, :] = cs
            run_sc[0:1, :] = cs[LANE - 1:LANE, :]

    return pl.pallas_call(
        body, name=name, grid=(1,),
        out_shape=jax.ShapeDtypeStruct((t, LANE), F32),
        in_specs=[pl.BlockSpec((t, LANE), lambda i: (0, COL_FG // LANE)), pl.BlockSpec((1, LANE), lambda i: (0, 0))],
        out_specs=pl.BlockSpec((t, LANE), lambda i: (0, 0)),
        scratch_shapes=[pltpu.VMEM((8, LANE), F32)],
        compiler_params=_params(("arbitrary",), 2 * _nbytes((t, LANE), F32)),
    )(proj, fbias_row)


def _logf_cumsum_bwd(proj, fbias_row, pieces, name):
    t = proj.shape[0]
    tb = _pick(t, 2 * ROW_TILE, LANE)
    nblk = t // tb
    npiece = len(pieces)

    def body(*refs):
        f_ref, b_ref = refs[:2]
        piece_refs = refs[2:2 + npiece]
        df_ref, db_ref, run_sc = refs[2 + npiece:]
        i = pl.program_id(0)
        tri = _tri(False)

        @pl.when(i == 0)
        def _():
            run_sc[...] = jnp.zeros_like(run_sc)
            db_ref[...] = jnp.zeros_like(db_ref)

        for c in reversed(range(tb // LANE)):
            rows = slice(c * LANE, (c + 1) * LANE)
            slabs = [p_ref[n, rows, :] for p_ref in piece_refs for n in range(p_ref.shape[0])]
            dcc = slabs[0]
            for slab in slabs[1:]:
                dcc = dcc + slab
            ss = lax.dot_general(tri, dcc, (NN, ((), ())), precision=lax.Precision.HIGHEST,
                                 preferred_element_type=F32) + run_sc[0:1, :]
            z = f_ref[rows, :] + b_ref[...]
            dz = ss * (1.0 / (1.0 + jnp.exp(z)))
            df_ref[rows, :] = dz.astype(BF16)
            run_sc[0:1, :] = ss[0:1, :]
            db_ref[...] += jnp.sum(dz, axis=0, keepdims=True)

    piece_specs = [pl.BlockSpec((p.shape[0], tb, LANE), lambda i: (0, nblk - 1 - i, 0)) for p in pieces]
    nslab = sum(p.shape[0] for p in pieces)
    return pl.pallas_call(
        body, name=name, grid=(nblk,),
        out_shape=(jax.ShapeDtypeStruct((t, LANE), BF16), jax.ShapeDtypeStruct((1, LANE), F32)),
        in_specs=[pl.BlockSpec((tb, LANE), lambda i: (nblk - 1 - i, COL_FG // LANE)),
                  pl.BlockSpec((1, LANE), lambda i: (0, 0))] + piece_specs,
        out_specs=(pl.BlockSpec((tb, LANE), lambda i: (nblk - 1 - i, 0)), pl.BlockSpec((1, LANE), lambda i: (0, 0))),
        scratch_shapes=[pltpu.VMEM((8, LANE), F32)],
        compiler_params=_params(("arbitrary",), (4 + nslab) * _nbytes((tb, LANE), F32)),
    )(proj, fbias_row, *pieces)


def _lo_mask():
    return lax.broadcasted_iota(jnp.int32, (1, LANE), 1) < HEAD


def _causal_steps(n, key_major):
    if key_major:
        pairs = [(iq, ik) for ik in range(n) for iq in range(ik, n)]
    else:
        pairs = [(iq, ik) for iq in range(n) for ik in range(iq + 1)]
    return (jnp.asarray([p[0] for p in pairs], jnp.int32), jnp.asarray([p[1] for p in pairs], jnp.int32))


def _head_lanes(j, pair_vals):
    lane = lax.broadcasted_iota(jnp.int32, (1, LANE), 1)
    return jnp.where(lane == 2 * j, pair_vals[0], 0.0) + jnp.where(lane == 2 * j + 1, pair_vals[1], 0.0)


def _fox_fwd(proj, c_col, c_row, name):
    t = proj.shape[0]
    tq = _pick(t, FOX_TILE, LANE)
    nq = t // tq
    rep = tq // LANE
    scale = HEAD ** -0.5
    cq, ck, cv = COL_FOX // LANE, COL_FOX // LANE + 4, COL_FOX // LANE + 8
    q_tab, k_tab = _causal_steps(nq, False)

    def body(qt_ref, kt_ref, q_ref, k_ref, v_ref, ck_ref, cqr_ref, y_ref, lse_ref, m_sc, l_sc, acc_sc):
        step_id = pl.program_id(1)
        iq, ik = qt_ref[step_id], kt_ref[step_id]
        lo = _lo_mask()
        lo_rows = lax.broadcasted_iota(jnp.int32, (LANE, 1), 0) < HEAD

        @pl.when(ik == 0)
        def _():
            m_sc[...] = jnp.full(m_sc.shape, NEG, F32)
            l_sc[...] = jnp.zeros_like(l_sc)
            acc_sc[...] = jnp.zeros_like(acc_sc)

        def step(diag):
            q2 = (q_ref[...] * scale).astype(BF16)
            k2 = k_ref[...].astype(BF16)
            v2 = v_ref[...].astype(BF16)
            alphas, adds = [], []
            for h in range(2):
                msk = lo if h == 0 else jnp.logical_not(lo)
                kh = jnp.where(msk, k2, jnp.zeros_like(k2))
                vh = jnp.where(msk, v2, jnp.zeros_like(v2))
                st = _dot(kh, q2, NT) + cqr_ref[h] - jnp.tile(ck_ref[h], (1, rep))
                if diag:
                    krow = lax.broadcasted_iota(jnp.int32, (tq, tq), 0)
                    qcol = lax.broadcasted_iota(jnp.int32, (tq, tq), 1)
                    st = jnp.where(krow <= qcol, st, NEG)
                m_prev = m_sc[h]
                m_new = jnp.maximum(m_prev, jnp.max(st, axis=0, keepdims=True))
                alpha = jnp.exp(m_prev - m_new)
                pt = jnp.exp(st - m_new)
                l_sc[h] = alpha * l_sc[h] + jnp.sum(pt, axis=0, keepdims=True)
                m_sc[h] = m_new
                alphas.append(alpha)
                adds.append(_dot(vh, pt.astype(BF16), TN))
            acc_sc[...] = acc_sc[...] * jnp.where(lo_rows, alphas[0], alphas[1]) + (adds[0] + adds[1])

        @pl.when(ik < iq)
        def _():
            step(False)

        @pl.when(ik == iq)
        def _():
            step(True)
            yt = acc_sc[...] / jnp.where(lo_rows, l_sc[0], l_sc[1])
            y_ref[...] = yt.T.astype(BF16)
            lse_ref[...] = m_sc[...] + jnp.log(l_sc[...])

    def kv(c):
        return pl.BlockSpec((tq, LANE), lambda j, s, qt, kt, c=c: (kt[s], c + j))

    qrow = pl.BlockSpec((2, 1, tq), lambda j, s, qt, kt: (j, 0, qt[s]))
    grid_spec = pltpu.PrefetchScalarGridSpec(
        num_scalar_prefetch=2, grid=(4, int(q_tab.shape[0])),
        in_specs=[pl.BlockSpec((tq, LANE), lambda j, s, qt, kt: (qt[s], cq + j)), kv(ck), kv(cv),
                  pl.BlockSpec((2, tq, LANE), lambda j, s, qt, kt: (j, kt[s], 0)), qrow],
        out_specs=(pl.BlockSpec((tq, LANE), lambda j, s, qt, kt: (qt[s], j)), qrow),
        scratch_shapes=[pltpu.VMEM((2, 1, tq), F32), pltpu.VMEM((2, 1, tq), F32), pltpu.VMEM((LANE, tq), F32)])
    return pl.pallas_call(
        body, name=name, grid_spec=grid_spec,
        out_shape=(jax.ShapeDtypeStruct((t, BRANCH), BF16), jax.ShapeDtypeStruct((8, 1, t), F32)),
        compiler_params=_params(("parallel", "arbitrary"),
                                16 * _nbytes((tq, LANE), F32) + 6 * _nbytes((tq, tq), F32)),
    )(q_tab, k_tab, proj, proj, proj, c_col, c_row)


def _fox_bwd(proj, c_col, c_row, lse_row, y, dy, name):
    t = proj.shape[0]
    tb = _pick(t, FOX_TILE, LANE)
    nb = t // tb
    rep = tb // LANE
    scale = HEAD ** -0.5
    cq, ck, cv = COL_FOX // LANE, COL_FOX // LANE + 4, COL_FOX // LANE + 8
    q_tab, k_tab = _causal_steps(nb, True)
    nsteps = int(q_tab.shape[0])

    def body(qt_ref, kt_ref, k_ref, v_ref, q_ref, y_ref, dy_ref, ck_ref, cqr_ref, lser_ref,
             dq_ref, dk_ref, dv_ref, dck_ref, dcq_ref, dk_sc, dv_sc, dc_sc, dqt_sc, dcq_sc, d_sc):
        j, step_id = pl.program_id(0), pl.program_id(1)
        iq, ik = qt_ref[step_id], kt_ref[step_id]
        lo = _lo_mask()

        @pl.when(step_id == 0)
        def _():
            dqt_sc[...] = jnp.zeros_like(dqt_sc)
            dcq_sc[...] = jnp.zeros_like(dcq_sc)

        @pl.when(iq == ik)
        def _():
            dk_sc[...] = jnp.zeros_like(dk_sc)
            dv_sc[...] = jnp.zeros_like(dv_sc)
            dc_sc[...] = jnp.zeros_like(dc_sc)

        @pl.when(ik == 0)
        def _():
            prod = y_ref[...].astype(F32) * dy_ref[...].astype(F32)
            row = lax.broadcasted_iota(jnp.int32, (8, LANE), 0)
            sel = jnp.logical_or(jnp.logical_and(row == 0, lo), jnp.logical_and(row == 1, jnp.logical_not(lo)))
            d_sc[iq] = lax.dot_general(jnp.where(sel, 1.0, 0.0).astype(F32), prod, (NT, ((), ())),
                                       precision=lax.Precision.HIGHEST, preferred_element_type=F32)

        def step(diag):
            k2 = k_ref[...].astype(BF16)
            v2 = v_ref[...].astype(BF16)
            q2 = (q_ref[...] * scale).astype(BF16)
            do2 = dy_ref[...]
            d_rows = d_sc[iq]
            for h in range(2):
                msk = lo if h == 0 else jnp.logical_not(lo)
                kh = jnp.where(msk, k2, jnp.zeros_like(k2))
                vh = jnp.where(msk, v2, jnp.zeros_like(v2))
                st = _dot(kh, q2, NT) + (cqr_ref[h] - lser_ref[h]) - jnp.tile(ck_ref[h], (1, rep))
                if diag:
                    krow = lax.broadcasted_iota(jnp.int32, (tb, tb), 0)
                    qcol = lax.broadcasted_iota(jnp.int32, (tb, tb), 1)
                    st = jnp.where(krow <= qcol, st, NEG)
                pt = jnp.exp(st)
                dpt = _dot(vh, do2, NT)
                dst = pt * (dpt - d_rows[h:h + 1, :])
                dsb = dst.astype(BF16)
                dv_sc[h] += _dot(pt.astype(BF16), do2, NN)
                dk_sc[h] += _dot(dsb, q2, NN)
                dc_sc[h] -= jnp.sum(dst, axis=1, keepdims=True)
                dqt_sc[iq] += _dot(kh, dsb, TN)
                dcq_sc[h, iq] += jnp.sum(dst, axis=0, keepdims=True)

        @pl.when(iq > ik)
        def _():
            step(False)

        @pl.when(iq == ik)
        def _():
            step(True)

        @pl.when(iq == nb - 1)
        def _():
            dk_ref[...] = jnp.where(lo, dk_sc[0], dk_sc[1]).astype(BF16)
            dv_ref[...] = jnp.where(lo, dv_sc[0], dv_sc[1]).astype(BF16)
            dck_ref[...] = _head_lanes(j, dc_sc)

        @pl.when(step_id == nsteps - 1)
        def _():
            for i in range(nb):
                dq_ref[i * tb:(i + 1) * tb, :] = (dqt_sc[i].T * scale).astype(BF16)
                for h in range(2):
                    dcq_ref[h, :, i * tb:(i + 1) * tb] = dcq_sc[h, i]

    def kcol(c):
        return pl.BlockSpec((tb, LANE), lambda j, s, qt, kt, c=c: (kt[s], c + j))

    qrow = pl.BlockSpec((2, 1, tb), lambda j, s, qt, kt: (j, 0, qt[s]))
    pair_q = pl.BlockSpec((tb, LANE), lambda j, s, qt, kt: (qt[s], j))
    pair_k = pl.BlockSpec((tb, LANE), lambda j, s, qt, kt: (kt[s], j))
    grid_spec = pltpu.PrefetchScalarGridSpec(
        num_scalar_prefetch=2, grid=(4, nsteps),
        in_specs=[kcol(ck), kcol(cv), pl.BlockSpec((tb, LANE), lambda j, s, qt, kt: (qt[s], cq + j)), pair_q, pair_q,
                  pl.BlockSpec((2, tb, LANE), lambda j, s, qt, kt: (j, kt[s], 0)), qrow, qrow],
        out_specs=(pl.BlockSpec((t, LANE), lambda j, s, qt, kt: (0, j)), pair_k, pair_k,
                   pl.BlockSpec((None, tb, LANE), lambda j, s, qt, kt: (j, kt[s], 0)),
                   pl.BlockSpec((2, 1, t), lambda j, s, qt, kt: (j, 0, 0))),
        scratch_shapes=[pltpu.VMEM((2, tb, LANE), F32)] * 3
        + [pltpu.VMEM((nb, LANE, tb), F32), pltpu.VMEM((2, nb, 1, tb), F32), pltpu.VMEM((nb, 8, tb), F32)])
    return pl.pallas_call(
        body, name=name, grid_spec=grid_spec,
        out_shape=(jax.ShapeDtypeStruct((t, BRANCH), BF16), jax.ShapeDtypeStruct((t, BRANCH), BF16),
                   jax.ShapeDtypeStruct((t, BRANCH), BF16), jax.ShapeDtypeStruct((4, t, LANE), F32),
                   jax.ShapeDtypeStruct((8, 1, t), F32)),
        compiler_params=_params(("parallel", "arbitrary"),
                                24 * _nbytes((tb, LANE), F32) + 8 * _nbytes((tb, tb), F32)
                                + 2 * _nbytes((t, LANE), F32)),
    )(q_tab, k_tab, proj, proj, proj, y, dy, c_col, c_row, lse_row)


def _swa_tables(rel_bias):
    tq = np.arange(SWA_BLOCK)[:, None]
    sk = np.arange(2 * SWA_BLOCK)[None, :]
    dist = SWA_BLOCK + tq - sk
    inwin = (dist >= 0) & (dist < SWA_BLOCK)
    n = np.maximum(dist, 0)
    max_exact = N_BUCKETS // 2
    large = max_exact + (np.log(np.maximum(n, 1).astype(np.float32) / max_exact)
                         / math.log(SWA_BLOCK / max_exact) * (N_BUCKETS - max_exact)).astype(np.int32)
    bucket = np.where(n < max_exact, n, np.minimum(large, N_BUCKETS - 1))
    onehot = (bucket[..., None] == np.arange(N_BUCKETS)) & inwin[..., None]
    onehot = jnp.asarray(onehot.astype(np.float32))
    bias = jnp.einsum("tsb,bh->hts", onehot, rel_bias, precision=lax.Precision.HIGHEST)
    bias = jnp.where(jnp.asarray(inwin)[None], bias, NEG)
    return onehot, bias


SWA_BLOCKS_PER_STEP = 8


def _swa_specs(nb, r_blk, csk, csv):
    def prev(c):
        return pl.BlockSpec((SWA_BLOCK, LANE), lambda kvh, n, c=c: (jnp.maximum(n * r_blk - 1, 0), c))

    def cur(c):
        return pl.BlockSpec((r_blk * SWA_BLOCK, LANE), lambda kvh, n, c=c: (n, c))

    return [prev(csk), cur(csk), prev(csv), cur(csv)]


def _swa_fwd(proj, bias, sink_rep, name):
    t = proj.shape[0]
    nb = t // SWA_BLOCK
    r_blk = _pick(nb, SWA_BLOCKS_PER_STEP, 1)
    rows = r_blk * SWA_BLOCK
    scale = HEAD ** -0.5
    csq, csk, csv = COL_SQ // 256, COL_SK // LANE, COL_SV // LANE

    def body(q_ref, kp_ref, kc_ref, vp_ref, vc_ref, b_ref, sk_ref, y_ref, lse_ref):
        kvh, n = pl.program_id(0), pl.program_id(1)
        lane = lax.broadcasted_iota(jnp.int32, (1, LANE), 1)
        lo = lane < HEAD
        kvm = jnp.logical_and(lane >= kvh * HEAD, lane < (kvh + 1) * HEAD)

        def both(prev_ref, cur_ref):
            band = jnp.concatenate([prev_ref[...], cur_ref[...]], axis=0)
            band = jnp.where(kvm, band, 0.0)
            return (band + pltpu.roll(band, HEAD, 1)).astype(BF16)

        kb_all, vb_all = both(kp_ref, kc_ref), both(vp_ref, vc_ref)
        col = lax.broadcasted_iota(jnp.int32, (SWA_BLOCK, 2 * SWA_BLOCK), 1)
        first = jnp.logical_and(n == 0, col < SWA_BLOCK)
        for r in range(r_blk):
            rs = slice(r * SWA_BLOCK, (r + 1) * SWA_BLOCK)
            kb = kb_all[r * SWA_BLOCK:(r + 2) * SWA_BLOCK]
            vb = vb_all[r * SWA_BLOCK:(r + 2) * SWA_BLOCK]
            outs = []
            for g in range(SWA_GROUP):
                half = q_ref[rs, (g // 2) * LANE:(g // 2 + 1) * LANE]
                hm = lo if g % 2 == 0 else jnp.logical_not(lo)
                qg = jnp.where(hm, half, 0.0).astype(BF16)
                s = _dot(qg, kb, NT) * scale + b_ref[g]
                if r == 0:
                    s = jnp.where(first, NEG, s)
                snk = sk_ref[g:g + 1, :]
                m = jnp.maximum(jnp.max(s, axis=1, keepdims=True), snk)
                p = jnp.exp(s - jnp.tile(m, (1, 2)))
                denom = jnp.sum(p, axis=1, keepdims=True) + jnp.exp(snk - m)
                outs.append(_dot(p.astype(BF16), vb, NN) / denom)
                lse_ref[g, rs, :] = m + jnp.log(denom)
            y_ref[rs, 0:LANE] = jnp.where(lo, outs[0], outs[1]).astype(BF16)
            y_ref[rs, LANE:2 * LANE] = jnp.where(lo, outs[2], outs[3]).astype(BF16)

    return pl.pallas_call(
        body, name=name, grid=(2, nb // r_blk),
        out_shape=(jax.ShapeDtypeStruct((t, BRANCH), BF16), jax.ShapeDtypeStruct((8, t, LANE), F32)),
        in_specs=[pl.BlockSpec((rows, 256), lambda kvh, n: (n, csq + kvh))] + _swa_specs(nb, r_blk, csk, csv)
        + [pl.BlockSpec((None, SWA_GROUP, SWA_BLOCK, 256), lambda kvh, n: (kvh, 0, 0, 0)),
           pl.BlockSpec((None, SWA_GROUP, LANE), lambda kvh, n: (kvh, 0, 0))],
        out_specs=(pl.BlockSpec((rows, 256), lambda kvh, n: (n, kvh)),
                   pl.BlockSpec((SWA_GROUP, rows, LANE), lambda kvh, n: (kvh, n, 0))),
        compiler_params=_params(("parallel", "arbitrary"), 8 << 20),
    )(proj, proj, proj, proj, proj, bias.reshape(2, SWA_GROUP, SWA_BLOCK, 256), sink_rep)


def _swa_bwd(proj, bias, sink_rep, lse, y, dy, name):
    t = proj.shape[0]
    nb = t // SWA_BLOCK
    r_blk = _pick(nb, SWA_BLOCKS_PER_STEP, 1)
    rows = r_blk * SWA_BLOCK
    scale = HEAD ** -0.5
    csq, csk, csv = COL_SQ // 256, COL_SK // LANE, COL_SV // LANE

    def body(q_ref, kp_ref, kc_ref, vp_ref, vc_ref, b_ref, sk_ref, lse_ref, y_ref, dy_ref,
             dq_ref, dkp_ref, dvp_ref, db_ref, dsk_ref):
        kvh, n = pl.program_id(0), pl.program_id(1)
        lane = lax.broadcasted_iota(jnp.int32, (1, LANE), 1)
        lo = lane < HEAD
        kvm = jnp.logical_and(lane >= kvh * HEAD, lane < (kvh + 1) * HEAD)

        def both(prev_ref, cur_ref):
            band = jnp.concatenate([prev_ref[...], cur_ref[...]], axis=0)
            band = jnp.where(kvm, band, 0.0)
            return (band + pltpu.roll(band, HEAD, 1)).astype(BF16)

        kb_all, vb_all = both(kp_ref, kc_ref), both(vp_ref, vc_ref)
        col = lax.broadcasted_iota(jnp.int32, (SWA_BLOCK, 2 * SWA_BLOCK), 1)
        first = jnp.logical_and(n == 0, col < SWA_BLOCK)

        @pl.when(n == 0)
        def _():
            db_ref[...] = jnp.zeros_like(db_ref)
            dsk_ref[...] = jnp.zeros_like(dsk_ref)

        for r in range(r_blk):
            rs = slice(r * SWA_BLOCK, (r + 1) * SWA_BLOCK)
            kb = kb_all[r * SWA_BLOCK:(r + 2) * SWA_BLOCK]
            vb = vb_all[r * SWA_BLOCK:(r + 2) * SWA_BLOCK]
            dk_full = jnp.zeros((2 * SWA_BLOCK, LANE), F32)
            dv_full = jnp.zeros((2 * SWA_BLOCK, LANE), F32)
            dqs = []
            for g in range(SWA_GROUP):
                sl = slice((g // 2) * LANE, (g // 2 + 1) * LANE)
                hm = lo if g % 2 == 0 else jnp.logical_not(lo)
                qg = jnp.where(hm, q_ref[rs, sl], 0.0).astype(BF16)
                dog = jnp.where(hm, dy_ref[rs, sl], jnp.zeros((SWA_BLOCK, LANE), BF16))
                dmat = jnp.where(hm, y_ref[rs, sl].astype(F32) * dy_ref[rs, sl].astype(F32), 0.0)
                dg = jnp.sum(dmat, axis=1, keepdims=True)
                s = _dot(qg, kb, NT) * scale + b_ref[g]
                if r == 0:
                    s = jnp.where(first, NEG, s)
                lse_g = lse_ref[g, rs, :]
                p = jnp.exp(s - jnp.tile(lse_g, (1, 2)))
                dp = _dot(dog, vb, NT)
                ds = p * (dp - dg)
                dsb = ds.astype(BF16)
                dqs.append(_dot(dsb, kb, NN) * scale)
                dk_full = dk_full + _dot(dsb, qg, TN)
                dv_full = dv_full + _dot(p.astype(BF16), dog, TN)
                db_ref[g] += ds
                psink = jnp.exp(sk_ref[g:g + 1, :] - lse_g)
                dsk_ref[g:g + 1, :] -= jnp.sum(psink * dg, axis=0, keepdims=True)
            dq_ref[rs, 0:LANE] = jnp.where(lo, dqs[0], dqs[1]).astype(BF16)
            dq_ref[rs, LANE:2 * LANE] = jnp.where(lo, dqs[2], dqs[3]).astype(BF16)
            dkp_ref[r] = jnp.where(kvm, (dk_full + pltpu.roll(dk_full, HEAD, 1)) * scale, 0.0)
            dvp_ref[r] = jnp.where(kvm, dv_full + pltpu.roll(dv_full, HEAD, 1), 0.0)

    qblk = pl.BlockSpec((rows, 256), lambda kvh, n: (n, kvh))
    part = pl.BlockSpec((None, r_blk, 2 * SWA_BLOCK, LANE), lambda kvh, n: (kvh, n, 0, 0))
    bspec = pl.BlockSpec((None, SWA_GROUP, SWA_BLOCK, 256), lambda kvh, n: (kvh, 0, 0, 0))
    sspec = pl.BlockSpec((None, SWA_GROUP, LANE), lambda kvh, n: (kvh, 0, 0))
    return pl.pallas_call(
        body, name=name, grid=(2, nb // r_blk),
        out_shape=(jax.ShapeDtypeStruct((t, BRANCH), BF16),
                   jax.ShapeDtypeStruct((2, nb, 2 * SWA_BLOCK, LANE), F32),
                   jax.ShapeDtypeStruct((2, nb, 2 * SWA_BLOCK, LANE), F32),
                   jax.ShapeDtypeStruct((2, SWA_GROUP, SWA_BLOCK, 256), F32),
                   jax.ShapeDtypeStruct((2, SWA_GROUP, LANE), F32)),
        in_specs=[pl.BlockSpec((rows, 256), lambda kvh, n: (n, csq + kvh))] + _swa_specs(nb, r_blk, csk, csv)
        + [bspec, sspec, pl.BlockSpec((SWA_GROUP, rows, LANE), lambda kvh, n: (kvh, n, 0)), qblk, qblk],
        out_specs=(qblk, part, part, bspec, sspec),
        compiler_params=_params(("parallel", "arbitrary"), 12 << 20),
    )(proj, proj, proj, proj, proj, bias.reshape(2, SWA_GROUP, SWA_BLOCK, 256), sink_rep, lse, y, dy)


def _gate_fwd(proj, pb, name):
    t = proj.shape[0]
    tr = _pick(t, ROW_TILE // 2, BF16_SUBLANE)

    def body(g0, g1, g2, p0, p1, p2, o_ref):
        acc = jax.nn.sigmoid(g0[...]) * p0[...]
        acc = acc + jax.nn.sigmoid(g1[...]) * p1[...]
        acc = acc + jax.nn.sigmoid(g2[...]) * p2[...]
        o_ref[...] = acc.astype(BF16)

    row = pl.BlockSpec((tr, D_MODEL), lambda i: (i, 0))
    gates = [pl.BlockSpec((tr, D_MODEL), lambda i, b=b: (i, b)) for b in range(3)]
    return pl.pallas_call(
        body, name=name, grid=(t // tr,),
        out_shape=jax.ShapeDtypeStruct((t, D_MODEL), BF16),
        in_specs=gates + [row] * 3, out_specs=row,
        compiler_params=_params(("parallel",), 7 * _nbytes((tr, D_MODEL), F32)),
    )(proj, proj, proj, *pb)


def _gate_bwd(proj, pb, dmerged, name):
    t = proj.shape[0]
    tr = _pick(t, ROW_TILE // 2, BF16_SUBLANE)

    def body(g0, g1, g2, p0, p1, p2, dm_ref, dp0, dp1, dp2, dg_ref):
        dm = dm_ref[...]
        for b, (g_ref, p_ref, dp_ref) in enumerate(((g0, p0, dp0), (g1, p1, dp1), (g2, p2, dp2))):
            sg = jax.nn.sigmoid(g_ref[...])
            dp_ref[...] = (dm * sg).astype(BF16)
            dg_ref[:, b * D_MODEL:(b + 1) * D_MODEL] = (dm * p_ref[...] * sg * (1.0 - sg)).astype(BF16)

    row = pl.BlockSpec((tr, D_MODEL), lambda i: (i, 0))
    gates = [pl.BlockSpec((tr, D_MODEL), lambda i, b=b: (i, b)) for b in range(3)]
    return pl.pallas_call(
        body, name=name, grid=(t // tr,),
        out_shape=(jax.ShapeDtypeStruct((t, D_MODEL), BF16),) * 3 + (jax.ShapeDtypeStruct((t, 3 * D_MODEL), BF16),),
        in_specs=gates + [row] * 4,
        out_specs=(row, row, row, pl.BlockSpec((tr, 3 * D_MODEL), lambda i: (i, 0))),
        compiler_params=_params(("parallel",), 11 * _nbytes((tr, D_MODEL), F32)),
    )(proj, proj, proj, *pb, dmerged)


def _swiglu_fwd(ab, name):
    t = ab.shape[0]
    tr = _pick(t, ROW_TILE, BF16_SUBLANE)
    tc = D_FF_P // 2

    def body(a_ref, b_ref, o_ref):
        a = a_ref[...]
        o_ref[...] = (a * jax.nn.sigmoid(a) * b_ref[...]).astype(BF16)

    return pl.pallas_call(
        body, name=name, grid=(t // tr, 2),
        out_shape=jax.ShapeDtypeStruct((t, D_FF_P), BF16),
        in_specs=[pl.BlockSpec((tr, tc), lambda i, j: (i, j)), pl.BlockSpec((tr, tc), lambda i, j: (i, j + 2))],
        out_specs=pl.BlockSpec((tr, tc), lambda i, j: (i, j)),
        compiler_params=_params(("parallel", "parallel"), 3 * _nbytes((tr, tc), F32)),
    )(ab, ab)


def _swiglu_bwd(ab, dh, name):
    t = ab.shape[0]
    tr = _pick(t, ROW_TILE, BF16_SUBLANE)
    tc = D_FF_P // 2

    def body(a_ref, b_ref, dh_ref, da_ref, db_ref):
        a, b, d = a_ref[...], b_ref[...], dh_ref[...]
        sg = jax.nn.sigmoid(a)
        da_ref[...] = (d * b * (sg * (1.0 + a * (1.0 - sg)))).astype(BF16)
        db_ref[...] = (d * (a * sg)).astype(BF16)

    blk = pl.BlockSpec((tr, tc), lambda i, j: (i, j))
    return pl.pallas_call(
        body, name=name, grid=(t // tr, 2),
        out_shape=(jax.ShapeDtypeStruct((t, D_FF_P), BF16),) * 2,
        in_specs=[blk, pl.BlockSpec((tr, tc), lambda i, j: (i, j + 2)), blk],
        out_specs=(blk, blk),
        compiler_params=_params(("parallel", "parallel"), 5 * _nbytes((tr, tc), F32)),
    )(ab, ab, dh)


def _xattn_fwd(q, kv, name):
    t = q.shape[0]
    tq = _pick(t, ROW_TILE, BF16_SUBLANE)
    mlen = kv.shape[0]
    scale = X_HEAD ** -0.5

    def body(q_ref, kv_ref, o_ref):
        for h in range(X_HEADS):
            sl = slice(h * X_HEAD, (h + 1) * X_HEAD)
            kh = kv_ref[:, sl]
            vh = kv_ref[:, D_MODEL + h * X_HEAD:D_MODEL + (h + 1) * X_HEAD]
            s = _dot(q_ref[:, sl], kh, NT) * scale
            p = jnp.exp(s - jnp.max(s, axis=1, keepdims=True))
            l = jnp.sum(p, axis=1, keepdims=True)
            o_ref[:, sl] = (_dot(p.astype(BF16), vh, NN) / l).astype(BF16)

    return pl.pallas_call(
        body, name=name, grid=(t // tq,),
        out_shape=jax.ShapeDtypeStruct((t, D_MODEL), BF16),
        in_specs=[pl.BlockSpec((tq, D_MODEL), lambda i: (i, 0)), pl.BlockSpec((mlen, 2 * D_MODEL), lambda i: (0, 0))],
        out_specs=pl.BlockSpec((tq, D_MODEL), lambda i: (i, 0)),
        compiler_params=_params(("parallel",), 4 * _nbytes((tq, D_MODEL), F32)),
    )(q, kv)


def _xattn_bwd(q, kv, do, name):
    t = q.shape[0]
    tq = _pick(t, ROW_TILE, BF16_SUBLANE)
    mlen = kv.shape[0]
    scale = X_HEAD ** -0.5

    def body(q_ref, kv_ref, do_ref, dq_ref, dkv_ref):
        i = pl.program_id(0)

        @pl.when(i == 0)
        def _():
            dkv_ref[...] = jnp.zeros_like(dkv_ref)

        for h in range(X_HEADS):
            sl = slice(h * X_HEAD, (h + 1) * X_HEAD)
            vsl = slice(D_MODEL + h * X_HEAD, D_MODEL + (h + 1) * X_HEAD)
            qh, kh, vh, doh = q_ref[:, sl], kv_ref[:, sl], kv_ref[:, vsl], do_ref[:, sl]
            s = _dot(qh, kh, NT) * scale
            p = jnp.exp(s - jnp.max(s, axis=1, keepdims=True))
            p = p / jnp.sum(p, axis=1, keepdims=True)
            dp = _dot(doh, vh, NT)
            ds = p * (dp - jnp.sum(p * dp, axis=1, keepdims=True))
            dsb = ds.astype(BF16)
            dq_ref[:, sl] = (_dot(dsb, kh, NN) * scale).astype(BF16)
            dkv_ref[:, sl] += _dot(dsb, qh, TN) * scale
            dkv_ref[:, vsl] += _dot(p.astype(BF16), doh, TN)

    row = pl.BlockSpec((tq, D_MODEL), lambda i: (i, 0))
    whole = pl.BlockSpec((mlen, 2 * D_MODEL), lambda i: (0, 0))
    return pl.pallas_call(
        body, name=name, grid=(t // tq,),
        out_shape=(jax.ShapeDtypeStruct((t, D_MODEL), BF16), jax.ShapeDtypeStruct((mlen, 2 * D_MODEL), F32)),
        in_specs=[row, whole, row], out_specs=(row, whole),
        compiler_params=_params(("arbitrary",), 6 * _nbytes((tq, D_MODEL), F32)),
    )(q, kv, do)


def _position():
    return lax.axis_index("x"), lax.axis_index("y"), lax.axis_index("c")


N_PEER = N_DEV - 1


def _all_gather(xs, name):
    n = len(xs)

    def body(*refs):
        x_refs, out_refs = refs[:n], refs[n:2 * n]
        send_sems, recv_sems, local_sems = refs[2 * n:]
        mx, my, mc = _position()
        me, sib = (mx, my, mc), (mx, my, 1 - mc)
        chips = [(1 - mx, my), (mx, 1 - my), (1 - mx, 1 - my)]

        def slot(i, p):
            return out_refs[i].at[4 * p[0] + 2 * p[1] + p[2]]

        def copy(i, k, block, to, src=None):
            return pltpu.make_async_remote_copy(
                src_ref=slot(i, block) if src is None else src, dst_ref=slot(i, block),
                send_sem=send_sems.at[i * N_PEER + k], recv_sem=recv_sems.at[i * N_PEER + k],
                device_id=to, device_id_type=MESH)

        mine = [pltpu.make_async_copy(x_refs[i], slot(i, me), local_sems.at[i]) for i in range(n)]
        for cp in mine:
            cp.start()
        first = [copy(i, 1 + j, me, (*chip, mc), src=x_refs[i]) for j, chip in enumerate(chips) for i in range(n)]
        first += [copy(i, 0, me, sib, src=x_refs[i]) for i in range(n)]
        for cp in first:
            cp.start()
        passed = []
        for j, chip in enumerate(chips):
            for i in range(n):
                copy(i, 1 + j, (*chip, mc), me).wait_recv()
                passed.append(copy(i, 4 + j, (*chip, mc), sib))
                passed[-1].start()
        for i in range(n):
            copy(i, 0, sib, me).wait_recv()
        for j, chip in enumerate(chips):
            for i in range(n):
                copy(i, 4 + j, (*chip, 1 - mc), me).wait_recv()
        for cp in first + passed:
            cp.wait_send()
        for cp in mine:
            cp.wait()

    return pl.pallas_call(
        body, name=name,
        out_shape=tuple(jax.ShapeDtypeStruct((N_DEV,) + x.shape, x.dtype) for x in xs),
        in_specs=[pl.BlockSpec(memory_space=pl.ANY)] * n, out_specs=(pl.BlockSpec(memory_space=pl.ANY),) * n,
        scratch_shapes=[pltpu.SemaphoreType.DMA((n * N_PEER,)), pltpu.SemaphoreType.DMA((n * N_PEER,)),
                        pltpu.SemaphoreType.DMA((n,))],
    )(*xs)


PEER_RELS = [(dx, dy, dc) for dx in (0, 1) for dy in (0, 1) for dc in (0, 1)][1:]


def _peer_copy(rel_k, i, src_refs, land_refs, send_sems, recv_sems, layer, gather, arriving):
    mx, my, mc = _position()
    me_idx = 4 * mx + 2 * my + mc
    p = tuple((1 - v) if f else v for f, v in zip(PEER_RELS[rel_k], (mx, my, mc)))
    p_idx = 4 * p[0] + 2 * p[1] + p[2]
    src_slot, dst_slot = (me_idx, p_idx) if arriving else (p_idx, me_idx)
    src = src_refs[i] if gather else src_refs[i].at[src_slot]
    dst = land_refs[i].at[dst_slot] if layer is None else land_refs[i].at[dst_slot, layer]
    return pltpu.make_async_remote_copy(
        src_ref=src, dst_ref=dst, send_sem=send_sems.at[i * N_PEER + rel_k], recv_sem=recv_sems.at[i * N_PEER + rel_k],
        device_id=p, device_id_type=MESH)


HBM_SPEC = pl.BlockSpec(memory_space=pltpu.HBM)
SEM_SPEC = pl.BlockSpec(memory_space=pltpu.SEMAPHORE)
SIDE_EFFECT = pltpu.SideEffectType.DATAFLOW_SIDE_EFFECTING


def _own_slots(srcs, lands, layer, gather):
    mx, my, mc = _position()
    me_idx = 4 * mx + 2 * my + mc
    out = []
    for s, land in zip(srcs, lands):
        piece = s[None] if gather else lax.dynamic_index_in_dim(s, me_idx, 0, keepdims=True)
        if layer is None:
            start = (me_idx,) + (0,) * (land.ndim - 1)
        else:
            piece, start = piece[:, None], (me_idx, layer) + (0,) * (land.ndim - 2)
        out.append(lax.dynamic_update_slice(land, piece, start))
    return out


def _swap_start(srcs, lands, layer, gather, name):
    n = len(srcs)

    def body(*refs):
        src_refs, land_refs = refs[:n], refs[n:2 * n]
        send_sems, recv_sems = refs[2 * n], refs[2 * n + 1]
        token = refs[4 * n + 2]
        for i in range(n):
            for k in range(N_PEER):
                _peer_copy(k, i, src_refs, land_refs, send_sems, recv_sems, layer, gather, False).start()
        token[...] = jnp.zeros_like(token)

    hbm = [pltpu.with_memory_space_constraint(a, pltpu.HBM) for a in list(srcs) + list(lands)]
    out = pl.pallas_call(
        body, name=name,
        out_shape=(pltpu.SemaphoreType.DMA((n * N_PEER,)), pltpu.SemaphoreType.DMA((n * N_PEER,)))
        + tuple(pltpu.HBM(a.shape, a.dtype) for a in hbm) + (jax.ShapeDtypeStruct((8, LANE), F32),),
        in_specs=[HBM_SPEC] * (2 * n),
        out_specs=(SEM_SPEC, SEM_SPEC) + (HBM_SPEC,) * (2 * n) + (pl.BlockSpec(memory_space=pltpu.VMEM),),
        input_output_aliases={i: 2 + i for i in range(2 * n)},
        compiler_params=pltpu.CompilerParams(has_side_effects=SIDE_EFFECT),
    )(*hbm)
    return out[0], out[1], list(out[2:2 + n]), list(out[2 + n:2 + 2 * n]), out[2 + 2 * n]


def _swap_wait(send_sems, recv_sems, srcs, lands, after, layer, gather, name):
    n = len(srcs)

    def body(*refs):
        src_refs, land_refs = refs[:n], refs[n:2 * n]
        send_sems_ref, recv_sems_ref = refs[2 * n], refs[2 * n + 1]
        for i in range(n):
            for k in range(N_PEER):
                args = (src_refs, land_refs, send_sems_ref, recv_sems_ref, layer, gather)
                _peer_copy(k, i, *args, False).wait_send()
                _peer_copy(k, i, *args, True).wait_recv()

    out = pl.pallas_call(
        body, name=name,
        out_shape=tuple(pltpu.HBM(a.shape, a.dtype) for a in list(srcs) + list(lands)),
        in_specs=[HBM_SPEC] * (2 * n) + [SEM_SPEC, SEM_SPEC, pl.BlockSpec(memory_space=pl.ANY)],
        out_specs=(HBM_SPEC,) * (2 * n),
        input_output_aliases={i: i for i in range(2 * n)},
        compiler_params=pltpu.CompilerParams(has_side_effects=SIDE_EFFECT),
    )(*srcs, *lands, send_sems, recv_sems, after)
    return list(out[n:])


ADAMW_BLOCK_BYTES = 1 << 20


def _adamw(parts, w, m, v, name):
    r, l = w.shape
    tr = _pick(r, max(ADAMW_BLOCK_BYTES // (4 * l), BF16_SUBLANE), BF16_SUBLANE)
    c1 = 1.0 - ADAM_B1 ** ADAM_STEP
    c2 = 1.0 - ADAM_B2 ** ADAM_STEP

    def body(p_ref, w_ref, m_ref, v_ref, g_ref, d_ref, nm_ref, nv_ref):
        g = p_ref[0].astype(F32)
        for s in range(1, N_DEV):
            g = g + p_ref[s].astype(F32)
        nm = ADAM_B1 * m_ref[...] + (1.0 - ADAM_B1) * g
        nv = ADAM_B2 * v_ref[...] + (1.0 - ADAM_B2) * (g * g)
        m_hat = nm / c1
        v_hat = nv / c2
        g_ref[...] = g
        d_ref[...] = -ADAM_LR * (m_hat / (jnp.sqrt(v_hat) + ADAM_EPS) + ADAM_WD * w_ref[...])
        nm_ref[...] = nm
        nv_ref[...] = nv

    row = pl.BlockSpec((tr, l), lambda i: (i, 0))
    return pl.pallas_call(
        body, name=name, grid=(r // tr,),
        out_shape=(jax.ShapeDtypeStruct((r, l), F32),) * 4,
        in_specs=[pl.BlockSpec((N_DEV, tr, l), lambda i: (0, i, 0)), row, row, row],
        out_specs=(row,) * 4,
        compiler_params=_params(("parallel",), 12 * _nbytes((tr, l), F32)),
    )(parts, w, m, v)


MATRIX_WEIGHTS = (("w_in", 2), ("conv_w", 2), ("w_branch", 3), ("w_mix_out", 1), ("w_xq", 1), ("w_xkv", 2),
                  ("w_xo", 1), ("w_ffn_gate", 2), ("w_ffn_up", 2), ("w_ffn_down", 1))
SMALL_PARAMS = ("mix_norm_g", "xattn_norm_g", "mem_norm_g", "ffn_norm_g", "final_norm_g", "forget_bias", "sink",
                "rel_bias")


def _pack_small(pieces):
    flat = jnp.concatenate([p.astype(F32).reshape(-1) for p in pieces])
    total = -(-flat.shape[0] // (8 * LANE)) * (8 * LANE)
    return jnp.pad(flat, (0, total - flat.shape[0])).reshape(total // LANE, LANE)


def _rows(a):
    return a.reshape(-1, a.shape[-1])


def _to_full(gathered, axis):
    moved = jnp.moveaxis(gathered, 0, axis)
    shape = list(moved.shape)
    shape[axis:axis + 2] = [shape[axis] * shape[axis + 1]]
    return moved.reshape(shape)


def _to_blocks(full, axis):
    shape = list(full.shape)
    shape[axis:axis + 1] = [N_DEV, shape[axis] // N_DEV]
    return jnp.moveaxis(full.reshape(shape), axis, 0)


def _perm_in(w_in):
    pad = jnp.zeros((w_in.shape[0], PROJ_COLS - IN_COLS), w_in.dtype)
    return jnp.concatenate([w_in[:, 3848:6920], w_in[:, 0:3072], w_in[:, 3080:3848], w_in[:, 3072:3080], pad], axis=1)


def _unperm_in(dw):
    return jnp.concatenate([dw[:, 3072:6144], dw[:, 6912:6920], dw[:, 6144:6912], dw[:, 0:3072]], axis=1)


def _layer_fwd(l, x, mem, wt, sm, pre_branch=None):
    t = x.shape[0]
    tag = f"l{l}_"
    h = _rms_fwd(x, sm["mix_norm_g"][l], tag + "mix_norm")
    proj = _matmul(h, wt["w_in"][l], "nn", F32, tag + "in_proj")
    y_conv = _conv_fwd(proj, wt["conv_w"][l], tag + "conv")
    fbias_row = jnp.pad(sm["forget_bias"][l], (0, LANE - 8)).reshape(1, LANE)
    c = _logf_cumsum(proj, fbias_row, tag + "logf_cumsum")
    c8 = c[:, :8].T
    c_col = jnp.broadcast_to(c8[:, :, None], (8, t, LANE))
    c_row = c8.reshape(8, 1, t)
    y_fox, lse_fox = _fox_fwd(proj, c_col, c_row, tag + "fox")
    onehot, bias = sm["swa_tables"]
    sink_rep = jnp.broadcast_to(sm["sink"][l].reshape(2, SWA_GROUP, 1), (2, SWA_GROUP, LANE))
    y_swa, lse_swa = _swa_fwd(proj, bias, sink_rep, tag + "swa")
    ys = (y_conv, y_fox, y_swa)
    if pre_branch is not None:
        pre_branch(y_swa)
    pb = tuple(_matmul(ys[b], wt["w_branch"][l][b], "nn", F32, tag + f"branch{b}") for b in range(3))
    merged = _gate_fwd(proj, pb, tag + "gate")
    x1 = _matmul(merged, wt["w_mix_out"][l], "nn", F32, tag + "mix_out", residual=x)
    xn2 = _rms_fwd(x1, sm["xattn_norm_g"][l], tag + "xattn_norm")
    q = _matmul(xn2, wt["w_xq"][l], "nn", BF16, tag + "xq")
    mem_n = _rms_fwd(mem, sm["mem_norm_g"][l], tag + "mem_norm")
    kv = _matmul(mem_n, wt["w_xkv"][l], "nn", BF16, tag + "xkv")
    o = _xattn_fwd(q, kv, tag + "xattn")
    x2 = _matmul(o, wt["w_xo"][l], "nn", F32, tag + "xo", residual=x1)
    xn3 = _rms_fwd(x2, sm["ffn_norm_g"][l], tag + "ffn_norm")
    ab = _matmul(xn3, wt["w_gu"][l], "nn", F32, tag + "ffn_gu")
    h1 = _swiglu_fwd(ab, tag + "swiglu")
    x3 = _matmul(h1, wt["w_ffn_down"][l], "nn", F32, tag + "ffn_down", residual=x2)
    saved = dict(x=x, h=h, proj=proj, fbias_row=fbias_row, c_col=c_col, c_row=c_row, ys=ys, lse_fox=lse_fox,
                 onehot=onehot, bias=bias, sink_rep=sink_rep, lse_swa=lse_swa, pb=pb, merged=merged, x1=x1,
                 xn2=xn2, q=q, mem_n=mem_n, kv=kv, o=o, x2=x2, xn3=xn3, ab=ab, h1=h1)
    return x3, saved


def _layer_bwd(l, dx3, dx3_b, mem, wt, sm, sv, mid_hook=None, late_hook=None):
    t = dx3.shape[0]
    nb = t // SWA_BLOCK
    tag = f"l{l}_b_"
    gw, gs = {}, {}
    dh1 = _matmul(dx3_b, wt["w_ffn_down"][l], "nt", F32, tag + "d_h1")
    gw["w_ffn_down"] = _matmul(sv["h1"], dx3_b, "tn", F32, tag + "dw_down")
    da, db = _swiglu_bwd(sv["ab"], dh1, tag + "swiglu")
    dxn3 = _matmul(da, wt["w_ffn_gate"][l], "nt", F32, tag + "d_xn3_gate")
    dxn3 = _matmul(db, wt["w_ffn_up"][l], "nt", F32, tag + "d_xn3_up", residual=dxn3)
    gw["w_ffn_gate"] = _matmul(sv["xn3"], da, "tn", F32, tag + "dw_gate")
    gw["w_ffn_up"] = _matmul(sv["xn3"], db, "tn", F32, tag + "dw_up")
    dx2, dx2_b, gs["ffn_norm_g"] = _rms_bwd(sv["x2"], sm["ffn_norm_g"][l], dxn3, dx3, tag + "ffn_norm")
    do = _matmul(dx2_b, wt["w_xo"][l], "nt", BF16, tag + "d_o")
    gw["w_xo"] = _matmul(sv["o"], dx2_b, "tn", F32, tag + "dw_xo")
    dq, dkv = _xattn_bwd(sv["q"], sv["kv"], do, tag + "xattn")
    gw["w_xkv"] = _matmul(sv["mem_n"], dkv, "tn", F32, tag + "dw_xkv")
    dmem_n = _matmul(dkv, wt["w_xkv"][l], "nt", F32, tag + "d_memn")
    _, _, gs["mem_norm_g"] = _rms_bwd(mem, sm["mem_norm_g"][l], dmem_n, None, tag + "mem_norm")
    gw["w_xq"] = _matmul(sv["xn2"], dq, "tn", F32, tag + "dw_xq")
    dxn2 = _matmul(dq, wt["w_xq"][l], "nt", F32, tag + "d_xn2")
    dx1, dx1_b, gs["xattn_norm_g"] = _rms_bwd(sv["x1"], sm["xattn_norm_g"][l], dxn2, dx2, tag + "xattn_norm")
    dmerged = _matmul(dx1_b, wt["w_mix_out"][l], "nt", F32, tag + "d_merged")
    gw["w_mix_out"] = _matmul(sv["merged"], dx1_b, "tn", F32, tag + "dw_mix_out")
    dp0, dp1, dp2, dgate = _gate_bwd(sv["proj"], sv["pb"], dmerged, tag + "gate")
    dps = (dp0, dp1, dp2)
    dy_dtypes = (F32, BF16, BF16)
    dys = [_matmul(dps[b], wt["w_branch"][l][b], "nt", dy_dtypes[b], tag + f"d_y{b}") for b in range(3)]
    gw["w_branch"] = jnp.stack(
        [_matmul(sv["ys"][b], dps[b], "tn", F32, tag + f"dw_branch{b}") for b in range(3)])
    sink_rep = sv["sink_rep"] if mid_hook is None else sv["sink_rep"] + mid_hook(gw)
    dsq, dkp, dvp, dbias, dsink = _swa_bwd(sv["proj"], sv["bias"], sink_rep, sv["lse_swa"], sv["ys"][2],
                                           dys[2], tag + "swa")

    def band_add(part):
        tot = part[0] + part[1]
        cur = tot[:, SWA_BLOCK:, :]
        nxt = jnp.concatenate([tot[1:, :SWA_BLOCK, :], jnp.zeros((1, SWA_BLOCK, LANE), F32)], axis=0)
        return (cur + nxt).reshape(t, LANE).astype(BF16)

    dsk, dsv = band_add(dkp), band_add(dvp)
    gs["rel_bias_l"] = jnp.einsum("hts,tsb->bh", dbias.reshape(8, SWA_BLOCK, 2 * SWA_BLOCK), sv["onehot"],
                                  precision=lax.Precision.HIGHEST)
    gs["sink"] = dsink[:, :, 0].reshape(8)
    dfq, dfk, dfv, dck, dcq_row = _fox_bwd(sv["proj"], sv["c_col"], sv["c_row"], sv["lse_fox"], sv["ys"][1], dys[1],
                                           tag + "fox_bwd")
    dcq = jnp.pad(dcq_row.reshape(8, t).T, ((0, 0), (0, LANE - 8))).reshape(1, t, LANE)
    dfg, dfb = _logf_cumsum_bwd(sv["proj"], sv["fbias_row"], [dck, dcq], tag + "logf_cumsum")
    gs["forget_bias"] = dfb[0, :8]
    dcb, dcc, dcu, dconv = _conv_bwd(sv["proj"], wt["conv_w"][l], dys[0], tag + "conv")
    gw["conv_w"] = dconv[:3]
    dproj = jnp.concatenate([dgate, dcb, dcc, dcu, dfq, dfk, dfv, dsq, dsk, dsv, dfg], axis=1)
    gw["w_in"] = _unperm_in(_matmul(sv["h"], dproj, "tn", F32, tag + "dw_in"))
    g_mix = sm["mix_norm_g"][l] if late_hook is None else sm["mix_norm_g"][l] + late_hook(gw)
    dh = _matmul(dproj, wt["w_in"][l], "nt", F32, tag + "d_h")
    dx, dx_b, gs["mix_norm_g"] = _rms_bwd(sv["x"], g_mix, dh, dx1, tag + "mix_norm")
    return dx, dx_b, gw, gs


def kernel(x, mem, mix_norm_g, w_in, forget_bias, conv_w, sink, w_branch, w_mix_out, rel_bias, xattn_norm_g, mem_norm_g, w_xq, w_xkv, w_xo, ffn_norm_g, w_ffn_gate, w_ffn_up, w_ffn_down, final_norm_g, loss_target, m_mix_norm_g, m_w_in, m_forget_bias, m_conv_w, m_sink, m_w_branch, m_w_mix_out, m_rel_bias, m_xattn_norm_g, m_mem_norm_g, m_w_xq, m_w_xkv, m_w_xo, m_ffn_norm_g, m_w_ffn_gate, m_w_ffn_up, m_w_ffn_down, m_final_norm_g, v_mix_norm_g, v_w_in, v_forget_bias, v_conv_w, v_sink, v_w_branch, v_w_mix_out, v_rel_bias, v_xattn_norm_g, v_mem_norm_g, v_w_xq, v_w_xkv, v_w_xo, v_ffn_norm_g, v_w_ffn_gate, v_w_ffn_up, v_w_ffn_down, v_final_norm_g):
    args = dict(locals())
    names = [n for n, _ in MATRIX_WEIGHTS] + list(SMALL_PARAMS)
    w = {n: args[n] for n in names}
    mo = {n: args["m_" + n] for n in names}
    vo = {n: args["v_" + n] for n in names}
    x2d, mem2d, tgt = x[0], mem[0], loss_target[0]

    wire = {n: (F32 if n == "conv_w" else BF16) for n, _ in MATRIX_WEIGHTS}
    late = ("w_in", "conv_w")
    early_w = [(n, ax) for n, ax in MATRIX_WEIGHTS if n not in late]
    late_w = [(n, ax) for n, ax in MATRIX_WEIGHTS if n in late]
    wt = {n: [None] * DEPTH for n, _ in MATRIX_WEIGHTS}
    wt["w_gu"] = [None] * DEPTH

    ff_pad = FF_SHARD_P - FF_SHARD
    ff_axis = {"w_ffn_gate": 2, "w_ffn_up": 2, "w_ffn_down": 1}

    def pad_ffn(n, blocks):
        if n not in ff_axis:
            return blocks
        return jnp.pad(blocks, [(0, ff_pad if d == ff_axis[n] else 0) for d in range(blocks.ndim)])

    def unpad_ffn(n, blocks):
        return lax.slice_in_dim(blocks, 0, FF_SHARD, axis=ff_axis[n]) if n in ff_axis else blocks

    def place_weights(l, which, gathered):
        for (n, ax), g in zip(which, gathered):
            wt[n][l] = _to_full(pad_ffn(n, g), ax - 1)
        if "w_in" in dict(which):
            wt["w_in"][l] = _perm_in(wt["w_in"][l])
        if "w_ffn_gate" in dict(which):
            wt["w_gu"][l] = jnp.concatenate([wt["w_ffn_gate"][l], wt["w_ffn_up"][l]], axis=1)

    def shards_of(l, which):
        return [w[n][l].astype(wire[n]) for n, _ in which]

    def start_gather(srcs, name):
        lands = _own_slots(srcs, [lax.empty((N_DEV,) + s.shape, s.dtype) for s in srcs], None, True)
        return _swap_start(srcs, lands, None, True, name)

    place_weights(0, late_w, _all_gather(shards_of(0, late_w), "weights_gather_l0_first"))
    r_send, r_recv, r_srcs, r_lands, token = start_gather(shards_of(0, early_w), "weights_gather_l0_rest_start")
    shards1 = shards_of(1, MATRIX_WEIGHTS)
    shards1[0] = shards1[0] + token[0, 0].astype(shards1[0].dtype)
    w_send, w_recv, w_srcs, lands, token = start_gather(shards1, "weights_gather_l1_start")
    sm = {n: w[n] for n in SMALL_PARAMS}
    sm["mix_norm_g"] = w["mix_norm_g"].at[0].add(token[0, 0])
    sm["swa_tables"] = _swa_tables(w["rel_bias"])

    def rest_of_layer0(after):
        place_weights(0, early_w, _swap_wait(r_send, r_recv, r_srcs, r_lands, after, None, True,
                                             "weights_gather_l0_rest_wait"))

    saved = []
    xc = x2d
    for l in range(DEPTH):
        if l == 1:
            place_weights(1, MATRIX_WEIGHTS,
                          _swap_wait(w_send, w_recv, w_srcs, lands, xc, None, True, "weights_gather_l1_wait"))
        xc, sv = _layer_fwd(l, xc, mem2d, wt, sm, rest_of_layer0 if l == 0 else None)
        saved.append(sv)
    loss_row, dx, dx_b, dg_final = _loss_head(xc, sm["final_norm_g"], tgt, "loss_head")

    def grad_parts(gw, which):
        return [unpad_ffn(n, _to_blocks(gw[n], ax - 1)).astype(wire[n]) for n, ax in which]

    gw_all, gs_all = [None] * DEPTH, [None] * DEPTH
    dx, dx_b, gw_all[1], gs_all[1] = _layer_bwd(1, dx, dx_b, mem2d, wt, sm, saved[1])
    parts1 = grad_parts(gw_all[1], MATRIX_WEIGHTS)
    zones = _own_slots(parts1, [lax.empty((N_DEV, DEPTH) + p.shape[1:], p.dtype) for p in parts1], 1, False)
    g_send, g_recv, g_srcs, zones, token = _swap_start(parts1, zones, 1, False, "grads_exchange_l1_start")
    sm_b = dict(sm)
    sm_b["ffn_norm_g"] = sm["ffn_norm_g"].at[0].add(token[0, 0])
    mid = {}

    def mid_hook(gw):
        zone = dict(zip([n for n, _ in MATRIX_WEIGHTS],
                        _swap_wait(g_send, g_recv, g_srcs, zones, gw["w_mix_out"], 1, False, "grads_exchange_l1_wait")))
        parts0 = grad_parts(gw, early_w)
        early_zones = _own_slots(parts0, [zone[n] for n, _ in early_w], 0, False)
        mid["early"] = _swap_start(parts0, early_zones, 0, False, "grads_exchange_l0_early_start")
        mid["late_zones"] = [zone[n] for n, _ in late_w]
        return mid["early"][4][0, 0]

    def late_hook(gw):
        parts0 = grad_parts(gw, late_w)
        late_zones = _own_slots(parts0, mid["late_zones"], 0, False)
        mid["late"] = _swap_start(parts0, late_zones, 0, False, "grads_exchange_l0_late_start")
        return mid["late"][4][0, 0]

    dx, dx_b, gw_all[0], gs_all[0] = _layer_bwd(0, dx, dx_b, mem2d, wt, sm_b, saved[0], mid_hook, late_hook)
    grad_x = dx[None]
    recv_by_name = {}
    for key, which in (("early", early_w), ("late", late_w)):
        s_send, s_recv, s_srcs, s_zones, _ = mid[key]
        recv_by_name.update(zip([n for n, _ in which],
                                _swap_wait(s_send, s_recv, s_srcs, s_zones, dx, 0, False,
                                           "grads_exchange_l0_" + key + "_wait")))
    recv = [recv_by_name[n] for n, _ in MATRIX_WEIGHTS]

    outs = {}
    for (n, _), r in zip(MATRIX_WEIGHTS, recv):
        res = _adamw(r.reshape((N_DEV,) + _rows(w[n]).shape), _rows(w[n]), _rows(mo[n]), _rows(vo[n]), "adamw_" + n)
        outs[n] = [o.reshape(w[n].shape) for o in res]

    gsm = {n: jnp.stack([gs_all[l][n] for l in range(DEPTH)])
           for n in ("mix_norm_g", "xattn_norm_g", "mem_norm_g", "ffn_norm_g", "forget_bias", "sink")}
    gsm["final_norm_g"] = dg_final
    gsm["rel_bias"] = gs_all[0]["rel_bias_l"] + gs_all[1]["rel_bias_l"]
    zero = jnp.zeros((1,), F32)
    (small_parts,) = _all_gather([_pack_small([gsm[n] for n in SMALL_PARAMS] + [loss_row[0, :1]])],
                                 "small_grads_all_gather")
    outs_small = _adamw(small_parts, *[_pack_small([d[n] for n in SMALL_PARAMS] + [zero]) for d in (w, mo, vo)],
                        "adamw_small")
    for kind in range(4):
        flat, o = outs_small[kind].reshape(-1), 0
        for n in SMALL_PARAMS:
            sz = int(np.prod(w[n].shape))
            outs.setdefault(n, []).append(flat[o:o + sz].reshape(w[n].shape))
            o += sz
        if kind == 0:
            loss = flat[o]

    order = ["mix_norm_g", "w_in", "forget_bias", "conv_w", "sink", "w_branch", "w_mix_out", "rel_bias",
             "xattn_norm_g", "mem_norm_g", "w_xq", "w_xkv", "w_xo", "ffn_norm_g", "w_ffn_gate", "w_ffn_up",
             "w_ffn_down", "final_norm_g"]
    result = [loss, grad_x]
    for kind in range(4):
        result += [outs[n][kind] for n in order]
    return tuple(result)
```

```python
import math

import numpy as np
import jax
import jax.numpy as jnp
from jax import lax
from jax.experimental import pallas as pl
from jax.experimental.pallas import tpu as pltpu

F32 = jnp.float32
BF16 = jnp.bfloat16
MESH = pl.DeviceIdType.MESH

LANE = 128
BF16_SUBLANE = 16
V7X_VMEM_REQUEST_CAP = 56 * 2 ** 20
N_DEV = 8

D_MODEL = 1024
DEPTH = 2
HEAD = 64
BRANCH = 512
SWA_BLOCK = 128
SWA_GROUP = 4
N_BUCKETS = 32
X_HEADS = 4
X_HEAD = 256
D_FF = 2816
FF_SHARD = D_FF // N_DEV
FF_SHARD_P = -(-FF_SHARD // LANE) * LANE
D_FF_P = N_DEV * FF_SHARD_P
RMS_EPS = 1e-6
NEG = -1e30
ADAM_LR, ADAM_B1, ADAM_B2, ADAM_EPS, ADAM_WD, ADAM_STEP = 0.001, 0.9, 0.999, 1e-08, 0.01, 10

IN_COLS = 6920
PROJ_COLS = 7040
COL_GATE, COL_CONV, COL_FOX, COL_SQ, COL_SK, COL_SV, COL_FG = 0, 3072, 4608, 6144, 6656, 6784, 6912

ROW_TILE = 1024
FOX_TILE = 1024
MM_TM, MM_TN, MM_TK = 2048, 1536, 2048


def _pick(n, cap, mult):
    best = None
    for d in range(mult, min(n, cap) + 1, mult):
        if n % d == 0:
            best = d
    return n if best is None else best


def _params(semantics, block_bytes):
    limit = int(min(max(2 * block_bytes + (8 << 20), 24 << 20), V7X_VMEM_REQUEST_CAP))
    return pltpu.CompilerParams(dimension_semantics=semantics, vmem_limit_bytes=limit)


def _nbytes(shape, dtype):
    return int(np.prod(shape)) * jnp.dtype(dtype).itemsize


def _dot(a, b, dims):
    return lax.dot_general(a, b, (dims, ((), ())), preferred_element_type=F32)


NN = ((1,), (0,))
NT = ((1,), (1,))
TN = ((0,), (0,))


def _matmul(a, b, mode, out_dtype, name, residual=None):
    if mode == "nn":
        (m, k), (k2, n) = a.shape, b.shape
    elif mode == "nt":
        (m, k), (n, k2) = a.shape, b.shape
    else:
        (k, m), (k2, n) = a.shape, b.shape
    assert k == k2, (name, a.shape, b.shape)
    tn, tk = _pick(n, MM_TN, LANE), _pick(k, MM_TK, LANE)
    nk = k // tk
    dims = {"nn": NN, "nt": NT, "tn": TN}[mode]
    has_res = residual is not None
    tm = _pick(m, MM_TM if (nk == 1 and not has_res) else MM_TM // 2, LANE)

    def body(*refs):
        a_ref, b_ref = refs[0], refs[1]
        r_ref = refs[2] if has_res else None
        o_ref = refs[3] if has_res else refs[2]
        kk = pl.program_id(2)
        p = _dot(a_ref[...].astype(BF16), b_ref[...].astype(BF16), dims)
        if nk == 1:
            if has_res:
                p = p + r_ref[...]
            o_ref[...] = p.astype(out_dtype)
        else:
            acc_ref = refs[-1]

            @pl.when(kk == 0)
            def _():
                acc_ref[...] = p

            @pl.when(kk > 0)
            def _():
                acc_ref[...] += p

            @pl.when(kk == nk - 1)
            def _():
                res = acc_ref[...]
                if has_res:
                    res = res + r_ref[...]
                o_ref[...] = res.astype(out_dtype)

    if mode == "nn":
        a_spec = pl.BlockSpec((tm, tk), lambda i, j, kk: (i, kk))
        b_spec = pl.BlockSpec((tk, tn), lambda i, j, kk: (kk, j))
    elif mode == "nt":
        a_spec = pl.BlockSpec((tm, tk), lambda i, j, kk: (i, kk))
        b_spec = pl.BlockSpec((tn, tk), lambda i, j, kk: (j, kk))
    else:
        a_spec = pl.BlockSpec((tk, tm), lambda i, j, kk: (kk, i))
        b_spec = pl.BlockSpec((tk, tn), lambda i, j, kk: (kk, j))
    o_spec = pl.BlockSpec((tm, tn), lambda i, j, kk: (i, j))
    in_specs, args = [a_spec, b_spec], [a, b]
    if has_res:
        in_specs.append(o_spec)
        args.append(residual)
    blk = (_nbytes((tm, tk), a.dtype) + _nbytes((tk, tn), b.dtype) + _nbytes((tm, tn), out_dtype)
           + (_nbytes((tm, tn), F32) if has_res else 0))
    scratch = [pltpu.VMEM((tm, tn), F32)] if nk > 1 else []
    return pl.pallas_call(
        body, name=name, grid=(m // tm, n // tn, nk),
        out_shape=jax.ShapeDtypeStruct((m, n), out_dtype),
        in_specs=in_specs, out_specs=o_spec, scratch_shapes=scratch,
        compiler_params=_params(("parallel", "parallel", "arbitrary"), blk + _nbytes((tm, tn), F32)),
    )(*args)


def _rms_fwd(x, g, name):
    t, d = x.shape
    tr = _pick(t, ROW_TILE, BF16_SUBLANE)

    def body(x_ref, g_ref, y_ref):
        xv = x_ref[...]
        r = lax.rsqrt(jnp.mean(xv * xv, axis=-1, keepdims=True) + RMS_EPS)
        y_ref[...] = ((xv * r) * g_ref[...]).astype(BF16)

    return pl.pallas_call(
        body, name=name, grid=(t // tr,),
        out_shape=jax.ShapeDtypeStruct((t, d), BF16),
        in_specs=[pl.BlockSpec((tr, d), lambda i: (i, 0)), pl.BlockSpec((1, d), lambda i: (0, 0))],
        out_specs=pl.BlockSpec((tr, d), lambda i: (i, 0)),
        compiler_params=_params(("parallel",), 2 * _nbytes((tr, d), F32)),
    )(x, g.reshape(1, d))


def _rms_bwd(x, g, dy, dres, name):
    t, d = x.shape
    tr = _pick(t, ROW_TILE, BF16_SUBLANE)
    has_res = dres is not None

    def body(*refs):
        x_ref, g_ref, dy_ref = refs[:3]
        r_ref = refs[3] if has_res else None
        dx_ref, dxb_ref, dg_ref = refs[-3:]
        i = pl.program_id(0)
        xv = x_ref[...]
        r = lax.rsqrt(jnp.mean(xv * xv, axis=-1, keepdims=True) + RMS_EPS)
        xh = xv * r
        dyv = dy_ref[...].astype(F32)
        dxh = dyv * g_ref[...]
        dx = r * (dxh - xh * jnp.mean(dxh * xh, axis=-1, keepdims=True))
        if has_res:
            dx = dx + r_ref[...]
        dx_ref[...] = dx
        dxb_ref[...] = dx.astype(BF16)

        @pl.when(i == 0)
        def _():
            dg_ref[...] = jnp.zeros_like(dg_ref)

        dg_ref[...] += jnp.sum(dyv * xh, axis=0, keepdims=True)

    row = pl.BlockSpec((tr, d), lambda i: (i, 0))
    vec = pl.BlockSpec((1, d), lambda i: (0, 0))
    in_specs, args = [row, vec, row], [x, g.reshape(1, d), dy]
    if has_res:
        in_specs.append(row)
        args.append(dres)
    return pl.pallas_call(
        body, name=name, grid=(t // tr,),
        out_shape=(jax.ShapeDtypeStruct((t, d), F32), jax.ShapeDtypeStruct((t, d), BF16),
                   jax.ShapeDtypeStruct((1, d), F32)),
        in_specs=in_specs, out_specs=(row, row, vec),
        compiler_params=_params(("arbitrary",), 5 * _nbytes((tr, d), F32)),
    )(*args)


def _loss_head(x, g, target, name):
    t, d = x.shape
    tr = _pick(t, ROW_TILE, BF16_SUBLANE)

    def body(x_ref, g_ref, t_ref, loss_ref, dx_ref, dxb_ref, dg_ref):
        i = pl.program_id(0)
        xv = x_ref[...]
        gv = g_ref[...]
        r = lax.rsqrt(jnp.mean(xv * xv, axis=-1, keepdims=True) + RMS_EPS)
        xh = xv * r
        diff = xh * gv - t_ref[...]
        part = 0.5 * jnp.sum(jnp.mean(diff * diff, axis=-1, keepdims=True), axis=0, keepdims=True)
        dyv = diff * (1.0 / d)
        dxh = dyv * gv
        dx = r * (dxh - xh * jnp.mean(dxh * xh, axis=-1, keepdims=True))
        dx_ref[...] = dx
        dxb_ref[...] = dx.astype(BF16)

        @pl.when(i == 0)
        def _():
            dg_ref[...] = jnp.zeros_like(dg_ref)
            loss_ref[...] = jnp.zeros_like(loss_ref)

        dg_ref[...] += jnp.sum(dyv * xh, axis=0, keepdims=True)
        loss_ref[...] += jnp.broadcast_to(part, loss_ref.shape)

    row = pl.BlockSpec((tr, d), lambda i: (i, 0))
    vec = pl.BlockSpec((1, d), lambda i: (0, 0))
    return pl.pallas_call(
        body, name=name, grid=(t // tr,),
        out_shape=(jax.ShapeDtypeStruct((1, LANE), F32), jax.ShapeDtypeStruct((t, d), F32),
                   jax.ShapeDtypeStruct((t, d), BF16), jax.ShapeDtypeStruct((1, d), F32)),
        in_specs=[row, vec, row],
        out_specs=(pl.BlockSpec((1, LANE), lambda i: (0, 0)), row, row, vec),
        compiler_params=_params(("arbitrary",), 5 * _nbytes((tr, d), F32)),
    )(x, g.reshape(1, d), target)


HALO = 8


def _conv_fwd(proj, conv_w, name):
    t = proj.shape[0]
    tr = _pick(t, ROW_TILE, BF16_SUBLANE)
    c0 = COL_CONV // BRANCH
    hb = tr // HALO

    def body(cb_ref, cc_ref, cu_ref, hc_ref, hu_ref, w_ref, y_ref):
        i = pl.program_id(0)
        z = cc_ref[...] * cu_ref[...]
        hz = jnp.where(i > 0, hc_ref[...] * hu_ref[...], 0.0)
        zf = jnp.concatenate([hz, z], axis=0)
        z1 = pltpu.roll(zf, 1, 0)[HALO:]
        z2 = pltpu.roll(zf, 2, 0)[HALO:]
        y = w_ref[2:3, :] * z + w_ref[1:2, :] * z1 + w_ref[0:1, :] * z2
        y_ref[...] = (cb_ref[...] * y).astype(BF16)

    def col(c):
        return pl.BlockSpec((tr, BRANCH), lambda i, c=c: (i, c0 + c))

    def prev(c):
        return pl.BlockSpec((HALO, BRANCH), lambda i, c=c: (jnp.maximum(i * hb - 1, 0), c0 + c))

    return pl.pallas_call(
        body, name=name, grid=(t // tr,),
        out_shape=jax.ShapeDtypeStruct((t, BRANCH), BF16),
        in_specs=[col(0), col(1), col(2), prev(1), prev(2), pl.BlockSpec((3, BRANCH), lambda i: (0, 0))],
        out_specs=pl.BlockSpec((tr, BRANCH), lambda i: (i, 0)),
        compiler_params=_params(("parallel",), 6 * _nbytes((tr, BRANCH), F32)),
    )(proj, proj, proj, proj, proj, conv_w)


def _conv_bwd(proj, conv_w, dout, name):
    t = proj.shape[0]
    tr = _pick(t, ROW_TILE, BF16_SUBLANE)
    nblk = t // tr
    c0 = COL_CONV // BRANCH
    hb = tr // HALO
    last_halo = t // HALO - 1

    def body(cb_ref, cc_ref, cu_ref, hc_ref, hu_ref, do_ref, ndo_ref, ncb_ref, w_ref,
             dcb_ref, dcc_ref, dcu_ref, dw_ref):
        i = pl.program_id(0)
        cb, cc, cu = cb_ref[...], cc_ref[...], cu_ref[...]
        w0, w1, w2 = w_ref[0:1, :], w_ref[1:2, :], w_ref[2:3, :]
        z = cc * cu
        hz = jnp.where(i > 0, hc_ref[...] * hu_ref[...], 0.0)
        zf = jnp.concatenate([hz, z], axis=0)
        z1 = pltpu.roll(zf, 1, 0)[HALO:]
        z2 = pltpu.roll(zf, 2, 0)[HALO:]
        y = w2 * z + w1 * z1 + w0 * z2
        dout_v = do_ref[...]
        dyc = dout_v * cb
        hdy = jnp.where(i < nblk - 1, ndo_ref[...] * ncb_ref[...], 0.0)
        dyf = jnp.concatenate([dyc, hdy], axis=0)
        dy1 = pltpu.roll(dyf, tr + HALO - 1, 0)[:tr]
        dy2 = pltpu.roll(dyf, tr + HALO - 2, 0)[:tr]
        dz = w2 * dyc + w1 * dy1 + w0 * dy2
        dcb_ref[...] = (dout_v * y).astype(BF16)
        dcc_ref[...] = (dz * cu).astype(BF16)
        dcu_ref[...] = (dz * cc).astype(BF16)

        @pl.when(i == 0)
        def _():
            dw_ref[...] = jnp.zeros_like(dw_ref)

        dw_ref[0:1, :] += jnp.sum(dyc * z2, axis=0, keepdims=True)
        dw_ref[1:2, :] += jnp.sum(dyc * z1, axis=0, keepdims=True)
        dw_ref[2:3, :] += jnp.sum(dyc * z, axis=0, keepdims=True)

    def col(c):
        return pl.BlockSpec((tr, BRANCH), lambda i, c=c: (i, c0 + c))

    def prev(c):
        return pl.BlockSpec((HALO, BRANCH), lambda i, c=c: (jnp.maximum(i * hb - 1, 0), c0 + c))

    def nxt(c):
        return pl.BlockSpec((HALO, BRANCH), lambda i, c=c: (jnp.minimum((i + 1) * hb, last_halo), c))

    row = pl.BlockSpec((tr, BRANCH), lambda i: (i, 0))
    return pl.pallas_call(
        body, name=name, grid=(nblk,),
        out_shape=(jax.ShapeDtypeStruct((t, BRANCH), BF16),) * 3 + (jax.ShapeDtypeStruct((HALO, BRANCH), F32),),
        in_specs=[col(0), col(1), col(2), prev(1), prev(2), row, nxt(0), nxt(c0),
                  pl.BlockSpec((3, BRANCH), lambda i: (0, 0))],
        out_specs=(row, row, row, pl.BlockSpec((HALO, BRANCH), lambda i: (0, 0))),
        compiler_params=_params(("arbitrary",), 8 * _nbytes((tr, BRANCH), F32)),
    )(proj, proj, proj, proj, proj, dout, dout, proj, conv_w)


def _tri(lower):
    r = lax.broadcasted_iota(jnp.int32, (LANE, LANE), 0)
    c = lax.broadcasted_iota(jnp.int32, (LANE, LANE), 1)
    return jnp.where((c <= r) if lower else (c >= r), 1.0, 0.0).astype(F32)


def _logf_cumsum(proj, fbias_row, name):
    t = proj.shape[0]
    nchunk = t // LANE

    def body(f_ref, b_ref, c_ref, run_sc):
        tri = _tri(True)
        run_sc[...] = jnp.zeros_like(run_sc)

        @pl.loop(0, nchunk)
        def _(i):
            rows = pl.ds(pl.multiple_of(i * LANE, LANE), LANE)
            z = f_ref[rows, :] + b_ref[...]
            logf = jnp.minimum(z, 0.0) - jnp.log(1.0 + jnp.exp(-jnp.abs(z)))
            cs = lax.dot_general(tri, logf, (NN, ((), ())), precision=lax.Precision.HIGHEST,
                                 preferred_element_type=F32) + run_sc[0:1, :]
            c_ref[rows, :] = cs
            run_sc[0:1, :] = cs[LANE - 1:LANE, :]

    return pl.pallas_call(
        body, name=name, grid=(1,),
        out_shape=jax.ShapeDtypeStruct((t, LANE), F32),
        in_specs=[pl.BlockSpec((t, LANE), lambda i: (0, COL_FG // LANE)), pl.BlockSpec((1, LANE), lambda i: (0, 0))],
        out_specs=pl.BlockSpec((t, LANE), lambda i: (0, 0)),
        scratch_shapes=[pltpu.VMEM((8, LANE), F32)],
        compiler_params=_params(("arbitrary",), 2 * _nbytes((t, LANE), F32)),
    )(proj, fbias_row)


def _logf_cumsum_bwd(proj, fbias_row, pieces, name):
    t = proj.shape[0]
    tb = _pick(t, 2 * ROW_TILE, LANE)
    nblk = t // tb
    npiece = len(pieces)

    def body(*refs):
        f_ref, b_ref = refs[:2]
        piece_refs = refs[2:2 + npiece]
        df_ref, db_ref, run_sc = refs[2 + npiece:]
        i = pl.program_id(0)
        tri = _tri(False)

        @pl.when(i == 0)
        def _():
            run_sc[...] = jnp.zeros_like(run_sc)
            db_ref[...] = jnp.zeros_like(db_ref)

        for c in reversed(range(tb // LANE)):
            rows = slice(c * LANE, (c + 1) * LANE)
            slabs = [p_ref[n, rows, :] for p_ref in piece_refs for n in range(p_ref.shape[0])]
            dcc = slabs[0]
            for slab in slabs[1:]:
                dcc = dcc + slab
            ss = lax.dot_general(tri, dcc, (NN, ((), ())), precision=lax.Precision.HIGHEST,
                                 preferred_element_type=F32) + run_sc[0:1, :]
            z = f_ref[rows, :] + b_ref[...]
            dz = ss * (1.0 / (1.0 + jnp.exp(z)))
            df_ref[rows, :] = dz.astype(BF16)
            run_sc[0:1, :] = ss[0:1, :]
            db_ref[...] += jnp.sum(dz, axis=0, keepdims=True)

    piece_specs = [pl.BlockSpec((p.shape[0], tb, LANE), lambda i: (0, nblk - 1 - i, 0)) for p in pieces]
    nslab = sum(p.shape[0] for p in pieces)
    return pl.pallas_call(
        body, name=name, grid=(nblk,),
        out_shape=(jax.ShapeDtypeStruct((t, LANE), BF16), jax.ShapeDtypeStruct((1, LANE), F32)),
        in_specs=[pl.BlockSpec((tb, LANE), lambda i: (nblk - 1 - i, COL_FG // LANE)),
                  pl.BlockSpec((1, LANE), lambda i: (0, 0))] + piece_specs,
        out_specs=(pl.BlockSpec((tb, LANE), lambda i: (nblk - 1 - i, 0)), pl.BlockSpec((1, LANE), lambda i: (0, 0))),
        scratch_shapes=[pltpu.VMEM((8, LANE), F32)],
        compiler_params=_params(("arbitrary",), (4 + nslab) * _nbytes((tb, LANE), F32)),
    )(proj, fbias_row, *pieces)


def _lo_mask():
    return lax.broadcasted_iota(jnp.int32, (1, LANE), 1) < HEAD


def _causal_steps(n, key_major):
    if key_major:
        pairs = [(iq, ik) for ik in range(n) for iq in range(ik, n)]
    else:
        pairs = [(iq, ik) for iq in range(n) for ik in range(iq + 1)]
    return (jnp.asarray([p[0] for p in pairs], jnp.int32), jnp.asarray([p[1] for p in pairs], jnp.int32))


def _head_lanes(j, pair_vals):
    lane = lax.broadcasted_iota(jnp.int32, (1, LANE), 1)
    return jnp.where(lane == 2 * j, pair_vals[0], 0.0) + jnp.where(lane == 2 * j + 1, pair_vals[1], 0.0)


def _fox_fwd(proj, c_col, c_row, name):
    t = proj.shape[0]
    tq = _pick(t, FOX_TILE, LANE)
    nq = t // tq
    rep = tq // LANE
    scale = HEAD ** -0.5
    cq, ck, cv = COL_FOX // LANE, COL_FOX // LANE + 4, COL_FOX // LANE + 8
    q_tab, k_tab = _causal_steps(nq, False)

    def body(qt_ref, kt_ref, q_ref, k_ref, v_ref, ck_ref, cqr_ref, y_ref, lse_ref, m_sc, l_sc, acc_sc):
        step_id = pl.program_id(1)
        iq, ik = qt_ref[step_id], kt_ref[step_id]
        lo = _lo_mask()
        lo_rows = lax.broadcasted_iota(jnp.int32, (LANE, 1), 0) < HEAD

        @pl.when(ik == 0)
        def _():
            m_sc[...] = jnp.full(m_sc.shape, NEG, F32)
            l_sc[...] = jnp.zeros_like(l_sc)
            acc_sc[...] = jnp.zeros_like(acc_sc)

        def step(diag):
            q2 = (q_ref[...] * scale).astype(BF16)
            k2 = k_ref[...].astype(BF16)
            v2 = v_ref[...].astype(BF16)
            alphas, adds = [], []
            for h in range(2):
                msk = lo if h == 0 else jnp.logical_not(lo)
                kh = jnp.where(msk, k2, jnp.zeros_like(k2))
                vh = jnp.where(msk, v2, jnp.zeros_like(v2))
                st = _dot(kh, q2, NT) + cqr_ref[h] - jnp.tile(ck_ref[h], (1, rep))
                if diag:
                    krow = lax.broadcasted_iota(jnp.int32, (tq, tq), 0)
                    qcol = lax.broadcasted_iota(jnp.int32, (tq, tq), 1)
                    st = jnp.where(krow <= qcol, st, NEG)
                m_prev = m_sc[h]
                m_new = jnp.maximum(m_prev, jnp.max(st, axis=0, keepdims=True))
                alpha = jnp.exp(m_prev - m_new)
                pt = jnp.exp(st - m_new)
                l_sc[h] = alpha * l_sc[h] + jnp.sum(pt, axis=0, keepdims=True)
                m_sc[h] = m_new
                alphas.append(alpha)
                adds.append(_dot(vh, pt.astype(BF16), TN))
            acc_sc[...] = acc_sc[...] * jnp.where(lo_rows, alphas[0], alphas[1]) + (adds[0] + adds[1])

        @pl.when(ik < iq)
        def _():
            step(False)

        @pl.when(ik == iq)
        def _():
            step(True)
            yt = acc_sc[...] / jnp.where(lo_rows, l_sc[0], l_sc[1])
            y_ref[...] = yt.T.astype(BF16)
            lse_ref[...] = m_sc[...] + jnp.log(l_sc[...])

    def kv(c):
        return pl.BlockSpec((tq, LANE), lambda j, s, qt, kt, c=c: (kt[s], c + j))

    qrow = pl.BlockSpec((2, 1, tq), lambda j, s, qt, kt: (j, 0, qt[s]))
    grid_spec = pltpu.PrefetchScalarGridSpec(
        num_scalar_prefetch=2, grid=(4, int(q_tab.shape[0])),
        in_specs=[pl.BlockSpec((tq, LANE), lambda j, s, qt, kt: (qt[s], cq + j)), kv(ck), kv(cv),
                  pl.BlockSpec((2, tq, LANE), lambda j, s, qt, kt: (j, kt[s], 0)), qrow],
        out_specs=(pl.BlockSpec((tq, LANE), lambda j, s, qt, kt: (qt[s], j)), qrow),
        scratch_shapes=[pltpu.VMEM((2, 1, tq), F32), pltpu.VMEM((2, 1, tq), F32), pltpu.VMEM((LANE, tq), F32)])
    return pl.pallas_call(
        body, name=name, grid_spec=grid_spec,
        out_shape=(jax.ShapeDtypeStruct((t, BRANCH), BF16), jax.ShapeDtypeStruct((8, 1, t), F32)),
        compiler_params=_params(("parallel", "arbitrary"),
                                16 * _nbytes((tq, LANE), F32) + 6 * _nbytes((tq, tq), F32)),
    )(q_tab, k_tab, proj, proj, proj, c_col, c_row)


def _fox_bwd(proj, c_col, c_row, lse_row, y, dy, name):
    t = proj.shape[0]
    tb = _pick(t, FOX_TILE, LANE)
    nb = t // tb
    rep = tb // LANE
    scale = HEAD ** -0.5
    cq, ck, cv = COL_FOX // LANE, COL_FOX // LANE + 4, COL_FOX // LANE + 8
    q_tab, k_tab = _causal_steps(nb, True)
    nsteps = int(q_tab.shape[0])

    def body(qt_ref, kt_ref, k_ref, v_ref, q_ref, y_ref, dy_ref, ck_ref, cqr_ref, lser_ref,
             dq_ref, dk_ref, dv_ref, dck_ref, dcq_ref, dk_sc, dv_sc, dc_sc, dqt_sc, dcq_sc, d_sc):
        j, step_id = pl.program_id(0), pl.program_id(1)
        iq, ik = qt_ref[step_id], kt_ref[step_id]
        lo = _lo_mask()

        @pl.when(step_id == 0)
        def _():
            dqt_sc[...] = jnp.zeros_like(dqt_sc)
            dcq_sc[...] = jnp.zeros_like(dcq_sc)

        @pl.when(iq == ik)
        def _():
            dk_sc[...] = jnp.zeros_like(dk_sc)
            dv_sc[...] = jnp.zeros_like(dv_sc)
            dc_sc[...] = jnp.zeros_like(dc_sc)

        @pl.when(ik == 0)
        def _():
            prod = y_ref[...].astype(F32) * dy_ref[...].astype(F32)
            row = lax.broadcasted_iota(jnp.int32, (8, LANE), 0)
            sel = jnp.logical_or(jnp.logical_and(row == 0, lo), jnp.logical_and(row == 1, jnp.logical_not(lo)))
            d_sc[iq] = lax.dot_general(jnp.where(sel, 1.0, 0.0).astype(F32), prod, (NT, ((), ())),
                                       precision=lax.Precision.HIGHEST, preferred_element_type=F32)

        def step(diag):
            k2 = k_ref[...].astype(BF16)
            v2 = v_ref[...].astype(BF16)
            q2 = (q_ref[...] * scale).astype(BF16)
            do2 = dy_ref[...]
            d_rows = d_sc[iq]
            for h in range(2):
                msk = lo if h == 0 else jnp.logical_not(lo)
                kh = jnp.where(msk, k2, jnp.zeros_like(k2))
                vh = jnp.where(msk, v2, jnp.zeros_like(v2))
                st = _dot(kh, q2, NT) + (cqr_ref[h] - lser_ref[h]) - jnp.tile(ck_ref[h], (1, rep))
                if diag:
                    krow = lax.broadcasted_iota(jnp.int32, (tb, tb), 0)
                    qcol = lax.broadcasted_iota(jnp.int32, (tb, tb), 1)
                    st = jnp.where(krow <= qcol, st, NEG)
                pt = jnp.exp(st)
                dpt = _dot(vh, do2, NT)
                dst = pt * (dpt - d_rows[h:h + 1, :])
                dsb = dst.astype(BF16)
                dv_sc[h] += _dot(pt.astype(BF16), do2, NN)
                dk_sc[h] += _dot(dsb, q2, NN)
                dc_sc[h] -= jnp.sum(dst, axis=1, keepdims=True)
                dqt_sc[iq] += _dot(kh, dsb, TN)
                dcq_sc[h, iq] += jnp.sum(dst, axis=0, keepdims=True)

        @pl.when(iq > ik)
        def _():
            step(False)

        @pl.when(iq == ik)
        def _():
            step(True)

        @pl.when(iq == nb - 1)
        def _():
            dk_ref[...] = jnp.where(lo, dk_sc[0], dk_sc[1]).astype(BF16)
            dv_ref[...] = jnp.where(lo, dv_sc[0], dv_sc[1]).astype(BF16)
            dck_ref[...] = _head_lanes(j, dc_sc)

        @pl.when(step_id == nsteps - 1)
        def _():
            for i in range(nb):
                dq_ref[i * tb:(i + 1) * tb, :] = (dqt_sc[i].T * scale).astype(BF16)
                for h in range(2):
                    dcq_ref[h, :, i * tb:(i + 1) * tb] = dcq_sc[h, i]

    def kcol(c):
        return pl.BlockSpec((tb, LANE), lambda j, s, qt, kt, c=c: (kt[s], c + j))

    qrow = pl.BlockSpec((2, 1, tb), lambda j, s, qt, kt: (j, 0, qt[s]))
    pair_q = pl.BlockSpec((tb, LANE), lambda j, s, qt, kt: (qt[s], j))
    pair_k = pl.BlockSpec((tb, LANE), lambda j, s, qt, kt: (kt[s], j))
    grid_spec = pltpu.PrefetchScalarGridSpec(
        num_scalar_prefetch=2, grid=(4, nsteps),
        in_specs=[kcol(ck), kcol(cv), pl.BlockSpec((tb, LANE), lambda j, s, qt, kt: (qt[s], cq + j)), pair_q, pair_q,
                  pl.BlockSpec((2, tb, LANE), lambda j, s, qt, kt: (j, kt[s], 0)), qrow, qrow],
        out_specs=(pl.BlockSpec((t, LANE), lambda j, s, qt, kt: (0, j)), pair_k, pair_k,
                   pl.BlockSpec((None, tb, LANE), lambda j, s, qt, kt: (j, kt[s], 0)),
                   pl.BlockSpec((2, 1, t), lambda j, s, qt, kt: (j, 0, 0))),
        scratch_shapes=[pltpu.VMEM((2, tb, LANE), F32)] * 3
        + [pltpu.VMEM((nb, LANE, tb), F32), pltpu.VMEM((2, nb, 1, tb), F32), pltpu.VMEM((nb, 8, tb), F32)])
    return pl.pallas_call(
        body, name=name, grid_spec=grid_spec,
        out_shape=(jax.ShapeDtypeStruct((t, BRANCH), BF16), jax.ShapeDtypeStruct((t, BRANCH), BF16),
                   jax.ShapeDtypeStruct((t, BRANCH), BF16), jax.ShapeDtypeStruct((4, t, LANE), F32),
                   jax.ShapeDtypeStruct((8, 1, t), F32)),
        compiler_params=_params(("parallel", "arbitrary"),
                                24 * _nbytes((tb, LANE), F32) + 8 * _nbytes((tb, tb), F32)
                                + 2 * _nbytes((t, LANE), F32)),
    )(q_tab, k_tab, proj, proj, proj, y, dy, c_col, c_row, lse_row)


def _swa_tables(rel_bias):
    tq = np.arange(SWA_BLOCK)[:, None]
    sk = np.arange(2 * SWA_BLOCK)[None, :]
    dist = SWA_BLOCK + tq - sk
    inwin = (dist >= 0) & (dist < SWA_BLOCK)
    n = np.maximum(dist, 0)
    max_exact = N_BUCKETS // 2
    large = max_exact + (np.log(np.maximum(n, 1).astype(np.float32) / max_exact)
                         / math.log(SWA_BLOCK / max_exact) * (N_BUCKETS - max_exact)).astype(np.int32)
    bucket = np.where(n < max_exact, n, np.minimum(large, N_BUCKETS - 1))
    onehot = (bucket[..., None] == np.arange(N_BUCKETS)) & inwin[..., None]
    onehot = jnp.asarray(onehot.astype(np.float32))
    bias = jnp.einsum("tsb,bh->hts", onehot, rel_bias, precision=lax.Precision.HIGHEST)
    bias = jnp.where(jnp.asarray(inwin)[None], bias, NEG)
    return onehot, bias


SWA_BLOCKS_PER_STEP = 8


def _swa_specs(nb, r_blk, csk, csv):
    def prev(c):
        return pl.BlockSpec((SWA_BLOCK, LANE), lambda kvh, n, c=c: (jnp.maximum(n * r_blk - 1, 0), c))

    def cur(c):
        return pl.BlockSpec((r_blk * SWA_BLOCK, LANE), lambda kvh, n, c=c: (n, c))

    return [prev(csk), cur(csk), prev(csv), cur(csv)]


def _swa_fwd(proj, bias, sink_rep, name):
    t = proj.shape[0]
    nb = t // SWA_BLOCK
    r_blk = _pick(nb, SWA_BLOCKS_PER_STEP, 1)
    rows = r_blk * SWA_BLOCK
    scale = HEAD ** -0.5
    csq, csk, csv = COL_SQ // 256, COL_SK // LANE, COL_SV // LANE

    def body(q_ref, kp_ref, kc_ref, vp_ref, vc_ref, b_ref, sk_ref, y_ref, lse_ref):
        kvh, n = pl.program_id(0), pl.program_id(1)
        lane = lax.broadcasted_iota(jnp.int32, (1, LANE), 1)
        lo = lane < HEAD
        kvm = jnp.logical_and(lane >= kvh * HEAD, lane < (kvh + 1) * HEAD)

        def both(prev_ref, cur_ref):
            band = jnp.concatenate([prev_ref[...], cur_ref[...]], axis=0)
            band = jnp.where(kvm, band, 0.0)
            return (band + pltpu.roll(band, HEAD, 1)).astype(BF16)

        kb_all, vb_all = both(kp_ref, kc_ref), both(vp_ref, vc_ref)
        col = lax.broadcasted_iota(jnp.int32, (SWA_BLOCK, 2 * SWA_BLOCK), 1)
        first = jnp.logical_and(n == 0, col < SWA_BLOCK)
        for r in range(r_blk):
            rs = slice(r * SWA_BLOCK, (r + 1) * SWA_BLOCK)
            kb = kb_all[r * SWA_BLOCK:(r + 2) * SWA_BLOCK]
            vb = vb_all[r * SWA_BLOCK:(r + 2) * SWA_BLOCK]
            outs = []
            for g in range(SWA_GROUP):
                half = q_ref[rs, (g // 2) * LANE:(g // 2 + 1) * LANE]
                hm = lo if g % 2 == 0 else jnp.logical_not(lo)
                qg = jnp.where(hm, half, 0.0).astype(BF16)
                s = _dot(qg, kb, NT) * scale + b_ref[g]
                if r == 0:
                    s = jnp.where(first, NEG, s)
                snk = sk_ref[g:g + 1, :]
                m = jnp.maximum(jnp.max(s, axis=1, keepdims=True), snk)
                p = jnp.exp(s - jnp.tile(m, (1, 2)))
                denom = jnp.sum(p, axis=1, keepdims=True) + jnp.exp(snk - m)
                outs.append(_dot(p.astype(BF16), vb, NN) / denom)
                lse_ref[g, rs, :] = m + jnp.log(denom)
            y_ref[rs, 0:LANE] = jnp.where(lo, outs[0], outs[1]).astype(BF16)
            y_ref[rs, LANE:2 * LANE] = jnp.where(lo, outs[2], outs[3]).astype(BF16)

    return pl.pallas_call(
        body, name=name, grid=(2, nb // r_blk),
        out_shape=(jax.ShapeDtypeStruct((t, BRANCH), BF16), jax.ShapeDtypeStruct((8, t, LANE), F32)),
        in_specs=[pl.BlockSpec((rows, 256), lambda kvh, n: (n, csq + kvh))] + _swa_specs(nb, r_blk, csk, csv)
        + [pl.BlockSpec((None, SWA_GROUP, SWA_BLOCK, 256), lambda kvh, n: (kvh, 0, 0, 0)),
           pl.BlockSpec((None, SWA_GROUP, LANE), lambda kvh, n: (kvh, 0, 0))],
        out_specs=(pl.BlockSpec((rows, 256), lambda kvh, n: (n, kvh)),
                   pl.BlockSpec((SWA_GROUP, rows, LANE), lambda kvh, n: (kvh, n, 0))),
        compiler_params=_params(("parallel", "arbitrary"), 8 << 20),
    )(proj, proj, proj, proj, proj, bias.reshape(2, SWA_GROUP, SWA_BLOCK, 256), sink_rep)


def _swa_bwd(proj, bias, sink_rep, lse, y, dy, name):
    t = proj.shape[0]
    nb = t // SWA_BLOCK
    r_blk = _pick(nb, SWA_BLOCKS_PER_STEP, 1)
    rows = r_blk * SWA_BLOCK
    scale = HEAD ** -0.5
    csq, csk, csv = COL_SQ // 256, COL_SK // LANE, COL_SV // LANE

    def body(q_ref, kp_ref, kc_ref, vp_ref, vc_ref, b_ref, sk_ref, lse_ref, y_ref, dy_ref,
             dq_ref, dkp_ref, dvp_ref, db_ref, dsk_ref):
        kvh, n = pl.program_id(0), pl.program_id(1)
        lane = lax.broadcasted_iota(jnp.int32, (1, LANE), 1)
        lo = lane < HEAD
        kvm = jnp.logical_and(lane >= kvh * HEAD, lane < (kvh + 1) * HEAD)

        def both(prev_ref, cur_ref):
            band = jnp.concatenate([prev_ref[...], cur_ref[...]], axis=0)
            band = jnp.where(kvm, band, 0.0)
            return (band + pltpu.roll(band, HEAD, 1)).astype(BF16)

        kb_all, vb_all = both(kp_ref, kc_ref), both(vp_ref, vc_ref)
        col = lax.broadcasted_iota(jnp.int32, (SWA_BLOCK, 2 * SWA_BLOCK), 1)
        first = jnp.logical_and(n == 0, col < SWA_BLOCK)

        @pl.when(n == 0)
        def _():
            db_ref[...] = jnp.zeros_like(db_ref)
            dsk_ref[...] = jnp.zeros_like(dsk_ref)

        for r in range(r_blk):
            rs = slice(r * SWA_BLOCK, (r + 1) * SWA_BLOCK)
            kb = kb_all[r * SWA_BLOCK:(r + 2) * SWA_BLOCK]
            vb = vb_all[r * SWA_BLOCK:(r + 2) * SWA_BLOCK]
            dk_full = jnp.zeros((2 * SWA_BLOCK, LANE), F32)
            dv_full = jnp.zeros((2 * SWA_BLOCK, LANE), F32)
            dqs = []
            for g in range(SWA_GROUP):
                sl = slice((g // 2) * LANE, (g // 2 + 1) * LANE)
                hm = lo if g % 2 == 0 else jnp.logical_not(lo)
                qg = jnp.where(hm, q_ref[rs, sl], 0.0).astype(BF16)
                dog = jnp.where(hm, dy_ref[rs, sl], jnp.zeros((SWA_BLOCK, LANE), BF16))
                dmat = jnp.where(hm, y_ref[rs, sl].astype(F32) * dy_ref[rs, sl].astype(F32), 0.0)
                dg = jnp.sum(dmat, axis=1, keepdims=True)
                s = _dot(qg, kb, NT) * scale + b_ref[g]
                if r == 0:
                    s = jnp.where(first, NEG, s)
                lse_g = lse_ref[g, rs, :]
                p = jnp.exp(s - jnp.tile(lse_g, (1, 2)))
                dp = _dot(dog, vb, NT)
                ds = p * (dp - dg)
                dsb = ds.astype(BF16)
                dqs.append(_dot(dsb, kb, NN) * scale)
                dk_full = dk_full + _dot(dsb, qg, TN)
                dv_full = dv_full + _dot(p.astype(BF16), dog, TN)
                db_ref[g] += ds
                psink = jnp.exp(sk_ref[g:g + 1, :] - lse_g)
                dsk_ref[g:g + 1, :] -= jnp.sum(psink * dg, axis=0, keepdims=True)
            dq_ref[rs, 0:LANE] = jnp.where(lo, dqs[0], dqs[1]).astype(BF16)
            dq_ref[rs, LANE:2 * LANE] = jnp.where(lo, dqs[2], dqs[3]).astype(BF16)
            dkp_ref[r] = jnp.where(kvm, (dk_full + pltpu.roll(dk_full, HEAD, 1)) * scale, 0.0)
            dvp_ref[r] = jnp.where(kvm, dv_full + pltpu.roll(dv_full, HEAD, 1), 0.0)

    qblk = pl.BlockSpec((rows, 256), lambda kvh, n: (n, kvh))
    part = pl.BlockSpec((None, r_blk, 2 * SWA_BLOCK, LANE), lambda kvh, n: (kvh, n, 0, 0))
    bspec = pl.BlockSpec((None, SWA_GROUP, SWA_BLOCK, 256), lambda kvh, n: (kvh, 0, 0, 0))
    sspec = pl.BlockSpec((None, SWA_GROUP, LANE), lambda kvh, n: (kvh, 0, 0))
    return pl.pallas_call(
        body, name=name, grid=(2, nb // r_blk),
        out_shape=(jax.ShapeDtypeStruct((t, BRANCH), BF16),
                   jax.ShapeDtypeStruct((2, nb, 2 * SWA_BLOCK, LANE), F32),
                   jax.ShapeDtypeStruct((2, nb, 2 * SWA_BLOCK, LANE), F32),
                   jax.ShapeDtypeStruct((2, SWA_GROUP, SWA_BLOCK, 256), F32),
                   jax.ShapeDtypeStruct((2, SWA_GROUP, LANE), F32)),
        in_specs=[pl.BlockSpec((rows, 256), lambda kvh, n: (n, csq + kvh))] + _swa_specs(nb, r_blk, csk, csv)
        + [bspec, sspec, pl.BlockSpec((SWA_GROUP, rows, LANE), lambda kvh, n: (kvh, n, 0)), qblk, qblk],
        out_specs=(qblk, part, part, bspec, sspec),
        compiler_params=_params(("parallel", "arbitrary"), 12 << 20),
    )(proj, proj, proj, proj, proj, bias.reshape(2, SWA_GROUP, SWA_BLOCK, 256), sink_rep, lse, y, dy)


def _gate_fwd(proj, pb, name):
    t = proj.shape[0]
    tr = _pick(t, ROW_TILE // 2, BF16_SUBLANE)

    def body(g0, g1, g2, p0, p1, p2, o_ref):
        acc = jax.nn.sigmoid(g0[...]) * p0[...]
        acc = acc + jax.nn.sigmoid(g1[...]) * p1[...]
        acc = acc + jax.nn.sigmoid(g2[...]) * p2[...]
        o_ref[...] = acc.astype(BF16)

    row = pl.BlockSpec((tr, D_MODEL), lambda i: (i, 0))
    gates = [pl.BlockSpec((tr, D_MODEL), lambda i, b=b: (i, b)) for b in range(3)]
    return pl.pallas_call(
        body, name=name, grid=(t // tr,),
        out_shape=jax.ShapeDtypeStruct((t, D_MODEL), BF16),
        in_specs=gates + [row] * 3, out_specs=row,
        compiler_params=_params(("parallel",), 7 * _nbytes((tr, D_MODEL), F32)),
    )(proj, proj, proj, *pb)


def _gate_bwd(proj, pb, dmerged, name):
    t = proj.shape[0]
    tr = _pick(t, ROW_TILE // 2, BF16_SUBLANE)

    def body(g0, g1, g2, p0, p1, p2, dm_ref, dp0, dp1, dp2, dg_ref):
        dm = dm_ref[...]
        for b, (g_ref, p_ref, dp_ref) in enumerate(((g0, p0, dp0), (g1, p1, dp1), (g2, p2, dp2))):
            sg = jax.nn.sigmoid(g_ref[...])
            dp_ref[...] = (dm * sg).astype(BF16)
            dg_ref[:, b * D_MODEL:(b + 1) * D_MODEL] = (dm * p_ref[...] * sg * (1.0 - sg)).astype(BF16)

    row = pl.BlockSpec((tr, D_MODEL), lambda i: (i, 0))
    gates = [pl.BlockSpec((tr, D_MODEL), lambda i, b=b: (i, b)) for b in range(3)]
    return pl.pallas_call(
        body, name=name, grid=(t // tr,),
        out_shape=(jax.ShapeDtypeStruct((t, D_MODEL), BF16),) * 3 + (jax.ShapeDtypeStruct((t, 3 * D_MODEL), BF16),),
        in_specs=gates + [row] * 4,
        out_specs=(row, row, row, pl.BlockSpec((tr, 3 * D_MODEL), lambda i: (i, 0))),
        compiler_params=_params(("parallel",), 11 * _nbytes((tr, D_MODEL), F32)),
    )(proj, proj, proj, *pb, dmerged)


def _swiglu_fwd(ab, name):
    t = ab.shape[0]
    tr = _pick(t, ROW_TILE, BF16_SUBLANE)
    tc = D_FF_P // 2

    def body(a_ref, b_ref, o_ref):
        a = a_ref[...]
        o_ref[...] = (a * jax.nn.sigmoid(a) * b_ref[...]).astype(BF16)

    return pl.pallas_call(
        body, name=name, grid=(t // tr, 2),
        out_shape=jax.ShapeDtypeStruct((t, D_FF_P), BF16),
        in_specs=[pl.BlockSpec((tr, tc), lambda i, j: (i, j)), pl.BlockSpec((tr, tc), lambda i, j: (i, j + 2))],
        out_specs=pl.BlockSpec((tr, tc), lambda i, j: (i, j)),
        compiler_params=_params(("parallel", "parallel"), 3 * _nbytes((tr, tc), F32)),
    )(ab, ab)


def _swiglu_bwd(ab, dh, name):
    t = ab.shape[0]
    tr = _pick(t, ROW_TILE, BF16_SUBLANE)
    tc = D_FF_P // 2

    def body(a_ref, b_ref, dh_ref, da_ref, db_ref):
        a, b, d = a_ref[...], b_ref[...], dh_ref[...]
        sg = jax.nn.sigmoid(a)
        da_ref[...] = (d * b * (sg * (1.0 + a * (1.0 - sg)))).astype(BF16)
        db_ref[...] = (d * (a * sg)).astype(BF16)

    blk = pl.BlockSpec((tr, tc), lambda i, j: (i, j))
    return pl.pallas_call(
        body, name=name, grid=(t // tr, 2),
        out_shape=(jax.ShapeDtypeStruct((t, D_FF_P), BF16),) * 2,
        in_specs=[blk, pl.BlockSpec((tr, tc), lambda i, j: (i, j + 2)), blk],
        out_specs=(blk, blk),
        compiler_params=_params(("parallel", "parallel"), 5 * _nbytes((tr, tc), F32)),
    )(ab, ab, dh)


def _xattn_fwd(q, kv, name):
    t = q.shape[0]
    tq = _pick(t, ROW_TILE, BF16_SUBLANE)
    mlen = kv.shape[0]
    scale = X_HEAD ** -0.5

    def body(q_ref, kv_ref, o_ref):
        for h in range(X_HEADS):
            sl = slice(h * X_HEAD, (h + 1) * X_HEAD)
            kh = kv_ref[:, sl]
            vh = kv_ref[:, D_MODEL + h * X_HEAD:D_MODEL + (h + 1) * X_HEAD]
            s = _dot(q_ref[:, sl], kh, NT) * scale
            p = jnp.exp(s - jnp.max(s, axis=1, keepdims=True))
            l = jnp.sum(p, axis=1, keepdims=True)
            o_ref[:, sl] = (_dot(p.astype(BF16), vh, NN) / l).astype(BF16)

    return pl.pallas_call(
        body, name=name, grid=(t // tq,),
        out_shape=jax.ShapeDtypeStruct((t, D_MODEL), BF16),
        in_specs=[pl.BlockSpec((tq, D_MODEL), lambda i: (i, 0)), pl.BlockSpec((mlen, 2 * D_MODEL), lambda i: (0, 0))],
        out_specs=pl.BlockSpec((tq, D_MODEL), lambda i: (i, 0)),
        compiler_params=_params(("parallel",), 4 * _nbytes((tq, D_MODEL), F32)),
    )(q, kv)


def _xattn_bwd(q, kv, do, name):
    t = q.shape[0]
    tq = _pick(t, ROW_TILE, BF16_SUBLANE)
    mlen = kv.shape[0]
    scale = X_HEAD ** -0.5

    def body(q_ref, kv_ref, do_ref, dq_ref, dkv_ref):
        i = pl.program_id(0)

        @pl.when(i == 0)
        def _():
            dkv_ref[...] = jnp.zeros_like(dkv_ref)

        for h in range(X_HEADS):
            sl = slice(h * X_HEAD, (h + 1) * X_HEAD)
            vsl = slice(D_MODEL + h * X_HEAD, D_MODEL + (h + 1) * X_HEAD)
            qh, kh, vh, doh = q_ref[:, sl], kv_ref[:, sl], kv_ref[:, vsl], do_ref[:, sl]
            s = _dot(qh, kh, NT) * scale
            p = jnp.exp(s - jnp.max(s, axis=1, keepdims=True))
            p = p / jnp.sum(p, axis=1, keepdims=True)
            dp = _dot(doh, vh, NT)
            ds = p * (dp - jnp.sum(p * dp, axis=1, keepdims=True))
            dsb = ds.astype(BF16)
            dq_ref[:, sl] = (_dot(dsb, kh, NN) * scale).astype(BF16)
            dkv_ref[:, sl] += _dot(dsb, qh, TN) * scale
            dkv_ref[:, vsl] += _dot(p.astype(BF16), doh, TN)

    row = pl.BlockSpec((tq, D_MODEL), lambda i: (i, 0))
    whole = pl.BlockSpec((mlen, 2 * D_MODEL), lambda i: (0, 0))
    return pl.pallas_call(
        body, name=name, grid=(t // tq,),
        out_shape=(jax.ShapeDtypeStruct((t, D_MODEL), BF16), jax.ShapeDtypeStruct((mlen, 2 * D_MODEL), F32)),
        in_specs=[row, whole, row], out_specs=(row, whole),
        compiler_params=_params(("arbitrary",), 6 * _nbytes((tq, D_MODEL), F32)),
    )(q, kv, do)


def _position():
    return lax.axis_index("x"), lax.axis_index("y"), lax.axis_index("c")


N_PEER = N_DEV - 1


def _all_gather(xs, name):
    n = len(xs)

    def body(*refs):
        x_refs, out_refs = refs[:n], refs[n:2 * n]
        send_sems, recv_sems, local_sems = refs[2 * n:]
        mx, my, mc = _position()
        me, sib = (mx, my, mc), (mx, my, 1 - mc)
        chips = [(1 - mx, my), (mx, 1 - my), (1 - mx, 1 - my)]

        def slot(i, p):
            return out_refs[i].at[4 * p[0] + 2 * p[1] + p[2]]

        def copy(i, k, block, to, src=None):
            return pltpu.make_async_remote_copy(
                src_ref=slot(i, block) if src is None else src, dst_ref=slot(i, block),
                send_sem=send_sems.at[i * N_PEER + k], recv_sem=recv_sems.at[i * N_PEER + k],
                device_id=to, device_id_type=MESH)

        mine = [pltpu.make_async_copy(x_refs[i], slot(i, me), local_sems.at[i]) for i in range(n)]
        for cp in mine:
            cp.start()
        first = [copy(i, 1 + j, me, (*chip, mc), src=x_refs[i]) for j, chip in enumerate(chips) for i in range(n)]
        first += [copy(i, 0, me, sib, src=x_refs[i]) for i in range(n)]
        for cp in first:
            cp.start()
        passed = []
        for j, chip in enumerate(chips):
            for i in range(n):
                copy(i, 1 + j, (*chip, mc), me).wait_recv()
                passed.append(copy(i, 4 + j, (*chip, mc), sib))
                passed[-1].start()
        for i in range(n):
            copy(i, 0, sib, me).wait_recv()
        for j, chip in enumerate(chips):
            for i in range(n):
                copy(i, 4 + j, (*chip, 1 - mc), me).wait_recv()
        for cp in first + passed:
            cp.wait_send()
        for cp in mine:
            cp.wait()

    return pl.pallas_call(
        body, name=name,
        out_shape=tuple(jax.ShapeDtypeStruct((N_DEV,) + x.shape, x.dtype) for x in xs),
        in_specs=[pl.BlockSpec(memory_space=pl.ANY)] * n, out_specs=(pl.BlockSpec(memory_space=pl.ANY),) * n,
        scratch_shapes=[pltpu.SemaphoreType.DMA((n * N_PEER,)), pltpu.SemaphoreType.DMA((n * N_PEER,)),
                        pltpu.SemaphoreType.DMA((n,))],
    )(*xs)


PEER_RELS = [(dx, dy, dc) for dx in (0, 1) for dy in (0, 1) for dc in (0, 1)][1:]


def _peer_copy(rel_k, i, src_refs, land_refs, send_sems, recv_sems, layer, gather, arriving):
    mx, my, mc = _position()
    me_idx = 4 * mx + 2 * my + mc
    p = tuple((1 - v) if f else v for f, v in zip(PEER_RELS[rel_k], (mx, my, mc)))
    p_idx = 4 * p[0] + 2 * p[1] + p[2]
    src_slot, dst_slot = (me_idx, p_idx) if arriving else (p_idx, me_idx)
    src = src_refs[i] if gather else src_refs[i].at[src_slot]
    dst = land_refs[i].at[dst_slot] if layer is None else land_refs[i].at[dst_slot, layer]
    return pltpu.make_async_remote_copy(
        src_ref=src, dst_ref=dst, send_sem=send_sems.at[i * N_PEER + rel_k], recv_sem=recv_sems.at[i * N_PEER + rel_k],
        device_id=p, device_id_type=MESH)


HBM_SPEC = pl.BlockSpec(memory_space=pltpu.HBM)
SEM_SPEC = pl.BlockSpec(memory_space=pltpu.SEMAPHORE)
SIDE_EFFECT = pltpu.SideEffectType.DATAFLOW_SIDE_EFFECTING


def _own_slots(srcs, lands, layer, gather):
    mx, my, mc = _position()
    me_idx = 4 * mx + 2 * my + mc
    out = []
    for s, land in zip(srcs, lands):
        piece = s[None] if gather else lax.dynamic_index_in_dim(s, me_idx, 0, keepdims=True)
        if layer is None:
            start = (me_idx,) + (0,) * (land.ndim - 1)
        else:
            piece, start = piece[:, None], (me_idx, layer) + (0,) * (land.ndim - 2)
        out.append(lax.dynamic_update_slice(land, piece, start))
    return out


def _swap_start(srcs, lands, layer, gather, name):
    n = len(srcs)

    def body(*refs):
        src_refs, land_refs = refs[:n], refs[n:2 * n]
        send_sems, recv_sems = refs[2 * n], refs[2 * n + 1]
        token = refs[4 * n + 2]
        for i in range(n):
            for k in range(N_PEER):
                _peer_copy(k, i, src_refs, land_refs, send_sems, recv_sems, layer, gather, False).start()
        token[...] = jnp.zeros_like(token)

    hbm = [pltpu.with_memory_space_constraint(a, pltpu.HBM) for a in list(srcs) + list(lands)]
    out = pl.pallas_call(
        body, name=name,
        out_shape=(pltpu.SemaphoreType.DMA((n * N_PEER,)), pltpu.SemaphoreType.DMA((n * N_PEER,)))
        + tuple(pltpu.HBM(a.shape, a.dtype) for a in hbm) + (jax.ShapeDtypeStruct((8, LANE), F32),),
        in_specs=[HBM_SPEC] * (2 * n),
        out_specs=(SEM_SPEC, SEM_SPEC) + (HBM_SPEC,) * (2 * n) + (pl.BlockSpec(memory_space=pltpu.VMEM),),
        input_output_aliases={i: 2 + i for i in range(2 * n)},
        compiler_params=pltpu.CompilerParams(has_side_effects=SIDE_EFFECT),
    )(*hbm)
    return out[0], out[1], list(out[2:2 + n]), list(out[2 + n:2 + 2 * n]), out[2 + 2 * n]


def _swap_wait(send_sems, recv_sems, srcs, lands, after, layer, gather, name):
    n = len(srcs)

    def body(*refs):
        src_refs, land_refs = refs[:n], refs[n:2 * n]
        send_sems_ref, recv_sems_ref = refs[2 * n], refs[2 * n + 1]
        for i in range(n):
            for k in range(N_PEER):
                args = (src_refs, land_refs, send_sems_ref, recv_sems_ref, layer, gather)
                _peer_copy(k, i, *args, False).wait_send()
                _peer_copy(k, i, *args, True).wait_recv()

    out = pl.pallas_call(
        body, name=name,
        out_shape=tuple(pltpu.HBM(a.shape, a.dtype) for a in list(srcs) + list(lands)),
        in_specs=[HBM_SPEC] * (2 * n) + [SEM_SPEC, SEM_SPEC, pl.BlockSpec(memory_space=pl.ANY)],
        out_specs=(HBM_SPEC,) * (2 * n),
        input_output_aliases={i: i for i in range(2 * n)},
        compiler_params=pltpu.CompilerParams(has_side_effects=SIDE_EFFECT),
    )(*srcs, *lands, send_sems, recv_sems, after)
    return list(out[n:])


ADAMW_BLOCK_BYTES = 1 << 20


def _adamw(parts, w, m, v, name):
    r, l = w.shape
    tr = _pick(r, max(ADAMW_BLOCK_BYTES // (4 * l), BF16_SUBLANE), BF16_SUBLANE)
    c1 = 1.0 - ADAM_B1 ** ADAM_STEP
    c2 = 1.0 - ADAM_B2 ** ADAM_STEP

    def body(p_ref, w_ref, m_ref, v_ref, g_ref, d_ref, nm_ref, nv_ref):
        g = p_ref[0].astype(F32)
        for s in range(1, N_DEV):
            g = g + p_ref[s].astype(F32)
        nm = ADAM_B1 * m_ref[...] + (1.0 - ADAM_B1) * g
        nv = ADAM_B2 * v_ref[...] + (1.0 - ADAM_B2) * (g * g)
        m_hat = nm / c1
        v_hat = nv / c2
        g_ref[...] = g
        d_ref[...] = -ADAM_LR * (m_hat / (jnp.sqrt(v_hat) + ADAM_EPS) + ADAM_WD * w_ref[...])
        nm_ref[...] = nm
        nv_ref[...] = nv

    row = pl.BlockSpec((tr, l), lambda i: (i, 0))
    return pl.pallas_call(
        body, name=name, grid=(r // tr,),
        out_shape=(jax.ShapeDtypeStruct((r, l), F32),) * 4,
        in_specs=[pl.BlockSpec((N_DEV, tr, l), lambda i: (0, i, 0)), row, row, row],
        out_specs=(row,) * 4,
        compiler_params=_params(("parallel",), 12 * _nbytes((tr, l), F32)),
    )(parts, w, m, v)


MATRIX_WEIGHTS = (("w_in", 2), ("conv_w", 2), ("w_branch", 3), ("w_mix_out", 1), ("w_xq", 1), ("w_xkv", 2),
                  ("w_xo", 1), ("w_ffn_gate", 2), ("w_ffn_up", 2), ("w_ffn_down", 1))
SMALL_PARAMS = ("mix_norm_g", "xattn_norm_g", "mem_norm_g", "ffn_norm_g", "final_norm_g", "forget_bias", "sink",
                "rel_bias")


def _pack_small(pieces):
    flat = jnp.concatenate([p.astype(F32).reshape(-1) for p in pieces])
    total = -(-flat.shape[0] // (8 * LANE)) * (8 * LANE)
    return jnp.pad(flat, (0, total - flat.shape[0])).reshape(total // LANE, LANE)


def _rows(a):
    return a.reshape(-1, a.shape[-1])


def _to_full(gathered, axis):
    moved = jnp.moveaxis(gathered, 0, axis)
    shape = list(moved.shape)
    shape[axis:axis + 2] = [shape[axis] * shape[axis + 1]]
    return moved.reshape(shape)


def _to_blocks(full, axis):
    shape = list(full.shape)
    shape[axis:axis + 1] = [N_DEV, shape[axis] // N_DEV]
    return jnp.moveaxis(full.reshape(shape), axis, 0)


def _perm_in(w_in):
    pad = jnp.zeros((w_in.shape[0], PROJ_COLS - IN_COLS), w_in.dtype)
    return jnp.concatenate([w_in[:, 3848:6920], w_in[:, 0:3072], w_in[:, 3080:3848], w_in[:, 3072:3080], pad], axis=1)


def _unperm_in(dw):
    return jnp.concatenate([dw[:, 3072:6144], dw[:, 6912:6920], dw[:, 6144:6912], dw[:, 0:3072]], axis=1)


def _layer_fwd(l, x, mem, wt, sm, pre_branch=None):
    t = x.shape[0]
    tag = f"l{l}_"
    h = _rms_fwd(x, sm["mix_norm_g"][l], tag + "mix_norm")
    proj = _matmul(h, wt["w_in"][l], "nn", F32, tag + "in_proj")
    y_conv = _conv_fwd(proj, wt["conv_w"][l], tag + "conv")
    fbias_row = jnp.pad(sm["forget_bias"][l], (0, LANE - 8)).reshape(1, LANE)
    c = _logf_cumsum(proj, fbias_row, tag + "logf_cumsum")
    c8 = c[:, :8].T
    c_col = jnp.broadcast_to(c8[:, :, None], (8, t, LANE))
    c_row = c8.reshape(8, 1, t)
    y_fox, lse_fox = _fox_fwd(proj, c_col, c_row, tag + "fox")
    onehot, bias = sm["swa_tables"]
    sink_rep = jnp.broadcast_to(sm["sink"][l].reshape(2, SWA_GROUP, 1), (2, SWA_GROUP, LANE))
    y_swa, lse_swa = _swa_fwd(proj, bias, sink_rep, tag + "swa")
    ys = (y_conv, y_fox, y_swa)
    if pre_branch is not None:
        pre_branch(y_swa)
    pb = tuple(_matmul(ys[b], wt["w_branch"][l][b], "nn", F32, tag + f"branch{b}") for b in range(3))
    merged = _gate_fwd(proj, pb, tag + "gate")
    x1 = _matmul(merged, wt["w_mix_out"][l], "nn", F32, tag + "mix_out", residual=x)
    xn2 = _rms_fwd(x1, sm["xattn_norm_g"][l], tag + "xattn_norm")
    q = _matmul(xn2, wt["w_xq"][l], "nn", BF16, tag + "xq")
    mem_n = _rms_fwd(mem, sm["mem_norm_g"][l], tag + "mem_norm")
    kv = _matmul(mem_n, wt["w_xkv"][l], "nn", BF16, tag + "xkv")
    o = _xattn_fwd(q, kv, tag + "xattn")
    x2 = _matmul(o, wt["w_xo"][l], "nn", F32, tag + "xo", residual=x1)
    xn3 = _rms_fwd(x2, sm["ffn_norm_g"][l], tag + "ffn_norm")
    ab = _matmul(xn3, wt["w_gu"][l], "nn", F32, tag + "ffn_gu")
    h1 = _swiglu_fwd(ab, tag + "swiglu")
    x3 = _matmul(h1, wt["w_ffn_down"][l], "nn", F32, tag + "ffn_down", residual=x2)
    saved = dict(x=x, h=h, proj=proj, fbias_row=fbias_row, c_col=c_col, c_row=c_row, ys=ys, lse_fox=lse_fox,
                 onehot=onehot, bias=bias, sink_rep=sink_rep, lse_swa=lse_swa, pb=pb, merged=merged, x1=x1,
                 xn2=xn2, q=q, mem_n=mem_n, kv=kv, o=o, x2=x2, xn3=xn3, ab=ab, h1=h1)
    return x3, saved


def _layer_bwd(l, dx3, dx3_b, mem, wt, sm, sv, mid_hook=None, late_hook=None):
    t = dx3.shape[0]
    nb = t // SWA_BLOCK
    tag = f"l{l}_b_"
    gw, gs = {}, {}
    dh1 = _matmul(dx3_b, wt["w_ffn_down"][l], "nt", F32, tag + "d_h1")
    gw["w_ffn_down"] = _matmul(sv["h1"], dx3_b, "tn", F32, tag + "dw_down")
    da, db = _swiglu_bwd(sv["ab"], dh1, tag + "swiglu")
    dxn3 = _matmul(da, wt["w_ffn_gate"][l], "nt", F32, tag + "d_xn3_gate")
    dxn3 = _matmul(db, wt["w_ffn_up"][l], "nt", F32, tag + "d_xn3_up", residual=dxn3)
    gw["w_ffn_gate"] = _matmul(sv["xn3"], da, "tn", F32, tag + "dw_gate")
    gw["w_ffn_up"] = _matmul(sv["xn3"], db, "tn", F32, tag + "dw_up")
    dx2, dx2_b, gs["ffn_norm_g"] = _rms_bwd(sv["x2"], sm["ffn_norm_g"][l], dxn3, dx3, tag + "ffn_norm")
    do = _matmul(dx2_b, wt["w_xo"][l], "nt", BF16, tag + "d_o")
    gw["w_xo"] = _matmul(sv["o"], dx2_b, "tn", F32, tag + "dw_xo")
    dq, dkv = _xattn_bwd(sv["q"], sv["kv"], do, tag + "xattn")
    gw["w_xkv"] = _matmul(sv["mem_n"], dkv, "tn", F32, tag + "dw_xkv")
    dmem_n = _matmul(dkv, wt["w_xkv"][l], "nt", F32, tag + "d_memn")
    _, _, gs["mem_norm_g"] = _rms_bwd(mem, sm["mem_norm_g"][l], dmem_n, None, tag + "mem_norm")
    gw["w_xq"] = _matmul(sv["xn2"], dq, "tn", F32, tag + "dw_xq")
    dxn2 = _matmul(dq, wt["w_xq"][l], "nt", F32, tag + "d_xn2")
    dx1, dx1_b, gs["xattn_norm_g"] = _rms_bwd(sv["x1"], sm["xattn_norm_g"][l], dxn2, dx2, tag + "xattn_norm")
    dmerged = _matmul(dx1_b, wt["w_mix_out"][l], "nt", F32, tag + "d_merged")
    gw["w_mix_out"] = _matmul(sv["merged"], dx1_b, "tn", F32, tag + "dw_mix_out")
    dp0, dp1, dp2, dgate = _gate_bwd(sv["proj"], sv["pb"], dmerged, tag + "gate")
    dps = (dp0, dp1, dp2)
    dy_dtypes = (F32, BF16, BF16)
    dys = [_matmul(dps[b], wt["w_branch"][l][b], "nt", dy_dtypes[b], tag + f"d_y{b}") for b in range(3)]
    gw["w_branch"] = jnp.stack(
        [_matmul(sv["ys"][b], dps[b], "tn", F32, tag + f"dw_branch{b}") for b in range(3)])
    sink_rep = sv["sink_rep"] if mid_hook is None else sv["sink_rep"] + mid_hook(gw)
    dsq, dkp, dvp, dbias, dsink = _swa_bwd(sv["proj"], sv["bias"], sink_rep, sv["lse_swa"], sv["ys"][2],
                                           dys[2], tag + "swa")

    def band_add(part):
        tot = part[0] + part[1]
        cur = tot[:, SWA_BLOCK:, :]
        nxt = jnp.concatenate([tot[1:, :SWA_BLOCK, :], jnp.zeros((1, SWA_BLOCK, LANE), F32)], axis=0)
        return (cur + nxt).reshape(t, LANE).astype(BF16)

    dsk, dsv = band_add(dkp), band_add(dvp)
    gs["rel_bias_l"] = jnp.einsum("hts,tsb->bh", dbias.reshape(8, SWA_BLOCK, 2 * SWA_BLOCK), sv["onehot"],
                                  precision=lax.Precision.HIGHEST)
    gs["sink"] = dsink[:, :, 0].reshape(8)
    dfq, dfk, dfv, dck, dcq_row = _fox_bwd(sv["proj"], sv["c_col"], sv["c_row"], sv["lse_fox"], sv["ys"][1], dys[1],
                                           tag + "fox_bwd")
    dcq = jnp.pad(dcq_row.reshape(8, t).T, ((0, 0), (0, LANE - 8))).reshape(1, t, LANE)
    dfg, dfb = _logf_cumsum_bwd(sv["proj"], sv["fbias_row"], [dck, dcq], tag + "logf_cumsum")
    gs["forget_bias"] = dfb[0, :8]
    dcb, dcc, dcu, dconv = _conv_bwd(sv["proj"], wt["conv_w"][l], dys[0], tag + "conv")
    gw["conv_w"] = dconv[:3]
    dproj = jnp.concatenate([dgate, dcb, dcc, dcu, dfq, dfk, dfv, dsq, dsk, dsv, dfg], axis=1)
    gw["w_in"] = _unperm_in(_matmul(sv["h"], dproj, "tn", F32, tag + "dw_in"))
    g_mix = sm["mix_norm_g"][l] if late_hook is None else sm["mix_norm_g"][l] + late_hook(gw)
    dh = _matmul(dproj, wt["w_in"][l], "nt", F32, tag + "d_h")
    dx, dx_b, gs["mix_norm_g"] = _rms_bwd(sv["x"], g_mix, dh, dx1, tag + "mix_norm")
    return dx, dx_b, gw, gs


def kernel(x, mem, mix_norm_g, w_in, forget_bias, conv_w, sink, w_branch, w_mix_out, rel_bias, xattn_norm_g, mem_norm_g, w_xq, w_xkv, w_xo, ffn_norm_g, w_ffn_gate, w_ffn_up, w_ffn_down, final_norm_g, loss_target, m_mix_norm_g, m_w_in, m_forget_bias, m_conv_w, m_sink, m_w_branch, m_w_mix_out, m_rel_bias, m_xattn_norm_g, m_mem_norm_g, m_w_xq, m_w_xkv, m_w_xo, m_ffn_norm_g, m_w_ffn_gate, m_w_ffn_up, m_w_ffn_down, m_final_norm_g, v_mix_norm_g, v_w_in, v_forget_bias, v_conv_w, v_sink, v_w_branch, v_w_mix_out, v_rel_bias, v_xattn_norm_g, v_mem_norm_g, v_w_xq, v_w_xkv, v_w_xo, v_ffn_norm_g, v_w_ffn_gate, v_w_ffn_up, v_w_ffn_down, v_final_norm_g):
    args = dict(locals())
    names = [n for n, _ in MATRIX_WEIGHTS] + list(SMALL_PARAMS)
    w = {n: args[n] for n in names}
    mo = {n: args["m_" + n] for n in names}
    vo = {n: args["v_" + n] for n in names}
    x2d, mem2d, tgt = x[0], mem[0], loss_target[0]

    wire = {n: (F32 if n == "conv_w" else BF16) for n, _ in MATRIX_WEIGHTS}
    late = ("w_in", "conv_w")
    early_w = [(n, ax) for n, ax in MATRIX_WEIGHTS if n not in late]
    late_w = [(n, ax) for n, ax in MATRIX_WEIGHTS if n in late]
    wt = {n: [None] * DEPTH for n, _ in MATRIX_WEIGHTS}
    wt["w_gu"] = [None] * DEPTH

    ff_pad = FF_SHARD_P - FF_SHARD
    ff_axis = {"w_ffn_gate": 2, "w_ffn_up": 2, "w_ffn_down": 1}

    def pad_ffn(n, blocks):
        if n not in ff_axis:
            return blocks
        return jnp.pad(blocks, [(0, ff_pad if d == ff_axis[n] else 0) for d in range(blocks.ndim)])

    def unpad_ffn(n, blocks):
        return lax.slice_in_dim(blocks, 0, FF_SHARD, axis=ff_axis[n]) if n in ff_axis else blocks

    def place_weights(l, which, gathered):
        for (n, ax), g in zip(which, gathered):
            wt[n][l] = _to_full(pad_ffn(n, g), ax - 1)
        if "w_in" in dict(which):
            wt["w_in"][l] = _perm_in(wt["w_in"][l])
        if "w_ffn_gate" in dict(which):
            wt["w_gu"][l] = jnp.concatenate([wt["w_ffn_gate"][l], wt["w_ffn_up"][l]], axis=1)

    def shards_of(l, which):
        return [w[n][l].astype(wire[n]) for n, _ in which]

    def start_gather(srcs, name):
        lands = _own_slots(srcs, [lax.empty((N_DEV,) + s.shape, s.dtype) for s in srcs], None, True)
        return _swap_start(srcs, lands, None, True, name)

    place_weights(0, late_w, _all_gather(shards_of(0, late_w), "weights_gather_l0_first"))
    r_send, r_recv, r_srcs, r_lands, token = start_gather(shards_of(0, early_w), "weights_gather_l0_rest_start")
    shards1 = shards_of(1, MATRIX_WEIGHTS)
    shards1[0] = shards1[0] + token[0, 0].astype(shards1[0].dtype)
    w_send, w_recv, w_srcs, lands, token = start_gather(shards1, "weights_gather_l1_start")
    sm = {n: w[n] for n in SMALL_PARAMS}
    sm["mix_norm_g"] = w["mix_norm_g"].at[0].add(token[0, 0])
    sm["swa_tables"] = _swa_tables(w["rel_bias"])

    def rest_of_layer0(after):
        place_weights(0, early_w, _swap_wait(r_send, r_recv, r_srcs, r_lands, after, None, True,
                                             "weights_gather_l0_rest_wait"))

    saved = []
    xc = x2d
    for l in range(DEPTH):
        if l == 1:
            place_weights(1, MATRIX_WEIGHTS,
                          _swap_wait(w_send, w_recv, w_srcs, lands, xc, None, True, "weights_gather_l1_wait"))
        xc, sv = _layer_fwd(l, xc, mem2d, wt, sm, rest_of_layer0 if l == 0 else None)
        saved.append(sv)
    loss_row, dx, dx_b, dg_final = _loss_head(xc, sm["final_norm_g"], tgt, "loss_head")

    def grad_parts(gw, which):
        return [unpad_ffn(n, _to_blocks(gw[n], ax - 1)).astype(wire[n]) for n, ax in which]

    gw_all, gs_all = [None] * DEPTH, [None] * DEPTH
    dx, dx_b, gw_all[1], gs_all[1] = _layer_bwd(1, dx, dx_b, mem2d, wt, sm, saved[1])
    parts1 = grad_parts(gw_all[1], MATRIX_WEIGHTS)
    zones = _own_slots(parts1, [lax.empty((N_DEV, DEPTH) + p.shape[1:], p.dtype) for p in parts1], 1, False)
    g_send, g_recv, g_srcs, zones, token = _swap_start(parts1, zones, 1, False, "grads_exchange_l1_start")
    sm_b = dict(sm)
    sm_b["ffn_norm_g"] = sm["ffn_norm_g"].at[0].add(token[0, 0])
    mid = {}

    def mid_hook(gw):
        zone = dict(zip([n for n, _ in MATRIX_WEIGHTS],
                        _swap_wait(g_send, g_recv, g_srcs, zones, gw["w_mix_out"], 1, False, "grads_exchange_l1_wait")))
        parts0 = grad_parts(gw, early_w)
        early_zones = _own_slots(parts0, [zone[n] for n, _ in early_w], 0, False)
        mid["early"] = _swap_start(parts0, early_zones, 0, False, "grads_exchange_l0_early_start")
        mid["late_zones"] = [zone[n] for n, _ in late_w]
        return mid["early"][4][0, 0]

    def late_hook(gw):
        parts0 = grad_parts(gw, late_w)
        late_zones = _own_slots(parts0, mid["late_zones"], 0, False)
        mid["late"] = _swap_start(parts0, late_zones, 0, False, "grads_exchange_l0_late_start")
        return mid["late"][4][0, 0]

    dx, dx_b, gw_all[0], gs_all[0] = _layer_bwd(0, dx, dx_b, mem2d, wt, sm_b, saved[0], mid_hook, late_hook)
    grad_x = dx[None]
    recv_by_name = {}
    for key, which in (("early", early_w), ("late", late_w)):
        s_send, s_recv, s_srcs, s_zones, _ = mid[key]
        recv_by_name.update(zip([n for n, _ in which],
                                _swap_wait(s_send, s_recv, s_srcs, s_zones, dx, 0, False,
                                           "grads_exchange_l0_" + key + "_wait")))
    recv = [recv_by_name[n] for n, _ in MATRIX_WEIGHTS]

    outs = {}
    for (n, _), r in zip(MATRIX_WEIGHTS, recv):
        res = _adamw(r.reshape((N_DEV,) + _rows(w[n]).shape), _rows(w[n]), _rows(mo[n]), _rows(vo[n]), "adamw_" + n)
        outs[n] = [o.reshape(w[n].shape) for o in res]

    gsm = {n: jnp.stack([gs_all[l][n] for l in range(DEPTH)])
           for n in ("mix_norm_g", "xattn_norm_g", "mem_norm_g", "ffn_norm_g", "forget_bias", "sink")}
    gsm["final_norm_g"] = dg_final
    gsm["rel_bias"] = gs_all[0]["rel_bias_l"] + gs_all[1]["rel_bias_l"]
    zero = jnp.zeros((1,), F32)
    (small_parts,) = _all_gather([_pack_small([gsm[n] for n in SMALL_PARAMS] + [loss_row[0, :1]])],
                                 "small_grads_all_gather")
    outs_small = _adamw(small_parts, *[_pack_small([d[n] for n in SMALL_PARAMS] + [zero]) for d in (w, mo, vo)],
                        "adamw_small")
    for kind in range(4):
        flat, o = outs_small[kind].reshape(-1), 0
        for n in SMALL_PARAMS:
            sz = int(np.prod(w[n].shape))
            outs.setdefault(n, []).append(flat[o:o + sz].reshape(w[n].shape))
            o += sz
        if kind == 0:
            loss = flat[o]

    order = ["mix_norm_g", "w_in", "forget_bias", "conv_w", "sink", "w_branch", "w_mix_out", "rel_bias",
             "xattn_norm_g", "mem_norm_g", "w_xq", "w_xkv", "w_xo", "ffn_norm_g", "w_ffn_gate", "w_ffn_up",
             "w_ffn_down", "final_norm_g"]
    result = [loss, grad_x]
    for kind in range(4):
        result += [outs[n][kind] for n in order]
    return tuple(result)
```
